```python
import math
import jax, jax.numpy as jnp
from jax import lax
import numpy as np

D_MODEL = 1024
BATCH = 8
SEQ = 4096
DEPTH = 2

BLOCK_Q = 128
RMS_EPS = 1e-6
SB_HEADS = 8
SB_HEAD_DIM = 64
SB_WIDTH = SB_HEADS * SB_HEAD_DIM
MLA_HEADS = 8
MLA_NOPE_DIM = 64
MLA_ROPE_DIM = 32
MLA_V_DIM = 64
MLA_Q_LORA = 384
MLA_KV_LORA = 256
MLA_WIDTH = MLA_HEADS * MLA_V_DIM
ROPE_THETA = 10000.0
FOX_HEADS = 16
FOX_HEAD_DIM = 64
FOX_WIDTH = FOX_HEADS * FOX_HEAD_DIM
EVEN_IN_WIDTH = 4 * SB_WIDTH + MLA_Q_LORA + MLA_KV_LORA + MLA_ROPE_DIM + MLA_WIDTH
ODD_IN_WIDTH = 4 * FOX_WIDTH + FOX_HEADS

kernel_name = "hybrid_stickbreak_mla_fox_sandwich"


def rms_norm(x, g):
    xf = x.astype(jnp.float32)
    var = jnp.mean(xf * xf, axis=-1, keepdims=True)
    return (xf * lax.rsqrt(var + RMS_EPS)).astype(x.dtype) * g


def split_heads(t, n_heads):
    b, s, _ = t.shape
    return t.reshape(b, s, n_heads, -1).transpose(0, 2, 1, 3)


def merge_heads(o):
    b, h, s, d = o.shape
    return o.transpose(0, 2, 1, 3).reshape(b, s, h * d)


def sweep_blocks(block_fn, n_blocks):
    out = lax.map(block_fn, jnp.arange(n_blocks))
    nb, b, h, bq, d = out.shape
    return out.transpose(1, 2, 0, 3, 4).reshape(b, h, nb * bq, d)


def rope_angles(positions, dim):
    inv_freq = ROPE_THETA ** (-jnp.arange(0, dim, 2, dtype=jnp.float32) / dim)
    ang = positions.astype(jnp.float32)[..., None] * inv_freq
    return jnp.cos(ang), jnp.sin(ang)


def apply_rope(x, cos, sin):
    x1, x2 = jnp.split(x, 2, axis=-1)
    cos = cos.astype(x.dtype)
    sin = sin.astype(x.dtype)
    return jnp.concatenate([x1 * cos - x2 * sin, x2 * cos + x1 * sin], axis=-1)


def stick_breaking_attention(q, k, v):
    s_len, d = q.shape[2], q.shape[3]
    scale = d ** -0.5
    k_pos = jnp.arange(s_len)

    def one_block(i):
        start = i * BLOCK_Q
        qb = lax.dynamic_slice_in_dim(q, start, BLOCK_Q, axis=2)
        z = jnp.einsum('bhqd,bhkd->bhqk', qb, k).astype(jnp.float32) * scale
        q_pos = start + jnp.arange(BLOCK_Q)
        before = k_pos[None, :] < q_pos[:, None]
        log_keep = jnp.where(before, jax.nn.log_sigmoid(-z), 0.0)
        log_remain = lax.cumsum(log_keep, axis=3, reverse=True) - log_keep
        w = jnp.where(before, jnp.exp(jax.nn.log_sigmoid(z) + log_remain), 0.0)
        return jnp.einsum('bhqk,bhkd->bhqd', w.astype(v.dtype), v)

    return sweep_blocks(one_block, s_len // BLOCK_Q)


def mla_attention(q_nope, q_rope, k_nope, k_rope, v):
    s_len = q_nope.shape[2]
    scale = (MLA_NOPE_DIM + MLA_ROPE_DIM) ** -0.5
    k_pos = jnp.arange(s_len)

    def one_block(i):
        start = i * BLOCK_Q
        qn = lax.dynamic_slice_in_dim(q_nope, start, BLOCK_Q, axis=2)
        qr = lax.dynamic_slice_in_dim(q_rope, start, BLOCK_Q, axis=2)
        z = (jnp.einsum('bhqd,bhkd->bhqk', qn, k_nope)
             + jnp.einsum('bhqr,bkr->bhqk', qr, k_rope)).astype(jnp.float32) * scale
        q_pos = start + jnp.arange(BLOCK_Q)
        causal = k_pos[None, :] <= q_pos[:, None]
        p = jax.nn.softmax(jnp.where(causal, z, -jnp.inf), axis=-1)
        return jnp.einsum('bhqk,bhkd->bhqd', p.astype(v.dtype), v)

    return sweep_blocks(one_block, s_len // BLOCK_Q)


def forgetting_attention(q, k, v, log_f):
    s_len, d = q.shape[2], q.shape[3]
    scale = d ** -0.5
    c = lax.cumsum(log_f, axis=2)
    k_pos = jnp.arange(s_len)

    def one_block(i):
        start = i * BLOCK_Q
        qb = lax.dynamic_slice_in_dim(q, start, BLOCK_Q, axis=2)
        cq = lax.dynamic_slice_in_dim(c, start, BLOCK_Q, axis=2)
        z = jnp.einsum('bhqd,bhkd->bhqk', qb, k).astype(jnp.float32) * scale
        z = z + cq[..., :, None] - c[..., None, :]
        q_pos = start + jnp.arange(BLOCK_Q)
        causal = k_pos[None, :] <= q_pos[:, None]
        p = jax.nn.softmax(jnp.where(causal, z, -jnp.inf), axis=-1)
        return jnp.einsum('bhqk,bhkd->bhqd', p.astype(v.dtype), v)

    return sweep_blocks(one_block, s_len // BLOCK_Q)


def even_layer(x, positions, pre_g, post_g, w_in, q_a_g, w_q_b, kv_a_g, w_kv_b, w_out):
    b, s, _ = x.shape
    h = rms_norm(x, pre_g)
    proj = h @ w_in
    cuts = [SB_WIDTH, 2 * SB_WIDTH, 3 * SB_WIDTH, 4 * SB_WIDTH,
            4 * SB_WIDTH + MLA_Q_LORA,
            4 * SB_WIDTH + MLA_Q_LORA + MLA_KV_LORA + MLA_ROPE_DIM]
    sb_q, sb_k, sb_v, sb_gate, q_a, kv_a, mla_gate = jnp.split(proj, cuts, axis=-1)

    o_a = stick_breaking_attention(split_heads(sb_q, SB_HEADS), split_heads(sb_k, SB_HEADS),
                                   split_heads(sb_v, SB_HEADS))
    o_a = merge_heads(o_a) * jax.nn.silu(sb_gate)

    q = (rms_norm(q_a, q_a_g) @ w_q_b).reshape(b, s, MLA_HEADS, MLA_NOPE_DIM + MLA_ROPE_DIM)
    q = q.transpose(0, 2, 1, 3)
    q_nope, q_rope = q[..., :MLA_NOPE_DIM], q[..., MLA_NOPE_DIM:]
    c_kv, k_rope = kv_a[..., :MLA_KV_LORA], kv_a[..., MLA_KV_LORA:]
    kv = (rms_norm(c_kv, kv_a_g) @ w_kv_b).reshape(b, s, MLA_HEADS, MLA_NOPE_DIM + MLA_V_DIM)
    kv = kv.transpose(0, 2, 1, 3)
    k_nope, v = kv[..., :MLA_NOPE_DIM], kv[..., MLA_NOPE_DIM:]
    cos, sin = rope_angles(positions, MLA_ROPE_DIM)
    q_rope = apply_rope(q_rope, cos[:, None], sin[:, None])
    k_rope = apply_rope(k_rope, cos, sin)
    o_b = mla_attention(q_nope, q_rope, k_nope, k_rope, v)
    o_b = merge_heads(o_b) * jax.nn.silu(mla_gate)

    y = jnp.concatenate([o_a, o_b], axis=-1) @ w_out
    return x + rms_norm(y, post_g)


def odd_layer(x, pre_g, post_g, w_in, b_f, w_out):
    h = rms_norm(x, pre_g)
    proj = h @ w_in
    cuts = [FOX_WIDTH, 2 * FOX_WIDTH, 3 * FOX_WIDTH, 4 * FOX_WIDTH]
    q, k, v, gate, f_logit = jnp.split(proj, cuts, axis=-1)
    log_f = jax.nn.log_sigmoid(f_logit.astype(jnp.float32) + b_f.astype(jnp.float32))
    log_f = log_f.transpose(0, 2, 1)
    o = forgetting_attention(split_heads(q, FOX_HEADS), split_heads(k, FOX_HEADS),
                             split_heads(v, FOX_HEADS), log_f)
    y = (merge_heads(o) * jax.nn.silu(gate)) @ w_out
    return x + rms_norm(y, post_g)


def _fwd_setup_inputs(seed: int = 0) -> dict:
    key = jax.random.key(seed)
    ks = jax.random.split(key, 20)

    def w(k, shape):
        return jax.random.normal(k, shape, jnp.float32) * (shape[0] ** -0.5)

    def gain(k, n):
        return 1.0 + 0.02 * jax.random.normal(k, (n,), jnp.float32)

    x = jax.random.normal(ks[0], (BATCH, SEQ, D_MODEL), jnp.float32)
    positions = jnp.broadcast_to(jnp.arange(SEQ, dtype=jnp.int32)[None, :], (BATCH, SEQ))
    return {
        "x": x,
        "positions": positions,
        "l0_pre_g": gain(ks[1], D_MODEL),
        "l0_post_g": gain(ks[2], D_MODEL),
        "l0_w_in": w(ks[3], (D_MODEL, EVEN_IN_WIDTH)),
        "l0_q_a_g": gain(ks[4], MLA_Q_LORA),
        "l0_w_q_b": w(ks[5], (MLA_Q_LORA, MLA_HEADS * (MLA_NOPE_DIM + MLA_ROPE_DIM))),
        "l0_kv_a_g": gain(ks[6], MLA_KV_LORA),
        "l0_w_kv_b": w(ks[7], (MLA_KV_LORA, MLA_HEADS * (MLA_NOPE_DIM + MLA_V_DIM))),
        "l0_w_out": w(ks[8], (SB_WIDTH + MLA_WIDTH, D_MODEL)),
        "l1_pre_g": gain(ks[9], D_MODEL),
        "l1_post_g": gain(ks[10], D_MODEL),
        "l1_w_in": w(ks[11], (D_MODEL, ODD_IN_WIDTH)),
        "l1_b_f": 2.0 + 0.5 * jax.random.normal(ks[12], (FOX_HEADS,), jnp.float32),
        "l1_w_out": w(ks[13], (FOX_WIDTH, D_MODEL)),
    }


def _fwd_reference(x, positions, l0_pre_g, l0_post_g, l0_w_in, l0_q_a_g, l0_w_q_b, l0_kv_a_g,
              l0_w_kv_b, l0_w_out, l1_pre_g, l1_post_g, l1_w_in, l1_b_f, l1_w_out):
    layer_params = [
        (l0_pre_g, l0_post_g, l0_w_in, l0_q_a_g, l0_w_q_b, l0_kv_a_g, l0_w_kv_b, l0_w_out),
        (l1_pre_g, l1_post_g, l1_w_in, l1_b_f, l1_w_out),
    ]
    for layer in range(DEPTH):
        p = layer_params[layer]
        if layer % 2 == 0:
            x = even_layer(x, positions, *p)
        else:
            x = odd_layer(x, *p)
    return x


import jax as _jax
import jax.numpy as _jnp

TWIN_FORMAT = 'train_step'
FWD_PARAMS = ['x', 'positions', 'l0_pre_g', 'l0_post_g', 'l0_w_in', 'l0_q_a_g', 'l0_w_q_b', 'l0_kv_a_g', 'l0_w_kv_b', 'l0_w_out', 'l1_pre_g', 'l1_post_g', 'l1_w_in', 'l1_b_f', 'l1_w_out']
TWIN_WEIGHTS = ['l0_pre_g', 'l0_post_g', 'l0_w_in', 'l0_q_a_g', 'l0_w_q_b', 'l0_kv_a_g', 'l0_w_kv_b', 'l0_w_out', 'l1_pre_g', 'l1_post_g', 'l1_w_in', 'l1_b_f', 'l1_w_out']
TWIN_DIFF_INPUT = 'x'
TWIN_INPUTS = ['x', 'positions', 'l0_pre_g', 'l0_post_g', 'l0_w_in', 'l0_q_a_g', 'l0_w_q_b', 'l0_kv_a_g', 'l0_w_kv_b', 'l0_w_out', 'l1_pre_g', 'l1_post_g', 'l1_w_in', 'l1_b_f', 'l1_w_out', 'loss_target', 'm_l0_pre_g', 'm_l0_post_g', 'm_l0_w_in', 'm_l0_q_a_g', 'm_l0_w_q_b', 'm_l0_kv_a_g', 'm_l0_w_kv_b', 'm_l0_w_out', 'm_l1_pre_g', 'm_l1_post_g', 'm_l1_w_in', 'm_l1_b_f', 'm_l1_w_out', 'v_l0_pre_g', 'v_l0_post_g', 'v_l0_w_in', 'v_l0_q_a_g', 'v_l0_w_q_b', 'v_l0_kv_a_g', 'v_l0_w_kv_b', 'v_l0_w_out', 'v_l1_pre_g', 'v_l1_post_g', 'v_l1_w_in', 'v_l1_b_f', 'v_l1_w_out']
TWIN_OUTPUTS = ['loss', 'grad_x', 'grad_l0_pre_g', 'grad_l0_post_g', 'grad_l0_w_in', 'grad_l0_q_a_g', 'grad_l0_w_q_b', 'grad_l0_kv_a_g', 'grad_l0_w_kv_b', 'grad_l0_w_out', 'grad_l1_pre_g', 'grad_l1_post_g', 'grad_l1_w_in', 'grad_l1_b_f', 'grad_l1_w_out', 'delta_l0_pre_g', 'delta_l0_post_g', 'delta_l0_w_in', 'delta_l0_q_a_g', 'delta_l0_w_q_b', 'delta_l0_kv_a_g', 'delta_l0_w_kv_b', 'delta_l0_w_out', 'delta_l1_pre_g', 'delta_l1_post_g', 'delta_l1_w_in', 'delta_l1_b_f', 'delta_l1_w_out', 'new_m_l0_pre_g', 'new_m_l0_post_g', 'new_m_l0_w_in', 'new_m_l0_q_a_g', 'new_m_l0_w_q_b', 'new_m_l0_kv_a_g', 'new_m_l0_w_kv_b', 'new_m_l0_w_out', 'new_m_l1_pre_g', 'new_m_l1_post_g', 'new_m_l1_w_in', 'new_m_l1_b_f', 'new_m_l1_w_out', 'new_v_l0_pre_g', 'new_v_l0_post_g', 'new_v_l0_w_in', 'new_v_l0_q_a_g', 'new_v_l0_w_q_b', 'new_v_l0_kv_a_g', 'new_v_l0_w_kv_b', 'new_v_l0_w_out', 'new_v_l1_pre_g', 'new_v_l1_post_g', 'new_v_l1_w_in', 'new_v_l1_b_f', 'new_v_l1_w_out']
TWIN_LEAF_KINDS = {'loss': 'loss', 'grad_x': 'grad_x', 'grad_l0_pre_g': 'grad_w', 'grad_l0_post_g': 'grad_w', 'grad_l0_w_in': 'grad_w', 'grad_l0_q_a_g': 'grad_w', 'grad_l0_w_q_b': 'grad_w', 'grad_l0_kv_a_g': 'grad_w', 'grad_l0_w_kv_b': 'grad_w', 'grad_l0_w_out': 'grad_w', 'grad_l1_pre_g': 'grad_w', 'grad_l1_post_g': 'grad_w', 'grad_l1_w_in': 'grad_w', 'grad_l1_b_f': 'grad_w', 'grad_l1_w_out': 'grad_w', 'delta_l0_pre_g': 'delta_w', 'delta_l0_post_g': 'delta_w', 'delta_l0_w_in': 'delta_w', 'delta_l0_q_a_g': 'delta_w', 'delta_l0_w_q_b': 'delta_w', 'delta_l0_kv_a_g': 'delta_w', 'delta_l0_w_kv_b': 'delta_w', 'delta_l0_w_out': 'delta_w', 'delta_l1_pre_g': 'delta_w', 'delta_l1_post_g': 'delta_w', 'delta_l1_w_in': 'delta_w', 'delta_l1_b_f': 'delta_w', 'delta_l1_w_out': 'delta_w', 'new_m_l0_pre_g': 'new_m', 'new_m_l0_post_g': 'new_m', 'new_m_l0_w_in': 'new_m', 'new_m_l0_q_a_g': 'new_m', 'new_m_l0_w_q_b': 'new_m', 'new_m_l0_kv_a_g': 'new_m', 'new_m_l0_w_kv_b': 'new_m', 'new_m_l0_w_out': 'new_m', 'new_m_l1_pre_g': 'new_m', 'new_m_l1_post_g': 'new_m', 'new_m_l1_w_in': 'new_m', 'new_m_l1_b_f': 'new_m', 'new_m_l1_w_out': 'new_m', 'new_v_l0_pre_g': 'new_v', 'new_v_l0_post_g': 'new_v', 'new_v_l0_w_in': 'new_v', 'new_v_l0_q_a_g': 'new_v', 'new_v_l0_w_q_b': 'new_v', 'new_v_l0_kv_a_g': 'new_v', 'new_v_l0_w_kv_b': 'new_v', 'new_v_l0_w_out': 'new_v', 'new_v_l1_pre_g': 'new_v', 'new_v_l1_post_g': 'new_v', 'new_v_l1_w_in': 'new_v', 'new_v_l1_b_f': 'new_v', 'new_v_l1_w_out': 'new_v'}


def _forward(args):
    return _fwd_reference(*[args[k] for k in FWD_PARAMS])


def _output_shape():
    out = _jax.eval_shape(lambda: _forward(_fwd_setup_inputs(0)))
    return out.shape, out.dtype

N_MICROBATCH = 1
ADAM_LR = 0.001
ADAM_B1 = 0.9
ADAM_B2 = 0.999
ADAM_EPS = 1e-08
ADAM_WD = 0.01
ADAM_STEP = 10
PER_EXAMPLE_BATCH_AXIS = {'x': 0, 'positions': 0, 'loss_target': 0}
SHARED_INPUTS = []
_WEIGHT_DTYPES = {'l0_pre_g': _jnp.float32, 'l0_post_g': _jnp.float32, 'l0_w_in': _jnp.float32, 'l0_q_a_g': _jnp.float32, 'l0_w_q_b': _jnp.float32, 'l0_kv_a_g': _jnp.float32, 'l0_w_kv_b': _jnp.float32, 'l0_w_out': _jnp.float32, 'l1_pre_g': _jnp.float32, 'l1_post_g': _jnp.float32, 'l1_w_in': _jnp.float32, 'l1_b_f': _jnp.float32, 'l1_w_out': _jnp.float32}
MOMENT_SCALE = {'l0_pre_g': 8.345936e-01, 'l0_post_g': 3.191352e+01, 'l0_w_in': 4.544502e-01, 'l0_q_a_g': 2.097805e-01, 'l0_w_q_b': 1.518997e-01, 'l0_kv_a_g': 3.934604e-01, 'l0_w_kv_b': 1.904155e-01, 'l0_w_out': 5.133463e-01, 'l1_pre_g': 5.627700e-01, 'l1_post_g': 3.189334e+01, 'l1_w_in': 2.865394e-01, 'l1_b_f': 1.780274e+00, 'l1_w_out': 3.363694e-01}


def _to_microbatches(a, axis):
    t = _jnp.moveaxis(a, axis, 0)
    t = t.reshape((N_MICROBATCH, t.shape[0] // N_MICROBATCH) + t.shape[1:])
    return _jnp.moveaxis(t, 1, axis + 1)


def setup_inputs(seed: int = 0) -> dict:
    inp = _fwd_setup_inputs(seed)
    key = _jax.random.fold_in(_jax.random.key(seed), 7919)
    shape, _ = _output_shape()
    out = dict(inp)
    out["loss_target"] = _jax.random.normal(_jax.random.fold_in(key, 0), shape, _jnp.float32)
    for i, name in enumerate(TWIN_WEIGHTS):
        w = inp[name].astype(_jnp.float32)
        if MOMENT_SCALE is None:
            s = _jnp.sqrt(_jnp.mean(_jnp.square(w)) + 1e-30)
        else:
            s = MOMENT_SCALE[name]
        km, kv = _jax.random.split(_jax.random.fold_in(key, i + 1))
        out[name] = w
        out["m_" + name] = s * _jax.random.normal(km, w.shape, _jnp.float32)
        out["v_" + name] = (s * s) * _jax.random.uniform(kv, w.shape, _jnp.float32, 0.5, 1.5)
    if N_MICROBATCH > 1:
        for name, axis in PER_EXAMPLE_BATCH_AXIS.items():
            out[name] = _to_microbatches(out[name], axis)
    return {'x': out['x'], 'positions': out['positions'], 'l0_pre_g': out['l0_pre_g'], 'l0_post_g': out['l0_post_g'], 'l0_w_in': out['l0_w_in'], 'l0_q_a_g': out['l0_q_a_g'], 'l0_w_q_b': out['l0_w_q_b'], 'l0_kv_a_g': out['l0_kv_a_g'], 'l0_w_kv_b': out['l0_w_kv_b'], 'l0_w_out': out['l0_w_out'], 'l1_pre_g': out['l1_pre_g'], 'l1_post_g': out['l1_post_g'], 'l1_w_in': out['l1_w_in'], 'l1_b_f': out['l1_b_f'], 'l1_w_out': out['l1_w_out'], 'loss_target': out['loss_target'], 'm_l0_pre_g': out['m_l0_pre_g'], 'm_l0_post_g': out['m_l0_post_g'], 'm_l0_w_in': out['m_l0_w_in'], 'm_l0_q_a_g': out['m_l0_q_a_g'], 'm_l0_w_q_b': out['m_l0_w_q_b'], 'm_l0_kv_a_g': out['m_l0_kv_a_g'], 'm_l0_w_kv_b': out['m_l0_w_kv_b'], 'm_l0_w_out': out['m_l0_w_out'], 'm_l1_pre_g': out['m_l1_pre_g'], 'm_l1_post_g': out['m_l1_post_g'], 'm_l1_w_in': out['m_l1_w_in'], 'm_l1_b_f': out['m_l1_b_f'], 'm_l1_w_out': out['m_l1_w_out'], 'v_l0_pre_g': out['v_l0_pre_g'], 'v_l0_post_g': out['v_l0_post_g'], 'v_l0_w_in': out['v_l0_w_in'], 'v_l0_q_a_g': out['v_l0_q_a_g'], 'v_l0_w_q_b': out['v_l0_w_q_b'], 'v_l0_kv_a_g': out['v_l0_kv_a_g'], 'v_l0_w_kv_b': out['v_l0_w_kv_b'], 'v_l0_w_out': out['v_l0_w_out'], 'v_l1_pre_g': out['v_l1_pre_g'], 'v_l1_post_g': out['v_l1_post_g'], 'v_l1_w_in': out['v_l1_w_in'], 'v_l1_b_f': out['v_l1_b_f'], 'v_l1_w_out': out['v_l1_w_out']}


def _loss(weights, diff, rest, loss_target):
    with _jax.named_scope("forward"):
        args = {**rest, TWIN_DIFF_INPUT: diff, **{k: w.astype(_WEIGHT_DTYPES[k]) for k, w in weights.items()}}
        y = _forward(args)
    with _jax.named_scope("loss_head"):
        err = _jnp.square(y.astype(_jnp.float32) - loss_target)
        return 0.5 * _jnp.sum(_jnp.mean(err, axis=-1)) if err.ndim else 0.5 * err


def _adamw(w, g, m, v):
    m = ADAM_B1 * m + (1.0 - ADAM_B1) * g
    v = ADAM_B2 * v + (1.0 - ADAM_B2) * _jnp.square(g)
    m_hat = m / (1.0 - ADAM_B1 ** ADAM_STEP)
    v_hat = v / (1.0 - ADAM_B2 ** ADAM_STEP)
    delta = -ADAM_LR * (m_hat / (_jnp.sqrt(v_hat) + ADAM_EPS) + ADAM_WD * w)
    return delta, m, v


def reference(x, positions, l0_pre_g, l0_post_g, l0_w_in, l0_q_a_g, l0_w_q_b, l0_kv_a_g, l0_w_kv_b, l0_w_out, l1_pre_g, l1_post_g, l1_w_in, l1_b_f, l1_w_out, loss_target, m_l0_pre_g, m_l0_post_g, m_l0_w_in, m_l0_q_a_g, m_l0_w_q_b, m_l0_kv_a_g, m_l0_w_kv_b, m_l0_w_out, m_l1_pre_g, m_l1_post_g, m_l1_w_in, m_l1_b_f, m_l1_w_out, v_l0_pre_g, v_l0_post_g, v_l0_w_in, v_l0_q_a_g, v_l0_w_q_b, v_l0_kv_a_g, v_l0_w_kv_b, v_l0_w_out, v_l1_pre_g, v_l1_post_g, v_l1_w_in, v_l1_b_f, v_l1_w_out):
    given = dict(x=x, positions=positions, l0_pre_g=l0_pre_g, l0_post_g=l0_post_g, l0_w_in=l0_w_in, l0_q_a_g=l0_q_a_g, l0_w_q_b=l0_w_q_b, l0_kv_a_g=l0_kv_a_g, l0_w_kv_b=l0_w_kv_b, l0_w_out=l0_w_out, l1_pre_g=l1_pre_g, l1_post_g=l1_post_g, l1_w_in=l1_w_in, l1_b_f=l1_b_f, l1_w_out=l1_w_out, loss_target=loss_target, m_l0_pre_g=m_l0_pre_g, m_l0_post_g=m_l0_post_g, m_l0_w_in=m_l0_w_in, m_l0_q_a_g=m_l0_q_a_g, m_l0_w_q_b=m_l0_w_q_b, m_l0_kv_a_g=m_l0_kv_a_g, m_l0_w_kv_b=m_l0_w_kv_b, m_l0_w_out=m_l0_w_out, m_l1_pre_g=m_l1_pre_g, m_l1_post_g=m_l1_post_g, m_l1_w_in=m_l1_w_in, m_l1_b_f=m_l1_b_f, m_l1_w_out=m_l1_w_out, v_l0_pre_g=v_l0_pre_g, v_l0_post_g=v_l0_post_g, v_l0_w_in=v_l0_w_in, v_l0_q_a_g=v_l0_q_a_g, v_l0_w_q_b=v_l0_w_q_b, v_l0_kv_a_g=v_l0_kv_a_g, v_l0_w_kv_b=v_l0_w_kv_b, v_l0_w_out=v_l0_w_out, v_l1_pre_g=v_l1_pre_g, v_l1_post_g=v_l1_post_g, v_l1_w_in=v_l1_w_in, v_l1_b_f=v_l1_b_f, v_l1_w_out=v_l1_w_out)
    weights = {n: given[n] for n in TWIN_WEIGHTS}
    shared = {n: given[n] for n in SHARED_INPUTS}
    per_example = {n: given[n] for n in ['x', 'positions']}
    grad_fn = _jax.value_and_grad(_loss, argnums=(0, 1))

    def one_microbatch(ex, loss_target):
        ex = dict(ex)
        diff = ex.pop(TWIN_DIFF_INPUT)
        return grad_fn(weights, diff, {**shared, **ex}, loss_target)

    if N_MICROBATCH == 1:
        loss, (grad_w, grad_x) = one_microbatch(per_example, given["loss_target"])
    else:
        def body(carry, xs):
            loss_sum, grad_sum = carry
            l_k, (gw_k, gx_k) = one_microbatch(xs[0], xs[1])
            with _jax.named_scope("update"):
                return (loss_sum + l_k, _jax.tree.map(_jnp.add, grad_sum, gw_k)), gx_k

        init = (_jnp.zeros((), _jnp.float32), _jax.tree.map(_jnp.zeros_like, weights))
        (loss, grad_w), grad_x = _jax.lax.scan(body, init, (per_example, given["loss_target"]))
    with _jax.named_scope("update"):
        delta_w, new_m, new_v = {}, {}, {}
        for n in TWIN_WEIGHTS:
            delta_w[n], new_m[n], new_v[n] = _adamw(weights[n], grad_w[n], given["m_" + n], given["v_" + n])
    return (loss, grad_x, *[grad_w[n] for n in TWIN_WEIGHTS], *[delta_w[n] for n in TWIN_WEIGHTS],
            *[new_m[n] for n in TWIN_WEIGHTS], *[new_v[n] for n in TWIN_WEIGHTS])
```

```python
import functools

import numpy as np
import jax
import jax.numpy as jnp
from jax import lax
from jax.experimental import pallas as pl
from jax.experimental.pallas import tpu as pltpu

F32 = jnp.float32
BF16 = jnp.bfloat16
MESH = pl.DeviceIdType.MESH

D_MODEL = 1024
RMS_EPS = 1e-6
ROPE_THETA = 10000.0
SB_WIDTH = 512
MLA_Q_LORA = 384
MLA_KV_LORA = 256
MLA_ROPE_DIM = 32
MLA_WIDTH = 512
FOX_WIDTH = 1024
FOX_HEADS = 16
EVEN_IN_WIDTH = 3232
ODD_IN_WIDTH = 4112

ADAM_LR = 0.001
ADAM_B1 = 0.9
ADAM_B2 = 0.999
ADAM_EPS = 1e-08
ADAM_WD = 0.01
ADAM_STEP = 10

LANES = 128
VMEM_LIMIT = 56 * 1024 * 1024

L0_PREP = 0
L0_PREP_W = 768
L0_SBG = 768
L0_MLG = 1280
L0_SBQ = 1792
L0_SBK = 2304
L0_SBV = 2816
L0_WIDTH = 3328
L1_Q = 0
L1_K = 1024
L1_V = 2048
L1_G = 3072
L1_F = 4096
L1_WIDTH = 4224

ATT_T = 256
NEG = -1e30

PACK_ROWS = 20480
PACK_HALF = PACK_ROWS // 2
SMALL_ROWS = 40


def _cparams(sem, **kw):
    return pltpu.CompilerParams(dimension_semantics=sem, vmem_limit_bytes=VMEM_LIMIT, **kw)


def _dot(a, b):
    return lax.dot_general(a, b, (((1,), (0,)), ((), ())), preferred_element_type=F32)


def _dot_nt(a, b):
    return lax.dot_general(a, b, (((1,), (1,)), ((), ())), preferred_element_type=F32)


def _dot_tn(a, b):
    return lax.dot_general(a, b, (((0,), (0,)), ((), ())), preferred_element_type=F32)


def _sigmoid(x):
    return 1.0 / (1.0 + jnp.exp(-x))


def _rstd(x):
    return lax.rsqrt(jnp.mean(x * x, axis=-1, keepdims=True) + RMS_EPS)


def _norm_bwd(x, g, dy):
    r = _rstd(x)
    xn = x * r
    dxn = dy * g
    dx = r * (dxn - xn * jnp.mean(dxn * xn, axis=-1, keepdims=True))
    return dx, dy * xn


def _split3(x):
    hi = x.astype(BF16)
    r1 = x - hi.astype(F32)
    mid = r1.astype(BF16)
    lo = (r1 - mid.astype(F32)).astype(BF16)
    return hi, mid, lo


def _pick(n, cands):
    for c in cands:
        if n % c == 0:
            return c
    raise ValueError(n)


def _norm_matmul(x, g, w, name):
    S, K = x.shape
    N = w.shape[1]
    tm = _pick(S, (512, 256))
    tn = _pick(N, (512, 384, 256, 128))

    def body(x_ref, g_ref, w_ref, o_ref, h_ref):
        @pl.when(pl.program_id(1) == 0)
        def _():
            xv = x_ref[...]
            h_ref[...] = ((xv * _rstd(xv)) * g_ref[...]).astype(BF16)
        o_ref[...] = _dot(h_ref[...], w_ref[...])

    return pl.pallas_call(
        body, name=name, grid=(S // tm, N // tn),
        in_specs=[pl.BlockSpec((tm, K), lambda i, j: (i, 0)),
                  pl.BlockSpec((1, K), lambda i, j: (0, 0)),
                  pl.BlockSpec((K, tn), lambda i, j: (0, j))],
        out_specs=[pl.BlockSpec((tm, tn), lambda i, j: (i, j)),
                   pl.BlockSpec((tm, K), lambda i, j: (i, 0))],
        out_shape=[jax.ShapeDtypeStruct((S, N), F32), jax.ShapeDtypeStruct((S, K), BF16)],
        compiler_params=_cparams(("parallel", "arbitrary")),
    )(x, g, w)


def _matmul_tn(a, b, name):
    S, M = a.shape
    N = b.shape[1]
    tm = _pick(M, (512, 384, 256))
    tn = _pick(N, (512, 384, 256, 128))
    ts = _pick(S, (512, 256))

    def body(a_ref, b_ref, o_ref):
        @pl.when(pl.program_id(2) == 0)
        def _():
            o_ref[...] = jnp.zeros_like(o_ref)
        o_ref[...] += _dot_tn(a_ref[...].astype(BF16), b_ref[...].astype(BF16))

    return pl.pallas_call(
        body, name=name, grid=(M // tm, N // tn, S // ts),
        in_specs=[pl.BlockSpec((ts, tm), lambda i, j, k: (k, i)),
                  pl.BlockSpec((ts, tn), lambda i, j, k: (k, j))],
        out_specs=pl.BlockSpec((tm, tn), lambda i, j, k: (i, j)),
        out_shape=jax.ShapeDtypeStruct((M, N), F32),
        compiler_params=_cparams(("parallel", "parallel", "arbitrary")),
    )(a, b)


def _in_proj_bwd(dproj, w, x, g, dx_up, name):
    S, N = dproj.shape
    K = w.shape[0]
    tm = _pick(S, (512, 256))
    tk = N // 2 if (N // 2) % LANES == 0 and N % 2 == 0 else N // 3
    nk = N // tk

    def body(d_ref, w_ref, x_ref, g_ref, u_ref, dx_ref, dg_ref, acc_ref):
        i, k = pl.program_id(0), pl.program_id(1)

        @pl.when(k == 0)
        def _():
            acc_ref[...] = jnp.zeros_like(acc_ref)

        @pl.when((i == 0) & (k == 0))
        def _():
            dg_ref[...] = jnp.zeros_like(dg_ref)

        acc_ref[...] += _dot_nt(d_ref[...].astype(BF16), w_ref[...])

        @pl.when(k == nk - 1)
        def _():
            dx, dgrow = _norm_bwd(x_ref[...], g_ref[...], acc_ref[...])
            dx_ref[...] = u_ref[...] + dx
            dg_ref[...] += jnp.sum(dgrow, axis=0, keepdims=True)

    return pl.pallas_call(
        body, name=name, grid=(S // tm, nk),
        in_specs=[pl.BlockSpec((tm, tk), lambda i, k: (i, k)),
                  pl.BlockSpec((K, tk), lambda i, k: (0, k)),
                  pl.BlockSpec((tm, K), lambda i, k: (i, 0)),
                  pl.BlockSpec((1, K), lambda i, k: (0, 0)),
                  pl.BlockSpec((tm, K), lambda i, k: (i, 0))],
        out_specs=[pl.BlockSpec((tm, K), lambda i, k: (i, 0)),
                   pl.BlockSpec((1, K), lambda i, k: (0, 0))],
        out_shape=[jax.ShapeDtypeStruct((S, K), F32), jax.ShapeDtypeStruct((1, K), F32)],
        scratch_shapes=[pltpu.VMEM((tm, K), F32)],
        compiler_params=_cparams(("arbitrary", "arbitrary")),
    )(dproj, w, x, g, dx_up)


def _out_proj(og_a, og_b, blk_a, blk_b, w, x, g, target, name):
    S = x.shape[0]
    D = x.shape[1]
    tm = _pick(S, (512, 256))
    with_loss = target is not None

    def body(*refs):
        if with_loss:
            a_ref, b_ref, wa_ref, wb_ref, x_ref, g_ref, t_ref, y_ref, o_ref, l_ref = refs
        else:
            a_ref, b_ref, wa_ref, wb_ref, x_ref, g_ref, y_ref, o_ref = refs
        y = _dot(a_ref[...], wa_ref[...]) + _dot(b_ref[...], wb_ref[...])
        y_ref[...] = y
        xn = x_ref[...] + (y * _rstd(y)) * g_ref[...]
        if with_loss:
            @pl.when(pl.program_id(0) == 0)
            def _():
                l_ref[...] = jnp.zeros_like(l_ref)
            d = xn - t_ref[...]
            o_ref[...] = d / float(D)
            l_ref[...] += jnp.sum(d * d, axis=0, keepdims=True)
        else:
            o_ref[...] = xn

    row = lambda i: (i, 0)
    in_specs = [pl.BlockSpec((tm, 512), lambda i: (i, blk_a)),
                pl.BlockSpec((tm, 512), lambda i: (i, blk_b)),
                pl.BlockSpec((512, D), lambda i: (0, 0)),
                pl.BlockSpec((512, D), lambda i: (1, 0)),
                pl.BlockSpec((tm, D), row),
                pl.BlockSpec((1, D), lambda i: (0, 0))]
    out_specs = [pl.BlockSpec((tm, D), row), pl.BlockSpec((tm, D), row)]
    out_shape = [jax.ShapeDtypeStruct((S, D), F32), jax.ShapeDtypeStruct((S, D), F32)]
    args = [og_a, og_b, w, w, x, g]
    if with_loss:
        in_specs.append(pl.BlockSpec((tm, D), row))
        out_specs.append(pl.BlockSpec((1, D), lambda i: (0, 0)))
        out_shape.append(jax.ShapeDtypeStruct((1, D), F32))
        args.append(target)
    return pl.pallas_call(
        body, name=name, grid=(S // tm,), in_specs=in_specs, out_specs=out_specs, out_shape=out_shape,
        compiler_params=_cparams(("arbitrary",)),
    )(*args)


def _out_proj_bwd(dx_up, y, g, w, proj, gate_offs, o_a, o_b, oblk_a, oblk_b, name):
    S, D = y.shape
    tm = _pick(S, (256,))
    gblk = [off // 256 + c for off in gate_offs for c in range(2)]

    def body(u_ref, y_ref, g_ref, w_ref, g0, g1, g2, g3, oa_ref, ob_ref, dy_ref, do_ref, dgate_ref, dg_ref):
        @pl.when(pl.program_id(0) == 0)
        def _():
            dg_ref[...] = jnp.zeros_like(dg_ref)
        dy, dgrow = _norm_bwd(y_ref[...], g_ref[...], u_ref[...])
        dg_ref[...] += jnp.sum(dgrow, axis=0, keepdims=True)
        dyb = dy.astype(BF16)
        dy_ref[...] = dyb
        dog = _dot_nt(dyb, w_ref[...])
        gates = (g0, g1, g2, g3)
        for c in range(4):
            gt = gates[c][...]
            sg = _sigmoid(gt)
            o_ref = oa_ref if c < 2 else ob_ref
            ov = o_ref[:, (c % 2) * 256:(c % 2 + 1) * 256]
            dc = dog[:, c * 256:(c + 1) * 256]
            do_ref[:, c * 256:(c + 1) * 256] = dc * (gt * sg)
            dgate_ref[:, c * 256:(c + 1) * 256] = dc * ov * (sg * (1.0 + gt * (1.0 - sg)))

    row = lambda i: (i, 0)
    gspec = lambda c: pl.BlockSpec((tm, 256), lambda i: (i, gblk[c]))
    return pl.pallas_call(
        body, name=name, grid=(S // tm,),
        in_specs=[pl.BlockSpec((tm, D), row), pl.BlockSpec((tm, D), row), pl.BlockSpec((1, D), lambda i: (0, 0)),
                  pl.BlockSpec((D, D), lambda i: (0, 0)),
                  gspec(0), gspec(1), gspec(2), gspec(3),
                  pl.BlockSpec((tm, 512), lambda i: (i, oblk_a)),
                  pl.BlockSpec((tm, 512), lambda i: (i, oblk_b))],
        out_specs=[pl.BlockSpec((tm, D), row), pl.BlockSpec((tm, D), row), pl.BlockSpec((tm, D), row),
                   pl.BlockSpec((1, D), lambda i: (0, 0))],
        out_shape=[jax.ShapeDtypeStruct((S, D), BF16), jax.ShapeDtypeStruct((S, D), F32),
                   jax.ShapeDtypeStruct((S, D), F32), jax.ShapeDtypeStruct((1, D), F32)],
        compiler_params=_cparams(("arbitrary",)),
    )(dx_up, y, g, w, proj, proj, proj, proj, o_a, o_b)


def _rope_tables(pos, invf, name):
    S = pos.shape[0]
    tm = _pick(S, (512, 256))

    def body(p_ref, f_ref, c_ref, s1_ref, s2_ref):
        lane = lax.broadcasted_iota(jnp.int32, (1, LANES), 1)
        ang = p_ref[...].astype(F32) * f_ref[...]
        c, s = jnp.cos(ang), jnp.sin(ang)
        c_ref[...] = jnp.where((lane >= 64) & (lane < 96), c, 1.0)
        s1_ref[...] = jnp.where((lane >= 64) & (lane < 80), -s, 0.0)
        s2_ref[...] = jnp.where((lane >= 80) & (lane < 96), s, 0.0)

    spec = pl.BlockSpec((tm, LANES), lambda i: (i, 0))
    return pl.pallas_call(
        body, name=name, grid=(S // tm,),
        in_specs=[pl.BlockSpec((tm, 1), lambda i: (i, 0)), pl.BlockSpec((1, LANES), lambda i: (0, 0))],
        out_specs=[spec, spec, spec],
        out_shape=[jax.ShapeDtypeStruct((S, LANES), F32)] * 3,
        compiler_params=_cparams(("parallel",)),
    )(pos, invf)


def _rope(x, c, s1, s2):
    return x * c + pltpu.roll(x, LANES - 16, 1) * s1 + pltpu.roll(x, 16, 1) * s2


def _rope_t(d, c, s1, s2):
    return d * c + pltpu.roll(d * s1, 16, 1) + pltpu.roll(d * s2, LANES - 16, 1)


def _mla_prep(proj, gq, gkv, wq, wkv, cosT, s1T, s2T, name):
    S = proj.shape[0]
    tm = _pick(S, (256,))

    def body(p_ref, gq_ref, gkv_ref, wq_ref, wkv_ref, c_ref, s1_ref, s2_ref, q_ref, k_ref, v_ref, qn_ref, cn_ref):
        qa = p_ref[:, 0:384]
        ckv = p_ref[:, 384:640]
        kr = p_ref[:, 640:768]
        qn = ((qa * _rstd(qa)) * gq_ref[...]).astype(BF16)
        cn = ((ckv * _rstd(ckv)) * gkv_ref[...]).astype(BF16)
        qn_ref[...] = qn
        cn_ref[...] = cn
        qb = _dot(qn, wq_ref[...])
        kvb = _dot(cn, wkv_ref[...])
        c, s1, s2 = c_ref[...], s1_ref[...], s2_ref[...]
        krr = _rope(kr, c, s1, s2)
        for h in range(8):
            sl = slice(h * LANES, (h + 1) * LANES)
            q_ref[:, sl] = _rope(qb[:, sl], c, s1, s2)
            k_ref[:, sl] = kvb[:, sl] + krr
        v_ref[...] = kvb[:, 1024:1536]

    row = lambda i: (i, 0)
    fixed = lambda i: (0, 0)
    tspec = pl.BlockSpec((tm, LANES), row)
    return pl.pallas_call(
        body, name=name, grid=(S // tm,),
        in_specs=[pl.BlockSpec((tm, L0_PREP_W), lambda i: (i, L0_PREP // L0_PREP_W)),
                  pl.BlockSpec((1, 384), fixed), pl.BlockSpec((1, 256), fixed),
                  pl.BlockSpec((384, 1024), fixed), pl.BlockSpec((256, 1536), fixed), tspec, tspec, tspec],
        out_specs=[pl.BlockSpec((tm, 1024), row), pl.BlockSpec((tm, 1024), row), pl.BlockSpec((tm, 512), row),
                   pl.BlockSpec((tm, 384), row), pl.BlockSpec((tm, 256), row)],
        out_shape=[jax.ShapeDtypeStruct((S, 1024), F32), jax.ShapeDtypeStruct((S, 1024), F32),
                   jax.ShapeDtypeStruct((S, 512), F32), jax.ShapeDtypeStruct((S, 384), BF16),
                   jax.ShapeDtypeStruct((S, 256), BF16)],
        compiler_params=_cparams(("parallel",)),
    )(proj, gq, gkv, wq, wkv, cosT, s1T, s2T)


def _mla_prep_bwd(dq, dk, dv, proj, gq, gkv, wq, wkv, cosT, s1T, s2T, name):
    S = proj.shape[0]
    tm = _pick(S, (256,))

    def body(dq_ref, dk_ref, dv_ref, p_ref, gq_ref, gkv_ref, wq_ref, wkv_ref, c_ref, s1_ref, s2_ref,
             dp_ref, dqb_ref, dkvb_ref, dgq_ref, dgkv_ref):
        @pl.when(pl.program_id(0) == 0)
        def _():
            dgq_ref[...] = jnp.zeros_like(dgq_ref)
            dgkv_ref[...] = jnp.zeros_like(dgkv_ref)
        c, s1, s2 = c_ref[...], s1_ref[...], s2_ref[...]
        lane = lax.broadcasted_iota(jnp.int32, (1, LANES), 1)
        dkr = jnp.zeros((tm, LANES), F32)
        for h in range(8):
            sl = slice(h * LANES, (h + 1) * LANES)
            dqb_ref[:, sl] = _rope_t(dq_ref[:, sl], c, s1, s2).astype(BF16)
            dkh = dk_ref[:, sl]
            dkvb_ref[:, sl] = dkh.astype(BF16)
            dkr = dkr + dkh
        dkvb_ref[:, 1024:1536] = dv_ref[...].astype(BF16)
        dkr = jnp.where((lane >= 64) & (lane < 96), _rope_t(dkr, c, s1, s2), 0.0)
        dqn = _dot_nt(dqb_ref[...], wq_ref[...])
        dcn = _dot_nt(dkvb_ref[...], wkv_ref[...])
        dqa, gq_row = _norm_bwd(p_ref[:, 0:384], gq_ref[...], dqn)
        dckv, gkv_row = _norm_bwd(p_ref[:, 384:640], gkv_ref[...], dcn)
        dp_ref[:, 0:384] = dqa
        dp_ref[:, 384:640] = dckv
        dp_ref[:, 640:768] = dkr
        dgq_ref[...] += jnp.sum(gq_row, axis=0, keepdims=True)
        dgkv_ref[...] += jnp.sum(gkv_row, axis=0, keepdims=True)

    row = lambda i: (i, 0)
    fixed = lambda i: (0, 0)
    tspec = pl.BlockSpec((tm, LANES), row)
    return pl.pallas_call(
        body, name=name, grid=(S // tm,),
        in_specs=[pl.BlockSpec((tm, 1024), row), pl.BlockSpec((tm, 1024), row), pl.BlockSpec((tm, 512), row),
                  pl.BlockSpec((tm, L0_PREP_W), lambda i: (i, L0_PREP // L0_PREP_W)),
                  pl.BlockSpec((1, 384), fixed), pl.BlockSpec((1, 256), fixed),
                  pl.BlockSpec((384, 1024), fixed), pl.BlockSpec((256, 1536), fixed), tspec, tspec, tspec],
        out_specs=[pl.BlockSpec((tm, L0_PREP_W), row), pl.BlockSpec((tm, 1024), row), pl.BlockSpec((tm, 1536), row),
                   pl.BlockSpec((1, 384), fixed), pl.BlockSpec((1, 256), fixed)],
        out_shape=[jax.ShapeDtypeStruct((S, L0_PREP_W), F32), jax.ShapeDtypeStruct((S, 1024), BF16),
                   jax.ShapeDtypeStruct((S, 1536), BF16), jax.ShapeDtypeStruct((1, 384), F32),
                   jax.ShapeDtypeStruct((1, 256), F32)],
        compiler_params=_cparams(("arbitrary",)),
    )(dq, dk, dv, proj, gq, gkv, wq, wkv, cosT, s1T, s2T)


def _fox_prep(proj, bf, name):
    S = proj.shape[0]
    tm = _pick(S, (256,))

    def body(f_ref, b_ref, c_ref, carry_ref):
        @pl.when(pl.program_id(0) == 0)
        def _():
            carry_ref[...] = jnp.zeros_like(carry_ref)
        u = f_ref[...] + b_ref[...]
        lf = jnp.minimum(u, 0.0) - jnp.log(1.0 + jnp.exp(-jnp.abs(u)))
        r = lax.broadcasted_iota(jnp.int32, (tm, tm), 0)
        cidx = lax.broadcasted_iota(jnp.int32, (tm, tm), 1)
        tri = (cidx <= r).astype(BF16)
        hi, mid, lo = _split3(lf)
        c = carry_ref[...] + (_dot(tri, hi) + _dot(tri, mid) + _dot(tri, lo))
        c_ref[...] = c
        carry_ref[...] = c[tm - 1:tm, :]

    return pl.pallas_call(
        body, name=name, grid=(S // tm,),
        in_specs=[pl.BlockSpec((tm, LANES), lambda i: (i, L1_F // LANES)), pl.BlockSpec((1, LANES), lambda i: (0, 0))],
        out_specs=pl.BlockSpec((tm, LANES), lambda i: (i, 0)),
        out_shape=jax.ShapeDtypeStruct((S, LANES), F32),
        scratch_shapes=[pltpu.VMEM((1, LANES), F32)],
        compiler_params=_cparams(("arbitrary",)),
    )(proj, bf)


def _fox_prep_bwd(dc, proj, bf, name):
    S = proj.shape[0]
    tm = _pick(S, (256,))
    nb = S // tm

    def body(dc_ref, f_ref, b_ref, df_ref, db_ref, carry_ref):
        @pl.when(pl.program_id(0) == 0)
        def _():
            carry_ref[...] = jnp.zeros_like(carry_ref)
            db_ref[...] = jnp.zeros_like(db_ref)
        r = lax.broadcasted_iota(jnp.int32, (tm, tm), 0)
        cidx = lax.broadcasted_iota(jnp.int32, (tm, tm), 1)
        tri = (cidx >= r).astype(BF16)
        hi, mid, lo = _split3(dc_ref[...])
        dlf = carry_ref[...] + (_dot(tri, hi) + _dot(tri, mid) + _dot(tri, lo))
        carry_ref[...] = dlf[0:1, :]
        u = f_ref[...] + b_ref[...]
        e = jnp.exp(-jnp.abs(u))
        sneg = jnp.where(u >= 0.0, e, 1.0) / (1.0 + e)
        lane = lax.broadcasted_iota(jnp.int32, (1, LANES), 1)
        df = jnp.where(lane < FOX_HEADS, dlf * sneg, 0.0)
        df_ref[...] = df
        db_ref[...] += jnp.sum(df, axis=0, keepdims=True)

    return pl.pallas_call(
        body, name=name, grid=(nb,),
        in_specs=[pl.BlockSpec((tm, LANES), lambda i: (nb - 1 - i, 0)),
                  pl.BlockSpec((tm, LANES), lambda i: (nb - 1 - i, L1_F // LANES)),
                  pl.BlockSpec((1, LANES), lambda i: (0, 0))],
        out_specs=[pl.BlockSpec((tm, LANES), lambda i: (nb - 1 - i, 0)), pl.BlockSpec((1, LANES), lambda i: (0, 0))],
        out_shape=[jax.ShapeDtypeStruct((S, LANES), F32), jax.ShapeDtypeStruct((1, LANES), F32)],
        scratch_shapes=[pltpu.VMEM((1, LANES), F32)],
        compiler_params=_cparams(("arbitrary",)),
    )(dc, proj, bf)


def _att_specs(kind, S, T):
    if kind == "sb":
        qo, ko, vo, go = L0_SBQ // LANES, L0_SBK // LANES, L0_SBV // LANES, L0_SBG // LANES
    elif kind == "fox":
        qo, ko, vo, go = L1_Q // LANES, L1_K // LANES, L1_V // LANES, L1_G // LANES
    else:
        go = L0_MLG // LANES
        return (pl.BlockSpec((T, 256), lambda p, i: (i, p)), pl.BlockSpec((S, 256), lambda p, i: (0, p)),
                pl.BlockSpec((S, LANES), lambda p, i: (0, p)), pl.BlockSpec((T, LANES), lambda p, i: (i, go + p)))
    return (pl.BlockSpec((T, LANES), lambda p, i: (i, qo + p)), pl.BlockSpec((S, LANES), lambda p, i: (0, ko + p)),
            pl.BlockSpec((S, LANES), lambda p, i: (0, vo + p)), pl.BlockSpec((T, LANES), lambda p, i: (i, go + p)))


def _head_q(kind, q_ref, m0, scale):
    if kind == "mla":
        return [q_ref[:, 0:LANES].astype(BF16), q_ref[:, LANES:2 * LANES].astype(BF16)]
    qv = q_ref[...] * scale
    return [jnp.where(m0, qv, 0.0).astype(BF16), jnp.where(m0, 0.0, qv).astype(BF16)]


def _head_k(kind, k_ref, start, T):
    if kind == "mla":
        return [k_ref[pl.ds(start, T), 0:LANES].astype(BF16), k_ref[pl.ds(start, T), LANES:2 * LANES].astype(BF16)]
    kb = k_ref[pl.ds(start, T), :].astype(BF16)
    return [kb, kb]


def _softmax_fwd(kind, qkvg, c_col, c_row, S, npairs, name):
    T = ATT_T
    nq = S // T
    fox = kind == "fox"
    scale = (96 if kind == "mla" else 64) ** -0.5

    def body(*refs):
        if fox:
            q_ref, k_ref, v_ref, g_ref, cc_ref, cr_ref, o_ref, og_ref, st_ref = refs
        else:
            q_ref, k_ref, v_ref, g_ref, o_ref, og_ref, st_ref = refs
        i = pl.program_id(1)
        m0 = lax.broadcasted_iota(jnp.int32, (1, LANES), 1) < 64
        causal = lax.broadcasted_iota(jnp.int32, (T, T), 1) <= lax.broadcasted_iota(jnp.int32, (T, T), 0)
        qh = _head_q(kind, q_ref, m0, scale)

        def tile(j, carry, masked):
            start = pl.multiple_of(j * T, T)
            vb = v_ref[pl.ds(start, T), :].astype(BF16)
            kh = _head_k(kind, k_ref, start, T)
            out = []
            for h in range(2):
                m, l, acc = carry[h]
                z = _dot_nt(qh[h], kh[h])
                if kind == "mla":
                    z = z * scale
                if fox:
                    z = z + (cc_ref[h] - cr_ref[h, pl.ds(j, 1), :])
                if masked:
                    z = jnp.where(causal, z, NEG)
                mn = jnp.maximum(m, jnp.max(z, axis=1, keepdims=True))
                a = jnp.exp(m - mn)
                p = jnp.exp(z - mn)
                out.append((mn, a * l + jnp.sum(p, axis=1, keepdims=True), a * acc + _dot(p.astype(BF16), vb)))
            return tuple(out)

        init = tuple((jnp.full((T, 1), NEG, F32), jnp.zeros((T, 1), F32), jnp.zeros((T, LANES), F32)) for _ in range(2))
        carry = lax.fori_loop(0, i, lambda j, c: tile(j, c, False), init)
        carry = tile(i, carry, True)
        o = jnp.where(m0, carry[0][2] / carry[0][1], carry[1][2] / carry[1][1])
        o_ref[...] = o
        gt = g_ref[...]
        og_ref[...] = (o * (gt * _sigmoid(gt))).astype(BF16)
        for h in range(2):
            st_ref[h] = carry[h][0] + jnp.log(carry[h][1])

    in_specs = list(_att_specs(kind, S, T))
    args = list(qkvg)
    if fox:
        in_specs += [pl.BlockSpec((2, T, 1), lambda p, i: (p, i, 0)), pl.BlockSpec((2, nq, T), lambda p, i: (p, 0, 0))]
        args += [c_col, c_row]
    W = npairs * LANES
    return pl.pallas_call(
        body, name=name, grid=(npairs, nq), in_specs=in_specs,
        out_specs=[pl.BlockSpec((T, LANES), lambda p, i: (i, p)), pl.BlockSpec((T, LANES), lambda p, i: (i, p)),
                   pl.BlockSpec((2, T, 1), lambda p, i: (p, i, 0))],
        out_shape=[jax.ShapeDtypeStruct((S, W), F32), jax.ShapeDtypeStruct((S, W), BF16),
                   jax.ShapeDtypeStruct((2 * npairs, S, 1), F32)],
        compiler_params=_cparams(("parallel", "parallel")),
    )(*args)


def _softmax_bwd(kind, qkv, do, do_off, o, lse, c_col, c_row, S, npairs, name):
    T = ATT_T
    nq = S // T
    fox = kind == "fox"
    mla = kind == "mla"
    scale = (96 if mla else 64) ** -0.5
    kw = 256 if mla else LANES

    def body(*refs):
        if fox:
            q_ref, k_ref, v_ref, do_ref, o_ref, st_ref, cc_ref, cr_ref, dq_ref, dk_ref, dv_ref, dc_ref, dcq_ref = refs
        else:
            q_ref, k_ref, v_ref, do_ref, o_ref, st_ref, dq_ref, dk_ref, dv_ref = refs
        i = pl.program_id(1)

        @pl.when(i == 0)
        def _():
            dk_ref[...] = jnp.zeros_like(dk_ref)
            dv_ref[...] = jnp.zeros_like(dv_ref)
            if fox:
                dc_ref[...] = jnp.zeros_like(dc_ref)

        m0 = lax.broadcasted_iota(jnp.int32, (1, LANES), 1) < 64
        causal = lax.broadcasted_iota(jnp.int32, (T, T), 1) <= lax.broadcasted_iota(jnp.int32, (T, T), 0)
        qh = _head_q(kind, q_ref, m0, scale)
        dov = do_ref[...]
        prod = dov * o_ref[...]
        dd = [jnp.sum(jnp.where(m0, prod, 0.0), axis=1, keepdims=True),
              jnp.sum(jnp.where(m0, 0.0, prod), axis=1, keepdims=True)]
        doh = [jnp.where(m0, dov, 0.0).astype(BF16), jnp.where(m0, 0.0, dov).astype(BF16)]
        lse_h = [st_ref[0], st_ref[1]]

        def tile(j, carry, masked):
            start = pl.multiple_of(j * T, T)
            vb = v_ref[pl.ds(start, T), :].astype(BF16)
            kh = _head_k(kind, k_ref, start, T)
            dqs = []
            dkc = []
            dvc = jnp.zeros((T, LANES), F32)
            for h in range(2):
                z = _dot_nt(qh[h], kh[h])
                if mla:
                    z = z * scale
                if fox:
                    z = z + (cc_ref[h] - cr_ref[h, pl.ds(j, 1), :])
                if masked:
                    z = jnp.where(causal, z, NEG)
                p = jnp.exp(z - lse_h[h])
                ds = p * (_dot_nt(doh[h], vb) - dd[h])
                dsb = ds.astype(BF16)
                dqh = carry[h][0] + _dot(dsb, kh[h])
                dkc.append(_dot_tn(dsb, qh[h]))
                dvc = dvc + _dot_tn(p.astype(BF16), doh[h])
                if fox:
                    dc_ref[h, pl.ds(j, 1), :] += -jnp.sum(ds, axis=0, keepdims=True)
                    dqs.append((dqh, carry[h][1] + jnp.sum(ds, axis=1, keepdims=True)))
                else:
                    dqs.append((dqh,))
            if mla:
                dk_ref[pl.ds(start, T), 0:LANES] += dkc[0] * scale
                dk_ref[pl.ds(start, T), LANES:2 * LANES] += dkc[1] * scale
            else:
                dk_ref[pl.ds(start, T), :] += dkc[0] + dkc[1]
            dv_ref[pl.ds(start, T), :] += dvc
            return tuple(dqs)

        one = (jnp.zeros((T, LANES), F32), jnp.zeros((T, 1), F32)) if fox else (jnp.zeros((T, LANES), F32),)
        carry = lax.fori_loop(0, i, lambda j, c: tile(j, c, False), (one, one))
        carry = tile(i, carry, True)
        if mla:
            dq_ref[:, 0:LANES] = carry[0][0] * scale
            dq_ref[:, LANES:2 * LANES] = carry[1][0] * scale
        else:
            dq_ref[...] = jnp.where(m0, carry[0][0], carry[1][0]) * scale
        if fox:
            dcq_ref[0] = carry[0][1]
            dcq_ref[1] = carry[1][1]

    qs, ks, vs, _ = _att_specs(kind, S, T)
    in_specs = [qs, ks, vs,
                pl.BlockSpec((T, LANES), lambda p, i: (i, do_off + p)),
                pl.BlockSpec((T, LANES), lambda p, i: (i, p)),
                pl.BlockSpec((2, T, 1), lambda p, i: (p, i, 0))]
    args = list(qkv) + [do, o, lse]
    W = npairs * LANES
    out_specs = [pl.BlockSpec((T, kw), lambda p, i: (i, p)), pl.BlockSpec((S, kw), lambda p, i: (0, p)),
                 pl.BlockSpec((S, LANES), lambda p, i: (0, p))]
    out_shape = [jax.ShapeDtypeStruct((S, npairs * kw), F32), jax.ShapeDtypeStruct((S, npairs * kw), F32),
                 jax.ShapeDtypeStruct((S, W), F32)]
    if fox:
        in_specs += [pl.BlockSpec((2, T, 1), lambda p, i: (p, i, 0)), pl.BlockSpec((2, nq, T), lambda p, i: (p, 0, 0))]
        args += [c_col, c_row]
        out_specs += [pl.BlockSpec((2, nq, T), lambda p, i: (p, 0, 0)), pl.BlockSpec((2, T, 1), lambda p, i: (p, i, 0))]
        out_shape += [jax.ShapeDtypeStruct((2 * npairs, nq, T), F32), jax.ShapeDtypeStruct((2 * npairs, S, 1), F32)]
    return pl.pallas_call(
        body, name=name, grid=(npairs, nq), in_specs=in_specs, out_specs=out_specs, out_shape=out_shape,
        compiler_params=_cparams(("parallel", "arbitrary")),
    )(*args)


def _softplus_parts(z):
    sp = jnp.maximum(z, 0.0) + jnp.log(1.0 + jnp.exp(-jnp.abs(z)))
    return -sp, z - sp


def _split2(x):
    hi = x.astype(BF16)
    return hi, (x - hi.astype(F32)).astype(BF16)


def _sb_fwd(proj, S, npairs, name):
    T = ATT_T
    nq = S // T
    scale = 64 ** -0.5

    def body(q_ref, k_ref, v_ref, g_ref, o_ref, og_ref, st_ref):
        i = pl.program_id(1)
        m0 = lax.broadcasted_iota(jnp.int32, (1, LANES), 1) < 64
        r = lax.broadcasted_iota(jnp.int32, (T, T), 0)
        c = lax.broadcasted_iota(jnp.int32, (T, T), 1)
        before = c < r
        after = (r > c).astype(BF16)
        qh = _head_q("sb", q_ref, m0, scale)

        def tile(j, carry, masked):
            start = pl.multiple_of(j * T, T)
            vb = v_ref[pl.ds(start, T), :].astype(BF16)
            kb = k_ref[pl.ds(start, T), :].astype(BF16)
            out = []
            for h in range(2):
                rem, acc = carry[h]
                z = _dot_nt(qh[h], kb)
                lk, la = _softplus_parts(z)
                if masked:
                    lk = jnp.where(before, lk, 0.0)
                hi, lo = _split2(lk)
                lr = rem + (_dot(hi, after) + _dot(lo, after))
                w = jnp.exp(la + lr)
                if masked:
                    w = jnp.where(before, w, 0.0)
                out.append((rem + jnp.sum(lk, axis=1, keepdims=True), acc + _dot(w.astype(BF16), vb)))
            return tuple(out)

        init = tuple((jnp.zeros((T, 1), F32), jnp.zeros((T, LANES), F32)) for _ in range(2))
        carry = tile(i, init, True)
        carry = lax.fori_loop(0, i, lambda jj, cr: tile(i - 1 - jj, cr, False), carry)
        o = jnp.where(m0, carry[0][1], carry[1][1])
        o_ref[...] = o
        gt = g_ref[...]
        og_ref[...] = (o * (gt * _sigmoid(gt))).astype(BF16)
        for h in range(2):
            st_ref[h] = carry[h][0]

    W = npairs * LANES
    return pl.pallas_call(
        body, name=name, grid=(npairs, nq), in_specs=list(_att_specs("sb", S, T)),
        out_specs=[pl.BlockSpec((T, LANES), lambda p, i: (i, p)), pl.BlockSpec((T, LANES), lambda p, i: (i, p)),
                   pl.BlockSpec((2, T, 1), lambda p, i: (p, i, 0))],
        out_shape=[jax.ShapeDtypeStruct((S, W), F32), jax.ShapeDtypeStruct((S, W), BF16),
                   jax.ShapeDtypeStruct((2 * npairs, S, 1), F32)],
        compiler_params=_cparams(("parallel", "parallel")),
    )(proj, proj, proj, proj)


def _sb_bwd(proj, do, tot, S, npairs, name):
    T = ATT_T
    nq = S // T
    scale = 64 ** -0.5

    def body(q_ref, k_ref, v_ref, do_ref, st_ref, dq_ref, dk_ref, dv_ref):
        i = pl.program_id(1)

        @pl.when(i == 0)
        def _():
            dk_ref[...] = jnp.zeros_like(dk_ref)
            dv_ref[...] = jnp.zeros_like(dv_ref)

        m0 = lax.broadcasted_iota(jnp.int32, (1, LANES), 1) < 64
        r = lax.broadcasted_iota(jnp.int32, (T, T), 0)
        c = lax.broadcasted_iota(jnp.int32, (T, T), 1)
        before = c < r
        upto = (r <= c).astype(BF16)
        left = (r < c).astype(BF16)
        qh = _head_q("sb", q_ref, m0, scale)
        dov = do_ref[...]
        doh = [jnp.where(m0, dov, 0.0).astype(BF16), jnp.where(m0, 0.0, dov).astype(BF16)]
        tot_h = [st_ref[0], st_ref[1]]

        def tile(j, carry, masked):
            start = pl.multiple_of(j * T, T)
            vb = v_ref[pl.ds(start, T), :].astype(BF16)
            kb = k_ref[pl.ds(start, T), :].astype(BF16)
            out = []
            dkc = jnp.zeros((T, LANES), F32)
            dvc = jnp.zeros((T, LANES), F32)
            for h in range(2):
                pre, gpre, dq = carry[h]
                z = _dot_nt(qh[h], kb)
                lk, la = _softplus_parts(z)
                if masked:
                    lk = jnp.where(before, lk, 0.0)
                hi, lo = _split2(lk)
                lr = (tot_h[h] - pre) - (_dot(hi, upto) + _dot(lo, upto))
                w = jnp.exp(la + lr)
                if masked:
                    w = jnp.where(before, w, 0.0)
                g = _dot_nt(doh[h], vb) * w
                gfull = gpre + _dot(g.astype(BF16), left)
                dz = g - (g + gfull) * jnp.exp(la)
                if masked:
                    dz = jnp.where(before, dz, 0.0)
                dzb = dz.astype(BF16)
                dkc = dkc + _dot_tn(dzb, qh[h])
                dvc = dvc + _dot_tn(w.astype(BF16), doh[h])
                out.append((pre + jnp.sum(lk, axis=1, keepdims=True), gpre + jnp.sum(g, axis=1, keepdims=True),
                            dq + _dot(dzb, kb)))
            dk_ref[pl.ds(start, T), :] += dkc
            dv_ref[pl.ds(start, T), :] += dvc
            return tuple(out)

        init = tuple((jnp.zeros((T, 1), F32), jnp.zeros((T, 1), F32), jnp.zeros((T, LANES), F32)) for _ in range(2))
        carry = lax.fori_loop(0, i, lambda j, cr: tile(j, cr, False), init)
        carry = tile(i, carry, True)
        dq_ref[...] = jnp.where(m0, carry[0][2], carry[1][2]) * scale

    qs, ks, vs, _ = _att_specs("sb", S, T)
    W = npairs * LANES
    return pl.pallas_call(
        body, name=name, grid=(npairs, nq),
        in_specs=[qs, ks, vs, pl.BlockSpec((T, LANES), lambda p, i: (i, p)),
                  pl.BlockSpec((2, T, 1), lambda p, i: (p, i, 0))],
        out_specs=[pl.BlockSpec((T, LANES), lambda p, i: (i, p)), pl.BlockSpec((S, LANES), lambda p, i: (0, p)),
                   pl.BlockSpec((S, LANES), lambda p, i: (0, p))],
        out_shape=[jax.ShapeDtypeStruct((S, W), F32)] * 3,
        compiler_params=_cparams(("parallel", "arbitrary")),
    )(proj, proj, proj, do, tot)


def _pad_w0(w):
    z = lambda n: jnp.zeros((w.shape[0], n), w.dtype)
    return jnp.concatenate([w[:, 2048:2432], w[:, 2432:2688], z(64), w[:, 2688:2720], z(32),
                            w[:, 1536:2048], w[:, 2720:3232], w[:, 0:512], w[:, 512:1024], w[:, 1024:1536]], axis=1)


def _unpad_w0(wp):
    return jnp.concatenate([wp[:, L0_SBQ:L0_SBQ + 512], wp[:, L0_SBK:L0_SBK + 512], wp[:, L0_SBV:L0_SBV + 512],
                            wp[:, L0_SBG:L0_SBG + 512], wp[:, 0:384], wp[:, 384:640], wp[:, 704:736],
                            wp[:, L0_MLG:L0_MLG + 512]], axis=1)


def _pad_wq(w):
    return jnp.pad(w.reshape(384, 8, 96), ((0, 0), (0, 0), (0, 32))).reshape(384, 1024)


def _unpad_wq(wp):
    return wp.reshape(384, 8, 128)[:, :, :96].reshape(384, 768)


def _pad_wkv(w):
    w3 = w.reshape(256, 8, 128)
    k = jnp.pad(w3[:, :, :64], ((0, 0), (0, 0), (0, 64))).reshape(256, 1024)
    return jnp.concatenate([k, w3[:, :, 64:].reshape(256, 512)], axis=1)


def _unpad_wkv(wp):
    k = wp[:, :1024].reshape(256, 8, 128)[:, :, :64]
    v = wp[:, 1024:].reshape(256, 8, 64)
    return jnp.concatenate([k, v], axis=-1).reshape(256, 1024)


def _pad_w1(w):
    return jnp.concatenate([w, jnp.zeros((w.shape[0], L1_WIDTH - ODD_IN_WIDTH), w.dtype)], axis=1)


def _local_step(x, positions, target, g, w0p, wqp, wkvp, wo0, w1p, wo1):
    S = x.shape[0]
    nq = S // ATT_T
    invf = ROPE_THETA ** (-jnp.arange(0, MLA_ROPE_DIM, 2, dtype=F32) / MLA_ROPE_DIM)
    invf = jnp.concatenate([jnp.zeros((64,), F32), invf, invf, jnp.zeros((32,), F32)]).reshape(1, LANES)
    cosT, s1T, s2T = _rope_tables(positions.reshape(S, 1), invf, "rope_tables")
    bfp = jnp.pad(g["l1_b_f"], ((0, 0), (0, LANES - FOX_HEADS)))

    proj0, h0 = _norm_matmul(x, g["l0_pre_g"], w0p, "l0_in_proj")
    qm, km, vm, qn, cn = _mla_prep(proj0, g["l0_q_a_g"], g["l0_kv_a_g"], wqp, wkvp, cosT, s1T, s2T, "mla_prep")
    o_sb, og_sb, tot_sb = _sb_fwd(proj0, S, 4, "sb_fwd")
    o_ml, og_ml, lse_ml = _softmax_fwd("mla", (qm, km, vm, proj0), None, None, S, 4, "mla_fwd")
    y0, x1 = _out_proj(og_sb, og_ml, 0, 0, wo0, x, g["l0_post_g"], None, "l0_out_proj")

    proj1, h1 = _norm_matmul(x1, g["l1_pre_g"], w1p, "l1_in_proj")
    cfx = _fox_prep(proj1, bfp, "fox_prep")
    c16 = cfx[:, :FOX_HEADS].T
    c_col = c16.reshape(FOX_HEADS, S, 1)
    c_row = c16.reshape(FOX_HEADS, nq, ATT_T)
    o_fx, og_fx, lse_fx = _softmax_fwd("fox", (proj1, proj1, proj1, proj1), c_col, c_row, S, 8, "fox_fwd")
    y1, dx2, lsum = _out_proj(og_fx, og_fx, 0, 1, wo1, x1, g["l1_post_g"], target, "l1_out_proj")

    dy1, do1, dgate1, d_post1 = _out_proj_bwd(dx2, y1, g["l1_post_g"], wo1, proj1, (L1_G, L1_G + 512), o_fx, o_fx, 0, 1, "l1_out_bwd")
    dwo1 = _matmul_tn(og_fx, dy1, "l1_dw_out")
    dq1, dk1, dv1, dc_row, dc_col = _softmax_bwd("fox", (proj1, proj1, proj1), do1, 0, o_fx, lse_fx, c_col, c_row, S, 8,
                                                 "fox_bwd")
    dc = jnp.pad((dc_row.reshape(FOX_HEADS, S) + dc_col.reshape(FOX_HEADS, S)).T, ((0, 0), (0, LANES - FOX_HEADS)))
    df, d_bf = _fox_prep_bwd(dc, proj1, bfp, "fox_prep_bwd")
    dproj1 = jnp.concatenate([dq1, dk1, dv1, dgate1, df], axis=1)
    dx1, d_pre1 = _in_proj_bwd(dproj1, w1p, x1, g["l1_pre_g"], dx2, "l1_in_bwd")
    dw1p = _matmul_tn(h1, dproj1, "l1_dw_in")

    dy0, do0, dgate0, d_post0 = _out_proj_bwd(dx1, y0, g["l0_post_g"], wo0, proj0, (L0_SBG, L0_MLG), o_sb, o_ml, 0, 0,
                                              "l0_out_bwd")
    og0 = jnp.concatenate([og_sb, og_ml], axis=1)
    dwo0 = _matmul_tn(og0, dy0, "l0_dw_out")
    dsq, dsk, dsv = _sb_bwd(proj0, do0, tot_sb, S, 4, "sb_bwd")
    dqm, dkm, dvm = _softmax_bwd("mla", (qm, km, vm), do0, 4, o_ml, lse_ml, None, None, S, 4, "mla_bwd")
    dprep, dqb, dkvb, d_qag, d_kvag = _mla_prep_bwd(dqm, dkm, dvm, proj0, g["l0_q_a_g"], g["l0_kv_a_g"], wqp, wkvp,
                                                    cosT, s1T, s2T, "mla_prep_bwd")
    dwqp = _matmul_tn(qn, dqb, "l0_dw_qb")
    dwkvp = _matmul_tn(cn, dkvb, "l0_dw_kvb")
    dproj0 = jnp.concatenate([dprep, dgate0[:, :512], dgate0[:, 512:], dsq, dsk, dsv], axis=1)
    dx0, d_pre0 = _in_proj_bwd(dproj0, w0p, x, g["l0_pre_g"], dx1, "l0_in_bwd")
    dw0p = _matmul_tn(h0, dproj0, "l0_dw_in")

    grads = {
        "l0_pre_g": d_pre0, "l0_post_g": d_post0, "l0_w_in": dw0p, "l0_q_a_g": d_qag, "l0_w_q_b": dwqp,
        "l0_kv_a_g": d_kvag, "l0_w_kv_b": dwkvp, "l0_w_out": dwo0, "l1_pre_g": d_pre1, "l1_post_g": d_post1,
        "l1_w_in": dw1p, "l1_b_f": d_bf[:, :FOX_HEADS], "l1_w_out": dwo1,
    }
    return lsum, dx0, grads


_ANY = pl.BlockSpec(memory_space=pl.ANY)


def _place():
    return lax.axis_index("x"), lax.axis_index("y"), lax.axis_index("c")


def _other_chips(x, y):
    return [(1 - x, y), (x, 1 - y), (1 - x, 1 - y)]


def _half(c):
    return pl.ds(c * PACK_HALF, PACK_HALF)


def _weight_gather(pack):
    def body(p_ref, out_ref, send_sems, recv_sems, local_sem):
        x, y, c = _place()
        sibling = (x, y, 1 - c)
        chips = _other_chips(x, y)

        def blk(chip, cc):
            return out_ref.at[2 * chip[0] + chip[1], _half(cc)]

        def copy(k, src, dst, to):
            return pltpu.make_async_remote_copy(src_ref=src, dst_ref=dst, send_sem=send_sems.at[k],
                                                recv_sem=recv_sems.at[k], device_id=to, device_id_type=MESH)

        mine = pltpu.make_async_copy(p_ref, out_ref.at[2 * x + y], local_sem)
        mine.start()
        first = [copy(j, p_ref.at[_half(c)], blk((x, y), c), (*chip, c)) for j, chip in enumerate(chips)]
        for cp in first:
            cp.start()
        passed = [copy(3 + j, blk(chip, c), blk(chip, c), sibling) for j, chip in enumerate(chips)]
        for j, chip in enumerate(chips):
            copy(j, blk(chip, c), blk(chip, c), (x, y, c)).wait_recv()
            passed[j].start()
        for j, chip in enumerate(chips):
            copy(3 + j, blk(chip, 1 - c), blk(chip, 1 - c), (x, y, c)).wait_recv()
        for cp in first + passed:
            cp.wait_send()
        mine.wait()

    return pl.pallas_call(
        body, name="weight_gather", in_specs=[_ANY], out_specs=_ANY,
        out_shape=jax.ShapeDtypeStruct((4,) + pack.shape, pack.dtype),
        scratch_shapes=[pltpu.SemaphoreType.DMA((6,)), pltpu.SemaphoreType.DMA((6,)), pltpu.SemaphoreType.DMA],
    )(pack)


def _grad_core_exchange(p):
    def body(p_ref, mine_ref, recv_ref, send_sems, recv_sems, local_sems):
        x, y, c = _place()
        sibling = (x, y, 1 - c)
        keep = [pltpu.make_async_copy(p_ref.at[j, _half(c)], mine_ref.at[j], local_sems.at[j]) for j in range(4)]
        give = [pltpu.make_async_remote_copy(src_ref=p_ref.at[j, _half(1 - c)], dst_ref=recv_ref.at[j],
                                             send_sem=send_sems.at[j], recv_sem=recv_sems.at[j],
                                             device_id=sibling, device_id_type=MESH) for j in range(4)]
        for cp in give + keep:
            cp.start()
        for cp in give + keep:
            cp.wait()

    shp = jax.ShapeDtypeStruct((4, PACK_HALF, LANES), p.dtype)
    return pl.pallas_call(
        body, name="grad_core_exchange", in_specs=[_ANY], out_specs=[_ANY, _ANY], out_shape=[shp, shp],
        scratch_shapes=[pltpu.SemaphoreType.DMA((4,)), pltpu.SemaphoreType.DMA((4,)), pltpu.SemaphoreType.DMA((4,))],
    )(p)


def _grad_chip_exchange(q):
    def body(q_ref, out_ref, send_sems, recv_sems, local_sem):
        x, y, c = _place()
        me = 2 * x + y
        chips = _other_chips(x, y)
        mine = pltpu.make_async_copy(q_ref.at[me], out_ref.at[me], local_sem)
        mine.start()
        sends = [pltpu.make_async_remote_copy(src_ref=q_ref.at[2 * chip[0] + chip[1]], dst_ref=out_ref.at[me],
                                              send_sem=send_sems.at[j], recv_sem=recv_sems.at[j],
                                              device_id=(*chip, c), device_id_type=MESH) for j, chip in enumerate(chips)]
        for cp in sends:
            cp.start()
        for j, chip in enumerate(chips):
            slot = out_ref.at[2 * chip[0] + chip[1]]
            pltpu.make_async_remote_copy(src_ref=slot, dst_ref=slot, send_sem=send_sems.at[j], recv_sem=recv_sems.at[j],
                                         device_id=(x, y, c), device_id_type=MESH).wait_recv()
        for cp in sends:
            cp.wait_send()
        mine.wait()

    return pl.pallas_call(
        body, name="grad_chip_exchange", in_specs=[_ANY], out_specs=_ANY,
        out_shape=jax.ShapeDtypeStruct(q.shape, q.dtype),
        scratch_shapes=[pltpu.SemaphoreType.DMA((3,)), pltpu.SemaphoreType.DMA((3,)), pltpu.SemaphoreType.DMA],
    )(q)


def _grad_core_gather(t):
    def body(t_ref, out_ref, send_sem, recv_sem, local_sem):
        x, y, c = _place()
        mine = pltpu.make_async_copy(t_ref, out_ref.at[c], local_sem)
        mine.start()
        give = pltpu.make_async_remote_copy(src_ref=t_ref, dst_ref=out_ref.at[c], send_sem=send_sem, recv_sem=recv_sem,
                                            device_id=(x, y, 1 - c), device_id_type=MESH)
        give.start()
        pltpu.make_async_remote_copy(src_ref=t_ref, dst_ref=out_ref.at[1 - c], send_sem=send_sem, recv_sem=recv_sem,
                                     device_id=(x, y, c), device_id_type=MESH).wait_recv()
        give.wait_send()
        mine.wait()

    return pl.pallas_call(
        body, name="grad_core_gather", in_specs=[_ANY], out_specs=_ANY,
        out_shape=jax.ShapeDtypeStruct((2,) + t.shape, t.dtype),
        scratch_shapes=[pltpu.SemaphoreType.DMA, pltpu.SemaphoreType.DMA, pltpu.SemaphoreType.DMA],
    )(t)


def _add2(a, b, name):
    tr = 2048

    def body(a_ref, b_ref, o_ref):
        o_ref[...] = a_ref[...] + b_ref[...]

    spec = pl.BlockSpec((None, tr, LANES), lambda j, r: (j, r, 0))
    return pl.pallas_call(
        body, name=name, grid=(4, PACK_HALF // tr), in_specs=[spec, spec], out_specs=spec,
        out_shape=jax.ShapeDtypeStruct(a.shape, a.dtype), compiler_params=_cparams(("parallel", "parallel")),
    )(a, b)


def _sum4(s, name):
    tr = 2048

    def body(s0, s1, s2, s3, o_ref):
        o_ref[...] = ((s0[...] + s1[...]) + s2[...]) + s3[...]

    specs = [pl.BlockSpec((None, tr, LANES), functools.partial(lambda r, j: (j, r, 0), j=j)) for j in range(4)]
    return pl.pallas_call(
        body, name=name, grid=(PACK_HALF // tr,), in_specs=specs, out_specs=pl.BlockSpec((tr, LANES), lambda r: (r, 0)),
        out_shape=jax.ShapeDtypeStruct(s.shape[1:], s.dtype), compiler_params=_cparams(("parallel",)),
    )(s, s, s, s)


def _small_allreduce(sp):
    def body(sp_ref, out_ref, gath_ref, send_sems, recv_sems):
        x, y, c = _place()
        me = 4 * x + 2 * y + c
        gath_ref[me] = sp_ref[...]
        peers = []
        for k in range(1, 8):
            px = 1 - x if k & 4 else x
            py = 1 - y if k & 2 else y
            pc = 1 - c if k & 1 else c
            peers.append((px, py, pc))
        sends = [pltpu.make_async_remote_copy(src_ref=sp_ref, dst_ref=gath_ref.at[me], send_sem=send_sems.at[k],
                                              recv_sem=recv_sems.at[k], device_id=peer, device_id_type=MESH)
                 for k, peer in enumerate(peers)]
        for cp in sends:
            cp.start()
        for k, (px, py, pc) in enumerate(peers):
            slot = gath_ref.at[4 * px + 2 * py + pc]
            pltpu.make_async_remote_copy(src_ref=slot, dst_ref=slot, send_sem=send_sems.at[k], recv_sem=recv_sems.at[k],
                                         device_id=(x, y, c), device_id_type=MESH).wait_recv()
        for cp in sends:
            cp.wait_send()
        tot = gath_ref[0]
        for d in range(1, 8):
            tot = tot + gath_ref[d]
        out_ref[...] = tot

    vm = pl.BlockSpec(memory_space=pltpu.VMEM)
    return pl.pallas_call(
        body, name="small_allreduce", in_specs=[vm], out_specs=vm, out_shape=jax.ShapeDtypeStruct(sp.shape, sp.dtype),
        scratch_shapes=[pltpu.VMEM((8,) + sp.shape, sp.dtype), pltpu.SemaphoreType.DMA((7,)), pltpu.SemaphoreType.DMA((7,))],
    )(sp)


def _adamw(w, g, m, v, name):
    rows = w.shape[0]
    tr = _pick(rows, (2048, rows))

    def body(w_ref, g_ref, m_ref, v_ref, d_ref, mo_ref, vo_ref):
        gv = g_ref[...]
        mn = ADAM_B1 * m_ref[...] + (1.0 - ADAM_B1) * gv
        vn = ADAM_B2 * v_ref[...] + (1.0 - ADAM_B2) * (gv * gv)
        m_hat = mn / (1.0 - ADAM_B1 ** ADAM_STEP)
        v_hat = vn / (1.0 - ADAM_B2 ** ADAM_STEP)
        d_ref[...] = -ADAM_LR * (m_hat / (jnp.sqrt(v_hat) + ADAM_EPS) + ADAM_WD * w_ref[...])
        mo_ref[...] = mn
        vo_ref[...] = vn

    spec = pl.BlockSpec((tr, LANES), lambda r: (r, 0))
    shp = jax.ShapeDtypeStruct(w.shape, F32)
    return pl.pallas_call(
        body, name=name, grid=(rows // tr,), in_specs=[spec] * 4, out_specs=[spec] * 3, out_shape=[shp] * 3,
        compiler_params=_cparams(("parallel",)),
    )(w, g, m, v)


MAT_NAMES = ("l0_w_in", "l0_w_q_b", "l0_w_kv_b", "l0_w_out", "l1_w_in", "l1_w_out")
VEC_NAMES = ("l0_pre_g", "l0_post_g", "l0_q_a_g", "l0_kv_a_g", "l1_pre_g", "l1_post_g", "l1_b_f")
WEIGHT_NAMES = ("l0_pre_g", "l0_post_g", "l0_w_in", "l0_q_a_g", "l0_w_q_b", "l0_kv_a_g", "l0_w_kv_b", "l0_w_out",
                "l1_pre_g", "l1_post_g", "l1_w_in", "l1_b_f", "l1_w_out")
MAT_SHARD = {"l0_w_in": (1024, 808), "l0_w_q_b": (384, 192), "l0_w_kv_b": (256, 256), "l0_w_out": (256, 1024),
             "l1_w_in": (1024, 1028), "l1_w_out": (256, 1024)}
ROW_SHARDED = ("l0_w_out", "l1_w_out")
VEC_LEN = {"l0_pre_g": 1024, "l0_post_g": 1024, "l0_q_a_g": 384, "l0_kv_a_g": 256, "l1_pre_g": 1024,
           "l1_post_g": 1024, "l1_b_f": 16}


def _mat_rows(n):
    r, c = MAT_SHARD[n]
    return r * c // LANES


def _pack_shards(shards):
    parts = [shards[n].reshape(_mat_rows(n), LANES) for n in MAT_NAMES]
    used = sum(_mat_rows(n) for n in MAT_NAMES)
    parts.append(jnp.zeros((PACK_ROWS - used, LANES), parts[0].dtype))
    return jnp.concatenate(parts, axis=0)


def _unpack_shards(pack):
    out, at = {}, 0
    for n in MAT_NAMES:
        out[n] = pack[..., at:at + _mat_rows(n), :].reshape(pack.shape[:-2] + MAT_SHARD[n])
        at += _mat_rows(n)
    return out


def _join_shards(n, s):
    if n in ROW_SHARDED:
        return s.reshape(4 * s.shape[1], s.shape[2])
    return s.transpose(1, 0, 2).reshape(s.shape[1], 4 * s.shape[2])


def _cut_shards(n, w):
    r, c = MAT_SHARD[n]
    if n in ROW_SHARDED:
        return w.reshape(4, r, c)
    return w.reshape(r, 4, c).transpose(1, 0, 2)


def _pack_vecs(vecs):
    parts = []
    for n in VEC_NAMES:
        v = vecs[n].reshape(-1)
        pad = (-v.shape[0]) % LANES
        parts.append(jnp.pad(v, (0, pad)).reshape(-1, LANES))
    used = sum(p.shape[0] for p in parts)
    parts.append(jnp.zeros((SMALL_ROWS - used, LANES), F32))
    return jnp.concatenate(parts, axis=0)


def _unpack_vecs(pack):
    out, at = {}, 0
    for n in VEC_NAMES:
        rows = -(-VEC_LEN[n] // LANES)
        out[n] = pack[at:at + rows].reshape(-1)[:VEC_LEN[n]]
        at += rows
    return out


def kernel(x, positions, l0_pre_g, l0_post_g, l0_w_in, l0_q_a_g, l0_w_q_b, l0_kv_a_g, l0_w_kv_b, l0_w_out, l1_pre_g, l1_post_g, l1_w_in, l1_b_f, l1_w_out, loss_target, m_l0_pre_g, m_l0_post_g, m_l0_w_in, m_l0_q_a_g, m_l0_w_q_b, m_l0_kv_a_g, m_l0_w_kv_b, m_l0_w_out, m_l1_pre_g, m_l1_post_g, m_l1_w_in, m_l1_b_f, m_l1_w_out, v_l0_pre_g, v_l0_post_g, v_l0_w_in, v_l0_q_a_g, v_l0_w_q_b, v_l0_kv_a_g, v_l0_w_kv_b, v_l0_w_out, v_l1_pre_g, v_l1_post_g, v_l1_w_in, v_l1_b_f, v_l1_w_out):
    w = dict(l0_pre_g=l0_pre_g, l0_post_g=l0_post_g, l0_w_in=l0_w_in, l0_q_a_g=l0_q_a_g, l0_w_q_b=l0_w_q_b,
             l0_kv_a_g=l0_kv_a_g, l0_w_kv_b=l0_w_kv_b, l0_w_out=l0_w_out, l1_pre_g=l1_pre_g, l1_post_g=l1_post_g,
             l1_w_in=l1_w_in, l1_b_f=l1_b_f, l1_w_out=l1_w_out)
    m = dict(l0_pre_g=m_l0_pre_g, l0_post_g=m_l0_post_g, l0_w_in=m_l0_w_in, l0_q_a_g=m_l0_q_a_g, l0_w_q_b=m_l0_w_q_b,
             l0_kv_a_g=m_l0_kv_a_g, l0_w_kv_b=m_l0_w_kv_b, l0_w_out=m_l0_w_out, l1_pre_g=m_l1_pre_g,
             l1_post_g=m_l1_post_g, l1_w_in=m_l1_w_in, l1_b_f=m_l1_b_f, l1_w_out=m_l1_w_out)
    v = dict(l0_pre_g=v_l0_pre_g, l0_post_g=v_l0_post_g, l0_w_in=v_l0_w_in, l0_q_a_g=v_l0_q_a_g, l0_w_q_b=v_l0_w_q_b,
             l0_kv_a_g=v_l0_kv_a_g, l0_w_kv_b=v_l0_w_kv_b, l0_w_out=v_l0_w_out, l1_pre_g=v_l1_pre_g,
             l1_post_g=v_l1_post_g, l1_w_in=v_l1_w_in, l1_b_f=v_l1_b_f, l1_w_out=v_l1_w_out)

    w_pack = _pack_shards(w)
    gathered = _unpack_shards(_weight_gather(w_pack.astype(BF16)))
    full = {n: _join_shards(n, gathered[n]) for n in MAT_NAMES}
    gains = {n: w[n].reshape(1, -1) for n in VEC_NAMES}

    lsum, dx0, grads = _local_step(
        x[0], positions[0], loss_target[0], gains, _pad_w0(full["l0_w_in"]), _pad_wq(full["l0_w_q_b"]),
        _pad_wkv(full["l0_w_kv_b"]), full["l0_w_out"], _pad_w1(full["l1_w_in"]), full["l1_w_out"])
    loss = lax.psum(0.5 * jnp.sum(lsum) / float(D_MODEL), ("x", "y", "c"))

    gfull = {"l0_w_in": _unpad_w0(grads["l0_w_in"]), "l0_w_q_b": _unpad_wq(grads["l0_w_q_b"]),
             "l0_w_kv_b": _unpad_wkv(grads["l0_w_kv_b"]), "l0_w_out": grads["l0_w_out"],
             "l1_w_in": grads["l1_w_in"][:, :ODD_IN_WIDTH], "l1_w_out": grads["l1_w_out"]}
    parts = [_cut_shards(n, gfull[n]).reshape(4, _mat_rows(n), LANES) for n in MAT_NAMES]
    used = sum(_mat_rows(n) for n in MAT_NAMES)
    parts.append(jnp.zeros((4, PACK_ROWS - used, LANES), F32))
    g_pack = jnp.concatenate(parts, axis=1)
    mine, theirs = _grad_core_exchange(g_pack)
    slots = _grad_chip_exchange(_add2(mine, theirs, "grad_add_cores"))
    g_shard = _grad_core_gather(_sum4(slots, "grad_add_chips")).reshape(PACK_ROWS, LANES)

    g_small = _small_allreduce(_pack_vecs({n: grads[n] for n in VEC_NAMES}))

    d_pack, m_pack, v_pack = _adamw(w_pack, g_shard, _pack_shards(m), _pack_shards(v), "adamw_mats")
    d_small, m_small, v_small = _adamw(_pack_vecs(w), g_small, _pack_vecs(m), _pack_vecs(v), "adamw_vecs")

    def unpack(mat_pack, vec_pack):
        out = dict(_unpack_shards(mat_pack))
        out.update(_unpack_vecs(vec_pack))
        return [out[n] for n in WEIGHT_NAMES]

    return (loss, dx0[None], *unpack(g_shard, g_small), *unpack(d_pack, d_small), *unpack(m_pack, m_small),
            *unpack(v_pack, v_small))
```

```python
import functools

import numpy as np
import jax
import jax.numpy as jnp
from jax import lax
from jax.experimental import pallas as pl
from jax.experimental.pallas import tpu as pltpu

F32 = jnp.float32
BF16 = jnp.bfloat16
MESH = pl.DeviceIdType.MESH

D_MODEL = 1024
RMS_EPS = 1e-6
ROPE_THETA = 10000.0
SB_WIDTH = 512
MLA_Q_LORA = 384
MLA_KV_LORA = 256
MLA_ROPE_DIM = 32
MLA_WIDTH = 512
FOX_WIDTH = 1024
FOX_HEADS = 16
EVEN_IN_WIDTH = 3232
ODD_IN_WIDTH = 4112

ADAM_LR = 0.001
ADAM_B1 = 0.9
ADAM_B2 = 0.999
ADAM_EPS = 1e-08
ADAM_WD = 0.01
ADAM_STEP = 10

LANES = 128
VMEM_LIMIT = 56 * 1024 * 1024

L0_PREP = 0
L0_PREP_W = 768
L0_SBG = 768
L0_MLG = 1280
L0_SBQ = 1792
L0_SBK = 2304
L0_SBV = 2816
L0_WIDTH = 3328
L1_Q = 0
L1_K = 1024
L1_V = 2048
L1_G = 3072
L1_F = 4096
L1_WIDTH = 4224

ATT_T = 256
ATT_RS = 64
NEG = -1e30

PACK_ROWS = 20480
PACK_HALF = PACK_ROWS // 2
VEC_ROWS = 8
SMALL_ROWS = 7 * VEC_ROWS


def _cparams(sem, **kw):
    return pltpu.CompilerParams(dimension_semantics=sem, vmem_limit_bytes=VMEM_LIMIT, **kw)


def _dot(a, b):
    return lax.dot_general(a, b, (((1,), (0,)), ((), ())), preferred_element_type=F32)


def _dot_nt(a, b):
    return lax.dot_general(a, b, (((1,), (1,)), ((), ())), preferred_element_type=F32)


def _dot_tn(a, b):
    return lax.dot_general(a, b, (((0,), (0,)), ((), ())), preferred_element_type=F32)


def _sigmoid(x):
    return 1.0 / (1.0 + jnp.exp(-x))


def _rstd(x):
    return lax.rsqrt(jnp.mean(x * x, axis=-1, keepdims=True) + RMS_EPS)


def _norm_bwd(x, g, dy):
    r = _rstd(x)
    xn = x * r
    dxn = dy * g
    dx = r * (dxn - xn * jnp.mean(dxn * xn, axis=-1, keepdims=True))
    return dx, dy * xn


def _split3(x):
    hi = x.astype(BF16)
    r1 = x - hi.astype(F32)
    mid = r1.astype(BF16)
    lo = (r1 - mid.astype(F32)).astype(BF16)
    return hi, mid, lo


def _pick(n, cands):
    for c in cands:
        if n % c == 0:
            return c
    raise ValueError(n)


def _norm_matmul(x, g, w, name):
    S, K = x.shape
    N = w.shape[1]
    tm = _pick(S, (512, 256))
    tn = _pick(N, (512, 384, 256, 128))

    def body(x_ref, g_ref, w_ref, o_ref, h_ref):
        @pl.when(pl.program_id(1) == 0)
        def _():
            xv = x_ref[...]
            h_ref[...] = ((xv * _rstd(xv)) * g_ref[...]).astype(BF16)
        o_ref[...] = _dot(h_ref[...], w_ref[...])

    return pl.pallas_call(
        body, name=name, grid=(S // tm, N // tn),
        in_specs=[pl.BlockSpec((tm, K), lambda i, j: (i, 0)),
                  pl.BlockSpec((1, K), lambda i, j: (0, 0)),
                  pl.BlockSpec((K, tn), lambda i, j: (0, j))],
        out_specs=[pl.BlockSpec((tm, tn), lambda i, j: (i, j)),
                   pl.BlockSpec((tm, K), lambda i, j: (i, 0))],
        out_shape=[jax.ShapeDtypeStruct((S, N), F32), jax.ShapeDtypeStruct((S, K), BF16)],
        compiler_params=_cparams(("parallel", "arbitrary")),
    )(x, g, w)


def _matmul_tn(a, b, name):
    S, M = a.shape
    N = b.shape[1]
    tm = _pick(M, (512, 384, 256))
    tn = _pick(N, (512, 384, 256, 128))
    ts = _pick(S, (512, 256))

    def body(a_ref, b_ref, o_ref):
        @pl.when(pl.program_id(2) == 0)
        def _():
            o_ref[...] = jnp.zeros_like(o_ref)
        o_ref[...] += _dot_tn(a_ref[...].astype(BF16), b_ref[...].astype(BF16))

    return pl.pallas_call(
        body, name=name, grid=(M // tm, N // tn, S // ts),
        in_specs=[pl.BlockSpec((ts, tm), lambda i, j, k: (k, i)),
                  pl.BlockSpec((ts, tn), lambda i, j, k: (k, j))],
        out_specs=pl.BlockSpec((tm, tn), lambda i, j, k: (i, j)),
        out_shape=jax.ShapeDtypeStruct((M, N), F32),
        compiler_params=_cparams(("parallel", "parallel", "arbitrary")),
    )(a, b)


def _in_proj_bwd(dproj, w, x, g, dx_up, name):
    S, N = dproj.shape
    K = w.shape[0]
    tm = _pick(S, (512, 256))
    tk = N // 2 if (N // 2) % LANES == 0 and N % 2 == 0 else N // 3
    nk = N // tk

    def body(d_ref, w_ref, x_ref, g_ref, u_ref, dx_ref, dg_ref, acc_ref):
        i, k = pl.program_id(0), pl.program_id(1)

        @pl.when(k == 0)
        def _():
            acc_ref[...] = jnp.zeros_like(acc_ref)

        @pl.when((i == 0) & (k == 0))
        def _():
            dg_ref[...] = jnp.zeros_like(dg_ref)

        acc_ref[...] += _dot_nt(d_ref[...].astype(BF16), w_ref[...])

        @pl.when(k == nk - 1)
        def _():
            dx, dgrow = _norm_bwd(x_ref[...], g_ref[...], acc_ref[...])
            dx_ref[...] = u_ref[...] + dx
            dg_ref[...] += jnp.sum(dgrow, axis=0, keepdims=True)

    return pl.pallas_call(
        body, name=name, grid=(S // tm, nk),
        in_specs=[pl.BlockSpec((tm, tk), lambda i, k: (i, k)),
                  pl.BlockSpec((K, tk), lambda i, k: (0, k)),
                  pl.BlockSpec((tm, K), lambda i, k: (i, 0)),
                  pl.BlockSpec((1, K), lambda i, k: (0, 0)),
                  pl.BlockSpec((tm, K), lambda i, k: (i, 0))],
        out_specs=[pl.BlockSpec((tm, K), lambda i, k: (i, 0)),
                   pl.BlockSpec((1, K), lambda i, k: (0, 0))],
        out_shape=[jax.ShapeDtypeStruct((S, K), F32), jax.ShapeDtypeStruct((1, K), F32)],
        scratch_shapes=[pltpu.VMEM((tm, K), F32)],
        compiler_params=_cparams(("arbitrary", "arbitrary")),
    )(dproj, w, x, g, dx_up)


def _out_proj(og_a, og_b, blk_a, blk_b, w, x, g, target, name):
    S = x.shape[0]
    D = x.shape[1]
    tm = _pick(S, (512, 256))
    with_loss = target is not None

    def body(*refs):
        if with_loss:
            a_ref, b_ref, wa_ref, wb_ref, x_ref, g_ref, t_ref, y_ref, o_ref, l_ref = refs
        else:
            a_ref, b_ref, wa_ref, wb_ref, x_ref, g_ref, y_ref, o_ref = refs
        y = _dot(a_ref[...], wa_ref[...]) + _dot(b_ref[...], wb_ref[...])
        y_ref[...] = y
        xn = x_ref[...] + (y * _rstd(y)) * g_ref[...]
        if with_loss:
            @pl.when(pl.program_id(0) == 0)
            def _():
                l_ref[...] = jnp.zeros_like(l_ref)
            d = xn - t_ref[...]
            o_ref[...] = d / float(D)
            l_ref[...] += jnp.sum(d * d, axis=0, keepdims=True)
        else:
            o_ref[...] = xn

    row = lambda i: (i, 0)
    in_specs = [pl.BlockSpec((tm, 512), lambda i: (i, blk_a)),
                pl.BlockSpec((tm, 512), lambda i: (i, blk_b)),
                pl.BlockSpec((512, D), lambda i: (0, 0)),
                pl.BlockSpec((512, D), lambda i: (1, 0)),
                pl.BlockSpec((tm, D), row),
                pl.BlockSpec((1, D), lambda i: (0, 0))]
    out_specs = [pl.BlockSpec((tm, D), row), pl.BlockSpec((tm, D), row)]
    out_shape = [jax.ShapeDtypeStruct((S, D), F32), jax.ShapeDtypeStruct((S, D), F32)]
    args = [og_a, og_b, w, w, x, g]
    if with_loss:
        in_specs.append(pl.BlockSpec((tm, D), row))
        out_specs.append(pl.BlockSpec((1, D), lambda i: (0, 0)))
        out_shape.append(jax.ShapeDtypeStruct((1, D), F32))
        args.append(target)
    return pl.pallas_call(
        body, name=name, grid=(S // tm,), in_specs=in_specs, out_specs=out_specs, out_shape=out_shape,
        compiler_params=_cparams(("arbitrary",)),
    )(*args)


def _out_proj_bwd(dx_up, y, g, w, proj, gate_offs, o_a, o_b, oblk_a, oblk_b, name):
    S, D = y.shape
    tm = _pick(S, (256,))
    gblk = [off // 256 + c for off in gate_offs for c in range(2)]

    def body(u_ref, y_ref, g_ref, w_ref, g0, g1, g2, g3, oa_ref, ob_ref, dy_ref, do_ref, dgate_ref, dg_ref):
        @pl.when(pl.program_id(0) == 0)
        def _():
            dg_ref[...] = jnp.zeros_like(dg_ref)
        dy, dgrow = _norm_bwd(y_ref[...], g_ref[...], u_ref[...])
        dg_ref[...] += jnp.sum(dgrow, axis=0, keepdims=True)
        dyb = dy.astype(BF16)
        dy_ref[...] = dyb
        dog = _dot_nt(dyb, w_ref[...])
        gates = (g0, g1, g2, g3)
        for c in range(4):
            gt = gates[c][...]
            sg = _sigmoid(gt)
            o_ref = oa_ref if c < 2 else ob_ref
            ov = o_ref[:, (c % 2) * 256:(c % 2 + 1) * 256]
            dc = dog[:, c * 256:(c + 1) * 256]
            do_ref[:, c * 256:(c + 1) * 256] = dc * (gt * sg)
            dgate_ref[:, c * 256:(c + 1) * 256] = dc * ov * (sg * (1.0 + gt * (1.0 - sg)))

    row = lambda i: (i, 0)
    gspec = lambda c: pl.BlockSpec((tm, 256), lambda i: (i, gblk[c]))
    return pl.pallas_call(
        body, name=name, grid=(S // tm,),
        in_specs=[pl.BlockSpec((tm, D), row), pl.BlockSpec((tm, D), row), pl.BlockSpec((1, D), lambda i: (0, 0)),
                  pl.BlockSpec((D, D), lambda i: (0, 0)),
                  gspec(0), gspec(1), gspec(2), gspec(3),
                  pl.BlockSpec((tm, 512), lambda i: (i, oblk_a)),
                  pl.BlockSpec((tm, 512), lambda i: (i, oblk_b))],
        out_specs=[pl.BlockSpec((tm, D), row), pl.BlockSpec((tm, D), row), pl.BlockSpec((tm, D), row),
                   pl.BlockSpec((1, D), lambda i: (0, 0))],
        out_shape=[jax.ShapeDtypeStruct((S, D), BF16), jax.ShapeDtypeStruct((S, D), F32),
                   jax.ShapeDtypeStruct((S, D), F32), jax.ShapeDtypeStruct((1, D), F32)],
        compiler_params=_cparams(("arbitrary",)),
    )(dx_up, y, g, w, proj, proj, proj, proj, o_a, o_b)


def _rope_tables(pos, invf, name):
    S = pos.shape[0]
    tm = _pick(S, (512, 256))

    def body(p_ref, f_ref, c_ref, s1_ref, s2_ref):
        lane = lax.broadcasted_iota(jnp.int32, (1, LANES), 1)
        ang = p_ref[...].astype(F32) * f_ref[...]
        c, s = jnp.cos(ang), jnp.sin(ang)
        c_ref[...] = jnp.where((lane >= 64) & (lane < 96), c, 1.0)
        s1_ref[...] = jnp.where((lane >= 64) & (lane < 80), -s, 0.0)
        s2_ref[...] = jnp.where((lane >= 80) & (lane < 96), s, 0.0)

    spec = pl.BlockSpec((tm, LANES), lambda i: (i, 0))
    return pl.pallas_call(
        body, name=name, grid=(S // tm,),
        in_specs=[pl.BlockSpec((tm, 1), lambda i: (i, 0)), pl.BlockSpec((1, LANES), lambda i: (0, 0))],
        out_specs=[spec, spec, spec],
        out_shape=[jax.ShapeDtypeStruct((S, LANES), F32)] * 3,
        compiler_params=_cparams(("parallel",)),
    )(pos, invf)


def _rope(x, c, s1, s2):
    return x * c + pltpu.roll(x, LANES - 16, 1) * s1 + pltpu.roll(x, 16, 1) * s2


def _rope_t(d, c, s1, s2):
    return d * c + pltpu.roll(d * s1, 16, 1) + pltpu.roll(d * s2, LANES - 16, 1)


def _mla_prep(proj, gq, gkv, wq, wkv, cosT, s1T, s2T, name):
    S = proj.shape[0]
    tm = _pick(S, (256,))

    def body(p_ref, gq_ref, gkv_ref, wq_ref, wkv_ref, c_ref, s1_ref, s2_ref, q_ref, k_ref, v_ref, qn_ref, cn_ref):
        qa = p_ref[:, 0:384]
        ckv = p_ref[:, 384:640]
        kr = p_ref[:, 640:768]
        qn = ((qa * _rstd(qa)) * gq_ref[...]).astype(BF16)
        cn = ((ckv * _rstd(ckv)) * gkv_ref[...]).astype(BF16)
        qn_ref[...] = qn
        cn_ref[...] = cn
        qb = _dot(qn, wq_ref[...])
        kvb = _dot(cn, wkv_ref[...])
        c, s1, s2 = c_ref[...], s1_ref[...], s2_ref[...]
        krr = _rope(kr, c, s1, s2)
        for h in range(8):
            sl = slice(h * LANES, (h + 1) * LANES)
            q_ref[:, sl] = _rope(qb[:, sl], c, s1, s2)
            k_ref[:, sl] = kvb[:, sl] + krr
        v_ref[...] = kvb[:, 1024:1536]

    row = lambda i: (i, 0)
    fixed = lambda i: (0, 0)
    tspec = pl.BlockSpec((tm, LANES), row)
    return pl.pallas_call(
        body, name=name, grid=(S // tm,),
        in_specs=[pl.BlockSpec((tm, L0_PREP_W), lambda i: (i, L0_PREP // L0_PREP_W)),
                  pl.BlockSpec((1, 384), fixed), pl.BlockSpec((1, 256), fixed),
                  pl.BlockSpec((384, 1024), fixed), pl.BlockSpec((256, 1536), fixed), tspec, tspec, tspec],
        out_specs=[pl.BlockSpec((tm, 1024), row), pl.BlockSpec((tm, 1024), row), pl.BlockSpec((tm, 512), row),
                   pl.BlockSpec((tm, 384), row), pl.BlockSpec((tm, 256), row)],
        out_shape=[jax.ShapeDtypeStruct((S, 1024), F32), jax.ShapeDtypeStruct((S, 1024), F32),
                   jax.ShapeDtypeStruct((S, 512), F32), jax.ShapeDtypeStruct((S, 384), BF16),
                   jax.ShapeDtypeStruct((S, 256), BF16)],
        compiler_params=_cparams(("parallel",)),
    )(proj, gq, gkv, wq, wkv, cosT, s1T, s2T)


def _mla_prep_bwd(dq, dk, dv, proj, gq, gkv, wq, wkv, cosT, s1T, s2T, name):
    S = proj.shape[0]
    tm = _pick(S, (256,))

    def body(dq_ref, dk_ref, dv_ref, p_ref, gq_ref, gkv_ref, wq_ref, wkv_ref, c_ref, s1_ref, s2_ref,
             dp_ref, dqb_ref, dkvb_ref, dgq_ref, dgkv_ref):
        @pl.when(pl.program_id(0) == 0)
        def _():
            dgq_ref[...] = jnp.zeros_like(dgq_ref)
            dgkv_ref[...] = jnp.zeros_like(dgkv_ref)
        c, s1, s2 = c_ref[...], s1_ref[...], s2_ref[...]
        lane = lax.broadcasted_iota(jnp.int32, (1, LANES), 1)
        dkr = jnp.zeros((tm, LANES), F32)
        for h in range(8):
            sl = slice(h * LANES, (h + 1) * LANES)
            dqb_ref[:, sl] = _rope_t(dq_ref[:, sl], c, s1, s2).astype(BF16)
            dkh = dk_ref[:, sl]
            dkvb_ref[:, sl] = dkh.astype(BF16)
            dkr = dkr + dkh
        dkvb_ref[:, 1024:1536] = dv_ref[...].astype(BF16)
        dkr = jnp.where((lane >= 64) & (lane < 96), _rope_t(dkr, c, s1, s2), 0.0)
        dqn = _dot_nt(dqb_ref[...], wq_ref[...])
        dcn = _dot_nt(dkvb_ref[...], wkv_ref[...])
        dqa, gq_row = _norm_bwd(p_ref[:, 0:384], gq_ref[...], dqn)
        dckv, gkv_row = _norm_bwd(p_ref[:, 384:640], gkv_ref[...], dcn)
        dp_ref[:, 0:384] = dqa
        dp_ref[:, 384:640] = dckv
        dp_ref[:, 640:768] = dkr
        dgq_ref[...] += jnp.sum(gq_row, axis=0, keepdims=True)
        dgkv_ref[...] += jnp.sum(gkv_row, axis=0, keepdims=True)

    row = lambda i: (i, 0)
    fixed = lambda i: (0, 0)
    tspec = pl.BlockSpec((tm, LANES), row)
    return pl.pallas_call(
        body, name=name, grid=(S // tm,),
        in_specs=[pl.BlockSpec((tm, 1024), row), pl.BlockSpec((tm, 1024), row), pl.BlockSpec((tm, 512), row),
                  pl.BlockSpec((tm, L0_PREP_W), lambda i: (i, L0_PREP // L0_PREP_W)),
                  pl.BlockSpec((1, 384), fixed), pl.BlockSpec((1, 256), fixed),
                  pl.BlockSpec((384, 1024), fixed), pl.BlockSpec((256, 1536), fixed), tspec, tspec, tspec],
        out_specs=[pl.BlockSpec((tm, L0_PREP_W), row), pl.BlockSpec((tm, 1024), row), pl.BlockSpec((tm, 1536), row),
                   pl.BlockSpec((1, 384), fixed), pl.BlockSpec((1, 256), fixed)],
        out_shape=[jax.ShapeDtypeStruct((S, L0_PREP_W), F32), jax.ShapeDtypeStruct((S, 1024), BF16),
                   jax.ShapeDtypeStruct((S, 1536), BF16), jax.ShapeDtypeStruct((1, 384), F32),
                   jax.ShapeDtypeStruct((1, 256), F32)],
        compiler_params=_cparams(("arbitrary",)),
    )(dq, dk, dv, proj, gq, gkv, wq, wkv, cosT, s1T, s2T)


def _fox_prep(proj, bf, name):
    S = proj.shape[0]
    tm = _pick(S, (256,))

    def body(f_ref, b_ref, c_ref, carry_ref):
        @pl.when(pl.program_id(0) == 0)
        def _():
            carry_ref[...] = jnp.zeros_like(carry_ref)
        u = f_ref[...] + b_ref[...]
        lf = jnp.minimum(u, 0.0) - jnp.log(1.0 + jnp.exp(-jnp.abs(u)))
        r = lax.broadcasted_iota(jnp.int32, (tm, tm), 0)
        cidx = lax.broadcasted_iota(jnp.int32, (tm, tm), 1)
        tri = (cidx <= r).astype(BF16)
        hi, mid, lo = _split3(lf)
        c = carry_ref[...] + (_dot(tri, hi) + _dot(tri, mid) + _dot(tri, lo))
        c_ref[...] = c
        carry_ref[...] = c[tm - 1:tm, :]

    return pl.pallas_call(
        body, name=name, grid=(S // tm,),
        in_specs=[pl.BlockSpec((tm, LANES), lambda i: (i, L1_F // LANES)), pl.BlockSpec((1, LANES), lambda i: (0, 0))],
        out_specs=pl.BlockSpec((tm, LANES), lambda i: (i, 0)),
        out_shape=jax.ShapeDtypeStruct((S, LANES), F32),
        scratch_shapes=[pltpu.VMEM((1, LANES), F32)],
        compiler_params=_cparams(("arbitrary",)),
    )(proj, bf)


def _fox_prep_bwd(dc, proj, bf, name):
    S = proj.shape[0]
    tm = _pick(S, (256,))
    nb = S // tm

    def body(dc_ref, f_ref, b_ref, df_ref, db_ref, carry_ref):
        @pl.when(pl.program_id(0) == 0)
        def _():
            carry_ref[...] = jnp.zeros_like(carry_ref)
            db_ref[...] = jnp.zeros_like(db_ref)
        r = lax.broadcasted_iota(jnp.int32, (tm, tm), 0)
        cidx = lax.broadcasted_iota(jnp.int32, (tm, tm), 1)
        tri = (cidx >= r).astype(BF16)
        hi, mid, lo = _split3(dc_ref[...])
        dlf = carry_ref[...] + (_dot(tri, hi) + _dot(tri, mid) + _dot(tri, lo))
        carry_ref[...] = dlf[0:1, :]
        u = f_ref[...] + b_ref[...]
        e = jnp.exp(-jnp.abs(u))
        sneg = jnp.where(u >= 0.0, e, 1.0) / (1.0 + e)
        lane = lax.broadcasted_iota(jnp.int32, (1, LANES), 1)
        df = jnp.where(lane < FOX_HEADS, dlf * sneg, 0.0)
        df_ref[...] = df
        db_ref[...] += jnp.sum(df, axis=0, keepdims=True)

    return pl.pallas_call(
        body, name=name, grid=(nb,),
        in_specs=[pl.BlockSpec((tm, LANES), lambda i: (nb - 1 - i, 0)),
                  pl.BlockSpec((tm, LANES), lambda i: (nb - 1 - i, L1_F // LANES)),
                  pl.BlockSpec((1, LANES), lambda i: (0, 0))],
        out_specs=[pl.BlockSpec((tm, LANES), lambda i: (nb - 1 - i, 0)), pl.BlockSpec((1, LANES), lambda i: (0, 0))],
        out_shape=[jax.ShapeDtypeStruct((S, LANES), F32), jax.ShapeDtypeStruct((1, LANES), F32)],
        scratch_shapes=[pltpu.VMEM((1, LANES), F32)],
        compiler_params=_cparams(("arbitrary",)),
    )(dc, proj, bf)


def _att_specs(kind, S, T):
    if kind == "sb":
        qo, ko, vo, go = L0_SBQ // LANES, L0_SBK // LANES, L0_SBV // LANES, L0_SBG // LANES
    elif kind == "fox":
        qo, ko, vo, go = L1_Q // LANES, L1_K // LANES, L1_V // LANES, L1_G // LANES
    else:
        go = L0_MLG // LANES
        return (pl.BlockSpec((T, 256), lambda p, i: (i, p)), pl.BlockSpec((S, 256), lambda p, i: (0, p)),
                pl.BlockSpec((S, LANES), lambda p, i: (0, p)), pl.BlockSpec((T, LANES), lambda p, i: (i, go + p)))
    return (pl.BlockSpec((T, LANES), lambda p, i: (i, qo + p)), pl.BlockSpec((S, LANES), lambda p, i: (0, ko + p)),
            pl.BlockSpec((S, LANES), lambda p, i: (0, vo + p)), pl.BlockSpec((T, LANES), lambda p, i: (i, go + p)))


def _head_q(kind, q_ref, m0, scale):
    if kind == "mla":
        return [q_ref[:, 0:LANES].astype(BF16), q_ref[:, LANES:2 * LANES].astype(BF16)]
    qv = q_ref[...] * scale
    return [jnp.where(m0, qv, 0.0).astype(BF16), jnp.where(m0, 0.0, qv).astype(BF16)]


def _head_k(kind, k_ref, start, T):
    if kind == "mla":
        return [k_ref[pl.ds(start, T), 0:LANES].astype(BF16), k_ref[pl.ds(start, T), LANES:2 * LANES].astype(BF16)]
    kb = k_ref[pl.ds(start, T), :].astype(BF16)
    return [kb, kb]


def _transpose_tiles(src, col_blk, npairs, name):
    S = src.shape[0]
    T = ATT_T

    def body(x_ref, o_ref):
        o_ref[...] = x_ref[...].T.astype(BF16)

    return pl.pallas_call(
        body, name=name, grid=(S // T, npairs),
        in_specs=[pl.BlockSpec((T, LANES), lambda j, p: (j, col_blk + p))],
        out_specs=pl.BlockSpec((None, None, LANES, T), lambda j, p: (p, j, 0, 0)),
        out_shape=jax.ShapeDtypeStruct((npairs, S // T, LANES, T), BF16),
        compiler_params=_cparams(("parallel", "parallel")),
    )(src)


def _softmax_fwd(kind, qkvg, c_col, S, npairs, name):
    T = ATT_T
    nq = S // T
    fox = kind == "fox"
    scale = (96 if kind == "mla" else 64) ** -0.5

    def body(*refs):
        if fox:
            q_ref, k_ref, vt_ref, g_ref, cc_ref, o_ref, og_ref, st_ref, m_ref, acc_ref = refs
        else:
            q_ref, k_ref, vt_ref, g_ref, o_ref, og_ref, st_ref, m_ref, acc_ref = refs
        i = pl.program_id(1)
        m0 = lax.broadcasted_iota(jnp.int32, (1, LANES), 1) < 64
        top = lax.broadcasted_iota(jnp.int32, (LANES, 1), 0) < 64
        key = lax.broadcasted_iota(jnp.int32, (T, LANES), 0)
        qrow = lax.broadcasted_iota(jnp.int32, (T, LANES), 1)
        qh = _head_q(kind, q_ref, m0, scale)
        m_ref[...] = jnp.full(m_ref.shape, NEG, F32)
        acc_ref[...] = jnp.zeros(acc_ref.shape, F32)
        chains = [(h, b) for h in range(2) for b in range(T // LANES)]

        def tile(j, masked):
            start = pl.multiple_of(j * T, T)
            vt = vt_ref[j]
            vth = [jnp.where(top, vt, 1.0).astype(BF16), jnp.where(top, 1.0, vt).astype(BF16)]
            kh = _head_k(kind, k_ref, start, T)
            zs = _split_blocks([_dot_nt(kh[h], qh[h]) for h in range(2)])
            ps = []
            alphas = []
            for (h, b), z in zip(chains, zs):
                lanes = slice(b * LANES, (b + 1) * LANES)
                if kind == "mla":
                    z = z * scale
                if fox:
                    z = z - cc_ref[h, pl.ds(start, T), :]
                if masked:
                    z = jnp.where(key <= qrow + b * LANES, z, NEG)
                m_prev = m_ref[h, :, lanes]
                m_new = jnp.maximum(m_prev, jnp.max(z, axis=0, keepdims=True))
                alphas.append(jnp.exp(m_prev - m_new))
                ps.append(jnp.exp(z - m_new).astype(BF16))
                m_ref[h, :, lanes] = m_new
            for h, (p, a) in enumerate(zip(_join_blocks(ps, T // LANES), _join_blocks(alphas, T // LANES))):
                acc_ref[h] = a * acc_ref[h] + _dot(vth[h], p)

        def step(j, carry):
            tile(j, False)
            return carry

        lax.fori_loop(0, i, step, 0)
        tile(i, True)
        acc = [acc_ref[0], acc_ref[1]]
        ot = jnp.concatenate([acc[0][0:64] / acc[0][64:128], acc[1][64:128] / acc[1][0:64]], axis=0)
        o = ot.T
        o_ref[...] = o
        gt = g_ref[...]
        og_ref[...] = (o * (gt * _sigmoid(gt))).astype(BF16)
        st_ref[0] = m_ref[0] + jnp.log(acc[0][64:65])
        st_ref[1] = m_ref[1] + jnp.log(acc[1][0:1])

    qs, ks, _, gs = _att_specs(kind, S, T)
    in_specs = [qs, ks, pl.BlockSpec((None, nq, LANES, T), lambda p, i: (p, 0, 0, 0)), gs]
    args = list(qkvg)
    if fox:
        in_specs += [pl.BlockSpec((2, S, 1), lambda p, i: (p, 0, 0))]
        args += [c_col]
    W = npairs * LANES
    return pl.pallas_call(
        body, name=name, grid=(npairs, nq), in_specs=in_specs,
        out_specs=[pl.BlockSpec((T, LANES), lambda p, i: (i, p)), pl.BlockSpec((T, LANES), lambda p, i: (i, p)),
                   pl.BlockSpec((2, None, 1, T), lambda p, i: (p, i, 0, 0))],
        out_shape=[jax.ShapeDtypeStruct((S, W), F32), jax.ShapeDtypeStruct((S, W), BF16),
                   jax.ShapeDtypeStruct((2 * npairs, nq, 1, T), F32)],
        scratch_shapes=[pltpu.VMEM((2, 1, T), F32), pltpu.VMEM((2, LANES, T), F32)],
        compiler_params=_cparams(("parallel", "parallel")),
    )(*args)


def _softmax_bwd(kind, qkv, do, do_off, o, lse, c_row, S, npairs, name):
    T = ATT_T
    nq = S // T
    fox = kind == "fox"
    mla = kind == "mla"
    scale = (96 if mla else 64) ** -0.5
    kw = 256 if mla else LANES

    def body(*refs):
        if fox:
            q_ref, k_ref, v_ref, do_ref, o_ref, st_ref, cr_ref, dq_ref, dk_ref, dv_ref, dc_ref, dcq_ref = refs
        else:
            q_ref, k_ref, v_ref, do_ref, o_ref, st_ref, dq_ref, dk_ref, dv_ref = refs
        i = pl.program_id(1)

        @pl.when(i == 0)
        def _():
            dk_ref[...] = jnp.zeros_like(dk_ref)
            dv_ref[...] = jnp.zeros_like(dv_ref)
            if fox:
                dc_ref[...] = jnp.zeros_like(dc_ref)

        m0 = lax.broadcasted_iota(jnp.int32, (1, LANES), 1) < 64
        causal = lax.broadcasted_iota(jnp.int32, (T, T), 1) <= lax.broadcasted_iota(jnp.int32, (T, T), 0)
        qh = _head_q(kind, q_ref, m0, scale)
        dov = do_ref[...]
        prod = dov * o_ref[...]
        dd = [jnp.sum(jnp.where(m0, prod, 0.0), axis=1, keepdims=True),
              jnp.sum(jnp.where(m0, 0.0, prod), axis=1, keepdims=True)]
        doh = [jnp.where(m0, dov, 0.0).astype(BF16), jnp.where(m0, 0.0, dov).astype(BF16)]
        lse_h = [st_ref[0], st_ref[1]]

        def tile(j, carry, masked):
            start = pl.multiple_of(j * T, T)
            vb = v_ref[pl.ds(start, T), :].astype(BF16)
            kh = _head_k(kind, k_ref, start, T)
            dqs = []
            dkc = []
            dvc = jnp.zeros((T, LANES), F32)
            for h in range(2):
                z = _dot_nt(qh[h], kh[h])
                if mla:
                    z = z * scale
                if fox:
                    z = z - cr_ref[h, pl.ds(j, 1), :]
                if masked:
                    z = jnp.where(causal, z, NEG)
                p = jnp.exp(z - lse_h[h])
                ds = p * (_dot_nt(doh[h], vb) - dd[h])
                dsb = ds.astype(BF16)
                dqh = carry[h][0] + _dot(dsb, kh[h])
                dkc.append(_dot_tn(dsb, qh[h]))
                dvc = dvc + _dot_tn(p.astype(BF16), doh[h])
                if fox:
                    dc_ref[h, pl.ds(j, 1), :] += -jnp.sum(ds, axis=0, keepdims=True)
                    dqs.append((dqh, carry[h][1] + jnp.sum(ds, axis=1, keepdims=True)))
                else:
                    dqs.append((dqh,))
            if mla:
                dk_ref[pl.ds(start, T), 0:LANES] += dkc[0] * scale
                dk_ref[pl.ds(start, T), LANES:2 * LANES] += dkc[1] * scale
            else:
                dk_ref[pl.ds(start, T), :] += dkc[0] + dkc[1]
            dv_ref[pl.ds(start, T), :] += dvc
            return tuple(dqs)

        one = (jnp.zeros((T, LANES), F32), jnp.zeros((T, 1), F32)) if fox else (jnp.zeros((T, LANES), F32),)
        carry = lax.fori_loop(0, i, lambda j, c: tile(j, c, False), (one, one))
        carry = tile(i, carry, True)
        if mla:
            dq_ref[:, 0:LANES] = carry[0][0] * scale
            dq_ref[:, LANES:2 * LANES] = carry[1][0] * scale
        else:
            dq_ref[...] = jnp.where(m0, carry[0][0], carry[1][0]) * scale
        if fox:
            dcq_ref[0] = carry[0][1]
            dcq_ref[1] = carry[1][1]

    qs, ks, vs, _ = _att_specs(kind, S, T)
    in_specs = [qs, ks, vs,
                pl.BlockSpec((T, LANES), lambda p, i: (i, do_off + p)),
                pl.BlockSpec((T, LANES), lambda p, i: (i, p)),
                pl.BlockSpec((2, T, 1), lambda p, i: (p, i, 0))]
    args = list(qkv) + [do, o, lse]
    W = npairs * LANES
    out_specs = [pl.BlockSpec((T, kw), lambda p, i: (i, p)), pl.BlockSpec((S, kw), lambda p, i: (0, p)),
                 pl.BlockSpec((S, LANES), lambda p, i: (0, p))]
    out_shape = [jax.ShapeDtypeStruct((S, npairs * kw), F32), jax.ShapeDtypeStruct((S, npairs * kw), F32),
                 jax.ShapeDtypeStruct((S, W), F32)]
    if fox:
        in_specs += [pl.BlockSpec((2, nq, T), lambda p, i: (p, 0, 0))]
        args += [c_row]
        out_specs +=[pl.BlockSpec((2, nq, T), lambda p, i: (p, 0, 0)), pl.BlockSpec((2, T, 1), lambda p, i: (p, i, 0))]
        out_shape += [jax.ShapeDtypeStruct((2 * npairs, nq, T), F32), jax.ShapeDtypeStruct((2 * npairs, S, 1), F32)]
    return pl.pallas_call(
        body, name=name, grid=(npairs, nq), in_specs=in_specs, out_specs=out_specs, out_shape=out_shape,
        compiler_params=_cparams(("parallel", "arbitrary")),
    )(*args)


def _softplus_parts(z):
    sp = jnp.maximum(z, 0.0) + jnp.log(1.0 + jnp.exp(-jnp.abs(z)))
    return -sp, z - sp


def _split2(x):
    hi = x.astype(BF16)
    return hi, (x - hi.astype(F32)).astype(BF16)


def _sb_fwd(proj, S, npairs, name):
    T = ATT_T
    nq = S // T
    scale = 64 ** -0.5

    def body(q_ref, k_ref, v_ref, g_ref, o_ref, og_ref, st_ref):
        i = pl.program_id(1)
        m0 = lax.broadcasted_iota(jnp.int32, (1, LANES), 1) < 64
        r = lax.broadcasted_iota(jnp.int32, (T, T), 0)
        c = lax.broadcasted_iota(jnp.int32, (T, T), 1)
        before = c < r
        after = (r > c).astype(BF16)
        qh = _head_q("sb", q_ref, m0, scale)

        def tile(j, carry, masked):
            start = pl.multiple_of(j * T, T)
            vb = v_ref[pl.ds(start, T), :].astype(BF16)
            kb = k_ref[pl.ds(start, T), :].astype(BF16)
            out = []
            for h in range(2):
                rem, acc = carry[h]
                z = _dot_nt(qh[h], kb)
                lk, la = _softplus_parts(z)
                if masked:
                    lk = jnp.where(before, lk, 0.0)
                hi, lo = _split2(lk)
                lr = rem + (_dot(hi, after) + _dot(lo, after))
                w = jnp.exp(la + lr)
                if masked:
                    w = jnp.where(before, w, 0.0)
                out.append((rem + jnp.sum(lk, axis=1, keepdims=True), acc + _dot(w.astype(BF16), vb)))
            return tuple(out)

        init = tuple((jnp.zeros((T, 1), F32), jnp.zeros((T, LANES), F32)) for _ in range(2))
        carry = tile(i, init, True)
        carry = lax.fori_loop(0, i, lambda jj, cr: tile(i - 1 - jj, cr, False), carry)
        o = jnp.where(m0, carry[0][1], carry[1][1])
        o_ref[...] = o
        gt = g_ref[...]
        og_ref[...] = (o * (gt * _sigmoid(gt))).astype(BF16)
        for h in range(2):
            st_ref[h] = carry[h][0]

    W = npairs * LANES
    return pl.pallas_call(
        body, name=name, grid=(npairs, nq), in_specs=list(_att_specs("sb", S, T)),
        out_specs=[pl.BlockSpec((T, LANES), lambda p, i: (i, p)), pl.BlockSpec((T, LANES), lambda p, i: (i, p)),
                   pl.BlockSpec((2, T, 1), lambda p, i: (p, i, 0))],
        out_shape=[jax.ShapeDtypeStruct((S, W), F32), jax.ShapeDtypeStruct((S, W), BF16),
                   jax.ShapeDtypeStruct((2 * npairs, S, 1), F32)],
        compiler_params=_cparams(("parallel", "parallel")),
    )(proj, proj, proj, proj)


def _sb_bwd(proj, do, tot, S, npairs, name):
    T = ATT_T
    nq = S // T
    scale = 64 ** -0.5

    def body(q_ref, k_ref, v_ref, do_ref, st_ref, dq_ref, dk_ref, dv_ref):
        i = pl.program_id(1)

        @pl.when(i == 0)
        def _():
            dk_ref[...] = jnp.zeros_like(dk_ref)
            dv_ref[...] = jnp.zeros_like(dv_ref)

        m0 = lax.broadcasted_iota(jnp.int32, (1, LANES), 1) < 64
        r = lax.broadcasted_iota(jnp.int32, (T, T), 0)
        c = lax.broadcasted_iota(jnp.int32, (T, T), 1)
        before = c < r
        upto = (r <= c).astype(BF16)
        left = (r < c).astype(BF16)
        qh = _head_q("sb", q_ref, m0, scale)
        dov = do_ref[...]
        doh = [jnp.where(m0, dov, 0.0).astype(BF16), jnp.where(m0, 0.0, dov).astype(BF16)]
        tot_h = [st_ref[0], st_ref[1]]

        def tile(j, carry, masked):
            start = pl.multiple_of(j * T, T)
            vb = v_ref[pl.ds(start, T), :].astype(BF16)
            kb = k_ref[pl.ds(start, T), :].astype(BF16)
            out = []
            dkc = jnp.zeros((T, LANES), F32)
            dvc = jnp.zeros((T, LANES), F32)
            for h in range(2):
                pre, gpre, dq = carry[h]
                z = _dot_nt(qh[h], kb)
                lk, la = _softplus_parts(z)
                if masked:
                    lk = jnp.where(before, lk, 0.0)
                hi, lo = _split2(lk)
                lr = (tot_h[h] - pre) - (_dot(hi, upto) + _dot(lo, upto))
                w = jnp.exp(la + lr)
                if masked:
                    w = jnp.where(before, w, 0.0)
                g = _dot_nt(doh[h], vb) * w
                gfull = gpre + _dot(g.astype(BF16), left)
                dz = g - (g + gfull) * jnp.exp(la)
                if masked:
                    dz = jnp.where(before, dz, 0.0)
                dzb = dz.astype(BF16)
                dkc = dkc + _dot_tn(dzb, qh[h])
                dvc = dvc + _dot_tn(w.astype(BF16), doh[h])
                out.append((pre + jnp.sum(lk, axis=1, keepdims=True), gpre + jnp.sum(g, axis=1, keepdims=True),
                            dq + _dot(dzb, kb)))
            dk_ref[pl.ds(start, T), :] += dkc
            dv_ref[pl.ds(start, T), :] += dvc
            return tuple(out)

        init = tuple((jnp.zeros((T, 1), F32), jnp.zeros((T, 1), F32), jnp.zeros((T, LANES), F32)) for _ in range(2))
        carry = lax.fori_loop(0, i, lambda j, cr: tile(j, cr, False), init)
        carry = tile(i, carry, True)
        dq_ref[...] = jnp.where(m0, carry[0][2], carry[1][2]) * scale

    qs, ks, vs, _ = _att_specs("sb", S, T)
    W = npairs * LANES
    return pl.pallas_call(
        body, name=name, grid=(npairs, nq),
        in_specs=[qs, ks, vs, pl.BlockSpec((T, LANES), lambda p, i: (i, p)),
                  pl.BlockSpec((2, T, 1), lambda p, i: (p, i, 0))],
        out_specs=[pl.BlockSpec((T, LANES), lambda p, i: (i, p)), pl.BlockSpec((S, LANES), lambda p, i: (0, p)),
                   pl.BlockSpec((S, LANES), lambda p, i: (0, p))],
        out_shape=[jax.ShapeDtypeStruct((S, W), F32)] * 3,
        compiler_params=_cparams(("parallel", "arbitrary")),
    )(proj, proj, proj, do, tot)


def _split_blocks(per_head):
    return [x[:, b * LANES:(b + 1) * LANES] for x in per_head for b in range(x.shape[1] // LANES)]


def _join_blocks(per_block, nb):
    return [jnp.concatenate(per_block[h * nb:(h + 1) * nb], axis=1) for h in range(len(per_block) // nb)]


def _row_of(col):
    return jnp.broadcast_to(col, (col.shape[0], LANES)).T[0:1]


def _softmax_bwd_t(kind, q, k, kt, v, do, do_off, o, lse, c_col, S, npairs, name):
    T = ATT_T
    nq = S // T
    nb = T // LANES
    fox = kind == "fox"
    mla = kind == "mla"
    scale = (96 if mla else 64) ** -0.5
    kw = 256 if mla else LANES

    def body(*refs):
        if fox:
            (q_ref, k_ref, kt_ref, v_ref, do_ref, o_ref, st_ref, cc_ref,
             dq_ref, dk_ref, dv_ref, dck_ref, dcq_ref, dqt_ref, rs_ref, dkx_ref) = refs
        else:
            q_ref, k_ref, kt_ref, v_ref, do_ref, o_ref, st_ref, dq_ref, dk_ref, dv_ref, dqt_ref = refs
        i = pl.program_id(1)

        @pl.when(i == 0)
        def _():
            dv_ref[...] = jnp.zeros_like(dv_ref)
            if fox:
                dkx_ref[...] = jnp.zeros_like(dkx_ref)
            else:
                dk_ref[...] = jnp.zeros_like(dk_ref)

        m0 = lax.broadcasted_iota(jnp.int32, (1, LANES), 1) < 64
        top = lax.broadcasted_iota(jnp.int32, (LANES, 1), 0) < 64
        key = lax.broadcasted_iota(jnp.int32, (T, LANES), 0)
        qrow = lax.broadcasted_iota(jnp.int32, (T, LANES), 1)
        qh = _head_q(kind, q_ref, m0, scale)
        if fox:
            qv = q_ref[...] * scale
            qk = [jnp.where(m0, qv, 1.0).astype(BF16), jnp.where(m0, 1.0, qv).astype(BF16)]
        else:
            qk = qh
        dov = do_ref[...]
        prod = dov * o_ref[...]
        dd = [_row_of(jnp.sum(jnp.where(m0, prod, 0.0), axis=1, keepdims=True)),
              _row_of(jnp.sum(jnp.where(m0, 0.0, prod), axis=1, keepdims=True))]
        doh = [jnp.where(m0, dov, 0.0).astype(BF16), jnp.where(m0, 0.0, dov).astype(BF16)]
        lse = [st_ref[0], st_ref[1]]
        dqt_ref[...] = jnp.zeros_like(dqt_ref)
        if fox:
            rs_ref[...] = jnp.zeros_like(rs_ref)
        chains = [(h, b) for h in range(2) for b in range(nb)]

        def tile(j, masked):
            start = pl.multiple_of(j * T, T)
            vb = v_ref[pl.ds(start, T), :].astype(BF16)
            kh = _head_k(kind, k_ref, start, T)
            kt = kt_ref[j]
            zs = _split_blocks([_dot_nt(kh[h], qh[h]) for h in range(2)])
            dps = _split_blocks([_dot_nt(vb, doh[h]) for h in range(2)])
            ps, dss = [], []
            for (h, b), z, dp in zip(chains, zs, dps):
                lanes = slice(b * LANES, (b + 1) * LANES)
                if mla:
                    z = z * scale
                if fox:
                    z = z - cc_ref[h, pl.ds(start, T), :]
                if masked:
                    z = jnp.where(key <= qrow + b * LANES, z, NEG)
                p = jnp.exp(z - lse[h][:, lanes])
                ds = p * (dp - dd[h][:, lanes])
                dsb = ds.astype(BF16)
                if fox:
                    rs_ref[h, :, lanes] += jnp.sum(dsb.astype(F32), axis=0, keepdims=True)
                ps.append(p.astype(BF16))
                dss.append(dsb)
            dvc = None
            for h in range(2):
                dsh = jnp.concatenate(dss[h * nb:(h + 1) * nb], axis=1)
                ph = jnp.concatenate(ps[h * nb:(h + 1) * nb], axis=1)
                dkh = _dot(dsh, qk[h])
                dvh = _dot(ph, doh[h])
                dvc = dvh if dvc is None else dvc + dvh
                kth = kt[h * LANES:(h + 1) * LANES] if mla else kt
                dqt_ref[h] += _dot(kth, dsh)
                if fox:
                    dkx_ref[h, pl.ds(start, T), :] += dkh
                elif mla:
                    dk_ref[pl.ds(start, T), h * LANES:(h + 1) * LANES] += dkh * scale
                else:
                    dk_ref[pl.ds(start, T), :] += dkh
            dv_ref[pl.ds(start, T), :] += dvc

        def step(j, carry):
            tile(j, False)
            return carry

        lax.fori_loop(0, i, step, 0)
        tile(i, True)
        if mla:
            dq_ref[:, 0:LANES] = dqt_ref[0].T * scale
            dq_ref[:, LANES:2 * LANES] = dqt_ref[1].T * scale
        else:
            dq_ref[...] = jnp.where(top, dqt_ref[0], dqt_ref[1]).T * scale
        if fox:
            dcq_ref[0] = rs_ref[0]
            dcq_ref[1] = rs_ref[1]

            @pl.when(i == nq - 1)
            def _():
                dk_ref[...] = jnp.where(m0, dkx_ref[0], dkx_ref[1])
                dck_ref[0] = dkx_ref[0][:, 64:65]
                dck_ref[1] = dkx_ref[1][:, 0:1]

    qs, ks, vs, _ = _att_specs(kind, S, T)
    stat = pl.BlockSpec((2, None, 1, T), lambda p, i: (p, i, 0, 0))
    in_specs = [qs, ks, pl.BlockSpec((None, nq, kw, T), lambda p, i: (p, 0, 0, 0)), vs,
                pl.BlockSpec((T, LANES), lambda p, i: (i, do_off + p)),
                pl.BlockSpec((T, LANES), lambda p, i: (i, p)), stat]
    args = [q, k, kt, v, do, o, lse]
    W = npairs * LANES
    out_specs = [pl.BlockSpec((T, kw), lambda p, i: (i, p)), pl.BlockSpec((S, kw), lambda p, i: (0, p)),
                 pl.BlockSpec((S, LANES), lambda p, i: (0, p))]
    out_shape = [jax.ShapeDtypeStruct((S, npairs * kw), F32), jax.ShapeDtypeStruct((S, npairs * kw), F32),
                 jax.ShapeDtypeStruct((S, W), F32)]
    scratch = [pltpu.VMEM((2, LANES, T), F32)]
    if fox:
        in_specs.append(pl.BlockSpec((2, S, 1), lambda p, i: (p, 0, 0)))
        args.append(c_col)
        out_specs += [pl.BlockSpec((2, S, 1), lambda p, i: (p, 0, 0)), stat]
        out_shape += [jax.ShapeDtypeStruct((2 * npairs, S, 1), F32), jax.ShapeDtypeStruct((2 * npairs, nq, 1, T), F32)]
        scratch += [pltpu.VMEM((2, 1, T), F32), pltpu.VMEM((2, S, LANES), F32)]
    return pl.pallas_call(
        body, name=name, grid=(npairs, nq), in_specs=in_specs, out_specs=out_specs, out_shape=out_shape,
        scratch_shapes=scratch, compiler_params=_cparams(("parallel", "arbitrary")),
    )(*args)


def _sb_fwd_t(proj, vt, S, npairs, name):
    T = ATT_T
    nq = S // T
    nb = T // LANES
    scale = 64 ** -0.5

    def body(q_ref, k_ref, vt_ref, g_ref, o_ref, og_ref, st_ref, rem_ref, acc_ref):
        i = pl.program_id(1)
        m0 = lax.broadcasted_iota(jnp.int32, (1, LANES), 1) < 64
        top = lax.broadcasted_iota(jnp.int32, (LANES, 1), 0) < 64
        key = lax.broadcasted_iota(jnp.int32, (T, LANES), 0)
        qrow = lax.broadcasted_iota(jnp.int32, (T, LANES), 1)
        r = lax.broadcasted_iota(jnp.int32, (T, T), 0)
        c = lax.broadcasted_iota(jnp.int32, (T, T), 1)
        after = (c > r).astype(BF16)
        qh = _head_q("sb", q_ref, m0, scale)
        rem_ref[...] = jnp.zeros_like(rem_ref)
        acc_ref[...] = jnp.zeros_like(acc_ref)
        chains = [(h, b) for h in range(2) for b in range(nb)]

        def tile(j, masked):
            start = pl.multiple_of(j * T, T)
            vtb = vt_ref[j]
            kb = k_ref[pl.ds(start, T), :].astype(BF16)
            zs = _split_blocks([_dot_nt(kb, qh[h]) for h in range(2)])
            lks, las, his, los = [], [], [], []
            for (h, b), z in zip(chains, zs):
                lk, la = _softplus_parts(z)
                if masked:
                    lk = jnp.where(key < qrow + b * LANES, lk, 0.0)
                hi, lo = _split2(lk)
                lks.append(lk)
                las.append(la)
                his.append(hi)
                los.append(lo)
            rcs = _split_blocks([_dot(after, hi) + _dot(after, lo)
                                 for hi, lo in zip(_join_blocks(his, nb), _join_blocks(los, nb))])
            ws = []
            for (h, b), lk, la, rc in zip(chains, lks, las, rcs):
                lanes = slice(b * LANES, (b + 1) * LANES)
                w = jnp.exp(la + (rem_ref[h, :, lanes] + rc))
                if masked:
                    w = jnp.where(key < qrow + b * LANES, w, 0.0)
                ws.append(w.astype(BF16))
                rem_ref[h, :, lanes] += jnp.sum(lk, axis=0, keepdims=True)
            for h, w in enumerate(_join_blocks(ws, nb)):
                acc_ref[h] += _dot(vtb, w)

        def step(jj, carry):
            tile(i - 1 - jj, False)
            return carry

        tile(i, True)
        lax.fori_loop(0, i, step, 0)
        o = jnp.where(top, acc_ref[0], acc_ref[1]).T
        o_ref[...] = o
        gt = g_ref[...]
        og_ref[...] = (o * (gt * _sigmoid(gt))).astype(BF16)
        st_ref[0] = rem_ref[0]
        st_ref[1] = rem_ref[1]

    qs, ks, _, gs = _att_specs("sb", S, T)
    W = npairs * LANES
    return pl.pallas_call(
        body, name=name, grid=(npairs, nq),
        in_specs=[qs, ks, pl.BlockSpec((None, nq, LANES, T), lambda p, i: (p, 0, 0, 0)), gs],
        out_specs=[pl.BlockSpec((T, LANES), lambda p, i: (i, p)), pl.BlockSpec((T, LANES), lambda p, i: (i, p)),
                   pl.BlockSpec((2, None, 1, T), lambda p, i: (p, i, 0, 0))],
        out_shape=[jax.ShapeDtypeStruct((S, W), F32), jax.ShapeDtypeStruct((S, W), BF16),
                   jax.ShapeDtypeStruct((2 * npairs, nq, 1, T), F32)],
        scratch_shapes=[pltpu.VMEM((2, 1, T), F32), pltpu.VMEM((2, LANES, T), F32)],
        compiler_params=_cparams(("parallel", "parallel")),
    )(proj, proj, vt, proj)


def _sb_bwd_t(proj, kt, do, tot, S, npairs, name):
    T = ATT_T
    nq = S // T
    nb = T // LANES
    scale = 64 ** -0.5

    def body(q_ref, k_ref, kt_ref, v_ref, do_ref, st_ref, dq_ref, dk_ref, dv_ref, dqt_ref, pre_ref, gpre_ref):
        i = pl.program_id(1)

        @pl.when(i == 0)
        def _():
            dk_ref[...] = jnp.zeros_like(dk_ref)
            dv_ref[...] = jnp.zeros_like(dv_ref)

        m0 = lax.broadcasted_iota(jnp.int32, (1, LANES), 1) < 64
        top = lax.broadcasted_iota(jnp.int32, (LANES, 1), 0) < 64
        key = lax.broadcasted_iota(jnp.int32, (T, LANES), 0)
        qrow = lax.broadcasted_iota(jnp.int32, (T, LANES), 1)
        r = lax.broadcasted_iota(jnp.int32, (T, T), 0)
        c = lax.broadcasted_iota(jnp.int32, (T, T), 1)
        upto = (c <= r).astype(BF16)
        left = (c < r).astype(BF16)
        qh = _head_q("sb", q_ref, m0, scale)
        dov = do_ref[...]
        doh = [jnp.where(m0, dov, 0.0).astype(BF16), jnp.where(m0, 0.0, dov).astype(BF16)]
        tot_h = [st_ref[0], st_ref[1]]
        dqt_ref[...] = jnp.zeros_like(dqt_ref)
        pre_ref[...] = jnp.zeros_like(pre_ref)
        gpre_ref[...] = jnp.zeros_like(gpre_ref)
        chains = [(h, b) for h in range(2) for b in range(nb)]

        def tile(j, masked):
            start = pl.multiple_of(j * T, T)
            vb = v_ref[pl.ds(start, T), :].astype(BF16)
            kb = k_ref[pl.ds(start, T), :].astype(BF16)
            kt = kt_ref[j]
            zs = _split_blocks([_dot_nt(kb, qh[h]) for h in range(2)])
            dws = _split_blocks([_dot_nt(vb, doh[h]) for h in range(2)])
            lks, las, his, los = [], [], [], []
            for (h, b), z in zip(chains, zs):
                lk, la = _softplus_parts(z)
                if masked:
                    lk = jnp.where(key < qrow + b * LANES, lk, 0.0)
                hi, lo = _split2(lk)
                lks.append(lk)
                las.append(la)
                his.append(hi)
                los.append(lo)
            pcs = _split_blocks([_dot(upto, hi) + _dot(upto, lo)
                                 for hi, lo in zip(_join_blocks(his, nb), _join_blocks(los, nb))])
            ws, gs = [], []
            for (h, b), lk, la, pc, dw in zip(chains, lks, las, pcs, dws):
                lanes = slice(b * LANES, (b + 1) * LANES)
                w = jnp.exp(la + ((tot_h[h][:, lanes] - pre_ref[h, :, lanes]) - pc))
                if masked:
                    w = jnp.where(key < qrow + b * LANES, w, 0.0)
                ws.append(w.astype(BF16))
                gs.append(dw * w)
                pre_ref[h, :, lanes] += jnp.sum(lk, axis=0, keepdims=True)
            gcs = _split_blocks([_dot(left, g) for g in _join_blocks([g.astype(BF16) for g in gs], nb)])
            dzs = []
            for (h, b), la, g, gc in zip(chains, las, gs, gcs):
                lanes = slice(b * LANES, (b + 1) * LANES)
                dz = g - (g + (gpre_ref[h, :, lanes] + gc)) * jnp.exp(la)
                if masked:
                    dz = jnp.where(key < qrow + b * LANES, dz, 0.0)
                dzs.append(dz.astype(BF16))
                gpre_ref[h, :, lanes] += jnp.sum(g, axis=0, keepdims=True)
            dkc = dvc = None
            for h in range(2):
                dzh = jnp.concatenate(dzs[h * nb:(h + 1) * nb], axis=1)
                wh = jnp.concatenate(ws[h * nb:(h + 1) * nb], axis=1)
                dkh = _dot(dzh, qh[h])
                dvh = _dot(wh, doh[h])
                dkc = dkh if dkc is None else dkc + dkh
                dvc = dvh if dvc is None else dvc + dvh
                dqt_ref[h] += _dot(kt, dzh)
            dk_ref[pl.ds(start, T), :] += dkc
            dv_ref[pl.ds(start, T), :] += dvc

        def step(j, carry):
            tile(j, False)
            return carry

        lax.fori_loop(0, i, step, 0)
        tile(i, True)
        dq_ref[...] = jnp.where(top, dqt_ref[0], dqt_ref[1]).T * scale

    qs, ks, vs, _ = _att_specs("sb", S, T)
    W = npairs * LANES
    return pl.pallas_call(
        body, name=name, grid=(npairs, nq),
        in_specs=[qs, ks, pl.BlockSpec((None, nq, LANES, T), lambda p, i: (p, 0, 0, 0)), vs,
                  pl.BlockSpec((T, LANES), lambda p, i: (i, p)),
                  pl.BlockSpec((2, None, 1, T), lambda p, i: (p, i, 0, 0))],
        out_specs=[pl.BlockSpec((T, LANES), lambda p, i: (i, p)), pl.BlockSpec((S, LANES), lambda p, i: (0, p)),
                   pl.BlockSpec((S, LANES), lambda p, i: (0, p))],
        out_shape=[jax.ShapeDtypeStruct((S, W), F32)] * 3,
        scratch_shapes=[pltpu.VMEM((2, LANES, T), F32), pltpu.VMEM((2, 1, T), F32), pltpu.VMEM((2, 1, T), F32)],
        compiler_params=_cparams(("parallel", "arbitrary")),
    )(proj, proj, kt, proj, do, tot)


def _pad_w0(w):
    z = lambda n: jnp.zeros((w.shape[0], n), w.dtype)
    return jnp.concatenate([w[:, 2048:2432], w[:, 2432:2688], z(64), w[:, 2688:2720], z(32),
                            w[:, 1536:2048], w[:, 2720:3232], w[:, 0:512], w[:, 512:1024], w[:, 1024:1536]], axis=1)


def _unpad_w0(wp):
    return jnp.concatenate([wp[:, L0_SBQ:L0_SBQ + 512], wp[:, L0_SBK:L0_SBK + 512], wp[:, L0_SBV:L0_SBV + 512],
                            wp[:, L0_SBG:L0_SBG + 512], wp[:, 0:384], wp[:, 384:640], wp[:, 704:736],
                            wp[:, L0_MLG:L0_MLG + 512]], axis=1)


def _pad_wq(w):
    return jnp.pad(w.reshape(384, 8, 96), ((0, 0), (0, 0), (0, 32))).reshape(384, 1024)


def _unpad_wq(wp):
    return wp.reshape(384, 8, 128)[:, :, :96].reshape(384, 768)


def _pad_wkv(w):
    w3 = w.reshape(256, 8, 128)
    k = jnp.pad(w3[:, :, :64], ((0, 0), (0, 0), (0, 64))).reshape(256, 1024)
    return jnp.concatenate([k, w3[:, :, 64:].reshape(256, 512)], axis=1)


def _unpad_wkv(wp):
    k = wp[:, :1024].reshape(256, 8, 128)[:, :, :64]
    v = wp[:, 1024:].reshape(256, 8, 64)
    return jnp.concatenate([k, v], axis=-1).reshape(256, 1024)


def _pad_w1(w):
    return jnp.concatenate([w, jnp.zeros((w.shape[0], L1_WIDTH - ODD_IN_WIDTH), w.dtype)], axis=1)


def _local_step(x, positions, target, g, w0p, wqp, wkvp, wo0, w1p, wo1):
    S = x.shape[0]
    nq = S // ATT_T
    invf = ROPE_THETA ** (-jnp.arange(0, MLA_ROPE_DIM, 2, dtype=F32) / MLA_ROPE_DIM)
    invf = jnp.concatenate([jnp.zeros((64,), F32), invf, invf, jnp.zeros((32,), F32)]).reshape(1, LANES)
    cosT, s1T, s2T = _rope_tables(positions.reshape(S, 1), invf, "rope_tables")
    bfp = jnp.pad(g["l1_b_f"], ((0, 0), (0, LANES - FOX_HEADS)))

    proj0, h0 = _norm_matmul(x, g["l0_pre_g"], w0p, "l0_in_proj")
    qm, km, vm, qn, cn = _mla_prep(proj0, g["l0_q_a_g"], g["l0_kv_a_g"], wqp, wkvp, cosT, s1T, s2T, "mla_prep")
    sb_vt = _transpose_tiles(proj0, L0_SBV // LANES, 4, "sb_vt")
    sb_kt = _transpose_tiles(proj0, L0_SBK // LANES, 4, "sb_kt")
    o_sb, og_sb, tot_sb = _sb_fwd_t(proj0, sb_vt, S, 4, "sb_fwd")
    vmt = _transpose_tiles(vm, 0, 4, "mla_vt")
    kmt = _transpose_tiles(km, 0, 8, "mla_kt").reshape(4, 2, nq, LANES, ATT_T).transpose(0, 2, 1, 3, 4)
    kmt = kmt.reshape(4, nq, 2 * LANES, ATT_T)
    o_ml, og_ml, lse_ml = _softmax_fwd("mla", (qm, km, vmt, proj0), None, S, 4, "mla_fwd")
    y0, x1 = _out_proj(og_sb, og_ml, 0, 0, wo0, x, g["l0_post_g"], None, "l0_out_proj")

    proj1, h1 = _norm_matmul(x1, g["l1_pre_g"], w1p, "l1_in_proj")
    cfx = _fox_prep(proj1, bfp, "fox_prep")
    c16 = cfx[:, :FOX_HEADS].T
    c_col = c16.reshape(FOX_HEADS, S, 1)
    vt1 = _transpose_tiles(proj1, L1_V // LANES, 8, "fox_vt")
    kt1 = _transpose_tiles(proj1, L1_K // LANES, 8, "fox_kt")
    o_fx, og_fx, lse_fx = _softmax_fwd("fox", (proj1, proj1, vt1, proj1), c_col, S, 8, "fox_fwd")
    y1, dx2, lsum = _out_proj(og_fx, og_fx, 0, 1, wo1, x1, g["l1_post_g"], target, "l1_out_proj")

    dy1, do1, dgate1, d_post1 = _out_proj_bwd(dx2, y1, g["l1_post_g"], wo1, proj1, (L1_G, L1_G + 512), o_fx, o_fx, 0, 1, "l1_out_bwd")
    dwo1 = _matmul_tn(og_fx, dy1, "l1_dw_out")
    dq1, dk1, dv1, dck, dcq = _softmax_bwd_t("fox", proj1, proj1, kt1, proj1, do1, 0, o_fx, lse_fx, c_col, S, 8,
                                             "fox_bwd")
    dc = jnp.pad((dcq.reshape(FOX_HEADS, S) - dck.reshape(FOX_HEADS, S)).T, ((0, 0), (0, LANES - FOX_HEADS)))
    df, d_bf = _fox_prep_bwd(dc, proj1, bfp, "fox_prep_bwd")
    dproj1 = jnp.concatenate([dq1, dk1, dv1, dgate1, df], axis=1)
    dx1, d_pre1 = _in_proj_bwd(dproj1, w1p, x1, g["l1_pre_g"], dx2, "l1_in_bwd")
    dw1p = _matmul_tn(h1, dproj1, "l1_dw_in")

    dy0, do0, dgate0, d_post0 = _out_proj_bwd(dx1, y0, g["l0_post_g"], wo0, proj0, (L0_SBG, L0_MLG), o_sb, o_ml, 0, 0,
                                              "l0_out_bwd")
    og0 = jnp.concatenate([og_sb, og_ml], axis=1)
    dwo0 = _matmul_tn(og0, dy0, "l0_dw_out")
    dsq, dsk, dsv = _sb_bwd_t(proj0, sb_kt, do0, tot_sb, S, 4, "sb_bwd")
    dqm, dkm, dvm = _softmax_bwd_t("mla", qm, km, kmt, vm, do0, 4, o_ml, lse_ml, None, S, 4, "mla_bwd")
    dprep, dqb, dkvb, d_qag, d_kvag = _mla_prep_bwd(dqm, dkm, dvm, proj0, g["l0_q_a_g"], g["l0_kv_a_g"], wqp, wkvp,
                                                    cosT, s1T, s2T, "mla_prep_bwd")
    dwqp = _matmul_tn(qn, dqb, "l0_dw_qb")
    dwkvp = _matmul_tn(cn, dkvb, "l0_dw_kvb")
    dproj0 = jnp.concatenate([dprep, dgate0[:, :512], dgate0[:, 512:], dsq, dsk, dsv], axis=1)
    dx0, d_pre0 = _in_proj_bwd(dproj0, w0p, x, g["l0_pre_g"], dx1, "l0_in_bwd")
    dw0p = _matmul_tn(h0, dproj0, "l0_dw_in")

    grads = {
        "l0_pre_g": d_pre0, "l0_post_g": d_post0, "l0_w_in": dw0p, "l0_q_a_g": d_qag, "l0_w_q_b": dwqp,
        "l0_kv_a_g": d_kvag, "l0_w_kv_b": dwkvp, "l0_w_out": dwo0, "l1_pre_g": d_pre1, "l1_post_g": d_post1,
        "l1_w_in": dw1p, "l1_b_f": d_bf[:, :FOX_HEADS], "l1_w_out": dwo1,
    }
    return lsum, dx0, grads


_ANY = pl.BlockSpec(memory_space=pl.ANY)


def _place():
    return lax.axis_index("x"), lax.axis_index("y"), lax.axis_index("c")


def _other_chips(x, y):
    return [(1 - x, y), (x, 1 - y), (1 - x, 1 - y)]


def _half(c):
    return pl.ds(c * PACK_HALF, PACK_HALF)


def _weight_gather(pack):
    def body(p_ref, out_ref, send_sems, recv_sems):
        x, y, c = _place()
        sibling = (x, y, 1 - c)
        chips = _other_chips(x, y)

        def blk(chip, cc):
            return out_ref.at[2 * chip[0] + chip[1], _half(cc)]

        def copy(k, src, dst, to):
            return pltpu.make_async_remote_copy(src_ref=src, dst_ref=dst, send_sem=send_sems.at[k],
                                                recv_sem=recv_sems.at[k], device_id=to, device_id_type=MESH)

        first = [copy(j, p_ref.at[_half(c)], blk((x, y), c), (*chip, c)) for j, chip in enumerate(chips)]
        for cp in first:
            cp.start()
        passed = [copy(3 + j, blk(chip, c), blk(chip, c), sibling) for j, chip in enumerate(chips)]
        for j, chip in enumerate(chips):
            copy(j, blk(chip, c), blk(chip, c), (x, y, c)).wait_recv()
            passed[j].start()
        for j, chip in enumerate(chips):
            copy(3 + j, blk(chip, 1 - c), blk(chip, 1 - c), (x, y, c)).wait_recv()
        for cp in first + passed:
            cp.wait_send()

    return pl.pallas_call(
        body, name="weight_gather", in_specs=[_ANY], out_specs=_ANY,
        out_shape=jax.ShapeDtypeStruct((4,) + pack.shape, pack.dtype),
        scratch_shapes=[pltpu.SemaphoreType.DMA((6,)), pltpu.SemaphoreType.DMA((6,))],
    )(pack)


GRAD_TR = 2048


def _grad_core_exchange(p):
    def body(p_ref, recv_ref, send_sems, recv_sems):
        x, y, c = _place()
        give = [pltpu.make_async_remote_copy(src_ref=p_ref.at[j, _half(1 - c)], dst_ref=recv_ref.at[j],
                                             send_sem=send_sems.at[j], recv_sem=recv_sems.at[j],
                                             device_id=(x, y, 1 - c), device_id_type=MESH) for j in range(4)]
        for cp in give:
            cp.start()
        for cp in give:
            cp.wait()

    return pl.pallas_call(
        body, name="grad_core_exchange", in_specs=[_ANY], out_specs=_ANY,
        out_shape=jax.ShapeDtypeStruct((4, PACK_HALF, LANES), p.dtype),
        scratch_shapes=[pltpu.SemaphoreType.DMA((4,)), pltpu.SemaphoreType.DMA((4,))],
    )(p)


def _grad_add_cores(p, theirs, c1):
    tr = GRAD_TR

    def body(c_ref, a_ref, b_ref, o_ref):
        o_ref[...] = a_ref[...] + b_ref[...]

    spec = pl.BlockSpec((None, tr, LANES), lambda j, r, c: (j, r, 0))
    grid_spec = pltpu.PrefetchScalarGridSpec(
        num_scalar_prefetch=1, grid=(4, PACK_HALF // tr),
        in_specs=[pl.BlockSpec((None, None, tr, LANES), lambda j, r, c: (j, c[0], r, 0)), spec], out_specs=spec)
    return pl.pallas_call(
        body, name="grad_add_cores", grid_spec=grid_spec, out_shape=jax.ShapeDtypeStruct(theirs.shape, theirs.dtype),
        compiler_params=_cparams(("parallel", "parallel")),
    )(c1, p.reshape(4, 2, PACK_HALF, LANES), theirs)


def _grad_chip_exchange(q):
    def body(q_ref, out_ref, send_sems, recv_sems):
        x, y, c = _place()
        me = 2 * x + y
        chips = _other_chips(x, y)
        sends = [pltpu.make_async_remote_copy(src_ref=q_ref.at[2 * chip[0] + chip[1]], dst_ref=out_ref.at[me],
                                              send_sem=send_sems.at[j], recv_sem=recv_sems.at[j],
                                              device_id=(*chip, c), device_id_type=MESH) for j, chip in enumerate(chips)]
        for cp in sends:
            cp.start()
        for j, chip in enumerate(chips):
            slot = out_ref.at[2 * chip[0] + chip[1]]
            pltpu.make_async_remote_copy(src_ref=slot, dst_ref=slot, send_sem=send_sems.at[j], recv_sem=recv_sems.at[j],
                                         device_id=(x, y, c), device_id_type=MESH).wait_recv()
        for cp in sends:
            cp.wait_send()

    return pl.pallas_call(
        body, name="grad_chip_exchange", in_specs=[_ANY], out_specs=_ANY,
        out_shape=jax.ShapeDtypeStruct(q.shape, q.dtype),
        scratch_shapes=[pltpu.SemaphoreType.DMA((3,)), pltpu.SemaphoreType.DMA((3,))],
    )(q)


def _grad_add_chips(q, slots, me1):
    tr = GRAD_TR

    def body(me_ref, own_ref, s0, s1, s2, s3, o_ref):
        me = me_ref[0]
        t = [jnp.where(me == j, own_ref[...], s[...]) for j, s in enumerate((s0, s1, s2, s3))]
        o_ref[...] = ((t[0] + t[1]) + t[2]) + t[3]

    def slot_spec(j):
        return pl.BlockSpec((None, tr, LANES), lambda r, me: (jnp.where(me[0] == j, (j + 1) % 4, j), r, 0))

    grid_spec = pltpu.PrefetchScalarGridSpec(
        num_scalar_prefetch=1, grid=(PACK_HALF // tr,),
        in_specs=[pl.BlockSpec((None, tr, LANES), lambda r, me: (me[0], r, 0))] + [slot_spec(j) for j in range(4)],
        out_specs=pl.BlockSpec((tr, LANES), lambda r, me: (r, 0)))
    return pl.pallas_call(
        body, name="grad_add_chips", grid_spec=grid_spec, out_shape=jax.ShapeDtypeStruct(q.shape[1:], q.dtype),
        compiler_params=_cparams(("parallel",)),
    )(me1, q, slots, slots, slots, slots)


def _grad_core_gather(t):
    def body(t_ref, out_ref, send_sem, recv_sem):
        x, y, c = _place()
        give = pltpu.make_async_remote_copy(src_ref=t_ref, dst_ref=out_ref, send_sem=send_sem, recv_sem=recv_sem,
                                            device_id=(x, y, 1 - c), device_id_type=MESH)
        give.start()
        give.wait()

    return pl.pallas_call(
        body, name="grad_core_gather", in_specs=[_ANY], out_specs=_ANY,
        out_shape=jax.ShapeDtypeStruct(t.shape, t.dtype),
        scratch_shapes=[pltpu.SemaphoreType.DMA, pltpu.SemaphoreType.DMA],
    )(t)


def _small_allreduce(sp):
    def body(sp_ref, out_ref, gath_ref, send_sems, recv_sems):
        x, y, c = _place()
        me = 4 * x + 2 * y + c
        gath_ref[me] = sp_ref[...]
        peers = []
        for k in range(1, 8):
            px = 1 - x if k & 4 else x
            py = 1 - y if k & 2 else y
            pc = 1 - c if k & 1 else c
            peers.append((px, py, pc))
        sends = [pltpu.make_async_remote_copy(src_ref=sp_ref, dst_ref=gath_ref.at[me], send_sem=send_sems.at[k],
                                              recv_sem=recv_sems.at[k], device_id=peer, device_id_type=MESH)
                 for k, peer in enumerate(peers)]
        for cp in sends:
            cp.start()
        for k, (px, py, pc) in enumerate(peers):
            slot = gath_ref.at[4 * px + 2 * py + pc]
            pltpu.make_async_remote_copy(src_ref=slot, dst_ref=slot, send_sem=send_sems.at[k], recv_sem=recv_sems.at[k],
                                         device_id=(x, y, c), device_id_type=MESH).wait_recv()
        for cp in sends:
            cp.wait_send()
        tot = gath_ref[0]
        for d in range(1, 8):
            tot = tot + gath_ref[d]
        out_ref[...] = tot

    vm = pl.BlockSpec(memory_space=pltpu.VMEM)
    return pl.pallas_call(
        body, name="small_allreduce", in_specs=[vm], out_specs=vm, out_shape=jax.ShapeDtypeStruct(sp.shape, sp.dtype),
        scratch_shapes=[pltpu.VMEM((8,) + sp.shape, sp.dtype), pltpu.SemaphoreType.DMA((7,)), pltpu.SemaphoreType.DMA((7,))],
    )(sp)


def _adamw_update(w, gv, m, v):
    mn = ADAM_B1 * m + (1.0 - ADAM_B1) * gv
    vn = ADAM_B2 * v + (1.0 - ADAM_B2) * (gv * gv)
    m_hat = mn / (1.0 - ADAM_B1 ** ADAM_STEP)
    v_hat = vn / (1.0 - ADAM_B2 ** ADAM_STEP)
    return -ADAM_LR * (m_hat / (jnp.sqrt(v_hat) + ADAM_EPS) + ADAM_WD * w), mn, vn


def _adamw(w, g, m, v, name):
    rows = w.shape[0]

    def body(w_ref, g_ref, m_ref, v_ref, d_ref, mo_ref, vo_ref):
        d_ref[...], mo_ref[...], vo_ref[...] = _adamw_update(w_ref[...], g_ref[...], m_ref[...], v_ref[...])

    spec = pl.BlockSpec((rows, LANES), lambda r: (0, 0))
    shp = jax.ShapeDtypeStruct(w.shape, F32)
    return pl.pallas_call(
        body, name=name, grid=(1,), in_specs=[spec] * 4, out_specs=[spec] * 3, out_shape=[shp] * 3,
        compiler_params=_cparams(("arbitrary",)),
    )(w, g, m, v)


def _adamw_mats(w, g_mine, g_theirs, m, v, c1):
    tr = GRAD_TR
    nb = PACK_HALF // tr

    def body(c_ref, w_ref, a_ref, b_ref, m_ref, v_ref, g_ref, d_ref, mo_ref, vo_ref):
        gv = jnp.where(pl.program_id(0) == c_ref[0], a_ref[...], b_ref[...])
        g_ref[...] = gv
        d_ref[...], mo_ref[...], vo_ref[...] = _adamw_update(w_ref[...], gv, m_ref[...], v_ref[...])

    full = pl.BlockSpec((tr, LANES), lambda h, r, c: (h * nb + r, 0))
    half = pl.BlockSpec((tr, LANES), lambda h, r, c: (r, 0))
    grid_spec = pltpu.PrefetchScalarGridSpec(num_scalar_prefetch=1, grid=(2, nb),
                                             in_specs=[full, half, half, full, full], out_specs=[full] * 4)
    shp = jax.ShapeDtypeStruct(w.shape, F32)
    return pl.pallas_call(
        body, name="adamw_mats", grid_spec=grid_spec, out_shape=[shp] * 4,
        compiler_params=_cparams(("parallel", "parallel")),
    )(c1, w, g_mine, g_theirs, m, v)


MAT_NAMES = ("l0_w_in", "l0_w_q_b", "l0_w_kv_b", "l0_w_out", "l1_w_in", "l1_w_out")
VEC_NAMES = ("l0_pre_g", "l0_post_g", "l0_q_a_g", "l0_kv_a_g", "l1_pre_g", "l1_post_g", "l1_b_f")
WEIGHT_NAMES = ("l0_pre_g", "l0_post_g", "l0_w_in", "l0_q_a_g", "l0_w_q_b", "l0_kv_a_g", "l0_w_kv_b", "l0_w_out",
                "l1_pre_g", "l1_post_g", "l1_w_in", "l1_b_f", "l1_w_out")
MAT_SHARD = {"l0_w_in": (1024, 808), "l0_w_q_b": (384, 192), "l0_w_kv_b": (256, 256), "l0_w_out": (256, 1024),
             "l1_w_in": (1024, 1028), "l1_w_out": (256, 1024)}
ROW_SHARDED = ("l0_w_out", "l1_w_out")
VEC_LEN = {"l0_pre_g": 1024, "l0_post_g": 1024, "l0_q_a_g": 384, "l0_kv_a_g": 256, "l1_pre_g": 1024,
           "l1_post_g": 1024, "l1_b_f": 16}


def _mat_rows(n):
    r, c = MAT_SHARD[n]
    return r * c // LANES


def _pack_shards(shards):
    parts = [shards[n].reshape(_mat_rows(n), LANES) for n in MAT_NAMES]
    used = sum(_mat_rows(n) for n in MAT_NAMES)
    parts.append(jnp.zeros((PACK_ROWS - used, LANES), parts[0].dtype))
    return jnp.concatenate(parts, axis=0)


def _unpack_shards(pack):
    out, at = {}, 0
    for n in MAT_NAMES:
        out[n] = pack[..., at:at + _mat_rows(n), :].reshape(pack.shape[:-2] + MAT_SHARD[n])
        at += _mat_rows(n)
    return out


def _join_shards(n, s):
    if n in ROW_SHARDED:
        return s.reshape(4 * s.shape[1], s.shape[2])
    return s.transpose(1, 0, 2).reshape(s.shape[1], 4 * s.shape[2])


def _cut_shards(n, w):
    r, c = MAT_SHARD[n]
    if n in ROW_SHARDED:
        return w.reshape(4, r, c)
    return w.reshape(r, 4, c).transpose(1, 0, 2)


def _pack_vecs(vecs):
    parts = []
    for n in VEC_NAMES:
        v = vecs[n].reshape(-1)
        parts.append(jnp.pad(v, (0, VEC_ROWS * LANES - v.shape[0])).reshape(VEC_ROWS, LANES))
    return jnp.concatenate(parts, axis=0)


def _unpack_vecs(pack):
    return {n: pack[k * VEC_ROWS:(k + 1) * VEC_ROWS].reshape(-1)[:VEC_LEN[n]] for k, n in enumerate(VEC_NAMES)}


def kernel(x, positions, l0_pre_g, l0_post_g, l0_w_in, l0_q_a_g, l0_w_q_b, l0_kv_a_g, l0_w_kv_b, l0_w_out, l1_pre_g, l1_post_g, l1_w_in, l1_b_f, l1_w_out, loss_target, m_l0_pre_g, m_l0_post_g, m_l0_w_in, m_l0_q_a_g, m_l0_w_q_b, m_l0_kv_a_g, m_l0_w_kv_b, m_l0_w_out, m_l1_pre_g, m_l1_post_g, m_l1_w_in, m_l1_b_f, m_l1_w_out, v_l0_pre_g, v_l0_post_g, v_l0_w_in, v_l0_q_a_g, v_l0_w_q_b, v_l0_kv_a_g, v_l0_w_kv_b, v_l0_w_out, v_l1_pre_g, v_l1_post_g, v_l1_w_in, v_l1_b_f, v_l1_w_out):
    w = dict(l0_pre_g=l0_pre_g, l0_post_g=l0_post_g, l0_w_in=l0_w_in, l0_q_a_g=l0_q_a_g, l0_w_q_b=l0_w_q_b,
             l0_kv_a_g=l0_kv_a_g, l0_w_kv_b=l0_w_kv_b, l0_w_out=l0_w_out, l1_pre_g=l1_pre_g, l1_post_g=l1_post_g,
             l1_w_in=l1_w_in, l1_b_f=l1_b_f, l1_w_out=l1_w_out)
    m = dict(l0_pre_g=m_l0_pre_g, l0_post_g=m_l0_post_g, l0_w_in=m_l0_w_in, l0_q_a_g=m_l0_q_a_g, l0_w_q_b=m_l0_w_q_b,
             l0_kv_a_g=m_l0_kv_a_g, l0_w_kv_b=m_l0_w_kv_b, l0_w_out=m_l0_w_out, l1_pre_g=m_l1_pre_g,
             l1_post_g=m_l1_post_g, l1_w_in=m_l1_w_in, l1_b_f=m_l1_b_f, l1_w_out=m_l1_w_out)
    v = dict(l0_pre_g=v_l0_pre_g, l0_post_g=v_l0_post_g, l0_w_in=v_l0_w_in, l0_q_a_g=v_l0_q_a_g, l0_w_q_b=v_l0_w_q_b,
             l0_kv_a_g=v_l0_kv_a_g, l0_w_kv_b=v_l0_w_kv_b, l0_w_out=v_l0_w_out, l1_pre_g=v_l1_pre_g,
             l1_post_g=v_l1_post_g, l1_w_in=v_l1_w_in, l1_b_f=v_l1_b_f, l1_w_out=v_l1_w_out)

    cx, cy, cc = _place()
    me1 = jnp.reshape(2 * cx + cy, (1,)).astype(jnp.int32)
    c1 = jnp.reshape(cc, (1,)).astype(jnp.int32)
    w_pack = _pack_shards(w)
    w_bf = w_pack.astype(BF16)
    gathered = lax.dynamic_update_slice(_weight_gather(w_bf), w_bf[None], (2 * cx + cy, 0, 0))
    gathered = _unpack_shards(gathered)
    full = {n: _join_shards(n, gathered[n]) for n in MAT_NAMES}
    gains = {n: w[n].reshape(1, -1) for n in VEC_NAMES}

    lsum, dx0, grads = _local_step(
        x[0], positions[0], loss_target[0], gains, _pad_w0(full["l0_w_in"]), _pad_wq(full["l0_w_q_b"]),
        _pad_wkv(full["l0_w_kv_b"]), full["l0_w_out"], _pad_w1(full["l1_w_in"]), full["l1_w_out"])
    loss = lax.psum(0.5 * jnp.sum(lsum) / float(D_MODEL), ("x", "y", "c"))

    gfull = {"l0_w_in": _unpad_w0(grads["l0_w_in"]), "l0_w_q_b": _unpad_wq(grads["l0_w_q_b"]),
             "l0_w_kv_b": _unpad_wkv(grads["l0_w_kv_b"]), "l0_w_out": grads["l0_w_out"],
             "l1_w_in": grads["l1_w_in"][:, :ODD_IN_WIDTH], "l1_w_out": grads["l1_w_out"]}
    parts = [_cut_shards(n, gfull[n]).reshape(4, _mat_rows(n), LANES) for n in MAT_NAMES]
    used = sum(_mat_rows(n) for n in MAT_NAMES)
    parts.append(jnp.zeros((4, PACK_ROWS - used, LANES), F32))
    g_pack = jnp.concatenate(parts, axis=1)
    q_cores = _grad_add_cores(g_pack, _grad_core_exchange(g_pack), c1)
    g_mine = _grad_add_chips(q_cores, _grad_chip_exchange(q_cores), me1)
    g_theirs = _grad_core_gather(g_mine)

    g_small = _small_allreduce(_pack_vecs({n: grads[n] for n in VEC_NAMES}))

    g_shard, d_pack, m_pack, v_pack = _adamw_mats(w_pack, g_mine, g_theirs, _pack_shards(m), _pack_shards(v), c1)
    d_small, m_small, v_small = _adamw(_pack_vecs(w), g_small, _pack_vecs(m), _pack_vecs(v), "adamw_vecs")

    def unpack(mat_pack, vec_pack):
        out = dict(_unpack_shards(mat_pack))
        out.update(_unpack_vecs(vec_pack))
        return [out[n] for n in WEIGHT_NAMES]

    return (loss, dx0[None], *unpack(g_shard, g_small), *unpack(d_pack, d_small), *unpack(m_pack, m_small),
            *unpack(v_pack, v_small))
```

```python
import functools

import numpy as np
import jax
import jax.numpy as jnp
from jax import lax
from jax.experimental import pallas as pl
from jax.experimental.pallas import tpu as pltpu

F32 = jnp.float32
BF16 = jnp.bfloat16
MESH = pl.DeviceIdType.MESH

D_MODEL = 1024
RMS_EPS = 1e-6
ROPE_THETA = 10000.0
SB_WIDTH = 512
MLA_Q_LORA = 384
MLA_KV_LORA = 256
MLA_ROPE_DIM = 32
MLA_WIDTH = 512
FOX_WIDTH = 1024
FOX_HEADS = 16
EVEN_IN_WIDTH = 3232
ODD_IN_WIDTH = 4112

ADAM_LR = 0.001
ADAM_B1 = 0.9
ADAM_B2 = 0.999
ADAM_EPS = 1e-08
ADAM_WD = 0.01
ADAM_STEP = 10

LANES = 128
VMEM_LIMIT = 56 * 1024 * 1024

L0_PREP = 0
L0_PREP_W = 768
L0_SBG = 768
L0_MLG = 1280
L0_SBQ = 1792
L0_SBK = 2304
L0_SBV = 2816
L0_WIDTH = 3328
L1_Q = 0
L1_K = 1024
L1_V = 2048
L1_G = 3072
L1_F = 4096
L1_WIDTH = 4224

ATT_T = 256
ATT_RS = 64
NEG = -1e30

PACK_ROWS = 20480
PACK_HALF = PACK_ROWS // 2
VEC_ROWS = 8
SMALL_ROWS = 7 * VEC_ROWS


def _cparams(sem, **kw):
    return pltpu.CompilerParams(dimension_semantics=sem, vmem_limit_bytes=VMEM_LIMIT, **kw)


def _dot(a, b):
    return lax.dot_general(a, b, (((1,), (0,)), ((), ())), preferred_element_type=F32)


def _dot_nt(a, b):
    return lax.dot_general(a, b, (((1,), (1,)), ((), ())), preferred_element_type=F32)


def _dot_tn(a, b):
    return lax.dot_general(a, b, (((0,), (0,)), ((), ())), preferred_element_type=F32)


def _sigmoid(x):
    return 1.0 / (1.0 + jnp.exp(-x))


def _rstd(x):
    return lax.rsqrt(jnp.mean(x * x, axis=-1, keepdims=True) + RMS_EPS)


def _norm_bwd(x, g, dy):
    r = _rstd(x)
    xn = x * r
    dxn = dy * g
    dx = r * (dxn - xn * jnp.mean(dxn * xn, axis=-1, keepdims=True))
    return dx, dy * xn


def _split3(x):
    hi = x.astype(BF16)
    r1 = x - hi.astype(F32)
    mid = r1.astype(BF16)
    lo = (r1 - mid.astype(F32)).astype(BF16)
    return hi, mid, lo


def _pick(n, cands):
    for c in cands:
        if n % c == 0:
            return c
    raise ValueError(n)


def _norm_matmul(x, g, w, name):
    S, K = x.shape
    N = w.shape[1]
    tm = _pick(S, (512, 256))
    tn = _pick(N, (512, 384, 256, 128))

    def body(x_ref, g_ref, w_ref, o_ref, ht_ref, h_ref):
        @pl.when(pl.program_id(1) == 0)
        def _():
            xv = x_ref[...]
            h = (xv * _rstd(xv)) * g_ref[...]
            h_ref[...] = h.astype(BF16)
            ht_ref[...] = h.T.astype(BF16)
        o_ref[...] = _dot(h_ref[...], w_ref[...])

    return pl.pallas_call(
        body, name=name, grid=(S // tm, N // tn),
        in_specs=[pl.BlockSpec((tm, K), lambda i, j: (i, 0)),
                  pl.BlockSpec((1, K), lambda i, j: (0, 0)),
                  pl.BlockSpec((K, tn), lambda i, j: (0, j))],
        out_specs=[pl.BlockSpec((tm, tn), lambda i, j: (i, j)),
                   pl.BlockSpec((K, tm), lambda i, j: (0, i))],
        out_shape=[jax.ShapeDtypeStruct((S, N), F32), jax.ShapeDtypeStruct((K, S), BF16)],
        scratch_shapes=[pltpu.VMEM((tm, K), BF16)],
        compiler_params=_cparams(("parallel", "arbitrary")),
    )(x, g, w)


def _matmul_t(at, b, name):
    M, S = at.shape
    N = b.shape[1]
    tn = _pick(N, (512, 384, 256, 128))
    ts = _pick(S, (512, 256))

    def body(a_ref, b_ref, o_ref):
        @pl.when(pl.program_id(1) == 0)
        def _():
            o_ref[...] = jnp.zeros_like(o_ref)
        o_ref[...] += _dot(a_ref[...], b_ref[...].astype(BF16))

    return pl.pallas_call(
        body, name=name, grid=(N // tn, S // ts),
        in_specs=[pl.BlockSpec((M, ts), lambda j, k: (0, k)),
                  pl.BlockSpec((ts, tn), lambda j, k: (k, j))],
        out_specs=pl.BlockSpec((M, tn), lambda j, k: (0, j)),
        out_shape=jax.ShapeDtypeStruct((M, N), F32),
        compiler_params=_cparams(("parallel", "arbitrary")),
    )(at, b)


def _in_proj_bwd(dproj, w, x, g, dx_up, name):
    S, N = dproj.shape
    K = w.shape[0]
    tm = _pick(S, (512, 256))
    tk = N // 2 if (N // 2) % LANES == 0 and N % 2 == 0 else N // 3
    nk = N // tk

    def body(d_ref, w_ref, x_ref, g_ref, u_ref, dx_ref, dg_ref, acc_ref):
        i, k = pl.program_id(0), pl.program_id(1)

        @pl.when(k == 0)
        def _():
            acc_ref[...] = jnp.zeros_like(acc_ref)

        @pl.when((i == 0) & (k == 0))
        def _():
            dg_ref[...] = jnp.zeros_like(dg_ref)

        acc_ref[...] += _dot_nt(d_ref[...].astype(BF16), w_ref[...])

        @pl.when(k == nk - 1)
        def _():
            dx, dgrow = _norm_bwd(x_ref[...], g_ref[...], acc_ref[...])
            dx_ref[...] = u_ref[...] + dx
            dg_ref[...] += jnp.sum(dgrow, axis=0, keepdims=True)

    return pl.pallas_call(
        body, name=name, grid=(S // tm, nk),
        in_specs=[pl.BlockSpec((tm, tk), lambda i, k: (i, k)),
                  pl.BlockSpec((K, tk), lambda i, k: (0, k)),
                  pl.BlockSpec((tm, K), lambda i, k: (i, 0)),
                  pl.BlockSpec((1, K), lambda i, k: (0, 0)),
                  pl.BlockSpec((tm, K), lambda i, k: (i, 0))],
        out_specs=[pl.BlockSpec((tm, K), lambda i, k: (i, 0)),
                   pl.BlockSpec((1, K), lambda i, k: (0, 0))],
        out_shape=[jax.ShapeDtypeStruct((S, K), F32), jax.ShapeDtypeStruct((1, K), F32)],
        scratch_shapes=[pltpu.VMEM((tm, K), F32)],
        compiler_params=_cparams(("arbitrary", "arbitrary")),
    )(dproj, w, x, g, dx_up)


def _out_proj(og_a, og_b, blk_a, blk_b, w, x, g, target, name):
    S = x.shape[0]
    D = x.shape[1]
    tm = _pick(S, (512, 256))
    with_loss = target is not None

    def body(*refs):
        if with_loss:
            a_ref, b_ref, wa_ref, wb_ref, x_ref, g_ref, t_ref, y_ref, o_ref, l_ref = refs
        else:
            a_ref, b_ref, wa_ref, wb_ref, x_ref, g_ref, y_ref, o_ref = refs
        y = _dot(a_ref[...], wa_ref[...]) + _dot(b_ref[...], wb_ref[...])
        y_ref[...] = y
        xn = x_ref[...] + (y * _rstd(y)) * g_ref[...]
        if with_loss:
            @pl.when(pl.program_id(0) == 0)
            def _():
                l_ref[...] = jnp.zeros_like(l_ref)
            d = xn - t_ref[...]
            o_ref[...] = d / float(D)
            l_ref[...] += jnp.sum(d * d, axis=0, keepdims=True)
        else:
            o_ref[...] = xn

    row = lambda i: (i, 0)
    in_specs = [pl.BlockSpec((tm, 512), lambda i: (i, blk_a)),
                pl.BlockSpec((tm, 512), lambda i: (i, blk_b)),
                pl.BlockSpec((512, D), lambda i: (0, 0)),
                pl.BlockSpec((512, D), lambda i: (1, 0)),
                pl.BlockSpec((tm, D), row),
                pl.BlockSpec((1, D), lambda i: (0, 0))]
    out_specs = [pl.BlockSpec((tm, D), row), pl.BlockSpec((tm, D), row)]
    out_shape = [jax.ShapeDtypeStruct((S, D), F32), jax.ShapeDtypeStruct((S, D), F32)]
    args = [og_a, og_b, w, w, x, g]
    if with_loss:
        in_specs.append(pl.BlockSpec((tm, D), row))
        out_specs.append(pl.BlockSpec((1, D), lambda i: (0, 0)))
        out_shape.append(jax.ShapeDtypeStruct((1, D), F32))
        args.append(target)
    return pl.pallas_call(
        body, name=name, grid=(S // tm,), in_specs=in_specs, out_specs=out_specs, out_shape=out_shape,
        compiler_params=_cparams(("arbitrary",)),
    )(*args)


def _out_proj_bwd(dx_up, y, g, w, proj, gate_offs, o_a, o_b, oblk_a, oblk_b, name):
    S, D = y.shape
    tm = _pick(S, (256,))
    gblk = [off // 256 + c for off in gate_offs for c in range(2)]

    def body(u_ref, y_ref, g_ref, w_ref, g0, g1, g2, g3, oa_ref, ob_ref, dy_ref, do_ref, dgate_ref, dg_ref):
        @pl.when(pl.program_id(0) == 0)
        def _():
            dg_ref[...] = jnp.zeros_like(dg_ref)
        dy, dgrow = _norm_bwd(y_ref[...], g_ref[...], u_ref[...])
        dg_ref[...] += jnp.sum(dgrow, axis=0, keepdims=True)
        dyb = dy.astype(BF16)
        dy_ref[...] = dyb
        dog = _dot_nt(dyb, w_ref[...])
        gates = (g0, g1, g2, g3)
        for c in range(4):
            gt = gates[c][...]
            sg = _sigmoid(gt)
            o_ref = oa_ref if c < 2 else ob_ref
            ov = o_ref[:, (c % 2) * 256:(c % 2 + 1) * 256]
            dc = dog[:, c * 256:(c + 1) * 256]
            do_ref[:, c * 256:(c + 1) * 256] = dc * (gt * sg)
            dgate_ref[:, c * 256:(c + 1) * 256] = dc * ov * (sg * (1.0 + gt * (1.0 - sg)))

    row = lambda i: (i, 0)
    gspec = lambda c: pl.BlockSpec((tm, 256), lambda i: (i, gblk[c]))
    return pl.pallas_call(
        body, name=name, grid=(S // tm,),
        in_specs=[pl.BlockSpec((tm, D), row), pl.BlockSpec((tm, D), row), pl.BlockSpec((1, D), lambda i: (0, 0)),
                  pl.BlockSpec((D, D), lambda i: (0, 0)),
                  gspec(0), gspec(1), gspec(2), gspec(3),
                  pl.BlockSpec((tm, 512), lambda i: (i, oblk_a)),
                  pl.BlockSpec((tm, 512), lambda i: (i, oblk_b))],
        out_specs=[pl.BlockSpec((tm, D), row), pl.BlockSpec((tm, D), row), pl.BlockSpec((tm, D), row),
                   pl.BlockSpec((1, D), lambda i: (0, 0))],
        out_shape=[jax.ShapeDtypeStruct((S, D), BF16), jax.ShapeDtypeStruct((S, D), F32),
                   jax.ShapeDtypeStruct((S, D), F32), jax.ShapeDtypeStruct((1, D), F32)],
        compiler_params=_cparams(("arbitrary",)),
    )(dx_up, y, g, w, proj, proj, proj, proj, o_a, o_b)


def _rope_tables(pos, invf, name):
    S = pos.shape[0]
    tm = _pick(S, (512, 256))

    def body(p_ref, f_ref, c_ref, s1_ref, s2_ref):
        lane = lax.broadcasted_iota(jnp.int32, (1, LANES), 1)
        ang = p_ref[...].astype(F32) * f_ref[...]
        c, s = jnp.cos(ang), jnp.sin(ang)
        c_ref[...] = jnp.where((lane >= 64) & (lane < 96), c, 1.0)
        s1_ref[...] = jnp.where((lane >= 64) & (lane < 80), -s, 0.0)
        s2_ref[...] = jnp.where((lane >= 80) & (lane < 96), s, 0.0)

    spec = pl.BlockSpec((tm, LANES), lambda i: (i, 0))
    return pl.pallas_call(
        body, name=name, grid=(S // tm,),
        in_specs=[pl.BlockSpec((tm, 1), lambda i: (i, 0)), pl.BlockSpec((1, LANES), lambda i: (0, 0))],
        out_specs=[spec, spec, spec],
        out_shape=[jax.ShapeDtypeStruct((S, LANES), F32)] * 3,
        compiler_params=_cparams(("parallel",)),
    )(pos, invf)


def _rope(x, c, s1, s2):
    return x * c + pltpu.roll(x, LANES - 16, 1) * s1 + pltpu.roll(x, 16, 1) * s2


def _rope_t(d, c, s1, s2):
    return d * c + pltpu.roll(d * s1, 16, 1) + pltpu.roll(d * s2, LANES - 16, 1)


def _mla_prep(proj, gq, gkv, wq, wkv, cosT, s1T, s2T, name):
    S = proj.shape[0]
    tm = _pick(S, (256,))

    def body(p_ref, gq_ref, gkv_ref, wq_ref, wkv_ref, c_ref, s1_ref, s2_ref, q_ref, k_ref, v_ref, qn_ref, cn_ref):
        qa = p_ref[:, 0:384]
        ckv = p_ref[:, 384:640]
        kr = p_ref[:, 640:768]
        qn32 = (qa * _rstd(qa)) * gq_ref[...]
        cn32 = (ckv * _rstd(ckv)) * gkv_ref[...]
        qn = qn32.astype(BF16)
        cn = cn32.astype(BF16)
        qn_ref[...] = qn32.T.astype(BF16)
        cn_ref[...] = cn32.T.astype(BF16)
        qb = _dot(qn, wq_ref[...])
        kvb = _dot(cn, wkv_ref[...])
        c, s1, s2 = c_ref[...], s1_ref[...], s2_ref[...]
        krr = _rope(kr, c, s1, s2)
        for h in range(8):
            sl = slice(h * LANES, (h + 1) * LANES)
            q_ref[:, sl] = _rope(qb[:, sl], c, s1, s2)
            k_ref[:, sl] = kvb[:, sl] + krr
        v_ref[...] = kvb[:, 1024:1536]

    row = lambda i: (i, 0)
    fixed = lambda i: (0, 0)
    tspec = pl.BlockSpec((tm, LANES), row)
    return pl.pallas_call(
        body, name=name, grid=(S // tm,),
        in_specs=[pl.BlockSpec((tm, L0_PREP_W), lambda i: (i, L0_PREP // L0_PREP_W)),
                  pl.BlockSpec((1, 384), fixed), pl.BlockSpec((1, 256), fixed),
                  pl.BlockSpec((384, 1024), fixed), pl.BlockSpec((256, 1536), fixed), tspec, tspec, tspec],
        out_specs=[pl.BlockSpec((tm, 1024), row), pl.BlockSpec((tm, 1024), row), pl.BlockSpec((tm, 512), row),
                   pl.BlockSpec((384, tm), lambda i: (0, i)), pl.BlockSpec((256, tm), lambda i: (0, i))],
        out_shape=[jax.ShapeDtypeStruct((S, 1024), F32), jax.ShapeDtypeStruct((S, 1024), F32),
                   jax.ShapeDtypeStruct((S, 512), F32), jax.ShapeDtypeStruct((384, S), BF16),
                   jax.ShapeDtypeStruct((256, S), BF16)],
        compiler_params=_cparams(("parallel",)),
    )(proj, gq, gkv, wq, wkv, cosT, s1T, s2T)


def _mla_prep_bwd(dq, dk, dv, proj, gq, gkv, wq, wkv, cosT, s1T, s2T, name):
    S = proj.shape[0]
    tm = _pick(S, (256,))

    def body(dq_ref, dk_ref, dv_ref, p_ref, gq_ref, gkv_ref, wq_ref, wkv_ref, c_ref, s1_ref, s2_ref,
             dp_ref, dqb_ref, dkvb_ref, dgq_ref, dgkv_ref):
        @pl.when(pl.program_id(0) == 0)
        def _():
            dgq_ref[...] = jnp.zeros_like(dgq_ref)
            dgkv_ref[...] = jnp.zeros_like(dgkv_ref)
        c, s1, s2 = c_ref[...], s1_ref[...], s2_ref[...]
        lane = lax.broadcasted_iota(jnp.int32, (1, LANES), 1)
        dkr = jnp.zeros((tm, LANES), F32)
        for h in range(8):
            sl = slice(h * LANES, (h + 1) * LANES)
            dqb_ref[:, sl] = _rope_t(dq_ref[:, sl], c, s1, s2).astype(BF16)
            dkh = dk_ref[:, sl]
            dkvb_ref[:, sl] = dkh.astype(BF16)
            dkr = dkr + dkh
        dkvb_ref[:, 1024:1536] = dv_ref[...].astype(BF16)
        dkr = jnp.where((lane >= 64) & (lane < 96), _rope_t(dkr, c, s1, s2), 0.0)
        dqn = _dot_nt(dqb_ref[...], wq_ref[...])
        dcn = _dot_nt(dkvb_ref[...], wkv_ref[...])
        dqa, gq_row = _norm_bwd(p_ref[:, 0:384], gq_ref[...], dqn)
        dckv, gkv_row = _norm_bwd(p_ref[:, 384:640], gkv_ref[...], dcn)
        dp_ref[:, 0:384] = dqa
        dp_ref[:, 384:640] = dckv
        dp_ref[:, 640:768] = dkr
        dgq_ref[...] += jnp.sum(gq_row, axis=0, keepdims=True)
        dgkv_ref[...] += jnp.sum(gkv_row, axis=0, keepdims=True)

    row = lambda i: (i, 0)
    fixed = lambda i: (0, 0)
    tspec = pl.BlockSpec((tm, LANES), row)
    return pl.pallas_call(
        body, name=name, grid=(S // tm,),
        in_specs=[pl.BlockSpec((tm, 1024), row), pl.BlockSpec((tm, 1024), row), pl.BlockSpec((tm, 512), row),
                  pl.BlockSpec((tm, L0_PREP_W), lambda i: (i, L0_PREP // L0_PREP_W)),
                  pl.BlockSpec((1, 384), fixed), pl.BlockSpec((1, 256), fixed),
                  pl.BlockSpec((384, 1024), fixed), pl.BlockSpec((256, 1536), fixed), tspec, tspec, tspec],
        out_specs=[pl.BlockSpec((tm, L0_PREP_W), row), pl.BlockSpec((tm, 1024), row), pl.BlockSpec((tm, 1536), row),
                   pl.BlockSpec((1, 384), fixed), pl.BlockSpec((1, 256), fixed)],
        out_shape=[jax.ShapeDtypeStruct((S, L0_PREP_W), F32), jax.ShapeDtypeStruct((S, 1024), BF16),
                   jax.ShapeDtypeStruct((S, 1536), BF16), jax.ShapeDtypeStruct((1, 384), F32),
                   jax.ShapeDtypeStruct((1, 256), F32)],
        compiler_params=_cparams(("arbitrary",)),
    )(dq, dk, dv, proj, gq, gkv, wq, wkv, cosT, s1T, s2T)


def _fox_prep(proj, bf, name):
    S = proj.shape[0]
    tm = _pick(S, (256,))

    def body(f_ref, b_ref, c_ref, carry_ref):
        @pl.when(pl.program_id(0) == 0)
        def _():
            carry_ref[...] = jnp.zeros_like(carry_ref)
        u = f_ref[...] + b_ref[...]
        lf = jnp.minimum(u, 0.0) - jnp.log(1.0 + jnp.exp(-jnp.abs(u)))
        r = lax.broadcasted_iota(jnp.int32, (tm, tm), 0)
        cidx = lax.broadcasted_iota(jnp.int32, (tm, tm), 1)
        tri = (cidx <= r).astype(BF16)
        hi, mid, lo = _split3(lf)
        c = carry_ref[...] + (_dot(tri, hi) + _dot(tri, mid) + _dot(tri, lo))
        c_ref[...] = c
        carry_ref[...] = c[tm - 1:tm, :]

    return pl.pallas_call(
        body, name=name, grid=(S // tm,),
        in_specs=[pl.BlockSpec((tm, LANES), lambda i: (i, L1_F // LANES)), pl.BlockSpec((1, LANES), lambda i: (0, 0))],
        out_specs=pl.BlockSpec((tm, LANES), lambda i: (i, 0)),
        out_shape=jax.ShapeDtypeStruct((S, LANES), F32),
        scratch_shapes=[pltpu.VMEM((1, LANES), F32)],
        compiler_params=_cparams(("arbitrary",)),
    )(proj, bf)


def _fox_prep_bwd(dc, proj, bf, name):
    S = proj.shape[0]
    tm = _pick(S, (256,))
    nb = S // tm

    def body(dc_ref, f_ref, b_ref, df_ref, db_ref, carry_ref):
        @pl.when(pl.program_id(0) == 0)
        def _():
            carry_ref[...] = jnp.zeros_like(carry_ref)
            db_ref[...] = jnp.zeros_like(db_ref)
        r = lax.broadcasted_iota(jnp.int32, (tm, tm), 0)
        cidx = lax.broadcasted_iota(jnp.int32, (tm, tm), 1)
        tri = (cidx >= r).astype(BF16)
        hi, mid, lo = _split3(dc_ref[...])
        dlf = carry_ref[...] + (_dot(tri, hi) + _dot(tri, mid) + _dot(tri, lo))
        carry_ref[...] = dlf[0:1, :]
        u = f_ref[...] + b_ref[...]
        e = jnp.exp(-jnp.abs(u))
        sneg = jnp.where(u >= 0.0, e, 1.0) / (1.0 + e)
        lane = lax.broadcasted_iota(jnp.int32, (1, LANES), 1)
        df = jnp.where(lane < FOX_HEADS, dlf * sneg, 0.0)
        df_ref[...] = df
        db_ref[...] += jnp.sum(df, axis=0, keepdims=True)

    return pl.pallas_call(
        body, name=name, grid=(nb,),
        in_specs=[pl.BlockSpec((tm, LANES), lambda i: (nb - 1 - i, 0)),
                  pl.BlockSpec((tm, LANES), lambda i: (nb - 1 - i, L1_F // LANES)),
                  pl.BlockSpec((1, LANES), lambda i: (0, 0))],
        out_specs=[pl.BlockSpec((tm, LANES), lambda i: (nb - 1 - i, 0)), pl.BlockSpec((1, LANES), lambda i: (0, 0))],
        out_shape=[jax.ShapeDtypeStruct((S, LANES), F32), jax.ShapeDtypeStruct((1, LANES), F32)],
        scratch_shapes=[pltpu.VMEM((1, LANES), F32)],
        compiler_params=_cparams(("arbitrary",)),
    )(dc, proj, bf)


def _att_specs(kind, S, T):
    if kind == "sb":
        qo, ko, vo, go = L0_SBQ // LANES, L0_SBK // LANES, L0_SBV // LANES, L0_SBG // LANES
    elif kind == "fox":
        qo, ko, vo, go = L1_Q // LANES, L1_K // LANES, L1_V // LANES, L1_G // LANES
    else:
        go = L0_MLG // LANES
        return (pl.BlockSpec((T, 256), lambda p, i: (i, p)), pl.BlockSpec((S, 256), lambda p, i: (0, p)),
                pl.BlockSpec((S, LANES), lambda p, i: (0, p)), pl.BlockSpec((T, LANES), lambda p, i: (i, go + p)))
    return (pl.BlockSpec((T, LANES), lambda p, i: (i, qo + p)), pl.BlockSpec((S, LANES), lambda p, i: (0, ko + p)),
            pl.BlockSpec((S, LANES), lambda p, i: (0, vo + p)), pl.BlockSpec((T, LANES), lambda p, i: (i, go + p)))


def _head_q(kind, q_ref, m0, scale):
    if kind == "mla":
        return [q_ref[:, 0:LANES].astype(BF16), q_ref[:, LANES:2 * LANES].astype(BF16)]
    qv = q_ref[...] * scale
    return [jnp.where(m0, qv, 0.0).astype(BF16), jnp.where(m0, 0.0, qv).astype(BF16)]


def _head_k(kind, k_ref, start, T):
    if kind == "mla":
        return [k_ref[pl.ds(start, T), 0:LANES].astype(BF16), k_ref[pl.ds(start, T), LANES:2 * LANES].astype(BF16)]
    kb = k_ref[pl.ds(start, T), :].astype(BF16)
    return [kb, kb]


def _transpose_tiles(src, col_off, n_out, cw, group, name):
    S = src.shape[0]
    T = ATT_T
    first = col_off // (group * cw)

    def body(x_ref, o_ref):
        for u in range(group):
            o_ref[u] = x_ref[:, u * cw:(u + 1) * cw].T.astype(BF16)

    return pl.pallas_call(
        body, name=name, grid=(S // T, n_out // group),
        in_specs=[pl.BlockSpec((T, group * cw), lambda j, g: (j, first + g))],
        out_specs=pl.BlockSpec((group, None, cw, T), lambda j, g: (g, j, 0, 0)),
        out_shape=jax.ShapeDtypeStruct((n_out, S // T, cw, T), BF16),
        compiler_params=_cparams(("parallel", "parallel")),
    )(src)


def _softmax_fwd(kind, qkvg, c_col, S, npairs, name):
    T = ATT_T
    nq = S // T
    fox = kind == "fox"
    scale = (96 if kind == "mla" else 64) ** -0.5

    def body(*refs):
        if fox:
            q_ref, k_ref, vt_ref, g_ref, cc_ref, o_ref, og_ref, ogt_ref, st_ref, m_ref, acc_ref = refs
        else:
            q_ref, k_ref, vt_ref, g_ref, o_ref, og_ref, ogt_ref, st_ref, m_ref, acc_ref = refs
        i = pl.program_id(1)
        m0 = lax.broadcasted_iota(jnp.int32, (1, LANES), 1) < 64
        top = lax.broadcasted_iota(jnp.int32, (LANES, 1), 0) < 64
        key = lax.broadcasted_iota(jnp.int32, (T, LANES), 0)
        qrow = lax.broadcasted_iota(jnp.int32, (T, LANES), 1)
        qh = _head_q(kind, q_ref, m0, scale)
        m_ref[...] = jnp.full(m_ref.shape, NEG, F32)
        acc_ref[...] = jnp.zeros(acc_ref.shape, F32)
        chains = [(h, b) for h in range(2) for b in range(T // LANES)]

        def tile(j, masked):
            start = pl.multiple_of(j * T, T)
            vt = vt_ref[j]
            vth = [jnp.where(top, vt, 1.0).astype(BF16), jnp.where(top, 1.0, vt).astype(BF16)]
            kh = _head_k(kind, k_ref, start, T)
            zs = _split_blocks([_dot_nt(kh[h], qh[h]) for h in range(2)])
            ps = []
            alphas = []
            for (h, b), z in zip(chains, zs):
                lanes = slice(b * LANES, (b + 1) * LANES)
                if kind == "mla":
                    z = z * scale
                if fox:
                    z = z - cc_ref[h, pl.ds(start, T), :]
                if masked:
                    z = jnp.where(key <= qrow + b * LANES, z, NEG)
                m_prev = m_ref[h, :, lanes]
                m_new = jnp.maximum(m_prev, jnp.max(z, axis=0, keepdims=True))
                alphas.append(jnp.exp(m_prev - m_new))
                ps.append(jnp.exp(z - m_new).astype(BF16))
                m_ref[h, :, lanes] = m_new
            for h, (p, a) in enumerate(zip(_join_blocks(ps, T // LANES), _join_blocks(alphas, T // LANES))):
                acc_ref[h] = a * acc_ref[h] + _dot(vth[h], p)

        def step(j, carry):
            tile(j, False)
            return carry

        lax.fori_loop(0, i, step, 0)
        tile(i, True)
        acc = [acc_ref[0], acc_ref[1]]
        ot = jnp.concatenate([acc[0][0:64] / acc[0][64:128], acc[1][64:128] / acc[1][0:64]], axis=0)
        o = ot.T
        o_ref[...] = o
        gt = g_ref[...]
        og = o * (gt * _sigmoid(gt))
        og_ref[...] = og.astype(BF16)
        ogt_ref[...] = og.T.astype(BF16)
        st_ref[0] = m_ref[0] + jnp.log(acc[0][64:65])
        st_ref[1] = m_ref[1] + jnp.log(acc[1][0:1])

    qs, ks, _, gs = _att_specs(kind, S, T)
    in_specs = [qs, ks, pl.BlockSpec((None, nq, LANES, T), lambda p, i: (p, 0, 0, 0)), gs]
    args = list(qkvg)
    if fox:
        in_specs += [pl.BlockSpec((2, S, 1), lambda p, i: (p, 0, 0))]
        args += [c_col]
    W = npairs * LANES
    return pl.pallas_call(
        body, name=name, grid=(npairs, nq), in_specs=in_specs,
        out_specs=[pl.BlockSpec((T, LANES), lambda p, i: (i, p)), pl.BlockSpec((T, LANES), lambda p, i: (i, p)),
                   pl.BlockSpec((LANES, T), lambda p, i: (p, i)),
                   pl.BlockSpec((2, None, 1, T), lambda p, i: (p, i, 0, 0))],
        out_shape=[jax.ShapeDtypeStruct((S, W), F32), jax.ShapeDtypeStruct((S, W), BF16),
                   jax.ShapeDtypeStruct((W, S), BF16),
                   jax.ShapeDtypeStruct((2 * npairs, nq, 1, T), F32)],
        scratch_shapes=[pltpu.VMEM((2, 1, T), F32), pltpu.VMEM((2, LANES, T), F32)],
        compiler_params=_cparams(("parallel", "parallel")),
    )(*args)


def _softmax_bwd(kind, qkv, do, do_off, o, lse, c_row, S, npairs, name):
    T = ATT_T
    nq = S // T
    fox = kind == "fox"
    mla = kind == "mla"
    scale = (96 if mla else 64) ** -0.5
    kw = 256 if mla else LANES

    def body(*refs):
        if fox:
            q_ref, k_ref, v_ref, do_ref, o_ref, st_ref, cr_ref, dq_ref, dk_ref, dv_ref, dc_ref, dcq_ref = refs
        else:
            q_ref, k_ref, v_ref, do_ref, o_ref, st_ref, dq_ref, dk_ref, dv_ref = refs
        i = pl.program_id(1)

        @pl.when(i == 0)
        def _():
            dk_ref[...] = jnp.zeros_like(dk_ref)
            dv_ref[...] = jnp.zeros_like(dv_ref)
            if fox:
                dc_ref[...] = jnp.zeros_like(dc_ref)

        m0 = lax.broadcasted_iota(jnp.int32, (1, LANES), 1) < 64
        causal = lax.broadcasted_iota(jnp.int32, (T, T), 1) <= lax.broadcasted_iota(jnp.int32, (T, T), 0)
        qh = _head_q(kind, q_ref, m0, scale)
        dov = do_ref[...]
        prod = dov * o_ref[...]
        dd = [jnp.sum(jnp.where(m0, prod, 0.0), axis=1, keepdims=True),
              jnp.sum(jnp.where(m0, 0.0, prod), axis=1, keepdims=True)]
        doh = [jnp.where(m0, dov, 0.0).astype(BF16), jnp.where(m0, 0.0, dov).astype(BF16)]
        lse_h = [st_ref[0], st_ref[1]]

        def tile(j, carry, masked):
            start = pl.multiple_of(j * T, T)
            vb = v_ref[pl.ds(start, T), :].astype(BF16)
            kh = _head_k(kind, k_ref, start, T)
            dqs = []
            dkc = []
            dvc = jnp.zeros((T, LANES), F32)
            for h in range(2):
                z = _dot_nt(qh[h], kh[h])
                if mla:
                    z = z * scale
                if fox:
                    z = z - cr_ref[h, pl.ds(j, 1), :]
                if masked:
                    z = jnp.where(causal, z, NEG)
                p = jnp.exp(z - lse_h[h])
                ds = p * (_dot_nt(doh[h], vb) - dd[h])
                dsb = ds.astype(BF16)
                dqh = carry[h][0] + _dot(dsb, kh[h])
                dkc.append(_dot_tn(dsb, qh[h]))
                dvc = dvc + _dot_tn(p.astype(BF16), doh[h])
                if fox:
                    dc_ref[h, pl.ds(j, 1), :] += -jnp.sum(ds, axis=0, keepdims=True)
                    dqs.append((dqh, carry[h][1] + jnp.sum(ds, axis=1, keepdims=True)))
                else:
                    dqs.append((dqh,))
            if mla:
                dk_ref[pl.ds(start, T), 0:LANES] += dkc[0] * scale
                dk_ref[pl.ds(start, T), LANES:2 * LANES] += dkc[1] * scale
            else:
                dk_ref[pl.ds(start, T), :] += dkc[0] + dkc[1]
            dv_ref[pl.ds(start, T), :] += dvc
            return tuple(dqs)

        one = (jnp.zeros((T, LANES), F32), jnp.zeros((T, 1), F32)) if fox else (jnp.zeros((T, LANES), F32),)
        carry = lax.fori_loop(0, i, lambda j, c: tile(j, c, False), (one, one))
        carry = tile(i, carry, True)
        if mla:
            dq_ref[:, 0:LANES] = carry[0][0] * scale
            dq_ref[:, LANES:2 * LANES] = carry[1][0] * scale
        else:
            dq_ref[...] = jnp.where(m0, carry[0][0], carry[1][0]) * scale
        if fox:
            dcq_ref[0] = carry[0][1]
            dcq_ref[1] = carry[1][1]

    qs, ks, vs, _ = _att_specs(kind, S, T)
    in_specs = [qs, ks, vs,
                pl.BlockSpec((T, LANES), lambda p, i: (i, do_off + p)),
                pl.BlockSpec((T, LANES), lambda p, i: (i, p)),
                pl.BlockSpec((2, T, 1), lambda p, i: (p, i, 0))]
    args = list(qkv) + [do, o, lse]
    W = npairs * LANES
    out_specs = [pl.BlockSpec((T, kw), lambda p, i: (i, p)), pl.BlockSpec((S, kw), lambda p, i: (0, p)),
                 pl.BlockSpec((S, LANES), lambda p, i: (0, p))]
    out_shape = [jax.ShapeDtypeStruct((S, npairs * kw), F32), jax.ShapeDtypeStruct((S, npairs * kw), F32),
                 jax.ShapeDtypeStruct((S, W), F32)]
    if fox:
        in_specs += [pl.BlockSpec((2, nq, T), lambda p, i: (p, 0, 0))]
        args += [c_row]
        out_specs +=[pl.BlockSpec((2, nq, T), lambda p, i: (p, 0, 0)), pl.BlockSpec((2, T, 1), lambda p, i: (p, i, 0))]
        out_shape += [jax.ShapeDtypeStruct((2 * npairs, nq, T), F32), jax.ShapeDtypeStruct((2 * npairs, S, 1), F32)]
    return pl.pallas_call(
        body, name=name, grid=(npairs, nq), in_specs=in_specs, out_specs=out_specs, out_shape=out_shape,
        compiler_params=_cparams(("parallel", "arbitrary")),
    )(*args)


def _softplus_parts(z):
    sp = jnp.maximum(z, 0.0) + jnp.log(1.0 + jnp.exp(-jnp.abs(z)))
    return -sp, z - sp


def _split2(x):
    hi = x.astype(BF16)
    return hi, (x - hi.astype(F32)).astype(BF16)


def _sb_fwd(proj, S, npairs, name):
    T = ATT_T
    nq = S // T
    scale = 64 ** -0.5

    def body(q_ref, k_ref, v_ref, g_ref, o_ref, og_ref, st_ref):
        i = pl.program_id(1)
        m0 = lax.broadcasted_iota(jnp.int32, (1, LANES), 1) < 64
        r = lax.broadcasted_iota(jnp.int32, (T, T), 0)
        c = lax.broadcasted_iota(jnp.int32, (T, T), 1)
        before = c < r
        after = (r > c).astype(BF16)
        qh = _head_q("sb", q_ref, m0, scale)

        def tile(j, carry, masked):
            start = pl.multiple_of(j * T, T)
            vb = v_ref[pl.ds(start, T), :].astype(BF16)
            kb = k_ref[pl.ds(start, T), :].astype(BF16)
            out = []
            for h in range(2):
                rem, acc = carry[h]
                z = _dot_nt(qh[h], kb)
                lk, la = _softplus_parts(z)
                if masked:
                    lk = jnp.where(before, lk, 0.0)
                hi, lo = _split2(lk)
                lr = rem + (_dot(hi, after) + _dot(lo, after))
                w = jnp.exp(la + lr)
                if masked:
                    w = jnp.where(before, w, 0.0)
                out.append((rem + jnp.sum(lk, axis=1, keepdims=True), acc + _dot(w.astype(BF16), vb)))
            return tuple(out)

        init = tuple((jnp.zeros((T, 1), F32), jnp.zeros((T, LANES), F32)) for _ in range(2))
        carry = tile(i, init, True)
        carry = lax.fori_loop(0, i, lambda jj, cr: tile(i - 1 - jj, cr, False), carry)
        o = jnp.where(m0, carry[0][1], carry[1][1])
        o_ref[...] = o
        gt = g_ref[...]
        og_ref[...] = (o * (gt * _sigmoid(gt))).astype(BF16)
        for h in range(2):
            st_ref[h] = carry[h][0]

    W = npairs * LANES
    return pl.pallas_call(
        body, name=name, grid=(npairs, nq), in_specs=list(_att_specs("sb", S, T)),
        out_specs=[pl.BlockSpec((T, LANES), lambda p, i: (i, p)), pl.BlockSpec((T, LANES), lambda p, i: (i, p)),
                   pl.BlockSpec((2, T, 1), lambda p, i: (p, i, 0))],
        out_shape=[jax.ShapeDtypeStruct((S, W), F32), jax.ShapeDtypeStruct((S, W), BF16),
                   jax.ShapeDtypeStruct((2 * npairs, S, 1), F32)],
        compiler_params=_cparams(("parallel", "parallel")),
    )(proj, proj, proj, proj)


def _sb_bwd(proj, do, tot, S, npairs, name):
    T = ATT_T
    nq = S // T
    scale = 64 ** -0.5

    def body(q_ref, k_ref, v_ref, do_ref, st_ref, dq_ref, dk_ref, dv_ref):
        i = pl.program_id(1)

        @pl.when(i == 0)
        def _():
            dk_ref[...] = jnp.zeros_like(dk_ref)
            dv_ref[...] = jnp.zeros_like(dv_ref)

        m0 = lax.broadcasted_iota(jnp.int32, (1, LANES), 1) < 64
        r = lax.broadcasted_iota(jnp.int32, (T, T), 0)
        c = lax.broadcasted_iota(jnp.int32, (T, T), 1)
        before = c < r
        upto = (r <= c).astype(BF16)
        left = (r < c).astype(BF16)
        qh = _head_q("sb", q_ref, m0, scale)
        dov = do_ref[...]
        doh = [jnp.where(m0, dov, 0.0).astype(BF16), jnp.where(m0, 0.0, dov).astype(BF16)]
        tot_h = [st_ref[0], st_ref[1]]

        def tile(j, carry, masked):
            start = pl.multiple_of(j * T, T)
            vb = v_ref[pl.ds(start, T), :].astype(BF16)
            kb = k_ref[pl.ds(start, T), :].astype(BF16)
            out = []
            dkc = jnp.zeros((T, LANES), F32)
            dvc = jnp.zeros((T, LANES), F32)
            for h in range(2):
                pre, gpre, dq = carry[h]
                z = _dot_nt(qh[h], kb)
                lk, la = _softplus_parts(z)
                if masked:
                    lk = jnp.where(before, lk, 0.0)
                hi, lo = _split2(lk)
                lr = (tot_h[h] - pre) - (_dot(hi, upto) + _dot(lo, upto))
                w = jnp.exp(la + lr)
                if masked:
                    w = jnp.where(before, w, 0.0)
                g = _dot_nt(doh[h], vb) * w
                gfull = gpre + _dot(g.astype(BF16), left)
                dz = g - (g + gfull) * jnp.exp(la)
                if masked:
                    dz = jnp.where(before, dz, 0.0)
                dzb = dz.astype(BF16)
                dkc = dkc + _dot_tn(dzb, qh[h])
                dvc = dvc + _dot_tn(w.astype(BF16), doh[h])
                out.append((pre + jnp.sum(lk, axis=1, keepdims=True), gpre + jnp.sum(g, axis=1, keepdims=True),
                            dq + _dot(dzb, kb)))
            dk_ref[pl.ds(start, T), :] += dkc
            dv_ref[pl.ds(start, T), :] += dvc
            return tuple(out)

        init = tuple((jnp.zeros((T, 1), F32), jnp.zeros((T, 1), F32), jnp.zeros((T, LANES), F32)) for _ in range(2))
        carry = lax.fori_loop(0, i, lambda j, cr: tile(j, cr, False), init)
        carry = tile(i, carry, True)
        dq_ref[...] = jnp.where(m0, carry[0][2], carry[1][2]) * scale

    qs, ks, vs, _ = _att_specs("sb", S, T)
    W = npairs * LANES
    return pl.pallas_call(
        body, name=name, grid=(npairs, nq),
        in_specs=[qs, ks, vs, pl.BlockSpec((T, LANES), lambda p, i: (i, p)),
                  pl.BlockSpec((2, T, 1), lambda p, i: (p, i, 0))],
        out_specs=[pl.BlockSpec((T, LANES), lambda p, i: (i, p)), pl.BlockSpec((S, LANES), lambda p, i: (0, p)),
                   pl.BlockSpec((S, LANES), lambda p, i: (0, p))],
        out_shape=[jax.ShapeDtypeStruct((S, W), F32)] * 3,
        compiler_params=_cparams(("parallel", "arbitrary")),
    )(proj, proj, proj, do, tot)


def _split_blocks(per_head):
    return [x[:, b * LANES:(b + 1) * LANES] for x in per_head for b in range(x.shape[1] // LANES)]


def _join_blocks(per_block, nb):
    return [jnp.concatenate(per_block[h * nb:(h + 1) * nb], axis=1) for h in range(len(per_block) // nb)]


def _row_of(col):
    return jnp.broadcast_to(col, (col.shape[0], LANES)).T[0:1]


def _softmax_bwd_t(kind, q, k, kt, v, do, do_off, o, lse, c_col, S, npairs, name):
    T = ATT_T
    nq = S // T
    nb = T // LANES
    fox = kind == "fox"
    mla = kind == "mla"
    scale = (96 if mla else 64) ** -0.5
    kw = 256 if mla else LANES

    def body(*refs):
        if fox:
            (q_ref, k_ref, kt_ref, v_ref, do_ref, o_ref, st_ref, cc_ref,
             dq_ref, dk_ref, dv_ref, dck_ref, dcq_ref, dqt_ref, rs_ref, dkx_ref) = refs
        else:
            q_ref, k_ref, kt_ref, v_ref, do_ref, o_ref, st_ref, dq_ref, dk_ref, dv_ref, dqt_ref = refs
        i = pl.program_id(1)

        @pl.when(i == 0)
        def _():
            dv_ref[...] = jnp.zeros_like(dv_ref)
            if fox:
                dkx_ref[...] = jnp.zeros_like(dkx_ref)
            else:
                dk_ref[...] = jnp.zeros_like(dk_ref)

        m0 = lax.broadcasted_iota(jnp.int32, (1, LANES), 1) < 64
        top = lax.broadcasted_iota(jnp.int32, (LANES, 1), 0) < 64
        key = lax.broadcasted_iota(jnp.int32, (T, LANES), 0)
        qrow = lax.broadcasted_iota(jnp.int32, (T, LANES), 1)
        qh = _head_q(kind, q_ref, m0, scale)
        if fox:
            qv = q_ref[...] * scale
            qk = [jnp.where(m0, qv, 1.0).astype(BF16), jnp.where(m0, 1.0, qv).astype(BF16)]
        else:
            qk = qh
        dov = do_ref[...]
        prod = dov * o_ref[...]
        dd = [_row_of(jnp.sum(jnp.where(m0, prod, 0.0), axis=1, keepdims=True)),
              _row_of(jnp.sum(jnp.where(m0, 0.0, prod), axis=1, keepdims=True))]
        doh = [jnp.where(m0, dov, 0.0).astype(BF16), jnp.where(m0, 0.0, dov).astype(BF16)]
        lse = [st_ref[0], st_ref[1]]
        dqt_ref[...] = jnp.zeros_like(dqt_ref)
        if fox:
            rs_ref[...] = jnp.zeros_like(rs_ref)
        chains = [(h, b) for h in range(2) for b in range(nb)]

        def tile(j, masked):
            start = pl.multiple_of(j * T, T)
            vb = v_ref[pl.ds(start, T), :].astype(BF16)
            kh = _head_k(kind, k_ref, start, T)
            kt = kt_ref[j]
            zs = _split_blocks([_dot_nt(kh[h], qh[h]) for h in range(2)])
            dps = _split_blocks([_dot_nt(vb, doh[h]) for h in range(2)])
            ps, dss = [], []
            for (h, b), z, dp in zip(chains, zs, dps):
                lanes = slice(b * LANES, (b + 1) * LANES)
                if mla:
                    z = z * scale
                if fox:
                    z = z - cc_ref[h, pl.ds(start, T), :]
                if masked:
                    z = jnp.where(key <= qrow + b * LANES, z, NEG)
                p = jnp.exp(z - lse[h][:, lanes])
                ds = p * (dp - dd[h][:, lanes])
                dsb = ds.astype(BF16)
                if fox:
                    rs_ref[h, :, lanes] += jnp.sum(dsb.astype(F32), axis=0, keepdims=True)
                ps.append(p.astype(BF16))
                dss.append(dsb)
            dvc = None
            for h in range(2):
                dsh = jnp.concatenate(dss[h * nb:(h + 1) * nb], axis=1)
                ph = jnp.concatenate(ps[h * nb:(h + 1) * nb], axis=1)
                dkh = _dot(dsh, qk[h])
                dvh = _dot(ph, doh[h])
                dvc = dvh if dvc is None else dvc + dvh
                kth = kt[h * LANES:(h + 1) * LANES] if mla else kt
                dqt_ref[h] += _dot(kth, dsh)
                if fox:
                    dkx_ref[h, pl.ds(start, T), :] += dkh
                elif mla:
                    dk_ref[pl.ds(start, T), h * LANES:(h + 1) * LANES] += dkh * scale
                else:
                    dk_ref[pl.ds(start, T), :] += dkh
            dv_ref[pl.ds(start, T), :] += dvc

        def step(j, carry):
            tile(j, False)
            return carry

        lax.fori_loop(0, i, step, 0)
        tile(i, True)
        if mla:
            dq_ref[:, 0:LANES] = dqt_ref[0].T * scale
            dq_ref[:, LANES:2 * LANES] = dqt_ref[1].T * scale
        else:
            dq_ref[...] = jnp.where(top, dqt_ref[0], dqt_ref[1]).T * scale
        if fox:
            dcq_ref[0] = rs_ref[0]
            dcq_ref[1] = rs_ref[1]

            @pl.when(i == nq - 1)
            def _():
                dk_ref[...] = jnp.where(m0, dkx_ref[0], dkx_ref[1])
                dck_ref[0] = dkx_ref[0][:, 64:65]
                dck_ref[1] = dkx_ref[1][:, 0:1]

    qs, ks, vs, _ = _att_specs(kind, S, T)
    stat = pl.BlockSpec((2, None, 1, T), lambda p, i: (p, i, 0, 0))
    in_specs = [qs, ks, pl.BlockSpec((None, nq, kw, T), lambda p, i: (p, 0, 0, 0)), vs,
                pl.BlockSpec((T, LANES), lambda p, i: (i, do_off + p)),
                pl.BlockSpec((T, LANES), lambda p, i: (i, p)), stat]
    args = [q, k, kt, v, do, o, lse]
    W = npairs * LANES
    out_specs = [pl.BlockSpec((T, kw), lambda p, i: (i, p)), pl.BlockSpec((S, kw), lambda p, i: (0, p)),
                 pl.BlockSpec((S, LANES), lambda p, i: (0, p))]
    out_shape = [jax.ShapeDtypeStruct((S, npairs * kw), F32), jax.ShapeDtypeStruct((S, npairs * kw), F32),
                 jax.ShapeDtypeStruct((S, W), F32)]
    scratch = [pltpu.VMEM((2, LANES, T), F32)]
    if fox:
        in_specs.append(pl.BlockSpec((2, S, 1), lambda p, i: (p, 0, 0)))
        args.append(c_col)
        out_specs += [pl.BlockSpec((2, S, 1), lambda p, i: (p, 0, 0)), stat]
        out_shape += [jax.ShapeDtypeStruct((2 * npairs, S, 1), F32), jax.ShapeDtypeStruct((2 * npairs, nq, 1, T), F32)]
        scratch += [pltpu.VMEM((2, 1, T), F32), pltpu.VMEM((2, S, LANES), F32)]
    return pl.pallas_call(
        body, name=name, grid=(npairs, nq), in_specs=in_specs, out_specs=out_specs, out_shape=out_shape,
        scratch_shapes=scratch, compiler_params=_cparams(("parallel", "arbitrary")),
    )(*args)


def _sb_fwd_t(proj, vt, S, npairs, name):
    T = ATT_T
    nq = S // T
    nb = T // LANES
    scale = 64 ** -0.5

    def body(q_ref, k_ref, vt_ref, g_ref, o_ref, og_ref, ogt_ref, st_ref, rem_ref, acc_ref):
        i = pl.program_id(1)
        m0 = lax.broadcasted_iota(jnp.int32, (1, LANES), 1) < 64
        top = lax.broadcasted_iota(jnp.int32, (LANES, 1), 0) < 64
        key = lax.broadcasted_iota(jnp.int32, (T, LANES), 0)
        qrow = lax.broadcasted_iota(jnp.int32, (T, LANES), 1)
        r = lax.broadcasted_iota(jnp.int32, (T, T), 0)
        c = lax.broadcasted_iota(jnp.int32, (T, T), 1)
        after = (c > r).astype(BF16)
        qh = _head_q("sb", q_ref, m0, scale)
        rem_ref[...] = jnp.zeros_like(rem_ref)
        acc_ref[...] = jnp.zeros_like(acc_ref)
        chains = [(h, b) for h in range(2) for b in range(nb)]

        def tile(j, masked):
            start = pl.multiple_of(j * T, T)
            vtb = vt_ref[j]
            kb = k_ref[pl.ds(start, T), :].astype(BF16)
            zs = _split_blocks([_dot_nt(kb, qh[h]) for h in range(2)])
            lks, las, his, los = [], [], [], []
            for (h, b), z in zip(chains, zs):
                lk, la = _softplus_parts(z)
                if masked:
                    lk = jnp.where(key < qrow + b * LANES, lk, 0.0)
                hi, lo = _split2(lk)
                lks.append(lk)
                las.append(la)
                his.append(hi)
                los.append(lo)
            rcs = _split_blocks([_dot(after, hi) + _dot(after, lo)
                                 for hi, lo in zip(_join_blocks(his, nb), _join_blocks(los, nb))])
            ws = []
            for (h, b), lk, la, rc in zip(chains, lks, las, rcs):
                lanes = slice(b * LANES, (b + 1) * LANES)
                w = jnp.exp(la + (rem_ref[h, :, lanes] + rc))
                if masked:
                    w = jnp.where(key < qrow + b * LANES, w, 0.0)
                ws.append(w.astype(BF16))
                rem_ref[h, :, lanes] += jnp.sum(lk, axis=0, keepdims=True)
            for h, w in enumerate(_join_blocks(ws, nb)):
                acc_ref[h] += _dot(vtb, w)

        def step(jj, carry):
            tile(i - 1 - jj, False)
            return carry

        tile(i, True)
        lax.fori_loop(0, i, step, 0)
        o = jnp.where(top, acc_ref[0], acc_ref[1]).T
        o_ref[...] = o
        gt = g_ref[...]
        og = o * (gt * _sigmoid(gt))
        og_ref[...] = og.astype(BF16)
        ogt_ref[...] = og.T.astype(BF16)
        st_ref[0] = rem_ref[0]
        st_ref[1] = rem_ref[1]

    qs, ks, _, gs = _att_specs("sb", S, T)
    W = npairs * LANES
    return pl.pallas_call(
        body, name=name, grid=(npairs, nq),
        in_specs=[qs, ks, pl.BlockSpec((None, nq, LANES, T), lambda p, i: (p, 0, 0, 0)), gs],
        out_specs=[pl.BlockSpec((T, LANES), lambda p, i: (i, p)), pl.BlockSpec((T, LANES), lambda p, i: (i, p)),
                   pl.BlockSpec((LANES, T), lambda p, i: (p, i)),
                   pl.BlockSpec((2, None, 1, T), lambda p, i: (p, i, 0, 0))],
        out_shape=[jax.ShapeDtypeStruct((S, W), F32), jax.ShapeDtypeStruct((S, W), BF16),
                   jax.ShapeDtypeStruct((W, S), BF16),
                   jax.ShapeDtypeStruct((2 * npairs, nq, 1, T), F32)],
        scratch_shapes=[pltpu.VMEM((2, 1, T), F32), pltpu.VMEM((2, LANES, T), F32)],
        compiler_params=_cparams(("parallel", "parallel")),
    )(proj, proj, vt, proj)


def _sb_bwd_t(proj, kt, do, tot, S, npairs, name):
    T = ATT_T
    nq = S // T
    nb = T // LANES
    scale = 64 ** -0.5

    def body(q_ref, k_ref, kt_ref, v_ref, do_ref, st_ref, dq_ref, dk_ref, dv_ref, dqt_ref, pre_ref, gpre_ref):
        i = pl.program_id(1)

        @pl.when(i == 0)
        def _():
            dk_ref[...] = jnp.zeros_like(dk_ref)
            dv_ref[...] = jnp.zeros_like(dv_ref)

        m0 = lax.broadcasted_iota(jnp.int32, (1, LANES), 1) < 64
        top = lax.broadcasted_iota(jnp.int32, (LANES, 1), 0) < 64
        key = lax.broadcasted_iota(jnp.int32, (T, LANES), 0)
        qrow = lax.broadcasted_iota(jnp.int32, (T, LANES), 1)
        r = lax.broadcasted_iota(jnp.int32, (T, T), 0)
        c = lax.broadcasted_iota(jnp.int32, (T, T), 1)
        upto = (c <= r).astype(BF16)
        left = (c < r).astype(BF16)
        qh = _head_q("sb", q_ref, m0, scale)
        dov = do_ref[...]
        doh = [jnp.where(m0, dov, 0.0).astype(BF16), jnp.where(m0, 0.0, dov).astype(BF16)]
        tot_h = [st_ref[0], st_ref[1]]
        dqt_ref[...] = jnp.zeros_like(dqt_ref)
        pre_ref[...] = jnp.zeros_like(pre_ref)
        gpre_ref[...] = jnp.zeros_like(gpre_ref)
        chains = [(h, b) for h in range(2) for b in range(nb)]

        def tile(j, masked):
            start = pl.multiple_of(j * T, T)
            vb = v_ref[pl.ds(start, T), :].astype(BF16)
            kb = k_ref[pl.ds(start, T), :].astype(BF16)
            kt = kt_ref[j]
            zs = _split_blocks([_dot_nt(kb, qh[h]) for h in range(2)])
            dws = _split_blocks([_dot_nt(vb, doh[h]) for h in range(2)])
            lks, las, his, los = [], [], [], []
            for (h, b), z in zip(chains, zs):
                lk, la = _softplus_parts(z)
                if masked:
                    lk = jnp.where(key < qrow + b * LANES, lk, 0.0)
                hi, lo = _split2(lk)
                lks.append(lk)
                las.append(la)
                his.append(hi)
                los.append(lo)
            pcs = _split_blocks([_dot(upto, hi) + _dot(upto, lo)
                                 for hi, lo in zip(_join_blocks(his, nb), _join_blocks(los, nb))])
            ws, gs = [], []
            for (h, b), lk, la, pc, dw in zip(chains, lks, las, pcs, dws):
                lanes = slice(b * LANES, (b + 1) * LANES)
                w = jnp.exp(la + ((tot_h[h][:, lanes] - pre_ref[h, :, lanes]) - pc))
                if masked:
                    w = jnp.where(key < qrow + b * LANES, w, 0.0)
                ws.append(w.astype(BF16))
                gs.append(dw * w)
                pre_ref[h, :, lanes] += jnp.sum(lk, axis=0, keepdims=True)
            gcs = _split_blocks([_dot(left, g) for g in _join_blocks([g.astype(BF16) for g in gs], nb)])
            dzs = []
            for (h, b), la, g, gc in zip(chains, las, gs, gcs):
                lanes = slice(b * LANES, (b + 1) * LANES)
                dz = g - (g + (gpre_ref[h, :, lanes] + gc)) * jnp.exp(la)
                if masked:
                    dz = jnp.where(key < qrow + b * LANES, dz, 0.0)
                dzs.append(dz.astype(BF16))
                gpre_ref[h, :, lanes] += jnp.sum(g, axis=0, keepdims=True)
            dkc = dvc = None
            for h in range(2):
                dzh = jnp.concatenate(dzs[h * nb:(h + 1) * nb], axis=1)
                wh = jnp.concatenate(ws[h * nb:(h + 1) * nb], axis=1)
                dkh = _dot(dzh, qh[h])
                dvh = _dot(wh, doh[h])
                dkc = dkh if dkc is None else dkc + dkh
                dvc = dvh if dvc is None else dvc + dvh
                dqt_ref[h] += _dot(kt, dzh)
            dk_ref[pl.ds(start, T), :] += dkc
            dv_ref[pl.ds(start, T), :] += dvc

        def step(j, carry):
            tile(j, False)
            return carry

        lax.fori_loop(0, i, step, 0)
        tile(i, True)
        dq_ref[...] = jnp.where(top, dqt_ref[0], dqt_ref[1]).T * scale

    qs, ks, vs, _ = _att_specs("sb", S, T)
    W = npairs * LANES
    return pl.pallas_call(
        body, name=name, grid=(npairs, nq),
        in_specs=[qs, ks, pl.BlockSpec((None, nq, LANES, T), lambda p, i: (p, 0, 0, 0)), vs,
                  pl.BlockSpec((T, LANES), lambda p, i: (i, p)),
                  pl.BlockSpec((2, None, 1, T), lambda p, i: (p, i, 0, 0))],
        out_specs=[pl.BlockSpec((T, LANES), lambda p, i: (i, p)), pl.BlockSpec((S, LANES), lambda p, i: (0, p)),
                   pl.BlockSpec((S, LANES), lambda p, i: (0, p))],
        out_shape=[jax.ShapeDtypeStruct((S, W), F32)] * 3,
        scratch_shapes=[pltpu.VMEM((2, LANES, T), F32), pltpu.VMEM((2, 1, T), F32), pltpu.VMEM((2, 1, T), F32)],
        compiler_params=_cparams(("parallel", "arbitrary")),
    )(proj, proj, kt, proj, do, tot)


def _pad_w0(w):
    z = lambda n: jnp.zeros((w.shape[0], n), w.dtype)
    return jnp.concatenate([w[:, 2048:2432], w[:, 2432:2688], z(64), w[:, 2688:2720], z(32),
                            w[:, 1536:2048], w[:, 2720:3232], w[:, 0:512], w[:, 512:1024], w[:, 1024:1536]], axis=1)


def _unpad_w0(wp):
    return jnp.concatenate([wp[:, L0_SBQ:L0_SBQ + 512], wp[:, L0_SBK:L0_SBK + 512], wp[:, L0_SBV:L0_SBV + 512],
                            wp[:, L0_SBG:L0_SBG + 512], wp[:, 0:384], wp[:, 384:640], wp[:, 704:736],
                            wp[:, L0_MLG:L0_MLG + 512]], axis=1)


def _pad_wq(w):
    return jnp.pad(w.reshape(384, 8, 96), ((0, 0), (0, 0), (0, 32))).reshape(384, 1024)


def _unpad_wq(wp):
    return wp.reshape(384, 8, 128)[:, :, :96].reshape(384, 768)


def _pad_wkv(w):
    w3 = w.reshape(256, 8, 128)
    k = jnp.pad(w3[:, :, :64], ((0, 0), (0, 0), (0, 64))).reshape(256, 1024)
    return jnp.concatenate([k, w3[:, :, 64:].reshape(256, 512)], axis=1)


def _unpad_wkv(wp):
    k = wp[:, :1024].reshape(256, 8, 128)[:, :, :64]
    v = wp[:, 1024:].reshape(256, 8, 64)
    return jnp.concatenate([k, v], axis=-1).reshape(256, 1024)


def _pad_w1(w):
    return jnp.concatenate([w, jnp.zeros((w.shape[0], L1_WIDTH - ODD_IN_WIDTH), w.dtype)], axis=1)


def _local_step(x, positions, target, g, w0p, wqp, wkvp, wo0, w1p, wo1):
    S = x.shape[0]
    nq = S // ATT_T
    invf = ROPE_THETA ** (-jnp.arange(0, MLA_ROPE_DIM, 2, dtype=F32) / MLA_ROPE_DIM)
    invf = jnp.concatenate([jnp.zeros((64,), F32), invf, invf, jnp.zeros((32,), F32)]).reshape(1, LANES)
    cosT, s1T, s2T = _rope_tables(positions.reshape(S, 1), invf, "rope_tables")
    bfp = jnp.pad(g["l1_b_f"], ((0, 0), (0, LANES - FOX_HEADS)))

    proj0, h0t = _norm_matmul(x, g["l0_pre_g"], w0p, "l0_in_proj")
    qm, km, vm, qnt, cnt = _mla_prep(proj0, g["l0_q_a_g"], g["l0_kv_a_g"], wqp, wkvp, cosT, s1T, s2T, "mla_prep")
    sb_vt = _transpose_tiles(proj0, L0_SBV, 4, LANES, 2, "sb_vt")
    sb_kt = _transpose_tiles(proj0, L0_SBK, 4, LANES, 2, "sb_kt")
    o_sb, og_sb, ogt_sb, tot_sb = _sb_fwd_t(proj0, sb_vt, S, 4, "sb_fwd")
    vmt = _transpose_tiles(vm, 0, 4, LANES, 4, "mla_vt")
    kmt = _transpose_tiles(km, 0, 4, 2 * LANES, 4, "mla_kt")
    o_ml, og_ml, ogt_ml, lse_ml = _softmax_fwd("mla", (qm, km, vmt, proj0), None, S, 4, "mla_fwd")
    y0, x1 = _out_proj(og_sb, og_ml, 0, 0, wo0, x, g["l0_post_g"], None, "l0_out_proj")

    proj1, h1t = _norm_matmul(x1, g["l1_pre_g"], w1p, "l1_in_proj")
    cfx = _fox_prep(proj1, bfp, "fox_prep")
    c16 = cfx[:, :FOX_HEADS].T
    c_col = c16.reshape(FOX_HEADS, S, 1)
    vt1 = _transpose_tiles(proj1, L1_V, 8, LANES, 8, "fox_vt")
    kt1 = _transpose_tiles(proj1, L1_K, 8, LANES, 8, "fox_kt")
    o_fx, og_fx, ogt_fx, lse_fx = _softmax_fwd("fox", (proj1, proj1, vt1, proj1), c_col, S, 8, "fox_fwd")
    y1, dx2, lsum = _out_proj(og_fx, og_fx, 0, 1, wo1, x1, g["l1_post_g"], target, "l1_out_proj")

    dy1, do1, dgate1, d_post1 = _out_proj_bwd(dx2, y1, g["l1_post_g"], wo1, proj1, (L1_G, L1_G + 512), o_fx, o_fx, 0, 1, "l1_out_bwd")
    dwo1 = _matmul_t(ogt_fx, dy1, "l1_dw_out")
    dq1, dk1, dv1, dck, dcq = _softmax_bwd_t("fox", proj1, proj1, kt1, proj1, do1, 0, o_fx, lse_fx, c_col, S, 8,
                                             "fox_bwd")
    dc = jnp.pad((dcq.reshape(FOX_HEADS, S) - dck.reshape(FOX_HEADS, S)).T, ((0, 0), (0, LANES - FOX_HEADS)))
    df, d_bf = _fox_prep_bwd(dc, proj1, bfp, "fox_prep_bwd")
    dproj1 = jnp.concatenate([dq1, dk1, dv1, dgate1, df], axis=1)
    dx1, d_pre1 = _in_proj_bwd(dproj1, w1p, x1, g["l1_pre_g"], dx2, "l1_in_bwd")
    dw1p = _matmul_t(h1t, dproj1, "l1_dw_in")

    dy0, do0, dgate0, d_post0 = _out_proj_bwd(dx1, y0, g["l0_post_g"], wo0, proj0, (L0_SBG, L0_MLG), o_sb, o_ml, 0, 0,
                                              "l0_out_bwd")
    dwo0 = jnp.concatenate([_matmul_t(ogt_sb, dy0, "l0_dw_out_sb"), _matmul_t(ogt_ml, dy0, "l0_dw_out_mla")], axis=0)
    dsq, dsk, dsv = _sb_bwd_t(proj0, sb_kt, do0, tot_sb, S, 4, "sb_bwd")
    dqm, dkm, dvm = _softmax_bwd_t("mla", qm, km, kmt, vm, do0, 4, o_ml, lse_ml, None, S, 4, "mla_bwd")
    dprep, dqb, dkvb, d_qag, d_kvag = _mla_prep_bwd(dqm, dkm, dvm, proj0, g["l0_q_a_g"], g["l0_kv_a_g"], wqp, wkvp,
                                                    cosT, s1T, s2T, "mla_prep_bwd")
    dwqp = _matmul_t(qnt, dqb, "l0_dw_qb")
    dwkvp = _matmul_t(cnt, dkvb, "l0_dw_kvb")
    dproj0 = jnp.concatenate([dprep, dgate0[:, :512], dgate0[:, 512:], dsq, dsk, dsv], axis=1)
    dx0, d_pre0 = _in_proj_bwd(dproj0, w0p, x, g["l0_pre_g"], dx1, "l0_in_bwd")
    dw0p = _matmul_t(h0t, dproj0, "l0_dw_in")

    grads = {
        "l0_pre_g": d_pre0, "l0_post_g": d_post0, "l0_w_in": dw0p, "l0_q_a_g": d_qag, "l0_w_q_b": dwqp,
        "l0_kv_a_g": d_kvag, "l0_w_kv_b": dwkvp, "l0_w_out": dwo0, "l1_pre_g": d_pre1, "l1_post_g": d_post1,
        "l1_w_in": dw1p, "l1_b_f": d_bf[:, :FOX_HEADS], "l1_w_out": dwo1,
    }
    return lsum, dx0, grads


_ANY = pl.BlockSpec(memory_space=pl.ANY)


def _place():
    return lax.axis_index("x"), lax.axis_index("y"), lax.axis_index("c")


def _other_chips(x, y):
    return [(1 - x, y), (x, 1 - y), (1 - x, 1 - y)]


def _half(c):
    return pl.ds(c * PACK_HALF, PACK_HALF)


def _weight_gather(pack):
    def body(p_ref, out_ref, send_sems, recv_sems):
        x, y, c = _place()
        sibling = (x, y, 1 - c)
        chips = _other_chips(x, y)

        def blk(chip, cc):
            return out_ref.at[2 * chip[0] + chip[1], _half(cc)]

        def copy(k, src, dst, to):
            return pltpu.make_async_remote_copy(src_ref=src, dst_ref=dst, send_sem=send_sems.at[k],
                                                recv_sem=recv_sems.at[k], device_id=to, device_id_type=MESH)

        first = [copy(j, p_ref.at[_half(c)], blk((x, y), c), (*chip, c)) for j, chip in enumerate(chips)]
        for cp in first:
            cp.start()
        passed = [copy(3 + j, blk(chip, c), blk(chip, c), sibling) for j, chip in enumerate(chips)]
        for j, chip in enumerate(chips):
            copy(j, blk(chip, c), blk(chip, c), (x, y, c)).wait_recv()
            passed[j].start()
        for j, chip in enumerate(chips):
            copy(3 + j, blk(chip, 1 - c), blk(chip, 1 - c), (x, y, c)).wait_recv()
        for cp in first + passed:
            cp.wait_send()

    return pl.pallas_call(
        body, name="weight_gather", in_specs=[_ANY], out_specs=_ANY,
        out_shape=jax.ShapeDtypeStruct((4,) + pack.shape, pack.dtype),
        scratch_shapes=[pltpu.SemaphoreType.DMA((6,)), pltpu.SemaphoreType.DMA((6,))],
    )(pack)


GRAD_TR = 2048


def _grad_core_exchange(p):
    def body(p_ref, recv_ref, send_sems, recv_sems):
        x, y, c = _place()
        give = [pltpu.make_async_remote_copy(src_ref=p_ref.at[j, _half(1 - c)], dst_ref=recv_ref.at[j],
                                             send_sem=send_sems.at[j], recv_sem=recv_sems.at[j],
                                             device_id=(x, y, 1 - c), device_id_type=MESH) for j in range(4)]
        for cp in give:
            cp.start()
        for cp in give:
            cp.wait()

    return pl.pallas_call(
        body, name="grad_core_exchange", in_specs=[_ANY], out_specs=_ANY,
        out_shape=jax.ShapeDtypeStruct((4, PACK_HALF, LANES), p.dtype),
        scratch_shapes=[pltpu.SemaphoreType.DMA((4,)), pltpu.SemaphoreType.DMA((4,))],
    )(p)


def _grad_add_cores(p, theirs, c1):
    tr = GRAD_TR

    def body(c_ref, a_ref, b_ref, o_ref):
        o_ref[...] = a_ref[...] + b_ref[...]

    spec = pl.BlockSpec((None, tr, LANES), lambda j, r, c: (j, r, 0))
    grid_spec = pltpu.PrefetchScalarGridSpec(
        num_scalar_prefetch=1, grid=(4, PACK_HALF // tr),
        in_specs=[pl.BlockSpec((None, None, tr, LANES), lambda j, r, c: (j, c[0], r, 0)), spec], out_specs=spec)
    return pl.pallas_call(
        body, name="grad_add_cores", grid_spec=grid_spec, out_shape=jax.ShapeDtypeStruct(theirs.shape, theirs.dtype),
        compiler_params=_cparams(("parallel", "parallel")),
    )(c1, p.reshape(4, 2, PACK_HALF, LANES), theirs)


def _grad_chip_exchange(q):
    def body(q_ref, out_ref, send_sems, recv_sems):
        x, y, c = _place()
        me = 2 * x + y
        chips = _other_chips(x, y)
        sends = [pltpu.make_async_remote_copy(src_ref=q_ref.at[2 * chip[0] + chip[1]], dst_ref=out_ref.at[me],
                                              send_sem=send_sems.at[j], recv_sem=recv_sems.at[j],
                                              device_id=(*chip, c), device_id_type=MESH) for j, chip in enumerate(chips)]
        for cp in sends:
            cp.start()
        for j, chip in enumerate(chips):
            slot = out_ref.at[2 * chip[0] + chip[1]]
            pltpu.make_async_remote_copy(src_ref=slot, dst_ref=slot, send_sem=send_sems.at[j], recv_sem=recv_sems.at[j],
                                         device_id=(x, y, c), device_id_type=MESH).wait_recv()
        for cp in sends:
            cp.wait_send()

    return pl.pallas_call(
        body, name="grad_chip_exchange", in_specs=[_ANY], out_specs=_ANY,
        out_shape=jax.ShapeDtypeStruct(q.shape, q.dtype),
        scratch_shapes=[pltpu.SemaphoreType.DMA((3,)), pltpu.SemaphoreType.DMA((3,))],
    )(q)


def _grad_add_chips(q, slots, me1):
    tr = GRAD_TR

    def body(me_ref, own_ref, s0, s1, s2, s3, o_ref):
        me = me_ref[0]
        t = [jnp.where(me == j, own_ref[...], s[...]) for j, s in enumerate((s0, s1, s2, s3))]
        o_ref[...] = ((t[0] + t[1]) + t[2]) + t[3]

    def slot_spec(j):
        return pl.BlockSpec((None, tr, LANES), lambda r, me: (jnp.where(me[0] == j, (j + 1) % 4, j), r, 0))

    grid_spec = pltpu.PrefetchScalarGridSpec(
        num_scalar_prefetch=1, grid=(PACK_HALF // tr,),
        in_specs=[pl.BlockSpec((None, tr, LANES), lambda r, me: (me[0], r, 0))] + [slot_spec(j) for j in range(4)],
        out_specs=pl.BlockSpec((tr, LANES), lambda r, me: (r, 0)))
    return pl.pallas_call(
        body, name="grad_add_chips", grid_spec=grid_spec, out_shape=jax.ShapeDtypeStruct(q.shape[1:], q.dtype),
        compiler_params=_cparams(("parallel",)),
    )(me1, q, slots, slots, slots, slots)


def _grad_core_gather(t):
    def body(t_ref, out_ref, send_sem, recv_sem):
        x, y, c = _place()
        give = pltpu.make_async_remote_copy(src_ref=t_ref, dst_ref=out_ref, send_sem=send_sem, recv_sem=recv_sem,
                                            device_id=(x, y, 1 - c), device_id_type=MESH)
        give.start()
        give.wait()

    return pl.pallas_call(
        body, name="grad_core_gather", in_specs=[_ANY], out_specs=_ANY,
        out_shape=jax.ShapeDtypeStruct(t.shape, t.dtype),
        scratch_shapes=[pltpu.SemaphoreType.DMA, pltpu.SemaphoreType.DMA],
    )(t)


def _small_allreduce(sp):
    def body(sp_ref, out_ref, gath_ref, send_sems, recv_sems):
        x, y, c = _place()
        me = 4 * x + 2 * y + c
        gath_ref[me] = sp_ref[...]
        peers = []
        for k in range(1, 8):
            px = 1 - x if k & 4 else x
            py = 1 - y if k & 2 else y
            pc = 1 - c if k & 1 else c
            peers.append((px, py, pc))
        sends = [pltpu.make_async_remote_copy(src_ref=sp_ref, dst_ref=gath_ref.at[me], send_sem=send_sems.at[k],
                                              recv_sem=recv_sems.at[k], device_id=peer, device_id_type=MESH)
                 for k, peer in enumerate(peers)]
        for cp in sends:
            cp.start()
        for k, (px, py, pc) in enumerate(peers):
            slot = gath_ref.at[4 * px + 2 * py + pc]
            pltpu.make_async_remote_copy(src_ref=slot, dst_ref=slot, send_sem=send_sems.at[k], recv_sem=recv_sems.at[k],
                                         device_id=(x, y, c), device_id_type=MESH).wait_recv()
        for cp in sends:
            cp.wait_send()
        tot = gath_ref[0]
        for d in range(1, 8):
            tot = tot + gath_ref[d]
        out_ref[...] = tot

    vm = pl.BlockSpec(memory_space=pltpu.VMEM)
    return pl.pallas_call(
        body, name="small_allreduce", in_specs=[vm], out_specs=vm, out_shape=jax.ShapeDtypeStruct(sp.shape, sp.dtype),
        scratch_shapes=[pltpu.VMEM((8,) + sp.shape, sp.dtype), pltpu.SemaphoreType.DMA((7,)), pltpu.SemaphoreType.DMA((7,))],
    )(sp)


def _adamw_update(w, gv, m, v):
    mn = ADAM_B1 * m + (1.0 - ADAM_B1) * gv
    vn = ADAM_B2 * v + (1.0 - ADAM_B2) * (gv * gv)
    m_hat = mn / (1.0 - ADAM_B1 ** ADAM_STEP)
    v_hat = vn / (1.0 - ADAM_B2 ** ADAM_STEP)
    return -ADAM_LR * (m_hat / (jnp.sqrt(v_hat) + ADAM_EPS) + ADAM_WD * w), mn, vn


def _adamw(w, g, m, v, name):
    rows = w.shape[0]

    def body(w_ref, g_ref, m_ref, v_ref, d_ref, mo_ref, vo_ref):
        d_ref[...], mo_ref[...], vo_ref[...] = _adamw_update(w_ref[...], g_ref[...], m_ref[...], v_ref[...])

    spec = pl.BlockSpec((rows, LANES), lambda r: (0, 0))
    shp = jax.ShapeDtypeStruct(w.shape, F32)
    return pl.pallas_call(
        body, name=name, grid=(1,), in_specs=[spec] * 4, out_specs=[spec] * 3, out_shape=[shp] * 3,
        compiler_params=_cparams(("arbitrary",)),
    )(w, g, m, v)


def _adamw_mats(w, g_mine, g_theirs, m, v, c1):
    tr = GRAD_TR
    nb = PACK_HALF // tr

    def body(c_ref, w_ref, a_ref, b_ref, m_ref, v_ref, g_ref, d_ref, mo_ref, vo_ref):
        gv = jnp.where(pl.program_id(0) == c_ref[0], a_ref[...], b_ref[...])
        g_ref[...] = gv
        d_ref[...], mo_ref[...], vo_ref[...] = _adamw_update(w_ref[...], gv, m_ref[...], v_ref[...])

    full = pl.BlockSpec((tr, LANES), lambda h, r, c: (h * nb + r, 0))
    half = pl.BlockSpec((tr, LANES), lambda h, r, c: (r, 0))
    grid_spec = pltpu.PrefetchScalarGridSpec(num_scalar_prefetch=1, grid=(2, nb),
                                             in_specs=[full, half, half, full, full], out_specs=[full] * 4)
    shp = jax.ShapeDtypeStruct(w.shape, F32)
    return pl.pallas_call(
        body, name="adamw_mats", grid_spec=grid_spec, out_shape=[shp] * 4,
        compiler_params=_cparams(("parallel", "parallel")),
    )(c1, w, g_mine, g_theirs, m, v)


MAT_NAMES = ("l0_w_in", "l0_w_q_b", "l0_w_kv_b", "l0_w_out", "l1_w_in", "l1_w_out")
VEC_NAMES = ("l0_pre_g", "l0_post_g", "l0_q_a_g", "l0_kv_a_g", "l1_pre_g", "l1_post_g", "l1_b_f")
WEIGHT_NAMES = ("l0_pre_g", "l0_post_g", "l0_w_in", "l0_q_a_g", "l0_w_q_b", "l0_kv_a_g", "l0_w_kv_b", "l0_w_out",
                "l1_pre_g", "l1_post_g", "l1_w_in", "l1_b_f", "l1_w_out")
MAT_SHARD = {"l0_w_in": (1024, 808), "l0_w_q_b": (384, 192), "l0_w_kv_b": (256, 256), "l0_w_out": (256, 1024),
             "l1_w_in": (1024, 1028), "l1_w_out": (256, 1024)}
ROW_SHARDED = ("l0_w_out", "l1_w_out")
VEC_LEN = {"l0_pre_g": 1024, "l0_post_g": 1024, "l0_q_a_g": 384, "l0_kv_a_g": 256, "l1_pre_g": 1024,
           "l1_post_g": 1024, "l1_b_f": 16}


def _mat_rows(n):
    r, c = MAT_SHARD[n]
    return r * c // LANES


def _pack_shards(shards):
    parts = [shards[n].reshape(_mat_rows(n), LANES) for n in MAT_NAMES]
    used = sum(_mat_rows(n) for n in MAT_NAMES)
    parts.append(jnp.zeros((PACK_ROWS - used, LANES), parts[0].dtype))
    return jnp.concatenate(parts, axis=0)


def _unpack_shards(pack):
    out, at = {}, 0
    for n in MAT_NAMES:
        out[n] = pack[..., at:at + _mat_rows(n), :].reshape(pack.shape[:-2] + MAT_SHARD[n])
        at += _mat_rows(n)
    return out


def _join_shards(n, s):
    if n in ROW_SHARDED:
        return s.reshape(4 * s.shape[1], s.shape[2])
    return s.transpose(1, 0, 2).reshape(s.shape[1], 4 * s.shape[2])


def _cut_shards(n, w):
    r, c = MAT_SHARD[n]
    if n in ROW_SHARDED:
        return w.reshape(4, r, c)
    return w.reshape(r, 4, c).transpose(1, 0, 2)


def _pack_vecs(vecs):
    parts = []
    for n in VEC_NAMES:
        v = vecs[n].reshape(-1)
        parts.append(jnp.pad(v, (0, VEC_ROWS * LANES - v.shape[0])).reshape(VEC_ROWS, LANES))
    return jnp.concatenate(parts, axis=0)


def _unpack_vecs(pack):
    return {n: pack[k * VEC_ROWS:(k + 1) * VEC_ROWS].reshape(-1)[:VEC_LEN[n]] for k, n in enumerate(VEC_NAMES)}


def kernel(x, positions, l0_pre_g, l0_post_g, l0_w_in, l0_q_a_g, l0_w_q_b, l0_kv_a_g, l0_w_kv_b, l0_w_out, l1_pre_g, l1_post_g, l1_w_in, l1_b_f, l1_w_out, loss_target, m_l0_pre_g, m_l0_post_g, m_l0_w_in, m_l0_q_a_g, m_l0_w_q_b, m_l0_kv_a_g, m_l0_w_kv_b, m_l0_w_out, m_l1_pre_g, m_l1_post_g, m_l1_w_in, m_l1_b_f, m_l1_w_out, v_l0_pre_g, v_l0_post_g, v_l0_w_in, v_l0_q_a_g, v_l0_w_q_b, v_l0_kv_a_g, v_l0_w_kv_b, v_l0_w_out, v_l1_pre_g, v_l1_post_g, v_l1_w_in, v_l1_b_f, v_l1_w_out):
    w = dict(l0_pre_g=l0_pre_g, l0_post_g=l0_post_g, l0_w_in=l0_w_in, l0_q_a_g=l0_q_a_g, l0_w_q_b=l0_w_q_b,
             l0_kv_a_g=l0_kv_a_g, l0_w_kv_b=l0_w_kv_b, l0_w_out=l0_w_out, l1_pre_g=l1_pre_g, l1_post_g=l1_post_g,
             l1_w_in=l1_w_in, l1_b_f=l1_b_f, l1_w_out=l1_w_out)
    m = dict(l0_pre_g=m_l0_pre_g, l0_post_g=m_l0_post_g, l0_w_in=m_l0_w_in, l0_q_a_g=m_l0_q_a_g, l0_w_q_b=m_l0_w_q_b,
             l0_kv_a_g=m_l0_kv_a_g, l0_w_kv_b=m_l0_w_kv_b, l0_w_out=m_l0_w_out, l1_pre_g=m_l1_pre_g,
             l1_post_g=m_l1_post_g, l1_w_in=m_l1_w_in, l1_b_f=m_l1_b_f, l1_w_out=m_l1_w_out)
    v = dict(l0_pre_g=v_l0_pre_g, l0_post_g=v_l0_post_g, l0_w_in=v_l0_w_in, l0_q_a_g=v_l0_q_a_g, l0_w_q_b=v_l0_w_q_b,
             l0_kv_a_g=v_l0_kv_a_g, l0_w_kv_b=v_l0_w_kv_b, l0_w_out=v_l0_w_out, l1_pre_g=v_l1_pre_g,
             l1_post_g=v_l1_post_g, l1_w_in=v_l1_w_in, l1_b_f=v_l1_b_f, l1_w_out=v_l1_w_out)

    cx, cy, cc = _place()
    me1 = jnp.reshape(2 * cx + cy, (1,)).astype(jnp.int32)
    c1 = jnp.reshape(cc, (1,)).astype(jnp.int32)
    w_pack = _pack_shards(w)
    w_bf = w_pack.astype(BF16)
    gathered = lax.dynamic_update_slice(_weight_gather(w_bf), w_bf[None], (2 * cx + cy, 0, 0))
    gathered = _unpack_shards(gathered)
    full = {n: _join_shards(n, gathered[n]) for n in MAT_NAMES}
    gains = {n: w[n].reshape(1, -1) for n in VEC_NAMES}

    lsum, dx0, grads = _local_step(
        x[0], positions[0], loss_target[0], gains, _pad_w0(full["l0_w_in"]), _pad_wq(full["l0_w_q_b"]),
        _pad_wkv(full["l0_w_kv_b"]), full["l0_w_out"], _pad_w1(full["l1_w_in"]), full["l1_w_out"])

    gfull = {"l0_w_in": _unpad_w0(grads["l0_w_in"]), "l0_w_q_b": _unpad_wq(grads["l0_w_q_b"]),
             "l0_w_kv_b": _unpad_wkv(grads["l0_w_kv_b"]), "l0_w_out": grads["l0_w_out"],
             "l1_w_in": grads["l1_w_in"][:, :ODD_IN_WIDTH], "l1_w_out": grads["l1_w_out"]}
    parts = [_cut_shards(n, gfull[n]).reshape(4, _mat_rows(n), LANES) for n in MAT_NAMES]
    used = sum(_mat_rows(n) for n in MAT_NAMES)
    parts.append(jnp.zeros((4, PACK_ROWS - used, LANES), F32))
    g_pack = jnp.concatenate(parts, axis=1)
    q_cores = _grad_add_cores(g_pack, _grad_core_exchange(g_pack), c1)
    g_mine = _grad_add_chips(q_cores, _grad_chip_exchange(q_cores), me1)
    g_theirs = _grad_core_gather(g_mine)

    small = _small_allreduce(jnp.concatenate([_pack_vecs({n: grads[n] for n in VEC_NAMES}),
                                              lsum.reshape(D_MODEL // LANES, LANES)], axis=0))
    g_small = small[:SMALL_ROWS]
    loss = 0.5 * jnp.sum(small[SMALL_ROWS:]) / float(D_MODEL)

    g_shard, d_pack, m_pack, v_pack = _adamw_mats(w_pack, g_mine, g_theirs, _pack_shards(m), _pack_shards(v), c1)
    d_small, m_small, v_small = _adamw(_pack_vecs(w), g_small, _pack_vecs(m), _pack_vecs(v), "adamw_vecs")

    def unpack(mat_pack, vec_pack):
        out = dict(_unpack_shards(mat_pack))
        out.update(_unpack_vecs(vec_pack))
        return [out[n] for n in WEIGHT_NAMES]

    return (loss, dx0[None], *unpack(g_shard, g_small), *unpack(d_pack, d_small), *unpack(m_pack, m_small),
            *unpack(v_pack, v_small))
```

```python
import functools

import numpy as np
import jax
import jax.numpy as jnp
from jax import lax
from jax.experimental import pallas as pl
from jax.experimental.pallas import tpu as pltpu

F32 = jnp.float32
BF16 = jnp.bfloat16
MESH = pl.DeviceIdType.MESH

D_MODEL = 1024
RMS_EPS = 1e-6
ROPE_THETA = 10000.0
SB_WIDTH = 512
MLA_Q_LORA = 384
MLA_KV_LORA = 256
MLA_ROPE_DIM = 32
MLA_WIDTH = 512
FOX_WIDTH = 1024
FOX_HEADS = 16
EVEN_IN_WIDTH = 3232
ODD_IN_WIDTH = 4112

ADAM_LR = 0.001
ADAM_B1 = 0.9
ADAM_B2 = 0.999
ADAM_EPS = 1e-08
ADAM_WD = 0.01
ADAM_STEP = 10

LANES = 128
VMEM_LIMIT = 56 * 1024 * 1024

L0_PREP = 0
L0_PREP_W = 768
L0_SBG = 768
L0_MLG = 1280
L0_SBQ = 1792
L0_SBK = 2304
L0_SBV = 2816
L0_WIDTH = 3328
L1_Q = 0
L1_K = 1024
L1_V = 2048
L1_G = 3072
L1_F = 4096
L1_WIDTH = 4224

ATT_T = 256
ATT_GROUP = 4
NEG = -1e30

PACK_ROWS = 20480
PACK_HALF = PACK_ROWS // 2
VEC_ROWS = 8
SMALL_ROWS = 7 * VEC_ROWS


def _cparams(sem, **kw):
    return pltpu.CompilerParams(dimension_semantics=sem, vmem_limit_bytes=VMEM_LIMIT, **kw)


def _dot(a, b):
    return lax.dot_general(a, b, (((1,), (0,)), ((), ())), preferred_element_type=F32)


def _dot_nt(a, b):
    return lax.dot_general(a, b, (((1,), (1,)), ((), ())), preferred_element_type=F32)


def _dot_tn(a, b):
    return lax.dot_general(a, b, (((0,), (0,)), ((), ())), preferred_element_type=F32)


def _sigmoid(x):
    return 1.0 / (1.0 + jnp.exp(-x))


def _rstd(x):
    return lax.rsqrt(jnp.mean(x * x, axis=-1, keepdims=True) + RMS_EPS)


def _norm_bwd(x, g, dy):
    r = _rstd(x)
    xn = x * r
    dxn = dy * g
    dx = r * (dxn - xn * jnp.mean(dxn * xn, axis=-1, keepdims=True))
    return dx, dy * xn


def _split3(x):
    hi = x.astype(BF16)
    r1 = x - hi.astype(F32)
    mid = r1.astype(BF16)
    lo = (r1 - mid.astype(F32)).astype(BF16)
    return hi, mid, lo


def _pick(n, cands):
    for c in cands:
        if n % c == 0:
            return c
    raise ValueError(n)


def _norm_matmul(x, g, w, name):
    S, K = x.shape
    N = w.shape[1]
    tm = _pick(S, (512, 256))
    tn = _pick(N, (512, 384, 256, 128))

    def body(x_ref, g_ref, w_ref, o_ref, ht_ref, h_ref):
        @pl.when(pl.program_id(1) == 0)
        def _():
            xv = x_ref[...]
            h = (xv * _rstd(xv)) * g_ref[...]
            h_ref[...] = h.astype(BF16)
            ht_ref[...] = h.T.astype(BF16)
        o_ref[...] = _dot(h_ref[...], w_ref[...])

    return pl.pallas_call(
        body, name=name, grid=(S // tm, N // tn),
        in_specs=[pl.BlockSpec((tm, K), lambda i, j: (i, 0)),
                  pl.BlockSpec((1, K), lambda i, j: (0, 0)),
                  pl.BlockSpec((K, tn), lambda i, j: (0, j))],
        out_specs=[pl.BlockSpec((tm, tn), lambda i, j: (i, j)),
                   pl.BlockSpec((K, tm), lambda i, j: (0, i))],
        out_shape=[jax.ShapeDtypeStruct((S, N), F32), jax.ShapeDtypeStruct((K, S), BF16)],
        scratch_shapes=[pltpu.VMEM((tm, K), BF16)],
        compiler_params=_cparams(("parallel", "arbitrary")),
    )(x, g, w)


def _matmul_t(at, b, name):
    M, S = at.shape
    N = b.shape[1]
    tn = _pick(N, (512, 384, 256, 128))
    ts = _pick(S, (512, 256))

    def body(a_ref, b_ref, o_ref):
        @pl.when(pl.program_id(1) == 0)
        def _():
            o_ref[...] = jnp.zeros_like(o_ref)
        o_ref[...] += _dot(a_ref[...], b_ref[...].astype(BF16))

    return pl.pallas_call(
        body, name=name, grid=(N // tn, S // ts),
        in_specs=[pl.BlockSpec((M, ts), lambda j, k: (0, k)),
                  pl.BlockSpec((ts, tn), lambda j, k: (k, j))],
        out_specs=pl.BlockSpec((M, tn), lambda j, k: (0, j)),
        out_shape=jax.ShapeDtypeStruct((M, N), F32),
        compiler_params=_cparams(("parallel", "arbitrary")),
    )(at, b)


def _in_proj_bwd(dproj, w, x, g, dx_up, name):
    S, N = dproj.shape
    K = w.shape[0]
    tm = _pick(S, (512, 256))
    tk = N // 2 if (N // 2) % LANES == 0 and N % 2 == 0 else N // 3
    nk = N // tk

    def body(d_ref, w_ref, x_ref, g_ref, u_ref, dx_ref, dg_ref, acc_ref):
        i, k = pl.program_id(0), pl.program_id(1)

        @pl.when(k == 0)
        def _():
            acc_ref[...] = jnp.zeros_like(acc_ref)

        @pl.when((i == 0) & (k == 0))
        def _():
            dg_ref[...] = jnp.zeros_like(dg_ref)

        acc_ref[...] += _dot_nt(d_ref[...].astype(BF16), w_ref[...])

        @pl.when(k == nk - 1)
        def _():
            dx, dgrow = _norm_bwd(x_ref[...], g_ref[...], acc_ref[...])
            dx_ref[...] = u_ref[...] + dx
            dg_ref[...] += jnp.sum(dgrow, axis=0, keepdims=True)

    return pl.pallas_call(
        body, name=name, grid=(S // tm, nk),
        in_specs=[pl.BlockSpec((tm, tk), lambda i, k: (i, k)),
                  pl.BlockSpec((K, tk), lambda i, k: (0, k)),
                  pl.BlockSpec((tm, K), lambda i, k: (i, 0)),
                  pl.BlockSpec((1, K), lambda i, k: (0, 0)),
                  pl.BlockSpec((tm, K), lambda i, k: (i, 0))],
        out_specs=[pl.BlockSpec((tm, K), lambda i, k: (i, 0)),
                   pl.BlockSpec((1, K), lambda i, k: (0, 0))],
        out_shape=[jax.ShapeDtypeStruct((S, K), F32), jax.ShapeDtypeStruct((1, K), F32)],
        scratch_shapes=[pltpu.VMEM((tm, K), F32)],
        compiler_params=_cparams(("arbitrary", "arbitrary")),
    )(dproj, w, x, g, dx_up)


def _out_proj(og_a, og_b, blk_a, blk_b, w, x, g, target, name):
    S = x.shape[0]
    D = x.shape[1]
    tm = _pick(S, (512, 256))
    with_loss = target is not None

    def body(*refs):
        if with_loss:
            a_ref, b_ref, wa_ref, wb_ref, x_ref, g_ref, t_ref, y_ref, o_ref, l_ref = refs
        else:
            a_ref, b_ref, wa_ref, wb_ref, x_ref, g_ref, y_ref, o_ref = refs
        y = _dot(a_ref[...], wa_ref[...]) + _dot(b_ref[...], wb_ref[...])
        y_ref[...] = y
        xn = x_ref[...] + (y * _rstd(y)) * g_ref[...]
        if with_loss:
            @pl.when(pl.program_id(0) == 0)
            def _():
                l_ref[...] = jnp.zeros_like(l_ref)
            d = xn - t_ref[...]
            o_ref[...] = d / float(D)
            l_ref[...] += jnp.sum(d * d, axis=0, keepdims=True)
        else:
            o_ref[...] = xn

    row = lambda i: (i, 0)
    in_specs = [pl.BlockSpec((tm, 512), lambda i: (i, blk_a)),
                pl.BlockSpec((tm, 512), lambda i: (i, blk_b)),
                pl.BlockSpec((512, D), lambda i: (0, 0)),
                pl.BlockSpec((512, D), lambda i: (1, 0)),
                pl.BlockSpec((tm, D), row),
                pl.BlockSpec((1, D), lambda i: (0, 0))]
    out_specs = [pl.BlockSpec((tm, D), row), pl.BlockSpec((tm, D), row)]
    out_shape = [jax.ShapeDtypeStruct((S, D), F32), jax.ShapeDtypeStruct((S, D), F32)]
    args = [og_a, og_b, w, w, x, g]
    if with_loss:
        in_specs.append(pl.BlockSpec((tm, D), row))
        out_specs.append(pl.BlockSpec((1, D), lambda i: (0, 0)))
        out_shape.append(jax.ShapeDtypeStruct((1, D), F32))
        args.append(target)
    return pl.pallas_call(
        body, name=name, grid=(S // tm,), in_specs=in_specs, out_specs=out_specs, out_shape=out_shape,
        compiler_params=_cparams(("arbitrary",)),
    )(*args)


def _out_proj_bwd(dx_up, y, g, w, proj, gate_offs, o_a, o_b, oblk_a, oblk_b, name):
    S, D = y.shape
    tm = _pick(S, (256,))
    gblk = [off // 256 + c for off in gate_offs for c in range(2)]

    def body(u_ref, y_ref, g_ref, w_ref, g0, g1, g2, g3, oa_ref, ob_ref, dy_ref, do_ref, dgate_ref, dg_ref):
        @pl.when(pl.program_id(0) == 0)
        def _():
            dg_ref[...] = jnp.zeros_like(dg_ref)
        dy, dgrow = _norm_bwd(y_ref[...], g_ref[...], u_ref[...])
        dg_ref[...] += jnp.sum(dgrow, axis=0, keepdims=True)
        dyb = dy.astype(BF16)
        dy_ref[...] = dyb
        dog = _dot_nt(dyb, w_ref[...])
        gates = (g0, g1, g2, g3)
        for c in range(4):
            gt = gates[c][...]
            sg = _sigmoid(gt)
            o_ref = oa_ref if c < 2 else ob_ref
            ov = o_ref[:, (c % 2) * 256:(c % 2 + 1) * 256]
            dc = dog[:, c * 256:(c + 1) * 256]
            do_ref[:, c * 256:(c + 1) * 256] = dc * (gt * sg)
            dgate_ref[:, c * 256:(c + 1) * 256] = dc * ov * (sg * (1.0 + gt * (1.0 - sg)))

    row = lambda i: (i, 0)
    gspec = lambda c: pl.BlockSpec((tm, 256), lambda i: (i, gblk[c]))
    return pl.pallas_call(
        body, name=name, grid=(S // tm,),
        in_specs=[pl.BlockSpec((tm, D), row), pl.BlockSpec((tm, D), row), pl.BlockSpec((1, D), lambda i: (0, 0)),
                  pl.BlockSpec((D, D), lambda i: (0, 0)),
                  gspec(0), gspec(1), gspec(2), gspec(3),
                  pl.BlockSpec((tm, 512), lambda i: (i, oblk_a)),
                  pl.BlockSpec((tm, 512), lambda i: (i, oblk_b))],
        out_specs=[pl.BlockSpec((tm, D), row), pl.BlockSpec((tm, D), row), pl.BlockSpec((tm, D), row),
                   pl.BlockSpec((1, D), lambda i: (0, 0))],
        out_shape=[jax.ShapeDtypeStruct((S, D), BF16), jax.ShapeDtypeStruct((S, D), F32),
                   jax.ShapeDtypeStruct((S, D), F32), jax.ShapeDtypeStruct((1, D), F32)],
        compiler_params=_cparams(("arbitrary",)),
    )(dx_up, y, g, w, proj, proj, proj, proj, o_a, o_b)


def _rope_tables(pos, invf, name):
    S = pos.shape[0]
    tm = _pick(S, (512, 256))

    def body(p_ref, f_ref, c_ref, s1_ref, s2_ref):
        lane = lax.broadcasted_iota(jnp.int32, (1, LANES), 1)
        ang = p_ref[...].astype(F32) * f_ref[...]
        c, s = jnp.cos(ang), jnp.sin(ang)
        c_ref[...] = jnp.where((lane >= 64) & (lane < 96), c, 1.0)
        s1_ref[...] = jnp.where((lane >= 64) & (lane < 80), -s, 0.0)
        s2_ref[...] = jnp.where((lane >= 80) & (lane < 96), s, 0.0)

    spec = pl.BlockSpec((tm, LANES), lambda i: (i, 0))
    return pl.pallas_call(
        body, name=name, grid=(S // tm,),
        in_specs=[pl.BlockSpec((tm, 1), lambda i: (i, 0)), pl.BlockSpec((1, LANES), lambda i: (0, 0))],
        out_specs=[spec, spec, spec],
        out_shape=[jax.ShapeDtypeStruct((S, LANES), F32)] * 3,
        compiler_params=_cparams(("parallel",)),
    )(pos, invf)


def _rope(x, c, s1, s2):
    return x * c + pltpu.roll(x, LANES - 16, 1) * s1 + pltpu.roll(x, 16, 1) * s2


def _rope_t(d, c, s1, s2):
    return d * c + pltpu.roll(d * s1, 16, 1) + pltpu.roll(d * s2, LANES - 16, 1)


def _mla_prep(proj, gq, gkv, wq, wkv, cosT, s1T, s2T, name):
    S = proj.shape[0]
    tm = _pick(S, (256,))

    def body(p_ref, gq_ref, gkv_ref, wq_ref, wkv_ref, c_ref, s1_ref, s2_ref, q_ref, k_ref, v_ref, qn_ref, cn_ref):
        qa = p_ref[:, 0:384]
        ckv = p_ref[:, 384:640]
        kr = p_ref[:, 640:768]
        qn32 = (qa * _rstd(qa)) * gq_ref[...]
        cn32 = (ckv * _rstd(ckv)) * gkv_ref[...]
        qn = qn32.astype(BF16)
        cn = cn32.astype(BF16)
        qn_ref[...] = qn32.T.astype(BF16)
        cn_ref[...] = cn32.T.astype(BF16)
        qb = _dot(qn, wq_ref[...])
        kvb = _dot(cn, wkv_ref[...])
        c, s1, s2 = c_ref[...], s1_ref[...], s2_ref[...]
        krr = _rope(kr, c, s1, s2)
        for h in range(8):
            sl = slice(h * LANES, (h + 1) * LANES)
            q_ref[:, sl] = _rope(qb[:, sl], c, s1, s2)
            k_ref[:, sl] = kvb[:, sl] + krr
        v_ref[...] = kvb[:, 1024:1536]

    row = lambda i: (i, 0)
    fixed = lambda i: (0, 0)
    tspec = pl.BlockSpec((tm, LANES), row)
    return pl.pallas_call(
        body, name=name, grid=(S // tm,),
        in_specs=[pl.BlockSpec((tm, L0_PREP_W), lambda i: (i, L0_PREP // L0_PREP_W)),
                  pl.BlockSpec((1, 384), fixed), pl.BlockSpec((1, 256), fixed),
                  pl.BlockSpec((384, 1024), fixed), pl.BlockSpec((256, 1536), fixed), tspec, tspec, tspec],
        out_specs=[pl.BlockSpec((tm, 1024), row), pl.BlockSpec((tm, 1024), row), pl.BlockSpec((tm, 512), row),
                   pl.BlockSpec((384, tm), lambda i: (0, i)), pl.BlockSpec((256, tm), lambda i: (0, i))],
        out_shape=[jax.ShapeDtypeStruct((S, 1024), F32), jax.ShapeDtypeStruct((S, 1024), F32),
                   jax.ShapeDtypeStruct((S, 512), F32), jax.ShapeDtypeStruct((384, S), BF16),
                   jax.ShapeDtypeStruct((256, S), BF16)],
        compiler_params=_cparams(("parallel",)),
    )(proj, gq, gkv, wq, wkv, cosT, s1T, s2T)


def _mla_prep_bwd(dq, dk, dv, proj, gq, gkv, wq, wkv, cosT, s1T, s2T, name):
    S = proj.shape[0]
    tm = _pick(S, (256,))

    def body(dq_ref, dk_ref, dv_ref, p_ref, gq_ref, gkv_ref, wq_ref, wkv_ref, c_ref, s1_ref, s2_ref,
             dp_ref, dqb_ref, dkvb_ref, dgq_ref, dgkv_ref):
        @pl.when(pl.program_id(0) == 0)
        def _():
            dgq_ref[...] = jnp.zeros_like(dgq_ref)
            dgkv_ref[...] = jnp.zeros_like(dgkv_ref)
        c, s1, s2 = c_ref[...], s1_ref[...], s2_ref[...]
        lane = lax.broadcasted_iota(jnp.int32, (1, LANES), 1)
        dkr = jnp.zeros((tm, LANES), F32)
        for h in range(8):
            sl = slice(h * LANES, (h + 1) * LANES)
            dqb_ref[:, sl] = _rope_t(dq_ref[:, sl], c, s1, s2).astype(BF16)
            dkh = dk_ref[:, sl]
            dkvb_ref[:, sl] = dkh.astype(BF16)
            dkr = dkr + dkh
        dkvb_ref[:, 1024:1536] = dv_ref[...].astype(BF16)
        dkr = jnp.where((lane >= 64) & (lane < 96), _rope_t(dkr, c, s1, s2), 0.0)
        dqn = _dot_nt(dqb_ref[...], wq_ref[...])
        dcn = _dot_nt(dkvb_ref[...], wkv_ref[...])
        dqa, gq_row = _norm_bwd(p_ref[:, 0:384], gq_ref[...], dqn)
        dckv, gkv_row = _norm_bwd(p_ref[:, 384:640], gkv_ref[...], dcn)
        dp_ref[:, 0:384] = dqa
        dp_ref[:, 384:640] = dckv
        dp_ref[:, 640:768] = dkr
        dgq_ref[...] += jnp.sum(gq_row, axis=0, keepdims=True)
        dgkv_ref[...] += jnp.sum(gkv_row, axis=0, keepdims=True)

    row = lambda i: (i, 0)
    fixed = lambda i: (0, 0)
    tspec = pl.BlockSpec((tm, LANES), row)
    return pl.pallas_call(
        body, name=name, grid=(S // tm,),
        in_specs=[pl.BlockSpec((tm, 1024), row), pl.BlockSpec((tm, 1024), row), pl.BlockSpec((tm, 512), row),
                  pl.BlockSpec((tm, L0_PREP_W), lambda i: (i, L0_PREP // L0_PREP_W)),
                  pl.BlockSpec((1, 384), fixed), pl.BlockSpec((1, 256), fixed),
                  pl.BlockSpec((384, 1024), fixed), pl.BlockSpec((256, 1536), fixed), tspec, tspec, tspec],
        out_specs=[pl.BlockSpec((tm, L0_PREP_W), row), pl.BlockSpec((tm, 1024), row), pl.BlockSpec((tm, 1536), row),
                   pl.BlockSpec((1, 384), fixed), pl.BlockSpec((1, 256), fixed)],
        out_shape=[jax.ShapeDtypeStruct((S, L0_PREP_W), F32), jax.ShapeDtypeStruct((S, 1024), BF16),
                   jax.ShapeDtypeStruct((S, 1536), BF16), jax.ShapeDtypeStruct((1, 384), F32),
                   jax.ShapeDtypeStruct((1, 256), F32)],
        compiler_params=_cparams(("arbitrary",)),
    )(dq, dk, dv, proj, gq, gkv, wq, wkv, cosT, s1T, s2T)


def _fox_prep(proj, bf, name):
    S = proj.shape[0]
    tm = _pick(S, (256,))

    def body(f_ref, b_ref, c_ref, carry_ref):
        @pl.when(pl.program_id(0) == 0)
        def _():
            carry_ref[...] = jnp.zeros_like(carry_ref)
        u = f_ref[...] + b_ref[...]
        lf = jnp.minimum(u, 0.0) - jnp.log(1.0 + jnp.exp(-jnp.abs(u)))
        r = lax.broadcasted_iota(jnp.int32, (tm, tm), 0)
        cidx = lax.broadcasted_iota(jnp.int32, (tm, tm), 1)
        tri = (cidx <= r).astype(BF16)
        hi, mid, lo = _split3(lf)
        c = carry_ref[...] + (_dot(tri, hi) + _dot(tri, mid) + _dot(tri, lo))
        c_ref[...] = c
        carry_ref[...] = c[tm - 1:tm, :]

    return pl.pallas_call(
        body, name=name, grid=(S // tm,),
        in_specs=[pl.BlockSpec((tm, LANES), lambda i: (i, L1_F // LANES)), pl.BlockSpec((1, LANES), lambda i: (0, 0))],
        out_specs=pl.BlockSpec((tm, LANES), lambda i: (i, 0)),
        out_shape=jax.ShapeDtypeStruct((S, LANES), F32),
        scratch_shapes=[pltpu.VMEM((1, LANES), F32)],
        compiler_params=_cparams(("arbitrary",)),
    )(proj, bf)


def _fox_prep_bwd(dc, proj, bf, name):
    S = proj.shape[0]
    tm = _pick(S, (256,))
    nb = S // tm

    def body(dc_ref, f_ref, b_ref, df_ref, db_ref, carry_ref):
        @pl.when(pl.program_id(0) == 0)
        def _():
            carry_ref[...] = jnp.zeros_like(carry_ref)
            db_ref[...] = jnp.zeros_like(db_ref)
        r = lax.broadcasted_iota(jnp.int32, (tm, tm), 0)
        cidx = lax.broadcasted_iota(jnp.int32, (tm, tm), 1)
        tri = (cidx >= r).astype(BF16)
        hi, mid, lo = _split3(dc_ref[...])
        dlf = carry_ref[...] + (_dot(tri, hi) + _dot(tri, mid) + _dot(tri, lo))
        carry_ref[...] = dlf[0:1, :]
        u = f_ref[...] + b_ref[...]
        e = jnp.exp(-jnp.abs(u))
        sneg = jnp.where(u >= 0.0, e, 1.0) / (1.0 + e)
        lane = lax.broadcasted_iota(jnp.int32, (1, LANES), 1)
        df = jnp.where(lane < FOX_HEADS, dlf * sneg, 0.0)
        df_ref[...] = df
        db_ref[...] += jnp.sum(df, axis=0, keepdims=True)

    return pl.pallas_call(
        body, name=name, grid=(nb,),
        in_specs=[pl.BlockSpec((tm, LANES), lambda i: (nb - 1 - i, 0)),
                  pl.BlockSpec((tm, LANES), lambda i: (nb - 1 - i, L1_F // LANES)),
                  pl.BlockSpec((1, LANES), lambda i: (0, 0))],
        out_specs=[pl.BlockSpec((tm, LANES), lambda i: (nb - 1 - i, 0)), pl.BlockSpec((1, LANES), lambda i: (0, 0))],
        out_shape=[jax.ShapeDtypeStruct((S, LANES), F32), jax.ShapeDtypeStruct((1, LANES), F32)],
        scratch_shapes=[pltpu.VMEM((1, LANES), F32)],
        compiler_params=_cparams(("arbitrary",)),
    )(dc, proj, bf)


def _att_specs(kind, S, T):
    if kind == "sb":
        qo, ko, vo, go = L0_SBQ // LANES, L0_SBK // LANES, L0_SBV // LANES, L0_SBG // LANES
    elif kind == "fox":
        qo, ko, vo, go = L1_Q // LANES, L1_K // LANES, L1_V // LANES, L1_G // LANES
    else:
        go = L0_MLG // LANES
        return (pl.BlockSpec((T, 256), lambda p, i: (i, p)), pl.BlockSpec((S, 256), lambda p, i: (0, p)),
                pl.BlockSpec((S, LANES), lambda p, i: (0, p)), pl.BlockSpec((T, LANES), lambda p, i: (i, go + p)))
    return (pl.BlockSpec((T, LANES), lambda p, i: (i, qo + p)), pl.BlockSpec((S, LANES), lambda p, i: (0, ko + p)),
            pl.BlockSpec((S, LANES), lambda p, i: (0, vo + p)), pl.BlockSpec((T, LANES), lambda p, i: (i, go + p)))


def _loop_tiles(n, tiles, order=lambda t: t):
    def group(g, carry):
        tiles([order(g * ATT_GROUP + u) for u in range(ATT_GROUP)], False)
        return carry

    def single(t, carry):
        tiles([order(t)], False)
        return carry

    lax.fori_loop(0, n // ATT_GROUP, group, 0)
    lax.fori_loop((n // ATT_GROUP) * ATT_GROUP, n, single, 0)


def _head_q(kind, q_ref, m0, scale):
    if kind == "mla":
        return [q_ref[:, 0:LANES].astype(BF16), q_ref[:, LANES:2 * LANES].astype(BF16)]
    qv = q_ref[...] * scale
    return [jnp.where(m0, qv, 0.0).astype(BF16), jnp.where(m0, 0.0, qv).astype(BF16)]


def _head_k(kind, k_ref, start, T):
    if kind == "mla":
        return [k_ref[pl.ds(start, T), 0:LANES].astype(BF16), k_ref[pl.ds(start, T), LANES:2 * LANES].astype(BF16)]
    kb = k_ref[pl.ds(start, T), :].astype(BF16)
    return [kb, kb]


def _transpose_tiles(src, col_off, n_out, cw, group, name):
    S = src.shape[0]
    T = ATT_T
    first = col_off // (group * cw)

    def body(x_ref, o_ref):
        for u in range(group):
            o_ref[u] = x_ref[:, u * cw:(u + 1) * cw].T.astype(BF16)

    return pl.pallas_call(
        body, name=name, grid=(S // T, n_out // group),
        in_specs=[pl.BlockSpec((T, group * cw), lambda j, g: (j, first + g))],
        out_specs=pl.BlockSpec((group, None, cw, T), lambda j, g: (g, j, 0, 0)),
        out_shape=jax.ShapeDtypeStruct((n_out, S // T, cw, T), BF16),
        compiler_params=_cparams(("parallel", "parallel")),
    )(src)


def _softmax_fwd(kind, qkvg, c_col, S, npairs, name):
    T = ATT_T
    nq = S // T
    fox = kind == "fox"
    scale = (96 if kind == "mla" else 64) ** -0.5

    def body(*refs):
        if fox:
            q_ref, k_ref, vt_ref, g_ref, cc_ref, o_ref, og_ref, ogt_ref, st_ref, m_ref, acc_ref = refs
        else:
            q_ref, k_ref, vt_ref, g_ref, o_ref, og_ref, ogt_ref, st_ref, m_ref, acc_ref = refs
        i = pl.program_id(1)
        m0 = lax.broadcasted_iota(jnp.int32, (1, LANES), 1) < 64
        top = lax.broadcasted_iota(jnp.int32, (LANES, 1), 0) < 64
        key = lax.broadcasted_iota(jnp.int32, (T, LANES), 0)
        qrow = lax.broadcasted_iota(jnp.int32, (T, LANES), 1)
        qh = _head_q(kind, q_ref, m0, scale)
        m_ref[...] = jnp.full(m_ref.shape, NEG, F32)
        acc_ref[...] = jnp.zeros(acc_ref.shape, F32)
        chains = [(h, b) for h in range(2) for b in range(T // LANES)]

        def tiles(js, masked):
            starts = [pl.multiple_of(j * T, T) for j in js]
            zss = []
            for start in starts:
                kh = _head_k(kind, k_ref, start, T)
                zss.append(_split_blocks([_dot_nt(kh[h], qh[h]) for h in range(2)]))
            pss, alss = [], []
            for start, zs in zip(starts, zss):
                ps, alphas = [], []
                for (h, b), z in zip(chains, zs):
                    lanes = slice(b * LANES, (b + 1) * LANES)
                    if kind == "mla":
                        z = z * scale
                    if fox:
                        z = z - cc_ref[h, pl.ds(start, T), :]
                    if masked:
                        z = jnp.where(key <= qrow + b * LANES, z, NEG)
                    m_prev = m_ref[h, :, lanes]
                    m_new = jnp.maximum(m_prev, jnp.max(z, axis=0, keepdims=True))
                    alphas.append(jnp.exp(m_prev - m_new))
                    ps.append(jnp.exp(z - m_new).astype(BF16))
                    m_ref[h, :, lanes] = m_new
                pss.append(_join_blocks(ps, T // LANES))
                alss.append(_join_blocks(alphas, T // LANES))
            for j, ps, alphas in zip(js, pss, alss):
                vt = vt_ref[j]
                vth = [jnp.where(top, vt, 1.0).astype(BF16), jnp.where(top, 1.0, vt).astype(BF16)]
                for h in range(2):
                    acc_ref[h] = alphas[h] * acc_ref[h] + _dot(vth[h], ps[h])

        _loop_tiles(i, tiles)
        tiles([i], True)
        acc = [acc_ref[0], acc_ref[1]]
        ot = jnp.concatenate([acc[0][0:64] / acc[0][64:128], acc[1][64:128] / acc[1][0:64]], axis=0)
        o = ot.T
        o_ref[...] = o
        gt = g_ref[...]
        og = o * (gt * _sigmoid(gt))
        og_ref[...] = og.astype(BF16)
        ogt_ref[...] = og.T.astype(BF16)
        st_ref[0] = m_ref[0] + jnp.log(acc[0][64:65])
        st_ref[1] = m_ref[1] + jnp.log(acc[1][0:1])

    qs, ks, _, gs = _att_specs(kind, S, T)
    in_specs = [qs, ks, pl.BlockSpec((None, nq, LANES, T), lambda p, i: (p, 0, 0, 0)), gs]
    args = list(qkvg)
    if fox:
        in_specs += [pl.BlockSpec((2, S, LANES), lambda p, i: (p, 0, 0))]
        args += [c_col]
    W = npairs * LANES
    return pl.pallas_call(
        body, name=name, grid=(npairs, nq), in_specs=in_specs,
        out_specs=[pl.BlockSpec((T, LANES), lambda p, i: (i, p)), pl.BlockSpec((T, LANES), lambda p, i: (i, p)),
                   pl.BlockSpec((LANES, T), lambda p, i: (p, i)),
                   pl.BlockSpec((2, None, 1, T), lambda p, i: (p, i, 0, 0))],
        out_shape=[jax.ShapeDtypeStruct((S, W), F32), jax.ShapeDtypeStruct((S, W), BF16),
                   jax.ShapeDtypeStruct((W, S), BF16),
                   jax.ShapeDtypeStruct((2 * npairs, nq, 1, T), F32)],
        scratch_shapes=[pltpu.VMEM((2, 1, T), F32), pltpu.VMEM((2, LANES, T), F32)],
        compiler_params=_cparams(("parallel", "parallel")),
    )(*args)


def _softmax_bwd(kind, qkv, do, do_off, o, lse, c_row, S, npairs, name):
    T = ATT_T
    nq = S // T
    fox = kind == "fox"
    mla = kind == "mla"
    scale = (96 if mla else 64) ** -0.5
    kw = 256 if mla else LANES

    def body(*refs):
        if fox:
            q_ref, k_ref, v_ref, do_ref, o_ref, st_ref, cr_ref, dq_ref, dk_ref, dv_ref, dc_ref, dcq_ref = refs
        else:
            q_ref, k_ref, v_ref, do_ref, o_ref, st_ref, dq_ref, dk_ref, dv_ref = refs
        i = pl.program_id(1)

        @pl.when(i == 0)
        def _():
            dk_ref[...] = jnp.zeros_like(dk_ref)
            dv_ref[...] = jnp.zeros_like(dv_ref)
            if fox:
                dc_ref[...] = jnp.zeros_like(dc_ref)

        m0 = lax.broadcasted_iota(jnp.int32, (1, LANES), 1) < 64
        causal = lax.broadcasted_iota(jnp.int32, (T, T), 1) <= lax.broadcasted_iota(jnp.int32, (T, T), 0)
        qh = _head_q(kind, q_ref, m0, scale)
        dov = do_ref[...]
        prod = dov * o_ref[...]
        dd = [jnp.sum(jnp.where(m0, prod, 0.0), axis=1, keepdims=True),
              jnp.sum(jnp.where(m0, 0.0, prod), axis=1, keepdims=True)]
        doh = [jnp.where(m0, dov, 0.0).astype(BF16), jnp.where(m0, 0.0, dov).astype(BF16)]
        lse_h = [st_ref[0], st_ref[1]]

        def tile(j, carry, masked):
            start = pl.multiple_of(j * T, T)
            vb = v_ref[pl.ds(start, T), :].astype(BF16)
            kh = _head_k(kind, k_ref, start, T)
            dqs = []
            dkc = []
            dvc = jnp.zeros((T, LANES), F32)
            for h in range(2):
                z = _dot_nt(qh[h], kh[h])
                if mla:
                    z = z * scale
                if fox:
                    z = z - cr_ref[h, pl.ds(j, 1), :]
                if masked:
                    z = jnp.where(causal, z, NEG)
                p = jnp.exp(z - lse_h[h])
                ds = p * (_dot_nt(doh[h], vb) - dd[h])
                dsb = ds.astype(BF16)
                dqh = carry[h][0] + _dot(dsb, kh[h])
                dkc.append(_dot_tn(dsb, qh[h]))
                dvc = dvc + _dot_tn(p.astype(BF16), doh[h])
                if fox:
                    dc_ref[h, pl.ds(j, 1), :] += -jnp.sum(ds, axis=0, keepdims=True)
                    dqs.append((dqh, carry[h][1] + jnp.sum(ds, axis=1, keepdims=True)))
                else:
                    dqs.append((dqh,))
            if mla:
                dk_ref[pl.ds(start, T), 0:LANES] += dkc[0] * scale
                dk_ref[pl.ds(start, T), LANES:2 * LANES] += dkc[1] * scale
            else:
                dk_ref[pl.ds(start, T), :] += dkc[0] + dkc[1]
            dv_ref[pl.ds(start, T), :] += dvc
            return tuple(dqs)

        one = (jnp.zeros((T, LANES), F32), jnp.zeros((T, 1), F32)) if fox else (jnp.zeros((T, LANES), F32),)
        carry = lax.fori_loop(0, i, lambda j, c: tile(j, c, False), (one, one))
        carry = tile(i, carry, True)
        if mla:
            dq_ref[:, 0:LANES] = carry[0][0] * scale
            dq_ref[:, LANES:2 * LANES] = carry[1][0] * scale
        else:
            dq_ref[...] = jnp.where(m0, carry[0][0], carry[1][0]) * scale
        if fox:
            dcq_ref[0] = carry[0][1]
            dcq_ref[1] = carry[1][1]

    qs, ks, vs, _ = _att_specs(kind, S, T)
    in_specs = [qs, ks, vs,
                pl.BlockSpec((T, LANES), lambda p, i: (i, do_off + p)),
                pl.BlockSpec((T, LANES), lambda p, i: (i, p)),
                pl.BlockSpec((2, T, 1), lambda p, i: (p, i, 0))]
    args = list(qkv) + [do, o, lse]
    W = npairs * LANES
    out_specs = [pl.BlockSpec((T, kw), lambda p, i: (i, p)), pl.BlockSpec((S, kw), lambda p, i: (0, p)),
                 pl.BlockSpec((S, LANES), lambda p, i: (0, p))]
    out_shape = [jax.ShapeDtypeStruct((S, npairs * kw), F32), jax.ShapeDtypeStruct((S, npairs * kw), F32),
                 jax.ShapeDtypeStruct((S, W), F32)]
    if fox:
        in_specs += [pl.BlockSpec((2, nq, T), lambda p, i: (p, 0, 0))]
        args += [c_row]
        out_specs +=[pl.BlockSpec((2, nq, T), lambda p, i: (p, 0, 0)), pl.BlockSpec((2, T, 1), lambda p, i: (p, i, 0))]
        out_shape += [jax.ShapeDtypeStruct((2 * npairs, nq, T), F32), jax.ShapeDtypeStruct((2 * npairs, S, 1), F32)]
    return pl.pallas_call(
        body, name=name, grid=(npairs, nq), in_specs=in_specs, out_specs=out_specs, out_shape=out_shape,
        compiler_params=_cparams(("parallel", "arbitrary")),
    )(*args)


def _softplus_parts(z):
    sp = jnp.maximum(z, 0.0) + jnp.log(1.0 + jnp.exp(-jnp.abs(z)))
    return -sp, z - sp


def _split2(x):
    hi = x.astype(BF16)
    return hi, (x - hi.astype(F32)).astype(BF16)


def _sb_fwd(proj, S, npairs, name):
    T = ATT_T
    nq = S // T
    scale = 64 ** -0.5

    def body(q_ref, k_ref, v_ref, g_ref, o_ref, og_ref, st_ref):
        i = pl.program_id(1)
        m0 = lax.broadcasted_iota(jnp.int32, (1, LANES), 1) < 64
        r = lax.broadcasted_iota(jnp.int32, (T, T), 0)
        c = lax.broadcasted_iota(jnp.int32, (T, T), 1)
        before = c < r
        after = (r > c).astype(BF16)
        qh = _head_q("sb", q_ref, m0, scale)

        def tile(j, carry, masked):
            start = pl.multiple_of(j * T, T)
            vb = v_ref[pl.ds(start, T), :].astype(BF16)
            kb = k_ref[pl.ds(start, T), :].astype(BF16)
            out = []
            for h in range(2):
                rem, acc = carry[h]
                z = _dot_nt(qh[h], kb)
                lk, la = _softplus_parts(z)
                if masked:
                    lk = jnp.where(before, lk, 0.0)
                hi, lo = _split2(lk)
                lr = rem + (_dot(hi, after) + _dot(lo, after))
                w = jnp.exp(la + lr)
                if masked:
                    w = jnp.where(before, w, 0.0)
                out.append((rem + jnp.sum(lk, axis=1, keepdims=True), acc + _dot(w.astype(BF16), vb)))
            return tuple(out)

        init = tuple((jnp.zeros((T, 1), F32), jnp.zeros((T, LANES), F32)) for _ in range(2))
        carry = tile(i, init, True)
        carry = lax.fori_loop(0, i, lambda jj, cr: tile(i - 1 - jj, cr, False), carry)
        o = jnp.where(m0, carry[0][1], carry[1][1])
        o_ref[...] = o
        gt = g_ref[...]
        og_ref[...] = (o * (gt * _sigmoid(gt))).astype(BF16)
        for h in range(2):
            st_ref[h] = carry[h][0]

    W = npairs * LANES
    return pl.pallas_call(
        body, name=name, grid=(npairs, nq), in_specs=list(_att_specs("sb", S, T)),
        out_specs=[pl.BlockSpec((T, LANES), lambda p, i: (i, p)), pl.BlockSpec((T, LANES), lambda p, i: (i, p)),
                   pl.BlockSpec((2, T, 1), lambda p, i: (p, i, 0))],
        out_shape=[jax.ShapeDtypeStruct((S, W), F32), jax.ShapeDtypeStruct((S, W), BF16),
                   jax.ShapeDtypeStruct((2 * npairs, S, 1), F32)],
        compiler_params=_cparams(("parallel", "parallel")),
    )(proj, proj, proj, proj)


def _sb_bwd(proj, do, tot, S, npairs, name):
    T = ATT_T
    nq = S // T
    scale = 64 ** -0.5

    def body(q_ref, k_ref, v_ref, do_ref, st_ref, dq_ref, dk_ref, dv_ref):
        i = pl.program_id(1)

        @pl.when(i == 0)
        def _():
            dk_ref[...] = jnp.zeros_like(dk_ref)
            dv_ref[...] = jnp.zeros_like(dv_ref)

        m0 = lax.broadcasted_iota(jnp.int32, (1, LANES), 1) < 64
        r = lax.broadcasted_iota(jnp.int32, (T, T), 0)
        c = lax.broadcasted_iota(jnp.int32, (T, T), 1)
        before = c < r
        upto = (r <= c).astype(BF16)
        left = (r < c).astype(BF16)
        qh = _head_q("sb", q_ref, m0, scale)
        dov = do_ref[...]
        doh = [jnp.where(m0, dov, 0.0).astype(BF16), jnp.where(m0, 0.0, dov).astype(BF16)]
        tot_h = [st_ref[0], st_ref[1]]

        def tile(j, carry, masked):
            start = pl.multiple_of(j * T, T)
            vb = v_ref[pl.ds(start, T), :].astype(BF16)
            kb = k_ref[pl.ds(start, T), :].astype(BF16)
            out = []
            dkc = jnp.zeros((T, LANES), F32)
            dvc = jnp.zeros((T, LANES), F32)
            for h in range(2):
                pre, gpre, dq = carry[h]
                z = _dot_nt(qh[h], kb)
                lk, la = _softplus_parts(z)
                if masked:
                    lk = jnp.where(before, lk, 0.0)
                hi, lo = _split2(lk)
                lr = (tot_h[h] - pre) - (_dot(hi, upto) + _dot(lo, upto))
                w = jnp.exp(la + lr)
                if masked:
                    w = jnp.where(before, w, 0.0)
                g = _dot_nt(doh[h], vb) * w
                gfull = gpre + _dot(g.astype(BF16), left)
                dz = g - (g + gfull) * jnp.exp(la)
                if masked:
                    dz = jnp.where(before, dz, 0.0)
                dzb = dz.astype(BF16)
                dkc = dkc + _dot_tn(dzb, qh[h])
                dvc = dvc + _dot_tn(w.astype(BF16), doh[h])
                out.append((pre + jnp.sum(lk, axis=1, keepdims=True), gpre + jnp.sum(g, axis=1, keepdims=True),
                            dq + _dot(dzb, kb)))
            dk_ref[pl.ds(start, T), :] += dkc
            dv_ref[pl.ds(start, T), :] += dvc
            return tuple(out)

        init = tuple((jnp.zeros((T, 1), F32), jnp.zeros((T, 1), F32), jnp.zeros((T, LANES), F32)) for _ in range(2))
        carry = lax.fori_loop(0, i, lambda j, cr: tile(j, cr, False), init)
        carry = tile(i, carry, True)
        dq_ref[...] = jnp.where(m0, carry[0][2], carry[1][2]) * scale

    qs, ks, vs, _ = _att_specs("sb", S, T)
    W = npairs * LANES
    return pl.pallas_call(
        body, name=name, grid=(npairs, nq),
        in_specs=[qs, ks, vs, pl.BlockSpec((T, LANES), lambda p, i: (i, p)),
                  pl.BlockSpec((2, T, 1), lambda p, i: (p, i, 0))],
        out_specs=[pl.BlockSpec((T, LANES), lambda p, i: (i, p)), pl.BlockSpec((S, LANES), lambda p, i: (0, p)),
                   pl.BlockSpec((S, LANES), lambda p, i: (0, p))],
        out_shape=[jax.ShapeDtypeStruct((S, W), F32)] * 3,
        compiler_params=_cparams(("parallel", "arbitrary")),
    )(proj, proj, proj, do, tot)


def _split_blocks(per_head):
    return [x[:, b * LANES:(b + 1) * LANES] for x in per_head for b in range(x.shape[1] // LANES)]


def _join_blocks(per_block, nb):
    return [jnp.concatenate(per_block[h * nb:(h + 1) * nb], axis=1) for h in range(len(per_block) // nb)]


def _row_of(col):
    return jnp.broadcast_to(col, (col.shape[0], LANES)).T[0:1]


def _softmax_bwd_t(kind, q, k, kt, v, do, do_off, o, lse, c_col, S, npairs, name):
    T = ATT_T
    nq = S // T
    nb = T // LANES
    fox = kind == "fox"
    mla = kind == "mla"
    scale = (96 if mla else 64) ** -0.5
    kw = 256 if mla else LANES

    def body(*refs):
        if fox:
            (q_ref, k_ref, kt_ref, v_ref, do_ref, o_ref, st_ref, cc_ref,
             dq_ref, dk_ref, dv_ref, dck_ref, dcq_ref, dqt_ref, rs_ref, dkx_ref) = refs
        else:
            q_ref, k_ref, kt_ref, v_ref, do_ref, o_ref, st_ref, dq_ref, dk_ref, dv_ref, dqt_ref = refs
        i = pl.program_id(1)

        @pl.when(i == 0)
        def _():
            dv_ref[...] = jnp.zeros_like(dv_ref)
            if fox:
                dkx_ref[...] = jnp.zeros_like(dkx_ref)
            else:
                dk_ref[...] = jnp.zeros_like(dk_ref)

        m0 = lax.broadcasted_iota(jnp.int32, (1, LANES), 1) < 64
        top = lax.broadcasted_iota(jnp.int32, (LANES, 1), 0) < 64
        key = lax.broadcasted_iota(jnp.int32, (T, LANES), 0)
        qrow = lax.broadcasted_iota(jnp.int32, (T, LANES), 1)
        qh = _head_q(kind, q_ref, m0, scale)
        if fox:
            qv = q_ref[...] * scale
            qk = [jnp.where(m0, qv, 1.0).astype(BF16), jnp.where(m0, 1.0, qv).astype(BF16)]
        else:
            qk = qh
        dov = do_ref[...]
        prod = dov * o_ref[...]
        dd = [_row_of(jnp.sum(jnp.where(m0, prod, 0.0), axis=1, keepdims=True)),
              _row_of(jnp.sum(jnp.where(m0, 0.0, prod), axis=1, keepdims=True))]
        doh = [jnp.where(m0, dov, 0.0).astype(BF16), jnp.where(m0, 0.0, dov).astype(BF16)]
        lse = [st_ref[0], st_ref[1]]
        dqt_ref[...] = jnp.zeros_like(dqt_ref)
        if fox:
            rs_ref[...] = jnp.zeros_like(rs_ref)
        chains = [(h, b) for h in range(2) for b in range(nb)]

        def tiles(js, masked):
            starts = [pl.multiple_of(j * T, T) for j in js]
            zss, dpss = [], []
            for start in starts:
                vb = v_ref[pl.ds(start, T), :].astype(BF16)
                kh = _head_k(kind, k_ref, start, T)
                zss.append(_split_blocks([_dot_nt(kh[h], qh[h]) for h in range(2)]))
                dpss.append(_split_blocks([_dot_nt(vb, doh[h]) for h in range(2)]))
            pss, dsss = [], []
            for start, zs, dps in zip(starts, zss, dpss):
                ps, dss = [], []
                for (h, b), z, dp in zip(chains, zs, dps):
                    lanes = slice(b * LANES, (b + 1) * LANES)
                    if mla:
                        z = z * scale
                    if fox:
                        z = z - cc_ref[h, pl.ds(start, T), :]
                    if masked:
                        z = jnp.where(key <= qrow + b * LANES, z, NEG)
                    p = jnp.exp(z - lse[h][:, lanes])
                    ds = p * (dp - dd[h][:, lanes])
                    dsb = ds.astype(BF16)
                    if fox:
                        rs_ref[h, :, lanes] += jnp.sum(dsb.astype(F32), axis=0, keepdims=True)
                    ps.append(p.astype(BF16))
                    dss.append(dsb)
                pss.append(_join_blocks(ps, nb))
                dsss.append(_join_blocks(dss, nb))
            for j, start, ps, dss in zip(js, starts, pss, dsss):
                kt = kt_ref[j]
                dvc = None
                for h in range(2):
                    dkh = _dot(dss[h], qk[h])
                    dvh = _dot(ps[h], doh[h])
                    dvc = dvh if dvc is None else dvc + dvh
                    kth = kt[h * LANES:(h + 1) * LANES] if mla else kt
                    dqt_ref[h] += _dot(kth, dss[h])
                    if fox:
                        dkx_ref[h, pl.ds(start, T), :] += dkh
                    elif mla:
                        dk_ref[pl.ds(start, T), h * LANES:(h + 1) * LANES] += dkh * scale
                    else:
                        dk_ref[pl.ds(start, T), :] += dkh
                dv_ref[pl.ds(start, T), :] += dvc

        _loop_tiles(i, tiles)
        tiles([i], True)
        if mla:
            dq_ref[:, 0:LANES] = dqt_ref[0].T * scale
            dq_ref[:, LANES:2 * LANES] = dqt_ref[1].T * scale
        else:
            dq_ref[...] = jnp.where(top, dqt_ref[0], dqt_ref[1]).T * scale
        if fox:
            dcq_ref[0] = rs_ref[0]
            dcq_ref[1] = rs_ref[1]

            @pl.when(i == nq - 1)
            def _():
                dk_ref[...] = jnp.where(m0, dkx_ref[0], dkx_ref[1])
                dck_ref[0] = dkx_ref[0][:, 64:65]
                dck_ref[1] = dkx_ref[1][:, 0:1]

    qs, ks, vs, _ = _att_specs(kind, S, T)
    stat = pl.BlockSpec((2, None, 1, T), lambda p, i: (p, i, 0, 0))
    in_specs = [qs, ks, pl.BlockSpec((None, nq, kw, T), lambda p, i: (p, 0, 0, 0)), vs,
                pl.BlockSpec((T, LANES), lambda p, i: (i, do_off + p)),
                pl.BlockSpec((T, LANES), lambda p, i: (i, p)), stat]
    args = [q, k, kt, v, do, o, lse]
    W = npairs * LANES
    out_specs = [pl.BlockSpec((T, kw), lambda p, i: (i, p)), pl.BlockSpec((S, kw), lambda p, i: (0, p)),
                 pl.BlockSpec((S, LANES), lambda p, i: (0, p))]
    out_shape = [jax.ShapeDtypeStruct((S, npairs * kw), F32), jax.ShapeDtypeStruct((S, npairs * kw), F32),
                 jax.ShapeDtypeStruct((S, W), F32)]
    scratch = [pltpu.VMEM((2, LANES, T), F32)]
    if fox:
        in_specs.append(pl.BlockSpec((2, S, LANES), lambda p, i: (p, 0, 0)))
        args.append(c_col)
        out_specs += [pl.BlockSpec((2, S, 1), lambda p, i: (p, 0, 0)), stat]
        out_shape += [jax.ShapeDtypeStruct((2 * npairs, S, 1), F32), jax.ShapeDtypeStruct((2 * npairs, nq, 1, T), F32)]
        scratch += [pltpu.VMEM((2, 1, T), F32), pltpu.VMEM((2, S, LANES), F32)]
    return pl.pallas_call(
        body, name=name, grid=(npairs, nq), in_specs=in_specs, out_specs=out_specs, out_shape=out_shape,
        scratch_shapes=scratch, compiler_params=_cparams(("parallel", "arbitrary")),
    )(*args)


def _sb_fwd_t(proj, vt, S, npairs, name):
    T = ATT_T
    nq = S // T
    nb = T // LANES
    scale = 64 ** -0.5

    def body(q_ref, k_ref, vt_ref, g_ref, o_ref, og_ref, ogt_ref, st_ref, rem_ref, acc_ref):
        i = pl.program_id(1)
        m0 = lax.broadcasted_iota(jnp.int32, (1, LANES), 1) < 64
        top = lax.broadcasted_iota(jnp.int32, (LANES, 1), 0) < 64
        key = lax.broadcasted_iota(jnp.int32, (T, LANES), 0)
        qrow = lax.broadcasted_iota(jnp.int32, (T, LANES), 1)
        r = lax.broadcasted_iota(jnp.int32, (T, T), 0)
        c = lax.broadcasted_iota(jnp.int32, (T, T), 1)
        after = (c > r).astype(BF16)
        qh = _head_q("sb", q_ref, m0, scale)
        rem_ref[...] = jnp.zeros_like(rem_ref)
        acc_ref[...] = jnp.zeros_like(acc_ref)
        chains = [(h, b) for h in range(2) for b in range(nb)]

        def tiles(js, masked):
            zss = []
            for j in js:
                kb = k_ref[pl.ds(pl.multiple_of(j * T, T), T), :].astype(BF16)
                zss.append(_split_blocks([_dot_nt(kb, qh[h]) for h in range(2)]))
            lass, sums, hiss, loss = [], [], [], []
            for zs in zss:
                las, sm, his, los = [], [], [], []
                for (h, b), z in zip(chains, zs):
                    lk, la = _softplus_parts(z)
                    if masked:
                        lk = jnp.where(key < qrow + b * LANES, lk, 0.0)
                    hi, lo = _split2(lk)
                    las.append(la)
                    sm.append(jnp.sum(lk, axis=0, keepdims=True))
                    his.append(hi)
                    los.append(lo)
                lass.append(las)
                sums.append(sm)
                hiss.append(_join_blocks(his, nb))
                loss.append(_join_blocks(los, nb))
            rcss = [_split_blocks([_dot(after, hi) + _dot(after, lo) for hi, lo in zip(his, los)])
                    for his, los in zip(hiss, loss)]
            wss = []
            for las, sm, rcs in zip(lass, sums, rcss):
                ws = []
                for (h, b), la, s, rc in zip(chains, las, sm, rcs):
                    lanes = slice(b * LANES, (b + 1) * LANES)
                    w = jnp.exp(la + (rem_ref[h, :, lanes] + rc))
                    if masked:
                        w = jnp.where(key < qrow + b * LANES, w, 0.0)
                    ws.append(w.astype(BF16))
                    rem_ref[h, :, lanes] += s
                wss.append(_join_blocks(ws, nb))
            for j, ws in zip(js, wss):
                vtb = vt_ref[j]
                for h in range(2):
                    acc_ref[h] += _dot(vtb, ws[h])

        tiles([i], True)
        _loop_tiles(i, tiles, lambda t: i - 1 - t)
        o = jnp.where(top, acc_ref[0], acc_ref[1]).T
        o_ref[...] = o
        gt = g_ref[...]
        og = o * (gt * _sigmoid(gt))
        og_ref[...] = og.astype(BF16)
        ogt_ref[...] = og.T.astype(BF16)
        st_ref[0] = rem_ref[0]
        st_ref[1] = rem_ref[1]

    qs, ks, _, gs = _att_specs("sb", S, T)
    W = npairs * LANES
    return pl.pallas_call(
        body, name=name, grid=(npairs, nq),
        in_specs=[qs, ks, pl.BlockSpec((None, nq, LANES, T), lambda p, i: (p, 0, 0, 0)), gs],
        out_specs=[pl.BlockSpec((T, LANES), lambda p, i: (i, p)), pl.BlockSpec((T, LANES), lambda p, i: (i, p)),
                   pl.BlockSpec((LANES, T), lambda p, i: (p, i)),
                   pl.BlockSpec((2, None, 1, T), lambda p, i: (p, i, 0, 0))],
        out_shape=[jax.ShapeDtypeStruct((S, W), F32), jax.ShapeDtypeStruct((S, W), BF16),
                   jax.ShapeDtypeStruct((W, S), BF16),
                   jax.ShapeDtypeStruct((2 * npairs, nq, 1, T), F32)],
        scratch_shapes=[pltpu.VMEM((2, 1, T), F32), pltpu.VMEM((2, LANES, T), F32)],
        compiler_params=_cparams(("parallel", "parallel")),
    )(proj, proj, vt, proj)


def _sb_bwd_t(proj, kt, do, tot, S, npairs, name):
    T = ATT_T
    nq = S // T
    nb = T // LANES
    scale = 64 ** -0.5

    def body(q_ref, k_ref, kt_ref, v_ref, do_ref, st_ref, dq_ref, dk_ref, dv_ref, dqt_ref, pre_ref, gpre_ref):
        i = pl.program_id(1)

        @pl.when(i == 0)
        def _():
            dk_ref[...] = jnp.zeros_like(dk_ref)
            dv_ref[...] = jnp.zeros_like(dv_ref)

        m0 = lax.broadcasted_iota(jnp.int32, (1, LANES), 1) < 64
        top = lax.broadcasted_iota(jnp.int32, (LANES, 1), 0) < 64
        key = lax.broadcasted_iota(jnp.int32, (T, LANES), 0)
        qrow = lax.broadcasted_iota(jnp.int32, (T, LANES), 1)
        r = lax.broadcasted_iota(jnp.int32, (T, T), 0)
        c = lax.broadcasted_iota(jnp.int32, (T, T), 1)
        upto = (c <= r).astype(BF16)
        left = (c < r).astype(BF16)
        qh = _head_q("sb", q_ref, m0, scale)
        dov = do_ref[...]
        doh = [jnp.where(m0, dov, 0.0).astype(BF16), jnp.where(m0, 0.0, dov).astype(BF16)]
        tot_h = [st_ref[0], st_ref[1]]
        dqt_ref[...] = jnp.zeros_like(dqt_ref)
        pre_ref[...] = jnp.zeros_like(pre_ref)
        gpre_ref[...] = jnp.zeros_like(gpre_ref)
        chains = [(h, b) for h in range(2) for b in range(nb)]

        def tiles(js, masked):
            starts = [pl.multiple_of(j * T, T) for j in js]
            zss, dwss = [], []
            for start in starts:
                vb = v_ref[pl.ds(start, T), :].astype(BF16)
                kb = k_ref[pl.ds(start, T), :].astype(BF16)
                zss.append(_split_blocks([_dot_nt(kb, qh[h]) for h in range(2)]))
                dwss.append(_split_blocks([_dot_nt(vb, doh[h]) for h in range(2)]))
            lass, sums, hiss, loss = [], [], [], []
            for zs in zss:
                las, sm, his, los = [], [], [], []
                for (h, b), z in zip(chains, zs):
                    lk, la = _softplus_parts(z)
                    if masked:
                        lk = jnp.where(key < qrow + b * LANES, lk, 0.0)
                    hi, lo = _split2(lk)
                    las.append(la)
                    sm.append(jnp.sum(lk, axis=0, keepdims=True))
                    his.append(hi)
                    los.append(lo)
                lass.append(las)
                sums.append(sm)
                hiss.append(_join_blocks(his, nb))
                loss.append(_join_blocks(los, nb))
            pcss = [_split_blocks([_dot(upto, hi) + _dot(upto, lo) for hi, lo in zip(his, los)])
                    for his, los in zip(hiss, loss)]
            wss, gss = [], []
            for las, sm, pcs, dws in zip(lass, sums, pcss, dwss):
                ws, gs = [], []
                for (h, b), la, s, pc, dw in zip(chains, las, sm, pcs, dws):
                    lanes = slice(b * LANES, (b + 1) * LANES)
                    w = jnp.exp(la + ((tot_h[h][:, lanes] - pre_ref[h, :, lanes]) - pc))
                    if masked:
                        w = jnp.where(key < qrow + b * LANES, w, 0.0)
                    ws.append(w.astype(BF16))
                    gs.append(dw * w)
                    pre_ref[h, :, lanes] += s
                wss.append(_join_blocks(ws, nb))
                gss.append(gs)
            gcss = [_split_blocks([_dot(left, g) for g in _join_blocks([g.astype(BF16) for g in gs], nb)]) for gs in gss]
            dzss = []
            for las, gs, gcs in zip(lass, gss, gcss):
                dzs = []
                for (h, b), la, g, gc in zip(chains, las, gs, gcs):
                    lanes = slice(b * LANES, (b + 1) * LANES)
                    dz = g - (g + (gpre_ref[h, :, lanes] + gc)) * jnp.exp(la)
                    if masked:
                        dz = jnp.where(key < qrow + b * LANES, dz, 0.0)
                    dzs.append(dz.astype(BF16))
                    gpre_ref[h, :, lanes] += jnp.sum(g, axis=0, keepdims=True)
                dzss.append(_join_blocks(dzs, nb))
            for j, start, ws, dzs in zip(js, starts, wss, dzss):
                kt = kt_ref[j]
                dkc = dvc = None
                for h in range(2):
                    dkh = _dot(dzs[h], qh[h])
                    dvh = _dot(ws[h], doh[h])
                    dkc = dkh if dkc is None else dkc + dkh
                    dvc = dvh if dvc is None else dvc + dvh
                    dqt_ref[h] += _dot(kt, dzs[h])
                dk_ref[pl.ds(start, T), :] += dkc
                dv_ref[pl.ds(start, T), :] += dvc

        _loop_tiles(i, tiles)
        tiles([i], True)
        dq_ref[...] = jnp.where(top, dqt_ref[0], dqt_ref[1]).T * scale

    qs, ks, vs, _ = _att_specs("sb", S, T)
    W = npairs * LANES
    return pl.pallas_call(
        body, name=name, grid=(npairs, nq),
        in_specs=[qs, ks, pl.BlockSpec((None, nq, LANES, T), lambda p, i: (p, 0, 0, 0)), vs,
                  pl.BlockSpec((T, LANES), lambda p, i: (i, p)),
                  pl.BlockSpec((2, None, 1, T), lambda p, i: (p, i, 0, 0))],
        out_specs=[pl.BlockSpec((T, LANES), lambda p, i: (i, p)), pl.BlockSpec((S, LANES), lambda p, i: (0, p)),
                   pl.BlockSpec((S, LANES), lambda p, i: (0, p))],
        out_shape=[jax.ShapeDtypeStruct((S, W), F32)] * 3,
        scratch_shapes=[pltpu.VMEM((2, LANES, T), F32), pltpu.VMEM((2, 1, T), F32), pltpu.VMEM((2, 1, T), F32)],
        compiler_params=_cparams(("parallel", "arbitrary")),
    )(proj, proj, kt, proj, do, tot)


def _pad_w0(w):
    z = lambda n: jnp.zeros((w.shape[0], n), w.dtype)
    return jnp.concatenate([w[:, 2048:2432], w[:, 2432:2688], z(64), w[:, 2688:2720], z(32),
                            w[:, 1536:2048], w[:, 2720:3232], w[:, 0:512], w[:, 512:1024], w[:, 1024:1536]], axis=1)


def _unpad_w0(wp):
    return jnp.concatenate([wp[:, L0_SBQ:L0_SBQ + 512], wp[:, L0_SBK:L0_SBK + 512], wp[:, L0_SBV:L0_SBV + 512],
                            wp[:, L0_SBG:L0_SBG + 512], wp[:, 0:384], wp[:, 384:640], wp[:, 704:736],
                            wp[:, L0_MLG:L0_MLG + 512]], axis=1)


def _pad_wq(w):
    return jnp.pad(w.reshape(384, 8, 96), ((0, 0), (0, 0), (0, 32))).reshape(384, 1024)


def _unpad_wq(wp):
    return wp.reshape(384, 8, 128)[:, :, :96].reshape(384, 768)


def _pad_wkv(w):
    w3 = w.reshape(256, 8, 128)
    k = jnp.pad(w3[:, :, :64], ((0, 0), (0, 0), (0, 64))).reshape(256, 1024)
    return jnp.concatenate([k, w3[:, :, 64:].reshape(256, 512)], axis=1)


def _unpad_wkv(wp):
    k = wp[:, :1024].reshape(256, 8, 128)[:, :, :64]
    v = wp[:, 1024:].reshape(256, 8, 64)
    return jnp.concatenate([k, v], axis=-1).reshape(256, 1024)


def _pad_w1(w):
    return jnp.concatenate([w, jnp.zeros((w.shape[0], L1_WIDTH - ODD_IN_WIDTH), w.dtype)], axis=1)


def _local_step(x, positions, target, g, w0p, wqp, wkvp, wo0, w1p, wo1):
    S = x.shape[0]
    nq = S // ATT_T
    invf = ROPE_THETA ** (-jnp.arange(0, MLA_ROPE_DIM, 2, dtype=F32) / MLA_ROPE_DIM)
    invf = jnp.concatenate([jnp.zeros((64,), F32), invf, invf, jnp.zeros((32,), F32)]).reshape(1, LANES)
    cosT, s1T, s2T = _rope_tables(positions.reshape(S, 1), invf, "rope_tables")
    bfp = jnp.pad(g["l1_b_f"], ((0, 0), (0, LANES - FOX_HEADS)))

    proj0, h0t = _norm_matmul(x, g["l0_pre_g"], w0p, "l0_in_proj")
    qm, km, vm, qnt, cnt = _mla_prep(proj0, g["l0_q_a_g"], g["l0_kv_a_g"], wqp, wkvp, cosT, s1T, s2T, "mla_prep")
    sb_vt = _transpose_tiles(proj0, L0_SBV, 4, LANES, 2, "sb_vt")
    sb_kt = _transpose_tiles(proj0, L0_SBK, 4, LANES, 2, "sb_kt")
    o_sb, og_sb, ogt_sb, tot_sb = _sb_fwd_t(proj0, sb_vt, S, 4, "sb_fwd")
    vmt = _transpose_tiles(vm, 0, 4, LANES, 4, "mla_vt")
    kmt = _transpose_tiles(km, 0, 4, 2 * LANES, 4, "mla_kt")
    o_ml, og_ml, ogt_ml, lse_ml = _softmax_fwd("mla", (qm, km, vmt, proj0), None, S, 4, "mla_fwd")
    y0, x1 = _out_proj(og_sb, og_ml, 0, 0, wo0, x, g["l0_post_g"], None, "l0_out_proj")

    proj1, h1t = _norm_matmul(x1, g["l1_pre_g"], w1p, "l1_in_proj")
    cfx = _fox_prep(proj1, bfp, "fox_prep")
    c16 = cfx[:, :FOX_HEADS].T
    c_col = jnp.broadcast_to(c16[:, :, None], (FOX_HEADS, S, LANES))
    vt1 = _transpose_tiles(proj1, L1_V, 8, LANES, 8, "fox_vt")
    kt1 = _transpose_tiles(proj1, L1_K, 8, LANES, 8, "fox_kt")
    o_fx, og_fx, ogt_fx, lse_fx = _softmax_fwd("fox", (proj1, proj1, vt1, proj1), c_col, S, 8, "fox_fwd")
    y1, dx2, lsum = _out_proj(og_fx, og_fx, 0, 1, wo1, x1, g["l1_post_g"], target, "l1_out_proj")

    dy1, do1, dgate1, d_post1 = _out_proj_bwd(dx2, y1, g["l1_post_g"], wo1, proj1, (L1_G, L1_G + 512), o_fx, o_fx, 0, 1, "l1_out_bwd")
    dwo1 = _matmul_t(ogt_fx, dy1, "l1_dw_out")
    dq1, dk1, dv1, dck, dcq = _softmax_bwd_t("fox", proj1, proj1, kt1, proj1, do1, 0, o_fx, lse_fx, c_col, S, 8,
                                             "fox_bwd")
    dc = jnp.pad((dcq.reshape(FOX_HEADS, S) - dck.reshape(FOX_HEADS, S)).T, ((0, 0), (0, LANES - FOX_HEADS)))
    df, d_bf = _fox_prep_bwd(dc, proj1, bfp, "fox_prep_bwd")
    dproj1 = jnp.concatenate([dq1, dk1, dv1, dgate1, df], axis=1)
    dx1, d_pre1 = _in_proj_bwd(dproj1, w1p, x1, g["l1_pre_g"], dx2, "l1_in_bwd")
    dw1p = _matmul_t(h1t, dproj1, "l1_dw_in")

    dy0, do0, dgate0, d_post0 = _out_proj_bwd(dx1, y0, g["l0_post_g"], wo0, proj0, (L0_SBG, L0_MLG), o_sb, o_ml, 0, 0,
                                              "l0_out_bwd")
    dwo0 = jnp.concatenate([_matmul_t(ogt_sb, dy0, "l0_dw_out_sb"), _matmul_t(ogt_ml, dy0, "l0_dw_out_mla")], axis=0)
    dsq, dsk, dsv = _sb_bwd_t(proj0, sb_kt, do0, tot_sb, S, 4, "sb_bwd")
    dqm, dkm, dvm = _softmax_bwd_t("mla", qm, km, kmt, vm, do0, 4, o_ml, lse_ml, None, S, 4, "mla_bwd")
    dprep, dqb, dkvb, d_qag, d_kvag = _mla_prep_bwd(dqm, dkm, dvm, proj0, g["l0_q_a_g"], g["l0_kv_a_g"], wqp, wkvp,
                                                    cosT, s1T, s2T, "mla_prep_bwd")
    dwqp = _matmul_t(qnt, dqb, "l0_dw_qb")
    dwkvp = _matmul_t(cnt, dkvb, "l0_dw_kvb")
    dproj0 = jnp.concatenate([dprep, dgate0[:, :512], dgate0[:, 512:], dsq, dsk, dsv], axis=1)
    dx0, d_pre0 = _in_proj_bwd(dproj0, w0p, x, g["l0_pre_g"], dx1, "l0_in_bwd")
    dw0p = _matmul_t(h0t, dproj0, "l0_dw_in")

    grads = {
        "l0_pre_g": d_pre0, "l0_post_g": d_post0, "l0_w_in": dw0p, "l0_q_a_g": d_qag, "l0_w_q_b": dwqp,
        "l0_kv_a_g": d_kvag, "l0_w_kv_b": dwkvp, "l0_w_out": dwo0, "l1_pre_g": d_pre1, "l1_post_g": d_post1,
        "l1_w_in": dw1p, "l1_b_f": d_bf[:, :FOX_HEADS], "l1_w_out": dwo1,
    }
    return lsum, dx0, grads


_ANY = pl.BlockSpec(memory_space=pl.ANY)


def _place():
    return lax.axis_index("x"), lax.axis_index("y"), lax.axis_index("c")


def _other_chips(x, y):
    return [(1 - x, y), (x, 1 - y), (1 - x, 1 - y)]


def _half(c):
    return pl.ds(c * PACK_HALF, PACK_HALF)


def _weight_gather(pack):
    def body(p_ref, out_ref, send_sems, recv_sems):
        x, y, c = _place()
        sibling = (x, y, 1 - c)
        chips = _other_chips(x, y)

        def blk(chip, cc):
            return out_ref.at[2 * chip[0] + chip[1], _half(cc)]

        def copy(k, src, dst, to):
            return pltpu.make_async_remote_copy(src_ref=src, dst_ref=dst, send_sem=send_sems.at[k],
                                                recv_sem=recv_sems.at[k], device_id=to, device_id_type=MESH)

        first = [copy(j, p_ref.at[_half(c)], blk((x, y), c), (*chip, c)) for j, chip in enumerate(chips)]
        for cp in first:
            cp.start()
        passed = [copy(3 + j, blk(chip, c), blk(chip, c), sibling) for j, chip in enumerate(chips)]
        for j, chip in enumerate(chips):
            copy(j, blk(chip, c), blk(chip, c), (x, y, c)).wait_recv()
            passed[j].start()
        for j, chip in enumerate(chips):
            copy(3 + j, blk(chip, 1 - c), blk(chip, 1 - c), (x, y, c)).wait_recv()
        for cp in first + passed:
            cp.wait_send()

    return pl.pallas_call(
        body, name="weight_gather", in_specs=[_ANY], out_specs=_ANY,
        out_shape=jax.ShapeDtypeStruct((4,) + pack.shape, pack.dtype),
        scratch_shapes=[pltpu.SemaphoreType.DMA((6,)), pltpu.SemaphoreType.DMA((6,))],
    )(pack)


GRAD_TR = 2048


def _grad_core_exchange(p):
    def body(p_ref, recv_ref, send_sems, recv_sems):
        x, y, c = _place()
        give = [pltpu.make_async_remote_copy(src_ref=p_ref.at[j, _half(1 - c)], dst_ref=recv_ref.at[j],
                                             send_sem=send_sems.at[j], recv_sem=recv_sems.at[j],
                                             device_id=(x, y, 1 - c), device_id_type=MESH) for j in range(4)]
        for cp in give:
            cp.start()
        for cp in give:
            cp.wait()

    return pl.pallas_call(
        body, name="grad_core_exchange", in_specs=[_ANY], out_specs=_ANY,
        out_shape=jax.ShapeDtypeStruct((4, PACK_HALF, LANES), p.dtype),
        scratch_shapes=[pltpu.SemaphoreType.DMA((4,)), pltpu.SemaphoreType.DMA((4,))],
    )(p)


def _grad_add_cores(p, theirs, c1):
    tr = GRAD_TR

    def body(c_ref, a_ref, b_ref, o_ref):
        o_ref[...] = a_ref[...] + b_ref[...]

    spec = pl.BlockSpec((None, tr, LANES), lambda j, r, c: (j, r, 0))
    grid_spec = pltpu.PrefetchScalarGridSpec(
        num_scalar_prefetch=1, grid=(4, PACK_HALF // tr),
        in_specs=[pl.BlockSpec((None, None, tr, LANES), lambda j, r, c: (j, c[0], r, 0)), spec], out_specs=spec)
    return pl.pallas_call(
        body, name="grad_add_cores", grid_spec=grid_spec, out_shape=jax.ShapeDtypeStruct(theirs.shape, theirs.dtype),
        compiler_params=_cparams(("parallel", "parallel")),
    )(c1, p.reshape(4, 2, PACK_HALF, LANES), theirs)


def _grad_chip_exchange(q):
    def body(q_ref, out_ref, send_sems, recv_sems):
        x, y, c = _place()
        me = 2 * x + y
        chips = _other_chips(x, y)
        sends = [pltpu.make_async_remote_copy(src_ref=q_ref.at[2 * chip[0] + chip[1]], dst_ref=out_ref.at[me],
                                              send_sem=send_sems.at[j], recv_sem=recv_sems.at[j],
                                              device_id=(*chip, c), device_id_type=MESH) for j, chip in enumerate(chips)]
        for cp in sends:
            cp.start()
        for j, chip in enumerate(chips):
            slot = out_ref.at[2 * chip[0] + chip[1]]
            pltpu.make_async_remote_copy(src_ref=slot, dst_ref=slot, send_sem=send_sems.at[j], recv_sem=recv_sems.at[j],
                                         device_id=(x, y, c), device_id_type=MESH).wait_recv()
        for cp in sends:
            cp.wait_send()

    return pl.pallas_call(
        body, name="grad_chip_exchange", in_specs=[_ANY], out_specs=_ANY,
        out_shape=jax.ShapeDtypeStruct(q.shape, q.dtype),
        scratch_shapes=[pltpu.SemaphoreType.DMA((3,)), pltpu.SemaphoreType.DMA((3,))],
    )(q)


def _grad_add_chips(q, slots, me1):
    tr = GRAD_TR

    def body(me_ref, own_ref, s0, s1, s2, s3, o_ref):
        me = me_ref[0]
        t = [jnp.where(me == j, own_ref[...], s[...]) for j, s in enumerate((s0, s1, s2, s3))]
        o_ref[...] = ((t[0] + t[1]) + t[2]) + t[3]

    def slot_spec(j):
        return pl.BlockSpec((None, tr, LANES), lambda r, me: (jnp.where(me[0] == j, (j + 1) % 4, j), r, 0))

    grid_spec = pltpu.PrefetchScalarGridSpec(
        num_scalar_prefetch=1, grid=(PACK_HALF // tr,),
        in_specs=[pl.BlockSpec((None, tr, LANES), lambda r, me: (me[0], r, 0))] + [slot_spec(j) for j in range(4)],
        out_specs=pl.BlockSpec((tr, LANES), lambda r, me: (r, 0)))
    return pl.pallas_call(
        body, name="grad_add_chips", grid_spec=grid_spec, out_shape=jax.ShapeDtypeStruct(q.shape[1:], q.dtype),
        compiler_params=_cparams(("parallel",)),
    )(me1, q, slots, slots, slots, slots)


def _grad_core_gather(t):
    def body(t_ref, out_ref, send_sem, recv_sem):
        x, y, c = _place()
        give = pltpu.make_async_remote_copy(src_ref=t_ref, dst_ref=out_ref, send_sem=send_sem, recv_sem=recv_sem,
                                            device_id=(x, y, 1 - c), device_id_type=MESH)
        give.start()
        give.wait()

    return pl.pallas_call(
        body, name="grad_core_gather", in_specs=[_ANY], out_specs=_ANY,
        out_shape=jax.ShapeDtypeStruct(t.shape, t.dtype),
        scratch_shapes=[pltpu.SemaphoreType.DMA, pltpu.SemaphoreType.DMA],
    )(t)


def _small_allreduce(sp):
    def body(sp_ref, out_ref, gath_ref, send_sems, recv_sems):
        x, y, c = _place()
        me = 4 * x + 2 * y + c
        gath_ref[me] = sp_ref[...]
        peers = []
        for k in range(1, 8):
            px = 1 - x if k & 4 else x
            py = 1 - y if k & 2 else y
            pc = 1 - c if k & 1 else c
            peers.append((px, py, pc))
        sends = [pltpu.make_async_remote_copy(src_ref=sp_ref, dst_ref=gath_ref.at[me], send_sem=send_sems.at[k],
                                              recv_sem=recv_sems.at[k], device_id=peer, device_id_type=MESH)
                 for k, peer in enumerate(peers)]
        for cp in sends:
            cp.start()
        for k, (px, py, pc) in enumerate(peers):
            slot = gath_ref.at[4 * px + 2 * py + pc]
            pltpu.make_async_remote_copy(src_ref=slot, dst_ref=slot, send_sem=send_sems.at[k], recv_sem=recv_sems.at[k],
                                         device_id=(x, y, c), device_id_type=MESH).wait_recv()
        for cp in sends:
            cp.wait_send()
        tot = gath_ref[0]
        for d in range(1, 8):
            tot = tot + gath_ref[d]
        out_ref[...] = tot

    vm = pl.BlockSpec(memory_space=pltpu.VMEM)
    return pl.pallas_call(
        body, name="small_allreduce", in_specs=[vm], out_specs=vm, out_shape=jax.ShapeDtypeStruct(sp.shape, sp.dtype),
        scratch_shapes=[pltpu.VMEM((8,) + sp.shape, sp.dtype), pltpu.SemaphoreType.DMA((7,)), pltpu.SemaphoreType.DMA((7,))],
    )(sp)


def _adamw_update(w, gv, m, v):
    mn = ADAM_B1 * m + (1.0 - ADAM_B1) * gv
    vn = ADAM_B2 * v + (1.0 - ADAM_B2) * (gv * gv)
    m_hat = mn / (1.0 - ADAM_B1 ** ADAM_STEP)
    v_hat = vn / (1.0 - ADAM_B2 ** ADAM_STEP)
    return -ADAM_LR * (m_hat / (jnp.sqrt(v_hat) + ADAM_EPS) + ADAM_WD * w), mn, vn


def _adamw(w, g, m, v, name):
    rows = w.shape[0]

    def body(w_ref, g_ref, m_ref, v_ref, d_ref, mo_ref, vo_ref):
        d_ref[...], mo_ref[...], vo_ref[...] = _adamw_update(w_ref[...], g_ref[...], m_ref[...], v_ref[...])

    spec = pl.BlockSpec((rows, LANES), lambda r: (0, 0))
    shp = jax.ShapeDtypeStruct(w.shape, F32)
    return pl.pallas_call(
        body, name=name, grid=(1,), in_specs=[spec] * 4, out_specs=[spec] * 3, out_shape=[shp] * 3,
        compiler_params=_cparams(("arbitrary",)),
    )(w, g, m, v)


def _adamw_mats(w, g_mine, g_theirs, m, v, c1):
    tr = GRAD_TR
    nb = PACK_HALF // tr

    def body(c_ref, w_ref, a_ref, b_ref, m_ref, v_ref, g_ref, d_ref, mo_ref, vo_ref):
        gv = jnp.where(pl.program_id(0) == c_ref[0], a_ref[...], b_ref[...])
        g_ref[...] = gv
        d_ref[...], mo_ref[...], vo_ref[...] = _adamw_update(w_ref[...], gv, m_ref[...], v_ref[...])

    full = pl.BlockSpec((tr, LANES), lambda h, r, c: (h * nb + r, 0))
    half = pl.BlockSpec((tr, LANES), lambda h, r, c: (r, 0))
    grid_spec = pltpu.PrefetchScalarGridSpec(num_scalar_prefetch=1, grid=(2, nb),
                                             in_specs=[full, half, half, full, full], out_specs=[full] * 4)
    shp = jax.ShapeDtypeStruct(w.shape, F32)
    return pl.pallas_call(
        body, name="adamw_mats", grid_spec=grid_spec, out_shape=[shp] * 4,
        compiler_params=_cparams(("parallel", "parallel")),
    )(c1, w, g_mine, g_theirs, m, v)


MAT_NAMES = ("l0_w_in", "l0_w_q_b", "l0_w_kv_b", "l0_w_out", "l1_w_in", "l1_w_out")
VEC_NAMES = ("l0_pre_g", "l0_post_g", "l0_q_a_g", "l0_kv_a_g", "l1_pre_g", "l1_post_g", "l1_b_f")
WEIGHT_NAMES = ("l0_pre_g", "l0_post_g", "l0_w_in", "l0_q_a_g", "l0_w_q_b", "l0_kv_a_g", "l0_w_kv_b", "l0_w_out",
                "l1_pre_g", "l1_post_g", "l1_w_in", "l1_b_f", "l1_w_out")
MAT_SHARD = {"l0_w_in": (1024, 808), "l0_w_q_b": (384, 192), "l0_w_kv_b": (256, 256), "l0_w_out": (256, 1024),
             "l1_w_in": (1024, 1028), "l1_w_out": (256, 1024)}
ROW_SHARDED = ("l0_w_out", "l1_w_out")
VEC_LEN = {"l0_pre_g": 1024, "l0_post_g": 1024, "l0_q_a_g": 384, "l0_kv_a_g": 256, "l1_pre_g": 1024,
           "l1_post_g": 1024, "l1_b_f": 16}


def _mat_rows(n):
    r, c = MAT_SHARD[n]
    return r * c // LANES


def _pack_shards(shards):
    parts = [shards[n].reshape(_mat_rows(n), LANES) for n in MAT_NAMES]
    used = sum(_mat_rows(n) for n in MAT_NAMES)
    parts.append(jnp.zeros((PACK_ROWS - used, LANES), parts[0].dtype))
    return jnp.concatenate(parts, axis=0)


def _unpack_shards(pack):
    out, at = {}, 0
    for n in MAT_NAMES:
        out[n] = pack[..., at:at + _mat_rows(n), :].reshape(pack.shape[:-2] + MAT_SHARD[n])
        at += _mat_rows(n)
    return out


def _join_shards(n, s):
    if n in ROW_SHARDED:
        return s.reshape(4 * s.shape[1], s.shape[2])
    return s.transpose(1, 0, 2).reshape(s.shape[1], 4 * s.shape[2])


def _cut_shards(n, w):
    r, c = MAT_SHARD[n]
    if n in ROW_SHARDED:
        return w.reshape(4, r, c)
    return w.reshape(r, 4, c).transpose(1, 0, 2)


def _pack_vecs(vecs):
    parts = []
    for n in VEC_NAMES:
        v = vecs[n].reshape(-1)
        parts.append(jnp.pad(v, (0, VEC_ROWS * LANES - v.shape[0])).reshape(VEC_ROWS, LANES))
    return jnp.concatenate(parts, axis=0)


def _unpack_vecs(pack):
    return {n: pack[k * VEC_ROWS:(k + 1) * VEC_ROWS].reshape(-1)[:VEC_LEN[n]] for k, n in enumerate(VEC_NAMES)}


def kernel(x, positions, l0_pre_g, l0_post_g, l0_w_in, l0_q_a_g, l0_w_q_b, l0_kv_a_g, l0_w_kv_b, l0_w_out, l1_pre_g, l1_post_g, l1_w_in, l1_b_f, l1_w_out, loss_target, m_l0_pre_g, m_l0_post_g, m_l0_w_in, m_l0_q_a_g, m_l0_w_q_b, m_l0_kv_a_g, m_l0_w_kv_b, m_l0_w_out, m_l1_pre_g, m_l1_post_g, m_l1_w_in, m_l1_b_f, m_l1_w_out, v_l0_pre_g, v_l0_post_g, v_l0_w_in, v_l0_q_a_g, v_l0_w_q_b, v_l0_kv_a_g, v_l0_w_kv_b, v_l0_w_out, v_l1_pre_g, v_l1_post_g, v_l1_w_in, v_l1_b_f, v_l1_w_out):
    w = dict(l0_pre_g=l0_pre_g, l0_post_g=l0_post_g, l0_w_in=l0_w_in, l0_q_a_g=l0_q_a_g, l0_w_q_b=l0_w_q_b,
             l0_kv_a_g=l0_kv_a_g, l0_w_kv_b=l0_w_kv_b, l0_w_out=l0_w_out, l1_pre_g=l1_pre_g, l1_post_g=l1_post_g,
             l1_w_in=l1_w_in, l1_b_f=l1_b_f, l1_w_out=l1_w_out)
    m = dict(l0_pre_g=m_l0_pre_g, l0_post_g=m_l0_post_g, l0_w_in=m_l0_w_in, l0_q_a_g=m_l0_q_a_g, l0_w_q_b=m_l0_w_q_b,
             l0_kv_a_g=m_l0_kv_a_g, l0_w_kv_b=m_l0_w_kv_b, l0_w_out=m_l0_w_out, l1_pre_g=m_l1_pre_g,
             l1_post_g=m_l1_post_g, l1_w_in=m_l1_w_in, l1_b_f=m_l1_b_f, l1_w_out=m_l1_w_out)
    v = dict(l0_pre_g=v_l0_pre_g, l0_post_g=v_l0_post_g, l0_w_in=v_l0_w_in, l0_q_a_g=v_l0_q_a_g, l0_w_q_b=v_l0_w_q_b,
             l0_kv_a_g=v_l0_kv_a_g, l0_w_kv_b=v_l0_w_kv_b, l0_w_out=v_l0_w_out, l1_pre_g=v_l1_pre_g,
             l1_post_g=v_l1_post_g, l1_w_in=v_l1_w_in, l1_b_f=v_l1_b_f, l1_w_out=v_l1_w_out)

    cx, cy, cc = _place()
    me1 = jnp.reshape(2 * cx + cy, (1,)).astype(jnp.int32)
    c1 = jnp.reshape(cc, (1,)).astype(jnp.int32)
    w_pack = _pack_shards(w)
    w_bf = w_pack.astype(BF16)
    gathered = lax.dynamic_update_slice(_weight_gather(w_bf), w_bf[None], (2 * cx + cy, 0, 0))
    gathered = _unpack_shards(gathered)
    full = {n: _join_shards(n, gathered[n]) for n in MAT_NAMES}
    gains = {n: w[n].reshape(1, -1) for n in VEC_NAMES}

    lsum, dx0, grads = _local_step(
        x[0], positions[0], loss_target[0], gains, _pad_w0(full["l0_w_in"]), _pad_wq(full["l0_w_q_b"]),
        _pad_wkv(full["l0_w_kv_b"]), full["l0_w_out"], _pad_w1(full["l1_w_in"]), full["l1_w_out"])

    gfull = {"l0_w_in": _unpad_w0(grads["l0_w_in"]), "l0_w_q_b": _unpad_wq(grads["l0_w_q_b"]),
             "l0_w_kv_b": _unpad_wkv(grads["l0_w_kv_b"]), "l0_w_out": grads["l0_w_out"],
             "l1_w_in": grads["l1_w_in"][:, :ODD_IN_WIDTH], "l1_w_out": grads["l1_w_out"]}
    parts = [_cut_shards(n, gfull[n]).reshape(4, _mat_rows(n), LANES) for n in MAT_NAMES]
    used = sum(_mat_rows(n) for n in MAT_NAMES)
    parts.append(jnp.zeros((4, PACK_ROWS - used, LANES), F32))
    g_pack = jnp.concatenate(parts, axis=1)
    q_cores = _grad_add_cores(g_pack, _grad_core_exchange(g_pack), c1)
    g_mine = _grad_add_chips(q_cores, _grad_chip_exchange(q_cores), me1)
    g_theirs = _grad_core_gather(g_mine)

    small = _small_allreduce(jnp.concatenate([_pack_vecs({n: grads[n] for n in VEC_NAMES}),
                                              lsum.reshape(D_MODEL // LANES, LANES)], axis=0))
    g_small = small[:SMALL_ROWS]
    loss = 0.5 * jnp.sum(small[SMALL_ROWS:]) / float(D_MODEL)

    g_shard, d_pack, m_pack, v_pack = _adamw_mats(w_pack, g_mine, g_theirs, _pack_shards(m), _pack_shards(v), c1)
    d_small, m_small, v_small = _adamw(_pack_vecs(w), g_small, _pack_vecs(m), _pack_vecs(v), "adamw_vecs")

    def unpack(mat_pack, vec_pack):
        out = dict(_unpack_shards(mat_pack))
        out.update(_unpack_vecs(vec_pack))
        return [out[n] for n in WEIGHT_NAMES]

    return (loss, dx0[None], *unpack(g_shard, g_small), *unpack(d_pack, d_small), *unpack(m_pack, m_small),
            *unpack(v_pack, v_small))
```

```python
import functools

import numpy as np
import jax
import jax.numpy as jnp
from jax import lax
from jax.experimental import pallas as pl
from jax.experimental.pallas import tpu as pltpu

F32 = jnp.float32
BF16 = jnp.bfloat16
MESH = pl.DeviceIdType.MESH

D_MODEL = 1024
RMS_EPS = 1e-6
ROPE_THETA = 10000.0
SB_WIDTH = 512
MLA_Q_LORA = 384
MLA_KV_LORA = 256
MLA_ROPE_DIM = 32
MLA_WIDTH = 512
FOX_WIDTH = 1024
FOX_HEADS = 16
EVEN_IN_WIDTH = 3232
ODD_IN_WIDTH = 4112

ADAM_LR = 0.001
ADAM_B1 = 0.9
ADAM_B2 = 0.999
ADAM_EPS = 1e-08
ADAM_WD = 0.01
ADAM_STEP = 10

LANES = 128
VMEM_LIMIT = 56 * 1024 * 1024

L0_PREP = 0
L0_PREP_W = 768
L0_SBG = 768
L0_MLG = 1280
L0_SBQ = 1792
L0_SBK = 2304
L0_SBV = 2816
L0_WIDTH = 3328
L1_Q = 0
L1_K = 1024
L1_V = 2048
L1_G = 3072
L1_F = 4096
L1_WIDTH = 4224

ATT_T = 256
ATT_GROUP = 4
NEG = -1e30

PACK_ROWS = 20480
PACK_HALF = PACK_ROWS // 2
VEC_ROWS = 8
SMALL_ROWS = 7 * VEC_ROWS


def _cparams(sem, **kw):
    return pltpu.CompilerParams(dimension_semantics=sem, vmem_limit_bytes=VMEM_LIMIT, **kw)


def _dot(a, b):
    return lax.dot_general(a, b, (((1,), (0,)), ((), ())), preferred_element_type=F32)


def _dot_nt(a, b):
    return lax.dot_general(a, b, (((1,), (1,)), ((), ())), preferred_element_type=F32)


def _dot_tn(a, b):
    return lax.dot_general(a, b, (((0,), (0,)), ((), ())), preferred_element_type=F32)


def _sigmoid(x):
    return 1.0 / (1.0 + jnp.exp(-x))


def _rstd(x):
    return lax.rsqrt(jnp.mean(x * x, axis=-1, keepdims=True) + RMS_EPS)


def _norm_bwd(x, g, dy):
    r = _rstd(x)
    xn = x * r
    dxn = dy * g
    dx = r * (dxn - xn * jnp.mean(dxn * xn, axis=-1, keepdims=True))
    return dx, dy * xn


def _split3(x):
    hi = x.astype(BF16)
    r1 = x - hi.astype(F32)
    mid = r1.astype(BF16)
    lo = (r1 - mid.astype(F32)).astype(BF16)
    return hi, mid, lo


def _wide_tile(n, cap=1792):
    return max(t for t in range(LANES, min(n, cap) + 1, LANES) if n % t == 0)


def _pick(n, cands):
    for c in cands:
        if n % c == 0:
            return c
    raise ValueError(n)


def _norm_matmul(x, g, w, name):
    S, K = x.shape
    N = w.shape[1]
    tm = _pick(S, (512, 256))
    tn = _wide_tile(N)

    def body(x_ref, g_ref, w_ref, o_ref, ht_ref, h_ref):
        @pl.when(pl.program_id(1) == 0)
        def _():
            xv = x_ref[...]
            h = (xv * _rstd(xv)) * g_ref[...]
            h_ref[...] = h.astype(BF16)
            ht_ref[...] = h.T.astype(BF16)
        o_ref[...] = _dot(h_ref[...], w_ref[...])

    return pl.pallas_call(
        body, name=name, grid=(S // tm, N // tn),
        in_specs=[pl.BlockSpec((tm, K), lambda i, j: (i, 0)),
                  pl.BlockSpec((1, K), lambda i, j: (0, 0)),
                  pl.BlockSpec((K, tn), lambda i, j: (0, j))],
        out_specs=[pl.BlockSpec((tm, tn), lambda i, j: (i, j)),
                   pl.BlockSpec((K, tm), lambda i, j: (0, i))],
        out_shape=[jax.ShapeDtypeStruct((S, N), F32), jax.ShapeDtypeStruct((K, S), BF16)],
        scratch_shapes=[pltpu.VMEM((tm, K), BF16)],
        compiler_params=_cparams(("parallel", "arbitrary")),
    )(x, g, w)


def _matmul_t(at, b, name):
    M, S = at.shape
    N = b.shape[1]
    tn = _wide_tile(N)
    ts = _pick(S, (512, 256))

    def body(a_ref, b_ref, o_ref):
        @pl.when(pl.program_id(1) == 0)
        def _():
            o_ref[...] = jnp.zeros_like(o_ref)
        o_ref[...] += _dot(a_ref[...], b_ref[...].astype(BF16))

    return pl.pallas_call(
        body, name=name, grid=(N // tn, S // ts),
        in_specs=[pl.BlockSpec((M, ts), lambda j, k: (0, k)),
                  pl.BlockSpec((ts, tn), lambda j, k: (k, j))],
        out_specs=pl.BlockSpec((M, tn), lambda j, k: (0, j)),
        out_shape=jax.ShapeDtypeStruct((M, N), F32),
        compiler_params=_cparams(("parallel", "arbitrary")),
    )(at, b)


def _in_proj_bwd(pieces, w, x, g, dx_up, name):
    S, K = x.shape
    N = w.shape[1]
    tm = _pick(S, (256,))
    offs = [off for off, _ in pieces]
    arrs = [a for _, a in pieces]

    def body(*refs):
        d_refs = refs[:len(arrs)]
        w_ref, x_ref, g_ref, u_ref, dx_ref, dg_ref = refs[len(arrs):]

        @pl.when(pl.program_id(0) == 0)
        def _():
            dg_ref[...] = jnp.zeros_like(dg_ref)

        acc = None
        for off, d_ref in zip(offs, d_refs):
            part = _dot_nt(d_ref[...].astype(BF16), w_ref[:, off:off + d_ref.shape[1]])
            acc = part if acc is None else acc + part
        dx, dgrow = _norm_bwd(x_ref[...], g_ref[...], acc)
        dx_ref[...] = u_ref[...] + dx
        dg_ref[...] += jnp.sum(dgrow, axis=0, keepdims=True)

    row = lambda i: (i, 0)
    fixed = lambda i: (0, 0)
    return pl.pallas_call(
        body, name=name, grid=(S // tm,),
        in_specs=[pl.BlockSpec((tm, a.shape[1]), row) for a in arrs] + [
            pl.BlockSpec((K, N), fixed), pl.BlockSpec((tm, K), row), pl.BlockSpec((1, K), fixed),
            pl.BlockSpec((tm, K), row)],
        out_specs=[pl.BlockSpec((tm, K), row), pl.BlockSpec((1, K), fixed)],
        out_shape=[jax.ShapeDtypeStruct((S, K), F32), jax.ShapeDtypeStruct((1, K), F32)],
        compiler_params=_cparams(("arbitrary",)),
    )(*arrs, w, x, g, dx_up)


def _out_proj(og_a, og_b, blk_a, blk_b, w, x, g, target, name):
    S = x.shape[0]
    D = x.shape[1]
    tm = _pick(S, (512, 256))
    with_loss = target is not None

    def body(*refs):
        if with_loss:
            a_ref, b_ref, wa_ref, wb_ref, x_ref, g_ref, t_ref, y_ref, o_ref, l_ref = refs
        else:
            a_ref, b_ref, wa_ref, wb_ref, x_ref, g_ref, y_ref, o_ref = refs
        y = _dot(a_ref[...], wa_ref[...]) + _dot(b_ref[...], wb_ref[...])
        y_ref[...] = y
        xn = x_ref[...] + (y * _rstd(y)) * g_ref[...]
        if with_loss:
            @pl.when(pl.program_id(0) == 0)
            def _():
                l_ref[...] = jnp.zeros_like(l_ref)
            d = xn - t_ref[...]
            o_ref[...] = d / float(D)
            l_ref[...] += jnp.sum(d * d, axis=0, keepdims=True)
        else:
            o_ref[...] = xn

    row = lambda i: (i, 0)
    in_specs = [pl.BlockSpec((tm, 512), lambda i: (i, blk_a)),
                pl.BlockSpec((tm, 512), lambda i: (i, blk_b)),
                pl.BlockSpec((512, D), lambda i: (0, 0)),
                pl.BlockSpec((512, D), lambda i: (1, 0)),
                pl.BlockSpec((tm, D), row),
                pl.BlockSpec((1, D), lambda i: (0, 0))]
    out_specs = [pl.BlockSpec((tm, D), row), pl.BlockSpec((tm, D), row)]
    out_shape = [jax.ShapeDtypeStruct((S, D), F32), jax.ShapeDtypeStruct((S, D), F32)]
    args = [og_a, og_b, w, w, x, g]
    if with_loss:
        in_specs.append(pl.BlockSpec((tm, D), row))
        out_specs.append(pl.BlockSpec((1, D), lambda i: (0, 0)))
        out_shape.append(jax.ShapeDtypeStruct((1, D), F32))
        args.append(target)
    return pl.pallas_call(
        body, name=name, grid=(S // tm,), in_specs=in_specs, out_specs=out_specs, out_shape=out_shape,
        compiler_params=_cparams(("arbitrary",)),
    )(*args)


def _out_proj_bwd(dx_up, y, g, w, proj, gate_offs, o_a, o_b, oblk_a, oblk_b, name):
    S, D = y.shape
    tm = _pick(S, (256,))
    gblk = [off // 256 + c for off in gate_offs for c in range(2)]

    def body(u_ref, y_ref, g_ref, w_ref, g0, g1, g2, g3, oa_ref, ob_ref, dy_ref, do_ref, dgate_ref, dg_ref):
        @pl.when(pl.program_id(0) == 0)
        def _():
            dg_ref[...] = jnp.zeros_like(dg_ref)
        dy, dgrow = _norm_bwd(y_ref[...], g_ref[...], u_ref[...])
        dg_ref[...] += jnp.sum(dgrow, axis=0, keepdims=True)
        dyb = dy.astype(BF16)
        dy_ref[...] = dyb
        dog = _dot_nt(dyb, w_ref[...])
        gates = (g0, g1, g2, g3)
        for c in range(4):
            gt = gates[c][...]
            sg = _sigmoid(gt)
            o_ref = oa_ref if c < 2 else ob_ref
            ov = o_ref[:, (c % 2) * 256:(c % 2 + 1) * 256]
            dc = dog[:, c * 256:(c + 1) * 256]
            do_ref[:, c * 256:(c + 1) * 256] = dc * (gt * sg)
            dgate_ref[:, c * 256:(c + 1) * 256] = dc * ov * (sg * (1.0 + gt * (1.0 - sg)))

    row = lambda i: (i, 0)
    gspec = lambda c: pl.BlockSpec((tm, 256), lambda i: (i, gblk[c]))
    return pl.pallas_call(
        body, name=name, grid=(S // tm,),
        in_specs=[pl.BlockSpec((tm, D), row), pl.BlockSpec((tm, D), row), pl.BlockSpec((1, D), lambda i: (0, 0)),
                  pl.BlockSpec((D, D), lambda i: (0, 0)),
                  gspec(0), gspec(1), gspec(2), gspec(3),
                  pl.BlockSpec((tm, 512), lambda i: (i, oblk_a)),
                  pl.BlockSpec((tm, 512), lambda i: (i, oblk_b))],
        out_specs=[pl.BlockSpec((tm, D), row), pl.BlockSpec((tm, D), row), pl.BlockSpec((tm, D), row),
                   pl.BlockSpec((1, D), lambda i: (0, 0))],
        out_shape=[jax.ShapeDtypeStruct((S, D), BF16), jax.ShapeDtypeStruct((S, D), F32),
                   jax.ShapeDtypeStruct((S, D), F32), jax.ShapeDtypeStruct((1, D), F32)],
        compiler_params=_cparams(("arbitrary",)),
    )(dx_up, y, g, w, proj, proj, proj, proj, o_a, o_b)


def _rope_tables(pos, invf, name):
    S = pos.shape[0]
    tm = _pick(S, (512, 256))

    def body(p_ref, f_ref, c_ref, s1_ref, s2_ref):
        lane = lax.broadcasted_iota(jnp.int32, (1, LANES), 1)
        ang = p_ref[...].astype(F32) * f_ref[...]
        c, s = jnp.cos(ang), jnp.sin(ang)
        c_ref[...] = jnp.where((lane >= 64) & (lane < 96), c, 1.0)
        s1_ref[...] = jnp.where((lane >= 64) & (lane < 80), -s, 0.0)
        s2_ref[...] = jnp.where((lane >= 80) & (lane < 96), s, 0.0)

    spec = pl.BlockSpec((tm, LANES), lambda i: (i, 0))
    return pl.pallas_call(
        body, name=name, grid=(S // tm,),
        in_specs=[pl.BlockSpec((tm, 1), lambda i: (i, 0)), pl.BlockSpec((1, LANES), lambda i: (0, 0))],
        out_specs=[spec, spec, spec],
        out_shape=[jax.ShapeDtypeStruct((S, LANES), F32)] * 3,
        compiler_params=_cparams(("parallel",)),
    )(pos, invf)


def _rope(x, c, s1, s2):
    return x * c + pltpu.roll(x, LANES - 16, 1) * s1 + pltpu.roll(x, 16, 1) * s2


def _rope_t(d, c, s1, s2):
    return d * c + pltpu.roll(d * s1, 16, 1) + pltpu.roll(d * s2, LANES - 16, 1)


def _mla_prep(proj, gq, gkv, wq, wkv, cosT, s1T, s2T, name):
    S = proj.shape[0]
    tm = _pick(S, (256,))

    def body(p_ref, gq_ref, gkv_ref, wq_ref, wkv_ref, c_ref, s1_ref, s2_ref, q_ref, k_ref, v_ref, qn_ref, cn_ref):
        qa = p_ref[:, 0:384]
        ckv = p_ref[:, 384:640]
        kr = p_ref[:, 640:768]
        qn32 = (qa * _rstd(qa)) * gq_ref[...]
        cn32 = (ckv * _rstd(ckv)) * gkv_ref[...]
        qn = qn32.astype(BF16)
        cn = cn32.astype(BF16)
        qn_ref[...] = qn32.T.astype(BF16)
        cn_ref[...] = cn32.T.astype(BF16)
        qb = _dot(qn, wq_ref[...])
        kvb = _dot(cn, wkv_ref[...])
        c, s1, s2 = c_ref[...], s1_ref[...], s2_ref[...]
        krr = _rope(kr, c, s1, s2)
        for h in range(8):
            sl = slice(h * LANES, (h + 1) * LANES)
            q_ref[:, sl] = _rope(qb[:, sl], c, s1, s2)
            k_ref[:, sl] = kvb[:, sl] + krr
        v_ref[...] = kvb[:, 1024:1536]

    row = lambda i: (i, 0)
    fixed = lambda i: (0, 0)
    tspec = pl.BlockSpec((tm, LANES), row)
    return pl.pallas_call(
        body, name=name, grid=(S // tm,),
        in_specs=[pl.BlockSpec((tm, L0_PREP_W), lambda i: (i, L0_PREP // L0_PREP_W)),
                  pl.BlockSpec((1, 384), fixed), pl.BlockSpec((1, 256), fixed),
                  pl.BlockSpec((384, 1024), fixed), pl.BlockSpec((256, 1536), fixed), tspec, tspec, tspec],
        out_specs=[pl.BlockSpec((tm, 1024), row), pl.BlockSpec((tm, 1024), row), pl.BlockSpec((tm, 512), row),
                   pl.BlockSpec((384, tm), lambda i: (0, i)), pl.BlockSpec((256, tm), lambda i: (0, i))],
        out_shape=[jax.ShapeDtypeStruct((S, 1024), F32), jax.ShapeDtypeStruct((S, 1024), F32),
                   jax.ShapeDtypeStruct((S, 512), F32), jax.ShapeDtypeStruct((384, S), BF16),
                   jax.ShapeDtypeStruct((256, S), BF16)],
        compiler_params=_cparams(("parallel",)),
    )(proj, gq, gkv, wq, wkv, cosT, s1T, s2T)


def _mla_prep_bwd(dq, dk, dv, proj, gq, gkv, wq, wkv, cosT, s1T, s2T, name):
    S = proj.shape[0]
    tm = _pick(S, (256,))

    def body(dq_ref, dk_ref, dv_ref, p_ref, gq_ref, gkv_ref, wq_ref, wkv_ref, c_ref, s1_ref, s2_ref,
             dp_ref, dqb_ref, dkvb_ref, dgq_ref, dgkv_ref):
        @pl.when(pl.program_id(0) == 0)
        def _():
            dgq_ref[...] = jnp.zeros_like(dgq_ref)
            dgkv_ref[...] = jnp.zeros_like(dgkv_ref)
        c, s1, s2 = c_ref[...], s1_ref[...], s2_ref[...]
        lane = lax.broadcasted_iota(jnp.int32, (1, LANES), 1)
        dkr = jnp.zeros((tm, LANES), F32)
        for h in range(8):
            sl = slice(h * LANES, (h + 1) * LANES)
            dqb_ref[:, sl] = _rope_t(dq_ref[:, sl], c, s1, s2).astype(BF16)
            dkh = dk_ref[:, sl]
            dkvb_ref[:, sl] = dkh.astype(BF16)
            dkr = dkr + dkh
        dkvb_ref[:, 1024:1536] = dv_ref[...].astype(BF16)
        dkr = jnp.where((lane >= 64) & (lane < 96), _rope_t(dkr, c, s1, s2), 0.0)
        dqn = _dot_nt(dqb_ref[...], wq_ref[...])
        dcn = _dot_nt(dkvb_ref[...], wkv_ref[...])
        dqa, gq_row = _norm_bwd(p_ref[:, 0:384], gq_ref[...], dqn)
        dckv, gkv_row = _norm_bwd(p_ref[:, 384:640], gkv_ref[...], dcn)
        dp_ref[:, 0:384] = dqa
        dp_ref[:, 384:640] = dckv
        dp_ref[:, 640:768] = dkr
        dgq_ref[...] += jnp.sum(gq_row, axis=0, keepdims=True)
        dgkv_ref[...] += jnp.sum(gkv_row, axis=0, keepdims=True)

    row = lambda i: (i, 0)
    fixed = lambda i: (0, 0)
    tspec = pl.BlockSpec((tm, LANES), row)
    return pl.pallas_call(
        body, name=name, grid=(S // tm,),
        in_specs=[pl.BlockSpec((tm, 1024), row), pl.BlockSpec((tm, 1024), row), pl.BlockSpec((tm, 512), row),
                  pl.BlockSpec((tm, L0_PREP_W), lambda i: (i, L0_PREP // L0_PREP_W)),
                  pl.BlockSpec((1, 384), fixed), pl.BlockSpec((1, 256), fixed),
                  pl.BlockSpec((384, 1024), fixed), pl.BlockSpec((256, 1536), fixed), tspec, tspec, tspec],
        out_specs=[pl.BlockSpec((tm, L0_PREP_W), row), pl.BlockSpec((tm, 1024), row), pl.BlockSpec((tm, 1536), row),
                   pl.BlockSpec((1, 384), fixed), pl.BlockSpec((1, 256), fixed)],
        out_shape=[jax.ShapeDtypeStruct((S, L0_PREP_W), F32), jax.ShapeDtypeStruct((S, 1024), BF16),
                   jax.ShapeDtypeStruct((S, 1536), BF16), jax.ShapeDtypeStruct((1, 384), F32),
                   jax.ShapeDtypeStruct((1, 256), F32)],
        compiler_params=_cparams(("arbitrary",)),
    )(dq, dk, dv, proj, gq, gkv, wq, wkv, cosT, s1T, s2T)


def _fox_prep(proj, bf, name):
    S = proj.shape[0]
    tm = _pick(S, (256,))

    def body(f_ref, b_ref, c_ref, carry_ref):
        @pl.when(pl.program_id(0) == 0)
        def _():
            carry_ref[...] = jnp.zeros_like(carry_ref)
        u = f_ref[...] + b_ref[...]
        lf = jnp.minimum(u, 0.0) - jnp.log(1.0 + jnp.exp(-jnp.abs(u)))
        r = lax.broadcasted_iota(jnp.int32, (tm, tm), 0)
        cidx = lax.broadcasted_iota(jnp.int32, (tm, tm), 1)
        tri = (cidx <= r).astype(BF16)
        hi, mid, lo = _split3(lf)
        c = carry_ref[...] + (_dot(tri, hi) + _dot(tri, mid) + _dot(tri, lo))
        c_ref[...] = c
        carry_ref[...] = c[tm - 1:tm, :]

    return pl.pallas_call(
        body, name=name, grid=(S // tm,),
        in_specs=[pl.BlockSpec((tm, LANES), lambda i: (i, L1_F // LANES)), pl.BlockSpec((1, LANES), lambda i: (0, 0))],
        out_specs=pl.BlockSpec((tm, LANES), lambda i: (i, 0)),
        out_shape=jax.ShapeDtypeStruct((S, LANES), F32),
        scratch_shapes=[pltpu.VMEM((1, LANES), F32)],
        compiler_params=_cparams(("arbitrary",)),
    )(proj, bf)


def _fox_prep_bwd(dc, proj, bf, name):
    S = proj.shape[0]
    tm = _pick(S, (256,))
    nb = S // tm

    def body(dc_ref, f_ref, b_ref, df_ref, db_ref, carry_ref):
        @pl.when(pl.program_id(0) == 0)
        def _():
            carry_ref[...] = jnp.zeros_like(carry_ref)
            db_ref[...] = jnp.zeros_like(db_ref)
        r = lax.broadcasted_iota(jnp.int32, (tm, tm), 0)
        cidx = lax.broadcasted_iota(jnp.int32, (tm, tm), 1)
        tri = (cidx >= r).astype(BF16)
        hi, mid, lo = _split3(dc_ref[...])
        dlf = carry_ref[...] + (_dot(tri, hi) + _dot(tri, mid) + _dot(tri, lo))
        carry_ref[...] = dlf[0:1, :]
        u = f_ref[...] + b_ref[...]
        e = jnp.exp(-jnp.abs(u))
        sneg = jnp.where(u >= 0.0, e, 1.0) / (1.0 + e)
        lane = lax.broadcasted_iota(jnp.int32, (1, LANES), 1)
        df = jnp.where(lane < FOX_HEADS, dlf * sneg, 0.0)
        df_ref[...] = df
        db_ref[...] += jnp.sum(df, axis=0, keepdims=True)

    return pl.pallas_call(
        body, name=name, grid=(nb,),
        in_specs=[pl.BlockSpec((tm, LANES), lambda i: (nb - 1 - i, 0)),
                  pl.BlockSpec((tm, LANES), lambda i: (nb - 1 - i, L1_F // LANES)),
                  pl.BlockSpec((1, LANES), lambda i: (0, 0))],
        out_specs=[pl.BlockSpec((tm, LANES), lambda i: (nb - 1 - i, 0)), pl.BlockSpec((1, LANES), lambda i: (0, 0))],
        out_shape=[jax.ShapeDtypeStruct((S, LANES), F32), jax.ShapeDtypeStruct((1, LANES), F32)],
        scratch_shapes=[pltpu.VMEM((1, LANES), F32)],
        compiler_params=_cparams(("arbitrary",)),
    )(dc, proj, bf)


def _att_specs(kind, S, T):
    if kind == "sb":
        qo, ko, vo, go = L0_SBQ // LANES, L0_SBK // LANES, L0_SBV // LANES, L0_SBG // LANES
    elif kind == "fox":
        qo, ko, vo, go = L1_Q // LANES, L1_K // LANES, L1_V // LANES, L1_G // LANES
    else:
        go = L0_MLG // LANES
        return (pl.BlockSpec((T, 256), lambda p, i: (i, p)), pl.BlockSpec((S, 256), lambda p, i: (0, p)),
                pl.BlockSpec((S, LANES), lambda p, i: (0, p)), pl.BlockSpec((T, LANES), lambda p, i: (i, go + p)))
    return (pl.BlockSpec((T, LANES), lambda p, i: (i, qo + p)), pl.BlockSpec((S, LANES), lambda p, i: (0, ko + p)),
            pl.BlockSpec((S, LANES), lambda p, i: (0, vo + p)), pl.BlockSpec((T, LANES), lambda p, i: (i, go + p)))


def _loop_tiles(n, tiles, order=lambda t: t):
    def group(g, carry):
        tiles([order(g * ATT_GROUP + u) for u in range(ATT_GROUP)], False)
        return carry

    def single(t, carry):
        tiles([order(t)], False)
        return carry

    lax.fori_loop(0, n // ATT_GROUP, group, 0)
    lax.fori_loop((n // ATT_GROUP) * ATT_GROUP, n, single, 0)


def _head_q(kind, q_ref, m0, scale):
    if kind == "mla":
        return [q_ref[:, 0:LANES].astype(BF16), q_ref[:, LANES:2 * LANES].astype(BF16)]
    qv = q_ref[...] * scale
    return [jnp.where(m0, qv, 0.0).astype(BF16), jnp.where(m0, 0.0, qv).astype(BF16)]


def _head_k(kind, k_ref, start, T):
    if kind == "mla":
        return [k_ref[pl.ds(start, T), 0:LANES].astype(BF16), k_ref[pl.ds(start, T), LANES:2 * LANES].astype(BF16)]
    kb = k_ref[pl.ds(start, T), :].astype(BF16)
    return [kb, kb]


def _transpose_tiles(src, col_off, n_out, cw, group, name):
    S = src.shape[0]
    T = ATT_T
    first = col_off // (group * cw)

    def body(x_ref, o_ref):
        for u in range(group):
            o_ref[u] = x_ref[:, u * cw:(u + 1) * cw].T.astype(BF16)

    return pl.pallas_call(
        body, name=name, grid=(S // T, n_out // group),
        in_specs=[pl.BlockSpec((T, group * cw), lambda j, g: (j, first + g))],
        out_specs=pl.BlockSpec((group, None, cw, T), lambda j, g: (g, j, 0, 0)),
        out_shape=jax.ShapeDtypeStruct((n_out, S // T, cw, T), BF16),
        compiler_params=_cparams(("parallel", "parallel")),
    )(src)


def _softmax_fwd(kind, qkvg, c_col, S, npairs, name):
    T = ATT_T
    nq = S // T
    fox = kind == "fox"
    scale = (96 if kind == "mla" else 64) ** -0.5

    def body(*refs):
        if fox:
            q_ref, k_ref, vt_ref, g_ref, cc_ref, o_ref, og_ref, ogt_ref, st_ref, m_ref, acc_ref = refs
        else:
            q_ref, k_ref, vt_ref, g_ref, o_ref, og_ref, ogt_ref, st_ref, m_ref, acc_ref = refs
        i = pl.program_id(1)
        m0 = lax.broadcasted_iota(jnp.int32, (1, LANES), 1) < 64
        top = lax.broadcasted_iota(jnp.int32, (LANES, 1), 0) < 64
        key = lax.broadcasted_iota(jnp.int32, (T, LANES), 0)
        qrow = lax.broadcasted_iota(jnp.int32, (T, LANES), 1)
        qh = _head_q(kind, q_ref, m0, scale)
        m_ref[...] = jnp.full(m_ref.shape, NEG, F32)
        acc_ref[...] = jnp.zeros(acc_ref.shape, F32)
        chains = [(h, b) for h in range(2) for b in range(T // LANES)]

        def tiles(js, masked):
            starts = [pl.multiple_of(j * T, T) for j in js]
            zss = []
            for start in starts:
                kh = _head_k(kind, k_ref, start, T)
                zss.append(_split_blocks([_dot_nt(kh[h], qh[h]) for h in range(2)]))
            pss, alss = [], []
            for start, zs in zip(starts, zss):
                ps, alphas = [], []
                for (h, b), z in zip(chains, zs):
                    lanes = slice(b * LANES, (b + 1) * LANES)
                    if kind == "mla":
                        z = z * scale
                    if fox:
                        z = z - cc_ref[h, pl.ds(start, T), :]
                    if masked:
                        z = jnp.where(key <= qrow + b * LANES, z, NEG)
                    m_prev = m_ref[h, :, lanes]
                    m_new = jnp.maximum(m_prev, jnp.max(z, axis=0, keepdims=True))
                    alphas.append(jnp.exp(m_prev - m_new))
                    ps.append(jnp.exp(z - m_new).astype(BF16))
                    m_ref[h, :, lanes] = m_new
                pss.append(_join_blocks(ps, T // LANES))
                alss.append(_join_blocks(alphas, T // LANES))
            for j, ps, alphas in zip(js, pss, alss):
                vt = vt_ref[j]
                vth = [jnp.where(top, vt, 1.0).astype(BF16), jnp.where(top, 1.0, vt).astype(BF16)]
                for h in range(2):
                    acc_ref[h] = alphas[h] * acc_ref[h] + _dot(vth[h], ps[h])

        _loop_tiles(i, tiles)
        tiles([i], True)
        acc = [acc_ref[0], acc_ref[1]]
        ot = jnp.concatenate([acc[0][0:64] / acc[0][64:128], acc[1][64:128] / acc[1][0:64]], axis=0)
        o = ot.T
        o_ref[...] = o
        gt = g_ref[...]
        og = o * (gt * _sigmoid(gt))
        og_ref[...] = og.astype(BF16)
        ogt_ref[...] = og.T.astype(BF16)
        st_ref[0] = m_ref[0] + jnp.log(acc[0][64:65])
        st_ref[1] = m_ref[1] + jnp.log(acc[1][0:1])

    qs, ks, _, gs = _att_specs(kind, S, T)
    in_specs = [qs, ks, pl.BlockSpec((None, nq, LANES, T), lambda p, i: (p, 0, 0, 0)), gs]
    args = list(qkvg)
    if fox:
        in_specs += [pl.BlockSpec((2, S, LANES), lambda p, i: (p, 0, 0))]
        args += [c_col]
    W = npairs * LANES
    return pl.pallas_call(
        body, name=name, grid=(npairs, nq), in_specs=in_specs,
        out_specs=[pl.BlockSpec((T, LANES), lambda p, i: (i, p)), pl.BlockSpec((T, LANES), lambda p, i: (i, p)),
                   pl.BlockSpec((LANES, T), lambda p, i: (p, i)),
                   pl.BlockSpec((2, None, 1, T), lambda p, i: (p, i, 0, 0))],
        out_shape=[jax.ShapeDtypeStruct((S, W), F32), jax.ShapeDtypeStruct((S, W), BF16),
                   jax.ShapeDtypeStruct((W, S), BF16),
                   jax.ShapeDtypeStruct((2 * npairs, nq, 1, T), F32)],
        scratch_shapes=[pltpu.VMEM((2, 1, T), F32), pltpu.VMEM((2, LANES, T), F32)],
        compiler_params=_cparams(("parallel", "parallel")),
    )(*args)


def _softmax_bwd(kind, qkv, do, do_off, o, lse, c_row, S, npairs, name):
    T = ATT_T
    nq = S // T
    fox = kind == "fox"
    mla = kind == "mla"
    scale = (96 if mla else 64) ** -0.5
    kw = 256 if mla else LANES

    def body(*refs):
        if fox:
            q_ref, k_ref, v_ref, do_ref, o_ref, st_ref, cr_ref, dq_ref, dk_ref, dv_ref, dc_ref, dcq_ref = refs
        else:
            q_ref, k_ref, v_ref, do_ref, o_ref, st_ref, dq_ref, dk_ref, dv_ref = refs
        i = pl.program_id(1)

        @pl.when(i == 0)
        def _():
            dk_ref[...] = jnp.zeros_like(dk_ref)
            dv_ref[...] = jnp.zeros_like(dv_ref)
            if fox:
                dc_ref[...] = jnp.zeros_like(dc_ref)

        m0 = lax.broadcasted_iota(jnp.int32, (1, LANES), 1) < 64
        causal = lax.broadcasted_iota(jnp.int32, (T, T), 1) <= lax.broadcasted_iota(jnp.int32, (T, T), 0)
        qh = _head_q(kind, q_ref, m0, scale)
        dov = do_ref[...]
        prod = dov * o_ref[...]
        dd = [jnp.sum(jnp.where(m0, prod, 0.0), axis=1, keepdims=True),
              jnp.sum(jnp.where(m0, 0.0, prod), axis=1, keepdims=True)]
        doh = [jnp.where(m0, dov, 0.0).astype(BF16), jnp.where(m0, 0.0, dov).astype(BF16)]
        lse_h = [st_ref[0], st_ref[1]]

        def tile(j, carry, masked):
            start = pl.multiple_of(j * T, T)
            vb = v_ref[pl.ds(start, T), :].astype(BF16)
            kh = _head_k(kind, k_ref, start, T)
            dqs = []
            dkc = []
            dvc = jnp.zeros((T, LANES), F32)
            for h in range(2):
                z = _dot_nt(qh[h], kh[h])
                if mla:
                    z = z * scale
                if fox:
                    z = z - cr_ref[h, pl.ds(j, 1), :]
                if masked:
                    z = jnp.where(causal, z, NEG)
                p = jnp.exp(z - lse_h[h])
                ds = p * (_dot_nt(doh[h], vb) - dd[h])
                dsb = ds.astype(BF16)
                dqh = carry[h][0] + _dot(dsb, kh[h])
                dkc.append(_dot_tn(dsb, qh[h]))
                dvc = dvc + _dot_tn(p.astype(BF16), doh[h])
                if fox:
                    dc_ref[h, pl.ds(j, 1), :] += -jnp.sum(ds, axis=0, keepdims=True)
                    dqs.append((dqh, carry[h][1] + jnp.sum(ds, axis=1, keepdims=True)))
                else:
                    dqs.append((dqh,))
            if mla:
                dk_ref[pl.ds(start, T), 0:LANES] += dkc[0] * scale
                dk_ref[pl.ds(start, T), LANES:2 * LANES] += dkc[1] * scale
            else:
                dk_ref[pl.ds(start, T), :] += dkc[0] + dkc[1]
            dv_ref[pl.ds(start, T), :] += dvc
            return tuple(dqs)

        one = (jnp.zeros((T, LANES), F32), jnp.zeros((T, 1), F32)) if fox else (jnp.zeros((T, LANES), F32),)
        carry = lax.fori_loop(0, i, lambda j, c: tile(j, c, False), (one, one))
        carry = tile(i, carry, True)
        if mla:
            dq_ref[:, 0:LANES] = carry[0][0] * scale
            dq_ref[:, LANES:2 * LANES] = carry[1][0] * scale
        else:
            dq_ref[...] = jnp.where(m0, carry[0][0], carry[1][0]) * scale
        if fox:
            dcq_ref[0] = carry[0][1]
            dcq_ref[1] = carry[1][1]

    qs, ks, vs, _ = _att_specs(kind, S, T)
    in_specs = [qs, ks, vs,
                pl.BlockSpec((T, LANES), lambda p, i: (i, do_off + p)),
                pl.BlockSpec((T, LANES), lambda p, i: (i, p)),
                pl.BlockSpec((2, T, 1), lambda p, i: (p, i, 0))]
    args = list(qkv) + [do, o, lse]
    W = npairs * LANES
    out_specs = [pl.BlockSpec((T, kw), lambda p, i: (i, p)), pl.BlockSpec((S, kw), lambda p, i: (0, p)),
                 pl.BlockSpec((S, LANES), lambda p, i: (0, p))]
    out_shape = [jax.ShapeDtypeStruct((S, npairs * kw), F32), jax.ShapeDtypeStruct((S, npairs * kw), F32),
                 jax.ShapeDtypeStruct((S, W), F32)]
    if fox:
        in_specs += [pl.BlockSpec((2, nq, T), lambda p, i: (p, 0, 0))]
        args += [c_row]
        out_specs +=[pl.BlockSpec((2, nq, T), lambda p, i: (p, 0, 0)), pl.BlockSpec((2, T, 1), lambda p, i: (p, i, 0))]
        out_shape += [jax.ShapeDtypeStruct((2 * npairs, nq, T), F32), jax.ShapeDtypeStruct((2 * npairs, S, 1), F32)]
    return pl.pallas_call(
        body, name=name, grid=(npairs, nq), in_specs=in_specs, out_specs=out_specs, out_shape=out_shape,
        compiler_params=_cparams(("parallel", "arbitrary")),
    )(*args)


def _softplus_parts(z):
    sp = jnp.maximum(z, 0.0) + jnp.log(1.0 + jnp.exp(-jnp.abs(z)))
    return -sp, z - sp


def _split2(x):
    hi = x.astype(BF16)
    return hi, (x - hi.astype(F32)).astype(BF16)


def _sb_fwd(proj, S, npairs, name):
    T = ATT_T
    nq = S // T
    scale = 64 ** -0.5

    def body(q_ref, k_ref, v_ref, g_ref, o_ref, og_ref, st_ref):
        i = pl.program_id(1)
        m0 = lax.broadcasted_iota(jnp.int32, (1, LANES), 1) < 64
        r = lax.broadcasted_iota(jnp.int32, (T, T), 0)
        c = lax.broadcasted_iota(jnp.int32, (T, T), 1)
        before = c < r
        after = (r > c).astype(BF16)
        qh = _head_q("sb", q_ref, m0, scale)

        def tile(j, carry, masked):
            start = pl.multiple_of(j * T, T)
            vb = v_ref[pl.ds(start, T), :].astype(BF16)
            kb = k_ref[pl.ds(start, T), :].astype(BF16)
            out = []
            for h in range(2):
                rem, acc = carry[h]
                z = _dot_nt(qh[h], kb)
                lk, la = _softplus_parts(z)
                if masked:
                    lk = jnp.where(before, lk, 0.0)
                hi, lo = _split2(lk)
                lr = rem + (_dot(hi, after) + _dot(lo, after))
                w = jnp.exp(la + lr)
                if masked:
                    w = jnp.where(before, w, 0.0)
                out.append((rem + jnp.sum(lk, axis=1, keepdims=True), acc + _dot(w.astype(BF16), vb)))
            return tuple(out)

        init = tuple((jnp.zeros((T, 1), F32), jnp.zeros((T, LANES), F32)) for _ in range(2))
        carry = tile(i, init, True)
        carry = lax.fori_loop(0, i, lambda jj, cr: tile(i - 1 - jj, cr, False), carry)
        o = jnp.where(m0, carry[0][1], carry[1][1])
        o_ref[...] = o
        gt = g_ref[...]
        og_ref[...] = (o * (gt * _sigmoid(gt))).astype(BF16)
        for h in range(2):
            st_ref[h] = carry[h][0]

    W = npairs * LANES
    return pl.pallas_call(
        body, name=name, grid=(npairs, nq), in_specs=list(_att_specs("sb", S, T)),
        out_specs=[pl.BlockSpec((T, LANES), lambda p, i: (i, p)), pl.BlockSpec((T, LANES), lambda p, i: (i, p)),
                   pl.BlockSpec((2, T, 1), lambda p, i: (p, i, 0))],
        out_shape=[jax.ShapeDtypeStruct((S, W), F32), jax.ShapeDtypeStruct((S, W), BF16),
                   jax.ShapeDtypeStruct((2 * npairs, S, 1), F32)],
        compiler_params=_cparams(("parallel", "parallel")),
    )(proj, proj, proj, proj)


def _sb_bwd(proj, do, tot, S, npairs, name):
    T = ATT_T
    nq = S // T
    scale = 64 ** -0.5

    def body(q_ref, k_ref, v_ref, do_ref, st_ref, dq_ref, dk_ref, dv_ref):
        i = pl.program_id(1)

        @pl.when(i == 0)
        def _():
            dk_ref[...] = jnp.zeros_like(dk_ref)
            dv_ref[...] = jnp.zeros_like(dv_ref)

        m0 = lax.broadcasted_iota(jnp.int32, (1, LANES), 1) < 64
        r = lax.broadcasted_iota(jnp.int32, (T, T), 0)
        c = lax.broadcasted_iota(jnp.int32, (T, T), 1)
        before = c < r
        upto = (r <= c).astype(BF16)
        left = (r < c).astype(BF16)
        qh = _head_q("sb", q_ref, m0, scale)
        dov = do_ref[...]
        doh = [jnp.where(m0, dov, 0.0).astype(BF16), jnp.where(m0, 0.0, dov).astype(BF16)]
        tot_h = [st_ref[0], st_ref[1]]

        def tile(j, carry, masked):
            start = pl.multiple_of(j * T, T)
            vb = v_ref[pl.ds(start, T), :].astype(BF16)
            kb = k_ref[pl.ds(start, T), :].astype(BF16)
            out = []
            dkc = jnp.zeros((T, LANES), F32)
            dvc = jnp.zeros((T, LANES), F32)
            for h in range(2):
                pre, gpre, dq = carry[h]
                z = _dot_nt(qh[h], kb)
                lk, la = _softplus_parts(z)
                if masked:
                    lk = jnp.where(before, lk, 0.0)
                hi, lo = _split2(lk)
                lr = (tot_h[h] - pre) - (_dot(hi, upto) + _dot(lo, upto))
                w = jnp.exp(la + lr)
                if masked:
                    w = jnp.where(before, w, 0.0)
                g = _dot_nt(doh[h], vb) * w
                gfull = gpre + _dot(g.astype(BF16), left)
                dz = g - (g + gfull) * jnp.exp(la)
                if masked:
                    dz = jnp.where(before, dz, 0.0)
                dzb = dz.astype(BF16)
                dkc = dkc + _dot_tn(dzb, qh[h])
                dvc = dvc + _dot_tn(w.astype(BF16), doh[h])
                out.append((pre + jnp.sum(lk, axis=1, keepdims=True), gpre + jnp.sum(g, axis=1, keepdims=True),
                            dq + _dot(dzb, kb)))
            dk_ref[pl.ds(start, T), :] += dkc
            dv_ref[pl.ds(start, T), :] += dvc
            return tuple(out)

        init = tuple((jnp.zeros((T, 1), F32), jnp.zeros((T, 1), F32), jnp.zeros((T, LANES), F32)) for _ in range(2))
        carry = lax.fori_loop(0, i, lambda j, cr: tile(j, cr, False), init)
        carry = tile(i, carry, True)
        dq_ref[...] = jnp.where(m0, carry[0][2], carry[1][2]) * scale

    qs, ks, vs, _ = _att_specs("sb", S, T)
    W = npairs * LANES
    return pl.pallas_call(
        body, name=name, grid=(npairs, nq),
        in_specs=[qs, ks, vs, pl.BlockSpec((T, LANES), lambda p, i: (i, p)),
                  pl.BlockSpec((2, T, 1), lambda p, i: (p, i, 0))],
        out_specs=[pl.BlockSpec((T, LANES), lambda p, i: (i, p)), pl.BlockSpec((S, LANES), lambda p, i: (0, p)),
                   pl.BlockSpec((S, LANES), lambda p, i: (0, p))],
        out_shape=[jax.ShapeDtypeStruct((S, W), F32)] * 3,
        compiler_params=_cparams(("parallel", "arbitrary")),
    )(proj, proj, proj, do, tot)


def _split_blocks(per_head):
    return [x[:, b * LANES:(b + 1) * LANES] for x in per_head for b in range(x.shape[1] // LANES)]


def _join_blocks(per_block, nb):
    return [jnp.concatenate(per_block[h * nb:(h + 1) * nb], axis=1) for h in range(len(per_block) // nb)]


def _row_of(col):
    return jnp.broadcast_to(col, (col.shape[0], LANES)).T[0:1]


def _softmax_bwd_t(kind, q, k, kt, v, do, do_off, o, lse, c_col, S, npairs, name):
    T = ATT_T
    nq = S // T
    nb = T // LANES
    fox = kind == "fox"
    mla = kind == "mla"
    scale = (96 if mla else 64) ** -0.5
    kw = 256 if mla else LANES

    def body(*refs):
        if fox:
            (q_ref, k_ref, kt_ref, v_ref, do_ref, o_ref, st_ref, cc_ref,
             dq_ref, dk_ref, dv_ref, dck_ref, dcq_ref, dqt_ref, rs_ref, dkx_ref) = refs
        else:
            q_ref, k_ref, kt_ref, v_ref, do_ref, o_ref, st_ref, dq_ref, dk_ref, dv_ref, dqt_ref = refs
        i = pl.program_id(1)

        @pl.when(i == 0)
        def _():
            dv_ref[...] = jnp.zeros_like(dv_ref)
            if fox:
                dkx_ref[...] = jnp.zeros_like(dkx_ref)
            else:
                dk_ref[...] = jnp.zeros_like(dk_ref)

        m0 = lax.broadcasted_iota(jnp.int32, (1, LANES), 1) < 64
        top = lax.broadcasted_iota(jnp.int32, (LANES, 1), 0) < 64
        key = lax.broadcasted_iota(jnp.int32, (T, LANES), 0)
        qrow = lax.broadcasted_iota(jnp.int32, (T, LANES), 1)
        qh = _head_q(kind, q_ref, m0, scale)
        if fox:
            qv = q_ref[...] * scale
            qk = [jnp.where(m0, qv, 1.0).astype(BF16), jnp.where(m0, 1.0, qv).astype(BF16)]
        else:
            qk = qh
        dov = do_ref[...]
        prod = dov * o_ref[...]
        dd = [_row_of(jnp.sum(jnp.where(m0, prod, 0.0), axis=1, keepdims=True)),
              _row_of(jnp.sum(jnp.where(m0, 0.0, prod), axis=1, keepdims=True))]
        doh = [jnp.where(m0, dov, 0.0).astype(BF16), jnp.where(m0, 0.0, dov).astype(BF16)]
        lse = [st_ref[0], st_ref[1]]
        dqt_ref[...] = jnp.zeros_like(dqt_ref)
        if fox:
            rs_ref[...] = jnp.zeros_like(rs_ref)
        chains = [(h, b) for h in range(2) for b in range(nb)]

        def tiles(js, masked):
            starts = [pl.multiple_of(j * T, T) for j in js]
            zss, dpss = [], []
            for start in starts:
                vb = v_ref[pl.ds(start, T), :].astype(BF16)
                kh = _head_k(kind, k_ref, start, T)
                zss.append(_split_blocks([_dot_nt(kh[h], qh[h]) for h in range(2)]))
                dpss.append(_split_blocks([_dot_nt(vb, doh[h]) for h in range(2)]))
            pss, dsss = [], []
            for start, zs, dps in zip(starts, zss, dpss):
                ps, dss = [], []
                for (h, b), z, dp in zip(chains, zs, dps):
                    lanes = slice(b * LANES, (b + 1) * LANES)
                    if mla:
                        z = z * scale
                    if fox:
                        z = z - cc_ref[h, pl.ds(start, T), :]
                    if masked:
                        z = jnp.where(key <= qrow + b * LANES, z, NEG)
                    p = jnp.exp(z - lse[h][:, lanes])
                    ds = p * (dp - dd[h][:, lanes])
                    dsb = ds.astype(BF16)
                    if fox:
                        rs_ref[h, :, lanes] += jnp.sum(dsb.astype(F32), axis=0, keepdims=True)
                    ps.append(p.astype(BF16))
                    dss.append(dsb)
                pss.append(_join_blocks(ps, nb))
                dsss.append(_join_blocks(dss, nb))
            for j, start, ps, dss in zip(js, starts, pss, dsss):
                kt = kt_ref[j]
                dvc = None
                for h in range(2):
                    dkh = _dot(dss[h], qk[h])
                    dvh = _dot(ps[h], doh[h])
                    dvc = dvh if dvc is None else dvc + dvh
                    kth = kt[h * LANES:(h + 1) * LANES] if mla else kt
                    dqt_ref[h] += _dot(kth, dss[h])
                    if fox:
                        dkx_ref[h, pl.ds(start, T), :] += dkh
                    elif mla:
                        dk_ref[pl.ds(start, T), h * LANES:(h + 1) * LANES] += dkh * scale
                    else:
                        dk_ref[pl.ds(start, T), :] += dkh
                dv_ref[pl.ds(start, T), :] += dvc

        _loop_tiles(i, tiles)
        tiles([i], True)
        if mla:
            dq_ref[:, 0:LANES] = dqt_ref[0].T * scale
            dq_ref[:, LANES:2 * LANES] = dqt_ref[1].T * scale
        else:
            dq_ref[...] = jnp.where(top, dqt_ref[0], dqt_ref[1]).T * scale
        if fox:
            dcq_ref[0] = rs_ref[0]
            dcq_ref[1] = rs_ref[1]

            @pl.when(i == nq - 1)
            def _():
                dk_ref[...] = jnp.where(m0, dkx_ref[0], dkx_ref[1])
                dck_ref[0] = dkx_ref[0][:, 64:65]
                dck_ref[1] = dkx_ref[1][:, 0:1]

    qs, ks, vs, _ = _att_specs(kind, S, T)
    stat = pl.BlockSpec((2, None, 1, T), lambda p, i: (p, i, 0, 0))
    in_specs = [qs, ks, pl.BlockSpec((None, nq, kw, T), lambda p, i: (p, 0, 0, 0)), vs,
                pl.BlockSpec((T, LANES), lambda p, i: (i, do_off + p)),
                pl.BlockSpec((T, LANES), lambda p, i: (i, p)), stat]
    args = [q, k, kt, v, do, o, lse]
    W = npairs * LANES
    out_specs = [pl.BlockSpec((T, kw), lambda p, i: (i, p)), pl.BlockSpec((S, kw), lambda p, i: (0, p)),
                 pl.BlockSpec((S, LANES), lambda p, i: (0, p))]
    out_shape = [jax.ShapeDtypeStruct((S, npairs * kw), F32), jax.ShapeDtypeStruct((S, npairs * kw), F32),
                 jax.ShapeDtypeStruct((S, W), F32)]
    scratch = [pltpu.VMEM((2, LANES, T), F32)]
    if fox:
        in_specs.append(pl.BlockSpec((2, S, LANES), lambda p, i: (p, 0, 0)))
        args.append(c_col)
        out_specs += [pl.BlockSpec((2, S, 1), lambda p, i: (p, 0, 0)), stat]
        out_shape += [jax.ShapeDtypeStruct((2 * npairs, S, 1), F32), jax.ShapeDtypeStruct((2 * npairs, nq, 1, T), F32)]
        scratch += [pltpu.VMEM((2, 1, T), F32), pltpu.VMEM((2, S, LANES), F32)]
    return pl.pallas_call(
        body, name=name, grid=(npairs, nq), in_specs=in_specs, out_specs=out_specs, out_shape=out_shape,
        scratch_shapes=scratch, compiler_params=_cparams(("parallel", "arbitrary")),
    )(*args)


def _sb_fwd_t(proj, vt, S, npairs, name):
    T = ATT_T
    nq = S // T
    nb = T // LANES
    scale = 64 ** -0.5

    def body(q_ref, k_ref, vt_ref, g_ref, o_ref, og_ref, ogt_ref, st_ref, rem_ref, acc_ref):
        i = pl.program_id(1)
        m0 = lax.broadcasted_iota(jnp.int32, (1, LANES), 1) < 64
        top = lax.broadcasted_iota(jnp.int32, (LANES, 1), 0) < 64
        key = lax.broadcasted_iota(jnp.int32, (T, LANES), 0)
        qrow = lax.broadcasted_iota(jnp.int32, (T, LANES), 1)
        r = lax.broadcasted_iota(jnp.int32, (T, T), 0)
        c = lax.broadcasted_iota(jnp.int32, (T, T), 1)
        after = (c > r).astype(BF16)
        qh = _head_q("sb", q_ref, m0, scale)
        rem_ref[...] = jnp.zeros_like(rem_ref)
        acc_ref[...] = jnp.zeros_like(acc_ref)
        chains = [(h, b) for h in range(2) for b in range(nb)]

        def tiles(js, masked):
            zss = []
            for j in js:
                kb = k_ref[pl.ds(pl.multiple_of(j * T, T), T), :].astype(BF16)
                zss.append(_split_blocks([_dot_nt(kb, qh[h]) for h in range(2)]))
            lass, sums, hiss, loss = [], [], [], []
            for zs in zss:
                las, sm, his, los = [], [], [], []
                for (h, b), z in zip(chains, zs):
                    lk, la = _softplus_parts(z)
                    if masked:
                        lk = jnp.where(key < qrow + b * LANES, lk, 0.0)
                    hi, lo = _split2(lk)
                    las.append(la)
                    sm.append(jnp.sum(lk, axis=0, keepdims=True))
                    his.append(hi)
                    los.append(lo)
                lass.append(las)
                sums.append(sm)
                hiss.append(_join_blocks(his, nb))
                loss.append(_join_blocks(los, nb))
            rcss = [_split_blocks([_dot(after, hi) + _dot(after, lo) for hi, lo in zip(his, los)])
                    for his, los in zip(hiss, loss)]
            wss = []
            for las, sm, rcs in zip(lass, sums, rcss):
                ws = []
                for (h, b), la, s, rc in zip(chains, las, sm, rcs):
                    lanes = slice(b * LANES, (b + 1) * LANES)
                    w = jnp.exp(la + (rem_ref[h, :, lanes] + rc))
                    if masked:
                        w = jnp.where(key < qrow + b * LANES, w, 0.0)
                    ws.append(w.astype(BF16))
                    rem_ref[h, :, lanes] += s
                wss.append(_join_blocks(ws, nb))
            for j, ws in zip(js, wss):
                vtb = vt_ref[j]
                for h in range(2):
                    acc_ref[h] += _dot(vtb, ws[h])

        tiles([i], True)
        _loop_tiles(i, tiles, lambda t: i - 1 - t)
        o = jnp.where(top, acc_ref[0], acc_ref[1]).T
        o_ref[...] = o
        gt = g_ref[...]
        og = o * (gt * _sigmoid(gt))
        og_ref[...] = og.astype(BF16)
        ogt_ref[...] = og.T.astype(BF16)
        st_ref[0] = rem_ref[0]
        st_ref[1] = rem_ref[1]

    qs, ks, _, gs = _att_specs("sb", S, T)
    W = npairs * LANES
    return pl.pallas_call(
        body, name=name, grid=(npairs, nq),
        in_specs=[qs, ks, pl.BlockSpec((None, nq, LANES, T), lambda p, i: (p, 0, 0, 0)), gs],
        out_specs=[pl.BlockSpec((T, LANES), lambda p, i: (i, p)), pl.BlockSpec((T, LANES), lambda p, i: (i, p)),
                   pl.BlockSpec((LANES, T), lambda p, i: (p, i)),
                   pl.BlockSpec((2, None, 1, T), lambda p, i: (p, i, 0, 0))],
        out_shape=[jax.ShapeDtypeStruct((S, W), F32), jax.ShapeDtypeStruct((S, W), BF16),
                   jax.ShapeDtypeStruct((W, S), BF16),
                   jax.ShapeDtypeStruct((2 * npairs, nq, 1, T), F32)],
        scratch_shapes=[pltpu.VMEM((2, 1, T), F32), pltpu.VMEM((2, LANES, T), F32)],
        compiler_params=_cparams(("parallel", "parallel")),
    )(proj, proj, vt, proj)


def _sb_bwd_t(proj, kt, do, tot, S, npairs, name):
    T = ATT_T
    nq = S // T
    nb = T // LANES
    scale = 64 ** -0.5

    def body(q_ref, k_ref, kt_ref, v_ref, do_ref, st_ref, dq_ref, dk_ref, dv_ref, dqt_ref, pre_ref, gpre_ref):
        i = pl.program_id(1)

        @pl.when(i == 0)
        def _():
            dk_ref[...] = jnp.zeros_like(dk_ref)
            dv_ref[...] = jnp.zeros_like(dv_ref)

        m0 = lax.broadcasted_iota(jnp.int32, (1, LANES), 1) < 64
        top = lax.broadcasted_iota(jnp.int32, (LANES, 1), 0) < 64
        key = lax.broadcasted_iota(jnp.int32, (T, LANES), 0)
        qrow = lax.broadcasted_iota(jnp.int32, (T, LANES), 1)
        r = lax.broadcasted_iota(jnp.int32, (T, T), 0)
        c = lax.broadcasted_iota(jnp.int32, (T, T), 1)
        upto = (c <= r).astype(BF16)
        left = (c < r).astype(BF16)
        qh = _head_q("sb", q_ref, m0, scale)
        dov = do_ref[...]
        doh = [jnp.where(m0, dov, 0.0).astype(BF16), jnp.where(m0, 0.0, dov).astype(BF16)]
        tot_h = [st_ref[0], st_ref[1]]
        dqt_ref[...] = jnp.zeros_like(dqt_ref)
        pre_ref[...] = jnp.zeros_like(pre_ref)
        gpre_ref[...] = jnp.zeros_like(gpre_ref)
        chains = [(h, b) for h in range(2) for b in range(nb)]

        def tiles(js, masked):
            starts = [pl.multiple_of(j * T, T) for j in js]
            zss, dwss = [], []
            for start in starts:
                vb = v_ref[pl.ds(start, T), :].astype(BF16)
                kb = k_ref[pl.ds(start, T), :].astype(BF16)
                zss.append(_split_blocks([_dot_nt(kb, qh[h]) for h in range(2)]))
                dwss.append(_split_blocks([_dot_nt(vb, doh[h]) for h in range(2)]))
            lass, sums, hiss, loss = [], [], [], []
            for zs in zss:
                las, sm, his, los = [], [], [], []
                for (h, b), z in zip(chains, zs):
                    lk, la = _softplus_parts(z)
                    if masked:
                        lk = jnp.where(key < qrow + b * LANES, lk, 0.0)
                    hi, lo = _split2(lk)
                    las.append(la)
                    sm.append(jnp.sum(lk, axis=0, keepdims=True))
                    his.append(hi)
                    los.append(lo)
                lass.append(las)
                sums.append(sm)
                hiss.append(_join_blocks(his, nb))
                loss.append(_join_blocks(los, nb))
            pcss = [_split_blocks([_dot(upto, hi) + _dot(upto, lo) for hi, lo in zip(his, los)])
                    for his, los in zip(hiss, loss)]
            wss, gss = [], []
            for las, sm, pcs, dws in zip(lass, sums, pcss, dwss):
                ws, gs = [], []
                for (h, b), la, s, pc, dw in zip(chains, las, sm, pcs, dws):
                    lanes = slice(b * LANES, (b + 1) * LANES)
                    w = jnp.exp(la + ((tot_h[h][:, lanes] - pre_ref[h, :, lanes]) - pc))
                    if masked:
                        w = jnp.where(key < qrow + b * LANES, w, 0.0)
                    ws.append(w.astype(BF16))
                    gs.append(dw * w)
                    pre_ref[h, :, lanes] += s
                wss.append(_join_blocks(ws, nb))
                gss.append(gs)
            gcss = [_split_blocks([_dot(left, g) for g in _join_blocks([g.astype(BF16) for g in gs], nb)]) for gs in gss]
            dzss = []
            for las, gs, gcs in zip(lass, gss, gcss):
                dzs = []
                for (h, b), la, g, gc in zip(chains, las, gs, gcs):
                    lanes = slice(b * LANES, (b + 1) * LANES)
                    dz = g - (g + (gpre_ref[h, :, lanes] + gc)) * jnp.exp(la)
                    if masked:
                        dz = jnp.where(key < qrow + b * LANES, dz, 0.0)
                    dzs.append(dz.astype(BF16))
                    gpre_ref[h, :, lanes] += jnp.sum(g, axis=0, keepdims=True)
                dzss.append(_join_blocks(dzs, nb))
            for j, start, ws, dzs in zip(js, starts, wss, dzss):
                kt = kt_ref[j]
                dkc = dvc = None
                for h in range(2):
                    dkh = _dot(dzs[h], qh[h])
                    dvh = _dot(ws[h], doh[h])
                    dkc = dkh if dkc is None else dkc + dkh
                    dvc = dvh if dvc is None else dvc + dvh
                    dqt_ref[h] += _dot(kt, dzs[h])
                dk_ref[pl.ds(start, T), :] += dkc
                dv_ref[pl.ds(start, T), :] += dvc

        _loop_tiles(i, tiles)
        tiles([i], True)
        dq_ref[...] = jnp.where(top, dqt_ref[0], dqt_ref[1]).T * scale

    qs, ks, vs, _ = _att_specs("sb", S, T)
    W = npairs * LANES
    return pl.pallas_call(
        body, name=name, grid=(npairs, nq),
        in_specs=[qs, ks, pl.BlockSpec((None, nq, LANES, T), lambda p, i: (p, 0, 0, 0)), vs,
                  pl.BlockSpec((T, LANES), lambda p, i: (i, p)),
                  pl.BlockSpec((2, None, 1, T), lambda p, i: (p, i, 0, 0))],
        out_specs=[pl.BlockSpec((T, LANES), lambda p, i: (i, p)), pl.BlockSpec((S, LANES), lambda p, i: (0, p)),
                   pl.BlockSpec((S, LANES), lambda p, i: (0, p))],
        out_shape=[jax.ShapeDtypeStruct((S, W), F32)] * 3,
        scratch_shapes=[pltpu.VMEM((2, LANES, T), F32), pltpu.VMEM((2, 1, T), F32), pltpu.VMEM((2, 1, T), F32)],
        compiler_params=_cparams(("parallel", "arbitrary")),
    )(proj, proj, kt, proj, do, tot)


def _pad_w0(w):
    z = lambda n: jnp.zeros((w.shape[0], n), w.dtype)
    return jnp.concatenate([w[:, 2048:2432], w[:, 2432:2688], z(64), w[:, 2688:2720], z(32),
                            w[:, 1536:2048], w[:, 2720:3232], w[:, 0:512], w[:, 512:1024], w[:, 1024:1536]], axis=1)


def _unpad_w0(wp):
    return jnp.concatenate([wp[:, L0_SBQ:L0_SBQ + 512], wp[:, L0_SBK:L0_SBK + 512], wp[:, L0_SBV:L0_SBV + 512],
                            wp[:, L0_SBG:L0_SBG + 512], wp[:, 0:384], wp[:, 384:640], wp[:, 704:736],
                            wp[:, L0_MLG:L0_MLG + 512]], axis=1)


def _pad_wq(w):
    return jnp.pad(w.reshape(384, 8, 96), ((0, 0), (0, 0), (0, 32))).reshape(384, 1024)


def _unpad_wq(wp):
    return wp.reshape(384, 8, 128)[:, :, :96].reshape(384, 768)


def _pad_wkv(w):
    w3 = w.reshape(256, 8, 128)
    k = jnp.pad(w3[:, :, :64], ((0, 0), (0, 0), (0, 64))).reshape(256, 1024)
    return jnp.concatenate([k, w3[:, :, 64:].reshape(256, 512)], axis=1)


def _unpad_wkv(wp):
    k = wp[:, :1024].reshape(256, 8, 128)[:, :, :64]
    v = wp[:, 1024:].reshape(256, 8, 64)
    return jnp.concatenate([k, v], axis=-1).reshape(256, 1024)


def _pad_w1(w):
    return jnp.concatenate([w, jnp.zeros((w.shape[0], L1_WIDTH - ODD_IN_WIDTH), w.dtype)], axis=1)


def _local_step(x, positions, target, g, w0p, wqp, wkvp, wo0, w1p, wo1):
    S = x.shape[0]
    nq = S // ATT_T
    invf = ROPE_THETA ** (-jnp.arange(0, MLA_ROPE_DIM, 2, dtype=F32) / MLA_ROPE_DIM)
    invf = jnp.concatenate([jnp.zeros((64,), F32), invf, invf, jnp.zeros((32,), F32)]).reshape(1, LANES)
    cosT, s1T, s2T = _rope_tables(positions.reshape(S, 1), invf, "rope_tables")
    bfp = jnp.pad(g["l1_b_f"], ((0, 0), (0, LANES - FOX_HEADS)))

    proj0, h0t = _norm_matmul(x, g["l0_pre_g"], w0p, "l0_in_proj")
    qm, km, vm, qnt, cnt = _mla_prep(proj0, g["l0_q_a_g"], g["l0_kv_a_g"], wqp, wkvp, cosT, s1T, s2T, "mla_prep")
    sb_vt = _transpose_tiles(proj0, L0_SBV, 4, LANES, 2, "sb_vt")
    sb_kt = _transpose_tiles(proj0, L0_SBK, 4, LANES, 2, "sb_kt")
    o_sb, og_sb, ogt_sb, tot_sb = _sb_fwd_t(proj0, sb_vt, S, 4, "sb_fwd")
    vmt = _transpose_tiles(vm, 0, 4, LANES, 4, "mla_vt")
    kmt = _transpose_tiles(km, 0, 4, 2 * LANES, 4, "mla_kt")
    o_ml, og_ml, ogt_ml, lse_ml = _softmax_fwd("mla", (qm, km, vmt, proj0), None, S, 4, "mla_fwd")
    y0, x1 = _out_proj(og_sb, og_ml, 0, 0, wo0, x, g["l0_post_g"], None, "l0_out_proj")

    proj1, h1t = _norm_matmul(x1, g["l1_pre_g"], w1p, "l1_in_proj")
    cfx = _fox_prep(proj1, bfp, "fox_prep")
    c16 = cfx[:, :FOX_HEADS].T
    c_col = jnp.broadcast_to(c16[:, :, None], (FOX_HEADS, S, LANES))
    vt1 = _transpose_tiles(proj1, L1_V, 8, LANES, 8, "fox_vt")
    kt1 = _transpose_tiles(proj1, L1_K, 8, LANES, 8, "fox_kt")
    o_fx, og_fx, ogt_fx, lse_fx = _softmax_fwd("fox", (proj1, proj1, vt1, proj1), c_col, S, 8, "fox_fwd")
    y1, dx2, lsum = _out_proj(og_fx, og_fx, 0, 1, wo1, x1, g["l1_post_g"], target, "l1_out_proj")

    dy1, do1, dgate1, d_post1 = _out_proj_bwd(dx2, y1, g["l1_post_g"], wo1, proj1, (L1_G, L1_G + 512), o_fx, o_fx, 0, 1, "l1_out_bwd")
    dwo1 = _matmul_t(ogt_fx, dy1, "l1_dw_out")
    dq1, dk1, dv1, dck, dcq = _softmax_bwd_t("fox", proj1, proj1, kt1, proj1, do1, 0, o_fx, lse_fx, c_col, S, 8,
                                             "fox_bwd")
    dc = jnp.pad((dcq.reshape(FOX_HEADS, S) - dck.reshape(FOX_HEADS, S)).T, ((0, 0), (0, LANES - FOX_HEADS)))
    df, d_bf = _fox_prep_bwd(dc, proj1, bfp, "fox_prep_bwd")
    pieces1 = [(L1_Q, dq1), (L1_K, dk1), (L1_V, dv1), (L1_G, dgate1), (L1_F, df)]
    dx1, d_pre1 = _in_proj_bwd(pieces1, w1p, x1, g["l1_pre_g"], dx2, "l1_in_bwd")
    dw1p = jnp.concatenate([_matmul_t(h1t, a, "l1_dw_in_%d" % k) for k, (_, a) in enumerate(pieces1)], axis=1)

    dy0, do0, dgate0, d_post0 = _out_proj_bwd(dx1, y0, g["l0_post_g"], wo0, proj0, (L0_SBG, L0_MLG), o_sb, o_ml, 0, 0,
                                              "l0_out_bwd")
    dwo0 = jnp.concatenate([_matmul_t(ogt_sb, dy0, "l0_dw_out_sb"), _matmul_t(ogt_ml, dy0, "l0_dw_out_mla")], axis=0)
    dsq, dsk, dsv = _sb_bwd_t(proj0, sb_kt, do0, tot_sb, S, 4, "sb_bwd")
    dqm, dkm, dvm = _softmax_bwd_t("mla", qm, km, kmt, vm, do0, 4, o_ml, lse_ml, None, S, 4, "mla_bwd")
    dprep, dqb, dkvb, d_qag, d_kvag = _mla_prep_bwd(dqm, dkm, dvm, proj0, g["l0_q_a_g"], g["l0_kv_a_g"], wqp, wkvp,
                                                    cosT, s1T, s2T, "mla_prep_bwd")
    dwqp = _matmul_t(qnt, dqb, "l0_dw_qb")
    dwkvp = _matmul_t(cnt, dkvb, "l0_dw_kvb")
    pieces0 = [(L0_PREP, dprep), (L0_SBG, dgate0), (L0_SBQ, dsq), (L0_SBK, dsk), (L0_SBV, dsv)]
    dx0, d_pre0 = _in_proj_bwd(pieces0, w0p, x, g["l0_pre_g"], dx1, "l0_in_bwd")
    dw0p = jnp.concatenate([_matmul_t(h0t, a, "l0_dw_in_%d" % k) for k, (_, a) in enumerate(pieces0)], axis=1)

    grads = {
        "l0_pre_g": d_pre0, "l0_post_g": d_post0, "l0_w_in": dw0p, "l0_q_a_g": d_qag, "l0_w_q_b": dwqp,
        "l0_kv_a_g": d_kvag, "l0_w_kv_b": dwkvp, "l0_w_out": dwo0, "l1_pre_g": d_pre1, "l1_post_g": d_post1,
        "l1_w_in": dw1p, "l1_b_f": d_bf[:, :FOX_HEADS], "l1_w_out": dwo1,
    }
    return lsum, dx0, grads


_ANY = pl.BlockSpec(memory_space=pl.ANY)


def _place():
    return lax.axis_index("x"), lax.axis_index("y"), lax.axis_index("c")


def _other_chips(x, y):
    return [(1 - x, y), (x, 1 - y), (1 - x, 1 - y)]


def _half(c):
    return pl.ds(c * PACK_HALF, PACK_HALF)


def _weight_gather(pack):
    def body(p_ref, out_ref, send_sems, recv_sems):
        x, y, c = _place()
        sibling = (x, y, 1 - c)
        chips = _other_chips(x, y)

        def blk(chip, cc):
            return out_ref.at[2 * chip[0] + chip[1], _half(cc)]

        def copy(k, src, dst, to):
            return pltpu.make_async_remote_copy(src_ref=src, dst_ref=dst, send_sem=send_sems.at[k],
                                                recv_sem=recv_sems.at[k], device_id=to, device_id_type=MESH)

        first = [copy(j, p_ref.at[_half(c)], blk((x, y), c), (*chip, c)) for j, chip in enumerate(chips)]
        for cp in first:
            cp.start()
        passed = [copy(3 + j, blk(chip, c), blk(chip, c), sibling) for j, chip in enumerate(chips)]
        for j, chip in enumerate(chips):
            copy(j, blk(chip, c), blk(chip, c), (x, y, c)).wait_recv()
            passed[j].start()
        for j, chip in enumerate(chips):
            copy(3 + j, blk(chip, 1 - c), blk(chip, 1 - c), (x, y, c)).wait_recv()
        for cp in first + passed:
            cp.wait_send()

    return pl.pallas_call(
        body, name="weight_gather", in_specs=[_ANY], out_specs=_ANY,
        out_shape=jax.ShapeDtypeStruct((4,) + pack.shape, pack.dtype),
        scratch_shapes=[pltpu.SemaphoreType.DMA((6,)), pltpu.SemaphoreType.DMA((6,))],
    )(pack)


GRAD_TR = 2048


def _grad_core_exchange(p):
    def body(p_ref, recv_ref, send_sems, recv_sems):
        x, y, c = _place()
        give = [pltpu.make_async_remote_copy(src_ref=p_ref.at[j, _half(1 - c)], dst_ref=recv_ref.at[j],
                                             send_sem=send_sems.at[j], recv_sem=recv_sems.at[j],
                                             device_id=(x, y, 1 - c), device_id_type=MESH) for j in range(4)]
        for cp in give:
            cp.start()
        for cp in give:
            cp.wait()

    return pl.pallas_call(
        body, name="grad_core_exchange", in_specs=[_ANY], out_specs=_ANY,
        out_shape=jax.ShapeDtypeStruct((4, PACK_HALF, LANES), p.dtype),
        scratch_shapes=[pltpu.SemaphoreType.DMA((4,)), pltpu.SemaphoreType.DMA((4,))],
    )(p)


def _grad_add_cores(p, theirs, c1):
    tr = GRAD_TR

    def body(c_ref, a_ref, b_ref, o_ref):
        o_ref[...] = a_ref[...] + b_ref[...]

    spec = pl.BlockSpec((None, tr, LANES), lambda j, r, c: (j, r, 0))
    grid_spec = pltpu.PrefetchScalarGridSpec(
        num_scalar_prefetch=1, grid=(4, PACK_HALF // tr),
        in_specs=[pl.BlockSpec((None, None, tr, LANES), lambda j, r, c: (j, c[0], r, 0)), spec], out_specs=spec)
    return pl.pallas_call(
        body, name="grad_add_cores", grid_spec=grid_spec, out_shape=jax.ShapeDtypeStruct(theirs.shape, theirs.dtype),
        compiler_params=_cparams(("parallel", "parallel")),
    )(c1, p.reshape(4, 2, PACK_HALF, LANES), theirs)


def _grad_chip_exchange(q):
    def body(q_ref, out_ref, send_sems, recv_sems):
        x, y, c = _place()
        me = 2 * x + y
        chips = _other_chips(x, y)
        sends = [pltpu.make_async_remote_copy(src_ref=q_ref.at[2 * chip[0] + chip[1]], dst_ref=out_ref.at[me],
                                              send_sem=send_sems.at[j], recv_sem=recv_sems.at[j],
                                              device_id=(*chip, c), device_id_type=MESH) for j, chip in enumerate(chips)]
        for cp in sends:
            cp.start()
        for j, chip in enumerate(chips):
            slot = out_ref.at[2 * chip[0] + chip[1]]
            pltpu.make_async_remote_copy(src_ref=slot, dst_ref=slot, send_sem=send_sems.at[j], recv_sem=recv_sems.at[j],
                                         device_id=(x, y, c), device_id_type=MESH).wait_recv()
        for cp in sends:
            cp.wait_send()

    return pl.pallas_call(
        body, name="grad_chip_exchange", in_specs=[_ANY], out_specs=_ANY,
        out_shape=jax.ShapeDtypeStruct(q.shape, q.dtype),
        scratch_shapes=[pltpu.SemaphoreType.DMA((3,)), pltpu.SemaphoreType.DMA((3,))],
    )(q)


def _grad_add_chips(q, slots, me1):
    tr = GRAD_TR

    def body(me_ref, own_ref, s0, s1, s2, s3, o_ref):
        me = me_ref[0]
        t = [jnp.where(me == j, own_ref[...], s[...]) for j, s in enumerate((s0, s1, s2, s3))]
        o_ref[...] = ((t[0] + t[1]) + t[2]) + t[3]

    def slot_spec(j):
        return pl.BlockSpec((None, tr, LANES), lambda r, me: (jnp.where(me[0] == j, (j + 1) % 4, j), r, 0))

    grid_spec = pltpu.PrefetchScalarGridSpec(
        num_scalar_prefetch=1, grid=(PACK_HALF // tr,),
        in_specs=[pl.BlockSpec((None, tr, LANES), lambda r, me: (me[0], r, 0))] + [slot_spec(j) for j in range(4)],
        out_specs=pl.BlockSpec((tr, LANES), lambda r, me: (r, 0)))
    return pl.pallas_call(
        body, name="grad_add_chips", grid_spec=grid_spec, out_shape=jax.ShapeDtypeStruct(q.shape[1:], q.dtype),
        compiler_params=_cparams(("parallel",)),
    )(me1, q, slots, slots, slots, slots)


def _grad_core_gather(t):
    def body(t_ref, out_ref, send_sem, recv_sem):
        x, y, c = _place()
        give = pltpu.make_async_remote_copy(src_ref=t_ref, dst_ref=out_ref, send_sem=send_sem, recv_sem=recv_sem,
                                            device_id=(x, y, 1 - c), device_id_type=MESH)
        give.start()
        give.wait()

    return pl.pallas_call(
        body, name="grad_core_gather", in_specs=[_ANY], out_specs=_ANY,
        out_shape=jax.ShapeDtypeStruct(t.shape, t.dtype),
        scratch_shapes=[pltpu.SemaphoreType.DMA, pltpu.SemaphoreType.DMA],
    )(t)


def _small_allreduce(sp):
    def body(sp_ref, out_ref, gath_ref, send_sems, recv_sems):
        x, y, c = _place()
        me = 4 * x + 2 * y + c
        gath_ref[me] = sp_ref[...]
        peers = []
        for k in range(1, 8):
            px = 1 - x if k & 4 else x
            py = 1 - y if k & 2 else y
            pc = 1 - c if k & 1 else c
            peers.append((px, py, pc))
        sends = [pltpu.make_async_remote_copy(src_ref=sp_ref, dst_ref=gath_ref.at[me], send_sem=send_sems.at[k],
                                              recv_sem=recv_sems.at[k], device_id=peer, device_id_type=MESH)
                 for k, peer in enumerate(peers)]
        for cp in sends:
            cp.start()
        for k, (px, py, pc) in enumerate(peers):
            slot = gath_ref.at[4 * px + 2 * py + pc]
            pltpu.make_async_remote_copy(src_ref=slot, dst_ref=slot, send_sem=send_sems.at[k], recv_sem=recv_sems.at[k],
                                         device_id=(x, y, c), device_id_type=MESH).wait_recv()
        for cp in sends:
            cp.wait_send()
        tot = gath_ref[0]
        for d in range(1, 8):
            tot = tot + gath_ref[d]
        out_ref[...] = tot

    vm = pl.BlockSpec(memory_space=pltpu.VMEM)
    return pl.pallas_call(
        body, name="small_allreduce", in_specs=[vm], out_specs=vm, out_shape=jax.ShapeDtypeStruct(sp.shape, sp.dtype),
        scratch_shapes=[pltpu.VMEM((8,) + sp.shape, sp.dtype), pltpu.SemaphoreType.DMA((7,)), pltpu.SemaphoreType.DMA((7,))],
    )(sp)


def _adamw_update(w, gv, m, v):
    mn = ADAM_B1 * m + (1.0 - ADAM_B1) * gv
    vn = ADAM_B2 * v + (1.0 - ADAM_B2) * (gv * gv)
    m_hat = mn / (1.0 - ADAM_B1 ** ADAM_STEP)
    v_hat = vn / (1.0 - ADAM_B2 ** ADAM_STEP)
    return -ADAM_LR * (m_hat / (jnp.sqrt(v_hat) + ADAM_EPS) + ADAM_WD * w), mn, vn


def _adamw(w, g, m, v, name):
    rows = w.shape[0]

    def body(w_ref, g_ref, m_ref, v_ref, d_ref, mo_ref, vo_ref):
        d_ref[...], mo_ref[...], vo_ref[...] = _adamw_update(w_ref[...], g_ref[...], m_ref[...], v_ref[...])

    spec = pl.BlockSpec((rows, LANES), lambda r: (0, 0))
    shp = jax.ShapeDtypeStruct(w.shape, F32)
    return pl.pallas_call(
        body, name=name, grid=(1,), in_specs=[spec] * 4, out_specs=[spec] * 3, out_shape=[shp] * 3,
        compiler_params=_cparams(("arbitrary",)),
    )(w, g, m, v)


def _adamw_mats(w, g_mine, g_theirs, m, v, c1):
    tr = GRAD_TR
    nb = PACK_HALF // tr

    def body(c_ref, w_ref, a_ref, b_ref, m_ref, v_ref, g_ref, d_ref, mo_ref, vo_ref):
        gv = jnp.where(pl.program_id(0) == c_ref[0], a_ref[...], b_ref[...])
        g_ref[...] = gv
        d_ref[...], mo_ref[...], vo_ref[...] = _adamw_update(w_ref[...], gv, m_ref[...], v_ref[...])

    full = pl.BlockSpec((tr, LANES), lambda h, r, c: (h * nb + r, 0))
    half = pl.BlockSpec((tr, LANES), lambda h, r, c: (r, 0))
    grid_spec = pltpu.PrefetchScalarGridSpec(num_scalar_prefetch=1, grid=(2, nb),
                                             in_specs=[full, half, half, full, full], out_specs=[full] * 4)
    shp = jax.ShapeDtypeStruct(w.shape, F32)
    return pl.pallas_call(
        body, name="adamw_mats", grid_spec=grid_spec, out_shape=[shp] * 4,
        compiler_params=_cparams(("parallel", "parallel")),
    )(c1, w, g_mine, g_theirs, m, v)


MAT_NAMES = ("l0_w_in", "l0_w_q_b", "l0_w_kv_b", "l0_w_out", "l1_w_in", "l1_w_out")
VEC_NAMES = ("l0_pre_g", "l0_post_g", "l0_q_a_g", "l0_kv_a_g", "l1_pre_g", "l1_post_g", "l1_b_f")
WEIGHT_NAMES = ("l0_pre_g", "l0_post_g", "l0_w_in", "l0_q_a_g", "l0_w_q_b", "l0_kv_a_g", "l0_w_kv_b", "l0_w_out",
                "l1_pre_g", "l1_post_g", "l1_w_in", "l1_b_f", "l1_w_out")
MAT_SHARD = {"l0_w_in": (1024, 808), "l0_w_q_b": (384, 192), "l0_w_kv_b": (256, 256), "l0_w_out": (256, 1024),
             "l1_w_in": (1024, 1028), "l1_w_out": (256, 1024)}
ROW_SHARDED = ("l0_w_out", "l1_w_out")
VEC_LEN = {"l0_pre_g": 1024, "l0_post_g": 1024, "l0_q_a_g": 384, "l0_kv_a_g": 256, "l1_pre_g": 1024,
           "l1_post_g": 1024, "l1_b_f": 16}


def _mat_rows(n):
    r, c = MAT_SHARD[n]
    return r * c // LANES


def _pack_shards(shards):
    parts = [shards[n].reshape(_mat_rows(n), LANES) for n in MAT_NAMES]
    used = sum(_mat_rows(n) for n in MAT_NAMES)
    parts.append(jnp.zeros((PACK_ROWS - used, LANES), parts[0].dtype))
    return jnp.concatenate(parts, axis=0)


def _unpack_shards(pack):
    out, at = {}, 0
    for n in MAT_NAMES:
        out[n] = pack[..., at:at + _mat_rows(n), :].reshape(pack.shape[:-2] + MAT_SHARD[n])
        at += _mat_rows(n)
    return out


def _join_shards(n, s):
    if n in ROW_SHARDED:
        return s.reshape(4 * s.shape[1], s.shape[2])
    return s.transpose(1, 0, 2).reshape(s.shape[1], 4 * s.shape[2])


def _cut_shards(n, w):
    r, c = MAT_SHARD[n]
    if n in ROW_SHARDED:
        return w.reshape(4, r, c)
    return w.reshape(r, 4, c).transpose(1, 0, 2)


def _pack_vecs(vecs):
    parts = []
    for n in VEC_NAMES:
        v = vecs[n].reshape(-1)
        parts.append(jnp.pad(v, (0, VEC_ROWS * LANES - v.shape[0])).reshape(VEC_ROWS, LANES))
    return jnp.concatenate(parts, axis=0)


def _unpack_vecs(pack):
    return {n: pack[k * VEC_ROWS:(k + 1) * VEC_ROWS].reshape(-1)[:VEC_LEN[n]] for k, n in enumerate(VEC_NAMES)}


def kernel(x, positions, l0_pre_g, l0_post_g, l0_w_in, l0_q_a_g, l0_w_q_b, l0_kv_a_g, l0_w_kv_b, l0_w_out, l1_pre_g, l1_post_g, l1_w_in, l1_b_f, l1_w_out, loss_target, m_l0_pre_g, m_l0_post_g, m_l0_w_in, m_l0_q_a_g, m_l0_w_q_b, m_l0_kv_a_g, m_l0_w_kv_b, m_l0_w_out, m_l1_pre_g, m_l1_post_g, m_l1_w_in, m_l1_b_f, m_l1_w_out, v_l0_pre_g, v_l0_post_g, v_l0_w_in, v_l0_q_a_g, v_l0_w_q_b, v_l0_kv_a_g, v_l0_w_kv_b, v_l0_w_out, v_l1_pre_g, v_l1_post_g, v_l1_w_in, v_l1_b_f, v_l1_w_out):
    w = dict(l0_pre_g=l0_pre_g, l0_post_g=l0_post_g, l0_w_in=l0_w_in, l0_q_a_g=l0_q_a_g, l0_w_q_b=l0_w_q_b,
             l0_kv_a_g=l0_kv_a_g, l0_w_kv_b=l0_w_kv_b, l0_w_out=l0_w_out, l1_pre_g=l1_pre_g, l1_post_g=l1_post_g,
             l1_w_in=l1_w_in, l1_b_f=l1_b_f, l1_w_out=l1_w_out)
    m = dict(l0_pre_g=m_l0_pre_g, l0_post_g=m_l0_post_g, l0_w_in=m_l0_w_in, l0_q_a_g=m_l0_q_a_g, l0_w_q_b=m_l0_w_q_b,
             l0_kv_a_g=m_l0_kv_a_g, l0_w_kv_b=m_l0_w_kv_b, l0_w_out=m_l0_w_out, l1_pre_g=m_l1_pre_g,
             l1_post_g=m_l1_post_g, l1_w_in=m_l1_w_in, l1_b_f=m_l1_b_f, l1_w_out=m_l1_w_out)
    v = dict(l0_pre_g=v_l0_pre_g, l0_post_g=v_l0_post_g, l0_w_in=v_l0_w_in, l0_q_a_g=v_l0_q_a_g, l0_w_q_b=v_l0_w_q_b,
             l0_kv_a_g=v_l0_kv_a_g, l0_w_kv_b=v_l0_w_kv_b, l0_w_out=v_l0_w_out, l1_pre_g=v_l1_pre_g,
             l1_post_g=v_l1_post_g, l1_w_in=v_l1_w_in, l1_b_f=v_l1_b_f, l1_w_out=v_l1_w_out)

    cx, cy, cc = _place()
    me1 = jnp.reshape(2 * cx + cy, (1,)).astype(jnp.int32)
    c1 = jnp.reshape(cc, (1,)).astype(jnp.int32)
    w_pack = _pack_shards(w)
    w_bf = w_pack.astype(BF16)
    gathered = lax.dynamic_update_slice(_weight_gather(w_bf), w_bf[None], (2 * cx + cy, 0, 0))
    gathered = _unpack_shards(gathered)
    full = {n: _join_shards(n, gathered[n]) for n in MAT_NAMES}
    gains = {n: w[n].reshape(1, -1) for n in VEC_NAMES}

    lsum, dx0, grads = _local_step(
        x[0], positions[0], loss_target[0], gains, _pad_w0(full["l0_w_in"]), _pad_wq(full["l0_w_q_b"]),
        _pad_wkv(full["l0_w_kv_b"]), full["l0_w_out"], _pad_w1(full["l1_w_in"]), full["l1_w_out"])

    gfull = {"l0_w_in": _unpad_w0(grads["l0_w_in"]), "l0_w_q_b": _unpad_wq(grads["l0_w_q_b"]),
             "l0_w_kv_b": _unpad_wkv(grads["l0_w_kv_b"]), "l0_w_out": grads["l0_w_out"],
             "l1_w_in": grads["l1_w_in"][:, :ODD_IN_WIDTH], "l1_w_out": grads["l1_w_out"]}
    parts = [_cut_shards(n, gfull[n]).reshape(4, _mat_rows(n), LANES) for n in MAT_NAMES]
    used = sum(_mat_rows(n) for n in MAT_NAMES)
    parts.append(jnp.zeros((4, PACK_ROWS - used, LANES), F32))
    g_pack = jnp.concatenate(parts, axis=1)
    q_cores = _grad_add_cores(g_pack, _grad_core_exchange(g_pack), c1)
    g_mine = _grad_add_chips(q_cores, _grad_chip_exchange(q_cores), me1)
    g_theirs = _grad_core_gather(g_mine)

    small = _small_allreduce(jnp.concatenate([_pack_vecs({n: grads[n] for n in VEC_NAMES}),
                                              lsum.reshape(D_MODEL // LANES, LANES)], axis=0))
    g_small = small[:SMALL_ROWS]
    loss = 0.5 * jnp.sum(small[SMALL_ROWS:]) / float(D_MODEL)

    g_shard, d_pack, m_pack, v_pack = _adamw_mats(w_pack, g_mine, g_theirs, _pack_shards(m), _pack_shards(v), c1)
    d_small, m_small, v_small = _adamw(_pack_vecs(w), g_small, _pack_vecs(m), _pack_vecs(v), "adamw_vecs")

    def unpack(mat_pack, vec_pack):
        out = dict(_unpack_shards(mat_pack))
        out.update(_unpack_vecs(vec_pack))
        return [out[n] for n in WEIGHT_NAMES]

    return (loss, dx0[None], *unpack(g_shard, g_small), *unpack(d_pack, d_small), *unpack(m_pack, m_small),
            *unpack(v_pack, v_small))
```

```python
import functools

import numpy as np
import jax
import jax.numpy as jnp
from jax import lax
from jax.experimental import pallas as pl
from jax.experimental.pallas import tpu as pltpu

F32 = jnp.float32
BF16 = jnp.bfloat16
MESH = pl.DeviceIdType.MESH

D_MODEL = 1024
RMS_EPS = 1e-6
ROPE_THETA = 10000.0
SB_WIDTH = 512
MLA_Q_LORA = 384
MLA_KV_LORA = 256
MLA_ROPE_DIM = 32
MLA_WIDTH = 512
FOX_WIDTH = 1024
FOX_HEADS = 16
EVEN_IN_WIDTH = 3232
ODD_IN_WIDTH = 4112

ADAM_LR = 0.001
ADAM_B1 = 0.9
ADAM_B2 = 0.999
ADAM_EPS = 1e-08
ADAM_WD = 0.01
ADAM_STEP = 10

LANES = 128
VMEM_LIMIT = 56 * 1024 * 1024

L0_PREP = 0
L0_PREP_W = 768
L0_SBG = 768
L0_MLG = 1280
L0_SBQ = 1792
L0_SBK = 2304
L0_SBV = 2816
L0_WIDTH = 3328
L1_Q = 0
L1_K = 1024
L1_V = 2048
L1_G = 3072
L1_F = 4096
L1_WIDTH = 4224

ATT_T = 256
ATT_GROUP = 4
NEG = -1e30

PACK_ROWS = 20480
PACK_HALF = PACK_ROWS // 2
VEC_ROWS = 8
SMALL_ROWS = 7 * VEC_ROWS


def _cparams(sem, **kw):
    return pltpu.CompilerParams(dimension_semantics=sem, vmem_limit_bytes=VMEM_LIMIT, **kw)


def _dot(a, b):
    return lax.dot_general(a, b, (((1,), (0,)), ((), ())), preferred_element_type=F32)


def _dot_nt(a, b):
    return lax.dot_general(a, b, (((1,), (1,)), ((), ())), preferred_element_type=F32)


def _dot_tn(a, b):
    return lax.dot_general(a, b, (((0,), (0,)), ((), ())), preferred_element_type=F32)


def _sigmoid(x):
    return 1.0 / (1.0 + jnp.exp(-x))


def _rstd(x):
    return lax.rsqrt(jnp.mean(x * x, axis=-1, keepdims=True) + RMS_EPS)


def _norm_bwd(x, g, dy):
    r = _rstd(x)
    xn = x * r
    dxn = dy * g
    dx = r * (dxn - xn * jnp.mean(dxn * xn, axis=-1, keepdims=True))
    return dx, dy * xn


def _split3(x):
    hi = x.astype(BF16)
    r1 = x - hi.astype(F32)
    mid = r1.astype(BF16)
    lo = (r1 - mid.astype(F32)).astype(BF16)
    return hi, mid, lo


def _wide_tile(n, cap=1792):
    return max(t for t in range(LANES, min(n, cap) + 1, LANES) if n % t == 0)


def _pick(n, cands):
    for c in cands:
        if n % c == 0:
            return c
    raise ValueError(n)


def _norm_matmul(x, g, w, name):
    S, K = x.shape
    N = w.shape[1]
    tm = _pick(S, (512, 256))
    tn = _wide_tile(N)

    def body(x_ref, g_ref, w_ref, o_ref, ht_ref, h_ref):
        @pl.when(pl.program_id(1) == 0)
        def _():
            xv = x_ref[...]
            h = (xv * _rstd(xv)) * g_ref[...]
            h_ref[...] = h.astype(BF16)
            ht_ref[...] = h.T.astype(BF16)
        o_ref[...] = _dot(h_ref[...], w_ref[...])

    return pl.pallas_call(
        body, name=name, grid=(S // tm, N // tn),
        in_specs=[pl.BlockSpec((tm, K), lambda i, j: (i, 0)),
                  pl.BlockSpec((1, K), lambda i, j: (0, 0)),
                  pl.BlockSpec((K, tn), lambda i, j: (0, j))],
        out_specs=[pl.BlockSpec((tm, tn), lambda i, j: (i, j)),
                   pl.BlockSpec((K, tm), lambda i, j: (0, i))],
        out_shape=[jax.ShapeDtypeStruct((S, N), F32), jax.ShapeDtypeStruct((K, S), BF16)],
        scratch_shapes=[pltpu.VMEM((tm, K), BF16)],
        compiler_params=_cparams(("parallel", "arbitrary")),
    )(x, g, w)


def _matmul_t(at, b, name):
    M, S = at.shape
    N = b.shape[1]
    tn = _wide_tile(N)
    ts = _pick(S, (512, 256))

    def body(a_ref, b_ref, o_ref):
        @pl.when(pl.program_id(1) == 0)
        def _():
            o_ref[...] = jnp.zeros_like(o_ref)
        o_ref[...] += _dot(a_ref[...], b_ref[...].astype(BF16))

    return pl.pallas_call(
        body, name=name, grid=(N // tn, S // ts),
        in_specs=[pl.BlockSpec((M, ts), lambda j, k: (0, k)),
                  pl.BlockSpec((ts, tn), lambda j, k: (k, j))],
        out_specs=pl.BlockSpec((M, tn), lambda j, k: (0, j)),
        out_shape=jax.ShapeDtypeStruct((M, N), F32),
        compiler_params=_cparams(("parallel", "arbitrary")),
    )(at, b)


def _in_proj_bwd(pieces, w, x, g, dx_up, name):
    S, K = x.shape
    N = w.shape[1]
    tm = _pick(S, (256,))
    offs = [off for off, _ in pieces]
    arrs = [a for _, a in pieces]

    def body(*refs):
        d_refs = refs[:len(arrs)]
        w_ref, x_ref, g_ref, u_ref, dx_ref, dg_ref = refs[len(arrs):]

        @pl.when(pl.program_id(0) == 0)
        def _():
            dg_ref[...] = jnp.zeros_like(dg_ref)

        acc = None
        for off, d_ref in zip(offs, d_refs):
            part = _dot_nt(d_ref[...].astype(BF16), w_ref[:, off:off + d_ref.shape[1]])
            acc = part if acc is None else acc + part
        dx, dgrow = _norm_bwd(x_ref[...], g_ref[...], acc)
        dx_ref[...] = u_ref[...] + dx
        dg_ref[...] += jnp.sum(dgrow, axis=0, keepdims=True)

    row = lambda i: (i, 0)
    fixed = lambda i: (0, 0)
    return pl.pallas_call(
        body, name=name, grid=(S // tm,),
        in_specs=[pl.BlockSpec((tm, a.shape[1]), row) for a in arrs] + [
            pl.BlockSpec((K, N), fixed), pl.BlockSpec((tm, K), row), pl.BlockSpec((1, K), fixed),
            pl.BlockSpec((tm, K), row)],
        out_specs=[pl.BlockSpec((tm, K), row), pl.BlockSpec((1, K), fixed)],
        out_shape=[jax.ShapeDtypeStruct((S, K), F32), jax.ShapeDtypeStruct((1, K), F32)],
        compiler_params=_cparams(("arbitrary",)),
    )(*arrs, w, x, g, dx_up)


def _out_proj(og_a, og_b, blk_a, blk_b, w, x, g, target, name):
    S = x.shape[0]
    D = x.shape[1]
    tm = _pick(S, (512, 256))
    with_loss = target is not None

    def body(*refs):
        if with_loss:
            a_ref, b_ref, wa_ref, wb_ref, x_ref, g_ref, t_ref, y_ref, o_ref, l_ref = refs
        else:
            a_ref, b_ref, wa_ref, wb_ref, x_ref, g_ref, y_ref, o_ref = refs
        y = _dot(a_ref[...], wa_ref[...]) + _dot(b_ref[...], wb_ref[...])
        y_ref[...] = y
        xn = x_ref[...] + (y * _rstd(y)) * g_ref[...]
        if with_loss:
            @pl.when(pl.program_id(0) == 0)
            def _():
                l_ref[...] = jnp.zeros_like(l_ref)
            d = xn - t_ref[...]
            o_ref[...] = d / float(D)
            l_ref[...] += jnp.sum(d * d, axis=0, keepdims=True)
        else:
            o_ref[...] = xn

    row = lambda i: (i, 0)
    in_specs = [pl.BlockSpec((tm, 512), lambda i: (i, blk_a)),
                pl.BlockSpec((tm, 512), lambda i: (i, blk_b)),
                pl.BlockSpec((512, D), lambda i: (0, 0)),
                pl.BlockSpec((512, D), lambda i: (1, 0)),
                pl.BlockSpec((tm, D), row),
                pl.BlockSpec((1, D), lambda i: (0, 0))]
    out_specs = [pl.BlockSpec((tm, D), row), pl.BlockSpec((tm, D), row)]
    out_shape = [jax.ShapeDtypeStruct((S, D), F32), jax.ShapeDtypeStruct((S, D), F32)]
    args = [og_a, og_b, w, w, x, g]
    if with_loss:
        in_specs.append(pl.BlockSpec((tm, D), row))
        out_specs.append(pl.BlockSpec((1, D), lambda i: (0, 0)))
        out_shape.append(jax.ShapeDtypeStruct((1, D), F32))
        args.append(target)
    return pl.pallas_call(
        body, name=name, grid=(S // tm,), in_specs=in_specs, out_specs=out_specs, out_shape=out_shape,
        compiler_params=_cparams(("arbitrary",)),
    )(*args)


def _out_proj_bwd(dx_up, y, g, w, proj, gate_offs, o_a, o_b, oblk_a, oblk_b, name):
    S, D = y.shape
    tm = _pick(S, (256,))
    gblk = [off // 256 + c for off in gate_offs for c in range(2)]

    def body(u_ref, y_ref, g_ref, w_ref, g0, g1, g2, g3, oa_ref, ob_ref, dy_ref, do_ref, dgate_ref, dg_ref):
        @pl.when(pl.program_id(0) == 0)
        def _():
            dg_ref[...] = jnp.zeros_like(dg_ref)
        dy, dgrow = _norm_bwd(y_ref[...], g_ref[...], u_ref[...])
        dg_ref[...] += jnp.sum(dgrow, axis=0, keepdims=True)
        dyb = dy.astype(BF16)
        dy_ref[...] = dyb
        dog = _dot_nt(dyb, w_ref[...])
        gates = (g0, g1, g2, g3)
        for c in range(4):
            gt = gates[c][...]
            sg = _sigmoid(gt)
            o_ref = oa_ref if c < 2 else ob_ref
            ov = o_ref[:, (c % 2) * 256:(c % 2 + 1) * 256]
            dc = dog[:, c * 256:(c + 1) * 256]
            do_ref[:, c * 256:(c + 1) * 256] = dc * (gt * sg)
            dgate_ref[:, c * 256:(c + 1) * 256] = dc * ov * (sg * (1.0 + gt * (1.0 - sg)))

    row = lambda i: (i, 0)
    gspec = lambda c: pl.BlockSpec((tm, 256), lambda i: (i, gblk[c]))
    return pl.pallas_call(
        body, name=name, grid=(S // tm,),
        in_specs=[pl.BlockSpec((tm, D), row), pl.BlockSpec((tm, D), row), pl.BlockSpec((1, D), lambda i: (0, 0)),
                  pl.BlockSpec((D, D), lambda i: (0, 0)),
                  gspec(0), gspec(1), gspec(2), gspec(3),
                  pl.BlockSpec((tm, 512), lambda i: (i, oblk_a)),
                  pl.BlockSpec((tm, 512), lambda i: (i, oblk_b))],
        out_specs=[pl.BlockSpec((tm, D), row), pl.BlockSpec((tm, D), row), pl.BlockSpec((tm, D), row),
                   pl.BlockSpec((1, D), lambda i: (0, 0))],
        out_shape=[jax.ShapeDtypeStruct((S, D), BF16), jax.ShapeDtypeStruct((S, D), F32),
                   jax.ShapeDtypeStruct((S, D), F32), jax.ShapeDtypeStruct((1, D), F32)],
        compiler_params=_cparams(("arbitrary",)),
    )(dx_up, y, g, w, proj, proj, proj, proj, o_a, o_b)


def _rope_tables(pos, invf, name):
    S = pos.shape[0]
    tm = _pick(S, (512, 256))

    def body(p_ref, f_ref, c_ref, s1_ref, s2_ref):
        lane = lax.broadcasted_iota(jnp.int32, (1, LANES), 1)
        ang = p_ref[...].astype(F32) * f_ref[...]
        c, s = jnp.cos(ang), jnp.sin(ang)
        c_ref[...] = jnp.where((lane >= 64) & (lane < 96), c, 1.0)
        s1_ref[...] = jnp.where((lane >= 64) & (lane < 80), -s, 0.0)
        s2_ref[...] = jnp.where((lane >= 80) & (lane < 96), s, 0.0)

    spec = pl.BlockSpec((tm, LANES), lambda i: (i, 0))
    return pl.pallas_call(
        body, name=name, grid=(S // tm,),
        in_specs=[pl.BlockSpec((tm, 1), lambda i: (i, 0)), pl.BlockSpec((1, LANES), lambda i: (0, 0))],
        out_specs=[spec, spec, spec],
        out_shape=[jax.ShapeDtypeStruct((S, LANES), F32)] * 3,
        compiler_params=_cparams(("parallel",)),
    )(pos, invf)


def _rope(x, c, s1, s2):
    return x * c + pltpu.roll(x, LANES - 16, 1) * s1 + pltpu.roll(x, 16, 1) * s2


def _rope_t(d, c, s1, s2):
    return d * c + pltpu.roll(d * s1, 16, 1) + pltpu.roll(d * s2, LANES - 16, 1)


def _mla_prep(proj, gq, gkv, wq, wkv, cosT, s1T, s2T, name):
    S = proj.shape[0]
    tm = _pick(S, (256,))

    def body(p_ref, gq_ref, gkv_ref, wq_ref, wkv_ref, c_ref, s1_ref, s2_ref, q_ref, k_ref, v_ref, qn_ref, cn_ref):
        qa = p_ref[:, 0:384]
        ckv = p_ref[:, 384:640]
        kr = p_ref[:, 640:768]
        qn32 = (qa * _rstd(qa)) * gq_ref[...]
        cn32 = (ckv * _rstd(ckv)) * gkv_ref[...]
        qn = qn32.astype(BF16)
        cn = cn32.astype(BF16)
        qn_ref[...] = qn32.T.astype(BF16)
        cn_ref[...] = cn32.T.astype(BF16)
        qb = _dot(qn, wq_ref[...])
        kvb = _dot(cn, wkv_ref[...])
        c, s1, s2 = c_ref[...], s1_ref[...], s2_ref[...]
        krr = _rope(kr, c, s1, s2)
        for h in range(8):
            sl = slice(h * LANES, (h + 1) * LANES)
            q_ref[:, sl] = _rope(qb[:, sl], c, s1, s2)
            k_ref[:, sl] = kvb[:, sl] + krr
        v_ref[...] = kvb[:, 1024:1536]

    row = lambda i: (i, 0)
    fixed = lambda i: (0, 0)
    tspec = pl.BlockSpec((tm, LANES), row)
    return pl.pallas_call(
        body, name=name, grid=(S // tm,),
        in_specs=[pl.BlockSpec((tm, L0_PREP_W), lambda i: (i, L0_PREP // L0_PREP_W)),
                  pl.BlockSpec((1, 384), fixed), pl.BlockSpec((1, 256), fixed),
                  pl.BlockSpec((384, 1024), fixed), pl.BlockSpec((256, 1536), fixed), tspec, tspec, tspec],
        out_specs=[pl.BlockSpec((tm, 1024), row), pl.BlockSpec((tm, 1024), row), pl.BlockSpec((tm, 512), row),
                   pl.BlockSpec((384, tm), lambda i: (0, i)), pl.BlockSpec((256, tm), lambda i: (0, i))],
        out_shape=[jax.ShapeDtypeStruct((S, 1024), F32), jax.ShapeDtypeStruct((S, 1024), F32),
                   jax.ShapeDtypeStruct((S, 512), F32), jax.ShapeDtypeStruct((384, S), BF16),
                   jax.ShapeDtypeStruct((256, S), BF16)],
        compiler_params=_cparams(("parallel",)),
    )(proj, gq, gkv, wq, wkv, cosT, s1T, s2T)


def _mla_prep_bwd(dq, dk, dv, proj, gq, gkv, wq, wkv, cosT, s1T, s2T, name):
    S = proj.shape[0]
    tm = _pick(S, (256,))

    def body(dq_ref, dk_ref, dv_ref, p_ref, gq_ref, gkv_ref, wq_ref, wkv_ref, c_ref, s1_ref, s2_ref,
             dp_ref, dqb_ref, dkvb_ref, dgq_ref, dgkv_ref):
        @pl.when(pl.program_id(0) == 0)
        def _():
            dgq_ref[...] = jnp.zeros_like(dgq_ref)
            dgkv_ref[...] = jnp.zeros_like(dgkv_ref)
        c, s1, s2 = c_ref[...], s1_ref[...], s2_ref[...]
        lane = lax.broadcasted_iota(jnp.int32, (1, LANES), 1)
        dkr = jnp.zeros((tm, LANES), F32)
        for h in range(8):
            sl = slice(h * LANES, (h + 1) * LANES)
            dqb_ref[:, sl] = _rope_t(dq_ref[:, sl], c, s1, s2).astype(BF16)
            dkh = dk_ref[:, sl]
            dkvb_ref[:, sl] = dkh.astype(BF16)
            dkr = dkr + dkh
        dkvb_ref[:, 1024:1536] = dv_ref[...].astype(BF16)
        dkr = jnp.where((lane >= 64) & (lane < 96), _rope_t(dkr, c, s1, s2), 0.0)
        dqn = _dot_nt(dqb_ref[...], wq_ref[...])
        dcn = _dot_nt(dkvb_ref[...], wkv_ref[...])
        dqa, gq_row = _norm_bwd(p_ref[:, 0:384], gq_ref[...], dqn)
        dckv, gkv_row = _norm_bwd(p_ref[:, 384:640], gkv_ref[...], dcn)
        dp_ref[:, 0:384] = dqa
        dp_ref[:, 384:640] = dckv
        dp_ref[:, 640:768] = dkr
        dgq_ref[...] += jnp.sum(gq_row, axis=0, keepdims=True)
        dgkv_ref[...] += jnp.sum(gkv_row, axis=0, keepdims=True)

    row = lambda i: (i, 0)
    fixed = lambda i: (0, 0)
    tspec = pl.BlockSpec((tm, LANES), row)
    return pl.pallas_call(
        body, name=name, grid=(S // tm,),
        in_specs=[pl.BlockSpec((tm, 1024), row), pl.BlockSpec((tm, 1024), row), pl.BlockSpec((tm, 512), row),
                  pl.BlockSpec((tm, L0_PREP_W), lambda i: (i, L0_PREP // L0_PREP_W)),
                  pl.BlockSpec((1, 384), fixed), pl.BlockSpec((1, 256), fixed),
                  pl.BlockSpec((384, 1024), fixed), pl.BlockSpec((256, 1536), fixed), tspec, tspec, tspec],
        out_specs=[pl.BlockSpec((tm, L0_PREP_W), row), pl.BlockSpec((tm, 1024), row), pl.BlockSpec((tm, 1536), row),
                   pl.BlockSpec((1, 384), fixed), pl.BlockSpec((1, 256), fixed)],
        out_shape=[jax.ShapeDtypeStruct((S, L0_PREP_W), F32), jax.ShapeDtypeStruct((S, 1024), BF16),
                   jax.ShapeDtypeStruct((S, 1536), BF16), jax.ShapeDtypeStruct((1, 384), F32),
                   jax.ShapeDtypeStruct((1, 256), F32)],
        compiler_params=_cparams(("arbitrary",)),
    )(dq, dk, dv, proj, gq, gkv, wq, wkv, cosT, s1T, s2T)


def _fox_prep(proj, bf, name):
    S = proj.shape[0]
    tm = _pick(S, (256,))

    def body(f_ref, b_ref, c_ref, carry_ref):
        @pl.when(pl.program_id(0) == 0)
        def _():
            carry_ref[...] = jnp.zeros_like(carry_ref)
        u = f_ref[...] + b_ref[...]
        lf = jnp.minimum(u, 0.0) - jnp.log(1.0 + jnp.exp(-jnp.abs(u)))
        r = lax.broadcasted_iota(jnp.int32, (tm, tm), 0)
        cidx = lax.broadcasted_iota(jnp.int32, (tm, tm), 1)
        tri = (cidx <= r).astype(BF16)
        hi, mid, lo = _split3(lf)
        c = carry_ref[...] + (_dot(tri, hi) + _dot(tri, mid) + _dot(tri, lo))
        c_ref[...] = c
        carry_ref[...] = c[tm - 1:tm, :]

    return pl.pallas_call(
        body, name=name, grid=(S // tm,),
        in_specs=[pl.BlockSpec((tm, LANES), lambda i: (i, L1_F // LANES)), pl.BlockSpec((1, LANES), lambda i: (0, 0))],
        out_specs=pl.BlockSpec((tm, LANES), lambda i: (i, 0)),
        out_shape=jax.ShapeDtypeStruct((S, LANES), F32),
        scratch_shapes=[pltpu.VMEM((1, LANES), F32)],
        compiler_params=_cparams(("arbitrary",)),
    )(proj, bf)


def _fox_prep_bwd(dc, proj, bf, name):
    S = proj.shape[0]
    tm = _pick(S, (256,))
    nb = S // tm

    def body(dc_ref, f_ref, b_ref, df_ref, db_ref, carry_ref):
        @pl.when(pl.program_id(0) == 0)
        def _():
            carry_ref[...] = jnp.zeros_like(carry_ref)
            db_ref[...] = jnp.zeros_like(db_ref)
        r = lax.broadcasted_iota(jnp.int32, (tm, tm), 0)
        cidx = lax.broadcasted_iota(jnp.int32, (tm, tm), 1)
        tri = (cidx >= r).astype(BF16)
        hi, mid, lo = _split3(dc_ref[...])
        dlf = carry_ref[...] + (_dot(tri, hi) + _dot(tri, mid) + _dot(tri, lo))
        carry_ref[...] = dlf[0:1, :]
        u = f_ref[...] + b_ref[...]
        e = jnp.exp(-jnp.abs(u))
        sneg = jnp.where(u >= 0.0, e, 1.0) / (1.0 + e)
        lane = lax.broadcasted_iota(jnp.int32, (1, LANES), 1)
        df = jnp.where(lane < FOX_HEADS, dlf * sneg, 0.0)
        df_ref[...] = df
        db_ref[...] += jnp.sum(df, axis=0, keepdims=True)

    return pl.pallas_call(
        body, name=name, grid=(nb,),
        in_specs=[pl.BlockSpec((tm, LANES), lambda i: (nb - 1 - i, 0)),
                  pl.BlockSpec((tm, LANES), lambda i: (nb - 1 - i, L1_F // LANES)),
                  pl.BlockSpec((1, LANES), lambda i: (0, 0))],
        out_specs=[pl.BlockSpec((tm, LANES), lambda i: (nb - 1 - i, 0)), pl.BlockSpec((1, LANES), lambda i: (0, 0))],
        out_shape=[jax.ShapeDtypeStruct((S, LANES), F32), jax.ShapeDtypeStruct((1, LANES), F32)],
        scratch_shapes=[pltpu.VMEM((1, LANES), F32)],
        compiler_params=_cparams(("arbitrary",)),
    )(dc, proj, bf)


def _att_specs(kind, S, T):
    if kind == "sb":
        qo, ko, vo, go = L0_SBQ // LANES, L0_SBK // LANES, L0_SBV // LANES, L0_SBG // LANES
    elif kind == "fox":
        qo, ko, vo, go = L1_Q // LANES, L1_K // LANES, L1_V // LANES, L1_G // LANES
    else:
        go = L0_MLG // LANES
        return (pl.BlockSpec((T, 256), lambda p, i: (i, p)), pl.BlockSpec((S, 256), lambda p, i: (0, p)),
                pl.BlockSpec((S, LANES), lambda p, i: (0, p)), pl.BlockSpec((T, LANES), lambda p, i: (i, go + p)))
    return (pl.BlockSpec((T, LANES), lambda p, i: (i, qo + p)), pl.BlockSpec((S, LANES), lambda p, i: (0, ko + p)),
            pl.BlockSpec((S, LANES), lambda p, i: (0, vo + p)), pl.BlockSpec((T, LANES), lambda p, i: (i, go + p)))


def _mask_flags(js, masked_at):
    return [t == masked_at for t in range(len(js))]


def _loop_tiles(i, tiles, right_to_left):
    G = ATT_GROUP
    ng = i // G
    rest = i - ng * G

    def leftover():
        for r in range(G):
            @pl.when(rest == r)
            def _():
                if right_to_left:
                    tiles([i - u for u in range(r + 1)], 0)
                else:
                    tiles([ng * G + u for u in range(r + 1)], r)

    def group(g, carry):
        if right_to_left:
            tiles([ng * G - 1 - (g * G + u) for u in range(G)], None)
        else:
            tiles([g * G + u for u in range(G)], None)
        return carry

    if right_to_left:
        leftover()
    lax.fori_loop(0, ng, group, 0)
    if not right_to_left:
        leftover()


def _head_q(kind, q_ref, m0, scale):
    if kind == "mla":
        return [q_ref[:, 0:LANES].astype(BF16), q_ref[:, LANES:2 * LANES].astype(BF16)]
    qv = q_ref[...] * scale
    return [jnp.where(m0, qv, 0.0).astype(BF16), jnp.where(m0, 0.0, qv).astype(BF16)]


def _head_k(kind, k_ref, start, T):
    if kind == "mla":
        return [k_ref[pl.ds(start, T), 0:LANES].astype(BF16), k_ref[pl.ds(start, T), LANES:2 * LANES].astype(BF16)]
    kb = k_ref[pl.ds(start, T), :].astype(BF16)
    return [kb, kb]


def _transpose_tiles(src, col_off, n_out, cw, group, name):
    S = src.shape[0]
    T = ATT_T
    first = col_off // (group * cw)

    def body(x_ref, o_ref):
        for u in range(group):
            o_ref[u] = x_ref[:, u * cw:(u + 1) * cw].T.astype(BF16)

    return pl.pallas_call(
        body, name=name, grid=(S // T, n_out // group),
        in_specs=[pl.BlockSpec((T, group * cw), lambda j, g: (j, first + g))],
        out_specs=pl.BlockSpec((group, None, cw, T), lambda j, g: (g, j, 0, 0)),
        out_shape=jax.ShapeDtypeStruct((n_out, S // T, cw, T), BF16),
        compiler_params=_cparams(("parallel", "parallel")),
    )(src)


def _softmax_fwd(kind, qkvg, c_col, S, npairs, name):
    T = ATT_T
    nq = S // T
    fox = kind == "fox"
    scale = (96 if kind == "mla" else 64) ** -0.5

    def body(*refs):
        if fox:
            q_ref, k_ref, vt_ref, g_ref, cc_ref, o_ref, og_ref, ogt_ref, st_ref, m_ref, acc_ref = refs
        else:
            q_ref, k_ref, vt_ref, g_ref, o_ref, og_ref, ogt_ref, st_ref, m_ref, acc_ref = refs
        i = pl.program_id(1)
        m0 = lax.broadcasted_iota(jnp.int32, (1, LANES), 1) < 64
        top = lax.broadcasted_iota(jnp.int32, (LANES, 1), 0) < 64
        key = lax.broadcasted_iota(jnp.int32, (T, LANES), 0)
        qrow = lax.broadcasted_iota(jnp.int32, (T, LANES), 1)
        qh = _head_q(kind, q_ref, m0, scale)
        m_ref[...] = jnp.full(m_ref.shape, NEG, F32)
        acc_ref[...] = jnp.zeros(acc_ref.shape, F32)
        chains = [(h, b) for h in range(2) for b in range(T // LANES)]

        def tiles(js, masked_at):
            starts = [pl.multiple_of(j * T, T) for j in js]
            zss = []
            for start in starts:
                kh = _head_k(kind, k_ref, start, T)
                zss.append(_split_blocks([_dot_nt(kh[h], qh[h]) for h in range(2)]))
            pss, alss = [], []
            for start, zs, masked in zip(starts, zss, _mask_flags(js, masked_at)):
                ps, alphas = [], []
                for (h, b), z in zip(chains, zs):
                    lanes = slice(b * LANES, (b + 1) * LANES)
                    if kind == "mla":
                        z = z * scale
                    if fox:
                        z = z - cc_ref[h, pl.ds(start, T), :]
                    if masked:
                        z = jnp.where(key <= qrow + b * LANES, z, NEG)
                    m_prev = m_ref[h, :, lanes]
                    m_new = jnp.maximum(m_prev, jnp.max(z, axis=0, keepdims=True))
                    alphas.append(jnp.exp(m_prev - m_new))
                    ps.append(jnp.exp(z - m_new).astype(BF16))
                    m_ref[h, :, lanes] = m_new
                pss.append(_join_blocks(ps, T // LANES))
                alss.append(_join_blocks(alphas, T // LANES))
            for j, ps, alphas in zip(js, pss, alss):
                vt = vt_ref[j]
                vth = [jnp.where(top, vt, 1.0).astype(BF16), jnp.where(top, 1.0, vt).astype(BF16)]
                for h in range(2):
                    acc_ref[h] = alphas[h] * acc_ref[h] + _dot(vth[h], ps[h])

        _loop_tiles(i, tiles, False)
        acc = [acc_ref[0], acc_ref[1]]
        ot = jnp.concatenate([acc[0][0:64] / acc[0][64:128], acc[1][64:128] / acc[1][0:64]], axis=0)
        o = ot.T
        o_ref[...] = o
        gt = g_ref[...]
        og = o * (gt * _sigmoid(gt))
        og_ref[...] = og.astype(BF16)
        ogt_ref[...] = og.T.astype(BF16)
        st_ref[0] = m_ref[0] + jnp.log(acc[0][64:65])
        st_ref[1] = m_ref[1] + jnp.log(acc[1][0:1])

    qs, ks, _, gs = _att_specs(kind, S, T)
    in_specs = [qs, ks, pl.BlockSpec((None, nq, LANES, T), lambda p, i: (p, 0, 0, 0)), gs]
    args = list(qkvg)
    if fox:
        in_specs += [pl.BlockSpec((2, S, LANES), lambda p, i: (p, 0, 0))]
        args += [c_col]
    W = npairs * LANES
    return pl.pallas_call(
        body, name=name, grid=(npairs, nq), in_specs=in_specs,
        out_specs=[pl.BlockSpec((T, LANES), lambda p, i: (i, p)), pl.BlockSpec((T, LANES), lambda p, i: (i, p)),
                   pl.BlockSpec((LANES, T), lambda p, i: (p, i)),
                   pl.BlockSpec((2, None, 1, T), lambda p, i: (p, i, 0, 0))],
        out_shape=[jax.ShapeDtypeStruct((S, W), F32), jax.ShapeDtypeStruct((S, W), BF16),
                   jax.ShapeDtypeStruct((W, S), BF16),
                   jax.ShapeDtypeStruct((2 * npairs, nq, 1, T), F32)],
        scratch_shapes=[pltpu.VMEM((2, 1, T), F32), pltpu.VMEM((2, LANES, T), F32)],
        compiler_params=_cparams(("parallel", "parallel")),
    )(*args)


def _softmax_bwd(kind, qkv, do, do_off, o, lse, c_row, S, npairs, name):
    T = ATT_T
    nq = S // T
    fox = kind == "fox"
    mla = kind == "mla"
    scale = (96 if mla else 64) ** -0.5
    kw = 256 if mla else LANES

    def body(*refs):
        if fox:
            q_ref, k_ref, v_ref, do_ref, o_ref, st_ref, cr_ref, dq_ref, dk_ref, dv_ref, dc_ref, dcq_ref = refs
        else:
            q_ref, k_ref, v_ref, do_ref, o_ref, st_ref, dq_ref, dk_ref, dv_ref = refs
        i = pl.program_id(1)

        @pl.when(i == 0)
        def _():
            dk_ref[...] = jnp.zeros_like(dk_ref)
            dv_ref[...] = jnp.zeros_like(dv_ref)
            if fox:
                dc_ref[...] = jnp.zeros_like(dc_ref)

        m0 = lax.broadcasted_iota(jnp.int32, (1, LANES), 1) < 64
        causal = lax.broadcasted_iota(jnp.int32, (T, T), 1) <= lax.broadcasted_iota(jnp.int32, (T, T), 0)
        qh = _head_q(kind, q_ref, m0, scale)
        dov = do_ref[...]
        prod = dov * o_ref[...]
        dd = [jnp.sum(jnp.where(m0, prod, 0.0), axis=1, keepdims=True),
              jnp.sum(jnp.where(m0, 0.0, prod), axis=1, keepdims=True)]
        doh = [jnp.where(m0, dov, 0.0).astype(BF16), jnp.where(m0, 0.0, dov).astype(BF16)]
        lse_h = [st_ref[0], st_ref[1]]

        def tile(j, carry, masked):
            start = pl.multiple_of(j * T, T)
            vb = v_ref[pl.ds(start, T), :].astype(BF16)
            kh = _head_k(kind, k_ref, start, T)
            dqs = []
            dkc = []
            dvc = jnp.zeros((T, LANES), F32)
            for h in range(2):
                z = _dot_nt(qh[h], kh[h])
                if mla:
                    z = z * scale
                if fox:
                    z = z - cr_ref[h, pl.ds(j, 1), :]
                if masked:
                    z = jnp.where(causal, z, NEG)
                p = jnp.exp(z - lse_h[h])
                ds = p * (_dot_nt(doh[h], vb) - dd[h])
                dsb = ds.astype(BF16)
                dqh = carry[h][0] + _dot(dsb, kh[h])
                dkc.append(_dot_tn(dsb, qh[h]))
                dvc = dvc + _dot_tn(p.astype(BF16), doh[h])
                if fox:
                    dc_ref[h, pl.ds(j, 1), :] += -jnp.sum(ds, axis=0, keepdims=True)
                    dqs.append((dqh, carry[h][1] + jnp.sum(ds, axis=1, keepdims=True)))
                else:
                    dqs.append((dqh,))
            if mla:
                dk_ref[pl.ds(start, T), 0:LANES] += dkc[0] * scale
                dk_ref[pl.ds(start, T), LANES:2 * LANES] += dkc[1] * scale
            else:
                dk_ref[pl.ds(start, T), :] += dkc[0] + dkc[1]
            dv_ref[pl.ds(start, T), :] += dvc
            return tuple(dqs)

        one = (jnp.zeros((T, LANES), F32), jnp.zeros((T, 1), F32)) if fox else (jnp.zeros((T, LANES), F32),)
        carry = lax.fori_loop(0, i, lambda j, c: tile(j, c, False), (one, one))
        carry = tile(i, carry, True)
        if mla:
            dq_ref[:, 0:LANES] = carry[0][0] * scale
            dq_ref[:, LANES:2 * LANES] = carry[1][0] * scale
        else:
            dq_ref[...] = jnp.where(m0, carry[0][0], carry[1][0]) * scale
        if fox:
            dcq_ref[0] = carry[0][1]
            dcq_ref[1] = carry[1][1]

    qs, ks, vs, _ = _att_specs(kind, S, T)
    in_specs = [qs, ks, vs,
                pl.BlockSpec((T, LANES), lambda p, i: (i, do_off + p)),
                pl.BlockSpec((T, LANES), lambda p, i: (i, p)),
                pl.BlockSpec((2, T, 1), lambda p, i: (p, i, 0))]
    args = list(qkv) + [do, o, lse]
    W = npairs * LANES
    out_specs = [pl.BlockSpec((T, kw), lambda p, i: (i, p)), pl.BlockSpec((S, kw), lambda p, i: (0, p)),
                 pl.BlockSpec((S, LANES), lambda p, i: (0, p))]
    out_shape = [jax.ShapeDtypeStruct((S, npairs * kw), F32), jax.ShapeDtypeStruct((S, npairs * kw), F32),
                 jax.ShapeDtypeStruct((S, W), F32)]
    if fox:
        in_specs += [pl.BlockSpec((2, nq, T), lambda p, i: (p, 0, 0))]
        args += [c_row]
        out_specs +=[pl.BlockSpec((2, nq, T), lambda p, i: (p, 0, 0)), pl.BlockSpec((2, T, 1), lambda p, i: (p, i, 0))]
        out_shape += [jax.ShapeDtypeStruct((2 * npairs, nq, T), F32), jax.ShapeDtypeStruct((2 * npairs, S, 1), F32)]
    return pl.pallas_call(
        body, name=name, grid=(npairs, nq), in_specs=in_specs, out_specs=out_specs, out_shape=out_shape,
        compiler_params=_cparams(("parallel", "arbitrary")),
    )(*args)


def _softplus_parts(z):
    sp = jnp.maximum(z, 0.0) + jnp.log(1.0 + jnp.exp(-jnp.abs(z)))
    return -sp, z - sp


def _split2(x):
    hi = x.astype(BF16)
    return hi, (x - hi.astype(F32)).astype(BF16)


def _sb_fwd(proj, S, npairs, name):
    T = ATT_T
    nq = S // T
    scale = 64 ** -0.5

    def body(q_ref, k_ref, v_ref, g_ref, o_ref, og_ref, st_ref):
        i = pl.program_id(1)
        m0 = lax.broadcasted_iota(jnp.int32, (1, LANES), 1) < 64
        r = lax.broadcasted_iota(jnp.int32, (T, T), 0)
        c = lax.broadcasted_iota(jnp.int32, (T, T), 1)
        before = c < r
        after = (r > c).astype(BF16)
        qh = _head_q("sb", q_ref, m0, scale)

        def tile(j, carry, masked):
            start = pl.multiple_of(j * T, T)
            vb = v_ref[pl.ds(start, T), :].astype(BF16)
            kb = k_ref[pl.ds(start, T), :].astype(BF16)
            out = []
            for h in range(2):
                rem, acc = carry[h]
                z = _dot_nt(qh[h], kb)
                lk, la = _softplus_parts(z)
                if masked:
                    lk = jnp.where(before, lk, 0.0)
                hi, lo = _split2(lk)
                lr = rem + (_dot(hi, after) + _dot(lo, after))
                w = jnp.exp(la + lr)
                if masked:
                    w = jnp.where(before, w, 0.0)
                out.append((rem + jnp.sum(lk, axis=1, keepdims=True), acc + _dot(w.astype(BF16), vb)))
            return tuple(out)

        init = tuple((jnp.zeros((T, 1), F32), jnp.zeros((T, LANES), F32)) for _ in range(2))
        carry = tile(i, init, True)
        carry = lax.fori_loop(0, i, lambda jj, cr: tile(i - 1 - jj, cr, False), carry)
        o = jnp.where(m0, carry[0][1], carry[1][1])
        o_ref[...] = o
        gt = g_ref[...]
        og_ref[...] = (o * (gt * _sigmoid(gt))).astype(BF16)
        for h in range(2):
            st_ref[h] = carry[h][0]

    W = npairs * LANES
    return pl.pallas_call(
        body, name=name, grid=(npairs, nq), in_specs=list(_att_specs("sb", S, T)),
        out_specs=[pl.BlockSpec((T, LANES), lambda p, i: (i, p)), pl.BlockSpec((T, LANES), lambda p, i: (i, p)),
                   pl.BlockSpec((2, T, 1), lambda p, i: (p, i, 0))],
        out_shape=[jax.ShapeDtypeStruct((S, W), F32), jax.ShapeDtypeStruct((S, W), BF16),
                   jax.ShapeDtypeStruct((2 * npairs, S, 1), F32)],
        compiler_params=_cparams(("parallel", "parallel")),
    )(proj, proj, proj, proj)


def _sb_bwd(proj, do, tot, S, npairs, name):
    T = ATT_T
    nq = S // T
    scale = 64 ** -0.5

    def body(q_ref, k_ref, v_ref, do_ref, st_ref, dq_ref, dk_ref, dv_ref):
        i = pl.program_id(1)

        @pl.when(i == 0)
        def _():
            dk_ref[...] = jnp.zeros_like(dk_ref)
            dv_ref[...] = jnp.zeros_like(dv_ref)

        m0 = lax.broadcasted_iota(jnp.int32, (1, LANES), 1) < 64
        r = lax.broadcasted_iota(jnp.int32, (T, T), 0)
        c = lax.broadcasted_iota(jnp.int32, (T, T), 1)
        before = c < r
        upto = (r <= c).astype(BF16)
        left = (r < c).astype(BF16)
        qh = _head_q("sb", q_ref, m0, scale)
        dov = do_ref[...]
        doh = [jnp.where(m0, dov, 0.0).astype(BF16), jnp.where(m0, 0.0, dov).astype(BF16)]
        tot_h = [st_ref[0], st_ref[1]]

        def tile(j, carry, masked):
            start = pl.multiple_of(j * T, T)
            vb = v_ref[pl.ds(start, T), :].astype(BF16)
            kb = k_ref[pl.ds(start, T), :].astype(BF16)
            out = []
            dkc = jnp.zeros((T, LANES), F32)
            dvc = jnp.zeros((T, LANES), F32)
            for h in range(2):
                pre, gpre, dq = carry[h]
                z = _dot_nt(qh[h], kb)
                lk, la = _softplus_parts(z)
                if masked:
                    lk = jnp.where(before, lk, 0.0)
                hi, lo = _split2(lk)
                lr = (tot_h[h] - pre) - (_dot(hi, upto) + _dot(lo, upto))
                w = jnp.exp(la + lr)
                if masked:
                    w = jnp.where(before, w, 0.0)
                g = _dot_nt(doh[h], vb) * w
                gfull = gpre + _dot(g.astype(BF16), left)
                dz = g - (g + gfull) * jnp.exp(la)
                if masked:
                    dz = jnp.where(before, dz, 0.0)
                dzb = dz.astype(BF16)
                dkc = dkc + _dot_tn(dzb, qh[h])
                dvc = dvc + _dot_tn(w.astype(BF16), doh[h])
                out.append((pre + jnp.sum(lk, axis=1, keepdims=True), gpre + jnp.sum(g, axis=1, keepdims=True),
                            dq + _dot(dzb, kb)))
            dk_ref[pl.ds(start, T), :] += dkc
            dv_ref[pl.ds(start, T), :] += dvc
            return tuple(out)

        init = tuple((jnp.zeros((T, 1), F32), jnp.zeros((T, 1), F32), jnp.zeros((T, LANES), F32)) for _ in range(2))
        carry = lax.fori_loop(0, i, lambda j, cr: tile(j, cr, False), init)
        carry = tile(i, carry, True)
        dq_ref[...] = jnp.where(m0, carry[0][2], carry[1][2]) * scale

    qs, ks, vs, _ = _att_specs("sb", S, T)
    W = npairs * LANES
    return pl.pallas_call(
        body, name=name, grid=(npairs, nq),
        in_specs=[qs, ks, vs, pl.BlockSpec((T, LANES), lambda p, i: (i, p)),
                  pl.BlockSpec((2, T, 1), lambda p, i: (p, i, 0))],
        out_specs=[pl.BlockSpec((T, LANES), lambda p, i: (i, p)), pl.BlockSpec((S, LANES), lambda p, i: (0, p)),
                   pl.BlockSpec((S, LANES), lambda p, i: (0, p))],
        out_shape=[jax.ShapeDtypeStruct((S, W), F32)] * 3,
        compiler_params=_cparams(("parallel", "arbitrary")),
    )(proj, proj, proj, do, tot)


def _split_blocks(per_head):
    return [x[:, b * LANES:(b + 1) * LANES] for x in per_head for b in range(x.shape[1] // LANES)]


def _join_blocks(per_block, nb):
    return [jnp.concatenate(per_block[h * nb:(h + 1) * nb], axis=1) for h in range(len(per_block) // nb)]


def _row_of(col):
    return jnp.broadcast_to(col, (col.shape[0], LANES)).T[0:1]


def _softmax_bwd_t(kind, q, k, kt, v, do, do_off, o, lse, c_col, S, npairs, name):
    T = ATT_T
    nq = S // T
    nb = T // LANES
    fox = kind == "fox"
    mla = kind == "mla"
    scale = (96 if mla else 64) ** -0.5
    kw = 256 if mla else LANES

    def body(*refs):
        if fox:
            (q_ref, k_ref, kt_ref, v_ref, do_ref, o_ref, st_ref, cc_ref,
             dq_ref, dk_ref, dv_ref, dck_ref, dcq_ref, dqt_ref, rs_ref, dkx_ref) = refs
        else:
            q_ref, k_ref, kt_ref, v_ref, do_ref, o_ref, st_ref, dq_ref, dk_ref, dv_ref, dqt_ref = refs
        i = pl.program_id(1)

        @pl.when(i == 0)
        def _():
            dv_ref[...] = jnp.zeros_like(dv_ref)
            if fox:
                dkx_ref[...] = jnp.zeros_like(dkx_ref)
            else:
                dk_ref[...] = jnp.zeros_like(dk_ref)

        m0 = lax.broadcasted_iota(jnp.int32, (1, LANES), 1) < 64
        top = lax.broadcasted_iota(jnp.int32, (LANES, 1), 0) < 64
        key = lax.broadcasted_iota(jnp.int32, (T, LANES), 0)
        qrow = lax.broadcasted_iota(jnp.int32, (T, LANES), 1)
        qh = _head_q(kind, q_ref, m0, scale)
        if fox:
            qv = q_ref[...] * scale
            qk = [jnp.where(m0, qv, 1.0).astype(BF16), jnp.where(m0, 1.0, qv).astype(BF16)]
        else:
            qk = qh
        dov = do_ref[...]
        prod = dov * o_ref[...]
        dd = [_row_of(jnp.sum(jnp.where(m0, prod, 0.0), axis=1, keepdims=True)),
              _row_of(jnp.sum(jnp.where(m0, 0.0, prod), axis=1, keepdims=True))]
        doh = [jnp.where(m0, dov, 0.0).astype(BF16), jnp.where(m0, 0.0, dov).astype(BF16)]
        lse = [st_ref[0], st_ref[1]]
        dqt_ref[...] = jnp.zeros_like(dqt_ref)
        if fox:
            rs_ref[...] = jnp.zeros_like(rs_ref)
        chains = [(h, b) for h in range(2) for b in range(nb)]

        def tiles(js, masked_at):
            starts = [pl.multiple_of(j * T, T) for j in js]
            zss, dpss = [], []
            for start in starts:
                vb = v_ref[pl.ds(start, T), :].astype(BF16)
                kh = _head_k(kind, k_ref, start, T)
                zss.append(_split_blocks([_dot_nt(kh[h], qh[h]) for h in range(2)]))
                dpss.append(_split_blocks([_dot_nt(vb, doh[h]) for h in range(2)]))
            pss, dsss = [], []
            for start, zs, dps, masked in zip(starts, zss, dpss, _mask_flags(js, masked_at)):
                ps, dss = [], []
                for (h, b), z, dp in zip(chains, zs, dps):
                    lanes = slice(b * LANES, (b + 1) * LANES)
                    if mla:
                        z = z * scale
                    if fox:
                        z = z - cc_ref[h, pl.ds(start, T), :]
                    if masked:
                        z = jnp.where(key <= qrow + b * LANES, z, NEG)
                    p = jnp.exp(z - lse[h][:, lanes])
                    ds = p * (dp - dd[h][:, lanes])
                    dsb = ds.astype(BF16)
                    if fox:
                        rs_ref[h, :, lanes] += jnp.sum(dsb.astype(F32), axis=0, keepdims=True)
                    ps.append(p.astype(BF16))
                    dss.append(dsb)
                pss.append(_join_blocks(ps, nb))
                dsss.append(_join_blocks(dss, nb))
            for j, start, ps, dss in zip(js, starts, pss, dsss):
                kt = kt_ref[j]
                dvc = None
                for h in range(2):
                    dkh = _dot(dss[h], qk[h])
                    dvh = _dot(ps[h], doh[h])
                    dvc = dvh if dvc is None else dvc + dvh
                    kth = kt[h * LANES:(h + 1) * LANES] if mla else kt
                    dqt_ref[h] += _dot(kth, dss[h])
                    if fox:
                        dkx_ref[h, pl.ds(start, T), :] += dkh
                    elif mla:
                        dk_ref[pl.ds(start, T), h * LANES:(h + 1) * LANES] += dkh * scale
                    else:
                        dk_ref[pl.ds(start, T), :] += dkh
                dv_ref[pl.ds(start, T), :] += dvc

        _loop_tiles(i, tiles, False)
        if mla:
            dq_ref[:, 0:LANES] = dqt_ref[0].T * scale
            dq_ref[:, LANES:2 * LANES] = dqt_ref[1].T * scale
        else:
            dq_ref[...] = jnp.where(top, dqt_ref[0], dqt_ref[1]).T * scale
        if fox:
            dcq_ref[0] = rs_ref[0]
            dcq_ref[1] = rs_ref[1]

            @pl.when(i == nq - 1)
            def _():
                dk_ref[...] = jnp.where(m0, dkx_ref[0], dkx_ref[1])
                dck_ref[0] = dkx_ref[0][:, 64:65]
                dck_ref[1] = dkx_ref[1][:, 0:1]

    qs, ks, vs, _ = _att_specs(kind, S, T)
    stat = pl.BlockSpec((2, None, 1, T), lambda p, i: (p, i, 0, 0))
    in_specs = [qs, ks, pl.BlockSpec((None, nq, kw, T), lambda p, i: (p, 0, 0, 0)), vs,
                pl.BlockSpec((T, LANES), lambda p, i: (i, do_off + p)),
                pl.BlockSpec((T, LANES), lambda p, i: (i, p)), stat]
    args = [q, k, kt, v, do, o, lse]
    W = npairs * LANES
    out_specs = [pl.BlockSpec((T, kw), lambda p, i: (i, p)), pl.BlockSpec((S, kw), lambda p, i: (0, p)),
                 pl.BlockSpec((S, LANES), lambda p, i: (0, p))]
    out_shape = [jax.ShapeDtypeStruct((S, npairs * kw), F32), jax.ShapeDtypeStruct((S, npairs * kw), F32),
                 jax.ShapeDtypeStruct((S, W), F32)]
    scratch = [pltpu.VMEM((2, LANES, T), F32)]
    if fox:
        in_specs.append(pl.BlockSpec((2, S, LANES), lambda p, i: (p, 0, 0)))
        args.append(c_col)
        out_specs += [pl.BlockSpec((2, S, 1), lambda p, i: (p, 0, 0)), stat]
        out_shape += [jax.ShapeDtypeStruct((2 * npairs, S, 1), F32), jax.ShapeDtypeStruct((2 * npairs, nq, 1, T), F32)]
        scratch += [pltpu.VMEM((2, 1, T), F32), pltpu.VMEM((2, S, LANES), F32)]
    return pl.pallas_call(
        body, name=name, grid=(npairs, nq), in_specs=in_specs, out_specs=out_specs, out_shape=out_shape,
        scratch_shapes=scratch, compiler_params=_cparams(("parallel", "arbitrary")),
    )(*args)


def _sb_fwd_t(proj, vt, S, npairs, name):
    T = ATT_T
    nq = S // T
    nb = T // LANES
    scale = 64 ** -0.5

    def body(q_ref, k_ref, vt_ref, g_ref, o_ref, og_ref, ogt_ref, st_ref, rem_ref, acc_ref):
        i = pl.program_id(1)
        m0 = lax.broadcasted_iota(jnp.int32, (1, LANES), 1) < 64
        top = lax.broadcasted_iota(jnp.int32, (LANES, 1), 0) < 64
        key = lax.broadcasted_iota(jnp.int32, (T, LANES), 0)
        qrow = lax.broadcasted_iota(jnp.int32, (T, LANES), 1)
        r = lax.broadcasted_iota(jnp.int32, (T, T), 0)
        c = lax.broadcasted_iota(jnp.int32, (T, T), 1)
        after = (c > r).astype(BF16)
        qh = _head_q("sb", q_ref, m0, scale)
        rem_ref[...] = jnp.zeros_like(rem_ref)
        acc_ref[...] = jnp.zeros_like(acc_ref)
        chains = [(h, b) for h in range(2) for b in range(nb)]

        def tiles(js, masked_at):
            zss = []
            for j in js:
                kb = k_ref[pl.ds(pl.multiple_of(j * T, T), T), :].astype(BF16)
                zss.append(_split_blocks([_dot_nt(kb, qh[h]) for h in range(2)]))
            lass, sums, hiss, loss = [], [], [], []
            for zs, masked in zip(zss, _mask_flags(js, masked_at)):
                las, sm, his, los = [], [], [], []
                for (h, b), z in zip(chains, zs):
                    lk, la = _softplus_parts(z)
                    if masked:
                        lk = jnp.where(key < qrow + b * LANES, lk, 0.0)
                    hi, lo = _split2(lk)
                    las.append(la)
                    sm.append(jnp.sum(lk, axis=0, keepdims=True))
                    his.append(hi)
                    los.append(lo)
                lass.append(las)
                sums.append(sm)
                hiss.append(_join_blocks(his, nb))
                loss.append(_join_blocks(los, nb))
            rcss = [_split_blocks([_dot(after, hi) + _dot(after, lo) for hi, lo in zip(his, los)])
                    for his, los in zip(hiss, loss)]
            wss = []
            for las, sm, rcs, masked in zip(lass, sums, rcss, _mask_flags(js, masked_at)):
                ws = []
                for (h, b), la, s, rc in zip(chains, las, sm, rcs):
                    lanes = slice(b * LANES, (b + 1) * LANES)
                    w = jnp.exp(la + (rem_ref[h, :, lanes] + rc))
                    if masked:
                        w = jnp.where(key < qrow + b * LANES, w, 0.0)
                    ws.append(w.astype(BF16))
                    rem_ref[h, :, lanes] += s
                wss.append(_join_blocks(ws, nb))
            for j, ws in zip(js, wss):
                vtb = vt_ref[j]
                for h in range(2):
                    acc_ref[h] += _dot(vtb, ws[h])

        _loop_tiles(i, tiles, True)
        o = jnp.where(top, acc_ref[0], acc_ref[1]).T
        o_ref[...] = o
        gt = g_ref[...]
        og = o * (gt * _sigmoid(gt))
        og_ref[...] = og.astype(BF16)
        ogt_ref[...] = og.T.astype(BF16)
        st_ref[0] = rem_ref[0]
        st_ref[1] = rem_ref[1]

    qs, ks, _, gs = _att_specs("sb", S, T)
    W = npairs * LANES
    return pl.pallas_call(
        body, name=name, grid=(npairs, nq),
        in_specs=[qs, ks, pl.BlockSpec((None, nq, LANES, T), lambda p, i: (p, 0, 0, 0)), gs],
        out_specs=[pl.BlockSpec((T, LANES), lambda p, i: (i, p)), pl.BlockSpec((T, LANES), lambda p, i: (i, p)),
                   pl.BlockSpec((LANES, T), lambda p, i: (p, i)),
                   pl.BlockSpec((2, None, 1, T), lambda p, i: (p, i, 0, 0))],
        out_shape=[jax.ShapeDtypeStruct((S, W), F32), jax.ShapeDtypeStruct((S, W), BF16),
                   jax.ShapeDtypeStruct((W, S), BF16),
                   jax.ShapeDtypeStruct((2 * npairs, nq, 1, T), F32)],
        scratch_shapes=[pltpu.VMEM((2, 1, T), F32), pltpu.VMEM((2, LANES, T), F32)],
        compiler_params=_cparams(("parallel", "parallel")),
    )(proj, proj, vt, proj)


def _sb_bwd_t(proj, kt, do, tot, S, npairs, name):
    T = ATT_T
    nq = S // T
    nb = T // LANES
    scale = 64 ** -0.5

    def body(q_ref, k_ref, kt_ref, v_ref, do_ref, st_ref, dq_ref, dk_ref, dv_ref, dqt_ref, pre_ref, gpre_ref):
        i = pl.program_id(1)

        @pl.when(i == 0)
        def _():
            dk_ref[...] = jnp.zeros_like(dk_ref)
            dv_ref[...] = jnp.zeros_like(dv_ref)

        m0 = lax.broadcasted_iota(jnp.int32, (1, LANES), 1) < 64
        top = lax.broadcasted_iota(jnp.int32, (LANES, 1), 0) < 64
        key = lax.broadcasted_iota(jnp.int32, (T, LANES), 0)
        qrow = lax.broadcasted_iota(jnp.int32, (T, LANES), 1)
        r = lax.broadcasted_iota(jnp.int32, (T, T), 0)
        c = lax.broadcasted_iota(jnp.int32, (T, T), 1)
        upto = (c <= r).astype(BF16)
        left = (c < r).astype(BF16)
        qh = _head_q("sb", q_ref, m0, scale)
        dov = do_ref[...]
        doh = [jnp.where(m0, dov, 0.0).astype(BF16), jnp.where(m0, 0.0, dov).astype(BF16)]
        tot_h = [st_ref[0], st_ref[1]]
        dqt_ref[...] = jnp.zeros_like(dqt_ref)
        pre_ref[...] = jnp.zeros_like(pre_ref)
        gpre_ref[...] = jnp.zeros_like(gpre_ref)
        chains = [(h, b) for h in range(2) for b in range(nb)]

        def tiles(js, masked_at):
            starts = [pl.multiple_of(j * T, T) for j in js]
            zss, dwss = [], []
            for start in starts:
                vb = v_ref[pl.ds(start, T), :].astype(BF16)
                kb = k_ref[pl.ds(start, T), :].astype(BF16)
                zss.append(_split_blocks([_dot_nt(kb, qh[h]) for h in range(2)]))
                dwss.append(_split_blocks([_dot_nt(vb, doh[h]) for h in range(2)]))
            lass, sums, hiss, loss = [], [], [], []
            for zs, masked in zip(zss, _mask_flags(js, masked_at)):
                las, sm, his, los = [], [], [], []
                for (h, b), z in zip(chains, zs):
                    lk, la = _softplus_parts(z)
                    if masked:
                        lk = jnp.where(key < qrow + b * LANES, lk, 0.0)
                    hi, lo = _split2(lk)
                    las.append(la)
                    sm.append(jnp.sum(lk, axis=0, keepdims=True))
                    his.append(hi)
                    los.append(lo)
                lass.append(las)
                sums.append(sm)
                hiss.append(_join_blocks(his, nb))
                loss.append(_join_blocks(los, nb))
            pcss = [_split_blocks([_dot(upto, hi) + _dot(upto, lo) for hi, lo in zip(his, los)])
                    for his, los in zip(hiss, loss)]
            wss, gss = [], []
            for las, sm, pcs, dws, masked in zip(lass, sums, pcss, dwss, _mask_flags(js, masked_at)):
                ws, gs = [], []
                for (h, b), la, s, pc, dw in zip(chains, las, sm, pcs, dws):
                    lanes = slice(b * LANES, (b + 1) * LANES)
                    w = jnp.exp(la + ((tot_h[h][:, lanes] - pre_ref[h, :, lanes]) - pc))
                    if masked:
                        w = jnp.where(key < qrow + b * LANES, w, 0.0)
                    ws.append(w.astype(BF16))
                    gs.append(dw * w)
                    pre_ref[h, :, lanes] += s
                wss.append(_join_blocks(ws, nb))
                gss.append(gs)
            gcss = [_split_blocks([_dot(left, g) for g in _join_blocks([g.astype(BF16) for g in gs], nb)]) for gs in gss]
            dzss = []
            for las, gs, gcs, masked in zip(lass, gss, gcss, _mask_flags(js, masked_at)):
                dzs = []
                for (h, b), la, g, gc in zip(chains, las, gs, gcs):
                    lanes = slice(b * LANES, (b + 1) * LANES)
                    dz = g - (g + (gpre_ref[h, :, lanes] + gc)) * jnp.exp(la)
                    if masked:
                        dz = jnp.where(key < qrow + b * LANES, dz, 0.0)
                    dzs.append(dz.astype(BF16))
                    gpre_ref[h, :, lanes] += jnp.sum(g, axis=0, keepdims=True)
                dzss.append(_join_blocks(dzs, nb))
            for j, start, ws, dzs in zip(js, starts, wss, dzss):
                kt = kt_ref[j]
                dkc = dvc = None
                for h in range(2):
                    dkh = _dot(dzs[h], qh[h])
                    dvh = _dot(ws[h], doh[h])
                    dkc = dkh if dkc is None else dkc + dkh
                    dvc = dvh if dvc is None else dvc + dvh
                    dqt_ref[h] += _dot(kt, dzs[h])
                dk_ref[pl.ds(start, T), :] += dkc
                dv_ref[pl.ds(start, T), :] += dvc

        _loop_tiles(i, tiles, False)
        dq_ref[...] = jnp.where(top, dqt_ref[0], dqt_ref[1]).T * scale

    qs, ks, vs, _ = _att_specs("sb", S, T)
    W = npairs * LANES
    return pl.pallas_call(
        body, name=name, grid=(npairs, nq),
        in_specs=[qs, ks, pl.BlockSpec((None, nq, LANES, T), lambda p, i: (p, 0, 0, 0)), vs,
                  pl.BlockSpec((T, LANES), lambda p, i: (i, p)),
                  pl.BlockSpec((2, None, 1, T), lambda p, i: (p, i, 0, 0))],
        out_specs=[pl.BlockSpec((T, LANES), lambda p, i: (i, p)), pl.BlockSpec((S, LANES), lambda p, i: (0, p)),
                   pl.BlockSpec((S, LANES), lambda p, i: (0, p))],
        out_shape=[jax.ShapeDtypeStruct((S, W), F32)] * 3,
        scratch_shapes=[pltpu.VMEM((2, LANES, T), F32), pltpu.VMEM((2, 1, T), F32), pltpu.VMEM((2, 1, T), F32)],
        compiler_params=_cparams(("parallel", "arbitrary")),
    )(proj, proj, kt, proj, do, tot)


def _pad_w0(w):
    z = lambda n: jnp.zeros((w.shape[0], n), w.dtype)
    return jnp.concatenate([w[:, 2048:2432], w[:, 2432:2688], z(64), w[:, 2688:2720], z(32),
                            w[:, 1536:2048], w[:, 2720:3232], w[:, 0:512], w[:, 512:1024], w[:, 1024:1536]], axis=1)


def _unpad_w0(wp):
    return jnp.concatenate([wp[:, L0_SBQ:L0_SBQ + 512], wp[:, L0_SBK:L0_SBK + 512], wp[:, L0_SBV:L0_SBV + 512],
                            wp[:, L0_SBG:L0_SBG + 512], wp[:, 0:384], wp[:, 384:640], wp[:, 704:736],
                            wp[:, L0_MLG:L0_MLG + 512]], axis=1)


def _pad_wq(w):
    return jnp.pad(w.reshape(384, 8, 96), ((0, 0), (0, 0), (0, 32))).reshape(384, 1024)


def _unpad_wq(wp):
    return wp.reshape(384, 8, 128)[:, :, :96].reshape(384, 768)


def _pad_wkv(w):
    w3 = w.reshape(256, 8, 128)
    k = jnp.pad(w3[:, :, :64], ((0, 0), (0, 0), (0, 64))).reshape(256, 1024)
    return jnp.concatenate([k, w3[:, :, 64:].reshape(256, 512)], axis=1)


def _unpad_wkv(wp):
    k = wp[:, :1024].reshape(256, 8, 128)[:, :, :64]
    v = wp[:, 1024:].reshape(256, 8, 64)
    return jnp.concatenate([k, v], axis=-1).reshape(256, 1024)


def _pad_w1(w):
    return jnp.concatenate([w, jnp.zeros((w.shape[0], L1_WIDTH - ODD_IN_WIDTH), w.dtype)], axis=1)


def _local_step(x, positions, target, g, w0p, wqp, wkvp, wo0, w1p, wo1):
    S = x.shape[0]
    nq = S // ATT_T
    invf = ROPE_THETA ** (-jnp.arange(0, MLA_ROPE_DIM, 2, dtype=F32) / MLA_ROPE_DIM)
    invf = jnp.concatenate([jnp.zeros((64,), F32), invf, invf, jnp.zeros((32,), F32)]).reshape(1, LANES)
    cosT, s1T, s2T = _rope_tables(positions.reshape(S, 1), invf, "rope_tables")
    bfp = jnp.pad(g["l1_b_f"], ((0, 0), (0, LANES - FOX_HEADS)))

    proj0, h0t = _norm_matmul(x, g["l0_pre_g"], w0p, "l0_in_proj")
    qm, km, vm, qnt, cnt = _mla_prep(proj0, g["l0_q_a_g"], g["l0_kv_a_g"], wqp, wkvp, cosT, s1T, s2T, "mla_prep")
    sb_vt = _transpose_tiles(proj0, L0_SBV, 4, LANES, 2, "sb_vt")
    sb_kt = _transpose_tiles(proj0, L0_SBK, 4, LANES, 2, "sb_kt")
    o_sb, og_sb, ogt_sb, tot_sb = _sb_fwd_t(proj0, sb_vt, S, 4, "sb_fwd")
    vmt = _transpose_tiles(vm, 0, 4, LANES, 4, "mla_vt")
    kmt = _transpose_tiles(km, 0, 4, 2 * LANES, 4, "mla_kt")
    o_ml, og_ml, ogt_ml, lse_ml = _softmax_fwd("mla", (qm, km, vmt, proj0), None, S, 4, "mla_fwd")
    y0, x1 = _out_proj(og_sb, og_ml, 0, 0, wo0, x, g["l0_post_g"], None, "l0_out_proj")

    proj1, h1t = _norm_matmul(x1, g["l1_pre_g"], w1p, "l1_in_proj")
    cfx = _fox_prep(proj1, bfp, "fox_prep")
    c16 = cfx[:, :FOX_HEADS].T
    c_col = jnp.broadcast_to(c16[:, :, None], (FOX_HEADS, S, LANES))
    vt1 = _transpose_tiles(proj1, L1_V, 8, LANES, 8, "fox_vt")
    kt1 = _transpose_tiles(proj1, L1_K, 8, LANES, 8, "fox_kt")
    o_fx, og_fx, ogt_fx, lse_fx = _softmax_fwd("fox", (proj1, proj1, vt1, proj1), c_col, S, 8, "fox_fwd")
    y1, dx2, lsum = _out_proj(og_fx, og_fx, 0, 1, wo1, x1, g["l1_post_g"], target, "l1_out_proj")

    dy1, do1, dgate1, d_post1 = _out_proj_bwd(dx2, y1, g["l1_post_g"], wo1, proj1, (L1_G, L1_G + 512), o_fx, o_fx, 0, 1, "l1_out_bwd")
    dwo1 = _matmul_t(ogt_fx, dy1, "l1_dw_out")
    dq1, dk1, dv1, dck, dcq = _softmax_bwd_t("fox", proj1, proj1, kt1, proj1, do1, 0, o_fx, lse_fx, c_col, S, 8,
                                             "fox_bwd")
    dc = jnp.pad((dcq.reshape(FOX_HEADS, S) - dck.reshape(FOX_HEADS, S)).T, ((0, 0), (0, LANES - FOX_HEADS)))
    df, d_bf = _fox_prep_bwd(dc, proj1, bfp, "fox_prep_bwd")
    pieces1 = [(L1_Q, dq1), (L1_K, dk1), (L1_V, dv1), (L1_G, dgate1), (L1_F, df)]
    dx1, d_pre1 = _in_proj_bwd(pieces1, w1p, x1, g["l1_pre_g"], dx2, "l1_in_bwd")
    dw1p = jnp.concatenate([_matmul_t(h1t, a, "l1_dw_in_%d" % k) for k, (_, a) in enumerate(pieces1)], axis=1)

    dy0, do0, dgate0, d_post0 = _out_proj_bwd(dx1, y0, g["l0_post_g"], wo0, proj0, (L0_SBG, L0_MLG), o_sb, o_ml, 0, 0,
                                              "l0_out_bwd")
    dwo0 = jnp.concatenate([_matmul_t(ogt_sb, dy0, "l0_dw_out_sb"), _matmul_t(ogt_ml, dy0, "l0_dw_out_mla")], axis=0)
    dsq, dsk, dsv = _sb_bwd_t(proj0, sb_kt, do0, tot_sb, S, 4, "sb_bwd")
    dqm, dkm, dvm = _softmax_bwd_t("mla", qm, km, kmt, vm, do0, 4, o_ml, lse_ml, None, S, 4, "mla_bwd")
    dprep, dqb, dkvb, d_qag, d_kvag = _mla_prep_bwd(dqm, dkm, dvm, proj0, g["l0_q_a_g"], g["l0_kv_a_g"], wqp, wkvp,
                                                    cosT, s1T, s2T, "mla_prep_bwd")
    dwqp = _matmul_t(qnt, dqb, "l0_dw_qb")
    dwkvp = _matmul_t(cnt, dkvb, "l0_dw_kvb")
    pieces0 = [(L0_PREP, dprep), (L0_SBG, dgate0), (L0_SBQ, dsq), (L0_SBK, dsk), (L0_SBV, dsv)]
    dx0, d_pre0 = _in_proj_bwd(pieces0, w0p, x, g["l0_pre_g"], dx1, "l0_in_bwd")
    dw0p = jnp.concatenate([_matmul_t(h0t, a, "l0_dw_in_%d" % k) for k, (_, a) in enumerate(pieces0)], axis=1)

    grads = {
        "l0_pre_g": d_pre0, "l0_post_g": d_post0, "l0_w_in": dw0p, "l0_q_a_g": d_qag, "l0_w_q_b": dwqp,
        "l0_kv_a_g": d_kvag, "l0_w_kv_b": dwkvp, "l0_w_out": dwo0, "l1_pre_g": d_pre1, "l1_post_g": d_post1,
        "l1_w_in": dw1p, "l1_b_f": d_bf[:, :FOX_HEADS], "l1_w_out": dwo1,
    }
    return lsum, dx0, grads


_ANY = pl.BlockSpec(memory_space=pl.ANY)


def _place():
    return lax.axis_index("x"), lax.axis_index("y"), lax.axis_index("c")


def _other_chips(x, y):
    return [(1 - x, y), (x, 1 - y), (1 - x, 1 - y)]


def _half(c):
    return pl.ds(c * PACK_HALF, PACK_HALF)


def _weight_gather(pack):
    def body(p_ref, out_ref, send_sems, recv_sems):
        x, y, c = _place()
        sibling = (x, y, 1 - c)
        chips = _other_chips(x, y)

        def blk(chip, cc):
            return out_ref.at[2 * chip[0] + chip[1], _half(cc)]

        def copy(k, src, dst, to):
            return pltpu.make_async_remote_copy(src_ref=src, dst_ref=dst, send_sem=send_sems.at[k],
                                                recv_sem=recv_sems.at[k], device_id=to, device_id_type=MESH)

        first = [copy(j, p_ref.at[_half(c)], blk((x, y), c), (*chip, c)) for j, chip in enumerate(chips)]
        for cp in first:
            cp.start()
        passed = [copy(3 + j, blk(chip, c), blk(chip, c), sibling) for j, chip in enumerate(chips)]
        for j, chip in enumerate(chips):
            copy(j, blk(chip, c), blk(chip, c), (x, y, c)).wait_recv()
            passed[j].start()
        for j, chip in enumerate(chips):
            copy(3 + j, blk(chip, 1 - c), blk(chip, 1 - c), (x, y, c)).wait_recv()
        for cp in first + passed:
            cp.wait_send()

    return pl.pallas_call(
        body, name="weight_gather", in_specs=[_ANY], out_specs=_ANY,
        out_shape=jax.ShapeDtypeStruct((4,) + pack.shape, pack.dtype),
        scratch_shapes=[pltpu.SemaphoreType.DMA((6,)), pltpu.SemaphoreType.DMA((6,))],
    )(pack)


GRAD_TR = 2048


def _grad_core_exchange(p):
    def body(p_ref, recv_ref, send_sems, recv_sems):
        x, y, c = _place()
        give = [pltpu.make_async_remote_copy(src_ref=p_ref.at[j, _half(1 - c)], dst_ref=recv_ref.at[j],
                                             send_sem=send_sems.at[j], recv_sem=recv_sems.at[j],
                                             device_id=(x, y, 1 - c), device_id_type=MESH) for j in range(4)]
        for cp in give:
            cp.start()
        for cp in give:
            cp.wait()

    return pl.pallas_call(
        body, name="grad_core_exchange", in_specs=[_ANY], out_specs=_ANY,
        out_shape=jax.ShapeDtypeStruct((4, PACK_HALF, LANES), p.dtype),
        scratch_shapes=[pltpu.SemaphoreType.DMA((4,)), pltpu.SemaphoreType.DMA((4,))],
    )(p)


def _grad_add_cores(p, theirs, c1):
    tr = GRAD_TR

    def body(c_ref, a_ref, b_ref, o_ref):
        o_ref[...] = a_ref[...] + b_ref[...]

    spec = pl.BlockSpec((None, tr, LANES), lambda j, r, c: (j, r, 0))
    grid_spec = pltpu.PrefetchScalarGridSpec(
        num_scalar_prefetch=1, grid=(4, PACK_HALF // tr),
        in_specs=[pl.BlockSpec((None, None, tr, LANES), lambda j, r, c: (j, c[0], r, 0)), spec], out_specs=spec)
    return pl.pallas_call(
        body, name="grad_add_cores", grid_spec=grid_spec, out_shape=jax.ShapeDtypeStruct(theirs.shape, theirs.dtype),
        compiler_params=_cparams(("parallel", "parallel")),
    )(c1, p.reshape(4, 2, PACK_HALF, LANES), theirs)


def _grad_chip_exchange(q):
    def body(q_ref, out_ref, send_sems, recv_sems):
        x, y, c = _place()
        me = 2 * x + y
        chips = _other_chips(x, y)
        sends = [pltpu.make_async_remote_copy(src_ref=q_ref.at[2 * chip[0] + chip[1]], dst_ref=out_ref.at[me],
                                              send_sem=send_sems.at[j], recv_sem=recv_sems.at[j],
                                              device_id=(*chip, c), device_id_type=MESH) for j, chip in enumerate(chips)]
        for cp in sends:
            cp.start()
        for j, chip in enumerate(chips):
            slot = out_ref.at[2 * chip[0] + chip[1]]
            pltpu.make_async_remote_copy(src_ref=slot, dst_ref=slot, send_sem=send_sems.at[j], recv_sem=recv_sems.at[j],
                                         device_id=(x, y, c), device_id_type=MESH).wait_recv()
        for cp in sends:
            cp.wait_send()

    return pl.pallas_call(
        body, name="grad_chip_exchange", in_specs=[_ANY], out_specs=_ANY,
        out_shape=jax.ShapeDtypeStruct(q.shape, q.dtype),
        scratch_shapes=[pltpu.SemaphoreType.DMA((3,)), pltpu.SemaphoreType.DMA((3,))],
    )(q)


def _grad_add_chips(q, slots, me1):
    tr = GRAD_TR

    def body(me_ref, own_ref, s0, s1, s2, s3, o_ref):
        me = me_ref[0]
        t = [jnp.where(me == j, own_ref[...], s[...]) for j, s in enumerate((s0, s1, s2, s3))]
        o_ref[...] = ((t[0] + t[1]) + t[2]) + t[3]

    def slot_spec(j):
        return pl.BlockSpec((None, tr, LANES), lambda r, me: (jnp.where(me[0] == j, (j + 1) % 4, j), r, 0))

    grid_spec = pltpu.PrefetchScalarGridSpec(
        num_scalar_prefetch=1, grid=(PACK_HALF // tr,),
        in_specs=[pl.BlockSpec((None, tr, LANES), lambda r, me: (me[0], r, 0))] + [slot_spec(j) for j in range(4)],
        out_specs=pl.BlockSpec((tr, LANES), lambda r, me: (r, 0)))
    return pl.pallas_call(
        body, name="grad_add_chips", grid_spec=grid_spec, out_shape=jax.ShapeDtypeStruct(q.shape[1:], q.dtype),
        compiler_params=_cparams(("parallel",)),
    )(me1, q, slots, slots, slots, slots)


def _grad_core_gather(t):
    def body(t_ref, out_ref, send_sem, recv_sem):
        x, y, c = _place()
        give = pltpu.make_async_remote_copy(src_ref=t_ref, dst_ref=out_ref, send_sem=send_sem, recv_sem=recv_sem,
                                            device_id=(x, y, 1 - c), device_id_type=MESH)
        give.start()
        give.wait()

    return pl.pallas_call(
        body, name="grad_core_gather", in_specs=[_ANY], out_specs=_ANY,
        out_shape=jax.ShapeDtypeStruct(t.shape, t.dtype),
        scratch_shapes=[pltpu.SemaphoreType.DMA, pltpu.SemaphoreType.DMA],
    )(t)


def _small_allreduce(sp):
    def body(sp_ref, out_ref, gath_ref, send_sems, recv_sems):
        x, y, c = _place()
        me = 4 * x + 2 * y + c
        gath_ref[me] = sp_ref[...]
        peers = []
        for k in range(1, 8):
            px = 1 - x if k & 4 else x
            py = 1 - y if k & 2 else y
            pc = 1 - c if k & 1 else c
            peers.append((px, py, pc))
        sends = [pltpu.make_async_remote_copy(src_ref=sp_ref, dst_ref=gath_ref.at[me], send_sem=send_sems.at[k],
                                              recv_sem=recv_sems.at[k], device_id=peer, device_id_type=MESH)
                 for k, peer in enumerate(peers)]
        for cp in sends:
            cp.start()
        for k, (px, py, pc) in enumerate(peers):
            slot = gath_ref.at[4 * px + 2 * py + pc]
            pltpu.make_async_remote_copy(src_ref=slot, dst_ref=slot, send_sem=send_sems.at[k], recv_sem=recv_sems.at[k],
                                         device_id=(x, y, c), device_id_type=MESH).wait_recv()
        for cp in sends:
            cp.wait_send()
        tot = gath_ref[0]
        for d in range(1, 8):
            tot = tot + gath_ref[d]
        out_ref[...] = tot

    vm = pl.BlockSpec(memory_space=pltpu.VMEM)
    return pl.pallas_call(
        body, name="small_allreduce", in_specs=[vm], out_specs=vm, out_shape=jax.ShapeDtypeStruct(sp.shape, sp.dtype),
        scratch_shapes=[pltpu.VMEM((8,) + sp.shape, sp.dtype), pltpu.SemaphoreType.DMA((7,)), pltpu.SemaphoreType.DMA((7,))],
    )(sp)


def _adamw_update(w, gv, m, v):
    mn = ADAM_B1 * m + (1.0 - ADAM_B1) * gv
    vn = ADAM_B2 * v + (1.0 - ADAM_B2) * (gv * gv)
    m_hat = mn / (1.0 - ADAM_B1 ** ADAM_STEP)
    v_hat = vn / (1.0 - ADAM_B2 ** ADAM_STEP)
    return -ADAM_LR * (m_hat / (jnp.sqrt(v_hat) + ADAM_EPS) + ADAM_WD * w), mn, vn


def _adamw(w, g, m, v, name):
    rows = w.shape[0]

    def body(w_ref, g_ref, m_ref, v_ref, d_ref, mo_ref, vo_ref):
        d_ref[...], mo_ref[...], vo_ref[...] = _adamw_update(w_ref[...], g_ref[...], m_ref[...], v_ref[...])

    spec = pl.BlockSpec((rows, LANES), lambda r: (0, 0))
    shp = jax.ShapeDtypeStruct(w.shape, F32)
    return pl.pallas_call(
        body, name=name, grid=(1,), in_specs=[spec] * 4, out_specs=[spec] * 3, out_shape=[shp] * 3,
        compiler_params=_cparams(("arbitrary",)),
    )(w, g, m, v)


def _adamw_mats(w, g_mine, g_theirs, m, v, c1):
    tr = GRAD_TR
    nb = PACK_HALF // tr

    def body(c_ref, w_ref, a_ref, b_ref, m_ref, v_ref, g_ref, d_ref, mo_ref, vo_ref):
        gv = jnp.where(pl.program_id(0) == c_ref[0], a_ref[...], b_ref[...])
        g_ref[...] = gv
        d_ref[...], mo_ref[...], vo_ref[...] = _adamw_update(w_ref[...], gv, m_ref[...], v_ref[...])

    full = pl.BlockSpec((tr, LANES), lambda h, r, c: (h * nb + r, 0))
    half = pl.BlockSpec((tr, LANES), lambda h, r, c: (r, 0))
    grid_spec = pltpu.PrefetchScalarGridSpec(num_scalar_prefetch=1, grid=(2, nb),
                                             in_specs=[full, half, half, full, full], out_specs=[full] * 4)
    shp = jax.ShapeDtypeStruct(w.shape, F32)
    return pl.pallas_call(
        body, name="adamw_mats", grid_spec=grid_spec, out_shape=[shp] * 4,
        compiler_params=_cparams(("parallel", "parallel")),
    )(c1, w, g_mine, g_theirs, m, v)


MAT_NAMES = ("l0_w_in", "l0_w_q_b", "l0_w_kv_b", "l0_w_out", "l1_w_in", "l1_w_out")
VEC_NAMES = ("l0_pre_g", "l0_post_g", "l0_q_a_g", "l0_kv_a_g", "l1_pre_g", "l1_post_g", "l1_b_f")
WEIGHT_NAMES = ("l0_pre_g", "l0_post_g", "l0_w_in", "l0_q_a_g", "l0_w_q_b", "l0_kv_a_g", "l0_w_kv_b", "l0_w_out",
                "l1_pre_g", "l1_post_g", "l1_w_in", "l1_b_f", "l1_w_out")
MAT_SHARD = {"l0_w_in": (1024, 808), "l0_w_q_b": (384, 192), "l0_w_kv_b": (256, 256), "l0_w_out": (256, 1024),
             "l1_w_in": (1024, 1028), "l1_w_out": (256, 1024)}
ROW_SHARDED = ("l0_w_out", "l1_w_out")
VEC_LEN = {"l0_pre_g": 1024, "l0_post_g": 1024, "l0_q_a_g": 384, "l0_kv_a_g": 256, "l1_pre_g": 1024,
           "l1_post_g": 1024, "l1_b_f": 16}


def _mat_rows(n):
    r, c = MAT_SHARD[n]
    return r * c // LANES


def _pack_shards(shards):
    parts = [shards[n].reshape(_mat_rows(n), LANES) for n in MAT_NAMES]
    used = sum(_mat_rows(n) for n in MAT_NAMES)
    parts.append(jnp.zeros((PACK_ROWS - used, LANES), parts[0].dtype))
    return jnp.concatenate(parts, axis=0)


def _unpack_shards(pack):
    out, at = {}, 0
    for n in MAT_NAMES:
        out[n] = pack[..., at:at + _mat_rows(n), :].reshape(pack.shape[:-2] + MAT_SHARD[n])
        at += _mat_rows(n)
    return out


def _join_shards(n, s):
    if n in ROW_SHARDED:
        return s.reshape(4 * s.shape[1], s.shape[2])
    return s.transpose(1, 0, 2).reshape(s.shape[1], 4 * s.shape[2])


def _cut_shards(n, w):
    r, c = MAT_SHARD[n]
    if n in ROW_SHARDED:
        return w.reshape(4, r, c)
    return w.reshape(r, 4, c).transpose(1, 0, 2)


def _pack_vecs(vecs):
    parts = []
    for n in VEC_NAMES:
        v = vecs[n].reshape(-1)
        parts.append(jnp.pad(v, (0, VEC_ROWS * LANES - v.shape[0])).reshape(VEC_ROWS, LANES))
    return jnp.concatenate(parts, axis=0)


def _unpack_vecs(pack):
    return {n: pack[k * VEC_ROWS:(k + 1) * VEC_ROWS].reshape(-1)[:VEC_LEN[n]] for k, n in enumerate(VEC_NAMES)}


def kernel(x, positions, l0_pre_g, l0_post_g, l0_w_in, l0_q_a_g, l0_w_q_b, l0_kv_a_g, l0_w_kv_b, l0_w_out, l1_pre_g, l1_post_g, l1_w_in, l1_b_f, l1_w_out, loss_target, m_l0_pre_g, m_l0_post_g, m_l0_w_in, m_l0_q_a_g, m_l0_w_q_b, m_l0_kv_a_g, m_l0_w_kv_b, m_l0_w_out, m_l1_pre_g, m_l1_post_g, m_l1_w_in, m_l1_b_f, m_l1_w_out, v_l0_pre_g, v_l0_post_g, v_l0_w_in, v_l0_q_a_g, v_l0_w_q_b, v_l0_kv_a_g, v_l0_w_kv_b, v_l0_w_out, v_l1_pre_g, v_l1_post_g, v_l1_w_in, v_l1_b_f, v_l1_w_out):
    w = dict(l0_pre_g=l0_pre_g, l0_post_g=l0_post_g, l0_w_in=l0_w_in, l0_q_a_g=l0_q_a_g, l0_w_q_b=l0_w_q_b,
             l0_kv_a_g=l0_kv_a_g, l0_w_kv_b=l0_w_kv_b, l0_w_out=l0_w_out, l1_pre_g=l1_pre_g, l1_post_g=l1_post_g,
             l1_w_in=l1_w_in, l1_b_f=l1_b_f, l1_w_out=l1_w_out)
    m = dict(l0_pre_g=m_l0_pre_g, l0_post_g=m_l0_post_g, l0_w_in=m_l0_w_in, l0_q_a_g=m_l0_q_a_g, l0_w_q_b=m_l0_w_q_b,
             l0_kv_a_g=m_l0_kv_a_g, l0_w_kv_b=m_l0_w_kv_b, l0_w_out=m_l0_w_out, l1_pre_g=m_l1_pre_g,
             l1_post_g=m_l1_post_g, l1_w_in=m_l1_w_in, l1_b_f=m_l1_b_f, l1_w_out=m_l1_w_out)
    v = dict(l0_pre_g=v_l0_pre_g, l0_post_g=v_l0_post_g, l0_w_in=v_l0_w_in, l0_q_a_g=v_l0_q_a_g, l0_w_q_b=v_l0_w_q_b,
             l0_kv_a_g=v_l0_kv_a_g, l0_w_kv_b=v_l0_w_kv_b, l0_w_out=v_l0_w_out, l1_pre_g=v_l1_pre_g,
             l1_post_g=v_l1_post_g, l1_w_in=v_l1_w_in, l1_b_f=v_l1_b_f, l1_w_out=v_l1_w_out)

    cx, cy, cc = _place()
    me1 = jnp.reshape(2 * cx + cy, (1,)).astype(jnp.int32)
    c1 = jnp.reshape(cc, (1,)).astype(jnp.int32)
    w_pack = _pack_shards(w)
    w_bf = w_pack.astype(BF16)
    gathered = lax.dynamic_update_slice(_weight_gather(w_bf), w_bf[None], (2 * cx + cy, 0, 0))
    gathered = _unpack_shards(gathered)
    full = {n: _join_shards(n, gathered[n]) for n in MAT_NAMES}
    gains = {n: w[n].reshape(1, -1) for n in VEC_NAMES}

    lsum, dx0, grads = _local_step(
        x[0], positions[0], loss_target[0], gains, _pad_w0(full["l0_w_in"]), _pad_wq(full["l0_w_q_b"]),
        _pad_wkv(full["l0_w_kv_b"]), full["l0_w_out"], _pad_w1(full["l1_w_in"]), full["l1_w_out"])

    gfull = {"l0_w_in": _unpad_w0(grads["l0_w_in"]), "l0_w_q_b": _unpad_wq(grads["l0_w_q_b"]),
             "l0_w_kv_b": _unpad_wkv(grads["l0_w_kv_b"]), "l0_w_out": grads["l0_w_out"],
             "l1_w_in": grads["l1_w_in"][:, :ODD_IN_WIDTH], "l1_w_out": grads["l1_w_out"]}
    parts = [_cut_shards(n, gfull[n]).reshape(4, _mat_rows(n), LANES) for n in MAT_NAMES]
    used = sum(_mat_rows(n) for n in MAT_NAMES)
    parts.append(jnp.zeros((4, PACK_ROWS - used, LANES), F32))
    g_pack = jnp.concatenate(parts, axis=1)
    q_cores = _grad_add_cores(g_pack, _grad_core_exchange(g_pack), c1)
    g_mine = _grad_add_chips(q_cores, _grad_chip_exchange(q_cores), me1)
    g_theirs = _grad_core_gather(g_mine)

    small = _small_allreduce(jnp.concatenate([_pack_vecs({n: grads[n] for n in VEC_NAMES}),
                                              lsum.reshape(D_MODEL // LANES, LANES)], axis=0))
    g_small = small[:SMALL_ROWS]
    loss = 0.5 * jnp.sum(small[SMALL_ROWS:]) / float(D_MODEL)

    g_shard, d_pack, m_pack, v_pack = _adamw_mats(w_pack, g_mine, g_theirs, _pack_shards(m), _pack_shards(v), c1)
    d_small, m_small, v_small = _adamw(_pack_vecs(w), g_small, _pack_vecs(m), _pack_vecs(v), "adamw_vecs")

    def unpack(mat_pack, vec_pack):
        out = dict(_unpack_shards(mat_pack))
        out.update(_unpack_vecs(vec_pack))
        return [out[n] for n in WEIGHT_NAMES]

    return (loss, dx0[None], *unpack(g_shard, g_small), *unpack(d_pack, d_small), *unpack(m_pack, m_small),
            *unpack(v_pack, v_small))
```

```python
import functools

import numpy as np
import jax
import jax.numpy as jnp
from jax import lax
from jax.experimental import pallas as pl
from jax.experimental.pallas import tpu as pltpu

F32 = jnp.float32
BF16 = jnp.bfloat16
MESH = pl.DeviceIdType.MESH

D_MODEL = 1024
RMS_EPS = 1e-6
ROPE_THETA = 10000.0
SB_WIDTH = 512
MLA_Q_LORA = 384
MLA_KV_LORA = 256
MLA_ROPE_DIM = 32
MLA_WIDTH = 512
FOX_WIDTH = 1024
FOX_HEADS = 16
EVEN_IN_WIDTH = 3232
ODD_IN_WIDTH = 4112

ADAM_LR = 0.001
ADAM_B1 = 0.9
ADAM_B2 = 0.999
ADAM_EPS = 1e-08
ADAM_WD = 0.01
ADAM_STEP = 10

LANES = 128
VMEM_LIMIT = 56 * 1024 * 1024

L0_PREP = 0
L0_PREP_W = 768
L0_SBG = 768
L0_MLG = 1280
L0_SBQ = 1792
L0_SBK = 2304
L0_SBV = 2816
L0_WIDTH = 3328
L1_Q = 0
L1_K = 1024
L1_V = 2048
L1_G = 3072
L1_F = 4096
L1_WIDTH = 4224

ATT_T = 256
ATT_GROUP = 4
NEG = -1e30

PACK_ROWS = 20480
PACK_HALF = PACK_ROWS // 2
VEC_ROWS = 8
SMALL_ROWS = 7 * VEC_ROWS


def _cparams(sem, **kw):
    return pltpu.CompilerParams(dimension_semantics=sem, vmem_limit_bytes=VMEM_LIMIT, **kw)


def _dot(a, b):
    return lax.dot_general(a, b, (((1,), (0,)), ((), ())), preferred_element_type=F32)


def _dot_nt(a, b):
    return lax.dot_general(a, b, (((1,), (1,)), ((), ())), preferred_element_type=F32)


def _dot_tn(a, b):
    return lax.dot_general(a, b, (((0,), (0,)), ((), ())), preferred_element_type=F32)


def _sigmoid(x):
    return 1.0 / (1.0 + jnp.exp(-x))


def _rstd(x):
    return lax.rsqrt(jnp.mean(x * x, axis=-1, keepdims=True) + RMS_EPS)


def _norm_bwd(x, g, dy):
    r = _rstd(x)
    xn = x * r
    dxn = dy * g
    dx = r * (dxn - xn * jnp.mean(dxn * xn, axis=-1, keepdims=True))
    return dx, dy * xn


def _split3(x):
    hi = x.astype(BF16)
    r1 = x - hi.astype(F32)
    mid = r1.astype(BF16)
    lo = (r1 - mid.astype(F32)).astype(BF16)
    return hi, mid, lo


def _wide_tile(n, cap=1792):
    return max(t for t in range(LANES, min(n, cap) + 1, LANES) if n % t == 0)


def _pick(n, cands):
    for c in cands:
        if n % c == 0:
            return c
    raise ValueError(n)


def _norm_matmul(x, g, w, name):
    S, K = x.shape
    N = w.shape[1]
    tm = _pick(S, (512, 256))
    tn = _wide_tile(N)

    def body(x_ref, g_ref, w_ref, o_ref, ht_ref, h_ref):
        @pl.when(pl.program_id(1) == 0)
        def _():
            xv = x_ref[...]
            h = (xv * _rstd(xv)) * g_ref[...]
            h_ref[...] = h.astype(BF16)
            ht_ref[...] = h.T.astype(BF16)
        o_ref[...] = _dot(h_ref[...], w_ref[...])

    return pl.pallas_call(
        body, name=name, grid=(S // tm, N // tn),
        in_specs=[pl.BlockSpec((tm, K), lambda i, j: (i, 0)),
                  pl.BlockSpec((1, K), lambda i, j: (0, 0)),
                  pl.BlockSpec((K, tn), lambda i, j: (0, j))],
        out_specs=[pl.BlockSpec((tm, tn), lambda i, j: (i, j)),
                   pl.BlockSpec((K, tm), lambda i, j: (0, i))],
        out_shape=[jax.ShapeDtypeStruct((S, N), F32), jax.ShapeDtypeStruct((K, S), BF16)],
        scratch_shapes=[pltpu.VMEM((tm, K), BF16)],
        compiler_params=_cparams(("parallel", "arbitrary")),
    )(x, g, w)


def _matmul_t(at, b, name):
    M, S = at.shape
    N = b.shape[1]
    tn = _wide_tile(N)
    ts = _pick(S, (512, 256))

    def body(a_ref, b_ref, o_ref):
        @pl.when(pl.program_id(1) == 0)
        def _():
            o_ref[...] = jnp.zeros_like(o_ref)
        o_ref[...] += _dot(a_ref[...], b_ref[...].astype(BF16))

    return pl.pallas_call(
        body, name=name, grid=(N // tn, S // ts),
        in_specs=[pl.BlockSpec((M, ts), lambda j, k: (0, k)),
                  pl.BlockSpec((ts, tn), lambda j, k: (k, j))],
        out_specs=pl.BlockSpec((M, tn), lambda j, k: (0, j)),
        out_shape=jax.ShapeDtypeStruct((M, N), F32),
        compiler_params=_cparams(("parallel", "arbitrary")),
    )(at, b)


def _in_proj_bwd(pieces, w, x, g, dx_up, name):
    S, K = x.shape
    N = w.shape[1]
    tm = _pick(S, (256,))
    offs = [off for off, _ in pieces]
    arrs = [a for _, a in pieces]

    def body(*refs):
        d_refs = refs[:len(arrs)]
        w_ref, x_ref, g_ref, u_ref, dx_ref, dg_ref = refs[len(arrs):]

        @pl.when(pl.program_id(0) == 0)
        def _():
            dg_ref[...] = jnp.zeros_like(dg_ref)

        acc = None
        for off, d_ref in zip(offs, d_refs):
            part = _dot_nt(d_ref[...].astype(BF16), w_ref[:, off:off + d_ref.shape[1]])
            acc = part if acc is None else acc + part
        dx, dgrow = _norm_bwd(x_ref[...], g_ref[...], acc)
        dx_ref[...] = u_ref[...] + dx
        dg_ref[...] += jnp.sum(dgrow, axis=0, keepdims=True)

    row = lambda i: (i, 0)
    fixed = lambda i: (0, 0)
    return pl.pallas_call(
        body, name=name, grid=(S // tm,),
        in_specs=[pl.BlockSpec((tm, a.shape[1]), row) for a in arrs] + [
            pl.BlockSpec((K, N), fixed), pl.BlockSpec((tm, K), row), pl.BlockSpec((1, K), fixed),
            pl.BlockSpec((tm, K), row)],
        out_specs=[pl.BlockSpec((tm, K), row), pl.BlockSpec((1, K), fixed)],
        out_shape=[jax.ShapeDtypeStruct((S, K), F32), jax.ShapeDtypeStruct((1, K), F32)],
        compiler_params=_cparams(("arbitrary",)),
    )(*arrs, w, x, g, dx_up)


def _out_proj(og_a, og_b, blk_a, blk_b, w, x, g, target, name):
    S = x.shape[0]
    D = x.shape[1]
    tm = _pick(S, (512, 256))
    with_loss = target is not None

    def body(*refs):
        if with_loss:
            a_ref, b_ref, wa_ref, wb_ref, x_ref, g_ref, t_ref, y_ref, o_ref, l_ref = refs
        else:
            a_ref, b_ref, wa_ref, wb_ref, x_ref, g_ref, y_ref, o_ref = refs
        y = _dot(a_ref[...], wa_ref[...]) + _dot(b_ref[...], wb_ref[...])
        y_ref[...] = y
        xn = x_ref[...] + (y * _rstd(y)) * g_ref[...]
        if with_loss:
            @pl.when(pl.program_id(0) == 0)
            def _():
                l_ref[...] = jnp.zeros_like(l_ref)
            d = xn - t_ref[...]
            o_ref[...] = d / float(D)
            l_ref[...] += jnp.sum(d * d, axis=0, keepdims=True)
        else:
            o_ref[...] = xn

    row = lambda i: (i, 0)
    in_specs = [pl.BlockSpec((tm, 512), lambda i: (i, blk_a)),
                pl.BlockSpec((tm, 512), lambda i: (i, blk_b)),
                pl.BlockSpec((512, D), lambda i: (0, 0)),
                pl.BlockSpec((512, D), lambda i: (1, 0)),
                pl.BlockSpec((tm, D), row),
                pl.BlockSpec((1, D), lambda i: (0, 0))]
    out_specs = [pl.BlockSpec((tm, D), row), pl.BlockSpec((tm, D), row)]
    out_shape = [jax.ShapeDtypeStruct((S, D), F32), jax.ShapeDtypeStruct((S, D), F32)]
    args = [og_a, og_b, w, w, x, g]
    if with_loss:
        in_specs.append(pl.BlockSpec((tm, D), row))
        out_specs.append(pl.BlockSpec((1, D), lambda i: (0, 0)))
        out_shape.append(jax.ShapeDtypeStruct((1, D), F32))
        args.append(target)
    return pl.pallas_call(
        body, name=name, grid=(S // tm,), in_specs=in_specs, out_specs=out_specs, out_shape=out_shape,
        compiler_params=_cparams(("arbitrary",)),
    )(*args)


def _out_proj_bwd(dx_up, y, g, w, proj, gate_offs, o_a, o_b, oblk_a, oblk_b, name):
    S, D = y.shape
    tm = _pick(S, (256,))
    gblk = [off // 256 + c for off in gate_offs for c in range(2)]

    def body(u_ref, y_ref, g_ref, w_ref, g0, g1, g2, g3, oa_ref, ob_ref, dy_ref, do_ref, dgate_ref, dg_ref):
        @pl.when(pl.program_id(0) == 0)
        def _():
            dg_ref[...] = jnp.zeros_like(dg_ref)
        dy, dgrow = _norm_bwd(y_ref[...], g_ref[...], u_ref[...])
        dg_ref[...] += jnp.sum(dgrow, axis=0, keepdims=True)
        dyb = dy.astype(BF16)
        dy_ref[...] = dyb
        dog = _dot_nt(dyb, w_ref[...])
        gates = (g0, g1, g2, g3)
        for c in range(4):
            gt = gates[c][...]
            sg = _sigmoid(gt)
            o_ref = oa_ref if c < 2 else ob_ref
            ov = o_ref[:, (c % 2) * 256:(c % 2 + 1) * 256]
            dc = dog[:, c * 256:(c + 1) * 256]
            do_ref[:, c * 256:(c + 1) * 256] = dc * (gt * sg)
            dgate_ref[:, c * 256:(c + 1) * 256] = dc * ov * (sg * (1.0 + gt * (1.0 - sg)))

    row = lambda i: (i, 0)
    gspec = lambda c: pl.BlockSpec((tm, 256), lambda i: (i, gblk[c]))
    return pl.pallas_call(
        body, name=name, grid=(S // tm,),
        in_specs=[pl.BlockSpec((tm, D), row), pl.BlockSpec((tm, D), row), pl.BlockSpec((1, D), lambda i: (0, 0)),
                  pl.BlockSpec((D, D), lambda i: (0, 0)),
                  gspec(0), gspec(1), gspec(2), gspec(3),
                  pl.BlockSpec((tm, 512), lambda i: (i, oblk_a)),
                  pl.BlockSpec((tm, 512), lambda i: (i, oblk_b))],
        out_specs=[pl.BlockSpec((tm, D), row), pl.BlockSpec((tm, D), row), pl.BlockSpec((tm, D), row),
                   pl.BlockSpec((1, D), lambda i: (0, 0))],
        out_shape=[jax.ShapeDtypeStruct((S, D), BF16), jax.ShapeDtypeStruct((S, D), F32),
                   jax.ShapeDtypeStruct((S, D), F32), jax.ShapeDtypeStruct((1, D), F32)],
        compiler_params=_cparams(("arbitrary",)),
    )(dx_up, y, g, w, proj, proj, proj, proj, o_a, o_b)


def _rope_tables(pos, invf, name):
    S = pos.shape[0]
    tm = _pick(S, (512, 256))

    def body(p_ref, f_ref, c_ref, s1_ref, s2_ref):
        lane = lax.broadcasted_iota(jnp.int32, (1, LANES), 1)
        ang = p_ref[...].astype(F32) * f_ref[...]
        c, s = jnp.cos(ang), jnp.sin(ang)
        c_ref[...] = jnp.where((lane >= 64) & (lane < 96), c, 1.0)
        s1_ref[...] = jnp.where((lane >= 64) & (lane < 80), -s, 0.0)
        s2_ref[...] = jnp.where((lane >= 80) & (lane < 96), s, 0.0)

    spec = pl.BlockSpec((tm, LANES), lambda i: (i, 0))
    return pl.pallas_call(
        body, name=name, grid=(S // tm,),
        in_specs=[pl.BlockSpec((tm, 1), lambda i: (i, 0)), pl.BlockSpec((1, LANES), lambda i: (0, 0))],
        out_specs=[spec, spec, spec],
        out_shape=[jax.ShapeDtypeStruct((S, LANES), F32)] * 3,
        compiler_params=_cparams(("parallel",)),
    )(pos, invf)


def _rope(x, c, s1, s2):
    return x * c + pltpu.roll(x, LANES - 16, 1) * s1 + pltpu.roll(x, 16, 1) * s2


def _rope_t(d, c, s1, s2):
    return d * c + pltpu.roll(d * s1, 16, 1) + pltpu.roll(d * s2, LANES - 16, 1)


def _mla_prep(proj, gq, gkv, wq, wkv, cosT, s1T, s2T, name):
    S = proj.shape[0]
    tm = _pick(S, (256,))

    def body(p_ref, gq_ref, gkv_ref, wq_ref, wkv_ref, c_ref, s1_ref, s2_ref, q_ref, k_ref, v_ref, qn_ref, cn_ref):
        qa = p_ref[:, 0:384]
        ckv = p_ref[:, 384:640]
        kr = p_ref[:, 640:768]
        qn32 = (qa * _rstd(qa)) * gq_ref[...]
        cn32 = (ckv * _rstd(ckv)) * gkv_ref[...]
        qn = qn32.astype(BF16)
        cn = cn32.astype(BF16)
        qn_ref[...] = qn32.T.astype(BF16)
        cn_ref[...] = cn32.T.astype(BF16)
        qb = _dot(qn, wq_ref[...])
        kvb = _dot(cn, wkv_ref[...])
        c, s1, s2 = c_ref[...], s1_ref[...], s2_ref[...]
        krr = _rope(kr, c, s1, s2)
        for h in range(8):
            sl = slice(h * LANES, (h + 1) * LANES)
            q_ref[:, sl] = _rope(qb[:, sl], c, s1, s2)
            k_ref[:, sl] = kvb[:, sl] + krr
        v_ref[...] = kvb[:, 1024:1536]

    row = lambda i: (i, 0)
    fixed = lambda i: (0, 0)
    tspec = pl.BlockSpec((tm, LANES), row)
    return pl.pallas_call(
        body, name=name, grid=(S // tm,),
        in_specs=[pl.BlockSpec((tm, L0_PREP_W), lambda i: (i, L0_PREP // L0_PREP_W)),
                  pl.BlockSpec((1, 384), fixed), pl.BlockSpec((1, 256), fixed),
                  pl.BlockSpec((384, 1024), fixed), pl.BlockSpec((256, 1536), fixed), tspec, tspec, tspec],
        out_specs=[pl.BlockSpec((tm, 1024), row), pl.BlockSpec((tm, 1024), row), pl.BlockSpec((tm, 512), row),
                   pl.BlockSpec((384, tm), lambda i: (0, i)), pl.BlockSpec((256, tm), lambda i: (0, i))],
        out_shape=[jax.ShapeDtypeStruct((S, 1024), F32), jax.ShapeDtypeStruct((S, 1024), F32),
                   jax.ShapeDtypeStruct((S, 512), F32), jax.ShapeDtypeStruct((384, S), BF16),
                   jax.ShapeDtypeStruct((256, S), BF16)],
        compiler_params=_cparams(("parallel",)),
    )(proj, gq, gkv, wq, wkv, cosT, s1T, s2T)


def _mla_prep_bwd(dq, dk, dv, proj, gq, gkv, wq, wkv, cosT, s1T, s2T, name):
    S = proj.shape[0]
    tm = _pick(S, (256,))

    def body(dq_ref, dk_ref, dv_ref, p_ref, gq_ref, gkv_ref, wq_ref, wkv_ref, c_ref, s1_ref, s2_ref,
             dp_ref, dqb_ref, dkvb_ref, dgq_ref, dgkv_ref):
        @pl.when(pl.program_id(0) == 0)
        def _():
            dgq_ref[...] = jnp.zeros_like(dgq_ref)
            dgkv_ref[...] = jnp.zeros_like(dgkv_ref)
        c, s1, s2 = c_ref[...], s1_ref[...], s2_ref[...]
        lane = lax.broadcasted_iota(jnp.int32, (1, LANES), 1)
        dkr = jnp.zeros((tm, LANES), F32)
        for h in range(8):
            sl = slice(h * LANES, (h + 1) * LANES)
            dqb_ref[:, sl] = _rope_t(dq_ref[:, sl], c, s1, s2).astype(BF16)
            dkh = dk_ref[:, sl]
            dkvb_ref[:, sl] = dkh.astype(BF16)
            dkr = dkr + dkh
        dkvb_ref[:, 1024:1536] = dv_ref[...].astype(BF16)
        dkr = jnp.where((lane >= 64) & (lane < 96), _rope_t(dkr, c, s1, s2), 0.0)
        dqn = _dot_nt(dqb_ref[...], wq_ref[...])
        dcn = _dot_nt(dkvb_ref[...], wkv_ref[...])
        dqa, gq_row = _norm_bwd(p_ref[:, 0:384], gq_ref[...], dqn)
        dckv, gkv_row = _norm_bwd(p_ref[:, 384:640], gkv_ref[...], dcn)
        dp_ref[:, 0:384] = dqa
        dp_ref[:, 384:640] = dckv
        dp_ref[:, 640:768] = dkr
        dgq_ref[...] += jnp.sum(gq_row, axis=0, keepdims=True)
        dgkv_ref[...] += jnp.sum(gkv_row, axis=0, keepdims=True)

    row = lambda i: (i, 0)
    fixed = lambda i: (0, 0)
    tspec = pl.BlockSpec((tm, LANES), row)
    return pl.pallas_call(
        body, name=name, grid=(S // tm,),
        in_specs=[pl.BlockSpec((tm, 1024), row), pl.BlockSpec((tm, 1024), row), pl.BlockSpec((tm, 512), row),
                  pl.BlockSpec((tm, L0_PREP_W), lambda i: (i, L0_PREP // L0_PREP_W)),
                  pl.BlockSpec((1, 384), fixed), pl.BlockSpec((1, 256), fixed),
                  pl.BlockSpec((384, 1024), fixed), pl.BlockSpec((256, 1536), fixed), tspec, tspec, tspec],
        out_specs=[pl.BlockSpec((tm, L0_PREP_W), row), pl.BlockSpec((tm, 1024), row), pl.BlockSpec((tm, 1536), row),
                   pl.BlockSpec((1, 384), fixed), pl.BlockSpec((1, 256), fixed)],
        out_shape=[jax.ShapeDtypeStruct((S, L0_PREP_W), F32), jax.ShapeDtypeStruct((S, 1024), BF16),
                   jax.ShapeDtypeStruct((S, 1536), BF16), jax.ShapeDtypeStruct((1, 384), F32),
                   jax.ShapeDtypeStruct((1, 256), F32)],
        compiler_params=_cparams(("arbitrary",)),
    )(dq, dk, dv, proj, gq, gkv, wq, wkv, cosT, s1T, s2T)


def _fox_prep(proj, bf, name):
    S = proj.shape[0]
    tm = _pick(S, (256,))

    def body(f_ref, b_ref, c_ref, carry_ref):
        @pl.when(pl.program_id(0) == 0)
        def _():
            carry_ref[...] = jnp.zeros_like(carry_ref)
        u = f_ref[...] + b_ref[...]
        lf = jnp.minimum(u, 0.0) - jnp.log(1.0 + jnp.exp(-jnp.abs(u)))
        r = lax.broadcasted_iota(jnp.int32, (tm, tm), 0)
        cidx = lax.broadcasted_iota(jnp.int32, (tm, tm), 1)
        tri = (cidx <= r).astype(BF16)
        hi, mid, lo = _split3(lf)
        c = carry_ref[...] + (_dot(tri, hi) + _dot(tri, mid) + _dot(tri, lo))
        c_ref[...] = c
        carry_ref[...] = c[tm - 1:tm, :]

    return pl.pallas_call(
        body, name=name, grid=(S // tm,),
        in_specs=[pl.BlockSpec((tm, LANES), lambda i: (i, L1_F // LANES)), pl.BlockSpec((1, LANES), lambda i: (0, 0))],
        out_specs=pl.BlockSpec((tm, LANES), lambda i: (i, 0)),
        out_shape=jax.ShapeDtypeStruct((S, LANES), F32),
        scratch_shapes=[pltpu.VMEM((1, LANES), F32)],
        compiler_params=_cparams(("arbitrary",)),
    )(proj, bf)


def _fox_prep_bwd(dc, proj, bf, name):
    S = proj.shape[0]
    tm = _pick(S, (256,))
    nb = S // tm

    def body(dc_ref, f_ref, b_ref, df_ref, db_ref, carry_ref):
        @pl.when(pl.program_id(0) == 0)
        def _():
            carry_ref[...] = jnp.zeros_like(carry_ref)
            db_ref[...] = jnp.zeros_like(db_ref)
        r = lax.broadcasted_iota(jnp.int32, (tm, tm), 0)
        cidx = lax.broadcasted_iota(jnp.int32, (tm, tm), 1)
        tri = (cidx >= r).astype(BF16)
        hi, mid, lo = _split3(dc_ref[...])
        dlf = carry_ref[...] + (_dot(tri, hi) + _dot(tri, mid) + _dot(tri, lo))
        carry_ref[...] = dlf[0:1, :]
        u = f_ref[...] + b_ref[...]
        e = jnp.exp(-jnp.abs(u))
        sneg = jnp.where(u >= 0.0, e, 1.0) / (1.0 + e)
        lane = lax.broadcasted_iota(jnp.int32, (1, LANES), 1)
        df = jnp.where(lane < FOX_HEADS, dlf * sneg, 0.0)
        df_ref[...] = df
        db_ref[...] += jnp.sum(df, axis=0, keepdims=True)

    return pl.pallas_call(
        body, name=name, grid=(nb,),
        in_specs=[pl.BlockSpec((tm, LANES), lambda i: (nb - 1 - i, 0)),
                  pl.BlockSpec((tm, LANES), lambda i: (nb - 1 - i, L1_F // LANES)),
                  pl.BlockSpec((1, LANES), lambda i: (0, 0))],
        out_specs=[pl.BlockSpec((tm, LANES), lambda i: (nb - 1 - i, 0)), pl.BlockSpec((1, LANES), lambda i: (0, 0))],
        out_shape=[jax.ShapeDtypeStruct((S, LANES), F32), jax.ShapeDtypeStruct((1, LANES), F32)],
        scratch_shapes=[pltpu.VMEM((1, LANES), F32)],
        compiler_params=_cparams(("arbitrary",)),
    )(dc, proj, bf)


def _att_specs(kind, S, T):
    if kind == "sb":
        qo, ko, vo, go = L0_SBQ // LANES, L0_SBK // LANES, L0_SBV // LANES, L0_SBG // LANES
    elif kind == "fox":
        qo, ko, vo, go = L1_Q // LANES, L1_K // LANES, L1_V // LANES, L1_G // LANES
    else:
        go = L0_MLG // LANES
        return (pl.BlockSpec((T, 256), lambda p, i: (i, p)), pl.BlockSpec((S, 256), lambda p, i: (0, p)),
                pl.BlockSpec((S, LANES), lambda p, i: (0, p)), pl.BlockSpec((T, LANES), lambda p, i: (i, go + p)))
    return (pl.BlockSpec((T, LANES), lambda p, i: (i, qo + p)), pl.BlockSpec((S, LANES), lambda p, i: (0, ko + p)),
            pl.BlockSpec((S, LANES), lambda p, i: (0, vo + p)), pl.BlockSpec((T, LANES), lambda p, i: (i, go + p)))


def _mask_flags(js, masked_at):
    return [t == masked_at for t in range(len(js))]


def _loop_tiles(i, tiles, right_to_left):
    G = ATT_GROUP
    ng = i // G
    rest = i - ng * G

    def leftover():
        for r in range(G):
            @pl.when(rest == r)
            def _():
                if right_to_left:
                    tiles([i - u for u in range(r + 1)], 0)
                else:
                    tiles([ng * G + u for u in range(r + 1)], r)

    def group(g, carry):
        if right_to_left:
            tiles([ng * G - 1 - (g * G + u) for u in range(G)], None)
        else:
            tiles([g * G + u for u in range(G)], None)
        return carry

    if right_to_left:
        leftover()
    lax.fori_loop(0, ng, group, 0)
    if not right_to_left:
        leftover()


def _head_q(kind, q_ref, m0, scale):
    if kind == "mla":
        return [q_ref[:, 0:LANES].astype(BF16), q_ref[:, LANES:2 * LANES].astype(BF16)]
    qv = q_ref[...] * scale
    return [jnp.where(m0, qv, 0.0).astype(BF16), jnp.where(m0, 0.0, qv).astype(BF16)]


def _head_k(kind, k_ref, start, T):
    if kind == "mla":
        return [k_ref[pl.ds(start, T), 0:LANES].astype(BF16), k_ref[pl.ds(start, T), LANES:2 * LANES].astype(BF16)]
    kb = k_ref[pl.ds(start, T), :].astype(BF16)
    return [kb, kb]


def _transpose_tiles(src, col_off, n_out, cw, group, name):
    S = src.shape[0]
    T = ATT_T
    first = col_off // (group * cw)

    def body(x_ref, o_ref):
        for u in range(group):
            o_ref[u] = x_ref[:, u * cw:(u + 1) * cw].T.astype(BF16)

    return pl.pallas_call(
        body, name=name, grid=(S // T, n_out // group),
        in_specs=[pl.BlockSpec((T, group * cw), lambda j, g: (j, first + g))],
        out_specs=pl.BlockSpec((group, None, cw, T), lambda j, g: (g, j, 0, 0)),
        out_shape=jax.ShapeDtypeStruct((n_out, S // T, cw, T), BF16),
        compiler_params=_cparams(("parallel", "parallel")),
    )(src)


def _softmax_fwd(kind, qkvg, c_col, S, npairs, name):
    T = ATT_T
    nq = S // T
    fox = kind == "fox"
    scale = (96 if kind == "mla" else 64) ** -0.5

    def body(*refs):
        if fox:
            q_ref, k_ref, vt_ref, g_ref, cc_ref, o_ref, og_ref, ogt_ref, st_ref, m_ref, acc_ref = refs
        else:
            q_ref, k_ref, vt_ref, g_ref, o_ref, og_ref, ogt_ref, st_ref, m_ref, acc_ref = refs
        i = pl.program_id(1)
        m0 = lax.broadcasted_iota(jnp.int32, (1, LANES), 1) < 64
        top = lax.broadcasted_iota(jnp.int32, (LANES, 1), 0) < 64
        key = lax.broadcasted_iota(jnp.int32, (T, LANES), 0)
        qrow = lax.broadcasted_iota(jnp.int32, (T, LANES), 1)
        qh = _head_q(kind, q_ref, m0, scale)
        m_ref[...] = jnp.full(m_ref.shape, NEG, F32)
        acc_ref[...] = jnp.zeros(acc_ref.shape, F32)
        chains = [(h, b) for h in range(2) for b in range(T // LANES)]

        def tiles(js, masked_at):
            starts = [pl.multiple_of(j * T, T) for j in js]
            zss = []
            for start in starts:
                kh = _head_k(kind, k_ref, start, T)
                zss.append(_split_blocks([_dot_nt(kh[h], qh[h]) for h in range(2)]))
            pss, alss = [], []
            for start, zs, masked in zip(starts, zss, _mask_flags(js, masked_at)):
                ps, alphas = [], []
                for (h, b), z in zip(chains, zs):
                    lanes = slice(b * LANES, (b + 1) * LANES)
                    if kind == "mla":
                        z = z * scale
                    if fox:
                        z = z - cc_ref[h, pl.ds(start, T), :]
                    if masked:
                        z = jnp.where(key <= qrow + b * LANES, z, NEG)
                    m_prev = m_ref[h, :, lanes]
                    m_new = jnp.maximum(m_prev, jnp.max(z, axis=0, keepdims=True))
                    alphas.append(jnp.exp(m_prev - m_new))
                    ps.append(jnp.exp(z - m_new).astype(BF16))
                    m_ref[h, :, lanes] = m_new
                pss.append(_join_blocks(ps, T // LANES))
                alss.append(_join_blocks(alphas, T // LANES))
            for j, ps, alphas in zip(js, pss, alss):
                vt = vt_ref[j]
                vth = [jnp.where(top, vt, 1.0).astype(BF16), jnp.where(top, 1.0, vt).astype(BF16)]
                for h in range(2):
                    acc_ref[h] = alphas[h] * acc_ref[h] + _dot(vth[h], ps[h])

        _loop_tiles(i, tiles, False)
        acc = [acc_ref[0], acc_ref[1]]
        ot = jnp.concatenate([acc[0][0:64] / acc[0][64:128], acc[1][64:128] / acc[1][0:64]], axis=0)
        o = ot.T
        o_ref[...] = o
        gt = g_ref[...]
        og = o * (gt * _sigmoid(gt))
        og_ref[...] = og.astype(BF16)
        ogt_ref[...] = og.T.astype(BF16)
        st_ref[0] = m_ref[0] + jnp.log(acc[0][64:65])
        st_ref[1] = m_ref[1] + jnp.log(acc[1][0:1])

    qs, ks, _, gs = _att_specs(kind, S, T)
    in_specs = [qs, ks, pl.BlockSpec((None, nq, LANES, T), lambda p, i: (p, 0, 0, 0)), gs]
    args = list(qkvg)
    if fox:
        in_specs += [pl.BlockSpec((2, S, LANES), lambda p, i: (p, 0, 0))]
        args += [c_col]
    W = npairs * LANES
    return pl.pallas_call(
        body, name=name, grid=(npairs, nq), in_specs=in_specs,
        out_specs=[pl.BlockSpec((T, LANES), lambda p, i: (i, p)), pl.BlockSpec((T, LANES), lambda p, i: (i, p)),
                   pl.BlockSpec((LANES, T), lambda p, i: (p, i)),
                   pl.BlockSpec((2, None, 1, T), lambda p, i: (p, i, 0, 0))],
        out_shape=[jax.ShapeDtypeStruct((S, W), F32), jax.ShapeDtypeStruct((S, W), BF16),
                   jax.ShapeDtypeStruct((W, S), BF16),
                   jax.ShapeDtypeStruct((2 * npairs, nq, 1, T), F32)],
        scratch_shapes=[pltpu.VMEM((2, 1, T), F32), pltpu.VMEM((2, LANES, T), F32)],
        compiler_params=_cparams(("parallel", "parallel")),
    )(*args)


def _softmax_bwd(kind, qkv, do, do_off, o, lse, c_row, S, npairs, name):
    T = ATT_T
    nq = S // T
    fox = kind == "fox"
    mla = kind == "mla"
    scale = (96 if mla else 64) ** -0.5
    kw = 256 if mla else LANES

    def body(*refs):
        if fox:
            q_ref, k_ref, v_ref, do_ref, o_ref, st_ref, cr_ref, dq_ref, dk_ref, dv_ref, dc_ref, dcq_ref = refs
        else:
            q_ref, k_ref, v_ref, do_ref, o_ref, st_ref, dq_ref, dk_ref, dv_ref = refs
        i = pl.program_id(1)

        @pl.when(i == 0)
        def _():
            dk_ref[...] = jnp.zeros_like(dk_ref)
            dv_ref[...] = jnp.zeros_like(dv_ref)
            if fox:
                dc_ref[...] = jnp.zeros_like(dc_ref)

        m0 = lax.broadcasted_iota(jnp.int32, (1, LANES), 1) < 64
        causal = lax.broadcasted_iota(jnp.int32, (T, T), 1) <= lax.broadcasted_iota(jnp.int32, (T, T), 0)
        qh = _head_q(kind, q_ref, m0, scale)
        dov = do_ref[...]
        prod = dov * o_ref[...]
        dd = [jnp.sum(jnp.where(m0, prod, 0.0), axis=1, keepdims=True),
              jnp.sum(jnp.where(m0, 0.0, prod), axis=1, keepdims=True)]
        doh = [jnp.where(m0, dov, 0.0).astype(BF16), jnp.where(m0, 0.0, dov).astype(BF16)]
        lse_h = [st_ref[0], st_ref[1]]

        def tile(j, carry, masked):
            start = pl.multiple_of(j * T, T)
            vb = v_ref[pl.ds(start, T), :].astype(BF16)
            kh = _head_k(kind, k_ref, start, T)
            dqs = []
            dkc = []
            dvc = jnp.zeros((T, LANES), F32)
            for h in range(2):
                z = _dot_nt(qh[h], kh[h])
                if mla:
                    z = z * scale
                if fox:
                    z = z - cr_ref[h, pl.ds(j, 1), :]
                if masked:
                    z = jnp.where(causal, z, NEG)
                p = jnp.exp(z - lse_h[h])
                ds = p * (_dot_nt(doh[h], vb) - dd[h])
                dsb = ds.astype(BF16)
                dqh = carry[h][0] + _dot(dsb, kh[h])
                dkc.append(_dot_tn(dsb, qh[h]))
                dvc = dvc + _dot_tn(p.astype(BF16), doh[h])
                if fox:
                    dc_ref[h, pl.ds(j, 1), :] += -jnp.sum(ds, axis=0, keepdims=True)
                    dqs.append((dqh, carry[h][1] + jnp.sum(ds, axis=1, keepdims=True)))
                else:
                    dqs.append((dqh,))
            if mla:
                dk_ref[pl.ds(start, T), 0:LANES] += dkc[0] * scale
                dk_ref[pl.ds(start, T), LANES:2 * LANES] += dkc[1] * scale
            else:
                dk_ref[pl.ds(start, T), :] += dkc[0] + dkc[1]
            dv_ref[pl.ds(start, T), :] += dvc
            return tuple(dqs)

        one = (jnp.zeros((T, LANES), F32), jnp.zeros((T, 1), F32)) if fox else (jnp.zeros((T, LANES), F32),)
        carry = lax.fori_loop(0, i, lambda j, c: tile(j, c, False), (one, one))
        carry = tile(i, carry, True)
        if mla:
            dq_ref[:, 0:LANES] = carry[0][0] * scale
            dq_ref[:, LANES:2 * LANES] = carry[1][0] * scale
        else:
            dq_ref[...] = jnp.where(m0, carry[0][0], carry[1][0]) * scale
        if fox:
            dcq_ref[0] = carry[0][1]
            dcq_ref[1] = carry[1][1]

    qs, ks, vs, _ = _att_specs(kind, S, T)
    in_specs = [qs, ks, vs,
                pl.BlockSpec((T, LANES), lambda p, i: (i, do_off + p)),
                pl.BlockSpec((T, LANES), lambda p, i: (i, p)),
                pl.BlockSpec((2, T, 1), lambda p, i: (p, i, 0))]
    args = list(qkv) + [do, o, lse]
    W = npairs * LANES
    out_specs = [pl.BlockSpec((T, kw), lambda p, i: (i, p)), pl.BlockSpec((S, kw), lambda p, i: (0, p)),
                 pl.BlockSpec((S, LANES), lambda p, i: (0, p))]
    out_shape = [jax.ShapeDtypeStruct((S, npairs * kw), F32), jax.ShapeDtypeStruct((S, npairs * kw), F32),
                 jax.ShapeDtypeStruct((S, W), F32)]
    if fox:
        in_specs += [pl.BlockSpec((2, nq, T), lambda p, i: (p, 0, 0))]
        args += [c_row]
        out_specs +=[pl.BlockSpec((2, nq, T), lambda p, i: (p, 0, 0)), pl.BlockSpec((2, T, 1), lambda p, i: (p, i, 0))]
        out_shape += [jax.ShapeDtypeStruct((2 * npairs, nq, T), F32), jax.ShapeDtypeStruct((2 * npairs, S, 1), F32)]
    return pl.pallas_call(
        body, name=name, grid=(npairs, nq), in_specs=in_specs, out_specs=out_specs, out_shape=out_shape,
        compiler_params=_cparams(("parallel", "arbitrary")),
    )(*args)


def _softplus_parts(z):
    sp = jnp.maximum(z, 0.0) + jnp.log(1.0 + jnp.exp(-jnp.abs(z)))
    return -sp, z - sp


def _split2(x):
    hi = x.astype(BF16)
    return hi, (x - hi.astype(F32)).astype(BF16)


def _sb_fwd(proj, S, npairs, name):
    T = ATT_T
    nq = S // T
    scale = 64 ** -0.5

    def body(q_ref, k_ref, v_ref, g_ref, o_ref, og_ref, st_ref):
        i = pl.program_id(1)
        m0 = lax.broadcasted_iota(jnp.int32, (1, LANES), 1) < 64
        r = lax.broadcasted_iota(jnp.int32, (T, T), 0)
        c = lax.broadcasted_iota(jnp.int32, (T, T), 1)
        before = c < r
        after = (r > c).astype(BF16)
        qh = _head_q("sb", q_ref, m0, scale)

        def tile(j, carry, masked):
            start = pl.multiple_of(j * T, T)
            vb = v_ref[pl.ds(start, T), :].astype(BF16)
            kb = k_ref[pl.ds(start, T), :].astype(BF16)
            out = []
            for h in range(2):
                rem, acc = carry[h]
                z = _dot_nt(qh[h], kb)
                lk, la = _softplus_parts(z)
                if masked:
                    lk = jnp.where(before, lk, 0.0)
                hi, lo = _split2(lk)
                lr = rem + (_dot(hi, after) + _dot(lo, after))
                w = jnp.exp(la + lr)
                if masked:
                    w = jnp.where(before, w, 0.0)
                out.append((rem + jnp.sum(lk, axis=1, keepdims=True), acc + _dot(w.astype(BF16), vb)))
            return tuple(out)

        init = tuple((jnp.zeros((T, 1), F32), jnp.zeros((T, LANES), F32)) for _ in range(2))
        carry = tile(i, init, True)
        carry = lax.fori_loop(0, i, lambda jj, cr: tile(i - 1 - jj, cr, False), carry)
        o = jnp.where(m0, carry[0][1], carry[1][1])
        o_ref[...] = o
        gt = g_ref[...]
        og_ref[...] = (o * (gt * _sigmoid(gt))).astype(BF16)
        for h in range(2):
            st_ref[h] = carry[h][0]

    W = npairs * LANES
    return pl.pallas_call(
        body, name=name, grid=(npairs, nq), in_specs=list(_att_specs("sb", S, T)),
        out_specs=[pl.BlockSpec((T, LANES), lambda p, i: (i, p)), pl.BlockSpec((T, LANES), lambda p, i: (i, p)),
                   pl.BlockSpec((2, T, 1), lambda p, i: (p, i, 0))],
        out_shape=[jax.ShapeDtypeStruct((S, W), F32), jax.ShapeDtypeStruct((S, W), BF16),
                   jax.ShapeDtypeStruct((2 * npairs, S, 1), F32)],
        compiler_params=_cparams(("parallel", "parallel")),
    )(proj, proj, proj, proj)


def _sb_bwd(proj, do, tot, S, npairs, name):
    T = ATT_T
    nq = S // T
    scale = 64 ** -0.5

    def body(q_ref, k_ref, v_ref, do_ref, st_ref, dq_ref, dk_ref, dv_ref):
        i = pl.program_id(1)

        @pl.when(i == 0)
        def _():
            dk_ref[...] = jnp.zeros_like(dk_ref)
            dv_ref[...] = jnp.zeros_like(dv_ref)

        m0 = lax.broadcasted_iota(jnp.int32, (1, LANES), 1) < 64
        r = lax.broadcasted_iota(jnp.int32, (T, T), 0)
        c = lax.broadcasted_iota(jnp.int32, (T, T), 1)
        before = c < r
        upto = (r <= c).astype(BF16)
        left = (r < c).astype(BF16)
        qh = _head_q("sb", q_ref, m0, scale)
        dov = do_ref[...]
        doh = [jnp.where(m0, dov, 0.0).astype(BF16), jnp.where(m0, 0.0, dov).astype(BF16)]
        tot_h = [st_ref[0], st_ref[1]]

        def tile(j, carry, masked):
            start = pl.multiple_of(j * T, T)
            vb = v_ref[pl.ds(start, T), :].astype(BF16)
            kb = k_ref[pl.ds(start, T), :].astype(BF16)
            out = []
            dkc = jnp.zeros((T, LANES), F32)
            dvc = jnp.zeros((T, LANES), F32)
            for h in range(2):
                pre, gpre, dq = carry[h]
                z = _dot_nt(qh[h], kb)
                lk, la = _softplus_parts(z)
                if masked:
                    lk = jnp.where(before, lk, 0.0)
                hi, lo = _split2(lk)
                lr = (tot_h[h] - pre) - (_dot(hi, upto) + _dot(lo, upto))
                w = jnp.exp(la + lr)
                if masked:
                    w = jnp.where(before, w, 0.0)
                g = _dot_nt(doh[h], vb) * w
                gfull = gpre + _dot(g.astype(BF16), left)
                dz = g - (g + gfull) * jnp.exp(la)
                if masked:
                    dz = jnp.where(before, dz, 0.0)
                dzb = dz.astype(BF16)
                dkc = dkc + _dot_tn(dzb, qh[h])
                dvc = dvc + _dot_tn(w.astype(BF16), doh[h])
                out.append((pre + jnp.sum(lk, axis=1, keepdims=True), gpre + jnp.sum(g, axis=1, keepdims=True),
                            dq + _dot(dzb, kb)))
            dk_ref[pl.ds(start, T), :] += dkc
            dv_ref[pl.ds(start, T), :] += dvc
            return tuple(out)

        init = tuple((jnp.zeros((T, 1), F32), jnp.zeros((T, 1), F32), jnp.zeros((T, LANES), F32)) for _ in range(2))
        carry = lax.fori_loop(0, i, lambda j, cr: tile(j, cr, False), init)
        carry = tile(i, carry, True)
        dq_ref[...] = jnp.where(m0, carry[0][2], carry[1][2]) * scale

    qs, ks, vs, _ = _att_specs("sb", S, T)
    W = npairs * LANES
    return pl.pallas_call(
        body, name=name, grid=(npairs, nq),
        in_specs=[qs, ks, vs, pl.BlockSpec((T, LANES), lambda p, i: (i, p)),
                  pl.BlockSpec((2, T, 1), lambda p, i: (p, i, 0))],
        out_specs=[pl.BlockSpec((T, LANES), lambda p, i: (i, p)), pl.BlockSpec((S, LANES), lambda p, i: (0, p)),
                   pl.BlockSpec((S, LANES), lambda p, i: (0, p))],
        out_shape=[jax.ShapeDtypeStruct((S, W), F32)] * 3,
        compiler_params=_cparams(("parallel", "arbitrary")),
    )(proj, proj, proj, do, tot)


def _split_blocks(per_head):
    return [x[:, b * LANES:(b + 1) * LANES] for x in per_head for b in range(x.shape[1] // LANES)]


def _join_blocks(per_block, nb):
    return [jnp.concatenate(per_block[h * nb:(h + 1) * nb], axis=1) for h in range(len(per_block) // nb)]


def _row_of(col):
    return jnp.broadcast_to(col, (col.shape[0], LANES)).T[0:1]


def _softmax_bwd_t(kind, q, k, kt, v, do, do_off, o, lse, c_col, S, npairs, name):
    T = ATT_T
    nq = S // T
    nb = T // LANES
    fox = kind == "fox"
    mla = kind == "mla"
    scale = (96 if mla else 64) ** -0.5
    kw = 256 if mla else LANES

    def body(*refs):
        if fox:
            (q_ref, k_ref, kt_ref, v_ref, do_ref, o_ref, st_ref, cc_ref,
             dq_ref, dk_ref, dv_ref, dck_ref, dcq_ref, dqt_ref, rs_ref, dkx_ref) = refs
        else:
            q_ref, k_ref, kt_ref, v_ref, do_ref, o_ref, st_ref, dq_ref, dk_ref, dv_ref, dqt_ref = refs
        i = pl.program_id(1)

        @pl.when(i == 0)
        def _():
            dv_ref[...] = jnp.zeros_like(dv_ref)
            if fox:
                dkx_ref[...] = jnp.zeros_like(dkx_ref)
            else:
                dk_ref[...] = jnp.zeros_like(dk_ref)

        m0 = lax.broadcasted_iota(jnp.int32, (1, LANES), 1) < 64
        top = lax.broadcasted_iota(jnp.int32, (LANES, 1), 0) < 64
        key = lax.broadcasted_iota(jnp.int32, (T, LANES), 0)
        qrow = lax.broadcasted_iota(jnp.int32, (T, LANES), 1)
        qh = _head_q(kind, q_ref, m0, scale)
        if fox:
            qv = q_ref[...] * scale
            qk = [jnp.where(m0, qv, 1.0).astype(BF16), jnp.where(m0, 1.0, qv).astype(BF16)]
        else:
            qk = qh
        dov = do_ref[...]
        prod = dov * o_ref[...]
        dd = [_row_of(jnp.sum(jnp.where(m0, prod, 0.0), axis=1, keepdims=True)),
              _row_of(jnp.sum(jnp.where(m0, 0.0, prod), axis=1, keepdims=True))]
        doh = [jnp.where(m0, dov, 0.0).astype(BF16), jnp.where(m0, 0.0, dov).astype(BF16)]
        lse = [st_ref[0], st_ref[1]]
        dqt_ref[...] = jnp.zeros_like(dqt_ref)
        if fox:
            rs_ref[...] = jnp.zeros_like(rs_ref)
        chains = [(h, b) for h in range(2) for b in range(nb)]

        def tiles(js, masked_at):
            starts = [pl.multiple_of(j * T, T) for j in js]
            zss, dpss = [], []
            for start in starts:
                vb = v_ref[pl.ds(start, T), :].astype(BF16)
                kh = _head_k(kind, k_ref, start, T)
                zss.append(_split_blocks([_dot_nt(kh[h], qh[h]) for h in range(2)]))
                dpss.append(_split_blocks([_dot_nt(vb, doh[h]) for h in range(2)]))
            pss, dsss = [], []
            for start, zs, dps, masked in zip(starts, zss, dpss, _mask_flags(js, masked_at)):
                ps, dss = [], []
                for (h, b), z, dp in zip(chains, zs, dps):
                    lanes = slice(b * LANES, (b + 1) * LANES)
                    if mla:
                        z = z * scale
                    if fox:
                        z = z - cc_ref[h, pl.ds(start, T), :]
                    if masked:
                        z = jnp.where(key <= qrow + b * LANES, z, NEG)
                    p = jnp.exp(z - lse[h][:, lanes])
                    ds = p * (dp - dd[h][:, lanes])
                    dsb = ds.astype(BF16)
                    if fox:
                        rs_ref[h, :, lanes] += jnp.sum(dsb.astype(F32), axis=0, keepdims=True)
                    ps.append(p.astype(BF16))
                    dss.append(dsb)
                pss.append(_join_blocks(ps, nb))
                dsss.append(_join_blocks(dss, nb))
            for j, start, ps, dss in zip(js, starts, pss, dsss):
                kt = kt_ref[j]
                dvc = None
                for h in range(2):
                    dkh = _dot(dss[h], qk[h])
                    dvh = _dot(ps[h], doh[h])
                    dvc = dvh if dvc is None else dvc + dvh
                    kth = kt[h * LANES:(h + 1) * LANES] if mla else kt
                    dqt_ref[h] += _dot(kth, dss[h])
                    if fox:
                        dkx_ref[h, pl.ds(start, T), :] += dkh
                    elif mla:
                        dk_ref[pl.ds(start, T), h * LANES:(h + 1) * LANES] += dkh * scale
                    else:
                        dk_ref[pl.ds(start, T), :] += dkh
                dv_ref[pl.ds(start, T), :] += dvc

        _loop_tiles(i, tiles, False)
        if mla:
            dq_ref[:, 0:LANES] = dqt_ref[0].T * scale
            dq_ref[:, LANES:2 * LANES] = dqt_ref[1].T * scale
        else:
            dq_ref[...] = jnp.where(top, dqt_ref[0], dqt_ref[1]).T * scale
        if fox:
            dcq_ref[0] = rs_ref[0]
            dcq_ref[1] = rs_ref[1]

            @pl.when(i == nq - 1)
            def _():
                dk_ref[...] = jnp.where(m0, dkx_ref[0], dkx_ref[1])
                dck_ref[0] = dkx_ref[0][:, 64:65]
                dck_ref[1] = dkx_ref[1][:, 0:1]

    qs, ks, vs, _ = _att_specs(kind, S, T)
    stat = pl.BlockSpec((2, None, 1, T), lambda p, i: (p, i, 0, 0))
    in_specs = [qs, ks, pl.BlockSpec((None, nq, kw, T), lambda p, i: (p, 0, 0, 0)), vs,
                pl.BlockSpec((T, LANES), lambda p, i: (i, do_off + p)),
                pl.BlockSpec((T, LANES), lambda p, i: (i, p)), stat]
    args = [q, k, kt, v, do, o, lse]
    W = npairs * LANES
    out_specs = [pl.BlockSpec((T, kw), lambda p, i: (i, p)), pl.BlockSpec((S, kw), lambda p, i: (0, p)),
                 pl.BlockSpec((S, LANES), lambda p, i: (0, p))]
    out_shape = [jax.ShapeDtypeStruct((S, npairs * kw), F32), jax.ShapeDtypeStruct((S, npairs * kw), F32),
                 jax.ShapeDtypeStruct((S, W), F32)]
    scratch = [pltpu.VMEM((2, LANES, T), F32)]
    if fox:
        in_specs.append(pl.BlockSpec((2, S, LANES), lambda p, i: (p, 0, 0)))
        args.append(c_col)
        out_specs += [pl.BlockSpec((2, S, 1), lambda p, i: (p, 0, 0)), stat]
        out_shape += [jax.ShapeDtypeStruct((2 * npairs, S, 1), F32), jax.ShapeDtypeStruct((2 * npairs, nq, 1, T), F32)]
        scratch += [pltpu.VMEM((2, 1, T), F32), pltpu.VMEM((2, S, LANES), F32)]
    return pl.pallas_call(
        body, name=name, grid=(npairs, nq), in_specs=in_specs, out_specs=out_specs, out_shape=out_shape,
        scratch_shapes=scratch, compiler_params=_cparams(("parallel", "arbitrary")),
    )(*args)


def _sb_fwd_t(proj, vt, S, npairs, name):
    T = ATT_T
    nq = S // T
    nb = T // LANES
    scale = 64 ** -0.5

    def body(q_ref, k_ref, vt_ref, g_ref, o_ref, og_ref, ogt_ref, st_ref, rem_ref, acc_ref):
        i = pl.program_id(1)
        m0 = lax.broadcasted_iota(jnp.int32, (1, LANES), 1) < 64
        top = lax.broadcasted_iota(jnp.int32, (LANES, 1), 0) < 64
        key = lax.broadcasted_iota(jnp.int32, (T, LANES), 0)
        qrow = lax.broadcasted_iota(jnp.int32, (T, LANES), 1)
        r = lax.broadcasted_iota(jnp.int32, (T, T), 0)
        c = lax.broadcasted_iota(jnp.int32, (T, T), 1)
        after = (c > r).astype(BF16)
        qh = _head_q("sb", q_ref, m0, scale)
        rem_ref[...] = jnp.zeros_like(rem_ref)
        acc_ref[...] = jnp.zeros_like(acc_ref)
        chains = [(h, b) for h in range(2) for b in range(nb)]

        def tiles(js, masked_at):
            zss = []
            for j in js:
                kb = k_ref[pl.ds(pl.multiple_of(j * T, T), T), :].astype(BF16)
                zss.append(_split_blocks([_dot_nt(kb, qh[h]) for h in range(2)]))
            lass, sums, hiss, loss = [], [], [], []
            for zs, masked in zip(zss, _mask_flags(js, masked_at)):
                las, sm, his, los = [], [], [], []
                for (h, b), z in zip(chains, zs):
                    lk, la = _softplus_parts(z)
                    if masked:
                        lk = jnp.where(key < qrow + b * LANES, lk, 0.0)
                    hi, lo = _split2(lk)
                    las.append(la)
                    sm.append(jnp.sum(lk, axis=0, keepdims=True))
                    his.append(hi)
                    los.append(lo)
                lass.append(las)
                sums.append(sm)
                hiss.append(_join_blocks(his, nb))
                loss.append(_join_blocks(los, nb))
            rcss = [_split_blocks([_dot(after, hi) + _dot(after, lo) for hi, lo in zip(his, los)])
                    for his, los in zip(hiss, loss)]
            wss = []
            for las, sm, rcs, masked in zip(lass, sums, rcss, _mask_flags(js, masked_at)):
                ws = []
                for (h, b), la, s, rc in zip(chains, las, sm, rcs):
                    lanes = slice(b * LANES, (b + 1) * LANES)
                    w = jnp.exp(la + (rem_ref[h, :, lanes] + rc))
                    if masked:
                        w = jnp.where(key < qrow + b * LANES, w, 0.0)
                    ws.append(w.astype(BF16))
                    rem_ref[h, :, lanes] += s
                wss.append(_join_blocks(ws, nb))
            for j, ws in zip(js, wss):
                vtb = vt_ref[j]
                for h in range(2):
                    acc_ref[h] += _dot(vtb, ws[h])

        _loop_tiles(i, tiles, True)
        o = jnp.where(top, acc_ref[0], acc_ref[1]).T
        o_ref[...] = o
        gt = g_ref[...]
        og = o * (gt * _sigmoid(gt))
        og_ref[...] = og.astype(BF16)
        ogt_ref[...] = og.T.astype(BF16)
        st_ref[0] = rem_ref[0]
        st_ref[1] = rem_ref[1]

    qs, ks, _, gs = _att_specs("sb", S, T)
    W = npairs * LANES
    return pl.pallas_call(
        body, name=name, grid=(npairs, nq),
        in_specs=[qs, ks, pl.BlockSpec((None, nq, LANES, T), lambda p, i: (p, 0, 0, 0)), gs],
        out_specs=[pl.BlockSpec((T, LANES), lambda p, i: (i, p)), pl.BlockSpec((T, LANES), lambda p, i: (i, p)),
                   pl.BlockSpec((LANES, T), lambda p, i: (p, i)),
                   pl.BlockSpec((2, None, 1, T), lambda p, i: (p, i, 0, 0))],
        out_shape=[jax.ShapeDtypeStruct((S, W), F32), jax.ShapeDtypeStruct((S, W), BF16),
                   jax.ShapeDtypeStruct((W, S), BF16),
                   jax.ShapeDtypeStruct((2 * npairs, nq, 1, T), F32)],
        scratch_shapes=[pltpu.VMEM((2, 1, T), F32), pltpu.VMEM((2, LANES, T), F32)],
        compiler_params=_cparams(("parallel", "parallel")),
    )(proj, proj, vt, proj)


def _sb_bwd_t(proj, kt, do, tot, S, npairs, name):
    T = ATT_T
    nq = S // T
    nb = T // LANES
    scale = 64 ** -0.5

    def body(q_ref, k_ref, kt_ref, v_ref, do_ref, st_ref, dq_ref, dk_ref, dv_ref, dqt_ref, pre_ref, gpre_ref):
        i = pl.program_id(1)

        @pl.when(i == 0)
        def _():
            dk_ref[...] = jnp.zeros_like(dk_ref)
            dv_ref[...] = jnp.zeros_like(dv_ref)

        m0 = lax.broadcasted_iota(jnp.int32, (1, LANES), 1) < 64
        top = lax.broadcasted_iota(jnp.int32, (LANES, 1), 0) < 64
        key = lax.broadcasted_iota(jnp.int32, (T, LANES), 0)
        qrow = lax.broadcasted_iota(jnp.int32, (T, LANES), 1)
        r = lax.broadcasted_iota(jnp.int32, (T, T), 0)
        c = lax.broadcasted_iota(jnp.int32, (T, T), 1)
        upto = (c <= r).astype(BF16)
        left = (c < r).astype(BF16)
        qh = _head_q("sb", q_ref, m0, scale)
        dov = do_ref[...]
        doh = [jnp.where(m0, dov, 0.0).astype(BF16), jnp.where(m0, 0.0, dov).astype(BF16)]
        tot_h = [st_ref[0], st_ref[1]]
        dqt_ref[...] = jnp.zeros_like(dqt_ref)
        pre_ref[...] = jnp.zeros_like(pre_ref)
        gpre_ref[...] = jnp.zeros_like(gpre_ref)
        chains = [(h, b) for h in range(2) for b in range(nb)]

        def tiles(js, masked_at):
            starts = [pl.multiple_of(j * T, T) for j in js]
            zss, dwss = [], []
            for start in starts:
                vb = v_ref[pl.ds(start, T), :].astype(BF16)
                kb = k_ref[pl.ds(start, T), :].astype(BF16)
                zss.append(_split_blocks([_dot_nt(kb, qh[h]) for h in range(2)]))
                dwss.append(_split_blocks([_dot_nt(vb, doh[h]) for h in range(2)]))
            lass, sums, hiss, loss = [], [], [], []
            for zs, masked in zip(zss, _mask_flags(js, masked_at)):
                las, sm, his, los = [], [], [], []
                for (h, b), z in zip(chains, zs):
                    lk, la = _softplus_parts(z)
                    if masked:
                        lk = jnp.where(key < qrow + b * LANES, lk, 0.0)
                    hi, lo = _split2(lk)
                    las.append(la)
                    sm.append(jnp.sum(lk, axis=0, keepdims=True))
                    his.append(hi)
                    los.append(lo)
                lass.append(las)
                sums.append(sm)
                hiss.append(_join_blocks(his, nb))
                loss.append(_join_blocks(los, nb))
            pcss = [_split_blocks([_dot(upto, hi) + _dot(upto, lo) for hi, lo in zip(his, los)])
                    for his, los in zip(hiss, loss)]
            wss, gss = [], []
            for las, sm, pcs, dws, masked in zip(lass, sums, pcss, dwss, _mask_flags(js, masked_at)):
                ws, gs = [], []
                for (h, b), la, s, pc, dw in zip(chains, las, sm, pcs, dws):
                    lanes = slice(b * LANES, (b + 1) * LANES)
                    w = jnp.exp(la + ((tot_h[h][:, lanes] - pre_ref[h, :, lanes]) - pc))
                    if masked:
                        w = jnp.where(key < qrow + b * LANES, w, 0.0)
                    ws.append(w.astype(BF16))
                    gs.append(dw * w)
                    pre_ref[h, :, lanes] += s
                wss.append(_join_blocks(ws, nb))
                gss.append(gs)
            gcss = [_split_blocks([_dot(left, g) for g in _join_blocks([g.astype(BF16) for g in gs], nb)]) for gs in gss]
            dzss = []
            for las, gs, gcs, masked in zip(lass, gss, gcss, _mask_flags(js, masked_at)):
                dzs = []
                for (h, b), la, g, gc in zip(chains, las, gs, gcs):
                    lanes = slice(b * LANES, (b + 1) * LANES)
                    dz = g - (g + (gpre_ref[h, :, lanes] + gc)) * jnp.exp(la)
                    if masked:
                        dz = jnp.where(key < qrow + b * LANES, dz, 0.0)
                    dzs.append(dz.astype(BF16))
                    gpre_ref[h, :, lanes] += jnp.sum(g, axis=0, keepdims=True)
                dzss.append(_join_blocks(dzs, nb))
            for j, start, ws, dzs in zip(js, starts, wss, dzss):
                kt = kt_ref[j]
                dkc = dvc = None
                for h in range(2):
                    dkh = _dot(dzs[h], qh[h])
                    dvh = _dot(ws[h], doh[h])
                    dkc = dkh if dkc is None else dkc + dkh
                    dvc = dvh if dvc is None else dvc + dvh
                    dqt_ref[h] += _dot(kt, dzs[h])
                dk_ref[pl.ds(start, T), :] += dkc
                dv_ref[pl.ds(start, T), :] += dvc

        _loop_tiles(i, tiles, False)
        dq_ref[...] = jnp.where(top, dqt_ref[0], dqt_ref[1]).T * scale

    qs, ks, vs, _ = _att_specs("sb", S, T)
    W = npairs * LANES
    return pl.pallas_call(
        body, name=name, grid=(npairs, nq),
        in_specs=[qs, ks, pl.BlockSpec((None, nq, LANES, T), lambda p, i: (p, 0, 0, 0)), vs,
                  pl.BlockSpec((T, LANES), lambda p, i: (i, p)),
                  pl.BlockSpec((2, None, 1, T), lambda p, i: (p, i, 0, 0))],
        out_specs=[pl.BlockSpec((T, LANES), lambda p, i: (i, p)), pl.BlockSpec((S, LANES), lambda p, i: (0, p)),
                   pl.BlockSpec((S, LANES), lambda p, i: (0, p))],
        out_shape=[jax.ShapeDtypeStruct((S, W), F32)] * 3,
        scratch_shapes=[pltpu.VMEM((2, LANES, T), F32), pltpu.VMEM((2, 1, T), F32), pltpu.VMEM((2, 1, T), F32)],
        compiler_params=_cparams(("parallel", "arbitrary")),
    )(proj, proj, kt, proj, do, tot)


def _pad_w0(w):
    z = lambda n: jnp.zeros((w.shape[0], n), w.dtype)
    return jnp.concatenate([w[:, 2048:2432], w[:, 2432:2688], z(64), w[:, 2688:2720], z(32),
                            w[:, 1536:2048], w[:, 2720:3232], w[:, 0:512], w[:, 512:1024], w[:, 1024:1536]], axis=1)


def _unpad_w0(wp):
    return jnp.concatenate([wp[:, L0_SBQ:L0_SBQ + 512], wp[:, L0_SBK:L0_SBK + 512], wp[:, L0_SBV:L0_SBV + 512],
                            wp[:, L0_SBG:L0_SBG + 512], wp[:, 0:384], wp[:, 384:640], wp[:, 704:736],
                            wp[:, L0_MLG:L0_MLG + 512]], axis=1)


def _pad_wq(w):
    return jnp.pad(w.reshape(384, 8, 96), ((0, 0), (0, 0), (0, 32))).reshape(384, 1024)


def _unpad_wq(wp):
    return wp.reshape(384, 8, 128)[:, :, :96].reshape(384, 768)


def _pad_wkv(w):
    w3 = w.reshape(256, 8, 128)
    k = jnp.pad(w3[:, :, :64], ((0, 0), (0, 0), (0, 64))).reshape(256, 1024)
    return jnp.concatenate([k, w3[:, :, 64:].reshape(256, 512)], axis=1)


def _unpad_wkv(wp):
    k = wp[:, :1024].reshape(256, 8, 128)[:, :, :64]
    v = wp[:, 1024:].reshape(256, 8, 64)
    return jnp.concatenate([k, v], axis=-1).reshape(256, 1024)


def _pad_w1(w):
    return jnp.concatenate([w, jnp.zeros((w.shape[0], L1_WIDTH - ODD_IN_WIDTH), w.dtype)], axis=1)


def _local_step(x, positions, target, g, w0p, wqp, wkvp, wo0, w1p, wo1):
    S = x.shape[0]
    nq = S // ATT_T
    invf = ROPE_THETA ** (-jnp.arange(0, MLA_ROPE_DIM, 2, dtype=F32) / MLA_ROPE_DIM)
    invf = jnp.concatenate([jnp.zeros((64,), F32), invf, invf, jnp.zeros((32,), F32)]).reshape(1, LANES)
    cosT, s1T, s2T = _rope_tables(positions.reshape(S, 1), invf, "rope_tables")
    bfp = jnp.pad(g["l1_b_f"], ((0, 0), (0, LANES - FOX_HEADS)))

    proj0, h0t = _norm_matmul(x, g["l0_pre_g"], w0p, "l0_in_proj")
    qm, km, vm, qnt, cnt = _mla_prep(proj0, g["l0_q_a_g"], g["l0_kv_a_g"], wqp, wkvp, cosT, s1T, s2T, "mla_prep")
    sb_vt = _transpose_tiles(proj0, L0_SBV, 4, LANES, 2, "sb_vt")
    sb_kt = _transpose_tiles(proj0, L0_SBK, 4, LANES, 2, "sb_kt")
    o_sb, og_sb, ogt_sb, tot_sb = _sb_fwd_t(proj0, sb_vt, S, 4, "sb_fwd")
    vmt = _transpose_tiles(vm, 0, 4, LANES, 4, "mla_vt")
    kmt = _transpose_tiles(km, 0, 4, 2 * LANES, 4, "mla_kt")
    o_ml, og_ml, ogt_ml, lse_ml = _softmax_fwd("mla", (qm, km, vmt, proj0), None, S, 4, "mla_fwd")
    y0, x1 = _out_proj(og_sb, og_ml, 0, 0, wo0, x, g["l0_post_g"], None, "l0_out_proj")

    proj1, h1t = _norm_matmul(x1, g["l1_pre_g"], w1p, "l1_in_proj")
    cfx = _fox_prep(proj1, bfp, "fox_prep")
    c16 = cfx[:, :FOX_HEADS].T
    c_col = jnp.broadcast_to(c16[:, :, None], (FOX_HEADS, S, LANES))
    vt1 = _transpose_tiles(proj1, L1_V, 8, LANES, 8, "fox_vt")
    kt1 = _transpose_tiles(proj1, L1_K, 8, LANES, 8, "fox_kt")
    o_fx, og_fx, ogt_fx, lse_fx = _softmax_fwd("fox", (proj1, proj1, vt1, proj1), c_col, S, 8, "fox_fwd")
    y1, dx2, lsum = _out_proj(og_fx, og_fx, 0, 1, wo1, x1, g["l1_post_g"], target, "l1_out_proj")

    dy1, do1, dgate1, d_post1 = _out_proj_bwd(dx2, y1, g["l1_post_g"], wo1, proj1, (L1_G, L1_G + 512), o_fx, o_fx, 0, 1, "l1_out_bwd")
    dwo1 = _matmul_t(ogt_fx, dy1, "l1_dw_out")
    dq1, dk1, dv1, dck, dcq = _softmax_bwd_t("fox", proj1, proj1, kt1, proj1, do1, 0, o_fx, lse_fx, c_col, S, 8,
                                             "fox_bwd")
    dc = jnp.pad((dcq.reshape(FOX_HEADS, S) - dck.reshape(FOX_HEADS, S)).T, ((0, 0), (0, LANES - FOX_HEADS)))
    df, d_bf = _fox_prep_bwd(dc, proj1, bfp, "fox_prep_bwd")
    pieces1 = [(L1_Q, dq1), (L1_K, dk1), (L1_V, dv1), (L1_G, dgate1), (L1_F, df)]
    dx1, d_pre1 = _in_proj_bwd(pieces1, w1p, x1, g["l1_pre_g"], dx2, "l1_in_bwd")
    dw1p = jnp.concatenate([_matmul_t(h1t, a, "l1_dw_in_%d" % k) for k, (_, a) in enumerate(pieces1)], axis=1)

    dy0, do0, dgate0, d_post0 = _out_proj_bwd(dx1, y0, g["l0_post_g"], wo0, proj0, (L0_SBG, L0_MLG), o_sb, o_ml, 0, 0,
                                              "l0_out_bwd")
    dwo0 = jnp.concatenate([_matmul_t(ogt_sb, dy0, "l0_dw_out_sb"), _matmul_t(ogt_ml, dy0, "l0_dw_out_mla")], axis=0)
    dsq, dsk, dsv = _sb_bwd_t(proj0, sb_kt, do0, tot_sb, S, 4, "sb_bwd")
    dqm, dkm, dvm = _softmax_bwd_t("mla", qm, km, kmt, vm, do0, 4, o_ml, lse_ml, None, S, 4, "mla_bwd")
    dprep, dqb, dkvb, d_qag, d_kvag = _mla_prep_bwd(dqm, dkm, dvm, proj0, g["l0_q_a_g"], g["l0_kv_a_g"], wqp, wkvp,
                                                    cosT, s1T, s2T, "mla_prep_bwd")
    dwqp = _matmul_t(qnt, dqb, "l0_dw_qb")
    dwkvp = _matmul_t(cnt, dkvb, "l0_dw_kvb")
    pieces0 = [(L0_PREP, dprep), (L0_SBG, dgate0), (L0_SBQ, dsq), (L0_SBK, dsk), (L0_SBV, dsv)]
    dx0, d_pre0 = _in_proj_bwd(pieces0, w0p, x, g["l0_pre_g"], dx1, "l0_in_bwd")
    dw0p = jnp.concatenate([_matmul_t(h0t, a, "l0_dw_in_%d" % k) for k, (_, a) in enumerate(pieces0)], axis=1)

    grads = {
        "l0_pre_g": d_pre0, "l0_post_g": d_post0, "l0_w_in": dw0p, "l0_q_a_g": d_qag, "l0_w_q_b": dwqp,
        "l0_kv_a_g": d_kvag, "l0_w_kv_b": dwkvp, "l0_w_out": dwo0, "l1_pre_g": d_pre1, "l1_post_g": d_post1,
        "l1_w_in": dw1p, "l1_b_f": d_bf[:, :FOX_HEADS], "l1_w_out": dwo1,
    }
    return lsum, dx0, grads


_ANY = pl.BlockSpec(memory_space=pl.ANY)


def _place():
    return lax.axis_index("x"), lax.axis_index("y"), lax.axis_index("c")


def _other_chips(x, y):
    return [(1 - x, y), (x, 1 - y), (1 - x, 1 - y)]


def _half(c):
    return pl.ds(c * PACK_HALF, PACK_HALF)


def _weight_gather(pack):
    def body(p_ref, out_ref, send_sems, recv_sems):
        x, y, c = _place()
        sibling = (x, y, 1 - c)
        chips = _other_chips(x, y)

        def blk(chip, cc):
            return out_ref.at[2 * chip[0] + chip[1], _half(cc)]

        def copy(k, src, dst, to):
            return pltpu.make_async_remote_copy(src_ref=src, dst_ref=dst, send_sem=send_sems.at[k],
                                                recv_sem=recv_sems.at[k], device_id=to, device_id_type=MESH)

        first = [copy(j, p_ref.at[_half(c)], blk((x, y), c), (*chip, c)) for j, chip in enumerate(chips)]
        for cp in first:
            cp.start()
        passed = [copy(3 + j, blk(chip, c), blk(chip, c), sibling) for j, chip in enumerate(chips)]
        for j, chip in enumerate(chips):
            copy(j, blk(chip, c), blk(chip, c), (x, y, c)).wait_recv()
            passed[j].start()
        for j, chip in enumerate(chips):
            copy(3 + j, blk(chip, 1 - c), blk(chip, 1 - c), (x, y, c)).wait_recv()
        for cp in first + passed:
            cp.wait_send()

    return pl.pallas_call(
        body, name="weight_gather", in_specs=[_ANY], out_specs=_ANY,
        out_shape=jax.ShapeDtypeStruct((4,) + pack.shape, pack.dtype),
        scratch_shapes=[pltpu.SemaphoreType.DMA((6,)), pltpu.SemaphoreType.DMA((6,))],
    )(pack)


GRAD_TR = 2048


def _grad_core_exchange(p):
    def body(p_ref, recv_ref, send_sems, recv_sems):
        x, y, c = _place()
        give = [pltpu.make_async_remote_copy(src_ref=p_ref.at[j, _half(1 - c)], dst_ref=recv_ref.at[j],
                                             send_sem=send_sems.at[j], recv_sem=recv_sems.at[j],
                                             device_id=(x, y, 1 - c), device_id_type=MESH) for j in range(4)]
        for cp in give:
            cp.start()
        for cp in give:
            cp.wait()

    return pl.pallas_call(
        body, name="grad_core_exchange", in_specs=[_ANY], out_specs=_ANY,
        out_shape=jax.ShapeDtypeStruct((4, PACK_HALF, LANES), p.dtype),
        scratch_shapes=[pltpu.SemaphoreType.DMA((4,)), pltpu.SemaphoreType.DMA((4,))],
    )(p)


def _grad_add_cores(p, theirs, c1):
    tr = GRAD_TR

    def body(c_ref, a_ref, b_ref, o_ref):
        o_ref[...] = (a_ref[...] + b_ref[...]).astype(BF16)

    spec = pl.BlockSpec((None, tr, LANES), lambda j, r, c: (j, r, 0))
    grid_spec = pltpu.PrefetchScalarGridSpec(
        num_scalar_prefetch=1, grid=(4, PACK_HALF // tr),
        in_specs=[pl.BlockSpec((None, None, tr, LANES), lambda j, r, c: (j, c[0], r, 0)), spec], out_specs=spec)
    return pl.pallas_call(
        body, name="grad_add_cores", grid_spec=grid_spec, out_shape=jax.ShapeDtypeStruct(theirs.shape, BF16),
        compiler_params=_cparams(("parallel", "parallel")),
    )(c1, p.reshape(4, 2, PACK_HALF, LANES), theirs)


def _grad_chip_exchange(q):
    def body(q_ref, out_ref, send_sems, recv_sems):
        x, y, c = _place()
        me = 2 * x + y
        chips = _other_chips(x, y)
        sends = [pltpu.make_async_remote_copy(src_ref=q_ref.at[2 * chip[0] + chip[1]], dst_ref=out_ref.at[me],
                                              send_sem=send_sems.at[j], recv_sem=recv_sems.at[j],
                                              device_id=(*chip, c), device_id_type=MESH) for j, chip in enumerate(chips)]
        for cp in sends:
            cp.start()
        for j, chip in enumerate(chips):
            slot = out_ref.at[2 * chip[0] + chip[1]]
            pltpu.make_async_remote_copy(src_ref=slot, dst_ref=slot, send_sem=send_sems.at[j], recv_sem=recv_sems.at[j],
                                         device_id=(x, y, c), device_id_type=MESH).wait_recv()
        for cp in sends:
            cp.wait_send()

    return pl.pallas_call(
        body, name="grad_chip_exchange", in_specs=[_ANY], out_specs=_ANY,
        out_shape=jax.ShapeDtypeStruct(q.shape, q.dtype),
        scratch_shapes=[pltpu.SemaphoreType.DMA((3,)), pltpu.SemaphoreType.DMA((3,))],
    )(q)


def _grad_add_chips(q, slots, me1):
    tr = GRAD_TR

    def body(me_ref, own_ref, s0, s1, s2, s3, o_ref):
        me = me_ref[0]
        t = [jnp.where(me == j, own_ref[...], s[...]).astype(F32) for j, s in enumerate((s0, s1, s2, s3))]
        o_ref[...] = ((t[0] + t[1]) + t[2]) + t[3]

    def slot_spec(j):
        return pl.BlockSpec((None, tr, LANES), lambda r, me: (jnp.where(me[0] == j, (j + 1) % 4, j), r, 0))

    grid_spec = pltpu.PrefetchScalarGridSpec(
        num_scalar_prefetch=1, grid=(PACK_HALF // tr,),
        in_specs=[pl.BlockSpec((None, tr, LANES), lambda r, me: (me[0], r, 0))] + [slot_spec(j) for j in range(4)],
        out_specs=pl.BlockSpec((tr, LANES), lambda r, me: (r, 0)))
    return pl.pallas_call(
        body, name="grad_add_chips", grid_spec=grid_spec, out_shape=jax.ShapeDtypeStruct(q.shape[1:], F32),
        compiler_params=_cparams(("parallel",)),
    )(me1, q, slots, slots, slots, slots)


def _grad_core_gather(t):
    def body(t_ref, out_ref, send_sem, recv_sem):
        x, y, c = _place()
        give = pltpu.make_async_remote_copy(src_ref=t_ref, dst_ref=out_ref, send_sem=send_sem, recv_sem=recv_sem,
                                            device_id=(x, y, 1 - c), device_id_type=MESH)
        give.start()
        give.wait()

    return pl.pallas_call(
        body, name="grad_core_gather", in_specs=[_ANY], out_specs=_ANY,
        out_shape=jax.ShapeDtypeStruct(t.shape, t.dtype),
        scratch_shapes=[pltpu.SemaphoreType.DMA, pltpu.SemaphoreType.DMA],
    )(t)


def _small_allreduce(sp):
    def body(sp_ref, out_ref, gath_ref, send_sems, recv_sems):
        x, y, c = _place()
        me = 4 * x + 2 * y + c
        gath_ref[me] = sp_ref[...]
        peers = []
        for k in range(1, 8):
            px = 1 - x if k & 4 else x
            py = 1 - y if k & 2 else y
            pc = 1 - c if k & 1 else c
            peers.append((px, py, pc))
        sends = [pltpu.make_async_remote_copy(src_ref=sp_ref, dst_ref=gath_ref.at[me], send_sem=send_sems.at[k],
                                              recv_sem=recv_sems.at[k], device_id=peer, device_id_type=MESH)
                 for k, peer in enumerate(peers)]
        for cp in sends:
            cp.start()
        for k, (px, py, pc) in enumerate(peers):
            slot = gath_ref.at[4 * px + 2 * py + pc]
            pltpu.make_async_remote_copy(src_ref=slot, dst_ref=slot, send_sem=send_sems.at[k], recv_sem=recv_sems.at[k],
                                         device_id=(x, y, c), device_id_type=MESH).wait_recv()
        for cp in sends:
            cp.wait_send()
        tot = gath_ref[0]
        for d in range(1, 8):
            tot = tot + gath_ref[d]
        out_ref[...] = tot

    vm = pl.BlockSpec(memory_space=pltpu.VMEM)
    return pl.pallas_call(
        body, name="small_allreduce", in_specs=[vm], out_specs=vm, out_shape=jax.ShapeDtypeStruct(sp.shape, sp.dtype),
        scratch_shapes=[pltpu.VMEM((8,) + sp.shape, sp.dtype), pltpu.SemaphoreType.DMA((7,)), pltpu.SemaphoreType.DMA((7,))],
    )(sp)


def _adamw_update(w, gv, m, v):
    mn = ADAM_B1 * m + (1.0 - ADAM_B1) * gv
    vn = ADAM_B2 * v + (1.0 - ADAM_B2) * (gv * gv)
    m_hat = mn / (1.0 - ADAM_B1 ** ADAM_STEP)
    v_hat = vn / (1.0 - ADAM_B2 ** ADAM_STEP)
    return -ADAM_LR * (m_hat / (jnp.sqrt(v_hat) + ADAM_EPS) + ADAM_WD * w), mn, vn


def _adamw(w, g, m, v, name):
    rows, cols = w.shape
    tr = _pick(rows, (256, rows))

    def body(w_ref, g_ref, m_ref, v_ref, d_ref, mo_ref, vo_ref):
        d_ref[...], mo_ref[...], vo_ref[...] = _adamw_update(w_ref[...], g_ref[...], m_ref[...], v_ref[...])

    spec = pl.BlockSpec((tr, cols), lambda r: (r, 0))
    shp = jax.ShapeDtypeStruct(w.shape, F32)
    return pl.pallas_call(
        body, name=name, grid=(rows // tr,), in_specs=[spec] * 4, out_specs=[spec] * 3, out_shape=[shp] * 3,
        compiler_params=_cparams(("parallel",)),
    )(w, g, m, v)


def _adamw_mats(w, g_mine, g_theirs, m, v, c1):
    tr = GRAD_TR
    nb = PACK_HALF // tr

    def body(c_ref, w_ref, a_ref, b_ref, m_ref, v_ref, g_ref, d_ref, mo_ref, vo_ref):
        gv = jnp.where(pl.program_id(0) == c_ref[0], a_ref[...], b_ref[...])
        g_ref[...] = gv
        d_ref[...], mo_ref[...], vo_ref[...] = _adamw_update(w_ref[...], gv, m_ref[...], v_ref[...])

    full = pl.BlockSpec((tr, LANES), lambda h, r, c: (h * nb + r, 0))
    half = pl.BlockSpec((tr, LANES), lambda h, r, c: (r, 0))
    grid_spec = pltpu.PrefetchScalarGridSpec(num_scalar_prefetch=1, grid=(2, nb),
                                             in_specs=[full, half, half, full, full], out_specs=[full] * 4)
    shp = jax.ShapeDtypeStruct(w.shape, F32)
    return pl.pallas_call(
        body, name="adamw_mats", grid_spec=grid_spec, out_shape=[shp] * 4,
        compiler_params=_cparams(("parallel", "parallel")),
    )(c1, w, g_mine, g_theirs, m, v)


MAT_NAMES = ("l0_w_in", "l0_w_q_b", "l0_w_kv_b", "l0_w_out", "l1_w_in", "l1_w_out")
VEC_NAMES = ("l0_pre_g", "l0_post_g", "l0_q_a_g", "l0_kv_a_g", "l1_pre_g", "l1_post_g", "l1_b_f")
WEIGHT_NAMES = ("l0_pre_g", "l0_post_g", "l0_w_in", "l0_q_a_g", "l0_w_q_b", "l0_kv_a_g", "l0_w_kv_b", "l0_w_out",
                "l1_pre_g", "l1_post_g", "l1_w_in", "l1_b_f", "l1_w_out")
MAT_SHARD = {"l0_w_in": (1024, 808), "l0_w_q_b": (384, 192), "l0_w_kv_b": (256, 256), "l0_w_out": (256, 1024),
             "l1_w_in": (1024, 1028), "l1_w_out": (256, 1024)}
ROW_SHARDED = ("l0_w_out", "l1_w_out")
VEC_LEN = {"l0_pre_g": 1024, "l0_post_g": 1024, "l0_q_a_g": 384, "l0_kv_a_g": 256, "l1_pre_g": 1024,
           "l1_post_g": 1024, "l1_b_f": 16}


def _mat_rows(n):
    r, c = MAT_SHARD[n]
    return r * c // LANES


def _pack_shards(shards):
    parts = [shards[n].reshape(_mat_rows(n), LANES) for n in MAT_NAMES]
    used = sum(_mat_rows(n) for n in MAT_NAMES)
    parts.append(jnp.zeros((PACK_ROWS - used, LANES), parts[0].dtype))
    return jnp.concatenate(parts, axis=0)


def _unpack_shards(pack):
    out, at = {}, 0
    for n in MAT_NAMES:
        out[n] = pack[..., at:at + _mat_rows(n), :].reshape(pack.shape[:-2] + MAT_SHARD[n])
        at += _mat_rows(n)
    return out


def _join_shards(n, s):
    if n in ROW_SHARDED:
        return s.reshape(4 * s.shape[1], s.shape[2])
    return s.transpose(1, 0, 2).reshape(s.shape[1], 4 * s.shape[2])


def _cut_shards(n, w):
    r, c = MAT_SHARD[n]
    if n in ROW_SHARDED:
        return w.reshape(4, r, c)
    return w.reshape(r, 4, c).transpose(1, 0, 2)


def _pack_vecs(vecs):
    parts = []
    for n in VEC_NAMES:
        v = vecs[n].reshape(-1)
        parts.append(jnp.pad(v, (0, VEC_ROWS * LANES - v.shape[0])).reshape(VEC_ROWS, LANES))
    return jnp.concatenate(parts, axis=0)


def _unpack_vecs(pack):
    return {n: pack[k * VEC_ROWS:(k + 1) * VEC_ROWS].reshape(-1)[:VEC_LEN[n]] for k, n in enumerate(VEC_NAMES)}


def kernel(x, positions, l0_pre_g, l0_post_g, l0_w_in, l0_q_a_g, l0_w_q_b, l0_kv_a_g, l0_w_kv_b, l0_w_out, l1_pre_g, l1_post_g, l1_w_in, l1_b_f, l1_w_out, loss_target, m_l0_pre_g, m_l0_post_g, m_l0_w_in, m_l0_q_a_g, m_l0_w_q_b, m_l0_kv_a_g, m_l0_w_kv_b, m_l0_w_out, m_l1_pre_g, m_l1_post_g, m_l1_w_in, m_l1_b_f, m_l1_w_out, v_l0_pre_g, v_l0_post_g, v_l0_w_in, v_l0_q_a_g, v_l0_w_q_b, v_l0_kv_a_g, v_l0_w_kv_b, v_l0_w_out, v_l1_pre_g, v_l1_post_g, v_l1_w_in, v_l1_b_f, v_l1_w_out):
    w = dict(l0_pre_g=l0_pre_g, l0_post_g=l0_post_g, l0_w_in=l0_w_in, l0_q_a_g=l0_q_a_g, l0_w_q_b=l0_w_q_b,
             l0_kv_a_g=l0_kv_a_g, l0_w_kv_b=l0_w_kv_b, l0_w_out=l0_w_out, l1_pre_g=l1_pre_g, l1_post_g=l1_post_g,
             l1_w_in=l1_w_in, l1_b_f=l1_b_f, l1_w_out=l1_w_out)
    m = dict(l0_pre_g=m_l0_pre_g, l0_post_g=m_l0_post_g, l0_w_in=m_l0_w_in, l0_q_a_g=m_l0_q_a_g, l0_w_q_b=m_l0_w_q_b,
             l0_kv_a_g=m_l0_kv_a_g, l0_w_kv_b=m_l0_w_kv_b, l0_w_out=m_l0_w_out, l1_pre_g=m_l1_pre_g,
             l1_post_g=m_l1_post_g, l1_w_in=m_l1_w_in, l1_b_f=m_l1_b_f, l1_w_out=m_l1_w_out)
    v = dict(l0_pre_g=v_l0_pre_g, l0_post_g=v_l0_post_g, l0_w_in=v_l0_w_in, l0_q_a_g=v_l0_q_a_g, l0_w_q_b=v_l0_w_q_b,
             l0_kv_a_g=v_l0_kv_a_g, l0_w_kv_b=v_l0_w_kv_b, l0_w_out=v_l0_w_out, l1_pre_g=v_l1_pre_g,
             l1_post_g=v_l1_post_g, l1_w_in=v_l1_w_in, l1_b_f=v_l1_b_f, l1_w_out=v_l1_w_out)

    cx, cy, cc = _place()
    me1 = jnp.reshape(2 * cx + cy, (1,)).astype(jnp.int32)
    c1 = jnp.reshape(cc, (1,)).astype(jnp.int32)
    w_bf = _pack_shards({n: w[n].astype(BF16) for n in MAT_NAMES})
    gathered = lax.dynamic_update_slice(_weight_gather(w_bf), w_bf[None], (2 * cx + cy, 0, 0))
    gathered = _unpack_shards(gathered)
    full = {n: _join_shards(n, gathered[n]) for n in MAT_NAMES}
    gains = {n: w[n].reshape(1, -1) for n in VEC_NAMES}

    lsum, dx0, grads = _local_step(
        x[0], positions[0], loss_target[0], gains, _pad_w0(full["l0_w_in"]), _pad_wq(full["l0_w_q_b"]),
        _pad_wkv(full["l0_w_kv_b"]), full["l0_w_out"], _pad_w1(full["l1_w_in"]), full["l1_w_out"])

    gfull = {"l0_w_in": _unpad_w0(grads["l0_w_in"]), "l0_w_q_b": _unpad_wq(grads["l0_w_q_b"]),
             "l0_w_kv_b": _unpad_wkv(grads["l0_w_kv_b"]), "l0_w_out": grads["l0_w_out"],
             "l1_w_in": grads["l1_w_in"][:, :ODD_IN_WIDTH], "l1_w_out": grads["l1_w_out"]}
    parts = [_cut_shards(n, gfull[n]).reshape(4, _mat_rows(n), LANES) for n in MAT_NAMES]
    used = sum(_mat_rows(n) for n in MAT_NAMES)
    parts.append(jnp.zeros((4, PACK_ROWS - used, LANES), F32))
    g_pack = jnp.concatenate(parts, axis=1)
    q_cores = _grad_add_cores(g_pack, _grad_core_exchange(g_pack), c1)
    g_mine = _grad_add_chips(q_cores, _grad_chip_exchange(q_cores), me1)
    g_theirs = _grad_core_gather(g_mine)

    small = _small_allreduce(jnp.concatenate([_pack_vecs({n: grads[n] for n in VEC_NAMES}),
                                              lsum.reshape(D_MODEL // LANES, LANES)], axis=0))
    g_small = small[:SMALL_ROWS]
    loss = 0.5 * jnp.sum(small[SMALL_ROWS:]) / float(D_MODEL)

    first = lax.select(cc == 0, g_mine, g_theirs)
    second = lax.select(cc == 0, g_theirs, g_mine)
    g_mats = _unpack_shards(jnp.concatenate([first, second], axis=0))
    d_mats, m_mats, v_mats = {}, {}, {}
    for n in MAT_NAMES:
        d_mats[n], m_mats[n], v_mats[n] = _adamw(w[n], g_mats[n], m[n], v[n], "adamw_" + n)
    d_small, m_small, v_small = _adamw(_pack_vecs(w), g_small, _pack_vecs(m), _pack_vecs(v), "adamw_vecs")

    def leaves(mats, vec_pack):
        out = dict(mats)
        out.update(_unpack_vecs(vec_pack))
        return [out[n] for n in WEIGHT_NAMES]

    return (loss, dx0[None], *leaves(g_mats, g_small), *leaves(d_mats, d_small), *leaves(m_mats, m_small),
            *leaves(v_mats, v_small))
```

```python
import jax
import jax.numpy as jnp
from jax import lax
from jax.experimental import pallas as pl
from jax.experimental.pallas import tpu as pltpu

F32 = jnp.float32
BF16 = jnp.bfloat16
MESH = pl.DeviceIdType.MESH

D_MODEL = 1024
RMS_EPS = 1e-6
ROPE_THETA = 10000.0
SB_WIDTH = 512
MLA_Q_LORA = 384
MLA_KV_LORA = 256
MLA_ROPE_DIM = 32
MLA_WIDTH = 512
FOX_WIDTH = 1024
FOX_HEADS = 16
EVEN_IN_WIDTH = 3232
ODD_IN_WIDTH = 4112

ADAM_LR = 0.001
ADAM_B1 = 0.9
ADAM_B2 = 0.999
ADAM_EPS = 1e-08
ADAM_WD = 0.01
ADAM_STEP = 10

LANES = 128
VMEM_LIMIT = 56 * 1024 * 1024

L0_PREP = 0
L0_PREP_W = 768
L0_SBG = 768
L0_MLG = 1280
L0_SBQ = 1792
L0_SBK = 2304
L0_SBV = 2816
L0_WIDTH = 3328
L1_Q = 0
L1_K = 1024
L1_V = 2048
L1_G = 3072
L1_F = 4096
L1_WIDTH = 4224

ATT_T = 256
ATT_GROUP = 4
NEG = -1e30

PACK_ROWS = 20480
PACK_HALF = PACK_ROWS // 2
VEC_ROWS = 8
SMALL_ROWS = 7 * VEC_ROWS


def _cparams(sem, **kw):
    return pltpu.CompilerParams(dimension_semantics=sem, vmem_limit_bytes=VMEM_LIMIT, **kw)


def _dot(a, b):
    return lax.dot_general(a, b, (((1,), (0,)), ((), ())), preferred_element_type=F32)


def _dot_nt(a, b):
    return lax.dot_general(a, b, (((1,), (1,)), ((), ())), preferred_element_type=F32)


def _sigmoid(x):
    return 1.0 / (1.0 + jnp.exp(-x))


def _rstd(x):
    return lax.rsqrt(jnp.mean(x * x, axis=-1, keepdims=True) + RMS_EPS)


def _norm_bwd(x, g, dy):
    r = _rstd(x)
    xn = x * r
    dxn = dy * g
    dx = r * (dxn - xn * jnp.mean(dxn * xn, axis=-1, keepdims=True))
    return dx, dy * xn


def _split3(x):
    hi = x.astype(BF16)
    r1 = x - hi.astype(F32)
    mid = r1.astype(BF16)
    lo = (r1 - mid.astype(F32)).astype(BF16)
    return hi, mid, lo


def _wide_tile(n, cap=1792):
    return max(t for t in range(LANES, min(n, cap) + 1, LANES) if n % t == 0)


def _pick(n, cands):
    for c in cands:
        if n % c == 0:
            return c
    raise ValueError(n)


def _norm_matmul(x, g, w, name):
    S, K = x.shape
    N = w.shape[1]
    tm = _pick(S, (512, 256))
    tn = _wide_tile(N)

    def body(x_ref, g_ref, w_ref, o_ref, ht_ref, h_ref):
        @pl.when(pl.program_id(1) == 0)
        def _():
            xv = x_ref[...]
            h = (xv * _rstd(xv)) * g_ref[...]
            h_ref[...] = h.astype(BF16)
            ht_ref[...] = h.T.astype(BF16)
        o_ref[...] = _dot(h_ref[...], w_ref[...])

    return pl.pallas_call(
        body, name=name, grid=(S // tm, N // tn),
        in_specs=[pl.BlockSpec((tm, K), lambda i, j: (i, 0)),
                  pl.BlockSpec((1, K), lambda i, j: (0, 0)),
                  pl.BlockSpec((K, tn), lambda i, j: (0, j))],
        out_specs=[pl.BlockSpec((tm, tn), lambda i, j: (i, j)),
                   pl.BlockSpec((K, tm), lambda i, j: (0, i))],
        out_shape=[jax.ShapeDtypeStruct((S, N), F32), jax.ShapeDtypeStruct((K, S), BF16)],
        scratch_shapes=[pltpu.VMEM((tm, K), BF16)],
        compiler_params=_cparams(("parallel", "arbitrary")),
    )(x, g, w)


def _matmul_t(at, b, name):
    M, S = at.shape
    N = b.shape[1]
    tn = _wide_tile(N)
    ts = _pick(S, (512, 256))

    def body(a_ref, b_ref, o_ref):
        @pl.when(pl.program_id(1) == 0)
        def _():
            o_ref[...] = jnp.zeros_like(o_ref)
        o_ref[...] += _dot(a_ref[...], b_ref[...].astype(BF16))

    return pl.pallas_call(
        body, name=name, grid=(N // tn, S // ts),
        in_specs=[pl.BlockSpec((M, ts), lambda j, k: (0, k)),
                  pl.BlockSpec((ts, tn), lambda j, k: (k, j))],
        out_specs=pl.BlockSpec((M, tn), lambda j, k: (0, j)),
        out_shape=jax.ShapeDtypeStruct((M, N), F32),
        compiler_params=_cparams(("parallel", "arbitrary")),
    )(at, b)


def _in_proj_bwd(pieces, w, x, g, dx_up, name):
    S, K = x.shape
    N = w.shape[1]
    tm = _pick(S, (256,))
    offs = [off for off, _ in pieces]
    arrs = [a for _, a in pieces]

    def body(*refs):
        d_refs = refs[:len(arrs)]
        w_ref, x_ref, g_ref, u_ref, dx_ref, dg_ref = refs[len(arrs):]

        @pl.when(pl.program_id(0) == 0)
        def _():
            dg_ref[...] = jnp.zeros_like(dg_ref)

        acc = None
        for off, d_ref in zip(offs, d_refs):
            part = _dot_nt(d_ref[...].astype(BF16), w_ref[:, off:off + d_ref.shape[1]])
            acc = part if acc is None else acc + part
        dx, dgrow = _norm_bwd(x_ref[...], g_ref[...], acc)
        dx_ref[...] = u_ref[...] + dx
        dg_ref[...] += jnp.sum(dgrow, axis=0, keepdims=True)

    row = lambda i: (i, 0)
    fixed = lambda i: (0, 0)
    return pl.pallas_call(
        body, name=name, grid=(S // tm,),
        in_specs=[pl.BlockSpec((tm, a.shape[1]), row) for a in arrs] + [
            pl.BlockSpec((K, N), fixed), pl.BlockSpec((tm, K), row), pl.BlockSpec((1, K), fixed),
            pl.BlockSpec((tm, K), row)],
        out_specs=[pl.BlockSpec((tm, K), row), pl.BlockSpec((1, K), fixed)],
        out_shape=[jax.ShapeDtypeStruct((S, K), F32), jax.ShapeDtypeStruct((1, K), F32)],
        compiler_params=_cparams(("arbitrary",)),
    )(*arrs, w, x, g, dx_up)


def _out_proj(og_a, og_b, blk_a, blk_b, w, x, g, target, name):
    S = x.shape[0]
    D = x.shape[1]
    tm = _pick(S, (512, 256))
    with_loss = target is not None

    def body(*refs):
        if with_loss:
            a_ref, b_ref, wa_ref, wb_ref, x_ref, g_ref, t_ref, y_ref, o_ref, l_ref = refs
        else:
            a_ref, b_ref, wa_ref, wb_ref, x_ref, g_ref, y_ref, o_ref = refs
        y = _dot(a_ref[...], wa_ref[...]) + _dot(b_ref[...], wb_ref[...])
        y_ref[...] = y
        xn = x_ref[...] + (y * _rstd(y)) * g_ref[...]
        if with_loss:
            @pl.when(pl.program_id(0) == 0)
            def _():
                l_ref[...] = jnp.zeros_like(l_ref)
            d = xn - t_ref[...]
            o_ref[...] = d / float(D)
            l_ref[...] += jnp.sum(d * d, axis=0, keepdims=True)
        else:
            o_ref[...] = xn

    row = lambda i: (i, 0)
    in_specs = [pl.BlockSpec((tm, 512), lambda i: (i, blk_a)),
                pl.BlockSpec((tm, 512), lambda i: (i, blk_b)),
                pl.BlockSpec((512, D), lambda i: (0, 0)),
                pl.BlockSpec((512, D), lambda i: (1, 0)),
                pl.BlockSpec((tm, D), row),
                pl.BlockSpec((1, D), lambda i: (0, 0))]
    out_specs = [pl.BlockSpec((tm, D), row), pl.BlockSpec((tm, D), row)]
    out_shape = [jax.ShapeDtypeStruct((S, D), F32), jax.ShapeDtypeStruct((S, D), F32)]
    args = [og_a, og_b, w, w, x, g]
    if with_loss:
        in_specs.append(pl.BlockSpec((tm, D), row))
        out_specs.append(pl.BlockSpec((1, D), lambda i: (0, 0)))
        out_shape.append(jax.ShapeDtypeStruct((1, D), F32))
        args.append(target)
    return pl.pallas_call(
        body, name=name, grid=(S // tm,), in_specs=in_specs, out_specs=out_specs, out_shape=out_shape,
        compiler_params=_cparams(("arbitrary",)),
    )(*args)


def _out_proj_bwd(dx_up, y, g, w, proj, gate_offs, o_a, o_b, oblk_a, oblk_b, name):
    S, D = y.shape
    tm = _pick(S, (256,))
    gblk = [off // 256 + c for off in gate_offs for c in range(2)]

    def body(u_ref, y_ref, g_ref, w_ref, g0, g1, g2, g3, oa_ref, ob_ref, dy_ref, do_ref, dgate_ref, dg_ref):
        @pl.when(pl.program_id(0) == 0)
        def _():
            dg_ref[...] = jnp.zeros_like(dg_ref)
        dy, dgrow = _norm_bwd(y_ref[...], g_ref[...], u_ref[...])
        dg_ref[...] += jnp.sum(dgrow, axis=0, keepdims=True)
        dyb = dy.astype(BF16)
        dy_ref[...] = dyb
        dog = _dot_nt(dyb, w_ref[...])
        gates = (g0, g1, g2, g3)
        for c in range(4):
            gt = gates[c][...]
            sg = _sigmoid(gt)
            o_ref = oa_ref if c < 2 else ob_ref
            ov = o_ref[:, (c % 2) * 256:(c % 2 + 1) * 256]
            dc = dog[:, c * 256:(c + 1) * 256]
            do_ref[:, c * 256:(c + 1) * 256] = dc * (gt * sg)
            dgate_ref[:, c * 256:(c + 1) * 256] = dc * ov * (sg * (1.0 + gt * (1.0 - sg)))

    row = lambda i: (i, 0)
    gspec = lambda c: pl.BlockSpec((tm, 256), lambda i: (i, gblk[c]))
    return pl.pallas_call(
        body, name=name, grid=(S // tm,),
        in_specs=[pl.BlockSpec((tm, D), row), pl.BlockSpec((tm, D), row), pl.BlockSpec((1, D), lambda i: (0, 0)),
                  pl.BlockSpec((D, D), lambda i: (0, 0)),
                  gspec(0), gspec(1), gspec(2), gspec(3),
                  pl.BlockSpec((tm, 512), lambda i: (i, oblk_a)),
                  pl.BlockSpec((tm, 512), lambda i: (i, oblk_b))],
        out_specs=[pl.BlockSpec((tm, D), row), pl.BlockSpec((tm, D), row), pl.BlockSpec((tm, D), row),
                   pl.BlockSpec((1, D), lambda i: (0, 0))],
        out_shape=[jax.ShapeDtypeStruct((S, D), BF16), jax.ShapeDtypeStruct((S, D), F32),
                   jax.ShapeDtypeStruct((S, D), F32), jax.ShapeDtypeStruct((1, D), F32)],
        compiler_params=_cparams(("arbitrary",)),
    )(dx_up, y, g, w, proj, proj, proj, proj, o_a, o_b)


def _rope_tables(pos, invf, name):
    S = pos.shape[0]
    tm = _pick(S, (512, 256))

    def body(p_ref, f_ref, c_ref, s1_ref, s2_ref):
        lane = lax.broadcasted_iota(jnp.int32, (1, LANES), 1)
        ang = p_ref[...].astype(F32) * f_ref[...]
        c, s = jnp.cos(ang), jnp.sin(ang)
        c_ref[...] = jnp.where((lane >= 64) & (lane < 96), c, 1.0)
        s1_ref[...] = jnp.where((lane >= 64) & (lane < 80), -s, 0.0)
        s2_ref[...] = jnp.where((lane >= 80) & (lane < 96), s, 0.0)

    spec = pl.BlockSpec((tm, LANES), lambda i: (i, 0))
    return pl.pallas_call(
        body, name=name, grid=(S // tm,),
        in_specs=[pl.BlockSpec((tm, 1), lambda i: (i, 0)), pl.BlockSpec((1, LANES), lambda i: (0, 0))],
        out_specs=[spec, spec, spec],
        out_shape=[jax.ShapeDtypeStruct((S, LANES), F32)] * 3,
        compiler_params=_cparams(("parallel",)),
    )(pos, invf)


def _rope(x, c, s1, s2):
    return x * c + pltpu.roll(x, LANES - 16, 1) * s1 + pltpu.roll(x, 16, 1) * s2


def _rope_t(d, c, s1, s2):
    return d * c + pltpu.roll(d * s1, 16, 1) + pltpu.roll(d * s2, LANES - 16, 1)


def _mla_prep(proj, gq, gkv, wq, wkv, cosT, s1T, s2T, name):
    S = proj.shape[0]
    tm = _pick(S, (256,))

    def body(p_ref, gq_ref, gkv_ref, wq_ref, wkv_ref, c_ref, s1_ref, s2_ref, q_ref, k_ref, v_ref, qn_ref, cn_ref):
        qa = p_ref[:, 0:384]
        ckv = p_ref[:, 384:640]
        kr = p_ref[:, 640:768]
        qn32 = (qa * _rstd(qa)) * gq_ref[...]
        cn32 = (ckv * _rstd(ckv)) * gkv_ref[...]
        qn = qn32.astype(BF16)
        cn = cn32.astype(BF16)
        qn_ref[...] = qn32.T.astype(BF16)
        cn_ref[...] = cn32.T.astype(BF16)
        qb = _dot(qn, wq_ref[...])
        kvb = _dot(cn, wkv_ref[...])
        c, s1, s2 = c_ref[...], s1_ref[...], s2_ref[...]
        krr = _rope(kr, c, s1, s2)
        for h in range(8):
            sl = slice(h * LANES, (h + 1) * LANES)
            q_ref[:, sl] = _rope(qb[:, sl], c, s1, s2)
            k_ref[:, sl] = kvb[:, sl] + krr
        v_ref[...] = kvb[:, 1024:1536]

    row = lambda i: (i, 0)
    fixed = lambda i: (0, 0)
    tspec = pl.BlockSpec((tm, LANES), row)
    return pl.pallas_call(
        body, name=name, grid=(S // tm,),
        in_specs=[pl.BlockSpec((tm, L0_PREP_W), lambda i: (i, L0_PREP // L0_PREP_W)),
                  pl.BlockSpec((1, 384), fixed), pl.BlockSpec((1, 256), fixed),
                  pl.BlockSpec((384, 1024), fixed), pl.BlockSpec((256, 1536), fixed), tspec, tspec, tspec],
        out_specs=[pl.BlockSpec((tm, 1024), row), pl.BlockSpec((tm, 1024), row), pl.BlockSpec((tm, 512), row),
                   pl.BlockSpec((384, tm), lambda i: (0, i)), pl.BlockSpec((256, tm), lambda i: (0, i))],
        out_shape=[jax.ShapeDtypeStruct((S, 1024), F32), jax.ShapeDtypeStruct((S, 1024), F32),
                   jax.ShapeDtypeStruct((S, 512), F32), jax.ShapeDtypeStruct((384, S), BF16),
                   jax.ShapeDtypeStruct((256, S), BF16)],
        compiler_params=_cparams(("parallel",)),
    )(proj, gq, gkv, wq, wkv, cosT, s1T, s2T)


def _mla_prep_bwd(dq, dk, dv, proj, gq, gkv, wq, wkv, cosT, s1T, s2T, name):
    S = proj.shape[0]
    tm = _pick(S, (256,))

    def body(dq_ref, dk_ref, dv_ref, p_ref, gq_ref, gkv_ref, wq_ref, wkv_ref, c_ref, s1_ref, s2_ref,
             dp_ref, dqb_ref, dkvb_ref, dgq_ref, dgkv_ref):
        @pl.when(pl.program_id(0) == 0)
        def _():
            dgq_ref[...] = jnp.zeros_like(dgq_ref)
            dgkv_ref[...] = jnp.zeros_like(dgkv_ref)
        c, s1, s2 = c_ref[...], s1_ref[...], s2_ref[...]
        lane = lax.broadcasted_iota(jnp.int32, (1, LANES), 1)
        dkr = jnp.zeros((tm, LANES), F32)
        for h in range(8):
            sl = slice(h * LANES, (h + 1) * LANES)
            dqb_ref[:, sl] = _rope_t(dq_ref[:, sl], c, s1, s2).astype(BF16)
            dkh = dk_ref[:, sl]
            dkvb_ref[:, sl] = dkh.astype(BF16)
            dkr = dkr + dkh
        dkvb_ref[:, 1024:1536] = dv_ref[...].astype(BF16)
        dkr = jnp.where((lane >= 64) & (lane < 96), _rope_t(dkr, c, s1, s2), 0.0)
        dqn = _dot_nt(dqb_ref[...], wq_ref[...])
        dcn = _dot_nt(dkvb_ref[...], wkv_ref[...])
        dqa, gq_row = _norm_bwd(p_ref[:, 0:384], gq_ref[...], dqn)
        dckv, gkv_row = _norm_bwd(p_ref[:, 384:640], gkv_ref[...], dcn)
        dp_ref[:, 0:384] = dqa
        dp_ref[:, 384:640] = dckv
        dp_ref[:, 640:768] = dkr
        dgq_ref[...] += jnp.sum(gq_row, axis=0, keepdims=True)
        dgkv_ref[...] += jnp.sum(gkv_row, axis=0, keepdims=True)

    row = lambda i: (i, 0)
    fixed = lambda i: (0, 0)
    tspec = pl.BlockSpec((tm, LANES), row)
    return pl.pallas_call(
        body, name=name, grid=(S // tm,),
        in_specs=[pl.BlockSpec((tm, 1024), row), pl.BlockSpec((tm, 1024), row), pl.BlockSpec((tm, 512), row),
                  pl.BlockSpec((tm, L0_PREP_W), lambda i: (i, L0_PREP // L0_PREP_W)),
                  pl.BlockSpec((1, 384), fixed), pl.BlockSpec((1, 256), fixed),
                  pl.BlockSpec((384, 1024), fixed), pl.BlockSpec((256, 1536), fixed), tspec, tspec, tspec],
        out_specs=[pl.BlockSpec((tm, L0_PREP_W), row), pl.BlockSpec((tm, 1024), row), pl.BlockSpec((tm, 1536), row),
                   pl.BlockSpec((1, 384), fixed), pl.BlockSpec((1, 256), fixed)],
        out_shape=[jax.ShapeDtypeStruct((S, L0_PREP_W), F32), jax.ShapeDtypeStruct((S, 1024), BF16),
                   jax.ShapeDtypeStruct((S, 1536), BF16), jax.ShapeDtypeStruct((1, 384), F32),
                   jax.ShapeDtypeStruct((1, 256), F32)],
        compiler_params=_cparams(("arbitrary",)),
    )(dq, dk, dv, proj, gq, gkv, wq, wkv, cosT, s1T, s2T)


def _fox_prep(proj, bf, name):
    S = proj.shape[0]
    tm = _pick(S, (256,))

    def body(f_ref, b_ref, c_ref, carry_ref):
        @pl.when(pl.program_id(0) == 0)
        def _():
            carry_ref[...] = jnp.zeros_like(carry_ref)
        u = f_ref[...] + b_ref[...]
        lf = jnp.minimum(u, 0.0) - jnp.log(1.0 + jnp.exp(-jnp.abs(u)))
        r = lax.broadcasted_iota(jnp.int32, (tm, tm), 0)
        cidx = lax.broadcasted_iota(jnp.int32, (tm, tm), 1)
        tri = (cidx <= r).astype(BF16)
        hi, mid, lo = _split3(lf)
        c = carry_ref[...] + (_dot(tri, hi) + _dot(tri, mid) + _dot(tri, lo))
        c_ref[...] = c
        carry_ref[...] = c[tm - 1:tm, :]

    return pl.pallas_call(
        body, name=name, grid=(S // tm,),
        in_specs=[pl.BlockSpec((tm, LANES), lambda i: (i, L1_F // LANES)), pl.BlockSpec((1, LANES), lambda i: (0, 0))],
        out_specs=pl.BlockSpec((tm, LANES), lambda i: (i, 0)),
        out_shape=jax.ShapeDtypeStruct((S, LANES), F32),
        scratch_shapes=[pltpu.VMEM((1, LANES), F32)],
        compiler_params=_cparams(("arbitrary",)),
    )(proj, bf)


def _fox_prep_bwd(dc, proj, bf, name):
    S = proj.shape[0]
    tm = _pick(S, (256,))
    nb = S // tm

    def body(dc_ref, f_ref, b_ref, df_ref, db_ref, carry_ref):
        @pl.when(pl.program_id(0) == 0)
        def _():
            carry_ref[...] = jnp.zeros_like(carry_ref)
            db_ref[...] = jnp.zeros_like(db_ref)
        r = lax.broadcasted_iota(jnp.int32, (tm, tm), 0)
        cidx = lax.broadcasted_iota(jnp.int32, (tm, tm), 1)
        tri = (cidx >= r).astype(BF16)
        hi, mid, lo = _split3(dc_ref[...])
        dlf = carry_ref[...] + (_dot(tri, hi) + _dot(tri, mid) + _dot(tri, lo))
        carry_ref[...] = dlf[0:1, :]
        u = f_ref[...] + b_ref[...]
        e = jnp.exp(-jnp.abs(u))
        sneg = jnp.where(u >= 0.0, e, 1.0) / (1.0 + e)
        lane = lax.broadcasted_iota(jnp.int32, (1, LANES), 1)
        df = jnp.where(lane < FOX_HEADS, dlf * sneg, 0.0)
        df_ref[...] = df
        db_ref[...] += jnp.sum(df, axis=0, keepdims=True)

    return pl.pallas_call(
        body, name=name, grid=(nb,),
        in_specs=[pl.BlockSpec((tm, LANES), lambda i: (nb - 1 - i, 0)),
                  pl.BlockSpec((tm, LANES), lambda i: (nb - 1 - i, L1_F // LANES)),
                  pl.BlockSpec((1, LANES), lambda i: (0, 0))],
        out_specs=[pl.BlockSpec((tm, LANES), lambda i: (nb - 1 - i, 0)), pl.BlockSpec((1, LANES), lambda i: (0, 0))],
        out_shape=[jax.ShapeDtypeStruct((S, LANES), F32), jax.ShapeDtypeStruct((1, LANES), F32)],
        scratch_shapes=[pltpu.VMEM((1, LANES), F32)],
        compiler_params=_cparams(("arbitrary",)),
    )(dc, proj, bf)


def _att_specs(kind, S, T):
    if kind == "sb":
        qo, ko, vo, go = L0_SBQ // LANES, L0_SBK // LANES, L0_SBV // LANES, L0_SBG // LANES
    elif kind == "fox":
        qo, ko, vo, go = L1_Q // LANES, L1_K // LANES, L1_V // LANES, L1_G // LANES
    else:
        go = L0_MLG // LANES
        return (pl.BlockSpec((T, 256), lambda p, i: (i, p)), pl.BlockSpec((S, 256), lambda p, i: (0, p)),
                pl.BlockSpec((S, LANES), lambda p, i: (0, p)), pl.BlockSpec((T, LANES), lambda p, i: (i, go + p)))
    return (pl.BlockSpec((T, LANES), lambda p, i: (i, qo + p)), pl.BlockSpec((S, LANES), lambda p, i: (0, ko + p)),
            pl.BlockSpec((S, LANES), lambda p, i: (0, vo + p)), pl.BlockSpec((T, LANES), lambda p, i: (i, go + p)))


def _mask_flags(js, masked_at):
    return [t == masked_at for t in range(len(js))]


def _loop_tiles(i, tiles, right_to_left, G=ATT_GROUP):
    ng = i // G
    rest = i - ng * G

    def leftover():
        for r in range(G):
            @pl.when(rest == r)
            def _():
                if right_to_left:
                    tiles([i - u for u in range(r + 1)], 0)
                else:
                    tiles([ng * G + u for u in range(r + 1)], r)

    def group(g, carry):
        if right_to_left:
            tiles([ng * G - 1 - (g * G + u) for u in range(G)], None)
        else:
            tiles([g * G + u for u in range(G)], None)
        return carry

    if right_to_left:
        leftover()
    lax.fori_loop(0, ng, group, 0)
    if not right_to_left:
        leftover()


def _head_q(kind, q_ref, m0, scale):
    if kind == "mla":
        return [q_ref[:, 0:LANES].astype(BF16), q_ref[:, LANES:2 * LANES].astype(BF16)]
    qv = q_ref[...] * scale
    return [jnp.where(m0, qv, 0.0).astype(BF16), jnp.where(m0, 0.0, qv).astype(BF16)]


def _head_k(kind, k_ref, start, T):
    if kind == "mla":
        return [k_ref[pl.ds(start, T), 0:LANES].astype(BF16), k_ref[pl.ds(start, T), LANES:2 * LANES].astype(BF16)]
    kb = k_ref[pl.ds(start, T), :].astype(BF16)
    return [kb, kb]


def _transpose_tiles(src, col_off, n_out, cw, group, name):
    S = src.shape[0]
    T = ATT_T
    first = col_off // (group * cw)

    def body(x_ref, o_ref):
        for u in range(group):
            o_ref[u] = x_ref[:, u * cw:(u + 1) * cw].T.astype(BF16)

    return pl.pallas_call(
        body, name=name, grid=(S // T, n_out // group),
        in_specs=[pl.BlockSpec((T, group * cw), lambda j, g: (j, first + g))],
        out_specs=pl.BlockSpec((group, None, cw, T), lambda j, g: (g, j, 0, 0)),
        out_shape=jax.ShapeDtypeStruct((n_out, S // T, cw, T), BF16),
        compiler_params=_cparams(("parallel", "parallel")),
    )(src)


def _softmax_fwd(kind, qkvg, c_col, S, npairs, name):
    T = ATT_T
    nq = S // T
    fox = kind == "fox"
    scale = (96 if kind == "mla" else 64) ** -0.5

    def body(*refs):
        if fox:
            q_ref, k_ref, vt_ref, g_ref, cc_ref, o_ref, og_ref, ogt_ref, st_ref, m_ref, acc_ref = refs
        else:
            q_ref, k_ref, vt_ref, g_ref, o_ref, og_ref, ogt_ref, st_ref, m_ref, acc_ref = refs
        i = pl.program_id(1)
        m0 = lax.broadcasted_iota(jnp.int32, (1, LANES), 1) < 64
        top = lax.broadcasted_iota(jnp.int32, (LANES, 1), 0) < 64
        key = lax.broadcasted_iota(jnp.int32, (T, LANES), 0)
        qrow = lax.broadcasted_iota(jnp.int32, (T, LANES), 1)
        qh = _head_q(kind, q_ref, m0, scale)
        m_ref[...] = jnp.full(m_ref.shape, NEG, F32)
        acc_ref[...] = jnp.zeros(acc_ref.shape, F32)
        chains = [(h, b) for h in range(2) for b in range(T // LANES)]

        def tiles(js, masked_at):
            starts = [pl.multiple_of(j * T, T) for j in js]
            zss = []
            for start in starts:
                kh = _head_k(kind, k_ref, start, T)
                zss.append(_split_blocks([_dot_nt(kh[h], qh[h]) for h in range(2)]))
            pss, alss = [], []
            for start, zs, masked in zip(starts, zss, _mask_flags(js, masked_at)):
                ps, alphas = [], []
                for (h, b), z in zip(chains, zs):
                    lanes = slice(b * LANES, (b + 1) * LANES)
                    if kind == "mla":
                        z = z * scale
                    if fox:
                        z = z - cc_ref[h, pl.ds(start, T), :]
                    if masked:
                        z = jnp.where(key <= qrow + b * LANES, z, NEG)
                    m_prev = m_ref[h, :, lanes]
                    m_new = jnp.maximum(m_prev, jnp.max(z, axis=0, keepdims=True))
                    alphas.append(jnp.exp(m_prev - m_new))
                    ps.append(jnp.exp(z - m_new).astype(BF16))
                    m_ref[h, :, lanes] = m_new
                pss.append(_join_blocks(ps, T // LANES))
                alss.append(_join_blocks(alphas, T // LANES))
            for j, ps, alphas in zip(js, pss, alss):
                vt = vt_ref[j]
                vth = [jnp.where(top, vt, 1.0).astype(BF16), jnp.where(top, 1.0, vt).astype(BF16)]
                for h in range(2):
                    acc_ref[h] = alphas[h] * acc_ref[h] + _dot(vth[h], ps[h])

        _loop_tiles(i, tiles, False, 2 * ATT_GROUP)
        acc = [acc_ref[0], acc_ref[1]]
        ot = jnp.concatenate([acc[0][0:64] / acc[0][64:128], acc[1][64:128] / acc[1][0:64]], axis=0)
        o = ot.T
        o_ref[...] = o
        gt = g_ref[...]
        og = o * (gt * _sigmoid(gt))
        og_ref[...] = og.astype(BF16)
        ogt_ref[...] = og.T.astype(BF16)
        st_ref[0] = m_ref[0] + jnp.log(acc[0][64:65])
        st_ref[1] = m_ref[1] + jnp.log(acc[1][0:1])

    qs, ks, _, gs = _att_specs(kind, S, T)
    in_specs = [qs, ks, pl.BlockSpec((None, nq, LANES, T), lambda p, i: (p, 0, 0, 0)), gs]
    args = list(qkvg)
    if fox:
        in_specs += [pl.BlockSpec((2, S, LANES), lambda p, i: (p, 0, 0))]
        args += [c_col]
    W = npairs * LANES
    return pl.pallas_call(
        body, name=name, grid=(npairs, nq), in_specs=in_specs,
        out_specs=[pl.BlockSpec((T, LANES), lambda p, i: (i, p)), pl.BlockSpec((T, LANES), lambda p, i: (i, p)),
                   pl.BlockSpec((LANES, T), lambda p, i: (p, i)),
                   pl.BlockSpec((2, None, 1, T), lambda p, i: (p, i, 0, 0))],
        out_shape=[jax.ShapeDtypeStruct((S, W), F32), jax.ShapeDtypeStruct((S, W), BF16),
                   jax.ShapeDtypeStruct((W, S), BF16),
                   jax.ShapeDtypeStruct((2 * npairs, nq, 1, T), F32)],
        scratch_shapes=[pltpu.VMEM((2, 1, T), F32), pltpu.VMEM((2, LANES, T), F32)],
        compiler_params=_cparams(("parallel", "parallel")),
    )(*args)


def _softplus_parts(z):
    sp = jnp.maximum(z, 0.0) + jnp.log(1.0 + jnp.exp(-jnp.abs(z)))
    return -sp, z - sp


def _split2(x):
    hi = x.astype(BF16)
    return hi, (x - hi.astype(F32)).astype(BF16)


def _split_blocks(per_head):
    return [x[:, b * LANES:(b + 1) * LANES] for x in per_head for b in range(x.shape[1] // LANES)]


def _join_blocks(per_block, nb):
    return [jnp.concatenate(per_block[h * nb:(h + 1) * nb], axis=1) for h in range(len(per_block) // nb)]


def _row_of(col):
    return jnp.broadcast_to(col, (col.shape[0], LANES)).T[0:1]


def _softmax_bwd_t(kind, q, k, kt, v, do, do_off, o, lse, c_col, S, npairs, name):
    T = ATT_T
    nq = S // T
    nb = T // LANES
    fox = kind == "fox"
    mla = kind == "mla"
    scale = (96 if mla else 64) ** -0.5
    kw = 256 if mla else LANES

    def body(*refs):
        if fox:
            (q_ref, k_ref, kt_ref, v_ref, do_ref, o_ref, st_ref, cc_ref,
             dq_ref, dk_ref, dv_ref, dck_ref, dcq_ref, dqt_ref, rs_ref, dkx_ref) = refs
        else:
            q_ref, k_ref, kt_ref, v_ref, do_ref, o_ref, st_ref, dq_ref, dk_ref, dv_ref, dqt_ref = refs
        i = pl.program_id(1)

        @pl.when(i == 0)
        def _():
            dv_ref[...] = jnp.zeros_like(dv_ref)
            if fox:
                dkx_ref[...] = jnp.zeros_like(dkx_ref)
            else:
                dk_ref[...] = jnp.zeros_like(dk_ref)

        m0 = lax.broadcasted_iota(jnp.int32, (1, LANES), 1) < 64
        top = lax.broadcasted_iota(jnp.int32, (LANES, 1), 0) < 64
        key = lax.broadcasted_iota(jnp.int32, (T, LANES), 0)
        qrow = lax.broadcasted_iota(jnp.int32, (T, LANES), 1)
        qh = _head_q(kind, q_ref, m0, scale)
        if fox:
            qv = q_ref[...] * scale
            qk = [jnp.where(m0, qv, 1.0).astype(BF16), jnp.where(m0, 1.0, qv).astype(BF16)]
        else:
            qk = qh
        dov = do_ref[...]
        prod = dov * o_ref[...]
        dd = [_row_of(jnp.sum(jnp.where(m0, prod, 0.0), axis=1, keepdims=True)),
              _row_of(jnp.sum(jnp.where(m0, 0.0, prod), axis=1, keepdims=True))]
        doh = [jnp.where(m0, dov, 0.0).astype(BF16), jnp.where(m0, 0.0, dov).astype(BF16)]
        lse = [st_ref[0], st_ref[1]]
        dqt_ref[...] = jnp.zeros_like(dqt_ref)
        if fox:
            rs_ref[...] = jnp.zeros_like(rs_ref)
        chains = [(h, b) for h in range(2) for b in range(nb)]

        def tiles(js, masked_at):
            starts = [pl.multiple_of(j * T, T) for j in js]
            zss, dpss = [], []
            for start in starts:
                vb = v_ref[pl.ds(start, T), :].astype(BF16)
                kh = _head_k(kind, k_ref, start, T)
                zss.append(_split_blocks([_dot_nt(kh[h], qh[h]) for h in range(2)]))
                dpss.append(_split_blocks([_dot_nt(vb, doh[h]) for h in range(2)]))
            pss, dsss = [], []
            for start, zs, dps, masked in zip(starts, zss, dpss, _mask_flags(js, masked_at)):
                ps, dss = [], []
                for (h, b), z, dp in zip(chains, zs, dps):
                    lanes = slice(b * LANES, (b + 1) * LANES)
                    if mla:
                        z = z * scale
                    if fox:
                        z = z - cc_ref[h, pl.ds(start, T), :]
                    if masked:
                        z = jnp.where(key <= qrow + b * LANES, z, NEG)
                    p = jnp.exp(z - lse[h][:, lanes])
                    ds = p * (dp - dd[h][:, lanes])
                    dsb = ds.astype(BF16)
                    if fox:
                        rs_ref[h, :, lanes] += jnp.sum(dsb.astype(F32), axis=0, keepdims=True)
                    ps.append(p.astype(BF16))
                    dss.append(dsb)
                pss.append(_join_blocks(ps, nb))
                dsss.append(_join_blocks(dss, nb))
            for j, start, ps, dss in zip(js, starts, pss, dsss):
                kt = kt_ref[j]
                dvc = None
                for h in range(2):
                    dkh = _dot(dss[h], qk[h])
                    dvh = _dot(ps[h], doh[h])
                    dvc = dvh if dvc is None else dvc + dvh
                    kth = kt[h * LANES:(h + 1) * LANES] if mla else kt
                    dqt_ref[h] += _dot(kth, dss[h])
                    if fox:
                        dkx_ref[h, pl.ds(start, T), :] += dkh
                    elif mla:
                        dk_ref[pl.ds(start, T), h * LANES:(h + 1) * LANES] += dkh * scale
                    else:
                        dk_ref[pl.ds(start, T), :] += dkh
                dv_ref[pl.ds(start, T), :] += dvc

        _loop_tiles(i, tiles, False)
        if mla:
            dq_ref[:, 0:LANES] = dqt_ref[0].T * scale
            dq_ref[:, LANES:2 * LANES] = dqt_ref[1].T * scale
        else:
            dq_ref[...] = jnp.where(top, dqt_ref[0], dqt_ref[1]).T * scale
        if fox:
            dcq_ref[0] = rs_ref[0]
            dcq_ref[1] = rs_ref[1]

            @pl.when(i == nq - 1)
            def _():
                dk_ref[...] = jnp.where(m0, dkx_ref[0], dkx_ref[1])
                dck_ref[0] = dkx_ref[0][:, 64:65]
                dck_ref[1] = dkx_ref[1][:, 0:1]

    qs, ks, vs, _ = _att_specs(kind, S, T)
    stat = pl.BlockSpec((2, None, 1, T), lambda p, i: (p, i, 0, 0))
    in_specs = [qs, ks, pl.BlockSpec((None, nq, kw, T), lambda p, i: (p, 0, 0, 0)), vs,
                pl.BlockSpec((T, LANES), lambda p, i: (i, do_off + p)),
                pl.BlockSpec((T, LANES), lambda p, i: (i, p)), stat]
    args = [q, k, kt, v, do, o, lse]
    W = npairs * LANES
    out_specs = [pl.BlockSpec((T, kw), lambda p, i: (i, p)), pl.BlockSpec((S, kw), lambda p, i: (0, p)),
                 pl.BlockSpec((S, LANES), lambda p, i: (0, p))]
    out_shape = [jax.ShapeDtypeStruct((S, npairs * kw), F32), jax.ShapeDtypeStruct((S, npairs * kw), F32),
                 jax.ShapeDtypeStruct((S, W), F32)]
    scratch = [pltpu.VMEM((2, LANES, T), F32)]
    if fox:
        in_specs.append(pl.BlockSpec((2, S, LANES), lambda p, i: (p, 0, 0)))
        args.append(c_col)
        out_specs += [pl.BlockSpec((2, S, 1), lambda p, i: (p, 0, 0)), stat]
        out_shape += [jax.ShapeDtypeStruct((2 * npairs, S, 1), F32), jax.ShapeDtypeStruct((2 * npairs, nq, 1, T), F32)]
        scratch += [pltpu.VMEM((2, 1, T), F32), pltpu.VMEM((2, S, LANES), F32)]
    return pl.pallas_call(
        body, name=name, grid=(npairs, nq), in_specs=in_specs, out_specs=out_specs, out_shape=out_shape,
        scratch_shapes=scratch, compiler_params=_cparams(("parallel", "arbitrary")),
    )(*args)


def _sb_fwd_t(proj, vt, S, npairs, name):
    T = ATT_T
    nq = S // T
    nb = T // LANES
    scale = 64 ** -0.5

    def body(q_ref, k_ref, vt_ref, g_ref, o_ref, og_ref, ogt_ref, st_ref, rem_ref, acc_ref):
        i = pl.program_id(1)
        m0 = lax.broadcasted_iota(jnp.int32, (1, LANES), 1) < 64
        top = lax.broadcasted_iota(jnp.int32, (LANES, 1), 0) < 64
        key = lax.broadcasted_iota(jnp.int32, (T, LANES), 0)
        qrow = lax.broadcasted_iota(jnp.int32, (T, LANES), 1)
        r = lax.broadcasted_iota(jnp.int32, (T, T), 0)
        c = lax.broadcasted_iota(jnp.int32, (T, T), 1)
        after = (c > r).astype(BF16)
        qh = _head_q("sb", q_ref, m0, scale)
        rem_ref[...] = jnp.zeros_like(rem_ref)
        acc_ref[...] = jnp.zeros_like(acc_ref)
        chains = [(h, b) for h in range(2) for b in range(nb)]

        def tiles(js, masked_at):
            zss = []
            for j in js:
                kb = k_ref[pl.ds(pl.multiple_of(j * T, T), T), :].astype(BF16)
                zss.append(_split_blocks([_dot_nt(kb, qh[h]) for h in range(2)]))
            lass, sums, hiss, loss = [], [], [], []
            for zs, masked in zip(zss, _mask_flags(js, masked_at)):
                las, sm, his, los = [], [], [], []
                for (h, b), z in zip(chains, zs):
                    lk, la = _softplus_parts(z)
                    if masked:
                        lk = jnp.where(key < qrow + b * LANES, lk, 0.0)
                    hi, lo = _split2(lk)
                    las.append(la)
                    sm.append(jnp.sum(lk, axis=0, keepdims=True))
                    his.append(hi)
                    los.append(lo)
                lass.append(las)
                sums.append(sm)
                hiss.append(_join_blocks(his, nb))
                loss.append(_join_blocks(los, nb))
            rcss = [_split_blocks([_dot(after, hi) + _dot(after, lo) for hi, lo in zip(his, los)])
                    for his, los in zip(hiss, loss)]
            wss = []
            for las, sm, rcs, masked in zip(lass, sums, rcss, _mask_flags(js, masked_at)):
                ws = []
                for (h, b), la, s, rc in zip(chains, las, sm, rcs):
                    lanes = slice(b * LANES, (b + 1) * LANES)
                    w = jnp.exp(la + (rem_ref[h, :, lanes] + rc))
                    if masked:
                        w = jnp.where(key < qrow + b * LANES, w, 0.0)
                    ws.append(w.astype(BF16))
                    rem_ref[h, :, lanes] += s
                wss.append(_join_blocks(ws, nb))
            for j, ws in zip(js, wss):
                vtb = vt_ref[j]
                for h in range(2):
                    acc_ref[h] += _dot(vtb, ws[h])

        _loop_tiles(i, tiles, True)
        o = jnp.where(top, acc_ref[0], acc_ref[1]).T
        o_ref[...] = o
        gt = g_ref[...]
        og = o * (gt * _sigmoid(gt))
        og_ref[...] = og.astype(BF16)
        ogt_ref[...] = og.T.astype(BF16)
        st_ref[0] = rem_ref[0]
        st_ref[1] = rem_ref[1]

    qs, ks, _, gs = _att_specs("sb", S, T)
    W = npairs * LANES
    return pl.pallas_call(
        body, name=name, grid=(npairs, nq),
        in_specs=[qs, ks, pl.BlockSpec((None, nq, LANES, T), lambda p, i: (p, 0, 0, 0)), gs],
        out_specs=[pl.BlockSpec((T, LANES), lambda p, i: (i, p)), pl.BlockSpec((T, LANES), lambda p, i: (i, p)),
                   pl.BlockSpec((LANES, T), lambda p, i: (p, i)),
                   pl.BlockSpec((2, None, 1, T), lambda p, i: (p, i, 0, 0))],
        out_shape=[jax.ShapeDtypeStruct((S, W), F32), jax.ShapeDtypeStruct((S, W), BF16),
                   jax.ShapeDtypeStruct((W, S), BF16),
                   jax.ShapeDtypeStruct((2 * npairs, nq, 1, T), F32)],
        scratch_shapes=[pltpu.VMEM((2, 1, T), F32), pltpu.VMEM((2, LANES, T), F32)],
        compiler_params=_cparams(("parallel", "parallel")),
    )(proj, proj, vt, proj)


def _sb_bwd_t(proj, kt, do, tot, S, npairs, name):
    T = ATT_T
    nq = S // T
    nb = T // LANES
    scale = 64 ** -0.5

    def body(q_ref, k_ref, kt_ref, v_ref, do_ref, st_ref, dq_ref, dk_ref, dv_ref, dqt_ref, pre_ref, gpre_ref):
        i = pl.program_id(1)

        @pl.when(i == 0)
        def _():
            dk_ref[...] = jnp.zeros_like(dk_ref)
            dv_ref[...] = jnp.zeros_like(dv_ref)

        m0 = lax.broadcasted_iota(jnp.int32, (1, LANES), 1) < 64
        top = lax.broadcasted_iota(jnp.int32, (LANES, 1), 0) < 64
        key = lax.broadcasted_iota(jnp.int32, (T, LANES), 0)
        qrow = lax.broadcasted_iota(jnp.int32, (T, LANES), 1)
        r = lax.broadcasted_iota(jnp.int32, (T, T), 0)
        c = lax.broadcasted_iota(jnp.int32, (T, T), 1)
        upto = (c <= r).astype(BF16)
        left = (c < r).astype(BF16)
        qh = _head_q("sb", q_ref, m0, scale)
        dov = do_ref[...]
        doh = [jnp.where(m0, dov, 0.0).astype(BF16), jnp.where(m0, 0.0, dov).astype(BF16)]
        tot_h = [st_ref[0], st_ref[1]]
        dqt_ref[...] = jnp.zeros_like(dqt_ref)
        pre_ref[...] = jnp.zeros_like(pre_ref)
        gpre_ref[...] = jnp.zeros_like(gpre_ref)
        chains = [(h, b) for h in range(2) for b in range(nb)]

        def tiles(js, masked_at):
            starts = [pl.multiple_of(j * T, T) for j in js]
            zss, dwss = [], []
            for start in starts:
                vb = v_ref[pl.ds(start, T), :].astype(BF16)
                kb = k_ref[pl.ds(start, T), :].astype(BF16)
                zss.append(_split_blocks([_dot_nt(kb, qh[h]) for h in range(2)]))
                dwss.append(_split_blocks([_dot_nt(vb, doh[h]) for h in range(2)]))
            lass, sums, hiss, loss = [], [], [], []
            for zs, masked in zip(zss, _mask_flags(js, masked_at)):
                las, sm, his, los = [], [], [], []
                for (h, b), z in zip(chains, zs):
                    lk, la = _softplus_parts(z)
                    if masked:
                        lk = jnp.where(key < qrow + b * LANES, lk, 0.0)
                    hi, lo = _split2(lk)
                    las.append(la)
                    sm.append(jnp.sum(lk, axis=0, keepdims=True))
                    his.append(hi)
                    los.append(lo)
                lass.append(las)
                sums.append(sm)
                hiss.append(_join_blocks(his, nb))
                loss.append(_join_blocks(los, nb))
            pcss = [_split_blocks([_dot(upto, hi) + _dot(upto, lo) for hi, lo in zip(his, los)])
                    for his, los in zip(hiss, loss)]
            wss, gss = [], []
            for las, sm, pcs, dws, masked in zip(lass, sums, pcss, dwss, _mask_flags(js, masked_at)):
                ws, gs = [], []
                for (h, b), la, s, pc, dw in zip(chains, las, sm, pcs, dws):
                    lanes = slice(b * LANES, (b + 1) * LANES)
                    w = jnp.exp(la + ((tot_h[h][:, lanes] - pre_ref[h, :, lanes]) - pc))
                    if masked:
                        w = jnp.where(key < qrow + b * LANES, w, 0.0)
                    ws.append(w.astype(BF16))
                    gs.append(dw * w)
                    pre_ref[h, :, lanes] += s
                wss.append(_join_blocks(ws, nb))
                gss.append(gs)
            gcss = [_split_blocks([_dot(left, g) for g in _join_blocks([g.astype(BF16) for g in gs], nb)]) for gs in gss]
            dzss = []
            for las, gs, gcs, masked in zip(lass, gss, gcss, _mask_flags(js, masked_at)):
                dzs = []
                for (h, b), la, g, gc in zip(chains, las, gs, gcs):
                    lanes = slice(b * LANES, (b + 1) * LANES)
                    dz = g - (g + (gpre_ref[h, :, lanes] + gc)) * jnp.exp(la)
                    if masked:
                        dz = jnp.where(key < qrow + b * LANES, dz, 0.0)
                    dzs.append(dz.astype(BF16))
                    gpre_ref[h, :, lanes] += jnp.sum(g, axis=0, keepdims=True)
                dzss.append(_join_blocks(dzs, nb))
            for j, start, ws, dzs in zip(js, starts, wss, dzss):
                kt = kt_ref[j]
                dkc = dvc = None
                for h in range(2):
                    dkh = _dot(dzs[h], qh[h])
                    dvh = _dot(ws[h], doh[h])
                    dkc = dkh if dkc is None else dkc + dkh
                    dvc = dvh if dvc is None else dvc + dvh
                    dqt_ref[h] += _dot(kt, dzs[h])
                dk_ref[pl.ds(start, T), :] += dkc
                dv_ref[pl.ds(start, T), :] += dvc

        _loop_tiles(i, tiles, False)
        dq_ref[...] = jnp.where(top, dqt_ref[0], dqt_ref[1]).T * scale

    qs, ks, vs, _ = _att_specs("sb", S, T)
    W = npairs * LANES
    return pl.pallas_call(
        body, name=name, grid=(npairs, nq),
        in_specs=[qs, ks, pl.BlockSpec((None, nq, LANES, T), lambda p, i: (p, 0, 0, 0)), vs,
                  pl.BlockSpec((T, LANES), lambda p, i: (i, p)),
                  pl.BlockSpec((2, None, 1, T), lambda p, i: (p, i, 0, 0))],
        out_specs=[pl.BlockSpec((T, LANES), lambda p, i: (i, p)), pl.BlockSpec((S, LANES), lambda p, i: (0, p)),
                   pl.BlockSpec((S, LANES), lambda p, i: (0, p))],
        out_shape=[jax.ShapeDtypeStruct((S, W), F32)] * 3,
        scratch_shapes=[pltpu.VMEM((2, LANES, T), F32), pltpu.VMEM((2, 1, T), F32), pltpu.VMEM((2, 1, T), F32)],
        compiler_params=_cparams(("parallel", "arbitrary")),
    )(proj, proj, kt, proj, do, tot)


def _pad_w0(w):
    z = lambda n: jnp.zeros((w.shape[0], n), w.dtype)
    return jnp.concatenate([w[:, 2048:2432], w[:, 2432:2688], z(64), w[:, 2688:2720], z(32),
                            w[:, 1536:2048], w[:, 2720:3232], w[:, 0:512], w[:, 512:1024], w[:, 1024:1536]], axis=1)


def _unpad_w0(wp):
    return jnp.concatenate([wp[:, L0_SBQ:L0_SBQ + 512], wp[:, L0_SBK:L0_SBK + 512], wp[:, L0_SBV:L0_SBV + 512],
                            wp[:, L0_SBG:L0_SBG + 512], wp[:, 0:384], wp[:, 384:640], wp[:, 704:736],
                            wp[:, L0_MLG:L0_MLG + 512]], axis=1)


def _pad_wq(w):
    return jnp.pad(w.reshape(384, 8, 96), ((0, 0), (0, 0), (0, 32))).reshape(384, 1024)


def _unpad_wq(wp):
    return wp.reshape(384, 8, 128)[:, :, :96].reshape(384, 768)


def _pad_wkv(w):
    w3 = w.reshape(256, 8, 128)
    k = jnp.pad(w3[:, :, :64], ((0, 0), (0, 0), (0, 64))).reshape(256, 1024)
    return jnp.concatenate([k, w3[:, :, 64:].reshape(256, 512)], axis=1)


def _unpad_wkv(wp):
    k = wp[:, :1024].reshape(256, 8, 128)[:, :, :64]
    v = wp[:, 1024:].reshape(256, 8, 64)
    return jnp.concatenate([k, v], axis=-1).reshape(256, 1024)


def _pad_w1(w):
    return jnp.concatenate([w, jnp.zeros((w.shape[0], L1_WIDTH - ODD_IN_WIDTH), w.dtype)], axis=1)


def _local_step(x, positions, target, g, w0p, wqp, wkvp, wo0, w1p, wo1):
    S = x.shape[0]
    nq = S // ATT_T
    invf = ROPE_THETA ** (-jnp.arange(0, MLA_ROPE_DIM, 2, dtype=F32) / MLA_ROPE_DIM)
    invf = jnp.concatenate([jnp.zeros((64,), F32), invf, invf, jnp.zeros((32,), F32)]).reshape(1, LANES)
    cosT, s1T, s2T = _rope_tables(positions.reshape(S, 1), invf, "rope_tables")
    bfp = jnp.pad(g["l1_b_f"], ((0, 0), (0, LANES - FOX_HEADS)))

    proj0, h0t = _norm_matmul(x, g["l0_pre_g"], w0p, "l0_in_proj")
    qm, km, vm, qnt, cnt = _mla_prep(proj0, g["l0_q_a_g"], g["l0_kv_a_g"], wqp, wkvp, cosT, s1T, s2T, "mla_prep")
    sb_vt = _transpose_tiles(proj0, L0_SBV, 4, LANES, 2, "sb_vt")
    sb_kt = _transpose_tiles(proj0, L0_SBK, 4, LANES, 2, "sb_kt")
    o_sb, og_sb, ogt_sb, tot_sb = _sb_fwd_t(proj0, sb_vt, S, 4, "sb_fwd")
    vmt = _transpose_tiles(vm, 0, 4, LANES, 4, "mla_vt")
    kmt = _transpose_tiles(km, 0, 4, 2 * LANES, 4, "mla_kt")
    o_ml, og_ml, ogt_ml, lse_ml = _softmax_fwd("mla", (qm, km, vmt, proj0), None, S, 4, "mla_fwd")
    y0, x1 = _out_proj(og_sb, og_ml, 0, 0, wo0, x, g["l0_post_g"], None, "l0_out_proj")

    proj1, h1t = _norm_matmul(x1, g["l1_pre_g"], w1p, "l1_in_proj")
    cfx = _fox_prep(proj1, bfp, "fox_prep")
    c16 = cfx[:, :FOX_HEADS].T
    c_col = jnp.broadcast_to(c16[:, :, None], (FOX_HEADS, S, LANES))
    vt1 = _transpose_tiles(proj1, L1_V, 8, LANES, 8, "fox_vt")
    kt1 = _transpose_tiles(proj1, L1_K, 8, LANES, 8, "fox_kt")
    o_fx, og_fx, ogt_fx, lse_fx = _softmax_fwd("fox", (proj1, proj1, vt1, proj1), c_col, S, 8, "fox_fwd")
    y1, dx2, lsum = _out_proj(og_fx, og_fx, 0, 1, wo1, x1, g["l1_post_g"], target, "l1_out_proj")

    dy1, do1, dgate1, d_post1 = _out_proj_bwd(dx2, y1, g["l1_post_g"], wo1, proj1, (L1_G, L1_G + 512), o_fx, o_fx, 0, 1, "l1_out_bwd")
    dwo1 = _matmul_t(ogt_fx, dy1, "l1_dw_out")
    dq1, dk1, dv1, dck, dcq = _softmax_bwd_t("fox", proj1, proj1, kt1, proj1, do1, 0, o_fx, lse_fx, c_col, S, 8,
                                             "fox_bwd")
    dc = jnp.pad((dcq.reshape(FOX_HEADS, S) - dck.reshape(FOX_HEADS, S)).T, ((0, 0), (0, LANES - FOX_HEADS)))
    df, d_bf = _fox_prep_bwd(dc, proj1, bfp, "fox_prep_bwd")
    pieces1 = [(L1_Q, dq1), (L1_K, dk1), (L1_V, dv1), (L1_G, dgate1), (L1_F, df)]
    dx1, d_pre1 = _in_proj_bwd(pieces1, w1p, x1, g["l1_pre_g"], dx2, "l1_in_bwd")
    dw1p = jnp.concatenate([_matmul_t(h1t, a, "l1_dw_in_%d" % k) for k, (_, a) in enumerate(pieces1)], axis=1)

    dy0, do0, dgate0, d_post0 = _out_proj_bwd(dx1, y0, g["l0_post_g"], wo0, proj0, (L0_SBG, L0_MLG), o_sb, o_ml, 0, 0,
                                              "l0_out_bwd")
    dwo0 = jnp.concatenate([_matmul_t(ogt_sb, dy0, "l0_dw_out_sb"), _matmul_t(ogt_ml, dy0, "l0_dw_out_mla")], axis=0)
    dsq, dsk, dsv = _sb_bwd_t(proj0, sb_kt, do0, tot_sb, S, 4, "sb_bwd")
    dqm, dkm, dvm = _softmax_bwd_t("mla", qm, km, kmt, vm, do0, 4, o_ml, lse_ml, None, S, 4, "mla_bwd")
    dprep, dqb, dkvb, d_qag, d_kvag = _mla_prep_bwd(dqm, dkm, dvm, proj0, g["l0_q_a_g"], g["l0_kv_a_g"], wqp, wkvp,
                                                    cosT, s1T, s2T, "mla_prep_bwd")
    dwqp = _matmul_t(qnt, dqb, "l0_dw_qb")
    dwkvp = _matmul_t(cnt, dkvb, "l0_dw_kvb")
    pieces0 = [(L0_PREP, dprep), (L0_SBG, dgate0), (L0_SBQ, dsq), (L0_SBK, dsk), (L0_SBV, dsv)]
    dx0, d_pre0 = _in_proj_bwd(pieces0, w0p, x, g["l0_pre_g"], dx1, "l0_in_bwd")
    dw0p = jnp.concatenate([_matmul_t(h0t, a, "l0_dw_in_%d" % k) for k, (_, a) in enumerate(pieces0)], axis=1)

    grads = {
        "l0_pre_g": d_pre0, "l0_post_g": d_post0, "l0_w_in": dw0p, "l0_q_a_g": d_qag, "l0_w_q_b": dwqp,
        "l0_kv_a_g": d_kvag, "l0_w_kv_b": dwkvp, "l0_w_out": dwo0, "l1_pre_g": d_pre1, "l1_post_g": d_post1,
        "l1_w_in": dw1p, "l1_b_f": d_bf[:, :FOX_HEADS], "l1_w_out": dwo1,
    }
    return lsum, dx0, grads


_ANY = pl.BlockSpec(memory_space=pl.ANY)


def _place():
    return lax.axis_index("x"), lax.axis_index("y"), lax.axis_index("c")


def _other_chips(x, y):
    return [(1 - x, y), (x, 1 - y), (1 - x, 1 - y)]


def _half(c):
    return pl.ds(c * PACK_HALF, PACK_HALF)


def _weight_gather(pack):
    def body(p_ref, out_ref, send_sems, recv_sems):
        x, y, c = _place()
        sibling = (x, y, 1 - c)
        chips = _other_chips(x, y)

        def blk(chip, cc):
            return out_ref.at[2 * chip[0] + chip[1], _half(cc)]

        def copy(k, src, dst, to):
            return pltpu.make_async_remote_copy(src_ref=src, dst_ref=dst, send_sem=send_sems.at[k],
                                                recv_sem=recv_sems.at[k], device_id=to, device_id_type=MESH)

        first = [copy(j, p_ref.at[_half(c)], blk((x, y), c), (*chip, c)) for j, chip in enumerate(chips)]
        for cp in first:
            cp.start()
        passed = [copy(3 + j, blk(chip, c), blk(chip, c), sibling) for j, chip in enumerate(chips)]
        for j, chip in enumerate(chips):
            copy(j, blk(chip, c), blk(chip, c), (x, y, c)).wait_recv()
            passed[j].start()
        for j, chip in enumerate(chips):
            copy(3 + j, blk(chip, 1 - c), blk(chip, 1 - c), (x, y, c)).wait_recv()
        for cp in first + passed:
            cp.wait_send()

    return pl.pallas_call(
        body, name="weight_gather", in_specs=[_ANY], out_specs=_ANY,
        out_shape=jax.ShapeDtypeStruct((4,) + pack.shape, pack.dtype),
        scratch_shapes=[pltpu.SemaphoreType.DMA((6,)), pltpu.SemaphoreType.DMA((6,))],
    )(pack)


GRAD_TR = 2048


def _grad_core_exchange(p):
    def body(p_ref, recv_ref, send_sems, recv_sems):
        x, y, c = _place()
        give = [pltpu.make_async_remote_copy(src_ref=p_ref.at[j, _half(1 - c)], dst_ref=recv_ref.at[j],
                                             send_sem=send_sems.at[j], recv_sem=recv_sems.at[j],
                                             device_id=(x, y, 1 - c), device_id_type=MESH) for j in range(4)]
        for cp in give:
            cp.start()
        for cp in give:
            cp.wait()

    return pl.pallas_call(
        body, name="grad_core_exchange", in_specs=[_ANY], out_specs=_ANY,
        out_shape=jax.ShapeDtypeStruct((4, PACK_HALF, LANES), p.dtype),
        scratch_shapes=[pltpu.SemaphoreType.DMA((4,)), pltpu.SemaphoreType.DMA((4,))],
    )(p)


def _grad_add_cores(p, theirs, c1):
    tr = GRAD_TR

    def body(c_ref, a_ref, b_ref, o_ref):
        o_ref[...] = (a_ref[...] + b_ref[...]).astype(BF16)

    spec = pl.BlockSpec((None, tr, LANES), lambda j, r, c: (j, r, 0))
    grid_spec = pltpu.PrefetchScalarGridSpec(
        num_scalar_prefetch=1, grid=(4, PACK_HALF // tr),
        in_specs=[pl.BlockSpec((None, None, tr, LANES), lambda j, r, c: (j, c[0], r, 0)), spec], out_specs=spec)
    return pl.pallas_call(
        body, name="grad_add_cores", grid_spec=grid_spec, out_shape=jax.ShapeDtypeStruct(theirs.shape, BF16),
        compiler_params=_cparams(("parallel", "parallel")),
    )(c1, p.reshape(4, 2, PACK_HALF, LANES), theirs)


def _grad_chip_exchange(q):
    def body(q_ref, out_ref, send_sems, recv_sems):
        x, y, c = _place()
        me = 2 * x + y
        chips = _other_chips(x, y)
        sends = [pltpu.make_async_remote_copy(src_ref=q_ref.at[2 * chip[0] + chip[1]], dst_ref=out_ref.at[me],
                                              send_sem=send_sems.at[j], recv_sem=recv_sems.at[j],
                                              device_id=(*chip, c), device_id_type=MESH) for j, chip in enumerate(chips)]
        for cp in sends:
            cp.start()
        for j, chip in enumerate(chips):
            slot = out_ref.at[2 * chip[0] + chip[1]]
            pltpu.make_async_remote_copy(src_ref=slot, dst_ref=slot, send_sem=send_sems.at[j], recv_sem=recv_sems.at[j],
                                         device_id=(x, y, c), device_id_type=MESH).wait_recv()
        for cp in sends:
            cp.wait_send()

    return pl.pallas_call(
        body, name="grad_chip_exchange", in_specs=[_ANY], out_specs=_ANY,
        out_shape=jax.ShapeDtypeStruct(q.shape, q.dtype),
        scratch_shapes=[pltpu.SemaphoreType.DMA((3,)), pltpu.SemaphoreType.DMA((3,))],
    )(q)


def _grad_add_chips(q, slots, me1):
    tr = GRAD_TR

    def body(me_ref, own_ref, s0, s1, s2, s3, o_ref):
        me = me_ref[0]
        t = [jnp.where(me == j, own_ref[...], s[...]).astype(F32) for j, s in enumerate((s0, s1, s2, s3))]
        o_ref[...] = ((t[0] + t[1]) + t[2]) + t[3]

    def slot_spec(j):
        return pl.BlockSpec((None, tr, LANES), lambda r, me: (jnp.where(me[0] == j, (j + 1) % 4, j), r, 0))

    grid_spec = pltpu.PrefetchScalarGridSpec(
        num_scalar_prefetch=1, grid=(PACK_HALF // tr,),
        in_specs=[pl.BlockSpec((None, tr, LANES), lambda r, me: (me[0], r, 0))] + [slot_spec(j) for j in range(4)],
        out_specs=pl.BlockSpec((tr, LANES), lambda r, me: (r, 0)))
    return pl.pallas_call(
        body, name="grad_add_chips", grid_spec=grid_spec, out_shape=jax.ShapeDtypeStruct(q.shape[1:], F32),
        compiler_params=_cparams(("parallel",)),
    )(me1, q, slots, slots, slots, slots)


def _grad_core_gather(t):
    def body(t_ref, out_ref, send_sem, recv_sem):
        x, y, c = _place()
        give = pltpu.make_async_remote_copy(src_ref=t_ref, dst_ref=out_ref, send_sem=send_sem, recv_sem=recv_sem,
                                            device_id=(x, y, 1 - c), device_id_type=MESH)
        give.start()
        give.wait()

    return pl.pallas_call(
        body, name="grad_core_gather", in_specs=[_ANY], out_specs=_ANY,
        out_shape=jax.ShapeDtypeStruct(t.shape, t.dtype),
        scratch_shapes=[pltpu.SemaphoreType.DMA, pltpu.SemaphoreType.DMA],
    )(t)


def _small_allreduce(sp):
    def body(sp_ref, out_ref, gath_ref, send_sems, recv_sems):
        x, y, c = _place()
        me = 4 * x + 2 * y + c
        gath_ref[me] = sp_ref[...]
        peers = []
        for k in range(1, 8):
            px = 1 - x if k & 4 else x
            py = 1 - y if k & 2 else y
            pc = 1 - c if k & 1 else c
            peers.append((px, py, pc))
        sends = [pltpu.make_async_remote_copy(src_ref=sp_ref, dst_ref=gath_ref.at[me], send_sem=send_sems.at[k],
                                              recv_sem=recv_sems.at[k], device_id=peer, device_id_type=MESH)
                 for k, peer in enumerate(peers)]
        for cp in sends:
            cp.start()
        for k, (px, py, pc) in enumerate(peers):
            slot = gath_ref.at[4 * px + 2 * py + pc]
            pltpu.make_async_remote_copy(src_ref=slot, dst_ref=slot, send_sem=send_sems.at[k], recv_sem=recv_sems.at[k],
                                         device_id=(x, y, c), device_id_type=MESH).wait_recv()
        for cp in sends:
            cp.wait_send()
        tot = gath_ref[0]
        for d in range(1, 8):
            tot = tot + gath_ref[d]
        out_ref[...] = tot

    vm = pl.BlockSpec(memory_space=pltpu.VMEM)
    return pl.pallas_call(
        body, name="small_allreduce", in_specs=[vm], out_specs=vm, out_shape=jax.ShapeDtypeStruct(sp.shape, sp.dtype),
        scratch_shapes=[pltpu.VMEM((8,) + sp.shape, sp.dtype), pltpu.SemaphoreType.DMA((7,)), pltpu.SemaphoreType.DMA((7,))],
    )(sp)


def _adamw_update(w, gv, m, v):
    mn = ADAM_B1 * m + (1.0 - ADAM_B1) * gv
    vn = ADAM_B2 * v + (1.0 - ADAM_B2) * (gv * gv)
    m_hat = mn / (1.0 - ADAM_B1 ** ADAM_STEP)
    v_hat = vn / (1.0 - ADAM_B2 ** ADAM_STEP)
    return -ADAM_LR * (m_hat / (jnp.sqrt(v_hat) + ADAM_EPS) + ADAM_WD * w), mn, vn


def _adamw(w, g, m, v, name):
    rows, cols = w.shape
    tr = _pick(rows, (256, rows))

    def body(w_ref, g_ref, m_ref, v_ref, d_ref, mo_ref, vo_ref):
        d_ref[...], mo_ref[...], vo_ref[...] = _adamw_update(w_ref[...], g_ref[...], m_ref[...], v_ref[...])

    spec = pl.BlockSpec((tr, cols), lambda r: (r, 0))
    shp = jax.ShapeDtypeStruct(w.shape, F32)
    return pl.pallas_call(
        body, name=name, grid=(rows // tr,), in_specs=[spec] * 4, out_specs=[spec] * 3, out_shape=[shp] * 3,
        compiler_params=_cparams(("parallel",)),
    )(w, g, m, v)


MAT_NAMES = ("l0_w_in", "l0_w_q_b", "l0_w_kv_b", "l0_w_out", "l1_w_in", "l1_w_out")
VEC_NAMES = ("l0_pre_g", "l0_post_g", "l0_q_a_g", "l0_kv_a_g", "l1_pre_g", "l1_post_g", "l1_b_f")
WEIGHT_NAMES = ("l0_pre_g", "l0_post_g", "l0_w_in", "l0_q_a_g", "l0_w_q_b", "l0_kv_a_g", "l0_w_kv_b", "l0_w_out",
                "l1_pre_g", "l1_post_g", "l1_w_in", "l1_b_f", "l1_w_out")
MAT_SHARD = {"l0_w_in": (1024, 808), "l0_w_q_b": (384, 192), "l0_w_kv_b": (256, 256), "l0_w_out": (256, 1024),
             "l1_w_in": (1024, 1028), "l1_w_out": (256, 1024)}
ROW_SHARDED = ("l0_w_out", "l1_w_out")
VEC_LEN = {"l0_pre_g": 1024, "l0_post_g": 1024, "l0_q_a_g": 384, "l0_kv_a_g": 256, "l1_pre_g": 1024,
           "l1_post_g": 1024, "l1_b_f": 16}


def _mat_rows(n):
    r, c = MAT_SHARD[n]
    return r * c // LANES


def _pack_shards(shards):
    parts = [shards[n].reshape(_mat_rows(n), LANES) for n in MAT_NAMES]
    used = sum(_mat_rows(n) for n in MAT_NAMES)
    parts.append(jnp.zeros((PACK_ROWS - used, LANES), parts[0].dtype))
    return jnp.concatenate(parts, axis=0)


def _unpack_shards(pack):
    out, at = {}, 0
    for n in MAT_NAMES:
        out[n] = pack[..., at:at + _mat_rows(n), :].reshape(pack.shape[:-2] + MAT_SHARD[n])
        at += _mat_rows(n)
    return out


def _join_shards(n, s):
    if n in ROW_SHARDED:
        return s.reshape(4 * s.shape[1], s.shape[2])
    return s.transpose(1, 0, 2).reshape(s.shape[1], 4 * s.shape[2])


def _cut_shards(n, w):
    r, c = MAT_SHARD[n]
    if n in ROW_SHARDED:
        return w.reshape(4, r, c)
    return w.reshape(r, 4, c).transpose(1, 0, 2)


def _pack_vecs(vecs):
    parts = []
    for n in VEC_NAMES:
        v = vecs[n].reshape(-1)
        parts.append(jnp.pad(v, (0, VEC_ROWS * LANES - v.shape[0])).reshape(VEC_ROWS, LANES))
    return jnp.concatenate(parts, axis=0)


def _unpack_vecs(pack):
    return {n: pack[k * VEC_ROWS:(k + 1) * VEC_ROWS].reshape(-1)[:VEC_LEN[n]] for k, n in enumerate(VEC_NAMES)}


def kernel(x, positions, l0_pre_g, l0_post_g, l0_w_in, l0_q_a_g, l0_w_q_b, l0_kv_a_g, l0_w_kv_b, l0_w_out, l1_pre_g, l1_post_g, l1_w_in, l1_b_f, l1_w_out, loss_target, m_l0_pre_g, m_l0_post_g, m_l0_w_in, m_l0_q_a_g, m_l0_w_q_b, m_l0_kv_a_g, m_l0_w_kv_b, m_l0_w_out, m_l1_pre_g, m_l1_post_g, m_l1_w_in, m_l1_b_f, m_l1_w_out, v_l0_pre_g, v_l0_post_g, v_l0_w_in, v_l0_q_a_g, v_l0_w_q_b, v_l0_kv_a_g, v_l0_w_kv_b, v_l0_w_out, v_l1_pre_g, v_l1_post_g, v_l1_w_in, v_l1_b_f, v_l1_w_out):
    w = dict(l0_pre_g=l0_pre_g, l0_post_g=l0_post_g, l0_w_in=l0_w_in, l0_q_a_g=l0_q_a_g, l0_w_q_b=l0_w_q_b,
             l0_kv_a_g=l0_kv_a_g, l0_w_kv_b=l0_w_kv_b, l0_w_out=l0_w_out, l1_pre_g=l1_pre_g, l1_post_g=l1_post_g,
             l1_w_in=l1_w_in, l1_b_f=l1_b_f, l1_w_out=l1_w_out)
    m = dict(l0_pre_g=m_l0_pre_g, l0_post_g=m_l0_post_g, l0_w_in=m_l0_w_in, l0_q_a_g=m_l0_q_a_g, l0_w_q_b=m_l0_w_q_b,
             l0_kv_a_g=m_l0_kv_a_g, l0_w_kv_b=m_l0_w_kv_b, l0_w_out=m_l0_w_out, l1_pre_g=m_l1_pre_g,
             l1_post_g=m_l1_post_g, l1_w_in=m_l1_w_in, l1_b_f=m_l1_b_f, l1_w_out=m_l1_w_out)
    v = dict(l0_pre_g=v_l0_pre_g, l0_post_g=v_l0_post_g, l0_w_in=v_l0_w_in, l0_q_a_g=v_l0_q_a_g, l0_w_q_b=v_l0_w_q_b,
             l0_kv_a_g=v_l0_kv_a_g, l0_w_kv_b=v_l0_w_kv_b, l0_w_out=v_l0_w_out, l1_pre_g=v_l1_pre_g,
             l1_post_g=v_l1_post_g, l1_w_in=v_l1_w_in, l1_b_f=v_l1_b_f, l1_w_out=v_l1_w_out)

    cx, cy, cc = _place()
    me1 = jnp.reshape(2 * cx + cy, (1,)).astype(jnp.int32)
    c1 = jnp.reshape(cc, (1,)).astype(jnp.int32)
    w_bf = _pack_shards({n: w[n].astype(BF16) for n in MAT_NAMES})
    gathered = lax.dynamic_update_slice(_weight_gather(w_bf), w_bf[None], (2 * cx + cy, 0, 0))
    gathered = _unpack_shards(gathered)
    full = {n: _join_shards(n, gathered[n]) for n in MAT_NAMES}
    gains = {n: w[n].reshape(1, -1) for n in VEC_NAMES}

    lsum, dx0, grads = _local_step(
        x[0], positions[0], loss_target[0], gains, _pad_w0(full["l0_w_in"]), _pad_wq(full["l0_w_q_b"]),
        _pad_wkv(full["l0_w_kv_b"]), full["l0_w_out"], _pad_w1(full["l1_w_in"]), full["l1_w_out"])

    gfull = {"l0_w_in": _unpad_w0(grads["l0_w_in"]), "l0_w_q_b": _unpad_wq(grads["l0_w_q_b"]),
             "l0_w_kv_b": _unpad_wkv(grads["l0_w_kv_b"]), "l0_w_out": grads["l0_w_out"],
             "l1_w_in": grads["l1_w_in"][:, :ODD_IN_WIDTH], "l1_w_out": grads["l1_w_out"]}
    parts = [_cut_shards(n, gfull[n]).reshape(4, _mat_rows(n), LANES) for n in MAT_NAMES]
    used = sum(_mat_rows(n) for n in MAT_NAMES)
    parts.append(jnp.zeros((4, PACK_ROWS - used, LANES), F32))
    g_pack = jnp.concatenate(parts, axis=1)
    q_cores = _grad_add_cores(g_pack, _grad_core_exchange(g_pack), c1)
    g_mine = _grad_add_chips(q_cores, _grad_chip_exchange(q_cores), me1)
    g_theirs = _grad_core_gather(g_mine)

    small = _small_allreduce(jnp.concatenate([_pack_vecs({n: grads[n] for n in VEC_NAMES}),
                                              lsum.reshape(D_MODEL // LANES, LANES)], axis=0))
    g_small = small[:SMALL_ROWS]
    loss = 0.5 * jnp.sum(small[SMALL_ROWS:]) / float(D_MODEL)

    first = lax.select(cc == 0, g_mine, g_theirs)
    second = lax.select(cc == 0, g_theirs, g_mine)
    g_mats = _unpack_shards(jnp.concatenate([first, second], axis=0))
    d_mats, m_mats, v_mats = {}, {}, {}
    for n in MAT_NAMES:
        d_mats[n], m_mats[n], v_mats[n] = _adamw(w[n], g_mats[n], m[n], v[n], "adamw_" + n)
    d_small, m_small, v_small = _adamw(_pack_vecs(w), g_small, _pack_vecs(m), _pack_vecs(v), "adamw_vecs")

    def leaves(mats, vec_pack):
        out = dict(mats)
        out.update(_unpack_vecs(vec_pack))
        return [out[n] for n in WEIGHT_NAMES]

    return (loss, dx0[None], *leaves(g_mats, g_small), *leaves(d_mats, d_small), *leaves(m_mats, m_small),
            *leaves(v_mats, v_small))
```

```python
import jax
import jax.numpy as jnp
from jax import lax
from jax.experimental import pallas as pl
from jax.experimental.pallas import tpu as pltpu

F32 = jnp.float32
BF16 = jnp.bfloat16
MESH = pl.DeviceIdType.MESH

D_MODEL = 1024
RMS_EPS = 1e-6
ROPE_THETA = 10000.0
SB_WIDTH = 512
MLA_Q_LORA = 384
MLA_KV_LORA = 256
MLA_ROPE_DIM = 32
MLA_WIDTH = 512
FOX_WIDTH = 1024
FOX_HEADS = 16
EVEN_IN_WIDTH = 3232
ODD_IN_WIDTH = 4112

ADAM_LR = 0.001
ADAM_B1 = 0.9
ADAM_B2 = 0.999
ADAM_EPS = 1e-08
ADAM_WD = 0.01
ADAM_STEP = 10

LANES = 128
VMEM_LIMIT = 56 * 1024 * 1024

L0_PREP = 0
L0_PREP_W = 768
L0_SBG = 768
L0_MLG = 1280
L0_SBQ = 1792
L0_SBK = 2304
L0_SBV = 2816
L0_WIDTH = 3328
L1_Q = 0
L1_K = 1024
L1_V = 2048
L1_G = 3072
L1_F = 4096
L1_WIDTH = 4224

ATT_T = 256
ATT_GROUP = 4
NEG = -1e30

PACK_ROWS = 20480
PACK_HALF = PACK_ROWS // 2
VEC_ROWS = 8
SMALL_ROWS = 7 * VEC_ROWS


def _cparams(sem, **kw):
    return pltpu.CompilerParams(dimension_semantics=sem, vmem_limit_bytes=VMEM_LIMIT, **kw)


def _dot(a, b):
    return lax.dot_general(a, b, (((1,), (0,)), ((), ())), preferred_element_type=F32)


def _dot_nt(a, b):
    return lax.dot_general(a, b, (((1,), (1,)), ((), ())), preferred_element_type=F32)


def _sigmoid(x):
    return 1.0 / (1.0 + jnp.exp(-x))


def _rstd(x):
    return lax.rsqrt(jnp.mean(x * x, axis=-1, keepdims=True) + RMS_EPS)


def _norm_bwd(x, g, dy):
    r = _rstd(x)
    xn = x * r
    dxn = dy * g
    dx = r * (dxn - xn * jnp.mean(dxn * xn, axis=-1, keepdims=True))
    return dx, dy * xn


def _split3(x):
    hi = x.astype(BF16)
    r1 = x - hi.astype(F32)
    mid = r1.astype(BF16)
    lo = (r1 - mid.astype(F32)).astype(BF16)
    return hi, mid, lo


def _wide_tile(n, cap=1792):
    return max(t for t in range(LANES, min(n, cap) + 1, LANES) if n % t == 0)


def _pick(n, cands):
    for c in cands:
        if n % c == 0:
            return c
    raise ValueError(n)


def _norm_matmul(x, g, w, name):
    S, K = x.shape
    N = w.shape[1]
    tm = _pick(S, (512, 256))
    tn = _wide_tile(N)

    def body(x_ref, g_ref, w_ref, o_ref, ht_ref, h_ref):
        @pl.when(pl.program_id(1) == 0)
        def _():
            xv = x_ref[...]
            h = (xv * _rstd(xv)) * g_ref[...]
            h_ref[...] = h.astype(BF16)
            ht_ref[...] = h.T.astype(BF16)
        o_ref[...] = _dot(h_ref[...], w_ref[...])

    return pl.pallas_call(
        body, name=name, grid=(S // tm, N // tn),
        in_specs=[pl.BlockSpec((tm, K), lambda i, j: (i, 0)),
                  pl.BlockSpec((1, K), lambda i, j: (0, 0)),
                  pl.BlockSpec((K, tn), lambda i, j: (0, j))],
        out_specs=[pl.BlockSpec((tm, tn), lambda i, j: (i, j)),
                   pl.BlockSpec((K, tm), lambda i, j: (0, i))],
        out_shape=[jax.ShapeDtypeStruct((S, N), F32), jax.ShapeDtypeStruct((K, S), BF16)],
        scratch_shapes=[pltpu.VMEM((tm, K), BF16)],
        compiler_params=_cparams(("parallel", "arbitrary")),
    )(x, g, w)


def _matmul_t(at, b, name):
    M, S = at.shape
    N = b.shape[1]
    tn = _wide_tile(N)
    ts = _pick(S, (512, 256))

    def body(a_ref, b_ref, o_ref):
        @pl.when(pl.program_id(1) == 0)
        def _():
            o_ref[...] = jnp.zeros_like(o_ref)
        o_ref[...] += _dot(a_ref[...], b_ref[...].astype(BF16))

    return pl.pallas_call(
        body, name=name, grid=(N // tn, S // ts),
        in_specs=[pl.BlockSpec((M, ts), lambda j, k: (0, k)),
                  pl.BlockSpec((ts, tn), lambda j, k: (k, j))],
        out_specs=pl.BlockSpec((M, tn), lambda j, k: (0, j)),
        out_shape=jax.ShapeDtypeStruct((M, N), F32),
        compiler_params=_cparams(("parallel", "arbitrary")),
    )(at, b)


def _matmul_t_many(at, bs, name):
    M, S = at.shape
    ts = _pick(S, (512, 256))
    n = len(bs)

    def body(*refs):
        a_ref, b_refs, o_refs = refs[0], refs[1:1 + n], refs[1 + n:]

        @pl.when(pl.program_id(0) == 0)
        def _():
            for o_ref in o_refs:
                o_ref[...] = jnp.zeros_like(o_ref)

        a = a_ref[...]
        for b_ref, o_ref in zip(b_refs, o_refs):
            o_ref[...] += _dot(a, b_ref[...].astype(BF16))

    return pl.pallas_call(
        body, name=name, grid=(S // ts,),
        in_specs=[pl.BlockSpec((M, ts), lambda k: (0, k))] + [pl.BlockSpec((ts, b.shape[1]), lambda k: (k, 0)) for b in bs],
        out_specs=[pl.BlockSpec((M, b.shape[1]), lambda k: (0, 0)) for b in bs],
        out_shape=[jax.ShapeDtypeStruct((M, b.shape[1]), F32) for b in bs],
        compiler_params=_cparams(("arbitrary",)),
    )(at, *bs)


def _in_proj_bwd(pieces, w, x, g, dx_up, name):
    S, K = x.shape
    N = w.shape[1]
    tm = _pick(S, (256,))
    offs = [off for off, _ in pieces]
    arrs = [a for _, a in pieces]

    def body(*refs):
        d_refs = refs[:len(arrs)]
        w_ref, x_ref, g_ref, u_ref, dx_ref, dg_ref = refs[len(arrs):]

        @pl.when(pl.program_id(0) == 0)
        def _():
            dg_ref[...] = jnp.zeros_like(dg_ref)

        acc = None
        for off, d_ref in zip(offs, d_refs):
            part = _dot_nt(d_ref[...].astype(BF16), w_ref[:, off:off + d_ref.shape[1]])
            acc = part if acc is None else acc + part
        dx, dgrow = _norm_bwd(x_ref[...], g_ref[...], acc)
        dx_ref[...] = u_ref[...] + dx
        dg_ref[...] += jnp.sum(dgrow, axis=0, keepdims=True)

    row = lambda i: (i, 0)
    fixed = lambda i: (0, 0)
    return pl.pallas_call(
        body, name=name, grid=(S // tm,),
        in_specs=[pl.BlockSpec((tm, a.shape[1]), row) for a in arrs] + [
            pl.BlockSpec((K, N), fixed), pl.BlockSpec((tm, K), row), pl.BlockSpec((1, K), fixed),
            pl.BlockSpec((tm, K), row)],
        out_specs=[pl.BlockSpec((tm, K), row), pl.BlockSpec((1, K), fixed)],
        out_shape=[jax.ShapeDtypeStruct((S, K), F32), jax.ShapeDtypeStruct((1, K), F32)],
        compiler_params=_cparams(("arbitrary",)),
    )(*arrs, w, x, g, dx_up)


def _out_proj(og_a, og_b, blk_a, blk_b, w, x, g, target, name):
    S = x.shape[0]
    D = x.shape[1]
    tm = _pick(S, (512, 256))
    with_loss = target is not None

    def body(*refs):
        if with_loss:
            a_ref, b_ref, wa_ref, wb_ref, x_ref, g_ref, t_ref, y_ref, o_ref, l_ref = refs
        else:
            a_ref, b_ref, wa_ref, wb_ref, x_ref, g_ref, y_ref, o_ref = refs
        y = _dot(a_ref[...], wa_ref[...]) + _dot(b_ref[...], wb_ref[...])
        y_ref[...] = y
        xn = x_ref[...] + (y * _rstd(y)) * g_ref[...]
        if with_loss:
            @pl.when(pl.program_id(0) == 0)
            def _():
                l_ref[...] = jnp.zeros_like(l_ref)
            d = xn - t_ref[...]
            o_ref[...] = d / float(D)
            l_ref[...] += jnp.sum(d * d, axis=0, keepdims=True)
        else:
            o_ref[...] = xn

    row = lambda i: (i, 0)
    in_specs = [pl.BlockSpec((tm, 512), lambda i: (i, blk_a)),
                pl.BlockSpec((tm, 512), lambda i: (i, blk_b)),
                pl.BlockSpec((512, D), lambda i: (0, 0)),
                pl.BlockSpec((512, D), lambda i: (1, 0)),
                pl.BlockSpec((tm, D), row),
                pl.BlockSpec((1, D), lambda i: (0, 0))]
    out_specs = [pl.BlockSpec((tm, D), row), pl.BlockSpec((tm, D), row)]
    out_shape = [jax.ShapeDtypeStruct((S, D), F32), jax.ShapeDtypeStruct((S, D), F32)]
    args = [og_a, og_b, w, w, x, g]
    if with_loss:
        in_specs.append(pl.BlockSpec((tm, D), row))
        out_specs.append(pl.BlockSpec((1, D), lambda i: (0, 0)))
        out_shape.append(jax.ShapeDtypeStruct((1, D), F32))
        args.append(target)
    return pl.pallas_call(
        body, name=name, grid=(S // tm,), in_specs=in_specs, out_specs=out_specs, out_shape=out_shape,
        compiler_params=_cparams(("arbitrary",)),
    )(*args)


def _out_proj_bwd(dx_up, y, g, w, proj, gate_offs, o_a, o_b, oblk_a, oblk_b, name):
    S, D = y.shape
    tm = _pick(S, (256,))
    gblk = [off // 256 + c for off in gate_offs for c in range(2)]

    def body(u_ref, y_ref, g_ref, w_ref, g0, g1, g2, g3, oa_ref, ob_ref, dy_ref, do_ref, dgate_ref, dg_ref):
        @pl.when(pl.program_id(0) == 0)
        def _():
            dg_ref[...] = jnp.zeros_like(dg_ref)
        dy, dgrow = _norm_bwd(y_ref[...], g_ref[...], u_ref[...])
        dg_ref[...] += jnp.sum(dgrow, axis=0, keepdims=True)
        dyb = dy.astype(BF16)
        dy_ref[...] = dyb
        dog = _dot_nt(dyb, w_ref[...])
        gates = (g0, g1, g2, g3)
        for c in range(4):
            gt = gates[c][...]
            sg = _sigmoid(gt)
            o_ref = oa_ref if c < 2 else ob_ref
            ov = o_ref[:, (c % 2) * 256:(c % 2 + 1) * 256]
            dc = dog[:, c * 256:(c + 1) * 256]
            do_ref[:, c * 256:(c + 1) * 256] = dc * (gt * sg)
            dgate_ref[:, c * 256:(c + 1) * 256] = dc * ov * (sg * (1.0 + gt * (1.0 - sg)))

    row = lambda i: (i, 0)
    gspec = lambda c: pl.BlockSpec((tm, 256), lambda i: (i, gblk[c]))
    return pl.pallas_call(
        body, name=name, grid=(S // tm,),
        in_specs=[pl.BlockSpec((tm, D), row), pl.BlockSpec((tm, D), row), pl.BlockSpec((1, D), lambda i: (0, 0)),
                  pl.BlockSpec((D, D), lambda i: (0, 0)),
                  gspec(0), gspec(1), gspec(2), gspec(3),
                  pl.BlockSpec((tm, 512), lambda i: (i, oblk_a)),
                  pl.BlockSpec((tm, 512), lambda i: (i, oblk_b))],
        out_specs=[pl.BlockSpec((tm, D), row), pl.BlockSpec((tm, D), row), pl.BlockSpec((tm, D), row),
                   pl.BlockSpec((1, D), lambda i: (0, 0))],
        out_shape=[jax.ShapeDtypeStruct((S, D), BF16), jax.ShapeDtypeStruct((S, D), F32),
                   jax.ShapeDtypeStruct((S, D), F32), jax.ShapeDtypeStruct((1, D), F32)],
        compiler_params=_cparams(("arbitrary",)),
    )(dx_up, y, g, w, proj, proj, proj, proj, o_a, o_b)


def _rope_tables(pos, invf, name):
    S = pos.shape[0]
    tm = _pick(S, (512, 256))

    def body(p_ref, f_ref, c_ref, s1_ref, s2_ref):
        lane = lax.broadcasted_iota(jnp.int32, (1, LANES), 1)
        ang = p_ref[...].astype(F32) * f_ref[...]
        c, s = jnp.cos(ang), jnp.sin(ang)
        c_ref[...] = jnp.where((lane >= 64) & (lane < 96), c, 1.0)
        s1_ref[...] = jnp.where((lane >= 64) & (lane < 80), -s, 0.0)
        s2_ref[...] = jnp.where((lane >= 80) & (lane < 96), s, 0.0)

    spec = pl.BlockSpec((tm, LANES), lambda i: (i, 0))
    return pl.pallas_call(
        body, name=name, grid=(S // tm,),
        in_specs=[pl.BlockSpec((tm, 1), lambda i: (i, 0)), pl.BlockSpec((1, LANES), lambda i: (0, 0))],
        out_specs=[spec, spec, spec],
        out_shape=[jax.ShapeDtypeStruct((S, LANES), F32)] * 3,
        compiler_params=_cparams(("parallel",)),
    )(pos, invf)


def _rope(x, c, s1, s2):
    return x * c + pltpu.roll(x, LANES - 16, 1) * s1 + pltpu.roll(x, 16, 1) * s2


def _rope_t(d, c, s1, s2):
    return d * c + pltpu.roll(d * s1, 16, 1) + pltpu.roll(d * s2, LANES - 16, 1)


def _mla_prep(proj, gq, gkv, wq, wkv, cosT, s1T, s2T, name):
    S = proj.shape[0]
    tm = _pick(S, (256,))

    def body(p_ref, gq_ref, gkv_ref, wq_ref, wkv_ref, c_ref, s1_ref, s2_ref, q_ref, k_ref, v_ref, qn_ref, cn_ref):
        qa = p_ref[:, 0:384]
        ckv = p_ref[:, 384:640]
        kr = p_ref[:, 640:768]
        qn32 = (qa * _rstd(qa)) * gq_ref[...]
        cn32 = (ckv * _rstd(ckv)) * gkv_ref[...]
        qn = qn32.astype(BF16)
        cn = cn32.astype(BF16)
        qn_ref[...] = qn32.T.astype(BF16)
        cn_ref[...] = cn32.T.astype(BF16)
        qb = _dot(qn, wq_ref[...])
        kvb = _dot(cn, wkv_ref[...])
        c, s1, s2 = c_ref[...], s1_ref[...], s2_ref[...]
        krr = _rope(kr, c, s1, s2)
        for h in range(8):
            sl = slice(h * LANES, (h + 1) * LANES)
            q_ref[:, sl] = _rope(qb[:, sl], c, s1, s2)
            k_ref[:, sl] = kvb[:, sl] + krr
        v_ref[...] = kvb[:, 1024:1536]

    row = lambda i: (i, 0)
    fixed = lambda i: (0, 0)
    tspec = pl.BlockSpec((tm, LANES), row)
    return pl.pallas_call(
        body, name=name, grid=(S // tm,),
        in_specs=[pl.BlockSpec((tm, L0_PREP_W), lambda i: (i, L0_PREP // L0_PREP_W)),
                  pl.BlockSpec((1, 384), fixed), pl.BlockSpec((1, 256), fixed),
                  pl.BlockSpec((384, 1024), fixed), pl.BlockSpec((256, 1536), fixed), tspec, tspec, tspec],
        out_specs=[pl.BlockSpec((tm, 1024), row), pl.BlockSpec((tm, 1024), row), pl.BlockSpec((tm, 512), row),
                   pl.BlockSpec((384, tm), lambda i: (0, i)), pl.BlockSpec((256, tm), lambda i: (0, i))],
        out_shape=[jax.ShapeDtypeStruct((S, 1024), F32), jax.ShapeDtypeStruct((S, 1024), F32),
                   jax.ShapeDtypeStruct((S, 512), F32), jax.ShapeDtypeStruct((384, S), BF16),
                   jax.ShapeDtypeStruct((256, S), BF16)],
        compiler_params=_cparams(("parallel",)),
    )(proj, gq, gkv, wq, wkv, cosT, s1T, s2T)


def _mla_prep_bwd(dq, dk, dv, proj, gq, gkv, wq, wkv, cosT, s1T, s2T, name):
    S = proj.shape[0]
    tm = _pick(S, (256,))

    def body(dq_ref, dk_ref, dv_ref, p_ref, gq_ref, gkv_ref, wq_ref, wkv_ref, c_ref, s1_ref, s2_ref,
             dp_ref, dqb_ref, dkvb_ref, dgq_ref, dgkv_ref):
        @pl.when(pl.program_id(0) == 0)
        def _():
            dgq_ref[...] = jnp.zeros_like(dgq_ref)
            dgkv_ref[...] = jnp.zeros_like(dgkv_ref)
        c, s1, s2 = c_ref[...], s1_ref[...], s2_ref[...]
        lane = lax.broadcasted_iota(jnp.int32, (1, LANES), 1)
        dkr = jnp.zeros((tm, LANES), F32)
        for h in range(8):
            sl = slice(h * LANES, (h + 1) * LANES)
            dqb_ref[:, sl] = _rope_t(dq_ref[:, sl], c, s1, s2).astype(BF16)
            dkh = dk_ref[:, sl]
            dkvb_ref[:, sl] = dkh.astype(BF16)
            dkr = dkr + dkh
        dkvb_ref[:, 1024:1536] = dv_ref[...].astype(BF16)
        dkr = jnp.where((lane >= 64) & (lane < 96), _rope_t(dkr, c, s1, s2), 0.0)
        dqn = _dot_nt(dqb_ref[...], wq_ref[...])
        dcn = _dot_nt(dkvb_ref[...], wkv_ref[...])
        dqa, gq_row = _norm_bwd(p_ref[:, 0:384], gq_ref[...], dqn)
        dckv, gkv_row = _norm_bwd(p_ref[:, 384:640], gkv_ref[...], dcn)
        dp_ref[:, 0:384] = dqa
        dp_ref[:, 384:640] = dckv
        dp_ref[:, 640:768] = dkr
        dgq_ref[...] += jnp.sum(gq_row, axis=0, keepdims=True)
        dgkv_ref[...] += jnp.sum(gkv_row, axis=0, keepdims=True)

    row = lambda i: (i, 0)
    fixed = lambda i: (0, 0)
    tspec = pl.BlockSpec((tm, LANES), row)
    return pl.pallas_call(
        body, name=name, grid=(S // tm,),
        in_specs=[pl.BlockSpec((tm, 1024), row), pl.BlockSpec((tm, 1024), row), pl.BlockSpec((tm, 512), row),
                  pl.BlockSpec((tm, L0_PREP_W), lambda i: (i, L0_PREP // L0_PREP_W)),
                  pl.BlockSpec((1, 384), fixed), pl.BlockSpec((1, 256), fixed),
                  pl.BlockSpec((384, 1024), fixed), pl.BlockSpec((256, 1536), fixed), tspec, tspec, tspec],
        out_specs=[pl.BlockSpec((tm, L0_PREP_W), row), pl.BlockSpec((tm, 1024), row), pl.BlockSpec((tm, 1536), row),
                   pl.BlockSpec((1, 384), fixed), pl.BlockSpec((1, 256), fixed)],
        out_shape=[jax.ShapeDtypeStruct((S, L0_PREP_W), F32), jax.ShapeDtypeStruct((S, 1024), BF16),
                   jax.ShapeDtypeStruct((S, 1536), BF16), jax.ShapeDtypeStruct((1, 384), F32),
                   jax.ShapeDtypeStruct((1, 256), F32)],
        compiler_params=_cparams(("arbitrary",)),
    )(dq, dk, dv, proj, gq, gkv, wq, wkv, cosT, s1T, s2T)


def _fox_prep(proj, bf, name):
    S = proj.shape[0]
    tm = _pick(S, (256,))

    def body(f_ref, b_ref, c_ref, carry_ref):
        @pl.when(pl.program_id(0) == 0)
        def _():
            carry_ref[...] = jnp.zeros_like(carry_ref)
        u = f_ref[...] + b_ref[...]
        lf = jnp.minimum(u, 0.0) - jnp.log(1.0 + jnp.exp(-jnp.abs(u)))
        r = lax.broadcasted_iota(jnp.int32, (tm, tm), 0)
        cidx = lax.broadcasted_iota(jnp.int32, (tm, tm), 1)
        tri = (cidx <= r).astype(BF16)
        hi, mid, lo = _split3(lf)
        c = carry_ref[...] + (_dot(tri, hi) + _dot(tri, mid) + _dot(tri, lo))
        c_ref[...] = c
        carry_ref[...] = c[tm - 1:tm, :]

    return pl.pallas_call(
        body, name=name, grid=(S // tm,),
        in_specs=[pl.BlockSpec((tm, LANES), lambda i: (i, L1_F // LANES)), pl.BlockSpec((1, LANES), lambda i: (0, 0))],
        out_specs=pl.BlockSpec((tm, LANES), lambda i: (i, 0)),
        out_shape=jax.ShapeDtypeStruct((S, LANES), F32),
        scratch_shapes=[pltpu.VMEM((1, LANES), F32)],
        compiler_params=_cparams(("arbitrary",)),
    )(proj, bf)


def _fox_prep_bwd(dc, proj, bf, name):
    S = proj.shape[0]
    tm = _pick(S, (256,))
    nb = S // tm

    def body(dc_ref, f_ref, b_ref, df_ref, db_ref, carry_ref):
        @pl.when(pl.program_id(0) == 0)
        def _():
            carry_ref[...] = jnp.zeros_like(carry_ref)
            db_ref[...] = jnp.zeros_like(db_ref)
        r = lax.broadcasted_iota(jnp.int32, (tm, tm), 0)
        cidx = lax.broadcasted_iota(jnp.int32, (tm, tm), 1)
        tri = (cidx >= r).astype(BF16)
        hi, mid, lo = _split3(dc_ref[...])
        dlf = carry_ref[...] + (_dot(tri, hi) + _dot(tri, mid) + _dot(tri, lo))
        carry_ref[...] = dlf[0:1, :]
        u = f_ref[...] + b_ref[...]
        e = jnp.exp(-jnp.abs(u))
        sneg = jnp.where(u >= 0.0, e, 1.0) / (1.0 + e)
        lane = lax.broadcasted_iota(jnp.int32, (1, LANES), 1)
        df = jnp.where(lane < FOX_HEADS, dlf * sneg, 0.0)
        df_ref[...] = df
        db_ref[...] += jnp.sum(df, axis=0, keepdims=True)

    return pl.pallas_call(
        body, name=name, grid=(nb,),
        in_specs=[pl.BlockSpec((tm, LANES), lambda i: (nb - 1 - i, 0)),
                  pl.BlockSpec((tm, LANES), lambda i: (nb - 1 - i, L1_F // LANES)),
                  pl.BlockSpec((1, LANES), lambda i: (0, 0))],
        out_specs=[pl.BlockSpec((tm, LANES), lambda i: (nb - 1 - i, 0)), pl.BlockSpec((1, LANES), lambda i: (0, 0))],
        out_shape=[jax.ShapeDtypeStruct((S, LANES), F32), jax.ShapeDtypeStruct((1, LANES), F32)],
        scratch_shapes=[pltpu.VMEM((1, LANES), F32)],
        compiler_params=_cparams(("arbitrary",)),
    )(dc, proj, bf)


def _att_specs(kind, S, T):
    if kind == "sb":
        qo, ko, vo, go = L0_SBQ // LANES, L0_SBK // LANES, L0_SBV // LANES, L0_SBG // LANES
    elif kind == "fox":
        qo, ko, vo, go = L1_Q // LANES, L1_K // LANES, L1_V // LANES, L1_G // LANES
    else:
        go = L0_MLG // LANES
        return (pl.BlockSpec((T, 256), lambda p, i: (i, p)), pl.BlockSpec((S, 256), lambda p, i: (0, p)),
                pl.BlockSpec((S, LANES), lambda p, i: (0, p)), pl.BlockSpec((T, LANES), lambda p, i: (i, go + p)))
    return (pl.BlockSpec((T, LANES), lambda p, i: (i, qo + p)), pl.BlockSpec((S, LANES), lambda p, i: (0, ko + p)),
            pl.BlockSpec((S, LANES), lambda p, i: (0, vo + p)), pl.BlockSpec((T, LANES), lambda p, i: (i, go + p)))


def _mask_flags(js, masked_at):
    return [t == masked_at for t in range(len(js))]


def _loop_tiles(i, tiles, right_to_left, G=ATT_GROUP):
    ng = i // G
    rest = i - ng * G

    def leftover():
        for r in range(G):
            @pl.when(rest == r)
            def _():
                if right_to_left:
                    tiles([i - u for u in range(r + 1)], 0)
                else:
                    tiles([ng * G + u for u in range(r + 1)], r)

    def group(g, carry):
        if right_to_left:
            tiles([ng * G - 1 - (g * G + u) for u in range(G)], None)
        else:
            tiles([g * G + u for u in range(G)], None)
        return carry

    if right_to_left:
        leftover()
    lax.fori_loop(0, ng, group, 0)
    if not right_to_left:
        leftover()


def _head_q(kind, q_ref, m0, scale):
    if kind == "mla":
        return [q_ref[:, 0:LANES].astype(BF16), q_ref[:, LANES:2 * LANES].astype(BF16)]
    qv = q_ref[...] * scale
    return [jnp.where(m0, qv, 0.0).astype(BF16), jnp.where(m0, 0.0, qv).astype(BF16)]


def _head_k(kind, k_ref, start, T):
    if kind == "mla":
        return [k_ref[pl.ds(start, T), 0:LANES].astype(BF16), k_ref[pl.ds(start, T), LANES:2 * LANES].astype(BF16)]
    kb = k_ref[pl.ds(start, T), :].astype(BF16)
    return [kb, kb]


def _transpose_tiles(src, col_off, n_out, cw, group, name):
    S = src.shape[0]
    T = ATT_T
    first = col_off // (group * cw)

    def body(x_ref, o_ref):
        for u in range(group):
            o_ref[u] = x_ref[:, u * cw:(u + 1) * cw].T.astype(BF16)

    return pl.pallas_call(
        body, name=name, grid=(S // T, n_out // group),
        in_specs=[pl.BlockSpec((T, group * cw), lambda j, g: (j, first + g))],
        out_specs=pl.BlockSpec((group, None, cw, T), lambda j, g: (g, j, 0, 0)),
        out_shape=jax.ShapeDtypeStruct((n_out, S // T, cw, T), BF16),
        compiler_params=_cparams(("parallel", "parallel")),
    )(src)


def _softmax_fwd(kind, qkvg, c_col, S, npairs, name):
    T = ATT_T
    nq = S // T
    fox = kind == "fox"
    scale = (96 if kind == "mla" else 64) ** -0.5

    def body(*refs):
        if fox:
            q_ref, k_ref, vt_ref, g_ref, cc_ref, o_ref, og_ref, ogt_ref, st_ref, m_ref, acc_ref = refs
        else:
            q_ref, k_ref, vt_ref, g_ref, o_ref, og_ref, ogt_ref, st_ref, m_ref, acc_ref = refs
        i = pl.program_id(1)
        m0 = lax.broadcasted_iota(jnp.int32, (1, LANES), 1) < 64
        top = lax.broadcasted_iota(jnp.int32, (LANES, 1), 0) < 64
        key = lax.broadcasted_iota(jnp.int32, (T, LANES), 0)
        qrow = lax.broadcasted_iota(jnp.int32, (T, LANES), 1)
        qh = _head_q(kind, q_ref, m0, scale)
        m_ref[...] = jnp.full(m_ref.shape, NEG, F32)
        acc_ref[...] = jnp.zeros(acc_ref.shape, F32)
        chains = [(h, b) for h in range(2) for b in range(T // LANES)]

        def tiles(js, masked_at):
            starts = [pl.multiple_of(j * T, T) for j in js]
            zss = []
            for start in starts:
                kh = _head_k(kind, k_ref, start, T)
                zss.append(_split_blocks([_dot_nt(kh[h], qh[h]) for h in range(2)]))
            pss, alss = [], []
            for start, zs, masked in zip(starts, zss, _mask_flags(js, masked_at)):
                ps, alphas = [], []
                for (h, b), z in zip(chains, zs):
                    lanes = slice(b * LANES, (b + 1) * LANES)
                    if kind == "mla":
                        z = z * scale
                    if fox:
                        z = z - cc_ref[h, pl.ds(start, T), :]
                    if masked:
                        z = jnp.where(key <= qrow + b * LANES, z, NEG)
                    m_prev = m_ref[h, :, lanes]
                    m_new = jnp.maximum(m_prev, jnp.max(z, axis=0, keepdims=True))
                    alphas.append(jnp.exp(m_prev - m_new))
                    ps.append(jnp.exp(z - m_new).astype(BF16))
                    m_ref[h, :, lanes] = m_new
                pss.append(_join_blocks(ps, T // LANES))
                alss.append(_join_blocks(alphas, T // LANES))
            for j, ps, alphas in zip(js, pss, alss):
                vt = vt_ref[j]
                vth = [jnp.where(top, vt, 1.0).astype(BF16), jnp.where(top, 1.0, vt).astype(BF16)]
                for h in range(2):
                    acc_ref[h] = alphas[h] * acc_ref[h] + _dot(vth[h], ps[h])

        _loop_tiles(i, tiles, False, 2 * ATT_GROUP)
        acc = [acc_ref[0], acc_ref[1]]
        ot = jnp.concatenate([acc[0][0:64] / acc[0][64:128], acc[1][64:128] / acc[1][0:64]], axis=0)
        o = ot.T
        o_ref[...] = o
        gt = g_ref[...]
        og = o * (gt * _sigmoid(gt))
        og_ref[...] = og.astype(BF16)
        ogt_ref[...] = og.T.astype(BF16)
        st_ref[0] = m_ref[0] + jnp.log(acc[0][64:65])
        st_ref[1] = m_ref[1] + jnp.log(acc[1][0:1])

    qs, ks, _, gs = _att_specs(kind, S, T)
    in_specs = [qs, ks, pl.BlockSpec((None, nq, LANES, T), lambda p, i: (p, 0, 0, 0)), gs]
    args = list(qkvg)
    if fox:
        in_specs += [pl.BlockSpec((2, S, LANES), lambda p, i: (p, 0, 0))]
        args += [c_col]
    W = npairs * LANES
    return pl.pallas_call(
        body, name=name, grid=(npairs, nq), in_specs=in_specs,
        out_specs=[pl.BlockSpec((T, LANES), lambda p, i: (i, p)), pl.BlockSpec((T, LANES), lambda p, i: (i, p)),
                   pl.BlockSpec((LANES, T), lambda p, i: (p, i)),
                   pl.BlockSpec((2, None, 1, T), lambda p, i: (p, i, 0, 0))],
        out_shape=[jax.ShapeDtypeStruct((S, W), F32), jax.ShapeDtypeStruct((S, W), BF16),
                   jax.ShapeDtypeStruct((W, S), BF16),
                   jax.ShapeDtypeStruct((2 * npairs, nq, 1, T), F32)],
        scratch_shapes=[pltpu.VMEM((2, 1, T), F32), pltpu.VMEM((2, LANES, T), F32)],
        compiler_params=_cparams(("parallel", "parallel")),
    )(*args)


def _softplus_parts(z):
    sp = jnp.maximum(z, 0.0) + jnp.log(1.0 + jnp.exp(-jnp.abs(z)))
    return -sp, z - sp


def _split2(x):
    hi = x.astype(BF16)
    return hi, (x - hi.astype(F32)).astype(BF16)


def _split_blocks(per_head):
    return [x[:, b * LANES:(b + 1) * LANES] for x in per_head for b in range(x.shape[1] // LANES)]


def _join_blocks(per_block, nb):
    return [jnp.concatenate(per_block[h * nb:(h + 1) * nb], axis=1) for h in range(len(per_block) // nb)]


def _row_of(col):
    return jnp.broadcast_to(col, (col.shape[0], LANES)).T[0:1]


def _softmax_bwd_t(kind, q, k, kt, v, do, do_off, o, lse, c_col, S, npairs, name):
    T = ATT_T
    nq = S // T
    nb = T // LANES
    fox = kind == "fox"
    mla = kind == "mla"
    scale = (96 if mla else 64) ** -0.5
    kw = 256 if mla else LANES

    def body(*refs):
        if fox:
            (q_ref, k_ref, kt_ref, v_ref, do_ref, o_ref, st_ref, cc_ref,
             dq_ref, dk_ref, dv_ref, dck_ref, dcq_ref, dqt_ref, rs_ref, dkx_ref) = refs
        else:
            q_ref, k_ref, kt_ref, v_ref, do_ref, o_ref, st_ref, dq_ref, dk_ref, dv_ref, dqt_ref = refs
        i = pl.program_id(1)

        @pl.when(i == 0)
        def _():
            dv_ref[...] = jnp.zeros_like(dv_ref)
            if fox:
                dkx_ref[...] = jnp.zeros_like(dkx_ref)
            else:
                dk_ref[...] = jnp.zeros_like(dk_ref)

        m0 = lax.broadcasted_iota(jnp.int32, (1, LANES), 1) < 64
        top = lax.broadcasted_iota(jnp.int32, (LANES, 1), 0) < 64
        key = lax.broadcasted_iota(jnp.int32, (T, LANES), 0)
        qrow = lax.broadcasted_iota(jnp.int32, (T, LANES), 1)
        qh = _head_q(kind, q_ref, m0, scale)
        if fox:
            qv = q_ref[...] * scale
            qk = [jnp.where(m0, qv, 1.0).astype(BF16), jnp.where(m0, 1.0, qv).astype(BF16)]
        else:
            qk = qh
        dov = do_ref[...]
        prod = dov * o_ref[...]
        dd = [_row_of(jnp.sum(jnp.where(m0, prod, 0.0), axis=1, keepdims=True)),
              _row_of(jnp.sum(jnp.where(m0, 0.0, prod), axis=1, keepdims=True))]
        doh = [jnp.where(m0, dov, 0.0).astype(BF16), jnp.where(m0, 0.0, dov).astype(BF16)]
        lse = [st_ref[0], st_ref[1]]
        dqt_ref[...] = jnp.zeros_like(dqt_ref)
        if fox:
            rs_ref[...] = jnp.zeros_like(rs_ref)
        chains = [(h, b) for h in range(2) for b in range(nb)]

        def tiles(js, masked_at):
            starts = [pl.multiple_of(j * T, T) for j in js]
            zss, dpss = [], []
            for start in starts:
                vb = v_ref[pl.ds(start, T), :].astype(BF16)
                kh = _head_k(kind, k_ref, start, T)
                zss.append(_split_blocks([_dot_nt(kh[h], qh[h]) for h in range(2)]))
                dpss.append(_split_blocks([_dot_nt(vb, doh[h]) for h in range(2)]))
            pss, dsss = [], []
            for start, zs, dps, masked in zip(starts, zss, dpss, _mask_flags(js, masked_at)):
                ps, dss = [], []
                for (h, b), z, dp in zip(chains, zs, dps):
                    lanes = slice(b * LANES, (b + 1) * LANES)
                    if mla:
                        z = z * scale
                    if fox:
                        z = z - cc_ref[h, pl.ds(start, T), :]
                    if masked:
                        z = jnp.where(key <= qrow + b * LANES, z, NEG)
                    p = jnp.exp(z - lse[h][:, lanes])
                    ds = p * (dp - dd[h][:, lanes])
                    dsb = ds.astype(BF16)
                    if fox:
                        rs_ref[h, :, lanes] += jnp.sum(dsb.astype(F32), axis=0, keepdims=True)
                    ps.append(p.astype(BF16))
                    dss.append(dsb)
                pss.append(_join_blocks(ps, nb))
                dsss.append(_join_blocks(dss, nb))
            for j, start, ps, dss in zip(js, starts, pss, dsss):
                kt = kt_ref[j]
                dvc = None
                for h in range(2):
                    dkh = _dot(dss[h], qk[h])
                    dvh = _dot(ps[h], doh[h])
                    dvc = dvh if dvc is None else dvc + dvh
                    kth = kt[h * LANES:(h + 1) * LANES] if mla else kt
                    dqt_ref[h] += _dot(kth, dss[h])
                    if fox:
                        dkx_ref[h, pl.ds(start, T), :] += dkh
                    elif mla:
                        dk_ref[pl.ds(start, T), h * LANES:(h + 1) * LANES] += dkh * scale
                    else:
                        dk_ref[pl.ds(start, T), :] += dkh
                dv_ref[pl.ds(start, T), :] += dvc

        _loop_tiles(i, tiles, False)
        if mla:
            dq_ref[:, 0:LANES] = dqt_ref[0].T * scale
            dq_ref[:, LANES:2 * LANES] = dqt_ref[1].T * scale
        else:
            dq_ref[...] = jnp.where(top, dqt_ref[0], dqt_ref[1]).T * scale
        if fox:
            dcq_ref[0] = rs_ref[0]
            dcq_ref[1] = rs_ref[1]

            @pl.when(i == nq - 1)
            def _():
                dk_ref[...] = jnp.where(m0, dkx_ref[0], dkx_ref[1])
                dck_ref[0] = dkx_ref[0].T[64:65]
                dck_ref[1] = dkx_ref[1].T[0:1]

    qs, ks, vs, _ = _att_specs(kind, S, T)
    stat = pl.BlockSpec((2, None, 1, T), lambda p, i: (p, i, 0, 0))
    in_specs = [qs, ks, pl.BlockSpec((None, nq, kw, T), lambda p, i: (p, 0, 0, 0)), vs,
                pl.BlockSpec((T, LANES), lambda p, i: (i, do_off + p)),
                pl.BlockSpec((T, LANES), lambda p, i: (i, p)), stat]
    args = [q, k, kt, v, do, o, lse]
    W = npairs * LANES
    out_specs = [pl.BlockSpec((T, kw), lambda p, i: (i, p)), pl.BlockSpec((S, kw), lambda p, i: (0, p)),
                 pl.BlockSpec((S, LANES), lambda p, i: (0, p))]
    out_shape = [jax.ShapeDtypeStruct((S, npairs * kw), F32), jax.ShapeDtypeStruct((S, npairs * kw), F32),
                 jax.ShapeDtypeStruct((S, W), F32)]
    scratch = [pltpu.VMEM((2, LANES, T), F32)]
    if fox:
        in_specs.append(pl.BlockSpec((2, S, LANES), lambda p, i: (p, 0, 0)))
        args.append(c_col)
        out_specs += [pl.BlockSpec((2, 1, S), lambda p, i: (p, 0, 0)), stat]
        out_shape += [jax.ShapeDtypeStruct((2 * npairs, 1, S), F32), jax.ShapeDtypeStruct((2 * npairs, nq, 1, T), F32)]
        scratch += [pltpu.VMEM((2, 1, T), F32), pltpu.VMEM((2, S, LANES), F32)]
    return pl.pallas_call(
        body, name=name, grid=(npairs, nq), in_specs=in_specs, out_specs=out_specs, out_shape=out_shape,
        scratch_shapes=scratch, compiler_params=_cparams(("parallel", "arbitrary")),
    )(*args)


def _sb_fwd_t(proj, vt, S, npairs, name):
    T = ATT_T
    nq = S // T
    nb = T // LANES
    scale = 64 ** -0.5

    def body(q_ref, k_ref, vt_ref, g_ref, o_ref, og_ref, ogt_ref, st_ref, rem_ref, acc_ref):
        i = pl.program_id(1)
        m0 = lax.broadcasted_iota(jnp.int32, (1, LANES), 1) < 64
        top = lax.broadcasted_iota(jnp.int32, (LANES, 1), 0) < 64
        key = lax.broadcasted_iota(jnp.int32, (T, LANES), 0)
        qrow = lax.broadcasted_iota(jnp.int32, (T, LANES), 1)
        r = lax.broadcasted_iota(jnp.int32, (T, T), 0)
        c = lax.broadcasted_iota(jnp.int32, (T, T), 1)
        after = (c > r).astype(BF16)
        qh = _head_q("sb", q_ref, m0, scale)
        rem_ref[...] = jnp.zeros_like(rem_ref)
        acc_ref[...] = jnp.zeros_like(acc_ref)
        chains = [(h, b) for h in range(2) for b in range(nb)]

        def tiles(js, masked_at):
            zss = []
            for j in js:
                kb = k_ref[pl.ds(pl.multiple_of(j * T, T), T), :].astype(BF16)
                zss.append(_split_blocks([_dot_nt(kb, qh[h]) for h in range(2)]))
            lass, sums, hiss, loss = [], [], [], []
            for zs, masked in zip(zss, _mask_flags(js, masked_at)):
                las, sm, his, los = [], [], [], []
                for (h, b), z in zip(chains, zs):
                    lk, la = _softplus_parts(z)
                    if masked:
                        lk = jnp.where(key < qrow + b * LANES, lk, 0.0)
                    hi, lo = _split2(lk)
                    las.append(la)
                    sm.append(jnp.sum(lk, axis=0, keepdims=True))
                    his.append(hi)
                    los.append(lo)
                lass.append(las)
                sums.append(sm)
                hiss.append(_join_blocks(his, nb))
                loss.append(_join_blocks(los, nb))
            rcss = [_split_blocks([_dot(after, hi) + _dot(after, lo) for hi, lo in zip(his, los)])
                    for his, los in zip(hiss, loss)]
            wss = []
            for las, sm, rcs, masked in zip(lass, sums, rcss, _mask_flags(js, masked_at)):
                ws = []
                for (h, b), la, s, rc in zip(chains, las, sm, rcs):
                    lanes = slice(b * LANES, (b + 1) * LANES)
                    w = jnp.exp(la + (rem_ref[h, :, lanes] + rc))
                    if masked:
                        w = jnp.where(key < qrow + b * LANES, w, 0.0)
                    ws.append(w.astype(BF16))
                    rem_ref[h, :, lanes] += s
                wss.append(_join_blocks(ws, nb))
            for j, ws in zip(js, wss):
                vtb = vt_ref[j]
                for h in range(2):
                    acc_ref[h] += _dot(vtb, ws[h])

        _loop_tiles(i, tiles, True)
        o = jnp.where(top, acc_ref[0], acc_ref[1]).T
        o_ref[...] = o
        gt = g_ref[...]
        og = o * (gt * _sigmoid(gt))
        og_ref[...] = og.astype(BF16)
        ogt_ref[...] = og.T.astype(BF16)
        st_ref[0] = rem_ref[0]
        st_ref[1] = rem_ref[1]

    qs, ks, _, gs = _att_specs("sb", S, T)
    W = npairs * LANES
    return pl.pallas_call(
        body, name=name, grid=(npairs, nq),
        in_specs=[qs, ks, pl.BlockSpec((None, nq, LANES, T), lambda p, i: (p, 0, 0, 0)), gs],
        out_specs=[pl.BlockSpec((T, LANES), lambda p, i: (i, p)), pl.BlockSpec((T, LANES), lambda p, i: (i, p)),
                   pl.BlockSpec((LANES, T), lambda p, i: (p, i)),
                   pl.BlockSpec((2, None, 1, T), lambda p, i: (p, i, 0, 0))],
        out_shape=[jax.ShapeDtypeStruct((S, W), F32), jax.ShapeDtypeStruct((S, W), BF16),
                   jax.ShapeDtypeStruct((W, S), BF16),
                   jax.ShapeDtypeStruct((2 * npairs, nq, 1, T), F32)],
        scratch_shapes=[pltpu.VMEM((2, 1, T), F32), pltpu.VMEM((2, LANES, T), F32)],
        compiler_params=_cparams(("parallel", "parallel")),
    )(proj, proj, vt, proj)


def _sb_bwd_t(proj, kt, do, tot, S, npairs, name):
    T = ATT_T
    nq = S // T
    nb = T // LANES
    scale = 64 ** -0.5

    def body(q_ref, k_ref, kt_ref, v_ref, do_ref, st_ref, dq_ref, dk_ref, dv_ref, dqt_ref, pre_ref, gpre_ref):
        i = pl.program_id(1)

        @pl.when(i == 0)
        def _():
            dk_ref[...] = jnp.zeros_like(dk_ref)
            dv_ref[...] = jnp.zeros_like(dv_ref)

        m0 = lax.broadcasted_iota(jnp.int32, (1, LANES), 1) < 64
        top = lax.broadcasted_iota(jnp.int32, (LANES, 1), 0) < 64
        key = lax.broadcasted_iota(jnp.int32, (T, LANES), 0)
        qrow = lax.broadcasted_iota(jnp.int32, (T, LANES), 1)
        r = lax.broadcasted_iota(jnp.int32, (T, T), 0)
        c = lax.broadcasted_iota(jnp.int32, (T, T), 1)
        upto = (c <= r).astype(BF16)
        left = (c < r).astype(BF16)
        qh = _head_q("sb", q_ref, m0, scale)
        dov = do_ref[...]
        doh = [jnp.where(m0, dov, 0.0).astype(BF16), jnp.where(m0, 0.0, dov).astype(BF16)]
        tot_h = [st_ref[0], st_ref[1]]
        dqt_ref[...] = jnp.zeros_like(dqt_ref)
        pre_ref[...] = jnp.zeros_like(pre_ref)
        gpre_ref[...] = jnp.zeros_like(gpre_ref)
        chains = [(h, b) for h in range(2) for b in range(nb)]

        def tiles(js, masked_at):
            starts = [pl.multiple_of(j * T, T) for j in js]
            zss, dwss = [], []
            for start in starts:
                vb = v_ref[pl.ds(start, T), :].astype(BF16)
                kb = k_ref[pl.ds(start, T), :].astype(BF16)
                zss.append(_split_blocks([_dot_nt(kb, qh[h]) for h in range(2)]))
                dwss.append(_split_blocks([_dot_nt(vb, doh[h]) for h in range(2)]))
            lass, sums, hiss, loss = [], [], [], []
            for zs, masked in zip(zss, _mask_flags(js, masked_at)):
                las, sm, his, los = [], [], [], []
                for (h, b), z in zip(chains, zs):
                    lk, la = _softplus_parts(z)
                    if masked:
                        lk = jnp.where(key < qrow + b * LANES, lk, 0.0)
                    hi, lo = _split2(lk)
                    las.append(la)
                    sm.append(jnp.sum(lk, axis=0, keepdims=True))
                    his.append(hi)
                    los.append(lo)
                lass.append(las)
                sums.append(sm)
                hiss.append(_join_blocks(his, nb))
                loss.append(_join_blocks(los, nb))
            pcss = [_split_blocks([_dot(upto, hi) + _dot(upto, lo) for hi, lo in zip(his, los)])
                    for his, los in zip(hiss, loss)]
            wss, gss = [], []
            for las, sm, pcs, dws, masked in zip(lass, sums, pcss, dwss, _mask_flags(js, masked_at)):
                ws, gs = [], []
                for (h, b), la, s, pc, dw in zip(chains, las, sm, pcs, dws):
                    lanes = slice(b * LANES, (b + 1) * LANES)
                    w = jnp.exp(la + ((tot_h[h][:, lanes] - pre_ref[h, :, lanes]) - pc))
                    if masked:
                        w = jnp.where(key < qrow + b * LANES, w, 0.0)
                    ws.append(w.astype(BF16))
                    gs.append(dw * w)
                    pre_ref[h, :, lanes] += s
                wss.append(_join_blocks(ws, nb))
                gss.append(gs)
            gcss = [_split_blocks([_dot(left, g) for g in _join_blocks([g.astype(BF16) for g in gs], nb)]) for gs in gss]
            dzss = []
            for las, gs, gcs, masked in zip(lass, gss, gcss, _mask_flags(js, masked_at)):
                dzs = []
                for (h, b), la, g, gc in zip(chains, las, gs, gcs):
                    lanes = slice(b * LANES, (b + 1) * LANES)
                    dz = g - (g + (gpre_ref[h, :, lanes] + gc)) * jnp.exp(la)
                    if masked:
                        dz = jnp.where(key < qrow + b * LANES, dz, 0.0)
                    dzs.append(dz.astype(BF16))
                    gpre_ref[h, :, lanes] += jnp.sum(g, axis=0, keepdims=True)
                dzss.append(_join_blocks(dzs, nb))
            for j, start, ws, dzs in zip(js, starts, wss, dzss):
                kt = kt_ref[j]
                dkc = dvc = None
                for h in range(2):
                    dkh = _dot(dzs[h], qh[h])
                    dvh = _dot(ws[h], doh[h])
                    dkc = dkh if dkc is None else dkc + dkh
                    dvc = dvh if dvc is None else dvc + dvh
                    dqt_ref[h] += _dot(kt, dzs[h])
                dk_ref[pl.ds(start, T), :] += dkc
                dv_ref[pl.ds(start, T), :] += dvc

        _loop_tiles(i, tiles, False)
        dq_ref[...] = jnp.where(top, dqt_ref[0], dqt_ref[1]).T * scale

    qs, ks, vs, _ = _att_specs("sb", S, T)
    W = npairs * LANES
    return pl.pallas_call(
        body, name=name, grid=(npairs, nq),
        in_specs=[qs, ks, pl.BlockSpec((None, nq, LANES, T), lambda p, i: (p, 0, 0, 0)), vs,
                  pl.BlockSpec((T, LANES), lambda p, i: (i, p)),
                  pl.BlockSpec((2, None, 1, T), lambda p, i: (p, i, 0, 0))],
        out_specs=[pl.BlockSpec((T, LANES), lambda p, i: (i, p)), pl.BlockSpec((S, LANES), lambda p, i: (0, p)),
                   pl.BlockSpec((S, LANES), lambda p, i: (0, p))],
        out_shape=[jax.ShapeDtypeStruct((S, W), F32)] * 3,
        scratch_shapes=[pltpu.VMEM((2, LANES, T), F32), pltpu.VMEM((2, 1, T), F32), pltpu.VMEM((2, 1, T), F32)],
        compiler_params=_cparams(("parallel", "arbitrary")),
    )(proj, proj, kt, proj, do, tot)


def _pad_w0(w):
    z = lambda n: jnp.zeros((w.shape[0], n), w.dtype)
    return jnp.concatenate([w[:, 2048:2432], w[:, 2432:2688], z(64), w[:, 2688:2720], z(32),
                            w[:, 1536:2048], w[:, 2720:3232], w[:, 0:512], w[:, 512:1024], w[:, 1024:1536]], axis=1)


def _unpad_w0(wp):
    return jnp.concatenate([wp[:, L0_SBQ:L0_SBQ + 512], wp[:, L0_SBK:L0_SBK + 512], wp[:, L0_SBV:L0_SBV + 512],
                            wp[:, L0_SBG:L0_SBG + 512], wp[:, 0:384], wp[:, 384:640], wp[:, 704:736],
                            wp[:, L0_MLG:L0_MLG + 512]], axis=1)


def _pad_wq(w):
    return jnp.pad(w.reshape(384, 8, 96), ((0, 0), (0, 0), (0, 32))).reshape(384, 1024)


def _unpad_wq(wp):
    return wp.reshape(384, 8, 128)[:, :, :96].reshape(384, 768)


def _pad_wkv(w):
    w3 = w.reshape(256, 8, 128)
    k = jnp.pad(w3[:, :, :64], ((0, 0), (0, 0), (0, 64))).reshape(256, 1024)
    return jnp.concatenate([k, w3[:, :, 64:].reshape(256, 512)], axis=1)


def _unpad_wkv(wp):
    k = wp[:, :1024].reshape(256, 8, 128)[:, :, :64]
    v = wp[:, 1024:].reshape(256, 8, 64)
    return jnp.concatenate([k, v], axis=-1).reshape(256, 1024)


def _pad_w1(w):
    return jnp.concatenate([w, jnp.zeros((w.shape[0], L1_WIDTH - ODD_IN_WIDTH), w.dtype)], axis=1)


def _local_step(x, positions, target, g, w0p, wqp, wkvp, wo0, w1p, wo1):
    S = x.shape[0]
    nq = S // ATT_T
    invf = ROPE_THETA ** (-jnp.arange(0, MLA_ROPE_DIM, 2, dtype=F32) / MLA_ROPE_DIM)
    invf = jnp.concatenate([jnp.zeros((64,), F32), invf, invf, jnp.zeros((32,), F32)]).reshape(1, LANES)
    cosT, s1T, s2T = _rope_tables(positions.reshape(S, 1), invf, "rope_tables")
    bfp = jnp.pad(g["l1_b_f"], ((0, 0), (0, LANES - FOX_HEADS)))

    proj0, h0t = _norm_matmul(x, g["l0_pre_g"], w0p, "l0_in_proj")
    qm, km, vm, qnt, cnt = _mla_prep(proj0, g["l0_q_a_g"], g["l0_kv_a_g"], wqp, wkvp, cosT, s1T, s2T, "mla_prep")
    sb_vt = _transpose_tiles(proj0, L0_SBV, 4, LANES, 2, "sb_vt")
    sb_kt = _transpose_tiles(proj0, L0_SBK, 4, LANES, 2, "sb_kt")
    o_sb, og_sb, ogt_sb, tot_sb = _sb_fwd_t(proj0, sb_vt, S, 4, "sb_fwd")
    vmt = _transpose_tiles(vm, 0, 4, LANES, 4, "mla_vt")
    kmt = _transpose_tiles(km, 0, 4, 2 * LANES, 4, "mla_kt")
    o_ml, og_ml, ogt_ml, lse_ml = _softmax_fwd("mla", (qm, km, vmt, proj0), None, S, 4, "mla_fwd")
    y0, x1 = _out_proj(og_sb, og_ml, 0, 0, wo0, x, g["l0_post_g"], None, "l0_out_proj")

    proj1, h1t = _norm_matmul(x1, g["l1_pre_g"], w1p, "l1_in_proj")
    cfx = _fox_prep(proj1, bfp, "fox_prep")
    c16 = cfx[:, :FOX_HEADS].T
    c_col = jnp.broadcast_to(c16[:, :, None], (FOX_HEADS, S, LANES))
    vt1 = _transpose_tiles(proj1, L1_V, 8, LANES, 8, "fox_vt")
    kt1 = _transpose_tiles(proj1, L1_K, 8, LANES, 8, "fox_kt")
    o_fx, og_fx, ogt_fx, lse_fx = _softmax_fwd("fox", (proj1, proj1, vt1, proj1), c_col, S, 8, "fox_fwd")
    y1, dx2, lsum = _out_proj(og_fx, og_fx, 0, 1, wo1, x1, g["l1_post_g"], target, "l1_out_proj")

    dy1, do1, dgate1, d_post1 = _out_proj_bwd(dx2, y1, g["l1_post_g"], wo1, proj1, (L1_G, L1_G + 512), o_fx, o_fx, 0, 1, "l1_out_bwd")
    dwo1 = _matmul_t(ogt_fx, dy1, "l1_dw_out")
    dq1, dk1, dv1, dck, dcq = _softmax_bwd_t("fox", proj1, proj1, kt1, proj1, do1, 0, o_fx, lse_fx, c_col, S, 8,
                                             "fox_bwd")
    dc = jnp.pad((dcq.reshape(FOX_HEADS, S) - dck.reshape(FOX_HEADS, S)).T, ((0, 0), (0, LANES - FOX_HEADS)))
    df, d_bf = _fox_prep_bwd(dc, proj1, bfp, "fox_prep_bwd")
    pieces1 = [(L1_Q, dq1), (L1_K, dk1), (L1_V, dv1), (L1_G, dgate1), (L1_F, df)]
    dx1, d_pre1 = _in_proj_bwd(pieces1, w1p, x1, g["l1_pre_g"], dx2, "l1_in_bwd")
    dw1p = jnp.concatenate(_matmul_t_many(h1t, [dq1, dk1], "l1_dw_in_a")
                           + _matmul_t_many(h1t, [dv1, dgate1, df], "l1_dw_in_b"), axis=1)

    dy0, do0, dgate0, d_post0 = _out_proj_bwd(dx1, y0, g["l0_post_g"], wo0, proj0, (L0_SBG, L0_MLG), o_sb, o_ml, 0, 0,
                                              "l0_out_bwd")
    dwo0 = jnp.concatenate([_matmul_t(ogt_sb, dy0, "l0_dw_out_sb"), _matmul_t(ogt_ml, dy0, "l0_dw_out_mla")], axis=0)
    dsq, dsk, dsv = _sb_bwd_t(proj0, sb_kt, do0, tot_sb, S, 4, "sb_bwd")
    dqm, dkm, dvm = _softmax_bwd_t("mla", qm, km, kmt, vm, do0, 4, o_ml, lse_ml, None, S, 4, "mla_bwd")
    dprep, dqb, dkvb, d_qag, d_kvag = _mla_prep_bwd(dqm, dkm, dvm, proj0, g["l0_q_a_g"], g["l0_kv_a_g"], wqp, wkvp,
                                                    cosT, s1T, s2T, "mla_prep_bwd")
    dwqp = _matmul_t(qnt, dqb, "l0_dw_qb")
    dwkvp = _matmul_t(cnt, dkvb, "l0_dw_kvb")
    pieces0 = [(L0_PREP, dprep), (L0_SBG, dgate0), (L0_SBQ, dsq), (L0_SBK, dsk), (L0_SBV, dsv)]
    dx0, d_pre0 = _in_proj_bwd(pieces0, w0p, x, g["l0_pre_g"], dx1, "l0_in_bwd")
    dw0p = jnp.concatenate(_matmul_t_many(h0t, [dprep, dgate0], "l0_dw_in_a")
                           + _matmul_t_many(h0t, [dsq, dsk, dsv], "l0_dw_in_b"), axis=1)

    grads = {
        "l0_pre_g": d_pre0, "l0_post_g": d_post0, "l0_w_in": dw0p, "l0_q_a_g": d_qag, "l0_w_q_b": dwqp,
        "l0_kv_a_g": d_kvag, "l0_w_kv_b": dwkvp, "l0_w_out": dwo0, "l1_pre_g": d_pre1, "l1_post_g": d_post1,
        "l1_w_in": dw1p, "l1_b_f": d_bf[:, :FOX_HEADS], "l1_w_out": dwo1,
    }
    return lsum, dx0, grads


_ANY = pl.BlockSpec(memory_space=pl.ANY)


def _place():
    return lax.axis_index("x"), lax.axis_index("y"), lax.axis_index("c")


def _other_chips(x, y):
    return [(1 - x, y), (x, 1 - y), (1 - x, 1 - y)]


def _half(rows, c):
    return pl.ds(c * (rows // 2), rows // 2)


def _weight_gather(parts):
    n = len(parts)

    def body(*refs):
        p_refs, out_refs, send_sems, recv_sems = refs[:n], refs[n:2 * n], refs[2 * n], refs[2 * n + 1]
        x, y, c = _place()
        sibling = (x, y, 1 - c)
        chips = _other_chips(x, y)

        def blk(k, chip, cc):
            return out_refs[k].at[2 * chip[0] + chip[1], _half(p_refs[k].shape[0], cc)]

        def copy(s, src, dst, to):
            return pltpu.make_async_remote_copy(src_ref=src, dst_ref=dst, send_sem=send_sems.at[s],
                                                recv_sem=recv_sems.at[s], device_id=to, device_id_type=MESH)

        first = [copy(6 * k + j, p_refs[k].at[_half(p_refs[k].shape[0], c)], blk(k, (x, y), c), (*chip, c))
                 for j, chip in enumerate(chips) for k in range(n)]
        for cp in first:
            cp.start()
        passed = []
        for j, chip in enumerate(chips):
            for k in range(n):
                copy(6 * k + j, blk(k, chip, c), blk(k, chip, c), (x, y, c)).wait_recv()
                passed.append(copy(6 * k + 3 + j, blk(k, chip, c), blk(k, chip, c), sibling))
                passed[-1].start()
        for j, chip in enumerate(chips):
            for k in range(n):
                copy(6 * k + 3 + j, blk(k, chip, 1 - c), blk(k, chip, 1 - c), (x, y, c)).wait_recv()
        for cp in first + passed:
            cp.wait_send()

    return pl.pallas_call(
        body, name="weight_gather", in_specs=[_ANY] * n, out_specs=[_ANY] * n,
        out_shape=[jax.ShapeDtypeStruct((4,) + a.shape, a.dtype) for a in parts],
        scratch_shapes=[pltpu.SemaphoreType.DMA((6 * n,)), pltpu.SemaphoreType.DMA((6 * n,))],
    )(*parts)


def _grad_core_exchange(ps):
    n = len(ps)

    def body(*refs):
        p_refs, recv_refs, send_sems, recv_sems = refs[:n], refs[n:2 * n], refs[2 * n], refs[2 * n + 1]
        x, y, c = _place()
        give = [pltpu.make_async_remote_copy(src_ref=p_refs[k].at[j, _half(p_refs[k].shape[1], 1 - c)],
                                             dst_ref=recv_refs[k].at[j], send_sem=send_sems.at[4 * k + j],
                                             recv_sem=recv_sems.at[4 * k + j], device_id=(x, y, 1 - c),
                                             device_id_type=MESH) for k in range(n) for j in range(4)]
        for cp in give:
            cp.start()
        for cp in give:
            cp.wait()

    return pl.pallas_call(
        body, name="grad_core_exchange", in_specs=[_ANY] * n, out_specs=[_ANY] * n,
        out_shape=[jax.ShapeDtypeStruct((4, p.shape[1] // 2, p.shape[2]), p.dtype) for p in ps],
        scratch_shapes=[pltpu.SemaphoreType.DMA((4 * n,)), pltpu.SemaphoreType.DMA((4 * n,))],
    )(*ps)


def _grad_rows(rows):
    return _pick(rows, (1296, 512, rows))


def _grad_add_cores(p, theirs, c1, name):
    _, rh, cols = theirs.shape
    tr = _grad_rows(rh)

    def body(c_ref, a_ref, b_ref, o_ref):
        o_ref[...] = (a_ref[...] + b_ref[...]).astype(BF16)

    spec = pl.BlockSpec((None, tr, cols), lambda j, r, c: (j, r, 0))
    grid_spec = pltpu.PrefetchScalarGridSpec(
        num_scalar_prefetch=1, grid=(4, rh // tr),
        in_specs=[pl.BlockSpec((None, None, tr, cols), lambda j, r, c: (j, c[0], r, 0)), spec], out_specs=spec)
    return pl.pallas_call(
        body, name=name, grid_spec=grid_spec, out_shape=jax.ShapeDtypeStruct(theirs.shape, BF16),
        compiler_params=_cparams(("parallel", "parallel")),
    )(c1, p.reshape(4, 2, rh, cols), theirs)


def _grad_chip_exchange(qs):
    n = len(qs)

    def body(*refs):
        q_refs, out_refs, send_sems, recv_sems = refs[:n], refs[n:2 * n], refs[2 * n], refs[2 * n + 1]
        x, y, c = _place()
        me = 2 * x + y
        chips = _other_chips(x, y)
        sends = [pltpu.make_async_remote_copy(src_ref=q_refs[k].at[2 * chip[0] + chip[1]], dst_ref=out_refs[k].at[me],
                                              send_sem=send_sems.at[3 * k + j], recv_sem=recv_sems.at[3 * k + j],
                                              device_id=(*chip, c), device_id_type=MESH)
                 for j, chip in enumerate(chips) for k in range(n)]
        for cp in sends:
            cp.start()
        for j, chip in enumerate(chips):
            for k in range(n):
                slot = out_refs[k].at[2 * chip[0] + chip[1]]
                pltpu.make_async_remote_copy(src_ref=slot, dst_ref=slot, send_sem=send_sems.at[3 * k + j],
                                             recv_sem=recv_sems.at[3 * k + j], device_id=(x, y, c),
                                             device_id_type=MESH).wait_recv()
        for cp in sends:
            cp.wait_send()

    return pl.pallas_call(
        body, name="grad_chip_exchange", in_specs=[_ANY] * n, out_specs=[_ANY] * n,
        out_shape=[jax.ShapeDtypeStruct(q.shape, q.dtype) for q in qs],
        scratch_shapes=[pltpu.SemaphoreType.DMA((3 * n,)), pltpu.SemaphoreType.DMA((3 * n,))],
    )(*qs)


def _grad_add_chips(q, slots, me1, name):
    _, rh, cols = q.shape
    tr = _grad_rows(rh)

    def body(me_ref, own_ref, s0, s1, s2, s3, o_ref):
        me = me_ref[0]
        t = [jnp.where(me == j, own_ref[...], s[...]).astype(F32) for j, s in enumerate((s0, s1, s2, s3))]
        o_ref[...] = ((t[0] + t[1]) + t[2]) + t[3]

    def slot_spec(j):
        return pl.BlockSpec((None, tr, cols), lambda r, me: (jnp.where(me[0] == j, (j + 1) % 4, j), r, 0))

    grid_spec = pltpu.PrefetchScalarGridSpec(
        num_scalar_prefetch=1, grid=(rh // tr,),
        in_specs=[pl.BlockSpec((None, tr, cols), lambda r, me: (me[0], r, 0))] + [slot_spec(j) for j in range(4)],
        out_specs=pl.BlockSpec((tr, cols), lambda r, me: (r, 0)))
    return pl.pallas_call(
        body, name=name, grid_spec=grid_spec, out_shape=jax.ShapeDtypeStruct(q.shape[1:], F32),
        compiler_params=_cparams(("parallel",)),
    )(me1, q, slots, slots, slots, slots)


def _grad_core_gather(ts):
    n = len(ts)

    def body(*refs):
        t_refs, out_refs, send_sems, recv_sems = refs[:n], refs[n:2 * n], refs[2 * n], refs[2 * n + 1]
        x, y, c = _place()
        give = [pltpu.make_async_remote_copy(src_ref=t_refs[k], dst_ref=out_refs[k], send_sem=send_sems.at[k],
                                             recv_sem=recv_sems.at[k], device_id=(x, y, 1 - c), device_id_type=MESH)
                for k in range(n)]
        for cp in give:
            cp.start()
        for cp in give:
            cp.wait()

    return pl.pallas_call(
        body, name="grad_core_gather", in_specs=[_ANY] * n, out_specs=[_ANY] * n,
        out_shape=[jax.ShapeDtypeStruct(t.shape, t.dtype) for t in ts],
        scratch_shapes=[pltpu.SemaphoreType.DMA((n,)), pltpu.SemaphoreType.DMA((n,))],
    )(*ts)


def _small_allreduce(sp):
    def body(sp_ref, out_ref, gath_ref, send_sems, recv_sems):
        x, y, c = _place()
        me = 4 * x + 2 * y + c
        gath_ref[me] = sp_ref[...]
        peers = []
        for k in range(1, 8):
            px = 1 - x if k & 4 else x
            py = 1 - y if k & 2 else y
            pc = 1 - c if k & 1 else c
            peers.append((px, py, pc))
        sends = [pltpu.make_async_remote_copy(src_ref=sp_ref, dst_ref=gath_ref.at[me], send_sem=send_sems.at[k],
                                              recv_sem=recv_sems.at[k], device_id=peer, device_id_type=MESH)
                 for k, peer in enumerate(peers)]
        for cp in sends:
            cp.start()
        for k, (px, py, pc) in enumerate(peers):
            slot = gath_ref.at[4 * px + 2 * py + pc]
            pltpu.make_async_remote_copy(src_ref=slot, dst_ref=slot, send_sem=send_sems.at[k], recv_sem=recv_sems.at[k],
                                         device_id=(x, y, c), device_id_type=MESH).wait_recv()
        for cp in sends:
            cp.wait_send()
        tot = gath_ref[0]
        for d in range(1, 8):
            tot = tot + gath_ref[d]
        out_ref[...] = tot

    vm = pl.BlockSpec(memory_space=pltpu.VMEM)
    return pl.pallas_call(
        body, name="small_allreduce", in_specs=[vm], out_specs=vm, out_shape=jax.ShapeDtypeStruct(sp.shape, sp.dtype),
        scratch_shapes=[pltpu.VMEM((8,) + sp.shape, sp.dtype), pltpu.SemaphoreType.DMA((7,)), pltpu.SemaphoreType.DMA((7,))],
    )(sp)


def _adamw_update(w, gv, m, v):
    mn = ADAM_B1 * m + (1.0 - ADAM_B1) * gv
    vn = ADAM_B2 * v + (1.0 - ADAM_B2) * (gv * gv)
    m_hat = mn / (1.0 - ADAM_B1 ** ADAM_STEP)
    v_hat = vn / (1.0 - ADAM_B2 ** ADAM_STEP)
    return -ADAM_LR * (m_hat / (jnp.sqrt(v_hat) + ADAM_EPS) + ADAM_WD * w), mn, vn


def _adamw(w, g, m, v, name):
    rows, cols = w.shape
    tr = _pick(rows, (256, rows))

    def body(w_ref, g_ref, m_ref, v_ref, d_ref, mo_ref, vo_ref):
        d_ref[...], mo_ref[...], vo_ref[...] = _adamw_update(w_ref[...], g_ref[...], m_ref[...], v_ref[...])

    spec = pl.BlockSpec((tr, cols), lambda r: (r, 0))
    shp = jax.ShapeDtypeStruct(w.shape, F32)
    return pl.pallas_call(
        body, name=name, grid=(rows // tr,), in_specs=[spec] * 4, out_specs=[spec] * 3, out_shape=[shp] * 3,
        compiler_params=_cparams(("parallel",)),
    )(w, g, m, v)


MAT_NAMES = ("l0_w_in", "l0_w_q_b", "l0_w_kv_b", "l0_w_out", "l1_w_in", "l1_w_out")
VEC_NAMES = ("l0_pre_g", "l0_post_g", "l0_q_a_g", "l0_kv_a_g", "l1_pre_g", "l1_post_g", "l1_b_f")
WEIGHT_NAMES = ("l0_pre_g", "l0_post_g", "l0_w_in", "l0_q_a_g", "l0_w_q_b", "l0_kv_a_g", "l0_w_kv_b", "l0_w_out",
                "l1_pre_g", "l1_post_g", "l1_w_in", "l1_b_f", "l1_w_out")
MAT_SHARD = {"l0_w_in": (1024, 808), "l0_w_q_b": (384, 192), "l0_w_kv_b": (256, 256), "l0_w_out": (256, 1024),
             "l1_w_in": (1024, 1028), "l1_w_out": (256, 1024)}
ROW_SHARDED = ("l0_w_out", "l1_w_out")
WHOLE_MATS = ("l0_w_in", "l1_w_in")
PACKED_MATS = ("l0_w_q_b", "l0_w_kv_b", "l0_w_out", "l1_w_out")
VEC_LEN = {"l0_pre_g": 1024, "l0_post_g": 1024, "l0_q_a_g": 384, "l0_kv_a_g": 256, "l1_pre_g": 1024,
           "l1_post_g": 1024, "l1_b_f": 16}


def _mat_rows(n):
    r, c = MAT_SHARD[n]
    return r * c // LANES


def _pack_shards(shards):
    return jnp.concatenate([shards[n].reshape(shards[n].shape[:-2] + (_mat_rows(n), LANES)) for n in PACKED_MATS],
                           axis=-2)


def _unpack_shards(pack):
    out, at = {}, 0
    for n in PACKED_MATS:
        out[n] = pack[..., at:at + _mat_rows(n), :].reshape(pack.shape[:-2] + MAT_SHARD[n])
        at += _mat_rows(n)
    return out


def _join_shards(n, s):
    if n in ROW_SHARDED:
        return s.reshape(4 * s.shape[1], s.shape[2])
    return s.transpose(1, 0, 2).reshape(s.shape[1], 4 * s.shape[2])


def _cut_shards(n, w):
    r, c = MAT_SHARD[n]
    if n in ROW_SHARDED:
        return w.reshape(4, r, c)
    return w.reshape(r, 4, c).transpose(1, 0, 2)


def _pack_vecs(vecs):
    parts = []
    for n in VEC_NAMES:
        v = vecs[n].reshape(-1)
        parts.append(jnp.pad(v, (0, VEC_ROWS * LANES - v.shape[0])).reshape(VEC_ROWS, LANES))
    return jnp.concatenate(parts, axis=0)


def _unpack_vecs(pack):
    return {n: pack[k * VEC_ROWS:(k + 1) * VEC_ROWS].reshape(-1)[:VEC_LEN[n]] for k, n in enumerate(VEC_NAMES)}


def kernel(x, positions, l0_pre_g, l0_post_g, l0_w_in, l0_q_a_g, l0_w_q_b, l0_kv_a_g, l0_w_kv_b, l0_w_out, l1_pre_g, l1_post_g, l1_w_in, l1_b_f, l1_w_out, loss_target, m_l0_pre_g, m_l0_post_g, m_l0_w_in, m_l0_q_a_g, m_l0_w_q_b, m_l0_kv_a_g, m_l0_w_kv_b, m_l0_w_out, m_l1_pre_g, m_l1_post_g, m_l1_w_in, m_l1_b_f, m_l1_w_out, v_l0_pre_g, v_l0_post_g, v_l0_w_in, v_l0_q_a_g, v_l0_w_q_b, v_l0_kv_a_g, v_l0_w_kv_b, v_l0_w_out, v_l1_pre_g, v_l1_post_g, v_l1_w_in, v_l1_b_f, v_l1_w_out):
    w = dict(l0_pre_g=l0_pre_g, l0_post_g=l0_post_g, l0_w_in=l0_w_in, l0_q_a_g=l0_q_a_g, l0_w_q_b=l0_w_q_b,
             l0_kv_a_g=l0_kv_a_g, l0_w_kv_b=l0_w_kv_b, l0_w_out=l0_w_out, l1_pre_g=l1_pre_g, l1_post_g=l1_post_g,
             l1_w_in=l1_w_in, l1_b_f=l1_b_f, l1_w_out=l1_w_out)
    m = dict(l0_pre_g=m_l0_pre_g, l0_post_g=m_l0_post_g, l0_w_in=m_l0_w_in, l0_q_a_g=m_l0_q_a_g, l0_w_q_b=m_l0_w_q_b,
             l0_kv_a_g=m_l0_kv_a_g, l0_w_kv_b=m_l0_w_kv_b, l0_w_out=m_l0_w_out, l1_pre_g=m_l1_pre_g,
             l1_post_g=m_l1_post_g, l1_w_in=m_l1_w_in, l1_b_f=m_l1_b_f, l1_w_out=m_l1_w_out)
    v = dict(l0_pre_g=v_l0_pre_g, l0_post_g=v_l0_post_g, l0_w_in=v_l0_w_in, l0_q_a_g=v_l0_q_a_g, l0_w_q_b=v_l0_w_q_b,
             l0_kv_a_g=v_l0_kv_a_g, l0_w_kv_b=v_l0_w_kv_b, l0_w_out=v_l0_w_out, l1_pre_g=v_l1_pre_g,
             l1_post_g=v_l1_post_g, l1_w_in=v_l1_w_in, l1_b_f=v_l1_b_f, l1_w_out=v_l1_w_out)

    cx, cy, cc = _place()
    me1 = jnp.reshape(2 * cx + cy, (1,)).astype(jnp.int32)
    c1 = jnp.reshape(cc, (1,)).astype(jnp.int32)
    w_bf = {n: w[n].astype(BF16) for n in MAT_NAMES}
    mine = [_pack_shards(w_bf)] + [w_bf[n] for n in WHOLE_MATS]
    got = [lax.dynamic_update_slice(g, a[None], (2 * cx + cy, 0, 0)) for g, a in zip(_weight_gather(mine), mine)]
    gathered = dict(_unpack_shards(got[0]), **dict(zip(WHOLE_MATS, got[1:])))
    full = {n: _join_shards(n, gathered[n]) for n in MAT_NAMES}
    gains = {n: w[n].reshape(1, -1) for n in VEC_NAMES}

    lsum, dx0, grads = _local_step(
        x[0], positions[0], loss_target[0], gains, _pad_w0(full["l0_w_in"]), _pad_wq(full["l0_w_q_b"]),
        _pad_wkv(full["l0_w_kv_b"]), full["l0_w_out"], _pad_w1(full["l1_w_in"]), full["l1_w_out"])

    gfull = {"l0_w_in": _unpad_w0(grads["l0_w_in"]), "l0_w_q_b": _unpad_wq(grads["l0_w_q_b"]),
             "l0_w_kv_b": _unpad_wkv(grads["l0_w_kv_b"]), "l0_w_out": grads["l0_w_out"],
             "l1_w_in": grads["l1_w_in"][:, :ODD_IN_WIDTH], "l1_w_out": grads["l1_w_out"]}
    cut = {n: _cut_shards(n, gfull[n]) for n in MAT_NAMES}
    tags = ("packed",) + WHOLE_MATS
    g_parts = [_pack_shards(cut)] + [cut[n] for n in WHOLE_MATS]
    q_cores = [_grad_add_cores(p, t, c1, "grad_add_cores_" + tag)
               for p, t, tag in zip(g_parts, _grad_core_exchange(g_parts), tags)]
    g_mine = [_grad_add_chips(q, s, me1, "grad_add_chips_" + tag)
              for q, s, tag in zip(q_cores, _grad_chip_exchange(q_cores), tags)]
    g_theirs = _grad_core_gather(g_mine)

    small = _small_allreduce(jnp.concatenate([_pack_vecs({n: grads[n] for n in VEC_NAMES}),
                                              lsum.reshape(D_MODEL // LANES, LANES)], axis=0))
    g_small = small[:SMALL_ROWS]
    loss = 0.5 * jnp.sum(small[SMALL_ROWS:]) / float(D_MODEL)

    whole = [jnp.concatenate([lax.select(cc == 0, a, b), lax.select(cc == 0, b, a)], axis=0)
             for a, b in zip(g_mine, g_theirs)]
    g_mats = dict(_unpack_shards(whole[0]), **dict(zip(WHOLE_MATS, whole[1:])))
    d_mats, m_mats, v_mats = {}, {}, {}
    for n in MAT_NAMES:
        d_mats[n], m_mats[n], v_mats[n] = _adamw(w[n], g_mats[n], m[n], v[n], "adamw_" + n)
    d_small, m_small, v_small = _adamw(_pack_vecs(w), g_small, _pack_vecs(m), _pack_vecs(v), "adamw_vecs")

    def leaves(mats, vec_pack):
        out = dict(mats)
        out.update(_unpack_vecs(vec_pack))
        return [out[n] for n in WEIGHT_NAMES]

    return (loss, dx0[None], *leaves(g_mats, g_small), *leaves(d_mats, d_small), *leaves(m_mats, m_small),
            *leaves(v_mats, v_small))
```

```python
import jax
import jax.numpy as jnp
from jax import lax
from jax.experimental import pallas as pl
from jax.experimental.pallas import tpu as pltpu

F32 = jnp.float32
BF16 = jnp.bfloat16
MESH = pl.DeviceIdType.MESH

D_MODEL = 1024
RMS_EPS = 1e-6
ROPE_THETA = 10000.0
SB_WIDTH = 512
MLA_Q_LORA = 384
MLA_KV_LORA = 256
MLA_ROPE_DIM = 32
MLA_WIDTH = 512
FOX_WIDTH = 1024
FOX_HEADS = 16
EVEN_IN_WIDTH = 3232
ODD_IN_WIDTH = 4112

ADAM_LR = 0.001
ADAM_B1 = 0.9
ADAM_B2 = 0.999
ADAM_EPS = 1e-08
ADAM_WD = 0.01
ADAM_STEP = 10

LANES = 128
VMEM_LIMIT = 56 * 1024 * 1024

L0_PREP = 0
L0_PREP_W = 768
L0_SBG = 768
L0_MLG = 1280
L0_SBQ = 1792
L0_SBK = 2304
L0_SBV = 2816
L0_WIDTH = 3328
L1_Q = 0
L1_K = 1024
L1_V = 2048
L1_G = 3072
L1_F = 4096
L1_WIDTH = 4224

ATT_T = 256
ATT_GROUP = 4
ATT_QSUB = 2
NEG = -1e30

VEC_ROWS = 8
SMALL_ROWS = 7 * VEC_ROWS


def _cparams(sem, **kw):
    return pltpu.CompilerParams(dimension_semantics=sem, vmem_limit_bytes=VMEM_LIMIT, **kw)


def _dot(a, b):
    return lax.dot_general(a, b, (((1,), (0,)), ((), ())), preferred_element_type=F32)


def _dot_nt(a, b):
    return lax.dot_general(a, b, (((1,), (1,)), ((), ())), preferred_element_type=F32)


def _sigmoid(x):
    return 1.0 / (1.0 + jnp.exp(-x))


def _rstd(x):
    return lax.rsqrt(jnp.mean(x * x, axis=-1, keepdims=True) + RMS_EPS)


def _norm_bwd(x, g, dy):
    r = _rstd(x)
    xn = x * r
    dxn = dy * g
    dx = r * (dxn - xn * jnp.mean(dxn * xn, axis=-1, keepdims=True))
    return dx, dy * xn


def _split3(x):
    hi = x.astype(BF16)
    r1 = x - hi.astype(F32)
    mid = r1.astype(BF16)
    lo = (r1 - mid.astype(F32)).astype(BF16)
    return hi, mid, lo


def _wide_tile(n, cap=1792):
    return max(t for t in range(LANES, min(n, cap) + 1, LANES) if n % t == 0)


def _pick(n, cands):
    for c in cands:
        if n % c == 0:
            return c
    raise ValueError(n)


def _norm_matmul(x, g, w, name):
    S, K = x.shape
    N = w.shape[1]
    tm = _pick(S, (512, 256))
    tn = _wide_tile(N)

    def body(x_ref, g_ref, w_ref, o_ref, ht_ref, h_ref):
        @pl.when(pl.program_id(1) == 0)
        def _():
            xv = x_ref[...]
            h = (xv * _rstd(xv)) * g_ref[...]
            h_ref[...] = h.astype(BF16)
            ht_ref[...] = h.T.astype(BF16)
        o_ref[...] = _dot(h_ref[...], w_ref[...])

    return pl.pallas_call(
        body, name=name, grid=(S // tm, N // tn),
        in_specs=[pl.BlockSpec((tm, K), lambda i, j: (i, 0)),
                  pl.BlockSpec((1, K), lambda i, j: (0, 0)),
                  pl.BlockSpec((K, tn), lambda i, j: (0, j))],
        out_specs=[pl.BlockSpec((tm, tn), lambda i, j: (i, j)),
                   pl.BlockSpec((K, tm), lambda i, j: (0, i))],
        out_shape=[jax.ShapeDtypeStruct((S, N), F32), jax.ShapeDtypeStruct((K, S), BF16)],
        scratch_shapes=[pltpu.VMEM((tm, K), BF16)],
        compiler_params=_cparams(("parallel", "arbitrary")),
    )(x, g, w)


def _matmul_t(at, b, name):
    M, S = at.shape
    N = b.shape[1]
    tn = _wide_tile(N)
    ts = _pick(S, (512, 256))

    def body(a_ref, b_ref, o_ref):
        @pl.when(pl.program_id(1) == 0)
        def _():
            o_ref[...] = jnp.zeros_like(o_ref)
        o_ref[...] += _dot(a_ref[...], b_ref[...].astype(BF16))

    return pl.pallas_call(
        body, name=name, grid=(N // tn, S // ts),
        in_specs=[pl.BlockSpec((M, ts), lambda j, k: (0, k)),
                  pl.BlockSpec((ts, tn), lambda j, k: (k, j))],
        out_specs=pl.BlockSpec((M, tn), lambda j, k: (0, j)),
        out_shape=jax.ShapeDtypeStruct((M, N), F32),
        compiler_params=_cparams(("parallel", "arbitrary")),
    )(at, b)


def _matmul_t_many(at, bs, name):
    M, S = at.shape
    ts = _pick(S, (512, 256))
    n = len(bs)

    def body(*refs):
        a_ref, b_refs, o_refs = refs[0], refs[1:1 + n], refs[1 + n:]

        @pl.when(pl.program_id(0) == 0)
        def _():
            for o_ref in o_refs:
                o_ref[...] = jnp.zeros_like(o_ref)

        a = a_ref[...]
        for b_ref, o_ref in zip(b_refs, o_refs):
            o_ref[...] += _dot(a, b_ref[...].astype(BF16))

    return pl.pallas_call(
        body, name=name, grid=(S // ts,),
        in_specs=[pl.BlockSpec((M, ts), lambda k: (0, k))] + [pl.BlockSpec((ts, b.shape[1]), lambda k: (k, 0)) for b in bs],
        out_specs=[pl.BlockSpec((M, b.shape[1]), lambda k: (0, 0)) for b in bs],
        out_shape=[jax.ShapeDtypeStruct((M, b.shape[1]), F32) for b in bs],
        compiler_params=_cparams(("arbitrary",)),
    )(at, *bs)


def _in_proj_bwd(pieces, w, x, g, dx_up, name):
    S, K = x.shape
    N = w.shape[1]
    tm = _pick(S, (256,))
    offs = [off for off, _ in pieces]
    arrs = [a for _, a in pieces]

    def body(*refs):
        d_refs = refs[:len(arrs)]
        w_ref, x_ref, g_ref, u_ref, dx_ref, dg_ref = refs[len(arrs):]

        @pl.when(pl.program_id(0) == 0)
        def _():
            dg_ref[...] = jnp.zeros_like(dg_ref)

        acc = None
        for off, d_ref in zip(offs, d_refs):
            part = _dot_nt(d_ref[...].astype(BF16), w_ref[:, off:off + d_ref.shape[1]])
            acc = part if acc is None else acc + part
        dx, dgrow = _norm_bwd(x_ref[...], g_ref[...], acc)
        dx_ref[...] = u_ref[...] + dx
        dg_ref[...] += jnp.sum(dgrow, axis=0, keepdims=True)

    row = lambda i: (i, 0)
    fixed = lambda i: (0, 0)
    return pl.pallas_call(
        body, name=name, grid=(S // tm,),
        in_specs=[pl.BlockSpec((tm, a.shape[1]), row) for a in arrs] + [
            pl.BlockSpec((K, N), fixed), pl.BlockSpec((tm, K), row), pl.BlockSpec((1, K), fixed),
            pl.BlockSpec((tm, K), row)],
        out_specs=[pl.BlockSpec((tm, K), row), pl.BlockSpec((1, K), fixed)],
        out_shape=[jax.ShapeDtypeStruct((S, K), F32), jax.ShapeDtypeStruct((1, K), F32)],
        compiler_params=_cparams(("arbitrary",)),
    )(*arrs, w, x, g, dx_up)


def _out_proj(og_a, og_b, blk_a, blk_b, w, x, g, target, name):
    S = x.shape[0]
    D = x.shape[1]
    tm = _pick(S, (512, 256))
    with_loss = target is not None

    def body(*refs):
        if with_loss:
            a_ref, b_ref, wa_ref, wb_ref, x_ref, g_ref, t_ref, y_ref, o_ref, l_ref = refs
        else:
            a_ref, b_ref, wa_ref, wb_ref, x_ref, g_ref, y_ref, o_ref = refs
        y = _dot(a_ref[...], wa_ref[...]) + _dot(b_ref[...], wb_ref[...])
        y_ref[...] = y
        xn = x_ref[...] + (y * _rstd(y)) * g_ref[...]
        if with_loss:
            @pl.when(pl.program_id(0) == 0)
            def _():
                l_ref[...] = jnp.zeros_like(l_ref)
            d = xn - t_ref[...]
            o_ref[...] = d / float(D)
            l_ref[...] += jnp.sum(d * d, axis=0, keepdims=True)
        else:
            o_ref[...] = xn

    row = lambda i: (i, 0)
    in_specs = [pl.BlockSpec((tm, 512), lambda i: (i, blk_a)),
                pl.BlockSpec((tm, 512), lambda i: (i, blk_b)),
                pl.BlockSpec((512, D), lambda i: (0, 0)),
                pl.BlockSpec((512, D), lambda i: (1, 0)),
                pl.BlockSpec((tm, D), row),
                pl.BlockSpec((1, D), lambda i: (0, 0))]
    out_specs = [pl.BlockSpec((tm, D), row), pl.BlockSpec((tm, D), row)]
    out_shape = [jax.ShapeDtypeStruct((S, D), F32), jax.ShapeDtypeStruct((S, D), F32)]
    args = [og_a, og_b, w, w, x, g]
    if with_loss:
        in_specs.append(pl.BlockSpec((tm, D), row))
        out_specs.append(pl.BlockSpec((1, D), lambda i: (0, 0)))
        out_shape.append(jax.ShapeDtypeStruct((1, D), F32))
        args.append(target)
    return pl.pallas_call(
        body, name=name, grid=(S // tm,), in_specs=in_specs, out_specs=out_specs, out_shape=out_shape,
        compiler_params=_cparams(("arbitrary",)),
    )(*args)


def _out_proj_bwd(dx_up, y, g, w, proj, gate_offs, o_a, o_b, oblk_a, oblk_b, name):
    S, D = y.shape
    tm = _pick(S, (256,))
    gblk = [off // 256 + c for off in gate_offs for c in range(2)]

    def body(u_ref, y_ref, g_ref, w_ref, g0, g1, g2, g3, oa_ref, ob_ref, dy_ref, do_ref, dgate_ref, dg_ref):
        @pl.when(pl.program_id(0) == 0)
        def _():
            dg_ref[...] = jnp.zeros_like(dg_ref)
        dy, dgrow = _norm_bwd(y_ref[...], g_ref[...], u_ref[...])
        dg_ref[...] += jnp.sum(dgrow, axis=0, keepdims=True)
        dyb = dy.astype(BF16)
        dy_ref[...] = dyb
        dog = _dot_nt(dyb, w_ref[...])
        gates = (g0, g1, g2, g3)
        for c in range(4):
            gt = gates[c][...]
            sg = _sigmoid(gt)
            o_ref = oa_ref if c < 2 else ob_ref
            ov = o_ref[:, (c % 2) * 256:(c % 2 + 1) * 256]
            dc = dog[:, c * 256:(c + 1) * 256]
            do_ref[:, c * 256:(c + 1) * 256] = dc * (gt * sg)
            dgate_ref[:, c * 256:(c + 1) * 256] = dc * ov * (sg * (1.0 + gt * (1.0 - sg)))

    row = lambda i: (i, 0)
    gspec = lambda c: pl.BlockSpec((tm, 256), lambda i: (i, gblk[c]))
    return pl.pallas_call(
        body, name=name, grid=(S // tm,),
        in_specs=[pl.BlockSpec((tm, D), row), pl.BlockSpec((tm, D), row), pl.BlockSpec((1, D), lambda i: (0, 0)),
                  pl.BlockSpec((D, D), lambda i: (0, 0)),
                  gspec(0), gspec(1), gspec(2), gspec(3),
                  pl.BlockSpec((tm, 512), lambda i: (i, oblk_a)),
                  pl.BlockSpec((tm, 512), lambda i: (i, oblk_b))],
        out_specs=[pl.BlockSpec((tm, D), row), pl.BlockSpec((tm, D), row), pl.BlockSpec((tm, D), row),
                   pl.BlockSpec((1, D), lambda i: (0, 0))],
        out_shape=[jax.ShapeDtypeStruct((S, D), BF16), jax.ShapeDtypeStruct((S, D), F32),
                   jax.ShapeDtypeStruct((S, D), F32), jax.ShapeDtypeStruct((1, D), F32)],
        compiler_params=_cparams(("arbitrary",)),
    )(dx_up, y, g, w, proj, proj, proj, proj, o_a, o_b)


def _rope_tables(pos, invf, name):
    S = pos.shape[0]
    tm = _pick(S, (512, 256))

    def body(p_ref, f_ref, c_ref, s1_ref, s2_ref):
        lane = lax.broadcasted_iota(jnp.int32, (1, LANES), 1)
        ang = p_ref[...].astype(F32) * f_ref[...]
        c, s = jnp.cos(ang), jnp.sin(ang)
        c_ref[...] = jnp.where((lane >= 64) & (lane < 96), c, 1.0)
        s1_ref[...] = jnp.where((lane >= 64) & (lane < 80), -s, 0.0)
        s2_ref[...] = jnp.where((lane >= 80) & (lane < 96), s, 0.0)

    spec = pl.BlockSpec((tm, LANES), lambda i: (i, 0))
    return pl.pallas_call(
        body, name=name, grid=(S // tm,),
        in_specs=[pl.BlockSpec((tm, 1), lambda i: (i, 0)), pl.BlockSpec((1, LANES), lambda i: (0, 0))],
        out_specs=[spec, spec, spec],
        out_shape=[jax.ShapeDtypeStruct((S, LANES), F32)] * 3,
        compiler_params=_cparams(("parallel",)),
    )(pos, invf)


def _rope(x, c, s1, s2):
    return x * c + pltpu.roll(x, LANES - 16, 1) * s1 + pltpu.roll(x, 16, 1) * s2


def _rope_t(d, c, s1, s2):
    return d * c + pltpu.roll(d * s1, 16, 1) + pltpu.roll(d * s2, LANES - 16, 1)


def _mla_prep(proj, gq, gkv, wq, wkv, cosT, s1T, s2T, name):
    S = proj.shape[0]
    tm = _pick(S, (256,))

    def body(p_ref, gq_ref, gkv_ref, wq_ref, wkv_ref, c_ref, s1_ref, s2_ref, q_ref, k_ref, v_ref, qn_ref, cn_ref):
        qa = p_ref[:, 0:384]
        ckv = p_ref[:, 384:640]
        kr = p_ref[:, 640:768]
        qn32 = (qa * _rstd(qa)) * gq_ref[...]
        cn32 = (ckv * _rstd(ckv)) * gkv_ref[...]
        qn = qn32.astype(BF16)
        cn = cn32.astype(BF16)
        qn_ref[...] = qn32.T.astype(BF16)
        cn_ref[...] = cn32.T.astype(BF16)
        qb = _dot(qn, wq_ref[...])
        kvb = _dot(cn, wkv_ref[...])
        c, s1, s2 = c_ref[...], s1_ref[...], s2_ref[...]
        krr = _rope(kr, c, s1, s2)
        for h in range(8):
            sl = slice(h * LANES, (h + 1) * LANES)
            q_ref[:, sl] = _rope(qb[:, sl], c, s1, s2)
            k_ref[:, sl] = kvb[:, sl] + krr
        v_ref[...] = kvb[:, 1024:1536]

    row = lambda i: (i, 0)
    fixed = lambda i: (0, 0)
    tspec = pl.BlockSpec((tm, LANES), row)
    return pl.pallas_call(
        body, name=name, grid=(S // tm,),
        in_specs=[pl.BlockSpec((tm, L0_PREP_W), lambda i: (i, L0_PREP // L0_PREP_W)),
                  pl.BlockSpec((1, 384), fixed), pl.BlockSpec((1, 256), fixed),
                  pl.BlockSpec((384, 1024), fixed), pl.BlockSpec((256, 1536), fixed), tspec, tspec, tspec],
        out_specs=[pl.BlockSpec((tm, 1024), row), pl.BlockSpec((tm, 1024), row), pl.BlockSpec((tm, 512), row),
                   pl.BlockSpec((384, tm), lambda i: (0, i)), pl.BlockSpec((256, tm), lambda i: (0, i))],
        out_shape=[jax.ShapeDtypeStruct((S, 1024), F32), jax.ShapeDtypeStruct((S, 1024), F32),
                   jax.ShapeDtypeStruct((S, 512), F32), jax.ShapeDtypeStruct((384, S), BF16),
                   jax.ShapeDtypeStruct((256, S), BF16)],
        compiler_params=_cparams(("parallel",)),
    )(proj, gq, gkv, wq, wkv, cosT, s1T, s2T)


def _mla_prep_bwd(dq, dk, dv, proj, gq, gkv, wq, wkv, cosT, s1T, s2T, name):
    S = proj.shape[0]
    tm = _pick(S, (256,))

    def body(dq_ref, dk_ref, dv_ref, p_ref, gq_ref, gkv_ref, wq_ref, wkv_ref, c_ref, s1_ref, s2_ref,
             dp_ref, dqb_ref, dkvb_ref, dgq_ref, dgkv_ref):
        @pl.when(pl.program_id(0) == 0)
        def _():
            dgq_ref[...] = jnp.zeros_like(dgq_ref)
            dgkv_ref[...] = jnp.zeros_like(dgkv_ref)
        c, s1, s2 = c_ref[...], s1_ref[...], s2_ref[...]
        lane = lax.broadcasted_iota(jnp.int32, (1, LANES), 1)
        dkr = jnp.zeros((tm, LANES), F32)
        for h in range(8):
            sl = slice(h * LANES, (h + 1) * LANES)
            dqb_ref[:, sl] = _rope_t(dq_ref[:, sl], c, s1, s2).astype(BF16)
            dkh = dk_ref[:, sl]
            dkvb_ref[:, sl] = dkh.astype(BF16)
            dkr = dkr + dkh
        dkvb_ref[:, 1024:1536] = dv_ref[...].astype(BF16)
        dkr = jnp.where((lane >= 64) & (lane < 96), _rope_t(dkr, c, s1, s2), 0.0)
        dqn = _dot_nt(dqb_ref[...], wq_ref[...])
        dcn = _dot_nt(dkvb_ref[...], wkv_ref[...])
        dqa, gq_row = _norm_bwd(p_ref[:, 0:384], gq_ref[...], dqn)
        dckv, gkv_row = _norm_bwd(p_ref[:, 384:640], gkv_ref[...], dcn)
        dp_ref[:, 0:384] = dqa
        dp_ref[:, 384:640] = dckv
        dp_ref[:, 640:768] = dkr
        dgq_ref[...] += jnp.sum(gq_row, axis=0, keepdims=True)
        dgkv_ref[...] += jnp.sum(gkv_row, axis=0, keepdims=True)

    row = lambda i: (i, 0)
    fixed = lambda i: (0, 0)
    tspec = pl.BlockSpec((tm, LANES), row)
    return pl.pallas_call(
        body, name=name, grid=(S // tm,),
        in_specs=[pl.BlockSpec((tm, 1024), row), pl.BlockSpec((tm, 1024), row), pl.BlockSpec((tm, 512), row),
                  pl.BlockSpec((tm, L0_PREP_W), lambda i: (i, L0_PREP // L0_PREP_W)),
                  pl.BlockSpec((1, 384), fixed), pl.BlockSpec((1, 256), fixed),
                  pl.BlockSpec((384, 1024), fixed), pl.BlockSpec((256, 1536), fixed), tspec, tspec, tspec],
        out_specs=[pl.BlockSpec((tm, L0_PREP_W), row), pl.BlockSpec((tm, 1024), row), pl.BlockSpec((tm, 1536), row),
                   pl.BlockSpec((1, 384), fixed), pl.BlockSpec((1, 256), fixed)],
        out_shape=[jax.ShapeDtypeStruct((S, L0_PREP_W), F32), jax.ShapeDtypeStruct((S, 1024), BF16),
                   jax.ShapeDtypeStruct((S, 1536), BF16), jax.ShapeDtypeStruct((1, 384), F32),
                   jax.ShapeDtypeStruct((1, 256), F32)],
        compiler_params=_cparams(("arbitrary",)),
    )(dq, dk, dv, proj, gq, gkv, wq, wkv, cosT, s1T, s2T)


def _fox_prep(proj, bf, name):
    S = proj.shape[0]
    tm = _pick(S, (256,))

    def body(f_ref, b_ref, c_ref, carry_ref):
        @pl.when(pl.program_id(0) == 0)
        def _():
            carry_ref[...] = jnp.zeros_like(carry_ref)
        u = f_ref[...] + b_ref[...]
        lf = jnp.minimum(u, 0.0) - jnp.log(1.0 + jnp.exp(-jnp.abs(u)))
        r = lax.broadcasted_iota(jnp.int32, (tm, tm), 0)
        cidx = lax.broadcasted_iota(jnp.int32, (tm, tm), 1)
        tri = (cidx <= r).astype(BF16)
        hi, mid, lo = _split3(lf)
        c = carry_ref[...] + (_dot(tri, hi) + _dot(tri, mid) + _dot(tri, lo))
        c_ref[...] = c
        carry_ref[...] = c[tm - 1:tm, :]

    return pl.pallas_call(
        body, name=name, grid=(S // tm,),
        in_specs=[pl.BlockSpec((tm, LANES), lambda i: (i, L1_F // LANES)), pl.BlockSpec((1, LANES), lambda i: (0, 0))],
        out_specs=pl.BlockSpec((tm, LANES), lambda i: (i, 0)),
        out_shape=jax.ShapeDtypeStruct((S, LANES), F32),
        scratch_shapes=[pltpu.VMEM((1, LANES), F32)],
        compiler_params=_cparams(("arbitrary",)),
    )(proj, bf)


def _fox_prep_bwd(dc, proj, bf, name):
    S = proj.shape[0]
    tm = _pick(S, (256,))
    nb = S // tm

    def body(dc_ref, f_ref, b_ref, df_ref, db_ref, carry_ref):
        @pl.when(pl.program_id(0) == 0)
        def _():
            carry_ref[...] = jnp.zeros_like(carry_ref)
            db_ref[...] = jnp.zeros_like(db_ref)
        r = lax.broadcasted_iota(jnp.int32, (tm, tm), 0)
        cidx = lax.broadcasted_iota(jnp.int32, (tm, tm), 1)
        tri = (cidx >= r).astype(BF16)
        hi, mid, lo = _split3(dc_ref[...])
        dlf = carry_ref[...] + (_dot(tri, hi) + _dot(tri, mid) + _dot(tri, lo))
        carry_ref[...] = dlf[0:1, :]
        u = f_ref[...] + b_ref[...]
        e = jnp.exp(-jnp.abs(u))
        sneg = jnp.where(u >= 0.0, e, 1.0) / (1.0 + e)
        lane = lax.broadcasted_iota(jnp.int32, (1, LANES), 1)
        df = jnp.where(lane < FOX_HEADS, dlf * sneg, 0.0)
        df_ref[...] = df
        db_ref[...] += jnp.sum(df, axis=0, keepdims=True)

    return pl.pallas_call(
        body, name=name, grid=(nb,),
        in_specs=[pl.BlockSpec((tm, LANES), lambda i: (nb - 1 - i, 0)),
                  pl.BlockSpec((tm, LANES), lambda i: (nb - 1 - i, L1_F // LANES)),
                  pl.BlockSpec((1, LANES), lambda i: (0, 0))],
        out_specs=[pl.BlockSpec((tm, LANES), lambda i: (nb - 1 - i, 0)), pl.BlockSpec((1, LANES), lambda i: (0, 0))],
        out_shape=[jax.ShapeDtypeStruct((S, LANES), F32), jax.ShapeDtypeStruct((1, LANES), F32)],
        scratch_shapes=[pltpu.VMEM((1, LANES), F32)],
        compiler_params=_cparams(("arbitrary",)),
    )(dc, proj, bf)


def _att_specs(kind, S, T):
    if kind == "sb":
        qo, ko, vo, go = L0_SBQ // LANES, L0_SBK // LANES, L0_SBV // LANES, L0_SBG // LANES
    elif kind == "fox":
        qo, ko, vo, go = L1_Q // LANES, L1_K // LANES, L1_V // LANES, L1_G // LANES
    else:
        go = L0_MLG // LANES
        return (pl.BlockSpec((T, 256), lambda p, i: (i, p)), pl.BlockSpec((S, 256), lambda p, i: (0, p)),
                pl.BlockSpec((S, LANES), lambda p, i: (0, p)), pl.BlockSpec((T, LANES), lambda p, i: (i, go + p)))
    return (pl.BlockSpec((T, LANES), lambda p, i: (i, qo + p)), pl.BlockSpec((S, LANES), lambda p, i: (0, ko + p)),
            pl.BlockSpec((S, LANES), lambda p, i: (0, vo + p)), pl.BlockSpec((T, LANES), lambda p, i: (i, go + p)))


def _per_q_tile(tile_body, hows):
    T = ATT_T

    def view(ref, u, how):
        if how == "rows":
            return ref.at[pl.ds(u * T, T)]
        if how == "lanes":
            return ref.at[:, pl.ds(u * T, T)]
        if how == "stat":
            return ref.at[:, u]
        return ref

    def body(*refs):
        for u in range(ATT_QSUB):
            tile_body(pl.program_id(1) * ATT_QSUB + u, *[view(r, u, how) for r, how in zip(refs, hows)])

    return body


def _mask_flags(js, masked_at):
    return [t == masked_at for t in range(len(js))]


def _loop_tiles(i, tiles, right_to_left, G=ATT_GROUP):
    ng = i // G
    rest = i - ng * G

    def leftover():
        for r in range(G):
            @pl.when(rest == r)
            def _():
                if right_to_left:
                    tiles([i - u for u in range(r + 1)], 0)
                else:
                    tiles([ng * G + u for u in range(r + 1)], r)

    def group(g, carry):
        if right_to_left:
            tiles([ng * G - 1 - (g * G + u) for u in range(G)], None)
        else:
            tiles([g * G + u for u in range(G)], None)
        return carry

    if right_to_left:
        leftover()
    lax.fori_loop(0, ng, group, 0)
    if not right_to_left:
        leftover()


def _head_q(kind, q_ref, m0, scale):
    if kind == "mla":
        return [q_ref[:, 0:LANES].astype(BF16), q_ref[:, LANES:2 * LANES].astype(BF16)]
    qv = q_ref[...] * scale
    return [jnp.where(m0, qv, 0.0).astype(BF16), jnp.where(m0, 0.0, qv).astype(BF16)]


def _head_k(kind, k_ref, start, T):
    if kind == "mla":
        return [k_ref[pl.ds(start, T), 0:LANES].astype(BF16), k_ref[pl.ds(start, T), LANES:2 * LANES].astype(BF16)]
    kb = k_ref[pl.ds(start, T), :].astype(BF16)
    return [kb, kb]


def _transpose_tiles(src, col_off, n_out, cw, group, name):
    S = src.shape[0]
    T = ATT_T
    first = col_off // (group * cw)

    def body(x_ref, o_ref):
        for u in range(group):
            o_ref[u] = x_ref[:, u * cw:(u + 1) * cw].T.astype(BF16)

    return pl.pallas_call(
        body, name=name, grid=(S // T, n_out // group),
        in_specs=[pl.BlockSpec((T, group * cw), lambda j, g: (j, first + g))],
        out_specs=pl.BlockSpec((group, None, cw, T), lambda j, g: (g, j, 0, 0)),
        out_shape=jax.ShapeDtypeStruct((n_out, S // T, cw, T), BF16),
        compiler_params=_cparams(("parallel", "parallel")),
    )(src)


def _softmax_fwd(kind, qkvg, c_col, S, npairs, name):
    T = ATT_T
    nq = S // T
    fox = kind == "fox"
    scale = (96 if kind == "mla" else 64) ** -0.5

    def body(i, *refs):
        if fox:
            q_ref, k_ref, vt_ref, g_ref, cc_ref, o_ref, og_ref, ogt_ref, st_ref, m_ref, acc_ref = refs
        else:
            q_ref, k_ref, vt_ref, g_ref, o_ref, og_ref, ogt_ref, st_ref, m_ref, acc_ref = refs
        m0 = lax.broadcasted_iota(jnp.int32, (1, LANES), 1) < 64
        top = lax.broadcasted_iota(jnp.int32, (LANES, 1), 0) < 64
        key = lax.broadcasted_iota(jnp.int32, (T, LANES), 0)
        qrow = lax.broadcasted_iota(jnp.int32, (T, LANES), 1)
        qh = _head_q(kind, q_ref, m0, scale)
        m_ref[...] = jnp.full(m_ref.shape, NEG, F32)
        acc_ref[...] = jnp.zeros(acc_ref.shape, F32)
        chains = [(h, b) for h in range(2) for b in range(T // LANES)]

        def tiles(js, masked_at):
            starts = [pl.multiple_of(j * T, T) for j in js]
            zss = []
            for start in starts:
                kh = _head_k(kind, k_ref, start, T)
                zss.append(_split_blocks([_dot_nt(kh[h], qh[h]) for h in range(2)]))
            pss, alss = [], []
            for start, zs, masked in zip(starts, zss, _mask_flags(js, masked_at)):
                ps, alphas = [], []
                for (h, b), z in zip(chains, zs):
                    lanes = slice(b * LANES, (b + 1) * LANES)
                    if kind == "mla":
                        z = z * scale
                    if fox:
                        z = z - cc_ref[h, pl.ds(start, T), :]
                    if masked:
                        z = jnp.where(key <= qrow + b * LANES, z, NEG)
                    m_prev = m_ref[h, :, lanes]
                    m_new = jnp.maximum(m_prev, jnp.max(z, axis=0, keepdims=True))
                    alphas.append(jnp.exp(m_prev - m_new))
                    ps.append(jnp.exp(z - m_new).astype(BF16))
                    m_ref[h, :, lanes] = m_new
                pss.append(_join_blocks(ps, T // LANES))
                alss.append(_join_blocks(alphas, T // LANES))
            for j, ps, alphas in zip(js, pss, alss):
                vt = vt_ref[j]
                vth = [jnp.where(top, vt, 1.0).astype(BF16), jnp.where(top, 1.0, vt).astype(BF16)]
                for h in range(2):
                    acc_ref[h] = alphas[h] * acc_ref[h] + _dot(vth[h], ps[h])

        _loop_tiles(i, tiles, False, 2 * ATT_GROUP)
        acc = [acc_ref[0], acc_ref[1]]
        ot = jnp.concatenate([acc[0][0:64] / acc[0][64:128], acc[1][64:128] / acc[1][0:64]], axis=0)
        o = ot.T
        o_ref[...] = o
        gt = g_ref[...]
        og = o * (gt * _sigmoid(gt))
        og_ref[...] = og.astype(BF16)
        ogt_ref[...] = og.T.astype(BF16)
        st_ref[0] = m_ref[0] + jnp.log(acc[0][64:65])
        st_ref[1] = m_ref[1] + jnp.log(acc[1][0:1])

    QT = ATT_QSUB * T
    qs, ks, _, gs = _att_specs(kind, S, QT)
    in_specs = [qs, ks, pl.BlockSpec((None, nq, LANES, T), lambda p, i: (p, 0, 0, 0)), gs]
    args = list(qkvg)
    hows = ["rows", None, None, "rows"]
    if fox:
        in_specs += [pl.BlockSpec((2, S, LANES), lambda p, i: (p, 0, 0))]
        args += [c_col]
        hows += [None]
    hows += ["rows", "rows", "lanes", "stat", None, None]
    W = npairs * LANES
    return pl.pallas_call(
        _per_q_tile(body, hows), name=name, grid=(npairs, nq // ATT_QSUB), in_specs=in_specs,
        out_specs=[pl.BlockSpec((QT, LANES), lambda p, i: (i, p)), pl.BlockSpec((QT, LANES), lambda p, i: (i, p)),
                   pl.BlockSpec((LANES, QT), lambda p, i: (p, i)),
                   pl.BlockSpec((2, ATT_QSUB, 1, T), lambda p, i: (p, i, 0, 0))],
        out_shape=[jax.ShapeDtypeStruct((S, W), F32), jax.ShapeDtypeStruct((S, W), BF16),
                   jax.ShapeDtypeStruct((W, S), BF16),
                   jax.ShapeDtypeStruct((2 * npairs, nq, 1, T), F32)],
        scratch_shapes=[pltpu.VMEM((2, 1, T), F32), pltpu.VMEM((2, LANES, T), F32)],
        compiler_params=_cparams(("parallel", "parallel")),
    )(*args)


def _softplus_parts(z):
    sp = jnp.maximum(z, 0.0) + jnp.log(1.0 + jnp.exp(-jnp.abs(z)))
    return sp, z - sp


def _cumsum_dot(tri2, his, los):
    return _split_blocks([_dot(tri2, jnp.concatenate([hi, lo], axis=0)) for hi, lo in zip(his, los)])


def _split2(x):
    hi = x.astype(BF16)
    return hi, (x - hi.astype(F32)).astype(BF16)


def _split_blocks(per_head):
    return [x[:, b * LANES:(b + 1) * LANES] for x in per_head for b in range(x.shape[1] // LANES)]


def _join_blocks(per_block, nb):
    return [jnp.concatenate(per_block[h * nb:(h + 1) * nb], axis=1) for h in range(len(per_block) // nb)]


def _row_of(col):
    return jnp.broadcast_to(col, (col.shape[0], LANES)).T[0:1]


def _softmax_bwd_t(kind, q, k, kt, v, do, do_off, o, lse, c_col, S, npairs, name):
    T = ATT_T
    nq = S // T
    nb = T // LANES
    fox = kind == "fox"
    mla = kind == "mla"
    scale = (96 if mla else 64) ** -0.5
    kw = 256 if mla else LANES

    def body(i, *refs):
        if fox:
            (q_ref, k_ref, kt_ref, v_ref, do_ref, o_ref, st_ref, cc_ref,
             dq_ref, dk_ref, dv_ref, dck_ref, dcq_ref, dqt_ref, rs_ref, dkx_ref) = refs
        else:
            q_ref, k_ref, kt_ref, v_ref, do_ref, o_ref, st_ref, dq_ref, dk_ref, dv_ref, dqt_ref = refs

        @pl.when(i == 0)
        def _():
            dv_ref[...] = jnp.zeros_like(dv_ref)
            if fox:
                dkx_ref[...] = jnp.zeros_like(dkx_ref)
            else:
                dk_ref[...] = jnp.zeros_like(dk_ref)

        m0 = lax.broadcasted_iota(jnp.int32, (1, LANES), 1) < 64
        top = lax.broadcasted_iota(jnp.int32, (LANES, 1), 0) < 64
        key = lax.broadcasted_iota(jnp.int32, (T, LANES), 0)
        qrow = lax.broadcasted_iota(jnp.int32, (T, LANES), 1)
        qh = _head_q(kind, q_ref, m0, scale)
        if fox:
            qv = q_ref[...] * scale
            qk = [jnp.where(m0, qv, 1.0).astype(BF16), jnp.where(m0, 1.0, qv).astype(BF16)]
        else:
            qk = qh
        dov = do_ref[...]
        prod = dov * o_ref[...]
        dd = [_row_of(jnp.sum(jnp.where(m0, prod, 0.0), axis=1, keepdims=True)),
              _row_of(jnp.sum(jnp.where(m0, 0.0, prod), axis=1, keepdims=True))]
        doh = [jnp.where(m0, dov, 0.0).astype(BF16), jnp.where(m0, 0.0, dov).astype(BF16)]
        lse = [st_ref[0], st_ref[1]]
        dqt_ref[...] = jnp.zeros_like(dqt_ref)
        if fox:
            rs_ref[...] = jnp.zeros_like(rs_ref)
        chains = [(h, b) for h in range(2) for b in range(nb)]

        def tiles(js, masked_at):
            starts = [pl.multiple_of(j * T, T) for j in js]
            zss, dpss = [], []
            for start in starts:
                vb = v_ref[pl.ds(start, T), :].astype(BF16)
                kh = _head_k(kind, k_ref, start, T)
                zss.append(_split_blocks([_dot_nt(kh[h], qh[h]) for h in range(2)]))
                dpss.append(_split_blocks([_dot_nt(vb, doh[h]) for h in range(2)]))
            pss, dsss = [], []
            for start, zs, dps, masked in zip(starts, zss, dpss, _mask_flags(js, masked_at)):
                ps, dss = [], []
                for (h, b), z, dp in zip(chains, zs, dps):
                    lanes = slice(b * LANES, (b + 1) * LANES)
                    if mla:
                        z = z * scale
                    if fox:
                        z = z - cc_ref[h, pl.ds(start, T), :]
                    if masked:
                        z = jnp.where(key <= qrow + b * LANES, z, NEG)
                    p = jnp.exp(z - lse[h][:, lanes])
                    ds = p * (dp - dd[h][:, lanes])
                    dsb = ds.astype(BF16)
                    if fox:
                        rs_ref[h, :, lanes] += jnp.sum(dsb.astype(F32), axis=0, keepdims=True)
                    ps.append(p.astype(BF16))
                    dss.append(dsb)
                pss.append(_join_blocks(ps, nb))
                dsss.append(_join_blocks(dss, nb))
            for j, start, ps, dss in zip(js, starts, pss, dsss):
                kt = kt_ref[j]
                dvc = None
                for h in range(2):
                    dkh = _dot(dss[h], qk[h])
                    dvh = _dot(ps[h], doh[h])
                    dvc = dvh if dvc is None else dvc + dvh
                    kth = kt[h * LANES:(h + 1) * LANES] if mla else kt
                    dqt_ref[h] += _dot(kth, dss[h])
                    if fox:
                        dkx_ref[h, pl.ds(start, T), :] += dkh
                    elif mla:
                        dk_ref[pl.ds(start, T), h * LANES:(h + 1) * LANES] += dkh * scale
                    else:
                        dk_ref[pl.ds(start, T), :] += dkh
                dv_ref[pl.ds(start, T), :] += dvc

        _loop_tiles(i, tiles, False)
        if mla:
            dq_ref[:, 0:LANES] = dqt_ref[0].T * scale
            dq_ref[:, LANES:2 * LANES] = dqt_ref[1].T * scale
        else:
            dq_ref[...] = jnp.where(top, dqt_ref[0], dqt_ref[1]).T * scale
        if fox:
            dcq_ref[0] = rs_ref[0]
            dcq_ref[1] = rs_ref[1]

            @pl.when(i == nq - 1)
            def _():
                dk_ref[...] = jnp.where(m0, dkx_ref[0], dkx_ref[1])
                dck_ref[0] = dkx_ref[0].T[64:65]
                dck_ref[1] = dkx_ref[1].T[0:1]

    QT = ATT_QSUB * T
    qs, ks, vs, _ = _att_specs(kind, S, QT)
    stat = pl.BlockSpec((2, ATT_QSUB, 1, T), lambda p, i: (p, i, 0, 0))
    in_specs = [qs, ks, pl.BlockSpec((None, nq, kw, T), lambda p, i: (p, 0, 0, 0)), vs,
                pl.BlockSpec((QT, LANES), lambda p, i: (i, do_off + p)),
                pl.BlockSpec((QT, LANES), lambda p, i: (i, p)), stat]
    args = [q, k, kt, v, do, o, lse]
    hows = ["rows", None, None, None, "rows", "rows", "stat"]
    W = npairs * LANES
    out_specs = [pl.BlockSpec((QT, kw), lambda p, i: (i, p)), pl.BlockSpec((S, kw), lambda p, i: (0, p)),
                 pl.BlockSpec((S, LANES), lambda p, i: (0, p))]
    out_shape = [jax.ShapeDtypeStruct((S, npairs * kw), F32), jax.ShapeDtypeStruct((S, npairs * kw), F32),
                 jax.ShapeDtypeStruct((S, W), F32)]
    scratch = [pltpu.VMEM((2, LANES, T), F32)]
    if fox:
        in_specs.append(pl.BlockSpec((2, S, LANES), lambda p, i: (p, 0, 0)))
        args.append(c_col)
        out_specs += [pl.BlockSpec((2, 1, S), lambda p, i: (p, 0, 0)), stat]
        out_shape += [jax.ShapeDtypeStruct((2 * npairs, 1, S), F32), jax.ShapeDtypeStruct((2 * npairs, nq, 1, T), F32)]
        scratch += [pltpu.VMEM((2, 1, T), F32), pltpu.VMEM((2, S, LANES), F32)]
        hows += [None, "rows", None, None, None, "stat", None, None, None]
    else:
        hows += ["rows", None, None, None]
    return pl.pallas_call(
        _per_q_tile(body, hows), name=name, grid=(npairs, nq // ATT_QSUB), in_specs=in_specs, out_specs=out_specs,
        out_shape=out_shape, scratch_shapes=scratch, compiler_params=_cparams(("parallel", "arbitrary")),
    )(*args)


def _sb_fwd_t(proj, vt, S, npairs, name):
    T = ATT_T
    nq = S // T
    nb = T // LANES
    scale = 64 ** -0.5

    def body(i, q_ref, k_ref, vt_ref, g_ref, o_ref, og_ref, ogt_ref, st_ref, rem_ref, acc_ref):
        m0 = lax.broadcasted_iota(jnp.int32, (1, LANES), 1) < 64
        top = lax.broadcasted_iota(jnp.int32, (LANES, 1), 0) < 64
        key = lax.broadcasted_iota(jnp.int32, (T, LANES), 0)
        qrow = lax.broadcasted_iota(jnp.int32, (T, LANES), 1)
        r = lax.broadcasted_iota(jnp.int32, (T, T), 0)
        c = lax.broadcasted_iota(jnp.int32, (T, T), 1)
        after = (c > r).astype(BF16)
        after2 = jnp.concatenate([after, after], axis=1)
        qh = _head_q("sb", q_ref, m0, scale)
        rem_ref[...] = jnp.zeros_like(rem_ref)
        acc_ref[...] = jnp.zeros_like(acc_ref)
        chains = [(h, b) for h in range(2) for b in range(nb)]

        def tiles(js, masked_at):
            zss = []
            for j in js:
                kb = k_ref[pl.ds(pl.multiple_of(j * T, T), T), :].astype(BF16)
                zss.append(_split_blocks([_dot_nt(kb, qh[h]) for h in range(2)]))
            lass, sums, hiss, loss = [], [], [], []
            for zs, masked in zip(zss, _mask_flags(js, masked_at)):
                las, sm, his, los = [], [], [], []
                for (h, b), z in zip(chains, zs):
                    sp, la = _softplus_parts(z)
                    if masked:
                        sp = jnp.where(key < qrow + b * LANES, sp, 0.0)
                    hi, lo = _split2(sp)
                    las.append(la)
                    sm.append(jnp.sum(sp, axis=0, keepdims=True))
                    his.append(hi)
                    los.append(lo)
                lass.append(las)
                sums.append(sm)
                hiss.append(_join_blocks(his, nb))
                loss.append(_join_blocks(los, nb))
            rcss = [_cumsum_dot(after2, his, los) for his, los in zip(hiss, loss)]
            wss = []
            for las, sm, rcs, masked in zip(lass, sums, rcss, _mask_flags(js, masked_at)):
                ws = []
                for (h, b), la, s, rc in zip(chains, las, sm, rcs):
                    lanes = slice(b * LANES, (b + 1) * LANES)
                    w = jnp.exp(la - (rem_ref[h, :, lanes] + rc))
                    if masked:
                        w = jnp.where(key < qrow + b * LANES, w, 0.0)
                    ws.append(w.astype(BF16))
                    rem_ref[h, :, lanes] += s
                wss.append(_join_blocks(ws, nb))
            for j, ws in zip(js, wss):
                vtb = vt_ref[j]
                for h in range(2):
                    acc_ref[h] += _dot(vtb, ws[h])

        _loop_tiles(i, tiles, True)
        o = jnp.where(top, acc_ref[0], acc_ref[1]).T
        o_ref[...] = o
        gt = g_ref[...]
        og = o * (gt * _sigmoid(gt))
        og_ref[...] = og.astype(BF16)
        ogt_ref[...] = og.T.astype(BF16)
        st_ref[0] = rem_ref[0]
        st_ref[1] = rem_ref[1]

    QT = ATT_QSUB * T
    qs, ks, _, gs = _att_specs("sb", S, QT)
    W = npairs * LANES
    hows = ["rows", None, None, "rows", "rows", "rows", "lanes", "stat", None, None]
    return pl.pallas_call(
        _per_q_tile(body, hows), name=name, grid=(npairs, nq // ATT_QSUB),
        in_specs=[qs, ks, pl.BlockSpec((None, nq, LANES, T), lambda p, i: (p, 0, 0, 0)), gs],
        out_specs=[pl.BlockSpec((QT, LANES), lambda p, i: (i, p)), pl.BlockSpec((QT, LANES), lambda p, i: (i, p)),
                   pl.BlockSpec((LANES, QT), lambda p, i: (p, i)),
                   pl.BlockSpec((2, ATT_QSUB, 1, T), lambda p, i: (p, i, 0, 0))],
        out_shape=[jax.ShapeDtypeStruct((S, W), F32), jax.ShapeDtypeStruct((S, W), BF16),
                   jax.ShapeDtypeStruct((W, S), BF16),
                   jax.ShapeDtypeStruct((2 * npairs, nq, 1, T), F32)],
        scratch_shapes=[pltpu.VMEM((2, 1, T), F32), pltpu.VMEM((2, LANES, T), F32)],
        compiler_params=_cparams(("parallel", "parallel")),
    )(proj, proj, vt, proj)


def _sb_bwd_t(proj, kt, do, tot, S, npairs, name):
    T = ATT_T
    nq = S // T
    nb = T // LANES
    scale = 64 ** -0.5

    def body(i, q_ref, k_ref, kt_ref, v_ref, do_ref, st_ref, dq_ref, dk_ref, dv_ref, dqt_ref, pre_ref, gpre_ref):

        @pl.when(i == 0)
        def _():
            dk_ref[...] = jnp.zeros_like(dk_ref)
            dv_ref[...] = jnp.zeros_like(dv_ref)

        m0 = lax.broadcasted_iota(jnp.int32, (1, LANES), 1) < 64
        top = lax.broadcasted_iota(jnp.int32, (LANES, 1), 0) < 64
        key = lax.broadcasted_iota(jnp.int32, (T, LANES), 0)
        qrow = lax.broadcasted_iota(jnp.int32, (T, LANES), 1)
        r = lax.broadcasted_iota(jnp.int32, (T, T), 0)
        c = lax.broadcasted_iota(jnp.int32, (T, T), 1)
        upto = (c <= r).astype(BF16)
        upto2 = jnp.concatenate([upto, upto], axis=1)
        left = (c < r).astype(BF16)
        qh = _head_q("sb", q_ref, m0, scale)
        dov = do_ref[...]
        doh = [jnp.where(m0, dov, 0.0).astype(BF16), jnp.where(m0, 0.0, dov).astype(BF16)]
        tot_h = [st_ref[0], st_ref[1]]
        dqt_ref[...] = jnp.zeros_like(dqt_ref)
        pre_ref[...] = jnp.zeros_like(pre_ref)
        gpre_ref[...] = jnp.zeros_like(gpre_ref)
        chains = [(h, b) for h in range(2) for b in range(nb)]

        def tiles(js, masked_at):
            starts = [pl.multiple_of(j * T, T) for j in js]
            zss, dwss = [], []
            for start in starts:
                vb = v_ref[pl.ds(start, T), :].astype(BF16)
                kb = k_ref[pl.ds(start, T), :].astype(BF16)
                zss.append(_split_blocks([_dot_nt(kb, qh[h]) for h in range(2)]))
                dwss.append(_split_blocks([_dot_nt(vb, doh[h]) for h in range(2)]))
            lass, sums, hiss, loss = [], [], [], []
            for zs, masked in zip(zss, _mask_flags(js, masked_at)):
                las, sm, his, los = [], [], [], []
                for (h, b), z in zip(chains, zs):
                    sp, la = _softplus_parts(z)
                    if masked:
                        sp = jnp.where(key < qrow + b * LANES, sp, 0.0)
                    hi, lo = _split2(sp)
                    las.append(la)
                    sm.append(jnp.sum(sp, axis=0, keepdims=True))
                    his.append(hi)
                    los.append(lo)
                lass.append(las)
                sums.append(sm)
                hiss.append(_join_blocks(his, nb))
                loss.append(_join_blocks(los, nb))
            pcss = [_cumsum_dot(upto2, his, los) for his, los in zip(hiss, loss)]
            wss, gss = [], []
            for las, sm, pcs, dws, masked in zip(lass, sums, pcss, dwss, _mask_flags(js, masked_at)):
                ws, gs = [], []
                for (h, b), la, s, pc, dw in zip(chains, las, sm, pcs, dws):
                    lanes = slice(b * LANES, (b + 1) * LANES)
                    w = jnp.exp(la - ((tot_h[h][:, lanes] - pre_ref[h, :, lanes]) - pc))
                    if masked:
                        w = jnp.where(key < qrow + b * LANES, w, 0.0)
                    ws.append(w.astype(BF16))
                    gs.append(dw * w)
                    pre_ref[h, :, lanes] += s
                wss.append(_join_blocks(ws, nb))
                gss.append(gs)
            gcss = [_split_blocks([_dot(left, g) for g in _join_blocks([g.astype(BF16) for g in gs], nb)]) for gs in gss]
            dzss = []
            for las, gs, gcs, masked in zip(lass, gss, gcss, _mask_flags(js, masked_at)):
                dzs = []
                for (h, b), la, g, gc in zip(chains, las, gs, gcs):
                    lanes = slice(b * LANES, (b + 1) * LANES)
                    dz = g - (g + (gpre_ref[h, :, lanes] + gc)) * jnp.exp(la)
                    if masked:
                        dz = jnp.where(key < qrow + b * LANES, dz, 0.0)
                    dzs.append(dz.astype(BF16))
                    gpre_ref[h, :, lanes] += jnp.sum(g, axis=0, keepdims=True)
                dzss.append(_join_blocks(dzs, nb))
            for j, start, ws, dzs in zip(js, starts, wss, dzss):
                kt = kt_ref[j]
                dkc = dvc = None
                for h in range(2):
                    dkh = _dot(dzs[h], qh[h])
                    dvh = _dot(ws[h], doh[h])
                    dkc = dkh if dkc is None else dkc + dkh
                    dvc = dvh if dvc is None else dvc + dvh
                    dqt_ref[h] += _dot(kt, dzs[h])
                dk_ref[pl.ds(start, T), :] += dkc
                dv_ref[pl.ds(start, T), :] += dvc

        _loop_tiles(i, tiles, False)
        dq_ref[...] = jnp.where(top, dqt_ref[0], dqt_ref[1]).T * scale

    QT = ATT_QSUB * T
    qs, ks, vs, _ = _att_specs("sb", S, QT)
    W = npairs * LANES
    hows = ["rows", None, None, None, "rows", "stat", "rows", None, None, None, None, None]
    return pl.pallas_call(
        _per_q_tile(body, hows), name=name, grid=(npairs, nq // ATT_QSUB),
        in_specs=[qs, ks, pl.BlockSpec((None, nq, LANES, T), lambda p, i: (p, 0, 0, 0)), vs,
                  pl.BlockSpec((QT, LANES), lambda p, i: (i, p)),
                  pl.BlockSpec((2, ATT_QSUB, 1, T), lambda p, i: (p, i, 0, 0))],
        out_specs=[pl.BlockSpec((QT, LANES), lambda p, i: (i, p)), pl.BlockSpec((S, LANES), lambda p, i: (0, p)),
                   pl.BlockSpec((S, LANES), lambda p, i: (0, p))],
        out_shape=[jax.ShapeDtypeStruct((S, W), F32)] * 3,
        scratch_shapes=[pltpu.VMEM((2, LANES, T), F32), pltpu.VMEM((2, 1, T), F32), pltpu.VMEM((2, 1, T), F32)],
        compiler_params=_cparams(("parallel", "arbitrary")),
    )(proj, proj, kt, proj, do, tot)


def _pad_w0(w):
    z = lambda n: jnp.zeros((w.shape[0], n), w.dtype)
    return jnp.concatenate([w[:, 2048:2432], w[:, 2432:2688], z(64), w[:, 2688:2720], z(32),
                            w[:, 1536:2048], w[:, 2720:3232], w[:, 0:512], w[:, 512:1024], w[:, 1024:1536]], axis=1)


def _unpad_w0(wp):
    return jnp.concatenate([wp[:, L0_SBQ:L0_SBQ + 512], wp[:, L0_SBK:L0_SBK + 512], wp[:, L0_SBV:L0_SBV + 512],
                            wp[:, L0_SBG:L0_SBG + 512], wp[:, 0:384], wp[:, 384:640], wp[:, 704:736],
                            wp[:, L0_MLG:L0_MLG + 512]], axis=1)


def _pad_wq(w):
    return jnp.pad(w.reshape(384, 8, 96), ((0, 0), (0, 0), (0, 32))).reshape(384, 1024)


def _unpad_wq(wp):
    return wp.reshape(384, 8, 128)[:, :, :96].reshape(384, 768)


def _pad_wkv(w):
    w3 = w.reshape(256, 8, 128)
    k = jnp.pad(w3[:, :, :64], ((0, 0), (0, 0), (0, 64))).reshape(256, 1024)
    return jnp.concatenate([k, w3[:, :, 64:].reshape(256, 512)], axis=1)


def _unpad_wkv(wp):
    k = wp[:, :1024].reshape(256, 8, 128)[:, :, :64]
    v = wp[:, 1024:].reshape(256, 8, 64)
    return jnp.concatenate([k, v], axis=-1).reshape(256, 1024)


def _pad_w1(w):
    return jnp.concatenate([w, jnp.zeros((w.shape[0], L1_WIDTH - ODD_IN_WIDTH), w.dtype)], axis=1)


def _local_step(x, positions, target, g, w0p, wqp, wkvp, wo0, w1p, wo1):
    S = x.shape[0]
    nq = S // ATT_T
    invf = ROPE_THETA ** (-jnp.arange(0, MLA_ROPE_DIM, 2, dtype=F32) / MLA_ROPE_DIM)
    invf = jnp.concatenate([jnp.zeros((64,), F32), invf, invf, jnp.zeros((32,), F32)]).reshape(1, LANES)
    cosT, s1T, s2T = _rope_tables(positions.reshape(S, 1), invf, "rope_tables")
    bfp = jnp.pad(g["l1_b_f"], ((0, 0), (0, LANES - FOX_HEADS)))

    proj0, h0t = _norm_matmul(x, g["l0_pre_g"], w0p, "l0_in_proj")
    qm, km, vm, qnt, cnt = _mla_prep(proj0, g["l0_q_a_g"], g["l0_kv_a_g"], wqp, wkvp, cosT, s1T, s2T, "mla_prep")
    sb_vt = _transpose_tiles(proj0, L0_SBV, 4, LANES, 2, "sb_vt")
    sb_kt = _transpose_tiles(proj0, L0_SBK, 4, LANES, 2, "sb_kt")
    o_sb, og_sb, ogt_sb, tot_sb = _sb_fwd_t(proj0, sb_vt, S, 4, "sb_fwd")
    vmt = _transpose_tiles(vm, 0, 4, LANES, 4, "mla_vt")
    kmt = _transpose_tiles(km, 0, 4, 2 * LANES, 4, "mla_kt")
    o_ml, og_ml, ogt_ml, lse_ml = _softmax_fwd("mla", (qm, km, vmt, proj0), None, S, 4, "mla_fwd")
    y0, x1 = _out_proj(og_sb, og_ml, 0, 0, wo0, x, g["l0_post_g"], None, "l0_out_proj")

    proj1, h1t = _norm_matmul(x1, g["l1_pre_g"], w1p, "l1_in_proj")
    cfx = _fox_prep(proj1, bfp, "fox_prep")
    c16 = cfx[:, :FOX_HEADS].T
    c_col = jnp.broadcast_to(c16[:, :, None], (FOX_HEADS, S, LANES))
    vt1 = _transpose_tiles(proj1, L1_V, 8, LANES, 8, "fox_vt")
    kt1 = _transpose_tiles(proj1, L1_K, 8, LANES, 8, "fox_kt")
    o_fx, og_fx, ogt_fx, lse_fx = _softmax_fwd("fox", (proj1, proj1, vt1, proj1), c_col, S, 8, "fox_fwd")
    y1, dx2, lsum = _out_proj(og_fx, og_fx, 0, 1, wo1, x1, g["l1_post_g"], target, "l1_out_proj")

    dy1, do1, dgate1, d_post1 = _out_proj_bwd(dx2, y1, g["l1_post_g"], wo1, proj1, (L1_G, L1_G + 512), o_fx, o_fx, 0, 1, "l1_out_bwd")
    dwo1 = _matmul_t(ogt_fx, dy1, "l1_dw_out")
    dq1, dk1, dv1, dck, dcq = _softmax_bwd_t("fox", proj1, proj1, kt1, proj1, do1, 0, o_fx, lse_fx, c_col, S, 8,
                                             "fox_bwd")
    dc = jnp.pad((dcq.reshape(FOX_HEADS, S) - dck.reshape(FOX_HEADS, S)).T, ((0, 0), (0, LANES - FOX_HEADS)))
    df, d_bf = _fox_prep_bwd(dc, proj1, bfp, "fox_prep_bwd")
    pieces1 = [(L1_Q, dq1), (L1_K, dk1), (L1_V, dv1), (L1_G, dgate1), (L1_F, df)]
    dx1, d_pre1 = _in_proj_bwd(pieces1, w1p, x1, g["l1_pre_g"], dx2, "l1_in_bwd")
    dw1p = jnp.concatenate(_matmul_t_many(h1t, [dq1, dk1], "l1_dw_in_a")
                           + _matmul_t_many(h1t, [dv1, dgate1, df], "l1_dw_in_b"), axis=1)

    dy0, do0, dgate0, d_post0 = _out_proj_bwd(dx1, y0, g["l0_post_g"], wo0, proj0, (L0_SBG, L0_MLG), o_sb, o_ml, 0, 0,
                                              "l0_out_bwd")
    dwo0 = jnp.concatenate([_matmul_t(ogt_sb, dy0, "l0_dw_out_sb"), _matmul_t(ogt_ml, dy0, "l0_dw_out_mla")], axis=0)
    dsq, dsk, dsv = _sb_bwd_t(proj0, sb_kt, do0, tot_sb, S, 4, "sb_bwd")
    dqm, dkm, dvm = _softmax_bwd_t("mla", qm, km, kmt, vm, do0, 4, o_ml, lse_ml, None, S, 4, "mla_bwd")
    dprep, dqb, dkvb, d_qag, d_kvag = _mla_prep_bwd(dqm, dkm, dvm, proj0, g["l0_q_a_g"], g["l0_kv_a_g"], wqp, wkvp,
                                                    cosT, s1T, s2T, "mla_prep_bwd")
    dwqp = _matmul_t(qnt, dqb, "l0_dw_qb")
    dwkvp = _matmul_t(cnt, dkvb, "l0_dw_kvb")
    pieces0 = [(L0_PREP, dprep), (L0_SBG, dgate0), (L0_SBQ, dsq), (L0_SBK, dsk), (L0_SBV, dsv)]
    dx0, d_pre0 = _in_proj_bwd(pieces0, w0p, x, g["l0_pre_g"], dx1, "l0_in_bwd")
    dw0p = jnp.concatenate(_matmul_t_many(h0t, [dprep, dgate0], "l0_dw_in_a")
                           + _matmul_t_many(h0t, [dsq, dsk, dsv], "l0_dw_in_b"), axis=1)

    grads = {
        "l0_pre_g": d_pre0, "l0_post_g": d_post0, "l0_w_in": dw0p, "l0_q_a_g": d_qag, "l0_w_q_b": dwqp,
        "l0_kv_a_g": d_kvag, "l0_w_kv_b": dwkvp, "l0_w_out": dwo0, "l1_pre_g": d_pre1, "l1_post_g": d_post1,
        "l1_w_in": dw1p, "l1_b_f": d_bf[:, :FOX_HEADS], "l1_w_out": dwo1,
    }
    return lsum, dx0, grads


_ANY = pl.BlockSpec(memory_space=pl.ANY)


def _place():
    return lax.axis_index("x"), lax.axis_index("y"), lax.axis_index("c")


def _other_chips(x, y):
    return [(1 - x, y), (x, 1 - y), (1 - x, 1 - y)]


def _half(rows, c):
    return pl.ds(c * (rows // 2), rows // 2)


def _weight_gather(parts):
    n = len(parts)

    def body(*refs):
        p_refs, out_refs, send_sems, recv_sems = refs[:n], refs[n:2 * n], refs[2 * n], refs[2 * n + 1]
        x, y, c = _place()
        sibling = (x, y, 1 - c)
        chips = _other_chips(x, y)

        def blk(k, chip, cc):
            return out_refs[k].at[2 * chip[0] + chip[1], _half(p_refs[k].shape[0], cc)]

        def copy(s, src, dst, to):
            return pltpu.make_async_remote_copy(src_ref=src, dst_ref=dst, send_sem=send_sems.at[s],
                                                recv_sem=recv_sems.at[s], device_id=to, device_id_type=MESH)

        first = [copy(6 * k + j, p_refs[k].at[_half(p_refs[k].shape[0], c)], blk(k, (x, y), c), (*chip, c))
                 for j, chip in enumerate(chips) for k in range(n)]
        for cp in first:
            cp.start()
        passed = []
        for j, chip in enumerate(chips):
            for k in range(n):
                copy(6 * k + j, blk(k, chip, c), blk(k, chip, c), (x, y, c)).wait_recv()
                passed.append(copy(6 * k + 3 + j, blk(k, chip, c), blk(k, chip, c), sibling))
                passed[-1].start()
        for j, chip in enumerate(chips):
            for k in range(n):
                copy(6 * k + 3 + j, blk(k, chip, 1 - c), blk(k, chip, 1 - c), (x, y, c)).wait_recv()
        for cp in first + passed:
            cp.wait_send()

    return pl.pallas_call(
        body, name="weight_gather", in_specs=[_ANY] * n, out_specs=[_ANY] * n,
        out_shape=[jax.ShapeDtypeStruct((4,) + a.shape, a.dtype) for a in parts],
        scratch_shapes=[pltpu.SemaphoreType.DMA((6 * n,)), pltpu.SemaphoreType.DMA((6 * n,))],
    )(*parts)


def _grad_core_exchange(ps):
    n = len(ps)

    def body(*refs):
        p_refs, recv_refs, send_sems, recv_sems = refs[:n], refs[n:2 * n], refs[2 * n], refs[2 * n + 1]
        x, y, c = _place()
        give = [pltpu.make_async_remote_copy(src_ref=p_refs[k].at[j, _half(p_refs[k].shape[1], 1 - c)],
                                             dst_ref=recv_refs[k].at[j], send_sem=send_sems.at[4 * k + j],
                                             recv_sem=recv_sems.at[4 * k + j], device_id=(x, y, 1 - c),
                                             device_id_type=MESH) for k in range(n) for j in range(4)]
        for cp in give:
            cp.start()
        for cp in give:
            cp.wait()

    return pl.pallas_call(
        body, name="grad_core_exchange", in_specs=[_ANY] * n, out_specs=[_ANY] * n,
        out_shape=[jax.ShapeDtypeStruct((4, p.shape[1] // 2, p.shape[2]), p.dtype) for p in ps],
        scratch_shapes=[pltpu.SemaphoreType.DMA((4 * n,)), pltpu.SemaphoreType.DMA((4 * n,))],
    )(*ps)


def _grad_rows(rows):
    return _pick(rows, (1296, 512, rows))


def _grad_add_cores(p, theirs, c1, name):
    _, rh, cols = theirs.shape
    tr = _grad_rows(rh)

    def body(c_ref, a_ref, b_ref, o_ref):
        o_ref[...] = (a_ref[...] + b_ref[...]).astype(BF16)

    spec = pl.BlockSpec((None, tr, cols), lambda j, r, c: (j, r, 0))
    grid_spec = pltpu.PrefetchScalarGridSpec(
        num_scalar_prefetch=1, grid=(4, rh // tr),
        in_specs=[pl.BlockSpec((None, None, tr, cols), lambda j, r, c: (j, c[0], r, 0)), spec], out_specs=spec)
    return pl.pallas_call(
        body, name=name, grid_spec=grid_spec, out_shape=jax.ShapeDtypeStruct(theirs.shape, BF16),
        compiler_params=_cparams(("parallel", "parallel")),
    )(c1, p.reshape(4, 2, rh, cols), theirs)


def _grad_chip_exchange(qs):
    n = len(qs)

    def body(*refs):
        q_refs, out_refs, send_sems, recv_sems = refs[:n], refs[n:2 * n], refs[2 * n], refs[2 * n + 1]
        x, y, c = _place()
        me = 2 * x + y
        chips = _other_chips(x, y)
        sends = [pltpu.make_async_remote_copy(src_ref=q_refs[k].at[2 * chip[0] + chip[1]], dst_ref=out_refs[k].at[me],
                                              send_sem=send_sems.at[3 * k + j], recv_sem=recv_sems.at[3 * k + j],
                                              device_id=(*chip, c), device_id_type=MESH)
                 for j, chip in enumerate(chips) for k in range(n)]
        for cp in sends:
            cp.start()
        for j, chip in enumerate(chips):
            for k in range(n):
                slot = out_refs[k].at[2 * chip[0] + chip[1]]
                pltpu.make_async_remote_copy(src_ref=slot, dst_ref=slot, send_sem=send_sems.at[3 * k + j],
                                             recv_sem=recv_sems.at[3 * k + j], device_id=(x, y, c),
                                             device_id_type=MESH).wait_recv()
        for cp in sends:
            cp.wait_send()

    return pl.pallas_call(
        body, name="grad_chip_exchange", in_specs=[_ANY] * n, out_specs=[_ANY] * n,
        out_shape=[jax.ShapeDtypeStruct(q.shape, q.dtype) for q in qs],
        scratch_shapes=[pltpu.SemaphoreType.DMA((3 * n,)), pltpu.SemaphoreType.DMA((3 * n,))],
    )(*qs)


def _grad_add_chips(q, slots, me1, name):
    _, rh, cols = q.shape
    tr = _grad_rows(rh)

    def body(me_ref, own_ref, s0, s1, s2, s3, o_ref):
        me = me_ref[0]
        t = [jnp.where(me == j, own_ref[...], s[...]).astype(F32) for j, s in enumerate((s0, s1, s2, s3))]
        o_ref[...] = ((t[0] + t[1]) + t[2]) + t[3]

    def slot_spec(j):
        return pl.BlockSpec((None, tr, cols), lambda r, me: (jnp.where(me[0] == j, (j + 1) % 4, j), r, 0))

    grid_spec = pltpu.PrefetchScalarGridSpec(
        num_scalar_prefetch=1, grid=(rh // tr,),
        in_specs=[pl.BlockSpec((None, tr, cols), lambda r, me: (me[0], r, 0))] + [slot_spec(j) for j in range(4)],
        out_specs=pl.BlockSpec((tr, cols), lambda r, me: (r, 0)))
    return pl.pallas_call(
        body, name=name, grid_spec=grid_spec, out_shape=jax.ShapeDtypeStruct(q.shape[1:], F32),
        compiler_params=_cparams(("parallel",)),
    )(me1, q, slots, slots, slots, slots)


def _grad_core_gather(ts):
    n = len(ts)

    def body(*refs):
        t_refs, out_refs, send_sems, recv_sems = refs[:n], refs[n:2 * n], refs[2 * n], refs[2 * n + 1]
        x, y, c = _place()
        give = [pltpu.make_async_remote_copy(src_ref=t_refs[k], dst_ref=out_refs[k], send_sem=send_sems.at[k],
                                             recv_sem=recv_sems.at[k], device_id=(x, y, 1 - c), device_id_type=MESH)
                for k in range(n)]
        for cp in give:
            cp.start()
        for cp in give:
            cp.wait()

    return pl.pallas_call(
        body, name="grad_core_gather", in_specs=[_ANY] * n, out_specs=[_ANY] * n,
        out_shape=[jax.ShapeDtypeStruct(t.shape, t.dtype) for t in ts],
        scratch_shapes=[pltpu.SemaphoreType.DMA((n,)), pltpu.SemaphoreType.DMA((n,))],
    )(*ts)


def _small_allreduce(sp):
    def body(sp_ref, out_ref, gath_ref, send_sems, recv_sems):
        x, y, c = _place()
        me = 4 * x + 2 * y + c
        gath_ref[me] = sp_ref[...]
        peers = []
        for k in range(1, 8):
            px = 1 - x if k & 4 else x
            py = 1 - y if k & 2 else y
            pc = 1 - c if k & 1 else c
            peers.append((px, py, pc))
        sends = [pltpu.make_async_remote_copy(src_ref=sp_ref, dst_ref=gath_ref.at[me], send_sem=send_sems.at[k],
                                              recv_sem=recv_sems.at[k], device_id=peer, device_id_type=MESH)
                 for k, peer in enumerate(peers)]
        for cp in sends:
            cp.start()
        for k, (px, py, pc) in enumerate(peers):
            slot = gath_ref.at[4 * px + 2 * py + pc]
            pltpu.make_async_remote_copy(src_ref=slot, dst_ref=slot, send_sem=send_sems.at[k], recv_sem=recv_sems.at[k],
                                         device_id=(x, y, c), device_id_type=MESH).wait_recv()
        for cp in sends:
            cp.wait_send()
        tot = gath_ref[0]
        for d in range(1, 8):
            tot = tot + gath_ref[d]
        out_ref[...] = tot

    vm = pl.BlockSpec(memory_space=pltpu.VMEM)
    return pl.pallas_call(
        body, name="small_allreduce", in_specs=[vm], out_specs=vm, out_shape=jax.ShapeDtypeStruct(sp.shape, sp.dtype),
        scratch_shapes=[pltpu.VMEM((8,) + sp.shape, sp.dtype), pltpu.SemaphoreType.DMA((7,)), pltpu.SemaphoreType.DMA((7,))],
    )(sp)


def _adamw_update(w, gv, m, v):
    mn = ADAM_B1 * m + (1.0 - ADAM_B1) * gv
    vn = ADAM_B2 * v + (1.0 - ADAM_B2) * (gv * gv)
    m_hat = mn / (1.0 - ADAM_B1 ** ADAM_STEP)
    v_hat = vn / (1.0 - ADAM_B2 ** ADAM_STEP)
    return -ADAM_LR * (m_hat / (jnp.sqrt(v_hat) + ADAM_EPS) + ADAM_WD * w), mn, vn


def _adamw(w, g, m, v, name):
    rows, cols = w.shape
    tr = _pick(rows, (256, rows))

    def body(w_ref, g_ref, m_ref, v_ref, d_ref, mo_ref, vo_ref):
        d_ref[...], mo_ref[...], vo_ref[...] = _adamw_update(w_ref[...], g_ref[...], m_ref[...], v_ref[...])

    spec = pl.BlockSpec((tr, cols), lambda r: (r, 0))
    shp = jax.ShapeDtypeStruct(w.shape, F32)
    return pl.pallas_call(
        body, name=name, grid=(rows // tr,), in_specs=[spec] * 4, out_specs=[spec] * 3, out_shape=[shp] * 3,
        compiler_params=_cparams(("parallel",)),
    )(w, g, m, v)


MAT_NAMES = ("l0_w_in", "l0_w_q_b", "l0_w_kv_b", "l0_w_out", "l1_w_in", "l1_w_out")
VEC_NAMES = ("l0_pre_g", "l0_post_g", "l0_q_a_g", "l0_kv_a_g", "l1_pre_g", "l1_post_g", "l1_b_f")
WEIGHT_NAMES = ("l0_pre_g", "l0_post_g", "l0_w_in", "l0_q_a_g", "l0_w_q_b", "l0_kv_a_g", "l0_w_kv_b", "l0_w_out",
                "l1_pre_g", "l1_post_g", "l1_w_in", "l1_b_f", "l1_w_out")
MAT_SHARD = {"l0_w_in": (1024, 808), "l0_w_q_b": (384, 192), "l0_w_kv_b": (256, 256), "l0_w_out": (256, 1024),
             "l1_w_in": (1024, 1028), "l1_w_out": (256, 1024)}
ROW_SHARDED = ("l0_w_out", "l1_w_out")
WHOLE_MATS = ("l0_w_in", "l1_w_in")
PACKED_MATS = ("l0_w_q_b", "l0_w_kv_b", "l0_w_out", "l1_w_out")
VEC_LEN = {"l0_pre_g": 1024, "l0_post_g": 1024, "l0_q_a_g": 384, "l0_kv_a_g": 256, "l1_pre_g": 1024,
           "l1_post_g": 1024, "l1_b_f": 16}


def _mat_rows(n):
    r, c = MAT_SHARD[n]
    return r * c // LANES


def _pack_shards(shards):
    return jnp.concatenate([shards[n].reshape(shards[n].shape[:-2] + (_mat_rows(n), LANES)) for n in PACKED_MATS],
                           axis=-2)


def _unpack_shards(pack):
    out, at = {}, 0
    for n in PACKED_MATS:
        out[n] = pack[..., at:at + _mat_rows(n), :].reshape(pack.shape[:-2] + MAT_SHARD[n])
        at += _mat_rows(n)
    return out


def _join_shards(n, s):
    if n in ROW_SHARDED:
        return s.reshape(4 * s.shape[1], s.shape[2])
    return s.transpose(1, 0, 2).reshape(s.shape[1], 4 * s.shape[2])


def _cut_shards(n, w):
    r, c = MAT_SHARD[n]
    if n in ROW_SHARDED:
        return w.reshape(4, r, c)
    return w.reshape(r, 4, c).transpose(1, 0, 2)


def _pack_vecs(vecs):
    parts = []
    for n in VEC_NAMES:
        v = vecs[n].reshape(-1)
        parts.append(jnp.pad(v, (0, VEC_ROWS * LANES - v.shape[0])).reshape(VEC_ROWS, LANES))
    return jnp.concatenate(parts, axis=0)


def _unpack_vecs(pack):
    return {n: pack[k * VEC_ROWS:(k + 1) * VEC_ROWS].reshape(-1)[:VEC_LEN[n]] for k, n in enumerate(VEC_NAMES)}


def kernel(x, positions, l0_pre_g, l0_post_g, l0_w_in, l0_q_a_g, l0_w_q_b, l0_kv_a_g, l0_w_kv_b, l0_w_out, l1_pre_g, l1_post_g, l1_w_in, l1_b_f, l1_w_out, loss_target, m_l0_pre_g, m_l0_post_g, m_l0_w_in, m_l0_q_a_g, m_l0_w_q_b, m_l0_kv_a_g, m_l0_w_kv_b, m_l0_w_out, m_l1_pre_g, m_l1_post_g, m_l1_w_in, m_l1_b_f, m_l1_w_out, v_l0_pre_g, v_l0_post_g, v_l0_w_in, v_l0_q_a_g, v_l0_w_q_b, v_l0_kv_a_g, v_l0_w_kv_b, v_l0_w_out, v_l1_pre_g, v_l1_post_g, v_l1_w_in, v_l1_b_f, v_l1_w_out):
    w = dict(l0_pre_g=l0_pre_g, l0_post_g=l0_post_g, l0_w_in=l0_w_in, l0_q_a_g=l0_q_a_g, l0_w_q_b=l0_w_q_b,
             l0_kv_a_g=l0_kv_a_g, l0_w_kv_b=l0_w_kv_b, l0_w_out=l0_w_out, l1_pre_g=l1_pre_g, l1_post_g=l1_post_g,
             l1_w_in=l1_w_in, l1_b_f=l1_b_f, l1_w_out=l1_w_out)
    m = dict(l0_pre_g=m_l0_pre_g, l0_post_g=m_l0_post_g, l0_w_in=m_l0_w_in, l0_q_a_g=m_l0_q_a_g, l0_w_q_b=m_l0_w_q_b,
             l0_kv_a_g=m_l0_kv_a_g, l0_w_kv_b=m_l0_w_kv_b, l0_w_out=m_l0_w_out, l1_pre_g=m_l1_pre_g,
             l1_post_g=m_l1_post_g, l1_w_in=m_l1_w_in, l1_b_f=m_l1_b_f, l1_w_out=m_l1_w_out)
    v = dict(l0_pre_g=v_l0_pre_g, l0_post_g=v_l0_post_g, l0_w_in=v_l0_w_in, l0_q_a_g=v_l0_q_a_g, l0_w_q_b=v_l0_w_q_b,
             l0_kv_a_g=v_l0_kv_a_g, l0_w_kv_b=v_l0_w_kv_b, l0_w_out=v_l0_w_out, l1_pre_g=v_l1_pre_g,
             l1_post_g=v_l1_post_g, l1_w_in=v_l1_w_in, l1_b_f=v_l1_b_f, l1_w_out=v_l1_w_out)

    cx, cy, cc = _place()
    me1 = jnp.reshape(2 * cx + cy, (1,)).astype(jnp.int32)
    c1 = jnp.reshape(cc, (1,)).astype(jnp.int32)
    w_bf = {n: w[n].astype(BF16) for n in MAT_NAMES}
    mine = [_pack_shards(w_bf)] + [w_bf[n] for n in WHOLE_MATS]
    got = [lax.dynamic_update_slice(g, a[None], (2 * cx + cy, 0, 0)) for g, a in zip(_weight_gather(mine), mine)]
    gathered = dict(_unpack_shards(got[0]), **dict(zip(WHOLE_MATS, got[1:])))
    full = {n: _join_shards(n, gathered[n]) for n in MAT_NAMES}
    gains = {n: w[n].reshape(1, -1) for n in VEC_NAMES}

    lsum, dx0, grads = _local_step(
        x[0], positions[0], loss_target[0], gains, _pad_w0(full["l0_w_in"]), _pad_wq(full["l0_w_q_b"]),
        _pad_wkv(full["l0_w_kv_b"]), full["l0_w_out"], _pad_w1(full["l1_w_in"]), full["l1_w_out"])

    gfull = {"l0_w_in": _unpad_w0(grads["l0_w_in"]), "l0_w_q_b": _unpad_wq(grads["l0_w_q_b"]),
             "l0_w_kv_b": _unpad_wkv(grads["l0_w_kv_b"]), "l0_w_out": grads["l0_w_out"],
             "l1_w_in": grads["l1_w_in"][:, :ODD_IN_WIDTH], "l1_w_out": grads["l1_w_out"]}
    cut = {n: _cut_shards(n, gfull[n]) for n in MAT_NAMES}
    tags = ("packed",) + WHOLE_MATS
    g_parts = [_pack_shards(cut)] + [cut[n] for n in WHOLE_MATS]
    q_cores = [_grad_add_cores(p, t, c1, "grad_add_cores_" + tag)
               for p, t, tag in zip(g_parts, _grad_core_exchange(g_parts), tags)]
    g_mine = [_grad_add_chips(q, s, me1, "grad_add_chips_" + tag)
              for q, s, tag in zip(q_cores, _grad_chip_exchange(q_cores), tags)]
    g_theirs = _grad_core_gather(g_mine)

    small = _small_allreduce(jnp.concatenate([_pack_vecs({n: grads[n] for n in VEC_NAMES}),
                                              lsum.reshape(D_MODEL // LANES, LANES)], axis=0))
    g_small = small[:SMALL_ROWS]
    loss = 0.5 * jnp.sum(small[SMALL_ROWS:]) / float(D_MODEL)

    whole = [jnp.concatenate([lax.select(cc == 0, a, b), lax.select(cc == 0, b, a)], axis=0)
             for a, b in zip(g_mine, g_theirs)]
    g_mats = dict(_unpack_shards(whole[0]), **dict(zip(WHOLE_MATS, whole[1:])))
    d_mats, m_mats, v_mats = {}, {}, {}
    for n in MAT_NAMES:
        d_mats[n], m_mats[n], v_mats[n] = _adamw(w[n], g_mats[n], m[n], v[n], "adamw_" + n)
    d_small, m_small, v_small = _adamw(_pack_vecs(w), g_small, _pack_vecs(m), _pack_vecs(v), "adamw_vecs")

    def leaves(mats, vec_pack):
        out = dict(mats)
        out.update(_unpack_vecs(vec_pack))
        return [out[n] for n in WEIGHT_NAMES]

    return (loss, dx0[None], *leaves(g_mats, g_small), *leaves(d_mats, d_small), *leaves(m_mats, m_small),
            *leaves(v_mats, v_small))
```

```python
import jax
import jax.numpy as jnp
from jax import lax
from jax.experimental import pallas as pl
from jax.experimental.pallas import tpu as pltpu

F32 = jnp.float32
BF16 = jnp.bfloat16
MESH = pl.DeviceIdType.MESH

D_MODEL = 1024
RMS_EPS = 1e-6
ROPE_THETA = 10000.0
SB_WIDTH = 512
MLA_Q_LORA = 384
MLA_KV_LORA = 256
MLA_ROPE_DIM = 32
MLA_WIDTH = 512
FOX_WIDTH = 1024
FOX_HEADS = 16
EVEN_IN_WIDTH = 3232
ODD_IN_WIDTH = 4112

ADAM_LR = 0.001
ADAM_B1 = 0.9
ADAM_B2 = 0.999
ADAM_EPS = 1e-08
ADAM_WD = 0.01
ADAM_STEP = 10

LANES = 128
VMEM_LIMIT = 56 * 1024 * 1024

L0_PREP = 0
L0_PREP_W = 768
L0_SBG = 768
L0_MLG = 1280
L0_SBQ = 1792
L0_SBK = 2304
L0_SBV = 2816
L0_WIDTH = 3328
L1_Q = 0
L1_K = 1024
L1_V = 2048
L1_G = 3072
L1_F = 4096
L1_WIDTH = 4224

ATT_T = 256
ATT_GROUP = 4
ATT_QSUB = 2
NEG = -1e30

VEC_ROWS = 8
SMALL_ROWS = 7 * VEC_ROWS


def _cparams(sem, **kw):
    return pltpu.CompilerParams(dimension_semantics=sem, vmem_limit_bytes=VMEM_LIMIT, **kw)


def _dot(a, b):
    return lax.dot_general(a, b, (((1,), (0,)), ((), ())), preferred_element_type=F32)


def _dot_nt(a, b):
    return lax.dot_general(a, b, (((1,), (1,)), ((), ())), preferred_element_type=F32)


def _sigmoid(x):
    return 1.0 / (1.0 + jnp.exp(-x))


def _rstd(x):
    return lax.rsqrt(jnp.mean(x * x, axis=-1, keepdims=True) + RMS_EPS)


def _norm_bwd(x, g, dy):
    r = _rstd(x)
    xn = x * r
    dxn = dy * g
    dx = r * (dxn - xn * jnp.mean(dxn * xn, axis=-1, keepdims=True))
    return dx, dy * xn


def _split3(x):
    hi = x.astype(BF16)
    r1 = x - hi.astype(F32)
    mid = r1.astype(BF16)
    lo = (r1 - mid.astype(F32)).astype(BF16)
    return hi, mid, lo


def _wide_tile(n, cap=1792):
    return max(t for t in range(LANES, min(n, cap) + 1, LANES) if n % t == 0)


def _pick(n, cands):
    for c in cands:
        if n % c == 0:
            return c
    raise ValueError(n)


def _norm_matmul(x, g, w, name):
    S, K = x.shape
    N = w.shape[1]
    tm = _pick(S, (512, 256))
    tn = _wide_tile(N)

    def body(x_ref, g_ref, w_ref, o_ref, ht_ref, h_ref):
        @pl.when(pl.program_id(1) == 0)
        def _():
            xv = x_ref[...]
            h = (xv * _rstd(xv)) * g_ref[...]
            h_ref[...] = h.astype(BF16)
            ht_ref[...] = h.T.astype(BF16)
        o_ref[...] = _dot(h_ref[...], w_ref[...])

    return pl.pallas_call(
        body, name=name, grid=(S // tm, N // tn),
        in_specs=[pl.BlockSpec((tm, K), lambda i, j: (i, 0)),
                  pl.BlockSpec((1, K), lambda i, j: (0, 0)),
                  pl.BlockSpec((K, tn), lambda i, j: (0, j))],
        out_specs=[pl.BlockSpec((tm, tn), lambda i, j: (i, j)),
                   pl.BlockSpec((K, tm), lambda i, j: (0, i))],
        out_shape=[jax.ShapeDtypeStruct((S, N), F32), jax.ShapeDtypeStruct((K, S), BF16)],
        scratch_shapes=[pltpu.VMEM((tm, K), BF16)],
        compiler_params=_cparams(("parallel", "arbitrary")),
    )(x, g, w)


def _matmul_t(at, b, name):
    M, S = at.shape
    N = b.shape[1]
    tn = _wide_tile(N)
    ts = _pick(S, (512, 256))

    def body(a_ref, b_ref, o_ref):
        @pl.when(pl.program_id(1) == 0)
        def _():
            o_ref[...] = jnp.zeros_like(o_ref)
        o_ref[...] += _dot(a_ref[...], b_ref[...].astype(BF16))

    return pl.pallas_call(
        body, name=name, grid=(N // tn, S // ts),
        in_specs=[pl.BlockSpec((M, ts), lambda j, k: (0, k)),
                  pl.BlockSpec((ts, tn), lambda j, k: (k, j))],
        out_specs=pl.BlockSpec((M, tn), lambda j, k: (0, j)),
        out_shape=jax.ShapeDtypeStruct((M, N), F32),
        compiler_params=_cparams(("parallel", "arbitrary")),
    )(at, b)


def _matmul_t_many(at, bs, name):
    M, S = at.shape
    ts = _pick(S, (512, 256))
    n = len(bs)

    def body(*refs):
        a_ref, b_refs, o_refs = refs[0], refs[1:1 + n], refs[1 + n:]

        @pl.when(pl.program_id(0) == 0)
        def _():
            for o_ref in o_refs:
                o_ref[...] = jnp.zeros_like(o_ref)

        a = a_ref[...]
        for b_ref, o_ref in zip(b_refs, o_refs):
            o_ref[...] += _dot(a, b_ref[...].astype(BF16))

    return pl.pallas_call(
        body, name=name, grid=(S // ts,),
        in_specs=[pl.BlockSpec((M, ts), lambda k: (0, k))] + [pl.BlockSpec((ts, b.shape[1]), lambda k: (k, 0)) for b in bs],
        out_specs=[pl.BlockSpec((M, b.shape[1]), lambda k: (0, 0)) for b in bs],
        out_shape=[jax.ShapeDtypeStruct((M, b.shape[1]), F32) for b in bs],
        compiler_params=_cparams(("arbitrary",)),
    )(at, *bs)


def _in_proj_bwd(pieces, w, x, g, dx_up, name):
    S, K = x.shape
    N = w.shape[1]
    tm = _pick(S, (256,))
    offs = [off for off, _ in pieces]
    arrs = [a for _, a in pieces]

    def body(*refs):
        d_refs = refs[:len(arrs)]
        w_ref, x_ref, g_ref, u_ref, dx_ref, dg_ref = refs[len(arrs):]

        @pl.when(pl.program_id(0) == 0)
        def _():
            dg_ref[...] = jnp.zeros_like(dg_ref)

        acc = None
        for off, d_ref in zip(offs, d_refs):
            part = _dot_nt(d_ref[...].astype(BF16), w_ref[:, off:off + d_ref.shape[1]])
            acc = part if acc is None else acc + part
        dx, dgrow = _norm_bwd(x_ref[...], g_ref[...], acc)
        dx_ref[...] = u_ref[...] + dx
        dg_ref[...] += jnp.sum(dgrow, axis=0, keepdims=True)

    row = lambda i: (i, 0)
    fixed = lambda i: (0, 0)
    return pl.pallas_call(
        body, name=name, grid=(S // tm,),
        in_specs=[pl.BlockSpec((tm, a.shape[1]), row) for a in arrs] + [
            pl.BlockSpec((K, N), fixed), pl.BlockSpec((tm, K), row), pl.BlockSpec((1, K), fixed),
            pl.BlockSpec((tm, K), row)],
        out_specs=[pl.BlockSpec((tm, K), row), pl.BlockSpec((1, K), fixed)],
        out_shape=[jax.ShapeDtypeStruct((S, K), F32), jax.ShapeDtypeStruct((1, K), F32)],
        compiler_params=_cparams(("arbitrary",)),
    )(*arrs, w, x, g, dx_up)


def _out_proj(og_a, og_b, blk_a, blk_b, w, x, g, target, name):
    S = x.shape[0]
    D = x.shape[1]
    tm = _pick(S, (512, 256))
    with_loss = target is not None

    def body(*refs):
        if with_loss:
            a_ref, b_ref, wa_ref, wb_ref, x_ref, g_ref, t_ref, y_ref, o_ref, l_ref = refs
        else:
            a_ref, b_ref, wa_ref, wb_ref, x_ref, g_ref, y_ref, o_ref = refs
        y = _dot(a_ref[...], wa_ref[...]) + _dot(b_ref[...], wb_ref[...])
        y_ref[...] = y
        xn = x_ref[...] + (y * _rstd(y)) * g_ref[...]
        if with_loss:
            @pl.when(pl.program_id(0) == 0)
            def _():
                l_ref[...] = jnp.zeros_like(l_ref)
            d = xn - t_ref[...]
            o_ref[...] = d / float(D)
            l_ref[...] += jnp.sum(d * d, axis=0, keepdims=True)
        else:
            o_ref[...] = xn

    row = lambda i: (i, 0)
    in_specs = [pl.BlockSpec((tm, 512), lambda i: (i, blk_a)),
                pl.BlockSpec((tm, 512), lambda i: (i, blk_b)),
                pl.BlockSpec((512, D), lambda i: (0, 0)),
                pl.BlockSpec((512, D), lambda i: (1, 0)),
                pl.BlockSpec((tm, D), row),
                pl.BlockSpec((1, D), lambda i: (0, 0))]
    out_specs = [pl.BlockSpec((tm, D), row), pl.BlockSpec((tm, D), row)]
    out_shape = [jax.ShapeDtypeStruct((S, D), F32), jax.ShapeDtypeStruct((S, D), F32)]
    args = [og_a, og_b, w, w, x, g]
    if with_loss:
        in_specs.append(pl.BlockSpec((tm, D), row))
        out_specs.append(pl.BlockSpec((1, D), lambda i: (0, 0)))
        out_shape.append(jax.ShapeDtypeStruct((1, D), F32))
        args.append(target)
    return pl.pallas_call(
        body, name=name, grid=(S // tm,), in_specs=in_specs, out_specs=out_specs, out_shape=out_shape,
        compiler_params=_cparams(("arbitrary",)),
    )(*args)


def _out_proj_bwd(dx_up, y, g, w, proj, gate_offs, o_a, o_b, oblk_a, oblk_b, name):
    S, D = y.shape
    tm = _pick(S, (256,))
    gblk = [off // 256 + c for off in gate_offs for c in range(2)]

    def body(u_ref, y_ref, g_ref, w_ref, g0, g1, g2, g3, oa_ref, ob_ref, dy_ref, do_ref, dgate_ref, dg_ref):
        @pl.when(pl.program_id(0) == 0)
        def _():
            dg_ref[...] = jnp.zeros_like(dg_ref)
        dy, dgrow = _norm_bwd(y_ref[...], g_ref[...], u_ref[...])
        dg_ref[...] += jnp.sum(dgrow, axis=0, keepdims=True)
        dyb = dy.astype(BF16)
        dy_ref[...] = dyb
        dog = _dot_nt(dyb, w_ref[...])
        gates = (g0, g1, g2, g3)
        for c in range(4):
            gt = gates[c][...]
            sg = _sigmoid(gt)
            o_ref = oa_ref if c < 2 else ob_ref
            ov = o_ref[:, (c % 2) * 256:(c % 2 + 1) * 256]
            dc = dog[:, c * 256:(c + 1) * 256]
            do_ref[:, c * 256:(c + 1) * 256] = dc * (gt * sg)
            dgate_ref[:, c * 256:(c + 1) * 256] = dc * ov * (sg * (1.0 + gt * (1.0 - sg)))

    row = lambda i: (i, 0)
    gspec = lambda c: pl.BlockSpec((tm, 256), lambda i: (i, gblk[c]))
    return pl.pallas_call(
        body, name=name, grid=(S // tm,),
        in_specs=[pl.BlockSpec((tm, D), row), pl.BlockSpec((tm, D), row), pl.BlockSpec((1, D), lambda i: (0, 0)),
                  pl.BlockSpec((D, D), lambda i: (0, 0)),
                  gspec(0), gspec(1), gspec(2), gspec(3),
                  pl.BlockSpec((tm, 512), lambda i: (i, oblk_a)),
                  pl.BlockSpec((tm, 512), lambda i: (i, oblk_b))],
        out_specs=[pl.BlockSpec((tm, D), row), pl.BlockSpec((tm, D), row), pl.BlockSpec((tm, D), row),
                   pl.BlockSpec((1, D), lambda i: (0, 0))],
        out_shape=[jax.ShapeDtypeStruct((S, D), BF16), jax.ShapeDtypeStruct((S, D), F32),
                   jax.ShapeDtypeStruct((S, D), F32), jax.ShapeDtypeStruct((1, D), F32)],
        compiler_params=_cparams(("arbitrary",)),
    )(dx_up, y, g, w, proj, proj, proj, proj, o_a, o_b)


def _rope_tables(pos, invf, name):
    S = pos.shape[0]
    tm = _pick(S, (512, 256))

    def body(p_ref, f_ref, c_ref, s1_ref, s2_ref):
        lane = lax.broadcasted_iota(jnp.int32, (1, LANES), 1)
        ang = p_ref[...].astype(F32) * f_ref[...]
        c, s = jnp.cos(ang), jnp.sin(ang)
        c_ref[...] = jnp.where((lane >= 64) & (lane < 96), c, 1.0)
        s1_ref[...] = jnp.where((lane >= 64) & (lane < 80), -s, 0.0)
        s2_ref[...] = jnp.where((lane >= 80) & (lane < 96), s, 0.0)

    spec = pl.BlockSpec((tm, LANES), lambda i: (i, 0))
    return pl.pallas_call(
        body, name=name, grid=(S // tm,),
        in_specs=[pl.BlockSpec((tm, 1), lambda i: (i, 0)), pl.BlockSpec((1, LANES), lambda i: (0, 0))],
        out_specs=[spec, spec, spec],
        out_shape=[jax.ShapeDtypeStruct((S, LANES), F32)] * 3,
        compiler_params=_cparams(("parallel",)),
    )(pos, invf)


def _rope(x, c, s1, s2):
    return x * c + pltpu.roll(x, LANES - 16, 1) * s1 + pltpu.roll(x, 16, 1) * s2


def _rope_t(d, c, s1, s2):
    return d * c + pltpu.roll(d * s1, 16, 1) + pltpu.roll(d * s2, LANES - 16, 1)


def _mla_prep(proj, gq, gkv, wq, wkv, cosT, s1T, s2T, name):
    S = proj.shape[0]
    tm = _pick(S, (256,))

    def body(p_ref, gq_ref, gkv_ref, wq_ref, wkv_ref, c_ref, s1_ref, s2_ref, q_ref, k_ref, v_ref, qn_ref, cn_ref):
        qa = p_ref[:, 0:384]
        ckv = p_ref[:, 384:640]
        kr = p_ref[:, 640:768]
        qn32 = (qa * _rstd(qa)) * gq_ref[...]
        cn32 = (ckv * _rstd(ckv)) * gkv_ref[...]
        qn = qn32.astype(BF16)
        cn = cn32.astype(BF16)
        qn_ref[...] = qn32.T.astype(BF16)
        cn_ref[...] = cn32.T.astype(BF16)
        qb = _dot(qn, wq_ref[...])
        kvb = _dot(cn, wkv_ref[...])
        c, s1, s2 = c_ref[...], s1_ref[...], s2_ref[...]
        krr = _rope(kr, c, s1, s2)
        for h in range(8):
            sl = slice(h * LANES, (h + 1) * LANES)
            q_ref[:, sl] = _rope(qb[:, sl], c, s1, s2)
            k_ref[:, sl] = kvb[:, sl] + krr
        v_ref[...] = kvb[:, 1024:1536]

    row = lambda i: (i, 0)
    fixed = lambda i: (0, 0)
    tspec = pl.BlockSpec((tm, LANES), row)
    return pl.pallas_call(
        body, name=name, grid=(S // tm,),
        in_specs=[pl.BlockSpec((tm, L0_PREP_W), lambda i: (i, L0_PREP // L0_PREP_W)),
                  pl.BlockSpec((1, 384), fixed), pl.BlockSpec((1, 256), fixed),
                  pl.BlockSpec((384, 1024), fixed), pl.BlockSpec((256, 1536), fixed), tspec, tspec, tspec],
        out_specs=[pl.BlockSpec((tm, 1024), row), pl.BlockSpec((tm, 1024), row), pl.BlockSpec((tm, 512), row),
                   pl.BlockSpec((384, tm), lambda i: (0, i)), pl.BlockSpec((256, tm), lambda i: (0, i))],
        out_shape=[jax.ShapeDtypeStruct((S, 1024), F32), jax.ShapeDtypeStruct((S, 1024), F32),
                   jax.ShapeDtypeStruct((S, 512), F32), jax.ShapeDtypeStruct((384, S), BF16),
                   jax.ShapeDtypeStruct((256, S), BF16)],
        compiler_params=_cparams(("parallel",)),
    )(proj, gq, gkv, wq, wkv, cosT, s1T, s2T)


def _mla_prep_bwd(dq, dk, dv, proj, gq, gkv, wq, wkv, cosT, s1T, s2T, name):
    S = proj.shape[0]
    tm = _pick(S, (256,))

    def body(dq_ref, dk_ref, dv_ref, p_ref, gq_ref, gkv_ref, wq_ref, wkv_ref, c_ref, s1_ref, s2_ref,
             dp_ref, dqb_ref, dkvb_ref, dgq_ref, dgkv_ref):
        @pl.when(pl.program_id(0) == 0)
        def _():
            dgq_ref[...] = jnp.zeros_like(dgq_ref)
            dgkv_ref[...] = jnp.zeros_like(dgkv_ref)
        c, s1, s2 = c_ref[...], s1_ref[...], s2_ref[...]
        lane = lax.broadcasted_iota(jnp.int32, (1, LANES), 1)
        dkr = jnp.zeros((tm, LANES), F32)
        for h in range(8):
            sl = slice(h * LANES, (h + 1) * LANES)
            dqb_ref[:, sl] = _rope_t(dq_ref[:, sl], c, s1, s2).astype(BF16)
            dkh = dk_ref[:, sl]
            dkvb_ref[:, sl] = dkh.astype(BF16)
            dkr = dkr + dkh
        dkvb_ref[:, 1024:1536] = dv_ref[...].astype(BF16)
        dkr = jnp.where((lane >= 64) & (lane < 96), _rope_t(dkr, c, s1, s2), 0.0)
        dqn = _dot_nt(dqb_ref[...], wq_ref[...])
        dcn = _dot_nt(dkvb_ref[...], wkv_ref[...])
        dqa, gq_row = _norm_bwd(p_ref[:, 0:384], gq_ref[...], dqn)
        dckv, gkv_row = _norm_bwd(p_ref[:, 384:640], gkv_ref[...], dcn)
        dp_ref[:, 0:384] = dqa
        dp_ref[:, 384:640] = dckv
        dp_ref[:, 640:768] = dkr
        dgq_ref[...] += jnp.sum(gq_row, axis=0, keepdims=True)
        dgkv_ref[...] += jnp.sum(gkv_row, axis=0, keepdims=True)

    row = lambda i: (i, 0)
    fixed = lambda i: (0, 0)
    tspec = pl.BlockSpec((tm, LANES), row)
    return pl.pallas_call(
        body, name=name, grid=(S // tm,),
        in_specs=[pl.BlockSpec((tm, 1024), row), pl.BlockSpec((tm, 1024), row), pl.BlockSpec((tm, 512), row),
                  pl.BlockSpec((tm, L0_PREP_W), lambda i: (i, L0_PREP // L0_PREP_W)),
                  pl.BlockSpec((1, 384), fixed), pl.BlockSpec((1, 256), fixed),
                  pl.BlockSpec((384, 1024), fixed), pl.BlockSpec((256, 1536), fixed), tspec, tspec, tspec],
        out_specs=[pl.BlockSpec((tm, L0_PREP_W), row), pl.BlockSpec((tm, 1024), row), pl.BlockSpec((tm, 1536), row),
                   pl.BlockSpec((1, 384), fixed), pl.BlockSpec((1, 256), fixed)],
        out_shape=[jax.ShapeDtypeStruct((S, L0_PREP_W), F32), jax.ShapeDtypeStruct((S, 1024), BF16),
                   jax.ShapeDtypeStruct((S, 1536), BF16), jax.ShapeDtypeStruct((1, 384), F32),
                   jax.ShapeDtypeStruct((1, 256), F32)],
        compiler_params=_cparams(("arbitrary",)),
    )(dq, dk, dv, proj, gq, gkv, wq, wkv, cosT, s1T, s2T)


def _fox_prep(proj, bf, name):
    S = proj.shape[0]
    tm = _pick(S, (256,))

    def body(f_ref, b_ref, c_ref, carry_ref):
        @pl.when(pl.program_id(0) == 0)
        def _():
            carry_ref[...] = jnp.zeros_like(carry_ref)
        u = f_ref[...] + b_ref[...]
        lf = jnp.minimum(u, 0.0) - jnp.log(1.0 + jnp.exp(-jnp.abs(u)))
        r = lax.broadcasted_iota(jnp.int32, (tm, tm), 0)
        cidx = lax.broadcasted_iota(jnp.int32, (tm, tm), 1)
        tri = (cidx <= r).astype(BF16)
        hi, mid, lo = _split3(lf)
        c = carry_ref[...] + (_dot(tri, hi) + _dot(tri, mid) + _dot(tri, lo))
        c_ref[...] = c
        carry_ref[...] = c[tm - 1:tm, :]

    return pl.pallas_call(
        body, name=name, grid=(S // tm,),
        in_specs=[pl.BlockSpec((tm, LANES), lambda i: (i, L1_F // LANES)), pl.BlockSpec((1, LANES), lambda i: (0, 0))],
        out_specs=pl.BlockSpec((tm, LANES), lambda i: (i, 0)),
        out_shape=jax.ShapeDtypeStruct((S, LANES), F32),
        scratch_shapes=[pltpu.VMEM((1, LANES), F32)],
        compiler_params=_cparams(("arbitrary",)),
    )(proj, bf)


def _fox_prep_bwd(dc, proj, bf, name):
    S = proj.shape[0]
    tm = _pick(S, (256,))
    nb = S // tm

    def body(dc_ref, f_ref, b_ref, df_ref, db_ref, carry_ref):
        @pl.when(pl.program_id(0) == 0)
        def _():
            carry_ref[...] = jnp.zeros_like(carry_ref)
            db_ref[...] = jnp.zeros_like(db_ref)
        r = lax.broadcasted_iota(jnp.int32, (tm, tm), 0)
        cidx = lax.broadcasted_iota(jnp.int32, (tm, tm), 1)
        tri = (cidx >= r).astype(BF16)
        hi, mid, lo = _split3(dc_ref[...])
        dlf = carry_ref[...] + (_dot(tri, hi) + _dot(tri, mid) + _dot(tri, lo))
        carry_ref[...] = dlf[0:1, :]
        u = f_ref[...] + b_ref[...]
        e = jnp.exp(-jnp.abs(u))
        sneg = jnp.where(u >= 0.0, e, 1.0) / (1.0 + e)
        lane = lax.broadcasted_iota(jnp.int32, (1, LANES), 1)
        df = jnp.where(lane < FOX_HEADS, dlf * sneg, 0.0)
        df_ref[...] = df
        db_ref[...] += jnp.sum(df, axis=0, keepdims=True)

    return pl.pallas_call(
        body, name=name, grid=(nb,),
        in_specs=[pl.BlockSpec((tm, LANES), lambda i: (nb - 1 - i, 0)),
                  pl.BlockSpec((tm, LANES), lambda i: (nb - 1 - i, L1_F // LANES)),
                  pl.BlockSpec((1, LANES), lambda i: (0, 0))],
        out_specs=[pl.BlockSpec((tm, LANES), lambda i: (nb - 1 - i, 0)), pl.BlockSpec((1, LANES), lambda i: (0, 0))],
        out_shape=[jax.ShapeDtypeStruct((S, LANES), F32), jax.ShapeDtypeStruct((1, LANES), F32)],
        scratch_shapes=[pltpu.VMEM((1, LANES), F32)],
        compiler_params=_cparams(("arbitrary",)),
    )(dc, proj, bf)


def _att_specs(kind, S, T):
    if kind == "sb":
        qo, ko, vo, go = L0_SBQ // LANES, L0_SBK // LANES, L0_SBV // LANES, L0_SBG // LANES
    elif kind == "fox":
        qo, ko, vo, go = L1_Q // LANES, L1_K // LANES, L1_V // LANES, L1_G // LANES
    else:
        go = L0_MLG // LANES
        return (pl.BlockSpec((T, 256), lambda p, i: (i, p)), pl.BlockSpec((S, 256), lambda p, i: (0, p)),
                pl.BlockSpec((S, LANES), lambda p, i: (0, p)), pl.BlockSpec((T, LANES), lambda p, i: (i, go + p)))
    return (pl.BlockSpec((T, LANES), lambda p, i: (i, qo + p)), pl.BlockSpec((S, LANES), lambda p, i: (0, ko + p)),
            pl.BlockSpec((S, LANES), lambda p, i: (0, vo + p)), pl.BlockSpec((T, LANES), lambda p, i: (i, go + p)))


def _per_q_tile(tile_body, hows):
    T = ATT_T

    def view(ref, u, how):
        if how == "rows":
            return ref.at[pl.ds(u * T, T)]
        if how == "lanes":
            return ref.at[:, pl.ds(u * T, T)]
        if how == "stat":
            return ref.at[:, u]
        return ref

    def body(*refs):
        for u in range(ATT_QSUB):
            tile_body(pl.program_id(1) * ATT_QSUB + u, *[view(r, u, how) for r, how in zip(refs, hows)])

    return body


def _mask_flags(js, masked_at):
    return [t == masked_at for t in range(len(js))]


def _loop_tiles(i, tiles, right_to_left, G=ATT_GROUP):
    ng = i // G
    rest = i - ng * G

    def leftover():
        for r in range(G):
            @pl.when(rest == r)
            def _():
                if right_to_left:
                    tiles([i - u for u in range(r + 1)], 0)
                else:
                    tiles([ng * G + u for u in range(r + 1)], r)

    def group(g, carry):
        if right_to_left:
            tiles([ng * G - 1 - (g * G + u) for u in range(G)], None)
        else:
            tiles([g * G + u for u in range(G)], None)
        return carry

    if right_to_left:
        leftover()
    lax.fori_loop(0, ng, group, 0)
    if not right_to_left:
        leftover()


def _head_q(kind, q_ref, m0, scale):
    if kind == "mla":
        return [q_ref[:, 0:LANES].astype(BF16), q_ref[:, LANES:2 * LANES].astype(BF16)]
    qv = q_ref[...] * scale
    return [jnp.where(m0, qv, 0.0).astype(BF16), jnp.where(m0, 0.0, qv).astype(BF16)]


def _head_k(kind, k_ref, start, T):
    if kind == "mla":
        return [k_ref[pl.ds(start, T), 0:LANES].astype(BF16), k_ref[pl.ds(start, T), LANES:2 * LANES].astype(BF16)]
    kb = k_ref[pl.ds(start, T), :].astype(BF16)
    return [kb, kb]


def _transpose_tiles(src, col_off, n_out, cw, group, name):
    S = src.shape[0]
    T = ATT_T
    first = col_off // (group * cw)

    def body(x_ref, o_ref):
        for u in range(group):
            o_ref[u] = x_ref[:, u * cw:(u + 1) * cw].T.astype(BF16)

    return pl.pallas_call(
        body, name=name, grid=(S // T, n_out // group),
        in_specs=[pl.BlockSpec((T, group * cw), lambda j, g: (j, first + g))],
        out_specs=pl.BlockSpec((group, None, cw, T), lambda j, g: (g, j, 0, 0)),
        out_shape=jax.ShapeDtypeStruct((n_out, S // T, cw, T), BF16),
        compiler_params=_cparams(("parallel", "parallel")),
    )(src)


def _softmax_fwd(kind, qkvg, c_col, S, npairs, name):
    T = ATT_T
    nq = S // T
    fox = kind == "fox"
    scale = (96 if kind == "mla" else 64) ** -0.5

    def body(i, *refs):
        if fox:
            q_ref, k_ref, vt_ref, g_ref, cc_ref, o_ref, og_ref, ogt_ref, st_ref, m_ref, acc_ref = refs
        else:
            q_ref, k_ref, vt_ref, g_ref, o_ref, og_ref, ogt_ref, st_ref, m_ref, acc_ref = refs
        m0 = lax.broadcasted_iota(jnp.int32, (1, LANES), 1) < 64
        top = lax.broadcasted_iota(jnp.int32, (LANES, 1), 0) < 64
        key = lax.broadcasted_iota(jnp.int32, (T, LANES), 0)
        qrow = lax.broadcasted_iota(jnp.int32, (T, LANES), 1)
        qh = _head_q(kind, q_ref, m0, scale)
        m_ref[...] = jnp.full(m_ref.shape, NEG, F32)
        acc_ref[...] = jnp.zeros(acc_ref.shape, F32)
        chains = [(h, b) for h in range(2) for b in range(T // LANES)]

        def tiles(js, masked_at):
            starts = [pl.multiple_of(j * T, T) for j in js]
            zss = []
            for start in starts:
                kh = _head_k(kind, k_ref, start, T)
                zss.append(_split_blocks([_dot_nt(kh[h], qh[h]) for h in range(2)]))
            pss, alss = [], []
            for start, zs, masked in zip(starts, zss, _mask_flags(js, masked_at)):
                ps, alphas = [], []
                for (h, b), z in zip(chains, zs):
                    lanes = slice(b * LANES, (b + 1) * LANES)
                    if kind == "mla":
                        z = z * scale
                    if fox:
                        z = z - cc_ref[h, pl.ds(start, T), :]
                    if masked:
                        z = jnp.where(key <= qrow + b * LANES, z, NEG)
                    m_prev = m_ref[h, :, lanes]
                    m_new = jnp.maximum(m_prev, jnp.max(z, axis=0, keepdims=True))
                    alphas.append(jnp.exp(m_prev - m_new))
                    ps.append(jnp.exp(z - m_new).astype(BF16))
                    m_ref[h, :, lanes] = m_new
                pss.append(_join_blocks(ps, T // LANES))
                alss.append(_join_blocks(alphas, T // LANES))
            for j, ps, alphas in zip(js, pss, alss):
                vt = vt_ref[j]
                vth = [jnp.where(top, vt, 1.0).astype(BF16), jnp.where(top, 1.0, vt).astype(BF16)]
                for h in range(2):
                    acc_ref[h] = alphas[h] * acc_ref[h] + _dot(vth[h], ps[h])

        _loop_tiles(i, tiles, False, 2 * ATT_GROUP)
        acc = [acc_ref[0], acc_ref[1]]
        ot = jnp.concatenate([acc[0][0:64] / acc[0][64:128], acc[1][64:128] / acc[1][0:64]], axis=0)
        o = ot.T
        o_ref[...] = o
        gt = g_ref[...]
        og = o * (gt * _sigmoid(gt))
        og_ref[...] = og.astype(BF16)
        ogt_ref[...] = og.T.astype(BF16)
        st_ref[0] = m_ref[0] + jnp.log(acc[0][64:65])
        st_ref[1] = m_ref[1] + jnp.log(acc[1][0:1])

    QT = ATT_QSUB * T
    qs, ks, _, gs = _att_specs(kind, S, QT)
    in_specs = [qs, ks, pl.BlockSpec((None, nq, LANES, T), lambda p, i: (p, 0, 0, 0)), gs]
    args = list(qkvg)
    hows = ["rows", None, None, "rows"]
    if fox:
        in_specs += [pl.BlockSpec((2, S, LANES), lambda p, i: (p, 0, 0))]
        args += [c_col]
        hows += [None]
    hows += ["rows", "rows", "lanes", "stat", None, None]
    W = npairs * LANES
    return pl.pallas_call(
        _per_q_tile(body, hows), name=name, grid=(npairs, nq // ATT_QSUB), in_specs=in_specs,
        out_specs=[pl.BlockSpec((QT, LANES), lambda p, i: (i, p)), pl.BlockSpec((QT, LANES), lambda p, i: (i, p)),
                   pl.BlockSpec((LANES, QT), lambda p, i: (p, i)),
                   pl.BlockSpec((2, ATT_QSUB, 1, T), lambda p, i: (p, i, 0, 0))],
        out_shape=[jax.ShapeDtypeStruct((S, W), F32), jax.ShapeDtypeStruct((S, W), BF16),
                   jax.ShapeDtypeStruct((W, S), BF16),
                   jax.ShapeDtypeStruct((2 * npairs, nq, 1, T), F32)],
        scratch_shapes=[pltpu.VMEM((2, 1, T), F32), pltpu.VMEM((2, LANES, T), F32)],
        compiler_params=_cparams(("parallel", "parallel")),
    )(*args)


def _softplus_parts(z):
    sp = jnp.maximum(z, 0.0) + jnp.log(1.0 + jnp.exp(-jnp.abs(z)))
    return sp, z - sp


def _cumsum_dot(tri2, his, los):
    return _split_blocks([_dot(tri2, jnp.concatenate([hi, lo], axis=0)) for hi, lo in zip(his, los)])


def _split2(x):
    hi = x.astype(BF16)
    return hi, (x - hi.astype(F32)).astype(BF16)


def _split_blocks(per_head):
    return [x[:, b * LANES:(b + 1) * LANES] for x in per_head for b in range(x.shape[1] // LANES)]


def _join_blocks(per_block, nb):
    return [jnp.concatenate(per_block[h * nb:(h + 1) * nb], axis=1) for h in range(len(per_block) // nb)]


def _row_of(col):
    return jnp.broadcast_to(col, (col.shape[0], LANES)).T[0:1]


def _softmax_bwd_t(kind, q, k, kt, v, do, do_off, o, lse, c_col, S, npairs, name):
    T = ATT_T
    nq = S // T
    nb = T // LANES
    fox = kind == "fox"
    mla = kind == "mla"
    scale = (96 if mla else 64) ** -0.5
    kw = 256 if mla else LANES

    def body(i, *refs):
        if fox:
            (q_ref, k_ref, kt_ref, v_ref, do_ref, o_ref, st_ref, cc_ref,
             dq_ref, dk_ref, dv_ref, dck_ref, dcq_ref, dqt_ref, rs_ref, dkx_ref) = refs
        else:
            q_ref, k_ref, kt_ref, v_ref, do_ref, o_ref, st_ref, dq_ref, dk_ref, dv_ref, dqt_ref = refs

        @pl.when(i == 0)
        def _():
            dv_ref[...] = jnp.zeros_like(dv_ref)
            if fox:
                dkx_ref[...] = jnp.zeros_like(dkx_ref)
            else:
                dk_ref[...] = jnp.zeros_like(dk_ref)

        m0 = lax.broadcasted_iota(jnp.int32, (1, LANES), 1) < 64
        top = lax.broadcasted_iota(jnp.int32, (LANES, 1), 0) < 64
        key = lax.broadcasted_iota(jnp.int32, (T, LANES), 0)
        qrow = lax.broadcasted_iota(jnp.int32, (T, LANES), 1)
        qh = _head_q(kind, q_ref, m0, scale)
        if fox:
            qv = q_ref[...] * scale
            qk = [jnp.where(m0, qv, 1.0).astype(BF16), jnp.where(m0, 1.0, qv).astype(BF16)]
        else:
            qk = qh
        dov = do_ref[...]
        prod = dov * o_ref[...]
        dd = [_row_of(jnp.sum(jnp.where(m0, prod, 0.0), axis=1, keepdims=True)),
              _row_of(jnp.sum(jnp.where(m0, 0.0, prod), axis=1, keepdims=True))]
        doh = [jnp.where(m0, dov, 0.0).astype(BF16), jnp.where(m0, 0.0, dov).astype(BF16)]
        lse = [st_ref[0], st_ref[1]]
        dqt_ref[...] = jnp.zeros_like(dqt_ref)
        if fox:
            rs_ref[...] = jnp.zeros_like(rs_ref)
        chains = [(h, b) for h in range(2) for b in range(nb)]

        def tiles(js, masked_at):
            starts = [pl.multiple_of(j * T, T) for j in js]
            zss, dpss = [], []
            for start in starts:
                vb = v_ref[pl.ds(start, T), :].astype(BF16)
                kh = _head_k(kind, k_ref, start, T)
                zss.append(_split_blocks([_dot_nt(kh[h], qh[h]) for h in range(2)]))
                dpss.append(_split_blocks([_dot_nt(vb, doh[h]) for h in range(2)]))
            pss, dsss = [], []
            for start, zs, dps, masked in zip(starts, zss, dpss, _mask_flags(js, masked_at)):
                ps, dss = [], []
                for (h, b), z, dp in zip(chains, zs, dps):
                    lanes = slice(b * LANES, (b + 1) * LANES)
                    if mla:
                        z = z * scale
                    if fox:
                        z = z - cc_ref[h, pl.ds(start, T), :]
                    if masked:
                        z = jnp.where(key <= qrow + b * LANES, z, NEG)
                    p = jnp.exp(z - lse[h][:, lanes])
                    ds = p * (dp - dd[h][:, lanes])
                    dsb = ds.astype(BF16)
                    if fox:
                        rs_ref[h, :, lanes] += jnp.sum(dsb.astype(F32), axis=0, keepdims=True)
                    ps.append(p.astype(BF16))
                    dss.append(dsb)
                pss.append(_join_blocks(ps, nb))
                dsss.append(_join_blocks(dss, nb))
            for j, start, ps, dss in zip(js, starts, pss, dsss):
                kt = kt_ref[j]
                dvc = None
                for h in range(2):
                    dkh = _dot(dss[h], qk[h])
                    dvh = _dot(ps[h], doh[h])
                    dvc = dvh if dvc is None else dvc + dvh
                    kth = kt[h * LANES:(h + 1) * LANES] if mla else kt
                    dqt_ref[h] += _dot(kth, dss[h])
                    if fox:
                        dkx_ref[h, pl.ds(start, T), :] += dkh
                    elif mla:
                        dk_ref[pl.ds(start, T), h * LANES:(h + 1) * LANES] += dkh * scale
                    else:
                        dk_ref[pl.ds(start, T), :] += dkh
                dv_ref[pl.ds(start, T), :] += dvc

        _loop_tiles(i, tiles, False)
        if mla:
            dq_ref[:, 0:LANES] = dqt_ref[0].T * scale
            dq_ref[:, LANES:2 * LANES] = dqt_ref[1].T * scale
        else:
            dq_ref[...] = jnp.where(top, dqt_ref[0], dqt_ref[1]).T * scale
        if fox:
            dcq_ref[0] = rs_ref[0]
            dcq_ref[1] = rs_ref[1]

            @pl.when(i == nq - 1)
            def _():
                dk_ref[...] = jnp.where(m0, dkx_ref[0], dkx_ref[1])
                dck_ref[0] = dkx_ref[0].T[64:65]
                dck_ref[1] = dkx_ref[1].T[0:1]

    QT = ATT_QSUB * T
    qs, ks, vs, _ = _att_specs(kind, S, QT)
    stat = pl.BlockSpec((2, ATT_QSUB, 1, T), lambda p, i: (p, i, 0, 0))
    in_specs = [qs, ks, pl.BlockSpec((None, nq, kw, T), lambda p, i: (p, 0, 0, 0)), vs,
                pl.BlockSpec((QT, LANES), lambda p, i: (i, do_off + p)),
                pl.BlockSpec((QT, LANES), lambda p, i: (i, p)), stat]
    args = [q, k, kt, v, do, o, lse]
    hows = ["rows", None, None, None, "rows", "rows", "stat"]
    W = npairs * LANES
    out_specs = [pl.BlockSpec((QT, kw), lambda p, i: (i, p)), pl.BlockSpec((S, kw), lambda p, i: (0, p)),
                 pl.BlockSpec((S, LANES), lambda p, i: (0, p))]
    out_shape = [jax.ShapeDtypeStruct((S, npairs * kw), F32), jax.ShapeDtypeStruct((S, npairs * kw), F32),
                 jax.ShapeDtypeStruct((S, W), F32)]
    scratch = [pltpu.VMEM((2, LANES, T), F32)]
    if fox:
        in_specs.append(pl.BlockSpec((2, S, LANES), lambda p, i: (p, 0, 0)))
        args.append(c_col)
        out_specs += [pl.BlockSpec((2, 1, S), lambda p, i: (p, 0, 0)), stat]
        out_shape += [jax.ShapeDtypeStruct((2 * npairs, 1, S), F32), jax.ShapeDtypeStruct((2 * npairs, nq, 1, T), F32)]
        scratch += [pltpu.VMEM((2, 1, T), F32), pltpu.VMEM((2, S, LANES), F32)]
        hows += [None, "rows", None, None, None, "stat", None, None, None]
    else:
        hows += ["rows", None, None, None]
    return pl.pallas_call(
        _per_q_tile(body, hows), name=name, grid=(npairs, nq // ATT_QSUB), in_specs=in_specs, out_specs=out_specs,
        out_shape=out_shape, scratch_shapes=scratch, compiler_params=_cparams(("parallel", "arbitrary")),
    )(*args)


def _sb_fwd_t(proj, vt, S, npairs, name):
    T = ATT_T
    nq = S // T
    nb = T // LANES
    scale = 64 ** -0.5

    def body(i, q_ref, k_ref, vt_ref, g_ref, o_ref, og_ref, ogt_ref, st_ref, rem_ref, acc_ref):
        m0 = lax.broadcasted_iota(jnp.int32, (1, LANES), 1) < 64
        top = lax.broadcasted_iota(jnp.int32, (LANES, 1), 0) < 64
        key = lax.broadcasted_iota(jnp.int32, (T, LANES), 0)
        qrow = lax.broadcasted_iota(jnp.int32, (T, LANES), 1)
        r = lax.broadcasted_iota(jnp.int32, (T, T), 0)
        c = lax.broadcasted_iota(jnp.int32, (T, T), 1)
        after = (c > r).astype(BF16)
        after2 = jnp.concatenate([after, after], axis=1)
        qh = _head_q("sb", q_ref, m0, scale)
        rem_ref[...] = jnp.zeros_like(rem_ref)
        acc_ref[...] = jnp.zeros_like(acc_ref)
        chains = [(h, b) for h in range(2) for b in range(nb)]

        def tiles(js, masked_at):
            zss = []
            for j in js:
                kb = k_ref[pl.ds(pl.multiple_of(j * T, T), T), :].astype(BF16)
                zss.append(_split_blocks([_dot_nt(kb, qh[h]) for h in range(2)]))
            lass, sums, hiss, loss = [], [], [], []
            for zs, masked in zip(zss, _mask_flags(js, masked_at)):
                las, sm, his, los = [], [], [], []
                for (h, b), z in zip(chains, zs):
                    sp, la = _softplus_parts(z)
                    if masked:
                        sp = jnp.where(key < qrow + b * LANES, sp, 0.0)
                    hi, lo = _split2(sp)
                    las.append(la)
                    sm.append(jnp.sum(sp, axis=0, keepdims=True))
                    his.append(hi)
                    los.append(lo)
                lass.append(las)
                sums.append(sm)
                hiss.append(_join_blocks(his, nb))
                loss.append(_join_blocks(los, nb))
            rcss = [_cumsum_dot(after2, his, los) for his, los in zip(hiss, loss)]
            wss = []
            for las, sm, rcs, masked in zip(lass, sums, rcss, _mask_flags(js, masked_at)):
                ws = []
                for (h, b), la, s, rc in zip(chains, las, sm, rcs):
                    lanes = slice(b * LANES, (b + 1) * LANES)
                    w = jnp.exp(la - (rem_ref[h, :, lanes] + rc))
                    if masked:
                        w = jnp.where(key < qrow + b * LANES, w, 0.0)
                    ws.append(w.astype(BF16))
                    rem_ref[h, :, lanes] += s
                wss.append(_join_blocks(ws, nb))
            for j, ws in zip(js, wss):
                vtb = vt_ref[j]
                for h in range(2):
                    acc_ref[h] += _dot(vtb, ws[h])

        _loop_tiles(i, tiles, True)
        o = jnp.where(top, acc_ref[0], acc_ref[1]).T
        o_ref[...] = o
        gt = g_ref[...]
        og = o * (gt * _sigmoid(gt))
        og_ref[...] = og.astype(BF16)
        ogt_ref[...] = og.T.astype(BF16)
        st_ref[0] = rem_ref[0]
        st_ref[1] = rem_ref[1]

    QT = ATT_QSUB * T
    qs, ks, _, gs = _att_specs("sb", S, QT)
    W = npairs * LANES
    hows = ["rows", None, None, "rows", "rows", "rows", "lanes", "stat", None, None]
    return pl.pallas_call(
        _per_q_tile(body, hows), name=name, grid=(npairs, nq // ATT_QSUB),
        in_specs=[qs, ks, pl.BlockSpec((None, nq, LANES, T), lambda p, i: (p, 0, 0, 0)), gs],
        out_specs=[pl.BlockSpec((QT, LANES), lambda p, i: (i, p)), pl.BlockSpec((QT, LANES), lambda p, i: (i, p)),
                   pl.BlockSpec((LANES, QT), lambda p, i: (p, i)),
                   pl.BlockSpec((2, ATT_QSUB, 1, T), lambda p, i: (p, i, 0, 0))],
        out_shape=[jax.ShapeDtypeStruct((S, W), F32), jax.ShapeDtypeStruct((S, W), BF16),
                   jax.ShapeDtypeStruct((W, S), BF16),
                   jax.ShapeDtypeStruct((2 * npairs, nq, 1, T), F32)],
        scratch_shapes=[pltpu.VMEM((2, 1, T), F32), pltpu.VMEM((2, LANES, T), F32)],
        compiler_params=_cparams(("parallel", "parallel")),
    )(proj, proj, vt, proj)


def _sb_bwd_t(proj, kt, do, tot, S, npairs, name):
    T = ATT_T
    nq = S // T
    nb = T // LANES
    scale = 64 ** -0.5

    def body(i, q_ref, k_ref, kt_ref, v_ref, do_ref, st_ref, dq_ref, dk_ref, dv_ref, dqt_ref, pre_ref, gpre_ref):

        @pl.when(i == 0)
        def _():
            dk_ref[...] = jnp.zeros_like(dk_ref)
            dv_ref[...] = jnp.zeros_like(dv_ref)

        m0 = lax.broadcasted_iota(jnp.int32, (1, LANES), 1) < 64
        top = lax.broadcasted_iota(jnp.int32, (LANES, 1), 0) < 64
        key = lax.broadcasted_iota(jnp.int32, (T, LANES), 0)
        qrow = lax.broadcasted_iota(jnp.int32, (T, LANES), 1)
        r = lax.broadcasted_iota(jnp.int32, (T, T), 0)
        c = lax.broadcasted_iota(jnp.int32, (T, T), 1)
        upto = (c <= r).astype(BF16)
        upto2 = jnp.concatenate([upto, upto], axis=1)
        left = (c < r).astype(BF16)
        qh = _head_q("sb", q_ref, m0, scale)
        dov = do_ref[...]
        doh = [jnp.where(m0, dov, 0.0).astype(BF16), jnp.where(m0, 0.0, dov).astype(BF16)]
        tot_h = [st_ref[0], st_ref[1]]
        dqt_ref[...] = jnp.zeros_like(dqt_ref)
        pre_ref[...] = jnp.zeros_like(pre_ref)
        gpre_ref[...] = jnp.zeros_like(gpre_ref)
        chains = [(h, b) for h in range(2) for b in range(nb)]

        def tiles(js, masked_at):
            starts = [pl.multiple_of(j * T, T) for j in js]
            zss, dwss = [], []
            for start in starts:
                vb = v_ref[pl.ds(start, T), :].astype(BF16)
                kb = k_ref[pl.ds(start, T), :].astype(BF16)
                zss.append(_split_blocks([_dot_nt(kb, qh[h]) for h in range(2)]))
                dwss.append(_split_blocks([_dot_nt(vb, doh[h]) for h in range(2)]))
            lass, sums, hiss, loss = [], [], [], []
            for zs, masked in zip(zss, _mask_flags(js, masked_at)):
                las, sm, his, los = [], [], [], []
                for (h, b), z in zip(chains, zs):
                    sp, la = _softplus_parts(z)
                    if masked:
                        sp = jnp.where(key < qrow + b * LANES, sp, 0.0)
                    hi, lo = _split2(sp)
                    las.append(la)
                    sm.append(jnp.sum(sp, axis=0, keepdims=True))
                    his.append(hi)
                    los.append(lo)
                lass.append(las)
                sums.append(sm)
                hiss.append(_join_blocks(his, nb))
                loss.append(_join_blocks(los, nb))
            pcss = [_cumsum_dot(upto2, his, los) for his, los in zip(hiss, loss)]
            wss, gss = [], []
            for las, sm, pcs, dws, masked in zip(lass, sums, pcss, dwss, _mask_flags(js, masked_at)):
                ws, gs = [], []
                for (h, b), la, s, pc, dw in zip(chains, las, sm, pcs, dws):
                    lanes = slice(b * LANES, (b + 1) * LANES)
                    w = jnp.exp(la - ((tot_h[h][:, lanes] - pre_ref[h, :, lanes]) - pc))
                    if masked:
                        w = jnp.where(key < qrow + b * LANES, w, 0.0)
                    ws.append(w.astype(BF16))
                    gs.append(dw * w)
                    pre_ref[h, :, lanes] += s
                wss.append(_join_blocks(ws, nb))
                gss.append(gs)
            gcss = [_split_blocks([_dot(left, g) for g in _join_blocks([g.astype(BF16) for g in gs], nb)]) for gs in gss]
            dzss = []
            for las, gs, gcs, masked in zip(lass, gss, gcss, _mask_flags(js, masked_at)):
                dzs = []
                for (h, b), la, g, gc in zip(chains, las, gs, gcs):
                    lanes = slice(b * LANES, (b + 1) * LANES)
                    dz = g - (g + (gpre_ref[h, :, lanes] + gc)) * jnp.exp(la)
                    if masked:
                        dz = jnp.where(key < qrow + b * LANES, dz, 0.0)
                    dzs.append(dz.astype(BF16))
                    gpre_ref[h, :, lanes] += jnp.sum(g, axis=0, keepdims=True)
                dzss.append(_join_blocks(dzs, nb))
            for j, start, ws, dzs in zip(js, starts, wss, dzss):
                kt = kt_ref[j]
                dkc = dvc = None
                for h in range(2):
                    dkh = _dot(dzs[h], qh[h])
                    dvh = _dot(ws[h], doh[h])
                    dkc = dkh if dkc is None else dkc + dkh
                    dvc = dvh if dvc is None else dvc + dvh
                    dqt_ref[h] += _dot(kt, dzs[h])
                dk_ref[pl.ds(start, T), :] += dkc
                dv_ref[pl.ds(start, T), :] += dvc

        _loop_tiles(i, tiles, False)
        dq_ref[...] = jnp.where(top, dqt_ref[0], dqt_ref[1]).T * scale

    QT = ATT_QSUB * T
    qs, ks, vs, _ = _att_specs("sb", S, QT)
    W = npairs * LANES
    hows = ["rows", None, None, None, "rows", "stat", "rows", None, None, None, None, None]
    return pl.pallas_call(
        _per_q_tile(body, hows), name=name, grid=(npairs, nq // ATT_QSUB),
        in_specs=[qs, ks, pl.BlockSpec((None, nq, LANES, T), lambda p, i: (p, 0, 0, 0)), vs,
                  pl.BlockSpec((QT, LANES), lambda p, i: (i, p)),
                  pl.BlockSpec((2, ATT_QSUB, 1, T), lambda p, i: (p, i, 0, 0))],
        out_specs=[pl.BlockSpec((QT, LANES), lambda p, i: (i, p)), pl.BlockSpec((S, LANES), lambda p, i: (0, p)),
                   pl.BlockSpec((S, LANES), lambda p, i: (0, p))],
        out_shape=[jax.ShapeDtypeStruct((S, W), F32)] * 3,
        scratch_shapes=[pltpu.VMEM((2, LANES, T), F32), pltpu.VMEM((2, 1, T), F32), pltpu.VMEM((2, 1, T), F32)],
        compiler_params=_cparams(("parallel", "arbitrary")),
    )(proj, proj, kt, proj, do, tot)


def _pad_w0(w):
    z = lambda n: jnp.zeros((w.shape[0], n), w.dtype)
    return jnp.concatenate([w[:, 2048:2432], w[:, 2432:2688], z(64), w[:, 2688:2720], z(32),
                            w[:, 1536:2048], w[:, 2720:3232], w[:, 0:512], w[:, 512:1024], w[:, 1024:1536]], axis=1)


def _unpad_w0(wp):
    return jnp.concatenate([wp[:, L0_SBQ:L0_SBQ + 512], wp[:, L0_SBK:L0_SBK + 512], wp[:, L0_SBV:L0_SBV + 512],
                            wp[:, L0_SBG:L0_SBG + 512], wp[:, 0:384], wp[:, 384:640], wp[:, 704:736],
                            wp[:, L0_MLG:L0_MLG + 512]], axis=1)


def _pad_wq(w):
    return jnp.pad(w.reshape(384, 8, 96), ((0, 0), (0, 0), (0, 32))).reshape(384, 1024)


def _unpad_wq(wp):
    return wp.reshape(384, 8, 128)[:, :, :96].reshape(384, 768)


def _pad_wkv(w):
    w3 = w.reshape(256, 8, 128)
    k = jnp.pad(w3[:, :, :64], ((0, 0), (0, 0), (0, 64))).reshape(256, 1024)
    return jnp.concatenate([k, w3[:, :, 64:].reshape(256, 512)], axis=1)


def _unpad_wkv(wp):
    k = wp[:, :1024].reshape(256, 8, 128)[:, :, :64]
    v = wp[:, 1024:].reshape(256, 8, 64)
    return jnp.concatenate([k, v], axis=-1).reshape(256, 1024)


def _pad_w1(w):
    return jnp.concatenate([w, jnp.zeros((w.shape[0], L1_WIDTH - ODD_IN_WIDTH), w.dtype)], axis=1)


def _local_step(x, positions, target, g, w0p, wqp, wkvp, wo0, w1p, wo1):
    S = x.shape[0]
    nq = S // ATT_T
    invf = ROPE_THETA ** (-jnp.arange(0, MLA_ROPE_DIM, 2, dtype=F32) / MLA_ROPE_DIM)
    invf = jnp.concatenate([jnp.zeros((64,), F32), invf, invf, jnp.zeros((32,), F32)]).reshape(1, LANES)
    cosT, s1T, s2T = _rope_tables(positions.reshape(S, 1), invf, "rope_tables")
    bfp = jnp.pad(g["l1_b_f"], ((0, 0), (0, LANES - FOX_HEADS)))

    proj0, h0t = _norm_matmul(x, g["l0_pre_g"], w0p, "l0_in_proj")
    qm, km, vm, qnt, cnt = _mla_prep(proj0, g["l0_q_a_g"], g["l0_kv_a_g"], wqp, wkvp, cosT, s1T, s2T, "mla_prep")
    sb_vt = _transpose_tiles(proj0, L0_SBV, 4, LANES, 2, "sb_vt")
    sb_kt = _transpose_tiles(proj0, L0_SBK, 4, LANES, 2, "sb_kt")
    o_sb, og_sb, ogt_sb, tot_sb = _sb_fwd_t(proj0, sb_vt, S, 4, "sb_fwd")
    vmt = _transpose_tiles(vm, 0, 4, LANES, 4, "mla_vt")
    kmt = _transpose_tiles(km, 0, 4, 2 * LANES, 4, "mla_kt")
    o_ml, og_ml, ogt_ml, lse_ml = _softmax_fwd("mla", (qm, km, vmt, proj0), None, S, 4, "mla_fwd")
    y0, x1 = _out_proj(og_sb, og_ml, 0, 0, wo0, x, g["l0_post_g"], None, "l0_out_proj")

    proj1, h1t = _norm_matmul(x1, g["l1_pre_g"], w1p, "l1_in_proj")
    cfx = _fox_prep(proj1, bfp, "fox_prep")
    c16 = cfx[:, :FOX_HEADS].T
    c_col = jnp.broadcast_to(c16[:, :, None], (FOX_HEADS, S, LANES))
    vt1 = _transpose_tiles(proj1, L1_V, 8, LANES, 8, "fox_vt")
    kt1 = _transpose_tiles(proj1, L1_K, 8, LANES, 8, "fox_kt")
    o_fx, og_fx, ogt_fx, lse_fx = _softmax_fwd("fox", (proj1, proj1, vt1, proj1), c_col, S, 8, "fox_fwd")
    y1, dx2, lsum = _out_proj(og_fx, og_fx, 0, 1, wo1, x1, g["l1_post_g"], target, "l1_out_proj")

    dy1, do1, dgate1, d_post1 = _out_proj_bwd(dx2, y1, g["l1_post_g"], wo1, proj1, (L1_G, L1_G + 512), o_fx, o_fx, 0, 1, "l1_out_bwd")
    dwo1 = _matmul_t(ogt_fx, dy1, "l1_dw_out")
    dq1, dk1, dv1, dck, dcq = _softmax_bwd_t("fox", proj1, proj1, kt1, proj1, do1, 0, o_fx, lse_fx, c_col, S, 8,
                                             "fox_bwd")
    dc = jnp.pad((dcq.reshape(FOX_HEADS, S) - dck.reshape(FOX_HEADS, S)).T, ((0, 0), (0, LANES - FOX_HEADS)))
    df, d_bf = _fox_prep_bwd(dc, proj1, bfp, "fox_prep_bwd")
    pieces1 = [(L1_Q, dq1), (L1_K, dk1), (L1_V, dv1), (L1_G, dgate1), (L1_F, df)]
    dx1, d_pre1 = _in_proj_bwd(pieces1, w1p, x1, g["l1_pre_g"], dx2, "l1_in_bwd")
    dw1p = jnp.concatenate(_matmul_t_many(h1t, [dq1, dk1], "l1_dw_in_a")
                           + _matmul_t_many(h1t, [dv1, dgate1, df], "l1_dw_in_b"), axis=1)

    dy0, do0, dgate0, d_post0 = _out_proj_bwd(dx1, y0, g["l0_post_g"], wo0, proj0, (L0_SBG, L0_MLG), o_sb, o_ml, 0, 0,
                                              "l0_out_bwd")
    dwo0 = jnp.concatenate([_matmul_t(ogt_sb, dy0, "l0_dw_out_sb"), _matmul_t(ogt_ml, dy0, "l0_dw_out_mla")], axis=0)
    dsq, dsk, dsv = _sb_bwd_t(proj0, sb_kt, do0, tot_sb, S, 4, "sb_bwd")
    dqm, dkm, dvm = _softmax_bwd_t("mla", qm, km, kmt, vm, do0, 4, o_ml, lse_ml, None, S, 4, "mla_bwd")
    dprep, dqb, dkvb, d_qag, d_kvag = _mla_prep_bwd(dqm, dkm, dvm, proj0, g["l0_q_a_g"], g["l0_kv_a_g"], wqp, wkvp,
                                                    cosT, s1T, s2T, "mla_prep_bwd")
    dwqp = _matmul_t(qnt, dqb, "l0_dw_qb")
    dwkvp = _matmul_t(cnt, dkvb, "l0_dw_kvb")
    pieces0 = [(L0_PREP, dprep), (L0_SBG, dgate0), (L0_SBQ, dsq), (L0_SBK, dsk), (L0_SBV, dsv)]
    dx0, d_pre0 = _in_proj_bwd(pieces0, w0p, x, g["l0_pre_g"], dx1, "l0_in_bwd")
    dw0p = jnp.concatenate(_matmul_t_many(h0t, [dprep, dgate0], "l0_dw_in_a")
                           + _matmul_t_many(h0t, [dsq, dsk, dsv], "l0_dw_in_b"), axis=1)

    grads = {
        "l0_pre_g": d_pre0, "l0_post_g": d_post0, "l0_w_in": dw0p, "l0_q_a_g": d_qag, "l0_w_q_b": dwqp,
        "l0_kv_a_g": d_kvag, "l0_w_kv_b": dwkvp, "l0_w_out": dwo0, "l1_pre_g": d_pre1, "l1_post_g": d_post1,
        "l1_w_in": dw1p, "l1_b_f": d_bf[:, :FOX_HEADS], "l1_w_out": dwo1,
    }
    return lsum, dx0, grads


_ANY = pl.BlockSpec(memory_space=pl.ANY)


def _place():
    return lax.axis_index("x"), lax.axis_index("y"), lax.axis_index("c")


def _other_chips(x, y):
    return [(1 - x, y), (x, 1 - y), (1 - x, 1 - y)]


def _half(rows, c):
    return pl.ds(c * (rows // 2), rows // 2)


def _weight_gather(parts):
    n = len(parts)

    def body(*refs):
        p_refs, out_refs, send_sems, recv_sems = refs[:n], refs[n:2 * n], refs[2 * n], refs[2 * n + 1]
        x, y, c = _place()
        sibling = (x, y, 1 - c)
        chips = _other_chips(x, y)

        def blk(k, chip, cc):
            return out_refs[k].at[2 * chip[0] + chip[1], _half(p_refs[k].shape[0], cc)]

        def copy(s, src, dst, to):
            return pltpu.make_async_remote_copy(src_ref=src, dst_ref=dst, send_sem=send_sems.at[s],
                                                recv_sem=recv_sems.at[s], device_id=to, device_id_type=MESH)

        first = [copy(6 * k + j, p_refs[k].at[_half(p_refs[k].shape[0], c)], blk(k, (x, y), c), (*chip, c))
                 for j, chip in enumerate(chips) for k in range(n)]
        for cp in first:
            cp.start()
        passed = []
        for j, chip in enumerate(chips):
            for k in range(n):
                copy(6 * k + j, blk(k, chip, c), blk(k, chip, c), (x, y, c)).wait_recv()
                passed.append(copy(6 * k + 3 + j, blk(k, chip, c), blk(k, chip, c), sibling))
                passed[-1].start()
        for j, chip in enumerate(chips):
            for k in range(n):
                copy(6 * k + 3 + j, blk(k, chip, 1 - c), blk(k, chip, 1 - c), (x, y, c)).wait_recv()
        for cp in first + passed:
            cp.wait_send()

    return pl.pallas_call(
        body, name="weight_gather", in_specs=[_ANY] * n, out_specs=[_ANY] * n,
        out_shape=[jax.ShapeDtypeStruct((4,) + a.shape, a.dtype) for a in parts],
        scratch_shapes=[pltpu.SemaphoreType.DMA((6 * n,)), pltpu.SemaphoreType.DMA((6 * n,))],
    )(*parts)


def _grad_core_exchange(ps):
    n = len(ps)

    def body(*refs):
        p_refs, recv_refs, send_sems, recv_sems = refs[:n], refs[n:2 * n], refs[2 * n], refs[2 * n + 1]
        x, y, c = _place()
        give = [pltpu.make_async_remote_copy(src_ref=p_refs[k].at[j, _half(p_refs[k].shape[1], 1 - c)],
                                             dst_ref=recv_refs[k].at[j], send_sem=send_sems.at[4 * k + j],
                                             recv_sem=recv_sems.at[4 * k + j], device_id=(x, y, 1 - c),
                                             device_id_type=MESH) for k in range(n) for j in range(4)]
        for cp in give:
            cp.start()
        for cp in give:
            cp.wait()

    return pl.pallas_call(
        body, name="grad_core_exchange", in_specs=[_ANY] * n, out_specs=[_ANY] * n,
        out_shape=[jax.ShapeDtypeStruct((4, p.shape[1] // 2, p.shape[2]), p.dtype) for p in ps],
        scratch_shapes=[pltpu.SemaphoreType.DMA((4 * n,)), pltpu.SemaphoreType.DMA((4 * n,))],
    )(*ps)


def _grad_rows(rows):
    return _pick(rows, (1296, 512, rows))


def _grad_add_cores(p, theirs, c1, name):
    _, rh, cols = theirs.shape
    tr = _grad_rows(rh)

    def body(c_ref, a_ref, b_ref, o_ref):
        o_ref[...] = (a_ref[...] + b_ref[...]).astype(BF16)

    spec = pl.BlockSpec((None, tr, cols), lambda j, r, c: (j, r, 0))
    grid_spec = pltpu.PrefetchScalarGridSpec(
        num_scalar_prefetch=1, grid=(4, rh // tr),
        in_specs=[pl.BlockSpec((None, None, tr, cols), lambda j, r, c: (j, c[0], r, 0)), spec], out_specs=spec)
    return pl.pallas_call(
        body, name=name, grid_spec=grid_spec, out_shape=jax.ShapeDtypeStruct(theirs.shape, BF16),
        compiler_params=_cparams(("parallel", "parallel")),
    )(c1, p.reshape(4, 2, rh, cols), theirs)


def _grad_chip_exchange(qs):
    n = len(qs)

    def body(*refs):
        q_refs, out_refs, send_sems, recv_sems = refs[:n], refs[n:2 * n], refs[2 * n], refs[2 * n + 1]
        x, y, c = _place()
        me = 2 * x + y
        chips = _other_chips(x, y)
        sends = [pltpu.make_async_remote_copy(src_ref=q_refs[k].at[2 * chip[0] + chip[1]], dst_ref=out_refs[k].at[me],
                                              send_sem=send_sems.at[3 * k + j], recv_sem=recv_sems.at[3 * k + j],
                                              device_id=(*chip, c), device_id_type=MESH)
                 for j, chip in enumerate(chips) for k in range(n)]
        for cp in sends:
            cp.start()
        for j, chip in enumerate(chips):
            for k in range(n):
                slot = out_refs[k].at[2 * chip[0] + chip[1]]
                pltpu.make_async_remote_copy(src_ref=slot, dst_ref=slot, send_sem=send_sems.at[3 * k + j],
                                             recv_sem=recv_sems.at[3 * k + j], device_id=(x, y, c),
                                             device_id_type=MESH).wait_recv()
        for cp in sends:
            cp.wait_send()

    return pl.pallas_call(
        body, name="grad_chip_exchange", in_specs=[_ANY] * n, out_specs=[_ANY] * n,
        out_shape=[jax.ShapeDtypeStruct(q.shape, q.dtype) for q in qs],
        scratch_shapes=[pltpu.SemaphoreType.DMA((3 * n,)), pltpu.SemaphoreType.DMA((3 * n,))],
    )(*qs)


def _grad_add_chips(q, slots, me1, name):
    _, rh, cols = q.shape
    tr = _grad_rows(rh)

    def body(me_ref, own_ref, s0, s1, s2, s3, o_ref):
        me = me_ref[0]
        t = [jnp.where(me == j, own_ref[...], s[...]).astype(F32) for j, s in enumerate((s0, s1, s2, s3))]
        o_ref[...] = ((t[0] + t[1]) + t[2]) + t[3]

    def slot_spec(j):
        return pl.BlockSpec((None, tr, cols), lambda r, me: (jnp.where(me[0] == j, (j + 1) % 4, j), r, 0))

    grid_spec = pltpu.PrefetchScalarGridSpec(
        num_scalar_prefetch=1, grid=(rh // tr,),
        in_specs=[pl.BlockSpec((None, tr, cols), lambda r, me: (me[0], r, 0))] + [slot_spec(j) for j in range(4)],
        out_specs=pl.BlockSpec((tr, cols), lambda r, me: (r, 0)))
    return pl.pallas_call(
        body, name=name, grid_spec=grid_spec, out_shape=jax.ShapeDtypeStruct(q.shape[1:], F32),
        compiler_params=_cparams(("parallel",)),
    )(me1, q, slots, slots, slots, slots)


def _grad_core_gather(ts):
    n = len(ts)

    def body(*refs):
        t_refs, out_refs, send_sems, recv_sems = refs[:n], refs[n:2 * n], refs[2 * n], refs[2 * n + 1]
        x, y, c = _place()
        give = [pltpu.make_async_remote_copy(src_ref=t_refs[k], dst_ref=out_refs[k], send_sem=send_sems.at[k],
                                             recv_sem=recv_sems.at[k], device_id=(x, y, 1 - c), device_id_type=MESH)
                for k in range(n)]
        for cp in give:
            cp.start()
        for cp in give:
            cp.wait()

    return pl.pallas_call(
        body, name="grad_core_gather", in_specs=[_ANY] * n, out_specs=[_ANY] * n,
        out_shape=[jax.ShapeDtypeStruct(t.shape, t.dtype) for t in ts],
        scratch_shapes=[pltpu.SemaphoreType.DMA((n,)), pltpu.SemaphoreType.DMA((n,))],
    )(*ts)


def _small_allreduce(sp):
    def body(sp_ref, out_ref, gath_ref, send_sems, recv_sems):
        x, y, c = _place()
        me = 4 * x + 2 * y + c
        gath_ref[me] = sp_ref[...]
        peers = []
        for k in range(1, 8):
            px = 1 - x if k & 4 else x
            py = 1 - y if k & 2 else y
            pc = 1 - c if k & 1 else c
            peers.append((px, py, pc))
        sends = [pltpu.make_async_remote_copy(src_ref=sp_ref, dst_ref=gath_ref.at[me], send_sem=send_sems.at[k],
                                              recv_sem=recv_sems.at[k], device_id=peer, device_id_type=MESH)
                 for k, peer in enumerate(peers)]
        for cp in sends:
            cp.start()
        for k, (px, py, pc) in enumerate(peers):
            slot = gath_ref.at[4 * px + 2 * py + pc]
            pltpu.make_async_remote_copy(src_ref=slot, dst_ref=slot, send_sem=send_sems.at[k], recv_sem=recv_sems.at[k],
                                         device_id=(x, y, c), device_id_type=MESH).wait_recv()
        for cp in sends:
            cp.wait_send()
        tot = gath_ref[0]
        for d in range(1, 8):
            tot = tot + gath_ref[d]
        out_ref[...] = tot

    vm = pl.BlockSpec(memory_space=pltpu.VMEM)
    return pl.pallas_call(
        body, name="small_allreduce", in_specs=[vm], out_specs=vm, out_shape=jax.ShapeDtypeStruct(sp.shape, sp.dtype),
        scratch_shapes=[pltpu.VMEM((8,) + sp.shape, sp.dtype), pltpu.SemaphoreType.DMA((7,)), pltpu.SemaphoreType.DMA((7,))],
    )(sp)


def _adamw_update(w, gv, m, v):
    mn = ADAM_B1 * m + (1.0 - ADAM_B1) * gv
    vn = ADAM_B2 * v + (1.0 - ADAM_B2) * (gv * gv)
    m_hat = mn / (1.0 - ADAM_B1 ** ADAM_STEP)
    v_hat = vn / (1.0 - ADAM_B2 ** ADAM_STEP)
    return -ADAM_LR * (m_hat / (jnp.sqrt(v_hat) + ADAM_EPS) + ADAM_WD * w), mn, vn


def _adamw(w, g, m, v, name):
    rows, cols = w.shape

    def body(w_ref, g_ref, m_ref, v_ref, d_ref, mo_ref, vo_ref):
        d_ref[...], mo_ref[...], vo_ref[...] = _adamw_update(w_ref[...], g_ref[...], m_ref[...], v_ref[...])

    if rows % 256 == 0 or cols % 256 != 0:
        tr = _pick(rows, (256, rows))
        grid, spec = (rows // tr,), pl.BlockSpec((tr, cols), lambda r: (r, 0))
    else:
        grid, spec = (cols // 256,), pl.BlockSpec((rows, 256), lambda r: (0, r))
    shp = jax.ShapeDtypeStruct(w.shape, F32)
    return pl.pallas_call(
        body, name=name, grid=grid, in_specs=[spec] * 4, out_specs=[spec] * 3, out_shape=[shp] * 3,
        compiler_params=_cparams(("parallel",)),
    )(w, g, m, v)


MAT_NAMES = ("l0_w_in", "l0_w_q_b", "l0_w_kv_b", "l0_w_out", "l1_w_in", "l1_w_out")
VEC_NAMES = ("l0_pre_g", "l0_post_g", "l0_q_a_g", "l0_kv_a_g", "l1_pre_g", "l1_post_g", "l1_b_f")
WEIGHT_NAMES = ("l0_pre_g", "l0_post_g", "l0_w_in", "l0_q_a_g", "l0_w_q_b", "l0_kv_a_g", "l0_w_kv_b", "l0_w_out",
                "l1_pre_g", "l1_post_g", "l1_w_in", "l1_b_f", "l1_w_out")
MAT_SHARD = {"l0_w_in": (1024, 808), "l0_w_q_b": (384, 192), "l0_w_kv_b": (256, 256), "l0_w_out": (256, 1024),
             "l1_w_in": (1024, 1028), "l1_w_out": (256, 1024)}
ROW_SHARDED = ("l0_w_out", "l1_w_out")
WHOLE_MATS = ("l0_w_in", "l1_w_in")
PACKED_MATS = ("l0_w_q_b", "l0_w_kv_b", "l0_w_out", "l1_w_out")
VEC_LEN = {"l0_pre_g": 1024, "l0_post_g": 1024, "l0_q_a_g": 384, "l0_kv_a_g": 256, "l1_pre_g": 1024,
           "l1_post_g": 1024, "l1_b_f": 16}


def _mat_rows(n):
    r, c = MAT_SHARD[n]
    return r * c // LANES


def _pack_shards(shards):
    return jnp.concatenate([shards[n].reshape(shards[n].shape[:-2] + (_mat_rows(n), LANES)) for n in PACKED_MATS],
                           axis=-2)


def _unpack_shards(pack):
    out, at = {}, 0
    for n in PACKED_MATS:
        out[n] = pack[..., at:at + _mat_rows(n), :].reshape(pack.shape[:-2] + MAT_SHARD[n])
        at += _mat_rows(n)
    return out


def _join_shards(n, s):
    if n in ROW_SHARDED:
        return s.reshape(4 * s.shape[1], s.shape[2])
    return s.transpose(1, 0, 2).reshape(s.shape[1], 4 * s.shape[2])


def _cut_shards(n, w):
    r, c = MAT_SHARD[n]
    if n in ROW_SHARDED:
        return w.reshape(4, r, c)
    return w.reshape(r, 4, c).transpose(1, 0, 2)


def _pack_vecs(vecs):
    parts = []
    for n in VEC_NAMES:
        v = vecs[n].reshape(-1)
        parts.append(jnp.pad(v, (0, VEC_ROWS * LANES - v.shape[0])).reshape(VEC_ROWS, LANES))
    return jnp.concatenate(parts, axis=0)


def _unpack_vecs(pack):
    return {n: pack[k * VEC_ROWS:(k + 1) * VEC_ROWS].reshape(-1)[:VEC_LEN[n]] for k, n in enumerate(VEC_NAMES)}


def kernel(x, positions, l0_pre_g, l0_post_g, l0_w_in, l0_q_a_g, l0_w_q_b, l0_kv_a_g, l0_w_kv_b, l0_w_out, l1_pre_g, l1_post_g, l1_w_in, l1_b_f, l1_w_out, loss_target, m_l0_pre_g, m_l0_post_g, m_l0_w_in, m_l0_q_a_g, m_l0_w_q_b, m_l0_kv_a_g, m_l0_w_kv_b, m_l0_w_out, m_l1_pre_g, m_l1_post_g, m_l1_w_in, m_l1_b_f, m_l1_w_out, v_l0_pre_g, v_l0_post_g, v_l0_w_in, v_l0_q_a_g, v_l0_w_q_b, v_l0_kv_a_g, v_l0_w_kv_b, v_l0_w_out, v_l1_pre_g, v_l1_post_g, v_l1_w_in, v_l1_b_f, v_l1_w_out):
    w = dict(l0_pre_g=l0_pre_g, l0_post_g=l0_post_g, l0_w_in=l0_w_in, l0_q_a_g=l0_q_a_g, l0_w_q_b=l0_w_q_b,
             l0_kv_a_g=l0_kv_a_g, l0_w_kv_b=l0_w_kv_b, l0_w_out=l0_w_out, l1_pre_g=l1_pre_g, l1_post_g=l1_post_g,
             l1_w_in=l1_w_in, l1_b_f=l1_b_f, l1_w_out=l1_w_out)
    m = dict(l0_pre_g=m_l0_pre_g, l0_post_g=m_l0_post_g, l0_w_in=m_l0_w_in, l0_q_a_g=m_l0_q_a_g, l0_w_q_b=m_l0_w_q_b,
             l0_kv_a_g=m_l0_kv_a_g, l0_w_kv_b=m_l0_w_kv_b, l0_w_out=m_l0_w_out, l1_pre_g=m_l1_pre_g,
             l1_post_g=m_l1_post_g, l1_w_in=m_l1_w_in, l1_b_f=m_l1_b_f, l1_w_out=m_l1_w_out)
    v = dict(l0_pre_g=v_l0_pre_g, l0_post_g=v_l0_post_g, l0_w_in=v_l0_w_in, l0_q_a_g=v_l0_q_a_g, l0_w_q_b=v_l0_w_q_b,
             l0_kv_a_g=v_l0_kv_a_g, l0_w_kv_b=v_l0_w_kv_b, l0_w_out=v_l0_w_out, l1_pre_g=v_l1_pre_g,
             l1_post_g=v_l1_post_g, l1_w_in=v_l1_w_in, l1_b_f=v_l1_b_f, l1_w_out=v_l1_w_out)

    cx, cy, cc = _place()
    me1 = jnp.reshape(2 * cx + cy, (1,)).astype(jnp.int32)
    c1 = jnp.reshape(cc, (1,)).astype(jnp.int32)
    w_bf = {n: w[n].astype(BF16) for n in MAT_NAMES}
    mine = [_pack_shards(w_bf)] + [w_bf[n] for n in WHOLE_MATS]
    got = [lax.dynamic_update_slice(g, a[None], (2 * cx + cy, 0, 0)) for g, a in zip(_weight_gather(mine), mine)]
    gathered = dict(_unpack_shards(got[0]), **dict(zip(WHOLE_MATS, got[1:])))
    full = {n: _join_shards(n, gathered[n]) for n in MAT_NAMES}
    gains = {n: w[n].reshape(1, -1) for n in VEC_NAMES}

    lsum, dx0, grads = _local_step(
        x[0], positions[0], loss_target[0], gains, _pad_w0(full["l0_w_in"]), _pad_wq(full["l0_w_q_b"]),
        _pad_wkv(full["l0_w_kv_b"]), full["l0_w_out"], _pad_w1(full["l1_w_in"]), full["l1_w_out"])

    gfull = {"l0_w_in": _unpad_w0(grads["l0_w_in"]), "l0_w_q_b": _unpad_wq(grads["l0_w_q_b"]),
             "l0_w_kv_b": _unpad_wkv(grads["l0_w_kv_b"]), "l0_w_out": grads["l0_w_out"],
             "l1_w_in": grads["l1_w_in"][:, :ODD_IN_WIDTH], "l1_w_out": grads["l1_w_out"]}
    cut = {n: _cut_shards(n, gfull[n]) for n in MAT_NAMES}
    tags = ("packed",) + WHOLE_MATS
    g_parts = [_pack_shards(cut)] + [cut[n] for n in WHOLE_MATS]
    q_cores = [_grad_add_cores(p, t, c1, "grad_add_cores_" + tag)
               for p, t, tag in zip(g_parts, _grad_core_exchange(g_parts), tags)]
    g_mine = [_grad_add_chips(q, s, me1, "grad_add_chips_" + tag)
              for q, s, tag in zip(q_cores, _grad_chip_exchange(q_cores), tags)]
    g_theirs = _grad_core_gather(g_mine)

    small = _small_allreduce(jnp.concatenate([_pack_vecs({n: grads[n] for n in VEC_NAMES}),
                                              lsum.reshape(D_MODEL // LANES, LANES)], axis=0))
    g_small = small[:SMALL_ROWS]
    loss = 0.5 * jnp.sum(small[SMALL_ROWS:]) / float(D_MODEL)

    whole = [jnp.concatenate([lax.select(cc == 0, a, b), lax.select(cc == 0, b, a)], axis=0)
             for a, b in zip(g_mine, g_theirs)]
    g_mats = dict(_unpack_shards(whole[0]), **dict(zip(WHOLE_MATS, whole[1:])))
    d_mats, m_mats, v_mats = {}, {}, {}
    for n in PACKED_MATS:
        d_mats[n], m_mats[n], v_mats[n] = _adamw(w[n], g_mats[n], m[n], v[n], "adamw_" + n)
    for n in WHOLE_MATS:
        gt = g_mats[n].T
        outs = _adamw(w[n].T, gt, m[n].T, v[n].T, "adamw_" + n)
        g_mats[n], d_mats[n], m_mats[n], v_mats[n] = gt.T, outs[0].T, outs[1].T, outs[2].T
    d_small, m_small, v_small = _adamw(_pack_vecs(w), g_small, _pack_vecs(m), _pack_vecs(v), "adamw_vecs")

    def leaves(mats, vec_pack):
        out = dict(mats)
        out.update(_unpack_vecs(vec_pack))
        return [out[n] for n in WEIGHT_NAMES]

    return (loss, dx0[None], *leaves(g_mats, g_small), *leaves(d_mats, d_small), *leaves(m_mats, m_small),
            *leaves(v_mats, v_small))
```

```python
import jax
import jax.numpy as jnp
from jax import lax
from jax.experimental import pallas as pl
from jax.experimental.pallas import tpu as pltpu

F32 = jnp.float32
BF16 = jnp.bfloat16
MESH = pl.DeviceIdType.MESH

D_MODEL = 1024
RMS_EPS = 1e-6
ROPE_THETA = 10000.0
SB_WIDTH = 512
MLA_Q_LORA = 384
MLA_KV_LORA = 256
MLA_ROPE_DIM = 32
MLA_WIDTH = 512
FOX_WIDTH = 1024
FOX_HEADS = 16
EVEN_IN_WIDTH = 3232
ODD_IN_WIDTH = 4112

ADAM_LR = 0.001
ADAM_B1 = 0.9
ADAM_B2 = 0.999
ADAM_EPS = 1e-08
ADAM_WD = 0.01
ADAM_STEP = 10

LANES = 128
VMEM_LIMIT = 56 * 1024 * 1024

L0_PREP = 0
L0_PREP_W = 768
L0_SBG = 768
L0_MLG = 1280
L0_SBQ = 1792
L0_SBK = 2304
L0_SBV = 2816
L0_WIDTH = 3328
L1_Q = 0
L1_K = 1024
L1_V = 2048
L1_G = 3072
L1_F = 4096
L1_WIDTH = 4224

ATT_T = 256
ATT_GROUP = 4
ATT_QSUB = 2
NEG = -1e30

VEC_ROWS = 8
SMALL_ROWS = 7 * VEC_ROWS


def _cparams(sem, **kw):
    return pltpu.CompilerParams(dimension_semantics=sem, vmem_limit_bytes=VMEM_LIMIT, **kw)


def _dot(a, b):
    return lax.dot_general(a, b, (((1,), (0,)), ((), ())), preferred_element_type=F32)


def _dot_nt(a, b):
    return lax.dot_general(a, b, (((1,), (1,)), ((), ())), preferred_element_type=F32)


def _sigmoid(x):
    return 1.0 / (1.0 + jnp.exp(-x))


def _rstd(x):
    return lax.rsqrt(jnp.mean(x * x, axis=-1, keepdims=True) + RMS_EPS)


def _norm_bwd(x, g, dy):
    r = _rstd(x)
    xn = x * r
    dxn = dy * g
    dx = r * (dxn - xn * jnp.mean(dxn * xn, axis=-1, keepdims=True))
    return dx, dy * xn


def _split3(x):
    hi = x.astype(BF16)
    r1 = x - hi.astype(F32)
    mid = r1.astype(BF16)
    lo = (r1 - mid.astype(F32)).astype(BF16)
    return hi, mid, lo


def _wide_tile(n, cap=1792):
    return max(t for t in range(LANES, min(n, cap) + 1, LANES) if n % t == 0)


def _pick(n, cands):
    for c in cands:
        if n % c == 0:
            return c
    raise ValueError(n)


def _norm_matmul(x, g, w, name):
    S, K = x.shape
    N = w.shape[1]
    tm = _pick(S, (512, 256))
    tn = _wide_tile(N)

    def body(x_ref, g_ref, w_ref, o_ref, ht_ref, h_ref):
        @pl.when(pl.program_id(1) == 0)
        def _():
            xv = x_ref[...]
            h = (xv * _rstd(xv)) * g_ref[...]
            h_ref[...] = h.astype(BF16)
            ht_ref[...] = h.T.astype(BF16)
        o_ref[...] = _dot(h_ref[...], w_ref[...])

    return pl.pallas_call(
        body, name=name, grid=(S // tm, N // tn),
        in_specs=[pl.BlockSpec((tm, K), lambda i, j: (i, 0)),
                  pl.BlockSpec((1, K), lambda i, j: (0, 0)),
                  pl.BlockSpec((K, tn), lambda i, j: (0, j))],
        out_specs=[pl.BlockSpec((tm, tn), lambda i, j: (i, j)),
                   pl.BlockSpec((K, tm), lambda i, j: (0, i))],
        out_shape=[jax.ShapeDtypeStruct((S, N), F32), jax.ShapeDtypeStruct((K, S), BF16)],
        scratch_shapes=[pltpu.VMEM((tm, K), BF16)],
        compiler_params=_cparams(("parallel", "arbitrary")),
    )(x, g, w)


def _matmul_t(at, b, name):
    M, S = at.shape
    N = b.shape[1]
    tn = _wide_tile(N)
    ts = _pick(S, (512, 256))

    def body(a_ref, b_ref, o_ref):
        @pl.when(pl.program_id(1) == 0)
        def _():
            o_ref[...] = jnp.zeros_like(o_ref)
        o_ref[...] += _dot(a_ref[...], b_ref[...].astype(BF16))

    return pl.pallas_call(
        body, name=name, grid=(N // tn, S // ts),
        in_specs=[pl.BlockSpec((M, ts), lambda j, k: (0, k)),
                  pl.BlockSpec((ts, tn), lambda j, k: (k, j))],
        out_specs=pl.BlockSpec((M, tn), lambda j, k: (0, j)),
        out_shape=jax.ShapeDtypeStruct((M, N), F32),
        compiler_params=_cparams(("parallel", "arbitrary")),
    )(at, b)


def _matmul_t_many(at, bs, name):
    M, S = at.shape
    ts = _pick(S, (512, 256))
    n = len(bs)

    def body(*refs):
        a_ref, b_refs, o_refs = refs[0], refs[1:1 + n], refs[1 + n:]

        @pl.when(pl.program_id(0) == 0)
        def _():
            for o_ref in o_refs:
                o_ref[...] = jnp.zeros_like(o_ref)

        a = a_ref[...]
        for b_ref, o_ref in zip(b_refs, o_refs):
            o_ref[...] += _dot(a, b_ref[...].astype(BF16))

    return pl.pallas_call(
        body, name=name, grid=(S // ts,),
        in_specs=[pl.BlockSpec((M, ts), lambda k: (0, k))] + [pl.BlockSpec((ts, b.shape[1]), lambda k: (k, 0)) for b in bs],
        out_specs=[pl.BlockSpec((M, b.shape[1]), lambda k: (0, 0)) for b in bs],
        out_shape=[jax.ShapeDtypeStruct((M, b.shape[1]), F32) for b in bs],
        compiler_params=_cparams(("arbitrary",)),
    )(at, *bs)


def _in_proj_bwd(pieces, w, x, g, dx_up, name):
    S, K = x.shape
    N = w.shape[1]
    tm = _pick(S, (256,))
    offs = [off for off, _ in pieces]
    arrs = [a for _, a in pieces]

    def body(*refs):
        d_refs = refs[:len(arrs)]
        w_ref, x_ref, g_ref, u_ref, dx_ref, dg_ref = refs[len(arrs):]

        @pl.when(pl.program_id(0) == 0)
        def _():
            dg_ref[...] = jnp.zeros_like(dg_ref)

        acc = None
        for off, d_ref in zip(offs, d_refs):
            part = _dot_nt(d_ref[...].astype(BF16), w_ref[:, off:off + d_ref.shape[1]])
            acc = part if acc is None else acc + part
        dx, dgrow = _norm_bwd(x_ref[...], g_ref[...], acc)
        dx_ref[...] = u_ref[...] + dx
        dg_ref[...] += jnp.sum(dgrow, axis=0, keepdims=True)

    row = lambda i: (i, 0)
    fixed = lambda i: (0, 0)
    return pl.pallas_call(
        body, name=name, grid=(S // tm,),
        in_specs=[pl.BlockSpec((tm, a.shape[1]), row) for a in arrs] + [
            pl.BlockSpec((K, N), fixed), pl.BlockSpec((tm, K), row), pl.BlockSpec((1, K), fixed),
            pl.BlockSpec((tm, K), row)],
        out_specs=[pl.BlockSpec((tm, K), row), pl.BlockSpec((1, K), fixed)],
        out_shape=[jax.ShapeDtypeStruct((S, K), F32), jax.ShapeDtypeStruct((1, K), F32)],
        compiler_params=_cparams(("arbitrary",)),
    )(*arrs, w, x, g, dx_up)


def _out_proj(og_a, og_b, blk_a, blk_b, w, x, g, target, name):
    S = x.shape[0]
    D = x.shape[1]
    tm = _pick(S, (512, 256))
    with_loss = target is not None

    def body(*refs):
        if with_loss:
            a_ref, b_ref, wa_ref, wb_ref, x_ref, g_ref, t_ref, y_ref, o_ref, l_ref = refs
        else:
            a_ref, b_ref, wa_ref, wb_ref, x_ref, g_ref, y_ref, o_ref = refs
        y = _dot(a_ref[...], wa_ref[...]) + _dot(b_ref[...], wb_ref[...])
        y_ref[...] = y
        xn = x_ref[...] + (y * _rstd(y)) * g_ref[...]
        if with_loss:
            @pl.when(pl.program_id(0) == 0)
            def _():
                l_ref[...] = jnp.zeros_like(l_ref)
            d = xn - t_ref[...]
            o_ref[...] = d / float(D)
            l_ref[...] += jnp.sum(d * d, axis=0, keepdims=True)
        else:
            o_ref[...] = xn

    row = lambda i: (i, 0)
    in_specs = [pl.BlockSpec((tm, 512), lambda i: (i, blk_a)),
                pl.BlockSpec((tm, 512), lambda i: (i, blk_b)),
                pl.BlockSpec((512, D), lambda i: (0, 0)),
                pl.BlockSpec((512, D), lambda i: (1, 0)),
                pl.BlockSpec((tm, D), row),
                pl.BlockSpec((1, D), lambda i: (0, 0))]
    out_specs = [pl.BlockSpec((tm, D), row), pl.BlockSpec((tm, D), row)]
    out_shape = [jax.ShapeDtypeStruct((S, D), F32), jax.ShapeDtypeStruct((S, D), F32)]
    args = [og_a, og_b, w, w, x, g]
    if with_loss:
        in_specs.append(pl.BlockSpec((tm, D), row))
        out_specs.append(pl.BlockSpec((1, D), lambda i: (0, 0)))
        out_shape.append(jax.ShapeDtypeStruct((1, D), F32))
        args.append(target)
    return pl.pallas_call(
        body, name=name, grid=(S // tm,), in_specs=in_specs, out_specs=out_specs, out_shape=out_shape,
        compiler_params=_cparams(("arbitrary",)),
    )(*args)


def _out_proj_bwd(dx_up, y, g, w, proj, gate_offs, o_a, o_b, oblk_a, oblk_b, name):
    S, D = y.shape
    tm = _pick(S, (256,))
    gblk = [off // 256 + c for off in gate_offs for c in range(2)]

    def body(u_ref, y_ref, g_ref, w_ref, g0, g1, g2, g3, oa_ref, ob_ref, dy_ref, do_ref, dgate_ref, dg_ref):
        @pl.when(pl.program_id(0) == 0)
        def _():
            dg_ref[...] = jnp.zeros_like(dg_ref)
        dy, dgrow = _norm_bwd(y_ref[...], g_ref[...], u_ref[...])
        dg_ref[...] += jnp.sum(dgrow, axis=0, keepdims=True)
        dyb = dy.astype(BF16)
        dy_ref[...] = dyb
        dog = _dot_nt(dyb, w_ref[...])
        gates = (g0, g1, g2, g3)
        for c in range(4):
            gt = gates[c][...]
            sg = _sigmoid(gt)
            o_ref = oa_ref if c < 2 else ob_ref
            ov = o_ref[:, (c % 2) * 256:(c % 2 + 1) * 256]
            dc = dog[:, c * 256:(c + 1) * 256]
            do_ref[:, c * 256:(c + 1) * 256] = dc * (gt * sg)
            dgate_ref[:, c * 256:(c + 1) * 256] = dc * ov * (sg * (1.0 + gt * (1.0 - sg)))

    row = lambda i: (i, 0)
    gspec = lambda c: pl.BlockSpec((tm, 256), lambda i: (i, gblk[c]))
    return pl.pallas_call(
        body, name=name, grid=(S // tm,),
        in_specs=[pl.BlockSpec((tm, D), row), pl.BlockSpec((tm, D), row), pl.BlockSpec((1, D), lambda i: (0, 0)),
                  pl.BlockSpec((D, D), lambda i: (0, 0)),
                  gspec(0), gspec(1), gspec(2), gspec(3),
                  pl.BlockSpec((tm, 512), lambda i: (i, oblk_a)),
                  pl.BlockSpec((tm, 512), lambda i: (i, oblk_b))],
        out_specs=[pl.BlockSpec((tm, D), row), pl.BlockSpec((tm, D), row), pl.BlockSpec((tm, D), row),
                   pl.BlockSpec((1, D), lambda i: (0, 0))],
        out_shape=[jax.ShapeDtypeStruct((S, D), BF16), jax.ShapeDtypeStruct((S, D), F32),
                   jax.ShapeDtypeStruct((S, D), F32), jax.ShapeDtypeStruct((1, D), F32)],
        compiler_params=_cparams(("arbitrary",)),
    )(dx_up, y, g, w, proj, proj, proj, proj, o_a, o_b)


def _rope_tables(pos, invf, name):
    S = pos.shape[0]
    tm = _pick(S, (512, 256))

    def body(p_ref, f_ref, c_ref, s1_ref, s2_ref):
        lane = lax.broadcasted_iota(jnp.int32, (1, LANES), 1)
        ang = p_ref[...].astype(F32) * f_ref[...]
        c, s = jnp.cos(ang), jnp.sin(ang)
        c_ref[...] = jnp.where((lane >= 64) & (lane < 96), c, 1.0)
        s1_ref[...] = jnp.where((lane >= 64) & (lane < 80), -s, 0.0)
        s2_ref[...] = jnp.where((lane >= 80) & (lane < 96), s, 0.0)

    spec = pl.BlockSpec((tm, LANES), lambda i: (i, 0))
    return pl.pallas_call(
        body, name=name, grid=(S // tm,),
        in_specs=[pl.BlockSpec((tm, 1), lambda i: (i, 0)), pl.BlockSpec((1, LANES), lambda i: (0, 0))],
        out_specs=[spec, spec, spec],
        out_shape=[jax.ShapeDtypeStruct((S, LANES), F32)] * 3,
        compiler_params=_cparams(("parallel",)),
    )(pos, invf)


def _rope(x, c, s1, s2):
    return x * c + pltpu.roll(x, LANES - 16, 1) * s1 + pltpu.roll(x, 16, 1) * s2


def _rope_t(d, c, s1, s2):
    return d * c + pltpu.roll(d * s1, 16, 1) + pltpu.roll(d * s2, LANES - 16, 1)


def _mla_prep(proj, gq, gkv, wq, wkv, cosT, s1T, s2T, name):
    S = proj.shape[0]
    tm = _pick(S, (256,))

    def body(p_ref, gq_ref, gkv_ref, wq_ref, wkv_ref, c_ref, s1_ref, s2_ref, q_ref, k_ref, v_ref, qn_ref, cn_ref):
        qa = p_ref[:, 0:384]
        ckv = p_ref[:, 384:640]
        kr = p_ref[:, 640:768]
        qn32 = (qa * _rstd(qa)) * gq_ref[...]
        cn32 = (ckv * _rstd(ckv)) * gkv_ref[...]
        qn = qn32.astype(BF16)
        cn = cn32.astype(BF16)
        qn_ref[...] = qn32.T.astype(BF16)
        cn_ref[...] = cn32.T.astype(BF16)
        qb = _dot(qn, wq_ref[...])
        kvb = _dot(cn, wkv_ref[...])
        c, s1, s2 = c_ref[...], s1_ref[...], s2_ref[...]
        krr = _rope(kr, c, s1, s2)
        for h in range(8):
            sl = slice(h * LANES, (h + 1) * LANES)
            q_ref[:, sl] = _rope(qb[:, sl], c, s1, s2)
            k_ref[:, sl] = kvb[:, sl] + krr
        v_ref[...] = kvb[:, 1024:1536]

    row = lambda i: (i, 0)
    fixed = lambda i: (0, 0)
    tspec = pl.BlockSpec((tm, LANES), row)
    return pl.pallas_call(
        body, name=name, grid=(S // tm,),
        in_specs=[pl.BlockSpec((tm, L0_PREP_W), lambda i: (i, L0_PREP // L0_PREP_W)),
                  pl.BlockSpec((1, 384), fixed), pl.BlockSpec((1, 256), fixed),
                  pl.BlockSpec((384, 1024), fixed), pl.BlockSpec((256, 1536), fixed), tspec, tspec, tspec],
        out_specs=[pl.BlockSpec((tm, 1024), row), pl.BlockSpec((tm, 1024), row), pl.BlockSpec((tm, 512), row),
                   pl.BlockSpec((384, tm), lambda i: (0, i)), pl.BlockSpec((256, tm), lambda i: (0, i))],
        out_shape=[jax.ShapeDtypeStruct((S, 1024), F32), jax.ShapeDtypeStruct((S, 1024), F32),
                   jax.ShapeDtypeStruct((S, 512), F32), jax.ShapeDtypeStruct((384, S), BF16),
                   jax.ShapeDtypeStruct((256, S), BF16)],
        compiler_params=_cparams(("parallel",)),
    )(proj, gq, gkv, wq, wkv, cosT, s1T, s2T)


def _mla_prep_bwd(dq, dk, dv, proj, gq, gkv, wq, wkv, cosT, s1T, s2T, name):
    S = proj.shape[0]
    tm = _pick(S, (256,))

    def body(dq_ref, dk_ref, dv_ref, p_ref, gq_ref, gkv_ref, wq_ref, wkv_ref, c_ref, s1_ref, s2_ref,
             dp_ref, dqb_ref, dkvb_ref, dgq_ref, dgkv_ref):
        @pl.when(pl.program_id(0) == 0)
        def _():
            dgq_ref[...] = jnp.zeros_like(dgq_ref)
            dgkv_ref[...] = jnp.zeros_like(dgkv_ref)
        c, s1, s2 = c_ref[...], s1_ref[...], s2_ref[...]
        lane = lax.broadcasted_iota(jnp.int32, (1, LANES), 1)
        dkr = jnp.zeros((tm, LANES), F32)
        for h in range(8):
            sl = slice(h * LANES, (h + 1) * LANES)
            dqb_ref[:, sl] = _rope_t(dq_ref[:, sl], c, s1, s2).astype(BF16)
            dkh = dk_ref[:, sl]
            dkvb_ref[:, sl] = dkh.astype(BF16)
            dkr = dkr + dkh
        dkvb_ref[:, 1024:1536] = dv_ref[...].astype(BF16)
        dkr = jnp.where((lane >= 64) & (lane < 96), _rope_t(dkr, c, s1, s2), 0.0)
        dqn = _dot_nt(dqb_ref[...], wq_ref[...])
        dcn = _dot_nt(dkvb_ref[...], wkv_ref[...])
        dqa, gq_row = _norm_bwd(p_ref[:, 0:384], gq_ref[...], dqn)
        dckv, gkv_row = _norm_bwd(p_ref[:, 384:640], gkv_ref[...], dcn)
        dp_ref[:, 0:384] = dqa
        dp_ref[:, 384:640] = dckv
        dp_ref[:, 640:768] = dkr
        dgq_ref[...] += jnp.sum(gq_row, axis=0, keepdims=True)
        dgkv_ref[...] += jnp.sum(gkv_row, axis=0, keepdims=True)

    row = lambda i: (i, 0)
    fixed = lambda i: (0, 0)
    tspec = pl.BlockSpec((tm, LANES), row)
    return pl.pallas_call(
        body, name=name, grid=(S // tm,),
        in_specs=[pl.BlockSpec((tm, 1024), row), pl.BlockSpec((tm, 1024), row), pl.BlockSpec((tm, 512), row),
                  pl.BlockSpec((tm, L0_PREP_W), lambda i: (i, L0_PREP // L0_PREP_W)),
                  pl.BlockSpec((1, 384), fixed), pl.BlockSpec((1, 256), fixed),
                  pl.BlockSpec((384, 1024), fixed), pl.BlockSpec((256, 1536), fixed), tspec, tspec, tspec],
        out_specs=[pl.BlockSpec((tm, L0_PREP_W), row), pl.BlockSpec((tm, 1024), row), pl.BlockSpec((tm, 1536), row),
                   pl.BlockSpec((1, 384), fixed), pl.BlockSpec((1, 256), fixed)],
        out_shape=[jax.ShapeDtypeStruct((S, L0_PREP_W), F32), jax.ShapeDtypeStruct((S, 1024), BF16),
                   jax.ShapeDtypeStruct((S, 1536), BF16), jax.ShapeDtypeStruct((1, 384), F32),
                   jax.ShapeDtypeStruct((1, 256), F32)],
        compiler_params=_cparams(("arbitrary",)),
    )(dq, dk, dv, proj, gq, gkv, wq, wkv, cosT, s1T, s2T)


def _fox_prep(proj, bf, name):
    S = proj.shape[0]
    tm = _pick(S, (256,))

    def body(f_ref, b_ref, c_ref, carry_ref):
        @pl.when(pl.program_id(0) == 0)
        def _():
            carry_ref[...] = jnp.zeros_like(carry_ref)
        u = f_ref[...] + b_ref[...]
        lf = jnp.minimum(u, 0.0) - jnp.log(1.0 + jnp.exp(-jnp.abs(u)))
        r = lax.broadcasted_iota(jnp.int32, (tm, tm), 0)
        cidx = lax.broadcasted_iota(jnp.int32, (tm, tm), 1)
        tri = (cidx <= r).astype(BF16)
        hi, mid, lo = _split3(lf)
        c = carry_ref[...] + (_dot(tri, hi) + _dot(tri, mid) + _dot(tri, lo))
        c_ref[...] = c
        carry_ref[...] = c[tm - 1:tm, :]

    return pl.pallas_call(
        body, name=name, grid=(S // tm,),
        in_specs=[pl.BlockSpec((tm, LANES), lambda i: (i, L1_F // LANES)), pl.BlockSpec((1, LANES), lambda i: (0, 0))],
        out_specs=pl.BlockSpec((tm, LANES), lambda i: (i, 0)),
        out_shape=jax.ShapeDtypeStruct((S, LANES), F32),
        scratch_shapes=[pltpu.VMEM((1, LANES), F32)],
        compiler_params=_cparams(("arbitrary",)),
    )(proj, bf)


def _fox_prep_bwd(dc, proj, bf, name):
    S = proj.shape[0]
    tm = _pick(S, (256,))
    nb = S // tm

    def body(dc_ref, f_ref, b_ref, df_ref, db_ref, carry_ref):
        @pl.when(pl.program_id(0) == 0)
        def _():
            carry_ref[...] = jnp.zeros_like(carry_ref)
            db_ref[...] = jnp.zeros_like(db_ref)
        r = lax.broadcasted_iota(jnp.int32, (tm, tm), 0)
        cidx = lax.broadcasted_iota(jnp.int32, (tm, tm), 1)
        tri = (cidx >= r).astype(BF16)
        hi, mid, lo = _split3(dc_ref[...])
        dlf = carry_ref[...] + (_dot(tri, hi) + _dot(tri, mid) + _dot(tri, lo))
        carry_ref[...] = dlf[0:1, :]
        u = f_ref[...] + b_ref[...]
        e = jnp.exp(-jnp.abs(u))
        sneg = jnp.where(u >= 0.0, e, 1.0) / (1.0 + e)
        lane = lax.broadcasted_iota(jnp.int32, (1, LANES), 1)
        df = jnp.where(lane < FOX_HEADS, dlf * sneg, 0.0)
        df_ref[...] = df
        db_ref[...] += jnp.sum(df, axis=0, keepdims=True)

    return pl.pallas_call(
        body, name=name, grid=(nb,),
        in_specs=[pl.BlockSpec((tm, LANES), lambda i: (nb - 1 - i, 0)),
                  pl.BlockSpec((tm, LANES), lambda i: (nb - 1 - i, L1_F // LANES)),
                  pl.BlockSpec((1, LANES), lambda i: (0, 0))],
        out_specs=[pl.BlockSpec((tm, LANES), lambda i: (nb - 1 - i, 0)), pl.BlockSpec((1, LANES), lambda i: (0, 0))],
        out_shape=[jax.ShapeDtypeStruct((S, LANES), F32), jax.ShapeDtypeStruct((1, LANES), F32)],
        scratch_shapes=[pltpu.VMEM((1, LANES), F32)],
        compiler_params=_cparams(("arbitrary",)),
    )(dc, proj, bf)


def _att_specs(kind, S, T):
    if kind == "sb":
        qo, ko, vo, go = L0_SBQ // LANES, L0_SBK // LANES, L0_SBV // LANES, L0_SBG // LANES
    elif kind == "fox":
        qo, ko, vo, go = L1_Q // LANES, L1_K // LANES, L1_V // LANES, L1_G // LANES
    else:
        go = L0_MLG // LANES
        return (pl.BlockSpec((T, 256), lambda p, i: (i, p)), pl.BlockSpec((S, 256), lambda p, i: (0, p)),
                pl.BlockSpec((S, LANES), lambda p, i: (0, p)), pl.BlockSpec((T, LANES), lambda p, i: (i, go + p)))
    return (pl.BlockSpec((T, LANES), lambda p, i: (i, qo + p)), pl.BlockSpec((S, LANES), lambda p, i: (0, ko + p)),
            pl.BlockSpec((S, LANES), lambda p, i: (0, vo + p)), pl.BlockSpec((T, LANES), lambda p, i: (i, go + p)))


def _per_q_tile(tile_body, hows):
    T = ATT_T

    def view(ref, u, how):
        if how == "rows":
            return ref.at[pl.ds(u * T, T)]
        if how == "lanes":
            return ref.at[:, pl.ds(u * T, T)]
        if how == "stat":
            return ref.at[:, u]
        return ref

    def body(*refs):
        for u in range(ATT_QSUB):
            tile_body(pl.program_id(1) * ATT_QSUB + u, *[view(r, u, how) for r, how in zip(refs, hows)])

    return body


def _mask_flags(js, masked_at):
    return [t == masked_at for t in range(len(js))]


def _loop_tiles(i, tiles, right_to_left, G=ATT_GROUP):
    ng = i // G
    rest = i - ng * G

    def leftover():
        for r in range(G):
            @pl.when(rest == r)
            def _():
                if right_to_left:
                    tiles([i - u for u in range(r + 1)], 0)
                else:
                    tiles([ng * G + u for u in range(r + 1)], r)

    def group(g, carry):
        if right_to_left:
            tiles([ng * G - 1 - (g * G + u) for u in range(G)], None)
        else:
            tiles([g * G + u for u in range(G)], None)
        return carry

    if right_to_left:
        leftover()
    lax.fori_loop(0, ng, group, 0)
    if not right_to_left:
        leftover()


def _head_q(kind, q_ref, m0, scale):
    if kind == "mla":
        return [q_ref[:, 0:LANES].astype(BF16), q_ref[:, LANES:2 * LANES].astype(BF16)]
    qv = q_ref[...] * scale
    return [jnp.where(m0, qv, 0.0).astype(BF16), jnp.where(m0, 0.0, qv).astype(BF16)]


def _head_k(kind, k_ref, start, T):
    if kind == "mla":
        return [k_ref[pl.ds(start, T), 0:LANES].astype(BF16), k_ref[pl.ds(start, T), LANES:2 * LANES].astype(BF16)]
    kb = k_ref[pl.ds(start, T), :].astype(BF16)
    return [kb, kb]


def _softmax_fwd(kind, qkvg, c_col, S, npairs, name):
    T = ATT_T
    nq = S // T
    fox = kind == "fox"
    scale = (96 if kind == "mla" else 64) ** -0.5

    def body(i, *refs):
        if fox:
            q_ref, k_ref, v_ref, g_ref, cc_ref, o_ref, og_ref, ogt_ref, st_ref, m_ref, acc_ref = refs
        else:
            q_ref, k_ref, v_ref, g_ref, o_ref, og_ref, ogt_ref, st_ref, m_ref, acc_ref = refs
        m0 = lax.broadcasted_iota(jnp.int32, (1, LANES), 1) < 64
        top = lax.broadcasted_iota(jnp.int32, (LANES, 1), 0) < 64
        key = lax.broadcasted_iota(jnp.int32, (T, LANES), 0)
        qrow = lax.broadcasted_iota(jnp.int32, (T, LANES), 1)
        qh = _head_q(kind, q_ref, m0, scale)
        m_ref[...] = jnp.full(m_ref.shape, NEG, F32)
        acc_ref[...] = jnp.zeros(acc_ref.shape, F32)
        chains = [(h, b) for h in range(2) for b in range(T // LANES)]

        def tiles(js, masked_at):
            starts = [pl.multiple_of(j * T, T) for j in js]
            zss = []
            for start in starts:
                kh = _head_k(kind, k_ref, start, T)
                zss.append(_split_blocks([_dot_nt(kh[h], qh[h]) for h in range(2)]))
            pss, alss = [], []
            for start, zs, masked in zip(starts, zss, _mask_flags(js, masked_at)):
                ps, alphas = [], []
                for (h, b), z in zip(chains, zs):
                    lanes = slice(b * LANES, (b + 1) * LANES)
                    if kind == "mla":
                        z = z * scale
                    if fox:
                        z = z - cc_ref[h, pl.ds(start, T), :]
                    if masked:
                        z = jnp.where(key <= qrow + b * LANES, z, NEG)
                    m_prev = m_ref[h, :, lanes]
                    m_new = jnp.maximum(m_prev, jnp.max(z, axis=0, keepdims=True))
                    alphas.append(jnp.exp(m_prev - m_new))
                    ps.append(jnp.exp(z - m_new).astype(BF16))
                    m_ref[h, :, lanes] = m_new
                pss.append(_join_blocks(ps, T // LANES))
                alss.append(_join_blocks(alphas, T // LANES))
            for start, ps, alphas in zip(starts, pss, alss):
                vt = v_ref[pl.ds(start, T), :].T
                vth = [jnp.where(top, vt, 1.0).astype(BF16), jnp.where(top, 1.0, vt).astype(BF16)]
                for h in range(2):
                    acc_ref[h] = alphas[h] * acc_ref[h] + _dot(vth[h], ps[h])

        _loop_tiles(i, tiles, False, 2 * ATT_GROUP)
        acc = [acc_ref[0], acc_ref[1]]
        ot = jnp.concatenate([acc[0][0:64] / acc[0][64:128], acc[1][64:128] / acc[1][0:64]], axis=0)
        o = ot.T
        o_ref[...] = o
        gt = g_ref[...]
        og = o * (gt * _sigmoid(gt))
        og_ref[...] = og.astype(BF16)
        ogt_ref[...] = og.T.astype(BF16)
        st_ref[0] = m_ref[0] + jnp.log(acc[0][64:65])
        st_ref[1] = m_ref[1] + jnp.log(acc[1][0:1])

    QT = ATT_QSUB * T
    qs, ks, vs, gs = _att_specs(kind, S, QT)
    in_specs = [qs, ks, vs, gs]
    args = list(qkvg)
    hows = ["rows", None, None, "rows"]
    if fox:
        in_specs += [pl.BlockSpec((2, S, LANES), lambda p, i: (p, 0, 0))]
        args += [c_col]
        hows += [None]
    hows += ["rows", "rows", "lanes", "stat", None, None]
    W = npairs * LANES
    return pl.pallas_call(
        _per_q_tile(body, hows), name=name, grid=(npairs, nq // ATT_QSUB), in_specs=in_specs,
        out_specs=[pl.BlockSpec((QT, LANES), lambda p, i: (i, p)), pl.BlockSpec((QT, LANES), lambda p, i: (i, p)),
                   pl.BlockSpec((LANES, QT), lambda p, i: (p, i)),
                   pl.BlockSpec((2, ATT_QSUB, 1, T), lambda p, i: (p, i, 0, 0))],
        out_shape=[jax.ShapeDtypeStruct((S, W), F32), jax.ShapeDtypeStruct((S, W), BF16),
                   jax.ShapeDtypeStruct((W, S), BF16),
                   jax.ShapeDtypeStruct((2 * npairs, nq, 1, T), F32)],
        scratch_shapes=[pltpu.VMEM((2, 1, T), F32), pltpu.VMEM((2, LANES, T), F32)],
        compiler_params=_cparams(("parallel", "parallel")),
    )(*args)


def _softplus_parts(z):
    sp = jnp.maximum(z, 0.0) + jnp.log(1.0 + jnp.exp(-jnp.abs(z)))
    return sp, z - sp


def _cumsum_dot(tri2, his, los):
    return _split_blocks([_dot(tri2, jnp.concatenate([hi, lo], axis=0)) for hi, lo in zip(his, los)])


def _split2(x):
    hi = x.astype(BF16)
    return hi, (x - hi.astype(F32)).astype(BF16)


def _split_blocks(per_head):
    return [x[:, b * LANES:(b + 1) * LANES] for x in per_head for b in range(x.shape[1] // LANES)]


def _join_blocks(per_block, nb):
    return [jnp.concatenate(per_block[h * nb:(h + 1) * nb], axis=1) for h in range(len(per_block) // nb)]


def _row_of(col):
    return jnp.broadcast_to(col, (col.shape[0], LANES)).T[0:1]


def _softmax_bwd_t(kind, q, k, v, do, do_off, o, lse, c_col, S, npairs, name):
    T = ATT_T
    nq = S // T
    nb = T // LANES
    fox = kind == "fox"
    mla = kind == "mla"
    scale = (96 if mla else 64) ** -0.5
    kw = 256 if mla else LANES

    def body(i, *refs):
        if fox:
            (q_ref, k_ref, v_ref, do_ref, o_ref, st_ref, cc_ref,
             dq_ref, dk_ref, dv_ref, dck_ref, dcq_ref, dqt_ref, rs_ref, dkx_ref) = refs
        else:
            q_ref, k_ref, v_ref, do_ref, o_ref, st_ref, dq_ref, dk_ref, dv_ref, dqt_ref = refs

        @pl.when(i == 0)
        def _():
            dv_ref[...] = jnp.zeros_like(dv_ref)
            if fox:
                dkx_ref[...] = jnp.zeros_like(dkx_ref)
            else:
                dk_ref[...] = jnp.zeros_like(dk_ref)

        m0 = lax.broadcasted_iota(jnp.int32, (1, LANES), 1) < 64
        top = lax.broadcasted_iota(jnp.int32, (LANES, 1), 0) < 64
        key = lax.broadcasted_iota(jnp.int32, (T, LANES), 0)
        qrow = lax.broadcasted_iota(jnp.int32, (T, LANES), 1)
        qh = _head_q(kind, q_ref, m0, scale)
        if fox:
            qv = q_ref[...] * scale
            qk = [jnp.where(m0, qv, 1.0).astype(BF16), jnp.where(m0, 1.0, qv).astype(BF16)]
        else:
            qk = qh
        dov = do_ref[...]
        prod = dov * o_ref[...]
        dd = [_row_of(jnp.sum(jnp.where(m0, prod, 0.0), axis=1, keepdims=True)),
              _row_of(jnp.sum(jnp.where(m0, 0.0, prod), axis=1, keepdims=True))]
        doh = [jnp.where(m0, dov, 0.0).astype(BF16), jnp.where(m0, 0.0, dov).astype(BF16)]
        lse = [st_ref[0], st_ref[1]]
        dqt_ref[...] = jnp.zeros_like(dqt_ref)
        if fox:
            rs_ref[...] = jnp.zeros_like(rs_ref)
        chains = [(h, b) for h in range(2) for b in range(nb)]

        def tiles(js, masked_at):
            starts = [pl.multiple_of(j * T, T) for j in js]
            zss, dpss = [], []
            for start in starts:
                vb = v_ref[pl.ds(start, T), :].astype(BF16)
                kh = _head_k(kind, k_ref, start, T)
                zss.append(_split_blocks([_dot_nt(kh[h], qh[h]) for h in range(2)]))
                dpss.append(_split_blocks([_dot_nt(vb, doh[h]) for h in range(2)]))
            pss, dsss = [], []
            for start, zs, dps, masked in zip(starts, zss, dpss, _mask_flags(js, masked_at)):
                ps, dss = [], []
                for (h, b), z, dp in zip(chains, zs, dps):
                    lanes = slice(b * LANES, (b + 1) * LANES)
                    if mla:
                        z = z * scale
                    if fox:
                        z = z - cc_ref[h, pl.ds(start, T), :]
                    if masked:
                        z = jnp.where(key <= qrow + b * LANES, z, NEG)
                    p = jnp.exp(z - lse[h][:, lanes])
                    ds = p * (dp - dd[h][:, lanes])
                    dsb = ds.astype(BF16)
                    if fox:
                        rs_ref[h, :, lanes] += jnp.sum(dsb.astype(F32), axis=0, keepdims=True)
                    ps.append(p.astype(BF16))
                    dss.append(dsb)
                pss.append(_join_blocks(ps, nb))
                dsss.append(_join_blocks(dss, nb))
            for start, ps, dss in zip(starts, pss, dsss):
                kt = k_ref[pl.ds(start, T), :].T.astype(BF16)
                dvc = None
                for h in range(2):
                    dkh = _dot(dss[h], qk[h])
                    dvh = _dot(ps[h], doh[h])
                    dvc = dvh if dvc is None else dvc + dvh
                    kth = kt[h * LANES:(h + 1) * LANES] if mla else kt
                    dqt_ref[h] += _dot(kth, dss[h])
                    if fox:
                        dkx_ref[h, pl.ds(start, T), :] += dkh
                    elif mla:
                        dk_ref[pl.ds(start, T), h * LANES:(h + 1) * LANES] += dkh * scale
                    else:
                        dk_ref[pl.ds(start, T), :] += dkh
                dv_ref[pl.ds(start, T), :] += dvc

        _loop_tiles(i, tiles, False)
        if mla:
            dq_ref[:, 0:LANES] = dqt_ref[0].T * scale
            dq_ref[:, LANES:2 * LANES] = dqt_ref[1].T * scale
        else:
            dq_ref[...] = jnp.where(top, dqt_ref[0], dqt_ref[1]).T * scale
        if fox:
            dcq_ref[0] = rs_ref[0]
            dcq_ref[1] = rs_ref[1]

            @pl.when(i == nq - 1)
            def _():
                dk_ref[...] = jnp.where(m0, dkx_ref[0], dkx_ref[1])
                dck_ref[0] = dkx_ref[0].T[64:65]
                dck_ref[1] = dkx_ref[1].T[0:1]

    QT = ATT_QSUB * T
    qs, ks, vs, _ = _att_specs(kind, S, QT)
    stat = pl.BlockSpec((2, ATT_QSUB, 1, T), lambda p, i: (p, i, 0, 0))
    in_specs = [qs, ks, vs,
                pl.BlockSpec((QT, LANES), lambda p, i: (i, do_off + p)),
                pl.BlockSpec((QT, LANES), lambda p, i: (i, p)), stat]
    args = [q, k, v, do, o, lse]
    hows = ["rows", None, None, "rows", "rows", "stat"]
    W = npairs * LANES
    out_specs = [pl.BlockSpec((QT, kw), lambda p, i: (i, p)), pl.BlockSpec((S, kw), lambda p, i: (0, p)),
                 pl.BlockSpec((S, LANES), lambda p, i: (0, p))]
    out_shape = [jax.ShapeDtypeStruct((S, npairs * kw), F32), jax.ShapeDtypeStruct((S, npairs * kw), F32),
                 jax.ShapeDtypeStruct((S, W), F32)]
    scratch = [pltpu.VMEM((2, LANES, T), F32)]
    if fox:
        in_specs.append(pl.BlockSpec((2, S, LANES), lambda p, i: (p, 0, 0)))
        args.append(c_col)
        out_specs += [pl.BlockSpec((2, 1, S), lambda p, i: (p, 0, 0)), stat]
        out_shape += [jax.ShapeDtypeStruct((2 * npairs, 1, S), F32), jax.ShapeDtypeStruct((2 * npairs, nq, 1, T), F32)]
        scratch += [pltpu.VMEM((2, 1, T), F32), pltpu.VMEM((2, S, LANES), F32)]
        hows += [None, "rows", None, None, None, "stat", None, None, None]
    else:
        hows += ["rows", None, None, None]
    return pl.pallas_call(
        _per_q_tile(body, hows), name=name, grid=(npairs, nq // ATT_QSUB), in_specs=in_specs, out_specs=out_specs,
        out_shape=out_shape, scratch_shapes=scratch, compiler_params=_cparams(("parallel", "arbitrary")),
    )(*args)


def _sb_fwd_t(proj, S, npairs, name):
    T = ATT_T
    nq = S // T
    nb = T // LANES
    scale = 64 ** -0.5

    def body(i, q_ref, k_ref, v_ref, g_ref, o_ref, og_ref, ogt_ref, st_ref, rem_ref, acc_ref):
        m0 = lax.broadcasted_iota(jnp.int32, (1, LANES), 1) < 64
        top = lax.broadcasted_iota(jnp.int32, (LANES, 1), 0) < 64
        key = lax.broadcasted_iota(jnp.int32, (T, LANES), 0)
        qrow = lax.broadcasted_iota(jnp.int32, (T, LANES), 1)
        r = lax.broadcasted_iota(jnp.int32, (T, T), 0)
        c = lax.broadcasted_iota(jnp.int32, (T, T), 1)
        after = (c > r).astype(BF16)
        after2 = jnp.concatenate([after, after], axis=1)
        qh = _head_q("sb", q_ref, m0, scale)
        rem_ref[...] = jnp.zeros_like(rem_ref)
        acc_ref[...] = jnp.zeros_like(acc_ref)
        chains = [(h, b) for h in range(2) for b in range(nb)]

        def tiles(js, masked_at):
            zss = []
            for j in js:
                kb = k_ref[pl.ds(pl.multiple_of(j * T, T), T), :].astype(BF16)
                zss.append(_split_blocks([_dot_nt(kb, qh[h]) for h in range(2)]))
            lass, sums, hiss, loss = [], [], [], []
            for zs, masked in zip(zss, _mask_flags(js, masked_at)):
                las, sm, his, los = [], [], [], []
                for (h, b), z in zip(chains, zs):
                    sp, la = _softplus_parts(z)
                    if masked:
                        sp = jnp.where(key < qrow + b * LANES, sp, 0.0)
                    hi, lo = _split2(sp)
                    las.append(la)
                    sm.append(jnp.sum(sp, axis=0, keepdims=True))
                    his.append(hi)
                    los.append(lo)
                lass.append(las)
                sums.append(sm)
                hiss.append(_join_blocks(his, nb))
                loss.append(_join_blocks(los, nb))
            rcss = [_cumsum_dot(after2, his, los) for his, los in zip(hiss, loss)]
            wss = []
            for las, sm, rcs, masked in zip(lass, sums, rcss, _mask_flags(js, masked_at)):
                ws = []
                for (h, b), la, s, rc in zip(chains, las, sm, rcs):
                    lanes = slice(b * LANES, (b + 1) * LANES)
                    w = jnp.exp(la - (rem_ref[h, :, lanes] + rc))
                    if masked:
                        w = jnp.where(key < qrow + b * LANES, w, 0.0)
                    ws.append(w.astype(BF16))
                    rem_ref[h, :, lanes] += s
                wss.append(_join_blocks(ws, nb))
            for j, ws in zip(js, wss):
                vtb = v_ref[pl.ds(pl.multiple_of(j * T, T), T), :].T.astype(BF16)
                for h in range(2):
                    acc_ref[h] += _dot(vtb, ws[h])

        _loop_tiles(i, tiles, True)
        o = jnp.where(top, acc_ref[0], acc_ref[1]).T
        o_ref[...] = o
        gt = g_ref[...]
        og = o * (gt * _sigmoid(gt))
        og_ref[...] = og.astype(BF16)
        ogt_ref[...] = og.T.astype(BF16)
        st_ref[0] = rem_ref[0]
        st_ref[1] = rem_ref[1]

    QT = ATT_QSUB * T
    qs, ks, vs, gs = _att_specs("sb", S, QT)
    W = npairs * LANES
    hows = ["rows", None, None, "rows", "rows", "rows", "lanes", "stat", None, None]
    return pl.pallas_call(
        _per_q_tile(body, hows), name=name, grid=(npairs, nq // ATT_QSUB),
        in_specs=[qs, ks, vs, gs],
        out_specs=[pl.BlockSpec((QT, LANES), lambda p, i: (i, p)), pl.BlockSpec((QT, LANES), lambda p, i: (i, p)),
                   pl.BlockSpec((LANES, QT), lambda p, i: (p, i)),
                   pl.BlockSpec((2, ATT_QSUB, 1, T), lambda p, i: (p, i, 0, 0))],
        out_shape=[jax.ShapeDtypeStruct((S, W), F32), jax.ShapeDtypeStruct((S, W), BF16),
                   jax.ShapeDtypeStruct((W, S), BF16),
                   jax.ShapeDtypeStruct((2 * npairs, nq, 1, T), F32)],
        scratch_shapes=[pltpu.VMEM((2, 1, T), F32), pltpu.VMEM((2, LANES, T), F32)],
        compiler_params=_cparams(("parallel", "parallel")),
    )(proj, proj, proj, proj)


def _sb_bwd_t(proj, do, tot, S, npairs, name):
    T = ATT_T
    nq = S // T
    nb = T // LANES
    scale = 64 ** -0.5

    def body(i, q_ref, k_ref, v_ref, do_ref, st_ref, dq_ref, dk_ref, dv_ref, dqt_ref, pre_ref, gpre_ref):

        @pl.when(i == 0)
        def _():
            dk_ref[...] = jnp.zeros_like(dk_ref)
            dv_ref[...] = jnp.zeros_like(dv_ref)

        m0 = lax.broadcasted_iota(jnp.int32, (1, LANES), 1) < 64
        top = lax.broadcasted_iota(jnp.int32, (LANES, 1), 0) < 64
        key = lax.broadcasted_iota(jnp.int32, (T, LANES), 0)
        qrow = lax.broadcasted_iota(jnp.int32, (T, LANES), 1)
        r = lax.broadcasted_iota(jnp.int32, (T, T), 0)
        c = lax.broadcasted_iota(jnp.int32, (T, T), 1)
        upto = (c <= r).astype(BF16)
        upto2 = jnp.concatenate([upto, upto], axis=1)
        left = (c < r).astype(BF16)
        qh = _head_q("sb", q_ref, m0, scale)
        dov = do_ref[...]
        doh = [jnp.where(m0, dov, 0.0).astype(BF16), jnp.where(m0, 0.0, dov).astype(BF16)]
        tot_h = [st_ref[0], st_ref[1]]
        dqt_ref[...] = jnp.zeros_like(dqt_ref)
        pre_ref[...] = jnp.zeros_like(pre_ref)
        gpre_ref[...] = jnp.zeros_like(gpre_ref)
        chains = [(h, b) for h in range(2) for b in range(nb)]

        def tiles(js, masked_at):
            starts = [pl.multiple_of(j * T, T) for j in js]
            zss, dwss = [], []
            for start in starts:
                vb = v_ref[pl.ds(start, T), :].astype(BF16)
                kb = k_ref[pl.ds(start, T), :].astype(BF16)
                zss.append(_split_blocks([_dot_nt(kb, qh[h]) for h in range(2)]))
                dwss.append(_split_blocks([_dot_nt(vb, doh[h]) for h in range(2)]))
            lass, sums, hiss, loss = [], [], [], []
            for zs, masked in zip(zss, _mask_flags(js, masked_at)):
                las, sm, his, los = [], [], [], []
                for (h, b), z in zip(chains, zs):
                    sp, la = _softplus_parts(z)
                    if masked:
                        sp = jnp.where(key < qrow + b * LANES, sp, 0.0)
                    hi, lo = _split2(sp)
                    las.append(la)
                    sm.append(jnp.sum(sp, axis=0, keepdims=True))
                    his.append(hi)
                    los.append(lo)
                lass.append(las)
                sums.append(sm)
                hiss.append(_join_blocks(his, nb))
                loss.append(_join_blocks(los, nb))
            pcss = [_cumsum_dot(upto2, his, los) for his, los in zip(hiss, loss)]
            wss, gss = [], []
            for las, sm, pcs, dws, masked in zip(lass, sums, pcss, dwss, _mask_flags(js, masked_at)):
                ws, gs = [], []
                for (h, b), la, s, pc, dw in zip(chains, las, sm, pcs, dws):
                    lanes = slice(b * LANES, (b + 1) * LANES)
                    w = jnp.exp(la - ((tot_h[h][:, lanes] - pre_ref[h, :, lanes]) - pc))
                    if masked:
                        w = jnp.where(key < qrow + b * LANES, w, 0.0)
                    ws.append(w.astype(BF16))
                    gs.append(dw * w)
                    pre_ref[h, :, lanes] += s
                wss.append(_join_blocks(ws, nb))
                gss.append(gs)
            gcss = [_split_blocks([_dot(left, g) for g in _join_blocks([g.astype(BF16) for g in gs], nb)]) for gs in gss]
            dzss = []
            for las, gs, gcs, masked in zip(lass, gss, gcss, _mask_flags(js, masked_at)):
                dzs = []
                for (h, b), la, g, gc in zip(chains, las, gs, gcs):
                    lanes = slice(b * LANES, (b + 1) * LANES)
                    dz = g - (g + (gpre_ref[h, :, lanes] + gc)) * jnp.exp(la)
                    if masked:
                        dz = jnp.where(key < qrow + b * LANES, dz, 0.0)
                    dzs.append(dz.astype(BF16))
                    gpre_ref[h, :, lanes] += jnp.sum(g, axis=0, keepdims=True)
                dzss.append(_join_blocks(dzs, nb))
            for start, ws, dzs in zip(starts, wss, dzss):
                kt = k_ref[pl.ds(start, T), :].T.astype(BF16)
                dkc = dvc = None
                for h in range(2):
                    dkh = _dot(dzs[h], qh[h])
                    dvh = _dot(ws[h], doh[h])
                    dkc = dkh if dkc is None else dkc + dkh
                    dvc = dvh if dvc is None else dvc + dvh
                    dqt_ref[h] += _dot(kt, dzs[h])
                dk_ref[pl.ds(start, T), :] += dkc
                dv_ref[pl.ds(start, T), :] += dvc

        _loop_tiles(i, tiles, False)
        dq_ref[...] = jnp.where(top, dqt_ref[0], dqt_ref[1]).T * scale

    QT = ATT_QSUB * T
    qs, ks, vs, _ = _att_specs("sb", S, QT)
    W = npairs * LANES
    hows = ["rows", None, None, "rows", "stat", "rows", None, None, None, None, None]
    return pl.pallas_call(
        _per_q_tile(body, hows), name=name, grid=(npairs, nq // ATT_QSUB),
        in_specs=[qs, ks, vs,
                  pl.BlockSpec((QT, LANES), lambda p, i: (i, p)),
                  pl.BlockSpec((2, ATT_QSUB, 1, T), lambda p, i: (p, i, 0, 0))],
        out_specs=[pl.BlockSpec((QT, LANES), lambda p, i: (i, p)), pl.BlockSpec((S, LANES), lambda p, i: (0, p)),
                   pl.BlockSpec((S, LANES), lambda p, i: (0, p))],
        out_shape=[jax.ShapeDtypeStruct((S, W), F32)] * 3,
        scratch_shapes=[pltpu.VMEM((2, LANES, T), F32), pltpu.VMEM((2, 1, T), F32), pltpu.VMEM((2, 1, T), F32)],
        compiler_params=_cparams(("parallel", "arbitrary")),
    )(proj, proj, proj, do, tot)


def _pad_w0(w):
    z = lambda n: jnp.zeros((w.shape[0], n), w.dtype)
    return jnp.concatenate([w[:, 2048:2432], w[:, 2432:2688], z(64), w[:, 2688:2720], z(32),
                            w[:, 1536:2048], w[:, 2720:3232], w[:, 0:512], w[:, 512:1024], w[:, 1024:1536]], axis=1)


def _unpad_w0(wp):
    return jnp.concatenate([wp[:, L0_SBQ:L0_SBQ + 512], wp[:, L0_SBK:L0_SBK + 512], wp[:, L0_SBV:L0_SBV + 512],
                            wp[:, L0_SBG:L0_SBG + 512], wp[:, 0:384], wp[:, 384:640], wp[:, 704:736],
                            wp[:, L0_MLG:L0_MLG + 512]], axis=1)


def _pad_wq(w):
    return jnp.pad(w.reshape(384, 8, 96), ((0, 0), (0, 0), (0, 32))).reshape(384, 1024)


def _unpad_wq(wp):
    return wp.reshape(384, 8, 128)[:, :, :96].reshape(384, 768)


def _pad_wkv(w):
    w3 = w.reshape(256, 8, 128)
    k = jnp.pad(w3[:, :, :64], ((0, 0), (0, 0), (0, 64))).reshape(256, 1024)
    return jnp.concatenate([k, w3[:, :, 64:].reshape(256, 512)], axis=1)


def _unpad_wkv(wp):
    k = wp[:, :1024].reshape(256, 8, 128)[:, :, :64]
    v = wp[:, 1024:].reshape(256, 8, 64)
    return jnp.concatenate([k, v], axis=-1).reshape(256, 1024)


def _pad_w1(w):
    return jnp.concatenate([w, jnp.zeros((w.shape[0], L1_WIDTH - ODD_IN_WIDTH), w.dtype)], axis=1)


def _local_step(x, positions, target, g, w0p, wqp, wkvp, wo0, w1p, wo1):
    S = x.shape[0]
    nq = S // ATT_T
    invf = ROPE_THETA ** (-jnp.arange(0, MLA_ROPE_DIM, 2, dtype=F32) / MLA_ROPE_DIM)
    invf = jnp.concatenate([jnp.zeros((64,), F32), invf, invf, jnp.zeros((32,), F32)]).reshape(1, LANES)
    cosT, s1T, s2T = _rope_tables(positions.reshape(S, 1), invf, "rope_tables")
    bfp = jnp.pad(g["l1_b_f"], ((0, 0), (0, LANES - FOX_HEADS)))

    proj0, h0t = _norm_matmul(x, g["l0_pre_g"], w0p, "l0_in_proj")
    qm, km, vm, qnt, cnt = _mla_prep(proj0, g["l0_q_a_g"], g["l0_kv_a_g"], wqp, wkvp, cosT, s1T, s2T, "mla_prep")
    o_sb, og_sb, ogt_sb, tot_sb = _sb_fwd_t(proj0, S, 4, "sb_fwd")
    o_ml, og_ml, ogt_ml, lse_ml = _softmax_fwd("mla", (qm, km, vm, proj0), None, S, 4, "mla_fwd")
    y0, x1 = _out_proj(og_sb, og_ml, 0, 0, wo0, x, g["l0_post_g"], None, "l0_out_proj")

    proj1, h1t = _norm_matmul(x1, g["l1_pre_g"], w1p, "l1_in_proj")
    cfx = _fox_prep(proj1, bfp, "fox_prep")
    c16 = cfx[:, :FOX_HEADS].T
    c_col = jnp.broadcast_to(c16[:, :, None], (FOX_HEADS, S, LANES))
    o_fx, og_fx, ogt_fx, lse_fx = _softmax_fwd("fox", (proj1, proj1, proj1, proj1), c_col, S, 8, "fox_fwd")
    y1, dx2, lsum = _out_proj(og_fx, og_fx, 0, 1, wo1, x1, g["l1_post_g"], target, "l1_out_proj")

    dy1, do1, dgate1, d_post1 = _out_proj_bwd(dx2, y1, g["l1_post_g"], wo1, proj1, (L1_G, L1_G + 512), o_fx, o_fx, 0, 1, "l1_out_bwd")
    dwo1 = _matmul_t(ogt_fx, dy1, "l1_dw_out")
    dq1, dk1, dv1, dck, dcq = _softmax_bwd_t("fox", proj1, proj1, proj1, do1, 0, o_fx, lse_fx, c_col, S, 8,
                                             "fox_bwd")
    dc = jnp.pad((dcq.reshape(FOX_HEADS, S) - dck.reshape(FOX_HEADS, S)).T, ((0, 0), (0, LANES - FOX_HEADS)))
    df, d_bf = _fox_prep_bwd(dc, proj1, bfp, "fox_prep_bwd")
    pieces1 = [(L1_Q, dq1), (L1_K, dk1), (L1_V, dv1), (L1_G, dgate1), (L1_F, df)]
    dx1, d_pre1 = _in_proj_bwd(pieces1, w1p, x1, g["l1_pre_g"], dx2, "l1_in_bwd")
    dw1p = jnp.concatenate(_matmul_t_many(h1t, [dq1, dk1], "l1_dw_in_a")
                           + _matmul_t_many(h1t, [dv1, dgate1, df], "l1_dw_in_b"), axis=1)

    dy0, do0, dgate0, d_post0 = _out_proj_bwd(dx1, y0, g["l0_post_g"], wo0, proj0, (L0_SBG, L0_MLG), o_sb, o_ml, 0, 0,
                                              "l0_out_bwd")
    dwo0 = jnp.concatenate([_matmul_t(ogt_sb, dy0, "l0_dw_out_sb"), _matmul_t(ogt_ml, dy0, "l0_dw_out_mla")], axis=0)
    dsq, dsk, dsv = _sb_bwd_t(proj0, do0, tot_sb, S, 4, "sb_bwd")
    dqm, dkm, dvm = _softmax_bwd_t("mla", qm, km, vm, do0, 4, o_ml, lse_ml, None, S, 4, "mla_bwd")
    dprep, dqb, dkvb, d_qag, d_kvag = _mla_prep_bwd(dqm, dkm, dvm, proj0, g["l0_q_a_g"], g["l0_kv_a_g"], wqp, wkvp,
                                                    cosT, s1T, s2T, "mla_prep_bwd")
    dwqp = _matmul_t(qnt, dqb, "l0_dw_qb")
    dwkvp = _matmul_t(cnt, dkvb, "l0_dw_kvb")
    pieces0 = [(L0_PREP, dprep), (L0_SBG, dgate0), (L0_SBQ, dsq), (L0_SBK, dsk), (L0_SBV, dsv)]
    dx0, d_pre0 = _in_proj_bwd(pieces0, w0p, x, g["l0_pre_g"], dx1, "l0_in_bwd")
    dw0p = jnp.concatenate(_matmul_t_many(h0t, [dprep, dgate0], "l0_dw_in_a")
                           + _matmul_t_many(h0t, [dsq, dsk, dsv], "l0_dw_in_b"), axis=1)

    grads = {
        "l0_pre_g": d_pre0, "l0_post_g": d_post0, "l0_w_in": dw0p, "l0_q_a_g": d_qag, "l0_w_q_b": dwqp,
        "l0_kv_a_g": d_kvag, "l0_w_kv_b": dwkvp, "l0_w_out": dwo0, "l1_pre_g": d_pre1, "l1_post_g": d_post1,
        "l1_w_in": dw1p, "l1_b_f": d_bf[:, :FOX_HEADS], "l1_w_out": dwo1,
    }
    return lsum, dx0, grads


_ANY = pl.BlockSpec(memory_space=pl.ANY)


def _place():
    return lax.axis_index("x"), lax.axis_index("y"), lax.axis_index("c")


def _other_chips(x, y):
    return [(1 - x, y), (x, 1 - y), (1 - x, 1 - y)]


def _half(rows, c):
    return pl.ds(c * (rows // 2), rows // 2)


def _weight_gather(parts):
    n = len(parts)

    def body(*refs):
        p_refs, out_refs, send_sems, recv_sems = refs[:n], refs[n:2 * n], refs[2 * n], refs[2 * n + 1]
        x, y, c = _place()
        sibling = (x, y, 1 - c)
        chips = _other_chips(x, y)

        def blk(k, chip, cc):
            return out_refs[k].at[2 * chip[0] + chip[1], _half(p_refs[k].shape[0], cc)]

        def copy(s, src, dst, to):
            return pltpu.make_async_remote_copy(src_ref=src, dst_ref=dst, send_sem=send_sems.at[s],
                                                recv_sem=recv_sems.at[s], device_id=to, device_id_type=MESH)

        first = [copy(6 * k + j, p_refs[k].at[_half(p_refs[k].shape[0], c)], blk(k, (x, y), c), (*chip, c))
                 for j, chip in enumerate(chips) for k in range(n)]
        for cp in first:
            cp.start()
        passed = []
        for j, chip in enumerate(chips):
            for k in range(n):
                copy(6 * k + j, blk(k, chip, c), blk(k, chip, c), (x, y, c)).wait_recv()
                passed.append(copy(6 * k + 3 + j, blk(k, chip, c), blk(k, chip, c), sibling))
                passed[-1].start()
        for j, chip in enumerate(chips):
            for k in range(n):
                copy(6 * k + 3 + j, blk(k, chip, 1 - c), blk(k, chip, 1 - c), (x, y, c)).wait_recv()
        for cp in first + passed:
            cp.wait_send()

    return pl.pallas_call(
        body, name="weight_gather", in_specs=[_ANY] * n, out_specs=[_ANY] * n,
        out_shape=[jax.ShapeDtypeStruct((4,) + a.shape, a.dtype) for a in parts],
        scratch_shapes=[pltpu.SemaphoreType.DMA((6 * n,)), pltpu.SemaphoreType.DMA((6 * n,))],
    )(*parts)


def _grad_core_exchange(ps):
    n = len(ps)

    def body(*refs):
        p_refs, recv_refs, send_sems, recv_sems = refs[:n], refs[n:2 * n], refs[2 * n], refs[2 * n + 1]
        x, y, c = _place()
        give = [pltpu.make_async_remote_copy(src_ref=p_refs[k].at[j, _half(p_refs[k].shape[1], 1 - c)],
                                             dst_ref=recv_refs[k].at[j], send_sem=send_sems.at[4 * k + j],
                                             recv_sem=recv_sems.at[4 * k + j], device_id=(x, y, 1 - c),
                                             device_id_type=MESH) for k in range(n) for j in range(4)]
        for cp in give:
            cp.start()
        for cp in give:
            cp.wait()

    return pl.pallas_call(
        body, name="grad_core_exchange", in_specs=[_ANY] * n, out_specs=[_ANY] * n,
        out_shape=[jax.ShapeDtypeStruct((4, p.shape[1] // 2, p.shape[2]), p.dtype) for p in ps],
        scratch_shapes=[pltpu.SemaphoreType.DMA((4 * n,)), pltpu.SemaphoreType.DMA((4 * n,))],
    )(*ps)


def _grad_rows(rows):
    return _pick(rows, (1296, 512, rows))


def _grad_add_cores(p, theirs, c1, name):
    _, rh, cols = theirs.shape
    tr = _grad_rows(rh)

    def body(c_ref, a_ref, b_ref, o_ref):
        o_ref[...] = (a_ref[...] + b_ref[...]).astype(BF16)

    spec = pl.BlockSpec((None, tr, cols), lambda j, r, c: (j, r, 0))
    grid_spec = pltpu.PrefetchScalarGridSpec(
        num_scalar_prefetch=1, grid=(4, rh // tr),
        in_specs=[pl.BlockSpec((None, None, tr, cols), lambda j, r, c: (j, c[0], r, 0)), spec], out_specs=spec)
    return pl.pallas_call(
        body, name=name, grid_spec=grid_spec, out_shape=jax.ShapeDtypeStruct(theirs.shape, BF16),
        compiler_params=_cparams(("parallel", "parallel")),
    )(c1, p.reshape(4, 2, rh, cols), theirs)


def _grad_chip_exchange(qs):
    n = len(qs)

    def body(*refs):
        q_refs, out_refs, send_sems, recv_sems = refs[:n], refs[n:2 * n], refs[2 * n], refs[2 * n + 1]
        x, y, c = _place()
        me = 2 * x + y
        chips = _other_chips(x, y)
        sends = [pltpu.make_async_remote_copy(src_ref=q_refs[k].at[2 * chip[0] + chip[1]], dst_ref=out_refs[k].at[me],
                                              send_sem=send_sems.at[3 * k + j], recv_sem=recv_sems.at[3 * k + j],
                                              device_id=(*chip, c), device_id_type=MESH)
                 for j, chip in enumerate(chips) for k in range(n)]
        for cp in sends:
            cp.start()
        for j, chip in enumerate(chips):
            for k in range(n):
                slot = out_refs[k].at[2 * chip[0] + chip[1]]
                pltpu.make_async_remote_copy(src_ref=slot, dst_ref=slot, send_sem=send_sems.at[3 * k + j],
                                             recv_sem=recv_sems.at[3 * k + j], device_id=(x, y, c),
                                             device_id_type=MESH).wait_recv()
        for cp in sends:
            cp.wait_send()

    return pl.pallas_call(
        body, name="grad_chip_exchange", in_specs=[_ANY] * n, out_specs=[_ANY] * n,
        out_shape=[jax.ShapeDtypeStruct(q.shape, q.dtype) for q in qs],
        scratch_shapes=[pltpu.SemaphoreType.DMA((3 * n,)), pltpu.SemaphoreType.DMA((3 * n,))],
    )(*qs)


def _grad_add_chips(q, slots, me1, name):
    _, rh, cols = q.shape
    tr = _grad_rows(rh)

    def body(me_ref, own_ref, s0, s1, s2, s3, o_ref):
        me = me_ref[0]
        t = [jnp.where(me == j, own_ref[...], s[...]).astype(F32) for j, s in enumerate((s0, s1, s2, s3))]
        o_ref[...] = ((t[0] + t[1]) + t[2]) + t[3]

    def slot_spec(j):
        return pl.BlockSpec((None, tr, cols), lambda r, me: (jnp.where(me[0] == j, (j + 1) % 4, j), r, 0))

    grid_spec = pltpu.PrefetchScalarGridSpec(
        num_scalar_prefetch=1, grid=(rh // tr,),
        in_specs=[pl.BlockSpec((None, tr, cols), lambda r, me: (me[0], r, 0))] + [slot_spec(j) for j in range(4)],
        out_specs=pl.BlockSpec((tr, cols), lambda r, me: (r, 0)))
    return pl.pallas_call(
        body, name=name, grid_spec=grid_spec, out_shape=jax.ShapeDtypeStruct(q.shape[1:], F32),
        compiler_params=_cparams(("parallel",)),
    )(me1, q, slots, slots, slots, slots)


def _grad_core_gather(ts):
    n = len(ts)

    def body(*refs):
        t_refs, out_refs, send_sems, recv_sems = refs[:n], refs[n:2 * n], refs[2 * n], refs[2 * n + 1]
        x, y, c = _place()
        give = [pltpu.make_async_remote_copy(src_ref=t_refs[k], dst_ref=out_refs[k], send_sem=send_sems.at[k],
                                             recv_sem=recv_sems.at[k], device_id=(x, y, 1 - c), device_id_type=MESH)
                for k in range(n)]
        for cp in give:
            cp.start()
        for cp in give:
            cp.wait()

    return pl.pallas_call(
        body, name="grad_core_gather", in_specs=[_ANY] * n, out_specs=[_ANY] * n,
        out_shape=[jax.ShapeDtypeStruct(t.shape, t.dtype) for t in ts],
        scratch_shapes=[pltpu.SemaphoreType.DMA((n,)), pltpu.SemaphoreType.DMA((n,))],
    )(*ts)


def _small_allreduce(sp):
    def body(sp_ref, out_ref, gath_ref, send_sems, recv_sems):
        x, y, c = _place()
        me = 4 * x + 2 * y + c
        gath_ref[me] = sp_ref[...]
        peers = []
        for k in range(1, 8):
            px = 1 - x if k & 4 else x
            py = 1 - y if k & 2 else y
            pc = 1 - c if k & 1 else c
            peers.append((px, py, pc))
        sends = [pltpu.make_async_remote_copy(src_ref=sp_ref, dst_ref=gath_ref.at[me], send_sem=send_sems.at[k],
                                              recv_sem=recv_sems.at[k], device_id=peer, device_id_type=MESH)
                 for k, peer in enumerate(peers)]
        for cp in sends:
            cp.start()
        for k, (px, py, pc) in enumerate(peers):
            slot = gath_ref.at[4 * px + 2 * py + pc]
            pltpu.make_async_remote_copy(src_ref=slot, dst_ref=slot, send_sem=send_sems.at[k], recv_sem=recv_sems.at[k],
                                         device_id=(x, y, c), device_id_type=MESH).wait_recv()
        for cp in sends:
            cp.wait_send()
        tot = gath_ref[0]
        for d in range(1, 8):
            tot = tot + gath_ref[d]
        out_ref[...] = tot

    vm = pl.BlockSpec(memory_space=pltpu.VMEM)
    return pl.pallas_call(
        body, name="small_allreduce", in_specs=[vm], out_specs=vm, out_shape=jax.ShapeDtypeStruct(sp.shape, sp.dtype),
        scratch_shapes=[pltpu.VMEM((8,) + sp.shape, sp.dtype), pltpu.SemaphoreType.DMA((7,)), pltpu.SemaphoreType.DMA((7,))],
    )(sp)


def _adamw_update(w, gv, m, v):
    mn = ADAM_B1 * m + (1.0 - ADAM_B1) * gv
    vn = ADAM_B2 * v + (1.0 - ADAM_B2) * (gv * gv)
    m_hat = mn / (1.0 - ADAM_B1 ** ADAM_STEP)
    v_hat = vn / (1.0 - ADAM_B2 ** ADAM_STEP)
    return -ADAM_LR * (m_hat / (jnp.sqrt(v_hat) + ADAM_EPS) + ADAM_WD * w), mn, vn


def _adamw(w, g, m, v, name):
    rows, cols = w.shape

    def body(w_ref, g_ref, m_ref, v_ref, d_ref, mo_ref, vo_ref):
        d_ref[...], mo_ref[...], vo_ref[...] = _adamw_update(w_ref[...], g_ref[...], m_ref[...], v_ref[...])

    if rows % 256 == 0 or cols % 256 != 0:
        tr = _pick(rows, (256, rows))
        grid, spec = (rows // tr,), pl.BlockSpec((tr, cols), lambda r: (r, 0))
    else:
        grid, spec = (cols // 256,), pl.BlockSpec((rows, 256), lambda r: (0, r))
    shp = jax.ShapeDtypeStruct(w.shape, F32)
    return pl.pallas_call(
        body, name=name, grid=grid, in_specs=[spec] * 4, out_specs=[spec] * 3, out_shape=[shp] * 3,
        compiler_params=_cparams(("parallel",)),
    )(w, g, m, v)


MAT_NAMES = ("l0_w_in", "l0_w_q_b", "l0_w_kv_b", "l0_w_out", "l1_w_in", "l1_w_out")
VEC_NAMES = ("l0_pre_g", "l0_post_g", "l0_q_a_g", "l0_kv_a_g", "l1_pre_g", "l1_post_g", "l1_b_f")
WEIGHT_NAMES = ("l0_pre_g", "l0_post_g", "l0_w_in", "l0_q_a_g", "l0_w_q_b", "l0_kv_a_g", "l0_w_kv_b", "l0_w_out",
                "l1_pre_g", "l1_post_g", "l1_w_in", "l1_b_f", "l1_w_out")
MAT_SHARD = {"l0_w_in": (1024, 808), "l0_w_q_b": (384, 192), "l0_w_kv_b": (256, 256), "l0_w_out": (256, 1024),
             "l1_w_in": (1024, 1028), "l1_w_out": (256, 1024)}
ROW_SHARDED = ("l0_w_out", "l1_w_out")
WHOLE_MATS = ("l0_w_in", "l1_w_in")
PACKED_MATS = ("l0_w_q_b", "l0_w_kv_b", "l0_w_out", "l1_w_out")
VEC_LEN = {"l0_pre_g": 1024, "l0_post_g": 1024, "l0_q_a_g": 384, "l0_kv_a_g": 256, "l1_pre_g": 1024,
           "l1_post_g": 1024, "l1_b_f": 16}


def _mat_rows(n):
    r, c = MAT_SHARD[n]
    return r * c // LANES


def _pack_shards(shards):
    return jnp.concatenate([shards[n].reshape(shards[n].shape[:-2] + (_mat_rows(n), LANES)) for n in PACKED_MATS],
                           axis=-2)


def _unpack_shards(pack):
    out, at = {}, 0
    for n in PACKED_MATS:
        out[n] = pack[..., at:at + _mat_rows(n), :].reshape(pack.shape[:-2] + MAT_SHARD[n])
        at += _mat_rows(n)
    return out


def _join_shards(n, s):
    if n in ROW_SHARDED:
        return s.reshape(4 * s.shape[1], s.shape[2])
    return s.transpose(1, 0, 2).reshape(s.shape[1], 4 * s.shape[2])


def _cut_shards(n, w):
    r, c = MAT_SHARD[n]
    if n in ROW_SHARDED:
        return w.reshape(4, r, c)
    return w.reshape(r, 4, c).transpose(1, 0, 2)


def _pack_vecs(vecs):
    parts = []
    for n in VEC_NAMES:
        v = vecs[n].reshape(-1)
        parts.append(jnp.pad(v, (0, VEC_ROWS * LANES - v.shape[0])).reshape(VEC_ROWS, LANES))
    return jnp.concatenate(parts, axis=0)


def _unpack_vecs(pack):
    return {n: pack[k * VEC_ROWS:(k + 1) * VEC_ROWS].reshape(-1)[:VEC_LEN[n]] for k, n in enumerate(VEC_NAMES)}


def kernel(x, positions, l0_pre_g, l0_post_g, l0_w_in, l0_q_a_g, l0_w_q_b, l0_kv_a_g, l0_w_kv_b, l0_w_out, l1_pre_g, l1_post_g, l1_w_in, l1_b_f, l1_w_out, loss_target, m_l0_pre_g, m_l0_post_g, m_l0_w_in, m_l0_q_a_g, m_l0_w_q_b, m_l0_kv_a_g, m_l0_w_kv_b, m_l0_w_out, m_l1_pre_g, m_l1_post_g, m_l1_w_in, m_l1_b_f, m_l1_w_out, v_l0_pre_g, v_l0_post_g, v_l0_w_in, v_l0_q_a_g, v_l0_w_q_b, v_l0_kv_a_g, v_l0_w_kv_b, v_l0_w_out, v_l1_pre_g, v_l1_post_g, v_l1_w_in, v_l1_b_f, v_l1_w_out):
    w = dict(l0_pre_g=l0_pre_g, l0_post_g=l0_post_g, l0_w_in=l0_w_in, l0_q_a_g=l0_q_a_g, l0_w_q_b=l0_w_q_b,
             l0_kv_a_g=l0_kv_a_g, l0_w_kv_b=l0_w_kv_b, l0_w_out=l0_w_out, l1_pre_g=l1_pre_g, l1_post_g=l1_post_g,
             l1_w_in=l1_w_in, l1_b_f=l1_b_f, l1_w_out=l1_w_out)
    m = dict(l0_pre_g=m_l0_pre_g, l0_post_g=m_l0_post_g, l0_w_in=m_l0_w_in, l0_q_a_g=m_l0_q_a_g, l0_w_q_b=m_l0_w_q_b,
             l0_kv_a_g=m_l0_kv_a_g, l0_w_kv_b=m_l0_w_kv_b, l0_w_out=m_l0_w_out, l1_pre_g=m_l1_pre_g,
             l1_post_g=m_l1_post_g, l1_w_in=m_l1_w_in, l1_b_f=m_l1_b_f, l1_w_out=m_l1_w_out)
    v = dict(l0_pre_g=v_l0_pre_g, l0_post_g=v_l0_post_g, l0_w_in=v_l0_w_in, l0_q_a_g=v_l0_q_a_g, l0_w_q_b=v_l0_w_q_b,
             l0_kv_a_g=v_l0_kv_a_g, l0_w_kv_b=v_l0_w_kv_b, l0_w_out=v_l0_w_out, l1_pre_g=v_l1_pre_g,
             l1_post_g=v_l1_post_g, l1_w_in=v_l1_w_in, l1_b_f=v_l1_b_f, l1_w_out=v_l1_w_out)

    cx, cy, cc = _place()
    me1 = jnp.reshape(2 * cx + cy, (1,)).astype(jnp.int32)
    c1 = jnp.reshape(cc, (1,)).astype(jnp.int32)
    w_bf = {n: w[n].astype(BF16) for n in MAT_NAMES}
    mine = [_pack_shards(w_bf)] + [w_bf[n] for n in WHOLE_MATS]
    got = [lax.dynamic_update_slice(g, a[None], (2 * cx + cy, 0, 0)) for g, a in zip(_weight_gather(mine), mine)]
    gathered = dict(_unpack_shards(got[0]), **dict(zip(WHOLE_MATS, got[1:])))
    full = {n: _join_shards(n, gathered[n]) for n in MAT_NAMES}
    gains = {n: w[n].reshape(1, -1) for n in VEC_NAMES}

    lsum, dx0, grads = _local_step(
        x[0], positions[0], loss_target[0], gains, _pad_w0(full["l0_w_in"]), _pad_wq(full["l0_w_q_b"]),
        _pad_wkv(full["l0_w_kv_b"]), full["l0_w_out"], _pad_w1(full["l1_w_in"]), full["l1_w_out"])

    gfull = {"l0_w_in": _unpad_w0(grads["l0_w_in"]), "l0_w_q_b": _unpad_wq(grads["l0_w_q_b"]),
             "l0_w_kv_b": _unpad_wkv(grads["l0_w_kv_b"]), "l0_w_out": grads["l0_w_out"],
             "l1_w_in": grads["l1_w_in"][:, :ODD_IN_WIDTH], "l1_w_out": grads["l1_w_out"]}
    cut = {n: _cut_shards(n, gfull[n]) for n in MAT_NAMES}
    tags = ("packed",) + WHOLE_MATS
    g_parts = [_pack_shards(cut)] + [cut[n] for n in WHOLE_MATS]
    q_cores = [_grad_add_cores(p, t, c1, "grad_add_cores_" + tag)
               for p, t, tag in zip(g_parts, _grad_core_exchange(g_parts), tags)]
    g_mine = [_grad_add_chips(q, s, me1, "grad_add_chips_" + tag)
              for q, s, tag in zip(q_cores, _grad_chip_exchange(q_cores), tags)]
    g_theirs = _grad_core_gather(g_mine)

    small = _small_allreduce(jnp.concatenate([_pack_vecs({n: grads[n] for n in VEC_NAMES}),
                                              lsum.reshape(D_MODEL // LANES, LANES)], axis=0))
    g_small = small[:SMALL_ROWS]
    loss = 0.5 * jnp.sum(small[SMALL_ROWS:]) / float(D_MODEL)

    whole = [jnp.concatenate([lax.select(cc == 0, a, b), lax.select(cc == 0, b, a)], axis=0)
             for a, b in zip(g_mine, g_theirs)]
    g_mats = dict(_unpack_shards(whole[0]), **dict(zip(WHOLE_MATS, whole[1:])))
    d_mats, m_mats, v_mats = {}, {}, {}
    for n in PACKED_MATS:
        d_mats[n], m_mats[n], v_mats[n] = _adamw(w[n], g_mats[n], m[n], v[n], "adamw_" + n)
    for n in WHOLE_MATS:
        gt = g_mats[n].T
        outs = _adamw(w[n].T, gt, m[n].T, v[n].T, "adamw_" + n)
        g_mats[n], d_mats[n], m_mats[n], v_mats[n] = gt.T, outs[0].T, outs[1].T, outs[2].T
    d_small, m_small, v_small = _adamw(_pack_vecs(w), g_small, _pack_vecs(m), _pack_vecs(v), "adamw_vecs")

    def leaves(mats, vec_pack):
        out = dict(mats)
        out.update(_unpack_vecs(vec_pack))
        return [out[n] for n in WEIGHT_NAMES]

    return (loss, dx0[None], *leaves(g_mats, g_small), *leaves(d_mats, d_small), *leaves(m_mats, m_small),
            *leaves(v_mats, v_small))
```

```python
import jax
import jax.numpy as jnp
from jax import lax
from jax.experimental import pallas as pl
from jax.experimental.pallas import tpu as pltpu

F32 = jnp.float32
BF16 = jnp.bfloat16
MESH = pl.DeviceIdType.MESH

D_MODEL = 1024
RMS_EPS = 1e-6
ROPE_THETA = 10000.0
SB_WIDTH = 512
MLA_Q_LORA = 384
MLA_KV_LORA = 256
MLA_ROPE_DIM = 32
MLA_WIDTH = 512
FOX_WIDTH = 1024
FOX_HEADS = 16
EVEN_IN_WIDTH = 3232
ODD_IN_WIDTH = 4112

ADAM_LR = 0.001
ADAM_B1 = 0.9
ADAM_B2 = 0.999
ADAM_EPS = 1e-08
ADAM_WD = 0.01
ADAM_STEP = 10

LANES = 128
VMEM_LIMIT = 56 * 1024 * 1024

L0_PREP = 0
L0_PREP_W = 768
L0_SBG = 768
L0_MLG = 1280
L0_SBQ = 1792
L0_SBK = 2304
L0_SBV = 2816
L0_WIDTH = 3328
L1_Q = 0
L1_K = 1024
L1_V = 2048
L1_G = 3072
L1_F = 4096
L1_WIDTH = 4224

ATT_T = 256
ATT_GROUP = 4
ATT_QSUB = 2
NEG = -1e30

VEC_ROWS = 8
SMALL_ROWS = 7 * VEC_ROWS


def _cparams(sem, **kw):
    return pltpu.CompilerParams(dimension_semantics=sem, vmem_limit_bytes=VMEM_LIMIT, **kw)


def _dot(a, b):
    return lax.dot_general(a, b, (((1,), (0,)), ((), ())), preferred_element_type=F32)


def _dot_nt(a, b):
    return lax.dot_general(a, b, (((1,), (1,)), ((), ())), preferred_element_type=F32)


def _sigmoid(x):
    return 1.0 / (1.0 + jnp.exp(-x))


def _rstd(x):
    return lax.rsqrt(jnp.mean(x * x, axis=-1, keepdims=True) + RMS_EPS)


def _norm_bwd(x, g, dy):
    r = _rstd(x)
    xn = x * r
    dxn = dy * g
    dx = r * (dxn - xn * jnp.mean(dxn * xn, axis=-1, keepdims=True))
    return dx, dy * xn


def _split3(x):
    hi = x.astype(BF16)
    r1 = x - hi.astype(F32)
    mid = r1.astype(BF16)
    lo = (r1 - mid.astype(F32)).astype(BF16)
    return hi, mid, lo


def _wide_tile(n, cap=1792):
    return max(t for t in range(LANES, min(n, cap) + 1, LANES) if n % t == 0)


def _pick(n, cands):
    for c in cands:
        if n % c == 0:
            return c
    raise ValueError(n)


def _norm_matmul(x, g, w, name):
    S, K = x.shape
    N = w.shape[1]
    tm = _pick(S, (1024, 512, 256))
    tn = _wide_tile(N)

    def body(x_ref, g_ref, w_ref, o_ref, ht_ref, h_ref):
        @pl.when(pl.program_id(1) == 0)
        def _():
            xv = x_ref[...]
            h = (xv * _rstd(xv)) * g_ref[...]
            h_ref[...] = h.astype(BF16)
            ht_ref[...] = h.T.astype(BF16)
        o_ref[...] = _dot(h_ref[...], w_ref[...])

    return pl.pallas_call(
        body, name=name, grid=(S // tm, N // tn),
        in_specs=[pl.BlockSpec((tm, K), lambda i, j: (i, 0)),
                  pl.BlockSpec((1, K), lambda i, j: (0, 0)),
                  pl.BlockSpec((K, tn), lambda i, j: (0, j))],
        out_specs=[pl.BlockSpec((tm, tn), lambda i, j: (i, j)),
                   pl.BlockSpec((K, tm), lambda i, j: (0, i))],
        out_shape=[jax.ShapeDtypeStruct((S, N), F32), jax.ShapeDtypeStruct((K, S), BF16)],
        scratch_shapes=[pltpu.VMEM((tm, K), BF16)],
        compiler_params=_cparams(("parallel", "arbitrary")),
    )(x, g, w)


def _matmul_t(at, b, name):
    M, S = at.shape
    N = b.shape[1]
    tn = _wide_tile(N)
    ts = _pick(S, (512, 256))

    def body(a_ref, b_ref, o_ref):
        @pl.when(pl.program_id(1) == 0)
        def _():
            o_ref[...] = jnp.zeros_like(o_ref)
        o_ref[...] += _dot(a_ref[...], b_ref[...].astype(BF16))

    return pl.pallas_call(
        body, name=name, grid=(N // tn, S // ts),
        in_specs=[pl.BlockSpec((M, ts), lambda j, k: (0, k)),
                  pl.BlockSpec((ts, tn), lambda j, k: (k, j))],
        out_specs=pl.BlockSpec((M, tn), lambda j, k: (0, j)),
        out_shape=jax.ShapeDtypeStruct((M, N), F32),
        compiler_params=_cparams(("parallel", "arbitrary")),
    )(at, b)


def _matmul_t_many(at, bs, name):
    M, S = at.shape
    ts = _pick(S, (512, 256))
    n = len(bs)

    def body(*refs):
        a_ref, b_refs, o_refs = refs[0], refs[1:1 + n], refs[1 + n:]

        @pl.when(pl.program_id(0) == 0)
        def _():
            for o_ref in o_refs:
                o_ref[...] = jnp.zeros_like(o_ref)

        a = a_ref[...]
        for b_ref, o_ref in zip(b_refs, o_refs):
            o_ref[...] += _dot(a, b_ref[...].astype(BF16))

    return pl.pallas_call(
        body, name=name, grid=(S // ts,),
        in_specs=[pl.BlockSpec((M, ts), lambda k: (0, k))] + [pl.BlockSpec((ts, b.shape[1]), lambda k: (k, 0)) for b in bs],
        out_specs=[pl.BlockSpec((M, b.shape[1]), lambda k: (0, 0)) for b in bs],
        out_shape=[jax.ShapeDtypeStruct((M, b.shape[1]), F32) for b in bs],
        compiler_params=_cparams(("arbitrary",)),
    )(at, *bs)


def _in_proj_bwd(pieces, w, x, g, dx_up, name):
    S, K = x.shape
    N = w.shape[1]
    tm = _pick(S, (256,))
    offs = [off for off, _ in pieces]
    arrs = [a for _, a in pieces]

    def body(*refs):
        d_refs = refs[:len(arrs)]
        w_ref, x_ref, g_ref, u_ref, dx_ref, dg_ref = refs[len(arrs):]

        @pl.when(pl.program_id(0) == 0)
        def _():
            dg_ref[...] = jnp.zeros_like(dg_ref)

        acc = None
        for off, d_ref in zip(offs, d_refs):
            part = _dot_nt(d_ref[...].astype(BF16), w_ref[:, off:off + d_ref.shape[1]])
            acc = part if acc is None else acc + part
        dx, dgrow = _norm_bwd(x_ref[...], g_ref[...], acc)
        dx_ref[...] = u_ref[...] + dx
        dg_ref[...] += jnp.sum(dgrow, axis=0, keepdims=True)

    row = lambda i: (i, 0)
    fixed = lambda i: (0, 0)
    return pl.pallas_call(
        body, name=name, grid=(S // tm,),
        in_specs=[pl.BlockSpec((tm, a.shape[1]), row) for a in arrs] + [
            pl.BlockSpec((K, N), fixed), pl.BlockSpec((tm, K), row), pl.BlockSpec((1, K), fixed),
            pl.BlockSpec((tm, K), row)],
        out_specs=[pl.BlockSpec((tm, K), row), pl.BlockSpec((1, K), fixed)],
        out_shape=[jax.ShapeDtypeStruct((S, K), F32), jax.ShapeDtypeStruct((1, K), F32)],
        compiler_params=_cparams(("arbitrary",)),
    )(*arrs, w, x, g, dx_up)


def _out_proj(og_a, og_b, blk_a, blk_b, w, x, g, target, name):
    S = x.shape[0]
    D = x.shape[1]
    tm = _pick(S, (512, 256))
    with_loss = target is not None

    def body(*refs):
        if with_loss:
            a_ref, b_ref, wa_ref, wb_ref, x_ref, g_ref, t_ref, y_ref, o_ref, l_ref = refs
        else:
            a_ref, b_ref, wa_ref, wb_ref, x_ref, g_ref, y_ref, o_ref = refs
        y = _dot(a_ref[...], wa_ref[...]) + _dot(b_ref[...], wb_ref[...])
        y_ref[...] = y
        xn = x_ref[...] + (y * _rstd(y)) * g_ref[...]
        if with_loss:
            @pl.when(pl.program_id(0) == 0)
            def _():
                l_ref[...] = jnp.zeros_like(l_ref)
            d = xn - t_ref[...]
            o_ref[...] = d / float(D)
            l_ref[...] += jnp.sum(d * d, axis=0, keepdims=True)
        else:
            o_ref[...] = xn

    row = lambda i: (i, 0)
    in_specs = [pl.BlockSpec((tm, 512), lambda i: (i, blk_a)),
                pl.BlockSpec((tm, 512), lambda i: (i, blk_b)),
                pl.BlockSpec((512, D), lambda i: (0, 0)),
                pl.BlockSpec((512, D), lambda i: (1, 0)),
                pl.BlockSpec((tm, D), row),
                pl.BlockSpec((1, D), lambda i: (0, 0))]
    out_specs = [pl.BlockSpec((tm, D), row), pl.BlockSpec((tm, D), row)]
    out_shape = [jax.ShapeDtypeStruct((S, D), F32), jax.ShapeDtypeStruct((S, D), F32)]
    args = [og_a, og_b, w, w, x, g]
    if with_loss:
        in_specs.append(pl.BlockSpec((tm, D), row))
        out_specs.append(pl.BlockSpec((1, D), lambda i: (0, 0)))
        out_shape.append(jax.ShapeDtypeStruct((1, D), F32))
        args.append(target)
    return pl.pallas_call(
        body, name=name, grid=(S // tm,), in_specs=in_specs, out_specs=out_specs, out_shape=out_shape,
        compiler_params=_cparams(("arbitrary",)),
    )(*args)


def _out_proj_bwd(dx_up, y, g, w, proj, gate_offs, o_a, o_b, oblk_a, oblk_b, name):
    S, D = y.shape
    tm = _pick(S, (256,))
    gblk = [off // 256 + c for off in gate_offs for c in range(2)]

    def body(u_ref, y_ref, g_ref, w_ref, g0, g1, g2, g3, oa_ref, ob_ref, dy_ref, do_ref, dgate_ref, dg_ref):
        @pl.when(pl.program_id(0) == 0)
        def _():
            dg_ref[...] = jnp.zeros_like(dg_ref)
        dy, dgrow = _norm_bwd(y_ref[...], g_ref[...], u_ref[...])
        dg_ref[...] += jnp.sum(dgrow, axis=0, keepdims=True)
        dyb = dy.astype(BF16)
        dy_ref[...] = dyb
        dog = _dot_nt(dyb, w_ref[...])
        gates = (g0, g1, g2, g3)
        for c in range(4):
            gt = gates[c][...]
            sg = _sigmoid(gt)
            o_ref = oa_ref if c < 2 else ob_ref
            ov = o_ref[:, (c % 2) * 256:(c % 2 + 1) * 256]
            dc = dog[:, c * 256:(c + 1) * 256]
            do_ref[:, c * 256:(c + 1) * 256] = dc * (gt * sg)
            dgate_ref[:, c * 256:(c + 1) * 256] = dc * ov * (sg * (1.0 + gt * (1.0 - sg)))

    row = lambda i: (i, 0)
    gspec = lambda c: pl.BlockSpec((tm, 256), lambda i: (i, gblk[c]))
    return pl.pallas_call(
        body, name=name, grid=(S // tm,),
        in_specs=[pl.BlockSpec((tm, D), row), pl.BlockSpec((tm, D), row), pl.BlockSpec((1, D), lambda i: (0, 0)),
                  pl.BlockSpec((D, D), lambda i: (0, 0)),
                  gspec(0), gspec(1), gspec(2), gspec(3),
                  pl.BlockSpec((tm, 512), lambda i: (i, oblk_a)),
                  pl.BlockSpec((tm, 512), lambda i: (i, oblk_b))],
        out_specs=[pl.BlockSpec((tm, D), row), pl.BlockSpec((tm, D), row), pl.BlockSpec((tm, D), row),
                   pl.BlockSpec((1, D), lambda i: (0, 0))],
        out_shape=[jax.ShapeDtypeStruct((S, D), BF16), jax.ShapeDtypeStruct((S, D), F32),
                   jax.ShapeDtypeStruct((S, D), F32), jax.ShapeDtypeStruct((1, D), F32)],
        compiler_params=_cparams(("arbitrary",)),
    )(dx_up, y, g, w, proj, proj, proj, proj, o_a, o_b)


def _rope_tables(pos, invf, name):
    S = pos.shape[0]
    tm = _pick(S, (512, 256))

    def body(p_ref, f_ref, c_ref, s1_ref, s2_ref):
        lane = lax.broadcasted_iota(jnp.int32, (1, LANES), 1)
        ang = p_ref[...].astype(F32) * f_ref[...]
        c, s = jnp.cos(ang), jnp.sin(ang)
        c_ref[...] = jnp.where((lane >= 64) & (lane < 96), c, 1.0)
        s1_ref[...] = jnp.where((lane >= 64) & (lane < 80), -s, 0.0)
        s2_ref[...] = jnp.where((lane >= 80) & (lane < 96), s, 0.0)

    spec = pl.BlockSpec((tm, LANES), lambda i: (i, 0))
    return pl.pallas_call(
        body, name=name, grid=(S // tm,),
        in_specs=[pl.BlockSpec((tm, 1), lambda i: (i, 0)), pl.BlockSpec((1, LANES), lambda i: (0, 0))],
        out_specs=[spec, spec, spec],
        out_shape=[jax.ShapeDtypeStruct((S, LANES), F32)] * 3,
        compiler_params=_cparams(("parallel",)),
    )(pos, invf)


def _rope(x, c, s1, s2):
    return x * c + pltpu.roll(x, LANES - 16, 1) * s1 + pltpu.roll(x, 16, 1) * s2


def _rope_t(d, c, s1, s2):
    return d * c + pltpu.roll(d * s1, 16, 1) + pltpu.roll(d * s2, LANES - 16, 1)


def _mla_prep(proj, gq, gkv, wq, wkv, cosT, s1T, s2T, name):
    S = proj.shape[0]
    tm = _pick(S, (256,))

    def body(p_ref, gq_ref, gkv_ref, wq_ref, wkv_ref, c_ref, s1_ref, s2_ref, q_ref, k_ref, v_ref, qn_ref, cn_ref):
        qa = p_ref[:, 0:384]
        ckv = p_ref[:, 384:640]
        kr = p_ref[:, 640:768]
        qn32 = (qa * _rstd(qa)) * gq_ref[...]
        cn32 = (ckv * _rstd(ckv)) * gkv_ref[...]
        qn = qn32.astype(BF16)
        cn = cn32.astype(BF16)
        qn_ref[...] = qn32.T.astype(BF16)
        cn_ref[...] = cn32.T.astype(BF16)
        qb = _dot(qn, wq_ref[...])
        kvb = _dot(cn, wkv_ref[...])
        c, s1, s2 = c_ref[...], s1_ref[...], s2_ref[...]
        krr = _rope(kr, c, s1, s2)
        for h in range(8):
            sl = slice(h * LANES, (h + 1) * LANES)
            q_ref[:, sl] = _rope(qb[:, sl], c, s1, s2)
            k_ref[:, sl] = kvb[:, sl] + krr
        v_ref[...] = kvb[:, 1024:1536]

    row = lambda i: (i, 0)
    fixed = lambda i: (0, 0)
    tspec = pl.BlockSpec((tm, LANES), row)
    return pl.pallas_call(
        body, name=name, grid=(S // tm,),
        in_specs=[pl.BlockSpec((tm, L0_PREP_W), lambda i: (i, L0_PREP // L0_PREP_W)),
                  pl.BlockSpec((1, 384), fixed), pl.BlockSpec((1, 256), fixed),
                  pl.BlockSpec((384, 1024), fixed), pl.BlockSpec((256, 1536), fixed), tspec, tspec, tspec],
        out_specs=[pl.BlockSpec((tm, 1024), row), pl.BlockSpec((tm, 1024), row), pl.BlockSpec((tm, 512), row),
                   pl.BlockSpec((384, tm), lambda i: (0, i)), pl.BlockSpec((256, tm), lambda i: (0, i))],
        out_shape=[jax.ShapeDtypeStruct((S, 1024), F32), jax.ShapeDtypeStruct((S, 1024), F32),
                   jax.ShapeDtypeStruct((S, 512), F32), jax.ShapeDtypeStruct((384, S), BF16),
                   jax.ShapeDtypeStruct((256, S), BF16)],
        compiler_params=_cparams(("parallel",)),
    )(proj, gq, gkv, wq, wkv, cosT, s1T, s2T)


def _mla_prep_bwd(dq, dk, dv, proj, gq, gkv, wq, wkv, cosT, s1T, s2T, name):
    S = proj.shape[0]
    tm = _pick(S, (256,))

    def body(dq_ref, dk_ref, dv_ref, p_ref, gq_ref, gkv_ref, wq_ref, wkv_ref, c_ref, s1_ref, s2_ref,
             dp_ref, dqb_ref, dkvb_ref, dgq_ref, dgkv_ref):
        @pl.when(pl.program_id(0) == 0)
        def _():
            dgq_ref[...] = jnp.zeros_like(dgq_ref)
            dgkv_ref[...] = jnp.zeros_like(dgkv_ref)
        c, s1, s2 = c_ref[...], s1_ref[...], s2_ref[...]
        lane = lax.broadcasted_iota(jnp.int32, (1, LANES), 1)
        dkr = jnp.zeros((tm, LANES), F32)
        for h in range(8):
            sl = slice(h * LANES, (h + 1) * LANES)
            dqb_ref[:, sl] = _rope_t(dq_ref[:, sl], c, s1, s2).astype(BF16)
            dkh = dk_ref[:, sl]
            dkvb_ref[:, sl] = dkh.astype(BF16)
            dkr = dkr + dkh
        dkvb_ref[:, 1024:1536] = dv_ref[...].astype(BF16)
        dkr = jnp.where((lane >= 64) & (lane < 96), _rope_t(dkr, c, s1, s2), 0.0)
        dqn = _dot_nt(dqb_ref[...], wq_ref[...])
        dcn = _dot_nt(dkvb_ref[...], wkv_ref[...])
        dqa, gq_row = _norm_bwd(p_ref[:, 0:384], gq_ref[...], dqn)
        dckv, gkv_row = _norm_bwd(p_ref[:, 384:640], gkv_ref[...], dcn)
        dp_ref[:, 0:384] = dqa
        dp_ref[:, 384:640] = dckv
        dp_ref[:, 640:768] = dkr
        dgq_ref[...] += jnp.sum(gq_row, axis=0, keepdims=True)
        dgkv_ref[...] += jnp.sum(gkv_row, axis=0, keepdims=True)

    row = lambda i: (i, 0)
    fixed = lambda i: (0, 0)
    tspec = pl.BlockSpec((tm, LANES), row)
    return pl.pallas_call(
        body, name=name, grid=(S // tm,),
        in_specs=[pl.BlockSpec((tm, 1024), row), pl.BlockSpec((tm, 1024), row), pl.BlockSpec((tm, 512), row),
                  pl.BlockSpec((tm, L0_PREP_W), lambda i: (i, L0_PREP // L0_PREP_W)),
                  pl.BlockSpec((1, 384), fixed), pl.BlockSpec((1, 256), fixed),
                  pl.BlockSpec((384, 1024), fixed), pl.BlockSpec((256, 1536), fixed), tspec, tspec, tspec],
        out_specs=[pl.BlockSpec((tm, L0_PREP_W), row), pl.BlockSpec((tm, 1024), row), pl.BlockSpec((tm, 1536), row),
                   pl.BlockSpec((1, 384), fixed), pl.BlockSpec((1, 256), fixed)],
        out_shape=[jax.ShapeDtypeStruct((S, L0_PREP_W), F32), jax.ShapeDtypeStruct((S, 1024), BF16),
                   jax.ShapeDtypeStruct((S, 1536), BF16), jax.ShapeDtypeStruct((1, 384), F32),
                   jax.ShapeDtypeStruct((1, 256), F32)],
        compiler_params=_cparams(("arbitrary",)),
    )(dq, dk, dv, proj, gq, gkv, wq, wkv, cosT, s1T, s2T)


def _fox_prep(proj, bf, name):
    S = proj.shape[0]
    tm = _pick(S, (256,))

    def body(f_ref, b_ref, c_ref, carry_ref):
        @pl.when(pl.program_id(0) == 0)
        def _():
            carry_ref[...] = jnp.zeros_like(carry_ref)
        u = f_ref[...] + b_ref[...]
        lf = jnp.minimum(u, 0.0) - jnp.log(1.0 + jnp.exp(-jnp.abs(u)))
        r = lax.broadcasted_iota(jnp.int32, (tm, tm), 0)
        cidx = lax.broadcasted_iota(jnp.int32, (tm, tm), 1)
        tri = (cidx <= r).astype(BF16)
        hi, mid, lo = _split3(lf)
        c = carry_ref[...] + (_dot(tri, hi) + _dot(tri, mid) + _dot(tri, lo))
        c_ref[...] = c
        carry_ref[...] = c[tm - 1:tm, :]

    return pl.pallas_call(
        body, name=name, grid=(S // tm,),
        in_specs=[pl.BlockSpec((tm, LANES), lambda i: (i, L1_F // LANES)), pl.BlockSpec((1, LANES), lambda i: (0, 0))],
        out_specs=pl.BlockSpec((tm, LANES), lambda i: (i, 0)),
        out_shape=jax.ShapeDtypeStruct((S, LANES), F32),
        scratch_shapes=[pltpu.VMEM((1, LANES), F32)],
        compiler_params=_cparams(("arbitrary",)),
    )(proj, bf)


def _fox_prep_bwd(dc, proj, bf, name):
    S = proj.shape[0]
    tm = _pick(S, (256,))
    nb = S // tm

    def body(dc_ref, f_ref, b_ref, df_ref, db_ref, carry_ref):
        @pl.when(pl.program_id(0) == 0)
        def _():
            carry_ref[...] = jnp.zeros_like(carry_ref)
            db_ref[...] = jnp.zeros_like(db_ref)
        r = lax.broadcasted_iota(jnp.int32, (tm, tm), 0)
        cidx = lax.broadcasted_iota(jnp.int32, (tm, tm), 1)
        tri = (cidx >= r).astype(BF16)
        hi, mid, lo = _split3(dc_ref[...])
        dlf = carry_ref[...] + (_dot(tri, hi) + _dot(tri, mid) + _dot(tri, lo))
        carry_ref[...] = dlf[0:1, :]
        u = f_ref[...] + b_ref[...]
        e = jnp.exp(-jnp.abs(u))
        sneg = jnp.where(u >= 0.0, e, 1.0) / (1.0 + e)
        lane = lax.broadcasted_iota(jnp.int32, (1, LANES), 1)
        df = jnp.where(lane < FOX_HEADS, dlf * sneg, 0.0)
        df_ref[...] = df
        db_ref[...] += jnp.sum(df, axis=0, keepdims=True)

    return pl.pallas_call(
        body, name=name, grid=(nb,),
        in_specs=[pl.BlockSpec((tm, LANES), lambda i: (nb - 1 - i, 0)),
                  pl.BlockSpec((tm, LANES), lambda i: (nb - 1 - i, L1_F // LANES)),
                  pl.BlockSpec((1, LANES), lambda i: (0, 0))],
        out_specs=[pl.BlockSpec((tm, LANES), lambda i: (nb - 1 - i, 0)), pl.BlockSpec((1, LANES), lambda i: (0, 0))],
        out_shape=[jax.ShapeDtypeStruct((S, LANES), F32), jax.ShapeDtypeStruct((1, LANES), F32)],
        scratch_shapes=[pltpu.VMEM((1, LANES), F32)],
        compiler_params=_cparams(("arbitrary",)),
    )(dc, proj, bf)


def _att_specs(kind, S, T):
    if kind == "sb":
        qo, ko, vo, go = L0_SBQ // LANES, L0_SBK // LANES, L0_SBV // LANES, L0_SBG // LANES
    elif kind == "fox":
        qo, ko, vo, go = L1_Q // LANES, L1_K // LANES, L1_V // LANES, L1_G // LANES
    else:
        go = L0_MLG // LANES
        return (pl.BlockSpec((T, 256), lambda p, i: (i, p)), pl.BlockSpec((S, 256), lambda p, i: (0, p)),
                pl.BlockSpec((S, LANES), lambda p, i: (0, p)), pl.BlockSpec((T, LANES), lambda p, i: (i, go + p)))
    return (pl.BlockSpec((T, LANES), lambda p, i: (i, qo + p)), pl.BlockSpec((S, LANES), lambda p, i: (0, ko + p)),
            pl.BlockSpec((S, LANES), lambda p, i: (0, vo + p)), pl.BlockSpec((T, LANES), lambda p, i: (i, go + p)))


def _per_q_tile(tile_body, hows):
    T = ATT_T

    def view(ref, u, how):
        if how == "rows":
            return ref.at[pl.ds(u * T, T)]
        if how == "lanes":
            return ref.at[:, pl.ds(u * T, T)]
        if how == "stat":
            return ref.at[:, u]
        return ref

    def body(*refs):
        for u in range(ATT_QSUB):
            tile_body(pl.program_id(1) * ATT_QSUB + u, *[view(r, u, how) for r, how in zip(refs, hows)])

    return body


def _mask_flags(js, masked_at):
    return [t == masked_at for t in range(len(js))]


def _loop_tiles(i, tiles, right_to_left, G=ATT_GROUP):
    ng = i // G
    rest = i - ng * G

    def leftover():
        for r in range(G):
            @pl.when(rest == r)
            def _():
                if right_to_left:
                    tiles([i - u for u in range(r + 1)], 0)
                else:
                    tiles([ng * G + u for u in range(r + 1)], r)

    def group(g, carry):
        if right_to_left:
            tiles([ng * G - 1 - (g * G + u) for u in range(G)], None)
        else:
            tiles([g * G + u for u in range(G)], None)
        return carry

    if right_to_left:
        leftover()
    lax.fori_loop(0, ng, group, 0)
    if not right_to_left:
        leftover()


def _head_q(kind, q_ref, m0, scale):
    if kind == "mla":
        return [q_ref[:, 0:LANES].astype(BF16), q_ref[:, LANES:2 * LANES].astype(BF16)]
    qv = q_ref[...] * scale
    return [jnp.where(m0, qv, 0.0).astype(BF16), jnp.where(m0, 0.0, qv).astype(BF16)]


def _head_k(kind, k_ref, start, T):
    if kind == "mla":
        return [k_ref[pl.ds(start, T), 0:LANES].astype(BF16), k_ref[pl.ds(start, T), LANES:2 * LANES].astype(BF16)]
    kb = k_ref[pl.ds(start, T), :].astype(BF16)
    return [kb, kb]


def _softmax_fwd(kind, qkvg, c_col, S, npairs, name):
    T = ATT_T
    nq = S // T
    fox = kind == "fox"
    scale = (96 if kind == "mla" else 64) ** -0.5

    def body(i, *refs):
        if fox:
            q_ref, k_ref, v_ref, g_ref, cc_ref, o_ref, og_ref, ogt_ref, st_ref, m_ref, acc_ref = refs
        else:
            q_ref, k_ref, v_ref, g_ref, o_ref, og_ref, ogt_ref, st_ref, m_ref, acc_ref = refs
        m0 = lax.broadcasted_iota(jnp.int32, (1, LANES), 1) < 64
        top = lax.broadcasted_iota(jnp.int32, (LANES, 1), 0) < 64
        key = lax.broadcasted_iota(jnp.int32, (T, LANES), 0)
        qrow = lax.broadcasted_iota(jnp.int32, (T, LANES), 1)
        qh = _head_q(kind, q_ref, m0, scale)
        m_ref[...] = jnp.full(m_ref.shape, NEG, F32)
        acc_ref[...] = jnp.zeros(acc_ref.shape, F32)
        chains = [(h, b) for h in range(2) for b in range(T // LANES)]

        def tiles(js, masked_at):
            starts = [pl.multiple_of(j * T, T) for j in js]
            zss = []
            for start in starts:
                kh = _head_k(kind, k_ref, start, T)
                zss.append(_split_blocks([_dot_nt(kh[h], qh[h]) for h in range(2)]))
            pss, alss = [], []
            for start, zs, masked in zip(starts, zss, _mask_flags(js, masked_at)):
                ps, alphas = [], []
                for (h, b), z in zip(chains, zs):
                    lanes = slice(b * LANES, (b + 1) * LANES)
                    if kind == "mla":
                        z = z * scale
                    if fox:
                        z = z - cc_ref[h, pl.ds(start, T), :]
                    if masked:
                        z = jnp.where(key <= qrow + b * LANES, z, NEG)
                    m_prev = m_ref[h, :, lanes]
                    m_new = jnp.maximum(m_prev, jnp.max(z, axis=0, keepdims=True))
                    alphas.append(jnp.exp(m_prev - m_new))
                    ps.append(jnp.exp(z - m_new).astype(BF16))
                    m_ref[h, :, lanes] = m_new
                pss.append(_join_blocks(ps, T // LANES))
                alss.append(_join_blocks(alphas, T // LANES))
            for start, ps, alphas in zip(starts, pss, alss):
                vt = v_ref[pl.ds(start, T), :].T
                vth = [jnp.where(top, vt, 1.0).astype(BF16), jnp.where(top, 1.0, vt).astype(BF16)]
                for h in range(2):
                    acc_ref[h] = alphas[h] * acc_ref[h] + _dot(vth[h], ps[h])

        _loop_tiles(i, tiles, False, 2 * ATT_GROUP)
        acc = [acc_ref[0], acc_ref[1]]
        ot = jnp.concatenate([acc[0][0:64] / acc[0][64:128], acc[1][64:128] / acc[1][0:64]], axis=0)
        o = ot.T
        o_ref[...] = o
        gt = g_ref[...]
        og = o * (gt * _sigmoid(gt))
        og_ref[...] = og.astype(BF16)
        ogt_ref[...] = og.T.astype(BF16)
        st_ref[0] = m_ref[0] + jnp.log(acc[0][64:65])
        st_ref[1] = m_ref[1] + jnp.log(acc[1][0:1])

    QT = ATT_QSUB * T
    qs, ks, vs, gs = _att_specs(kind, S, QT)
    in_specs = [qs, ks, vs, gs]
    args = list(qkvg)
    hows = ["rows", None, None, "rows"]
    if fox:
        in_specs += [pl.BlockSpec((2, S, LANES), lambda p, i: (p, 0, 0))]
        args += [c_col]
        hows += [None]
    hows += ["rows", "rows", "lanes", "stat", None, None]
    W = npairs * LANES
    return pl.pallas_call(
        _per_q_tile(body, hows), name=name, grid=(npairs, nq // ATT_QSUB), in_specs=in_specs,
        out_specs=[pl.BlockSpec((QT, LANES), lambda p, i: (i, p)), pl.BlockSpec((QT, LANES), lambda p, i: (i, p)),
                   pl.BlockSpec((LANES, QT), lambda p, i: (p, i)),
                   pl.BlockSpec((2, ATT_QSUB, 1, T), lambda p, i: (p, i, 0, 0))],
        out_shape=[jax.ShapeDtypeStruct((S, W), F32), jax.ShapeDtypeStruct((S, W), BF16),
                   jax.ShapeDtypeStruct((W, S), BF16),
                   jax.ShapeDtypeStruct((2 * npairs, nq, 1, T), F32)],
        scratch_shapes=[pltpu.VMEM((2, 1, T), F32), pltpu.VMEM((2, LANES, T), F32)],
        compiler_params=_cparams(("parallel", "parallel")),
    )(*args)


def _softplus_parts(z):
    sp = jnp.maximum(z, 0.0) + jnp.log(1.0 + jnp.exp(-jnp.abs(z)))
    return sp, z - sp


def _cumsum_dot(tri2, his, los):
    return _split_blocks([_dot(tri2, jnp.concatenate([hi, lo], axis=0)) for hi, lo in zip(his, los)])


def _split2(x):
    hi = x.astype(BF16)
    return hi, (x - hi.astype(F32)).astype(BF16)


def _split_blocks(per_head):
    return [x[:, b * LANES:(b + 1) * LANES] for x in per_head for b in range(x.shape[1] // LANES)]


def _join_blocks(per_block, nb):
    return [jnp.concatenate(per_block[h * nb:(h + 1) * nb], axis=1) for h in range(len(per_block) // nb)]


def _row_of(col):
    return jnp.broadcast_to(col, (col.shape[0], LANES)).T[0:1]


def _softmax_bwd_t(kind, q, k, v, do, do_off, o, lse, c_col, S, npairs, name):
    T = ATT_T
    nq = S // T
    nb = T // LANES
    fox = kind == "fox"
    mla = kind == "mla"
    scale = (96 if mla else 64) ** -0.5
    kw = 256 if mla else LANES

    def body(i, *refs):
        if fox:
            (q_ref, k_ref, v_ref, do_ref, o_ref, st_ref, cc_ref,
             dq_ref, dk_ref, dv_ref, dck_ref, dcq_ref, dqt_ref, rs_ref, dkx_ref) = refs
        else:
            q_ref, k_ref, v_ref, do_ref, o_ref, st_ref, dq_ref, dk_ref, dv_ref, dqt_ref = refs

        @pl.when(i == 0)
        def _():
            dv_ref[...] = jnp.zeros_like(dv_ref)
            if fox:
                dkx_ref[...] = jnp.zeros_like(dkx_ref)
            else:
                dk_ref[...] = jnp.zeros_like(dk_ref)

        m0 = lax.broadcasted_iota(jnp.int32, (1, LANES), 1) < 64
        top = lax.broadcasted_iota(jnp.int32, (LANES, 1), 0) < 64
        key = lax.broadcasted_iota(jnp.int32, (T, LANES), 0)
        qrow = lax.broadcasted_iota(jnp.int32, (T, LANES), 1)
        qh = _head_q(kind, q_ref, m0, scale)
        if fox:
            qv = q_ref[...] * scale
            qk = [jnp.where(m0, qv, 1.0).astype(BF16), jnp.where(m0, 1.0, qv).astype(BF16)]
        else:
            qk = qh
        dov = do_ref[...]
        prod = dov * o_ref[...]
        dd = [_row_of(jnp.sum(jnp.where(m0, prod, 0.0), axis=1, keepdims=True)),
              _row_of(jnp.sum(jnp.where(m0, 0.0, prod), axis=1, keepdims=True))]
        doh = [jnp.where(m0, dov, 0.0).astype(BF16), jnp.where(m0, 0.0, dov).astype(BF16)]
        lse = [st_ref[0], st_ref[1]]
        dqt_ref[...] = jnp.zeros_like(dqt_ref)
        if fox:
            rs_ref[...] = jnp.zeros_like(rs_ref)
        chains = [(h, b) for h in range(2) for b in range(nb)]

        def tiles(js, masked_at):
            starts = [pl.multiple_of(j * T, T) for j in js]
            zss, dpss = [], []
            for start in starts:
                vb = v_ref[pl.ds(start, T), :].astype(BF16)
                kh = _head_k(kind, k_ref, start, T)
                zss.append(_split_blocks([_dot_nt(kh[h], qh[h]) for h in range(2)]))
                dpss.append(_split_blocks([_dot_nt(vb, doh[h]) for h in range(2)]))
            pss, dsss = [], []
            for start, zs, dps, masked in zip(starts, zss, dpss, _mask_flags(js, masked_at)):
                ps, dss = [], []
                for (h, b), z, dp in zip(chains, zs, dps):
                    lanes = slice(b * LANES, (b + 1) * LANES)
                    if mla:
                        z = z * scale
                    if fox:
                        z = z - cc_ref[h, pl.ds(start, T), :]
                    if masked:
                        z = jnp.where(key <= qrow + b * LANES, z, NEG)
                    p = jnp.exp(z - lse[h][:, lanes])
                    ds = p * (dp - dd[h][:, lanes])
                    dsb = ds.astype(BF16)
                    if fox:
                        rs_ref[h, :, lanes] += jnp.sum(dsb.astype(F32), axis=0, keepdims=True)
                    ps.append(p.astype(BF16))
                    dss.append(dsb)
                pss.append(_join_blocks(ps, nb))
                dsss.append(_join_blocks(dss, nb))
            for start, ps, dss in zip(starts, pss, dsss):
                kt = k_ref[pl.ds(start, T), :].T.astype(BF16)
                dvc = None
                for h in range(2):
                    dkh = _dot(dss[h], qk[h])
                    dvh = _dot(ps[h], doh[h])
                    dvc = dvh if dvc is None else dvc + dvh
                    kth = kt[h * LANES:(h + 1) * LANES] if mla else kt
                    dqt_ref[h] += _dot(kth, dss[h])
                    if fox:
                        dkx_ref[h, pl.ds(start, T), :] += dkh
                    elif mla:
                        dk_ref[pl.ds(start, T), h * LANES:(h + 1) * LANES] += dkh * scale
                    else:
                        dk_ref[pl.ds(start, T), :] += dkh
                dv_ref[pl.ds(start, T), :] += dvc

        _loop_tiles(i, tiles, False)
        if mla:
            dq_ref[:, 0:LANES] = dqt_ref[0].T * scale
            dq_ref[:, LANES:2 * LANES] = dqt_ref[1].T * scale
        else:
            dq_ref[...] = jnp.where(top, dqt_ref[0], dqt_ref[1]).T * scale
        if fox:
            dcq_ref[0] = rs_ref[0]
            dcq_ref[1] = rs_ref[1]

            @pl.when(i == nq - 1)
            def _():
                dk_ref[...] = jnp.where(m0, dkx_ref[0], dkx_ref[1])
                dck_ref[0] = dkx_ref[0].T[64:65]
                dck_ref[1] = dkx_ref[1].T[0:1]

    QT = ATT_QSUB * T
    qs, ks, vs, _ = _att_specs(kind, S, QT)
    stat = pl.BlockSpec((2, ATT_QSUB, 1, T), lambda p, i: (p, i, 0, 0))
    in_specs = [qs, ks, vs,
                pl.BlockSpec((QT, LANES), lambda p, i: (i, do_off + p)),
                pl.BlockSpec((QT, LANES), lambda p, i: (i, p)), stat]
    args = [q, k, v, do, o, lse]
    hows = ["rows", None, None, "rows", "rows", "stat"]
    W = npairs * LANES
    out_specs = [pl.BlockSpec((QT, kw), lambda p, i: (i, p)), pl.BlockSpec((S, kw), lambda p, i: (0, p)),
                 pl.BlockSpec((S, LANES), lambda p, i: (0, p))]
    out_shape = [jax.ShapeDtypeStruct((S, npairs * kw), F32), jax.ShapeDtypeStruct((S, npairs * kw), F32),
                 jax.ShapeDtypeStruct((S, W), F32)]
    scratch = [pltpu.VMEM((2, LANES, T), F32)]
    if fox:
        in_specs.append(pl.BlockSpec((2, S, LANES), lambda p, i: (p, 0, 0)))
        args.append(c_col)
        out_specs += [pl.BlockSpec((2, 1, S), lambda p, i: (p, 0, 0)), stat]
        out_shape += [jax.ShapeDtypeStruct((2 * npairs, 1, S), F32), jax.ShapeDtypeStruct((2 * npairs, nq, 1, T), F32)]
        scratch += [pltpu.VMEM((2, 1, T), F32), pltpu.VMEM((2, S, LANES), F32)]
        hows += [None, "rows", None, None, None, "stat", None, None, None]
    else:
        hows += ["rows", None, None, None]
    return pl.pallas_call(
        _per_q_tile(body, hows), name=name, grid=(npairs, nq // ATT_QSUB), in_specs=in_specs, out_specs=out_specs,
        out_shape=out_shape, scratch_shapes=scratch, compiler_params=_cparams(("parallel", "arbitrary")),
    )(*args)


def _sb_fwd_t(proj, S, npairs, name):
    T = ATT_T
    nq = S // T
    nb = T // LANES
    scale = 64 ** -0.5

    def body(i, q_ref, k_ref, v_ref, g_ref, o_ref, og_ref, ogt_ref, st_ref, rem_ref, acc_ref):
        m0 = lax.broadcasted_iota(jnp.int32, (1, LANES), 1) < 64
        top = lax.broadcasted_iota(jnp.int32, (LANES, 1), 0) < 64
        key = lax.broadcasted_iota(jnp.int32, (T, LANES), 0)
        qrow = lax.broadcasted_iota(jnp.int32, (T, LANES), 1)
        r = lax.broadcasted_iota(jnp.int32, (T, T), 0)
        c = lax.broadcasted_iota(jnp.int32, (T, T), 1)
        after = (c > r).astype(BF16)
        after2 = jnp.concatenate([after, after], axis=1)
        qh = _head_q("sb", q_ref, m0, scale)
        rem_ref[...] = jnp.zeros_like(rem_ref)
        acc_ref[...] = jnp.zeros_like(acc_ref)
        chains = [(h, b) for h in range(2) for b in range(nb)]

        def tiles(js, masked_at):
            zss = []
            for j in js:
                kb = k_ref[pl.ds(pl.multiple_of(j * T, T), T), :].astype(BF16)
                zss.append(_split_blocks([_dot_nt(kb, qh[h]) for h in range(2)]))
            lass, sums, hiss, loss = [], [], [], []
            for zs, masked in zip(zss, _mask_flags(js, masked_at)):
                las, sm, his, los = [], [], [], []
                for (h, b), z in zip(chains, zs):
                    sp, la = _softplus_parts(z)
                    if masked:
                        sp = jnp.where(key < qrow + b * LANES, sp, 0.0)
                    hi, lo = _split2(sp)
                    las.append(la)
                    sm.append(jnp.sum(sp, axis=0, keepdims=True))
                    his.append(hi)
                    los.append(lo)
                lass.append(las)
                sums.append(sm)
                hiss.append(_join_blocks(his, nb))
                loss.append(_join_blocks(los, nb))
            rcss = [_cumsum_dot(after2, his, los) for his, los in zip(hiss, loss)]
            wss = []
            for las, sm, rcs, masked in zip(lass, sums, rcss, _mask_flags(js, masked_at)):
                ws = []
                for (h, b), la, s, rc in zip(chains, las, sm, rcs):
                    lanes = slice(b * LANES, (b + 1) * LANES)
                    w = jnp.exp(la - (rem_ref[h, :, lanes] + rc))
                    if masked:
                        w = jnp.where(key < qrow + b * LANES, w, 0.0)
                    ws.append(w.astype(BF16))
                    rem_ref[h, :, lanes] += s
                wss.append(_join_blocks(ws, nb))
            for j, ws in zip(js, wss):
                vtb = v_ref[pl.ds(pl.multiple_of(j * T, T), T), :].T.astype(BF16)
                for h in range(2):
                    acc_ref[h] += _dot(vtb, ws[h])

        _loop_tiles(i, tiles, True)
        o = jnp.where(top, acc_ref[0], acc_ref[1]).T
        o_ref[...] = o
        gt = g_ref[...]
        og = o * (gt * _sigmoid(gt))
        og_ref[...] = og.astype(BF16)
        ogt_ref[...] = og.T.astype(BF16)
        st_ref[0] = rem_ref[0]
        st_ref[1] = rem_ref[1]

    QT = ATT_QSUB * T
    qs, ks, vs, gs = _att_specs("sb", S, QT)
    W = npairs * LANES
    hows = ["rows", None, None, "rows", "rows", "rows", "lanes", "stat", None, None]
    return pl.pallas_call(
        _per_q_tile(body, hows), name=name, grid=(npairs, nq // ATT_QSUB),
        in_specs=[qs, ks, vs, gs],
        out_specs=[pl.BlockSpec((QT, LANES), lambda p, i: (i, p)), pl.BlockSpec((QT, LANES), lambda p, i: (i, p)),
                   pl.BlockSpec((LANES, QT), lambda p, i: (p, i)),
                   pl.BlockSpec((2, ATT_QSUB, 1, T), lambda p, i: (p, i, 0, 0))],
        out_shape=[jax.ShapeDtypeStruct((S, W), F32), jax.ShapeDtypeStruct((S, W), BF16),
                   jax.ShapeDtypeStruct((W, S), BF16),
                   jax.ShapeDtypeStruct((2 * npairs, nq, 1, T), F32)],
        scratch_shapes=[pltpu.VMEM((2, 1, T), F32), pltpu.VMEM((2, LANES, T), F32)],
        compiler_params=_cparams(("parallel", "parallel")),
    )(proj, proj, proj, proj)


def _sb_bwd_t(proj, do, tot, S, npairs, name):
    T = ATT_T
    nq = S // T
    nb = T // LANES
    scale = 64 ** -0.5

    def body(i, q_ref, k_ref, v_ref, do_ref, st_ref, dq_ref, dk_ref, dv_ref, dqt_ref, pre_ref, gpre_ref):

        @pl.when(i == 0)
        def _():
            dk_ref[...] = jnp.zeros_like(dk_ref)
            dv_ref[...] = jnp.zeros_like(dv_ref)

        m0 = lax.broadcasted_iota(jnp.int32, (1, LANES), 1) < 64
        top = lax.broadcasted_iota(jnp.int32, (LANES, 1), 0) < 64
        key = lax.broadcasted_iota(jnp.int32, (T, LANES), 0)
        qrow = lax.broadcasted_iota(jnp.int32, (T, LANES), 1)
        r = lax.broadcasted_iota(jnp.int32, (T, T), 0)
        c = lax.broadcasted_iota(jnp.int32, (T, T), 1)
        upto = (c <= r).astype(BF16)
        upto2 = jnp.concatenate([upto, upto], axis=1)
        left = (c < r).astype(BF16)
        qh = _head_q("sb", q_ref, m0, scale)
        dov = do_ref[...]
        doh = [jnp.where(m0, dov, 0.0).astype(BF16), jnp.where(m0, 0.0, dov).astype(BF16)]
        tot_h = [st_ref[0], st_ref[1]]
        dqt_ref[...] = jnp.zeros_like(dqt_ref)
        pre_ref[...] = jnp.zeros_like(pre_ref)
        gpre_ref[...] = jnp.zeros_like(gpre_ref)
        chains = [(h, b) for h in range(2) for b in range(nb)]

        def tiles(js, masked_at):
            starts = [pl.multiple_of(j * T, T) for j in js]
            zss, dwss = [], []
            for start in starts:
                vb = v_ref[pl.ds(start, T), :].astype(BF16)
                kb = k_ref[pl.ds(start, T), :].astype(BF16)
                zss.append(_split_blocks([_dot_nt(kb, qh[h]) for h in range(2)]))
                dwss.append(_split_blocks([_dot_nt(vb, doh[h]) for h in range(2)]))
            lass, sums, hiss, loss = [], [], [], []
            for zs, masked in zip(zss, _mask_flags(js, masked_at)):
                las, sm, his, los = [], [], [], []
                for (h, b), z in zip(chains, zs):
                    sp, la = _softplus_parts(z)
                    if masked:
                        sp = jnp.where(key < qrow + b * LANES, sp, 0.0)
                    hi, lo = _split2(sp)
                    las.append(la)
                    sm.append(jnp.sum(sp, axis=0, keepdims=True))
                    his.append(hi)
                    los.append(lo)
                lass.append(las)
                sums.append(sm)
                hiss.append(_join_blocks(his, nb))
                loss.append(_join_blocks(los, nb))
            pcss = [_cumsum_dot(upto2, his, los) for his, los in zip(hiss, loss)]
            wss, gss = [], []
            for las, sm, pcs, dws, masked in zip(lass, sums, pcss, dwss, _mask_flags(js, masked_at)):
                ws, gs = [], []
                for (h, b), la, s, pc, dw in zip(chains, las, sm, pcs, dws):
                    lanes = slice(b * LANES, (b + 1) * LANES)
                    w = jnp.exp(la - ((tot_h[h][:, lanes] - pre_ref[h, :, lanes]) - pc))
                    if masked:
                        w = jnp.where(key < qrow + b * LANES, w, 0.0)
                    ws.append(w.astype(BF16))
                    gs.append(dw * w)
                    pre_ref[h, :, lanes] += s
                wss.append(_join_blocks(ws, nb))
                gss.append(gs)
            gcss = [_split_blocks([_dot(left, g) for g in _join_blocks([g.astype(BF16) for g in gs], nb)]) for gs in gss]
            dzss = []
            for las, gs, gcs, masked in zip(lass, gss, gcss, _mask_flags(js, masked_at)):
                dzs = []
                for (h, b), la, g, gc in zip(chains, las, gs, gcs):
                    lanes = slice(b * LANES, (b + 1) * LANES)
                    dz = g - (g + (gpre_ref[h, :, lanes] + gc)) * jnp.exp(la)
                    if masked:
                        dz = jnp.where(key < qrow + b * LANES, dz, 0.0)
                    dzs.append(dz.astype(BF16))
                    gpre_ref[h, :, lanes] += jnp.sum(g, axis=0, keepdims=True)
                dzss.append(_join_blocks(dzs, nb))
            for start, ws, dzs in zip(starts, wss, dzss):
                kt = k_ref[pl.ds(start, T), :].T.astype(BF16)
                dkc = dvc = None
                for h in range(2):
                    dkh = _dot(dzs[h], qh[h])
                    dvh = _dot(ws[h], doh[h])
                    dkc = dkh if dkc is None else dkc + dkh
                    dvc = dvh if dvc is None else dvc + dvh
                    dqt_ref[h] += _dot(kt, dzs[h])
                dk_ref[pl.ds(start, T), :] += dkc
                dv_ref[pl.ds(start, T), :] += dvc

        _loop_tiles(i, tiles, False)
        dq_ref[...] = jnp.where(top, dqt_ref[0], dqt_ref[1]).T * scale

    QT = ATT_QSUB * T
    qs, ks, vs, _ = _att_specs("sb", S, QT)
    W = npairs * LANES
    hows = ["rows", None, None, "rows", "stat", "rows", None, None, None, None, None]
    return pl.pallas_call(
        _per_q_tile(body, hows), name=name, grid=(npairs, nq // ATT_QSUB),
        in_specs=[qs, ks, vs,
                  pl.BlockSpec((QT, LANES), lambda p, i: (i, p)),
                  pl.BlockSpec((2, ATT_QSUB, 1, T), lambda p, i: (p, i, 0, 0))],
        out_specs=[pl.BlockSpec((QT, LANES), lambda p, i: (i, p)), pl.BlockSpec((S, LANES), lambda p, i: (0, p)),
                   pl.BlockSpec((S, LANES), lambda p, i: (0, p))],
        out_shape=[jax.ShapeDtypeStruct((S, W), F32)] * 3,
        scratch_shapes=[pltpu.VMEM((2, LANES, T), F32), pltpu.VMEM((2, 1, T), F32), pltpu.VMEM((2, 1, T), F32)],
        compiler_params=_cparams(("parallel", "arbitrary")),
    )(proj, proj, proj, do, tot)


def _pad_w0(w):
    z = lambda n: jnp.zeros((w.shape[0], n), w.dtype)
    return jnp.concatenate([w[:, 2048:2432], w[:, 2432:2688], z(64), w[:, 2688:2720], z(32),
                            w[:, 1536:2048], w[:, 2720:3232], w[:, 0:512], w[:, 512:1024], w[:, 1024:1536]], axis=1)


def _unpad_w0(wp):
    return jnp.concatenate([wp[:, L0_SBQ:L0_SBQ + 512], wp[:, L0_SBK:L0_SBK + 512], wp[:, L0_SBV:L0_SBV + 512],
                            wp[:, L0_SBG:L0_SBG + 512], wp[:, 0:384], wp[:, 384:640], wp[:, 704:736],
                            wp[:, L0_MLG:L0_MLG + 512]], axis=1)


def _pad_wq(w):
    return jnp.pad(w.reshape(384, 8, 96), ((0, 0), (0, 0), (0, 32))).reshape(384, 1024)


def _unpad_wq(wp):
    return wp.reshape(384, 8, 128)[:, :, :96].reshape(384, 768)


def _pad_wkv(w):
    w3 = w.reshape(256, 8, 128)
    k = jnp.pad(w3[:, :, :64], ((0, 0), (0, 0), (0, 64))).reshape(256, 1024)
    return jnp.concatenate([k, w3[:, :, 64:].reshape(256, 512)], axis=1)


def _unpad_wkv(wp):
    k = wp[:, :1024].reshape(256, 8, 128)[:, :, :64]
    v = wp[:, 1024:].reshape(256, 8, 64)
    return jnp.concatenate([k, v], axis=-1).reshape(256, 1024)


def _pad_w1(w):
    return jnp.concatenate([w, jnp.zeros((w.shape[0], L1_WIDTH - ODD_IN_WIDTH), w.dtype)], axis=1)


def _local_step(x, positions, target, g, w0p, wqp, wkvp, wo0, w1p, wo1):
    S = x.shape[0]
    nq = S // ATT_T
    invf = ROPE_THETA ** (-jnp.arange(0, MLA_ROPE_DIM, 2, dtype=F32) / MLA_ROPE_DIM)
    invf = jnp.concatenate([jnp.zeros((64,), F32), invf, invf, jnp.zeros((32,), F32)]).reshape(1, LANES)
    cosT, s1T, s2T = _rope_tables(positions.reshape(S, 1), invf, "rope_tables")
    bfp = jnp.pad(g["l1_b_f"], ((0, 0), (0, LANES - FOX_HEADS)))

    proj0, h0t = _norm_matmul(x, g["l0_pre_g"], w0p, "l0_in_proj")
    qm, km, vm, qnt, cnt = _mla_prep(proj0, g["l0_q_a_g"], g["l0_kv_a_g"], wqp, wkvp, cosT, s1T, s2T, "mla_prep")
    o_sb, og_sb, ogt_sb, tot_sb = _sb_fwd_t(proj0, S, 4, "sb_fwd")
    o_ml, og_ml, ogt_ml, lse_ml = _softmax_fwd("mla", (qm, km, vm, proj0), None, S, 4, "mla_fwd")
    y0, x1 = _out_proj(og_sb, og_ml, 0, 0, wo0, x, g["l0_post_g"], None, "l0_out_proj")

    proj1, h1t = _norm_matmul(x1, g["l1_pre_g"], w1p, "l1_in_proj")
    cfx = _fox_prep(proj1, bfp, "fox_prep")
    c16 = cfx[:, :FOX_HEADS].T
    c_col = jnp.broadcast_to(c16[:, :, None], (FOX_HEADS, S, LANES))
    o_fx, og_fx, ogt_fx, lse_fx = _softmax_fwd("fox", (proj1, proj1, proj1, proj1), c_col, S, 8, "fox_fwd")
    y1, dx2, lsum = _out_proj(og_fx, og_fx, 0, 1, wo1, x1, g["l1_post_g"], target, "l1_out_proj")

    dy1, do1, dgate1, d_post1 = _out_proj_bwd(dx2, y1, g["l1_post_g"], wo1, proj1, (L1_G, L1_G + 512), o_fx, o_fx, 0, 1, "l1_out_bwd")
    dwo1 = _matmul_t(ogt_fx, dy1, "l1_dw_out")
    dq1, dk1, dv1, dck, dcq = _softmax_bwd_t("fox", proj1, proj1, proj1, do1, 0, o_fx, lse_fx, c_col, S, 8,
                                             "fox_bwd")
    dc = jnp.pad((dcq.reshape(FOX_HEADS, S) - dck.reshape(FOX_HEADS, S)).T, ((0, 0), (0, LANES - FOX_HEADS)))
    df, d_bf = _fox_prep_bwd(dc, proj1, bfp, "fox_prep_bwd")
    pieces1 = [(L1_Q, dq1), (L1_K, dk1), (L1_V, dv1), (L1_G, dgate1), (L1_F, df)]
    dx1, d_pre1 = _in_proj_bwd(pieces1, w1p, x1, g["l1_pre_g"], dx2, "l1_in_bwd")
    dw1p = jnp.concatenate(_matmul_t_many(h1t, [dq1, dk1], "l1_dw_in_a")
                           + _matmul_t_many(h1t, [dv1, dgate1, df], "l1_dw_in_b"), axis=1)

    dy0, do0, dgate0, d_post0 = _out_proj_bwd(dx1, y0, g["l0_post_g"], wo0, proj0, (L0_SBG, L0_MLG), o_sb, o_ml, 0, 0,
                                              "l0_out_bwd")
    dwo0 = jnp.concatenate([_matmul_t(ogt_sb, dy0, "l0_dw_out_sb"), _matmul_t(ogt_ml, dy0, "l0_dw_out_mla")], axis=0)
    dsq, dsk, dsv = _sb_bwd_t(proj0, do0, tot_sb, S, 4, "sb_bwd")
    dqm, dkm, dvm = _softmax_bwd_t("mla", qm, km, vm, do0, 4, o_ml, lse_ml, None, S, 4, "mla_bwd")
    dprep, dqb, dkvb, d_qag, d_kvag = _mla_prep_bwd(dqm, dkm, dvm, proj0, g["l0_q_a_g"], g["l0_kv_a_g"], wqp, wkvp,
                                                    cosT, s1T, s2T, "mla_prep_bwd")
    dwqp = _matmul_t(qnt, dqb, "l0_dw_qb")
    dwkvp = _matmul_t(cnt, dkvb, "l0_dw_kvb")
    pieces0 = [(L0_PREP, dprep), (L0_SBG, dgate0), (L0_SBQ, dsq), (L0_SBK, dsk), (L0_SBV, dsv)]
    dx0, d_pre0 = _in_proj_bwd(pieces0, w0p, x, g["l0_pre_g"], dx1, "l0_in_bwd")
    dw0p = jnp.concatenate(_matmul_t_many(h0t, [dprep, dgate0], "l0_dw_in_a")
                           + _matmul_t_many(h0t, [dsq, dsk, dsv], "l0_dw_in_b"), axis=1)

    grads = {
        "l0_pre_g": d_pre0, "l0_post_g": d_post0, "l0_w_in": dw0p, "l0_q_a_g": d_qag, "l0_w_q_b": dwqp,
        "l0_kv_a_g": d_kvag, "l0_w_kv_b": dwkvp, "l0_w_out": dwo0, "l1_pre_g": d_pre1, "l1_post_g": d_post1,
        "l1_w_in": dw1p, "l1_b_f": d_bf[:, :FOX_HEADS], "l1_w_out": dwo1,
    }
    return lsum, dx0, grads


_ANY = pl.BlockSpec(memory_space=pl.ANY)


def _place():
    return lax.axis_index("x"), lax.axis_index("y"), lax.axis_index("c")


def _other_chips(x, y):
    return [(1 - x, y), (x, 1 - y), (1 - x, 1 - y)]


def _half(rows, c):
    return pl.ds(c * (rows // 2), rows // 2)


def _weight_gather(parts):
    n = len(parts)

    def body(*refs):
        p_refs, out_refs, send_sems, recv_sems = refs[:n], refs[n:2 * n], refs[2 * n], refs[2 * n + 1]
        x, y, c = _place()
        sibling = (x, y, 1 - c)
        chips = _other_chips(x, y)

        def blk(k, chip, cc):
            return out_refs[k].at[2 * chip[0] + chip[1], _half(p_refs[k].shape[0], cc)]

        def copy(s, src, dst, to):
            return pltpu.make_async_remote_copy(src_ref=src, dst_ref=dst, send_sem=send_sems.at[s],
                                                recv_sem=recv_sems.at[s], device_id=to, device_id_type=MESH)

        first = [copy(6 * k + j, p_refs[k].at[_half(p_refs[k].shape[0], c)], blk(k, (x, y), c), (*chip, c))
                 for j, chip in enumerate(chips) for k in range(n)]
        for cp in first:
            cp.start()
        passed = []
        for j, chip in enumerate(chips):
            for k in range(n):
                copy(6 * k + j, blk(k, chip, c), blk(k, chip, c), (x, y, c)).wait_recv()
                passed.append(copy(6 * k + 3 + j, blk(k, chip, c), blk(k, chip, c), sibling))
                passed[-1].start()
        for j, chip in enumerate(chips):
            for k in range(n):
                copy(6 * k + 3 + j, blk(k, chip, 1 - c), blk(k, chip, 1 - c), (x, y, c)).wait_recv()
        for cp in first + passed:
            cp.wait_send()

    return pl.pallas_call(
        body, name="weight_gather", in_specs=[_ANY] * n, out_specs=[_ANY] * n,
        out_shape=[jax.ShapeDtypeStruct((4,) + a.shape, a.dtype) for a in parts],
        scratch_shapes=[pltpu.SemaphoreType.DMA((6 * n,)), pltpu.SemaphoreType.DMA((6 * n,))],
    )(*parts)


def _grad_core_exchange(ps):
    n = len(ps)

    def body(*refs):
        p_refs, recv_refs, send_sems, recv_sems = refs[:n], refs[n:2 * n], refs[2 * n], refs[2 * n + 1]
        x, y, c = _place()
        give = [pltpu.make_async_remote_copy(src_ref=p_refs[k].at[j, _half(p_refs[k].shape[1], 1 - c)],
                                             dst_ref=recv_refs[k].at[j], send_sem=send_sems.at[4 * k + j],
                                             recv_sem=recv_sems.at[4 * k + j], device_id=(x, y, 1 - c),
                                             device_id_type=MESH) for k in range(n) for j in range(4)]
        for cp in give:
            cp.start()
        for cp in give:
            cp.wait()

    return pl.pallas_call(
        body, name="grad_core_exchange", in_specs=[_ANY] * n, out_specs=[_ANY] * n,
        out_shape=[jax.ShapeDtypeStruct((4, p.shape[1] // 2, p.shape[2]), p.dtype) for p in ps],
        scratch_shapes=[pltpu.SemaphoreType.DMA((4 * n,)), pltpu.SemaphoreType.DMA((4 * n,))],
    )(*ps)


def _grad_rows(rows):
    return _pick(rows, (1296, 512, rows))


def _grad_add_cores(p, theirs, c1, name):
    _, rh, cols = theirs.shape
    tr = _grad_rows(rh)

    def body(c_ref, a_ref, b_ref, o_ref):
        o_ref[...] = (a_ref[...] + b_ref[...]).astype(BF16)

    spec = pl.BlockSpec((None, tr, cols), lambda j, r, c: (j, r, 0))
    grid_spec = pltpu.PrefetchScalarGridSpec(
        num_scalar_prefetch=1, grid=(4, rh // tr),
        in_specs=[pl.BlockSpec((None, None, tr, cols), lambda j, r, c: (j, c[0], r, 0)), spec], out_specs=spec)
    return pl.pallas_call(
        body, name=name, grid_spec=grid_spec, out_shape=jax.ShapeDtypeStruct(theirs.shape, BF16),
        compiler_params=_cparams(("parallel", "parallel")),
    )(c1, p.reshape(4, 2, rh, cols), theirs)


def _grad_chip_exchange(qs):
    n = len(qs)

    def body(*refs):
        q_refs, out_refs, send_sems, recv_sems = refs[:n], refs[n:2 * n], refs[2 * n], refs[2 * n + 1]
        x, y, c = _place()
        me = 2 * x + y
        chips = _other_chips(x, y)
        sends = [pltpu.make_async_remote_copy(src_ref=q_refs[k].at[2 * chip[0] + chip[1]], dst_ref=out_refs[k].at[me],
                                              send_sem=send_sems.at[3 * k + j], recv_sem=recv_sems.at[3 * k + j],
                                              device_id=(*chip, c), device_id_type=MESH)
                 for j, chip in enumerate(chips) for k in range(n)]
        for cp in sends:
            cp.start()
        for j, chip in enumerate(chips):
            for k in range(n):
                slot = out_refs[k].at[2 * chip[0] + chip[1]]
                pltpu.make_async_remote_copy(src_ref=slot, dst_ref=slot, send_sem=send_sems.at[3 * k + j],
                                             recv_sem=recv_sems.at[3 * k + j], device_id=(x, y, c),
                                             device_id_type=MESH).wait_recv()
        for cp in sends:
            cp.wait_send()

    return pl.pallas_call(
        body, name="grad_chip_exchange", in_specs=[_ANY] * n, out_specs=[_ANY] * n,
        out_shape=[jax.ShapeDtypeStruct(q.shape, q.dtype) for q in qs],
        scratch_shapes=[pltpu.SemaphoreType.DMA((3 * n,)), pltpu.SemaphoreType.DMA((3 * n,))],
    )(*qs)


def _grad_add_chips(q, slots, me1, name):
    _, rh, cols = q.shape
    tr = _grad_rows(rh)

    def body(me_ref, own_ref, s0, s1, s2, s3, o_ref):
        me = me_ref[0]
        t = [jnp.where(me == j, own_ref[...], s[...]).astype(F32) for j, s in enumerate((s0, s1, s2, s3))]
        o_ref[...] = ((t[0] + t[1]) + t[2]) + t[3]

    def slot_spec(j):
        return pl.BlockSpec((None, tr, cols), lambda r, me: (jnp.where(me[0] == j, (j + 1) % 4, j), r, 0))

    grid_spec = pltpu.PrefetchScalarGridSpec(
        num_scalar_prefetch=1, grid=(rh // tr,),
        in_specs=[pl.BlockSpec((None, tr, cols), lambda r, me: (me[0], r, 0))] + [slot_spec(j) for j in range(4)],
        out_specs=pl.BlockSpec((tr, cols), lambda r, me: (r, 0)))
    return pl.pallas_call(
        body, name=name, grid_spec=grid_spec, out_shape=jax.ShapeDtypeStruct(q.shape[1:], F32),
        compiler_params=_cparams(("parallel",)),
    )(me1, q, slots, slots, slots, slots)


def _grad_core_gather(ts):
    n = len(ts)

    def body(*refs):
        t_refs, out_refs, send_sems, recv_sems = refs[:n], refs[n:2 * n], refs[2 * n], refs[2 * n + 1]
        x, y, c = _place()
        give = [pltpu.make_async_remote_copy(src_ref=t_refs[k], dst_ref=out_refs[k], send_sem=send_sems.at[k],
                                             recv_sem=recv_sems.at[k], device_id=(x, y, 1 - c), device_id_type=MESH)
                for k in range(n)]
        for cp in give:
            cp.start()
        for cp in give:
            cp.wait()

    return pl.pallas_call(
        body, name="grad_core_gather", in_specs=[_ANY] * n, out_specs=[_ANY] * n,
        out_shape=[jax.ShapeDtypeStruct(t.shape, t.dtype) for t in ts],
        scratch_shapes=[pltpu.SemaphoreType.DMA((n,)), pltpu.SemaphoreType.DMA((n,))],
    )(*ts)


def _small_allreduce(sp):
    def body(sp_ref, out_ref, gath_ref, send_sems, recv_sems):
        x, y, c = _place()
        me = 4 * x + 2 * y + c
        gath_ref[me] = sp_ref[...]
        peers = []
        for k in range(1, 8):
            px = 1 - x if k & 4 else x
            py = 1 - y if k & 2 else y
            pc = 1 - c if k & 1 else c
            peers.append((px, py, pc))
        sends = [pltpu.make_async_remote_copy(src_ref=sp_ref, dst_ref=gath_ref.at[me], send_sem=send_sems.at[k],
                                              recv_sem=recv_sems.at[k], device_id=peer, device_id_type=MESH)
                 for k, peer in enumerate(peers)]
        for cp in sends:
            cp.start()
        for k, (px, py, pc) in enumerate(peers):
            slot = gath_ref.at[4 * px + 2 * py + pc]
            pltpu.make_async_remote_copy(src_ref=slot, dst_ref=slot, send_sem=send_sems.at[k], recv_sem=recv_sems.at[k],
                                         device_id=(x, y, c), device_id_type=MESH).wait_recv()
        for cp in sends:
            cp.wait_send()
        tot = gath_ref[0]
        for d in range(1, 8):
            tot = tot + gath_ref[d]
        out_ref[...] = tot

    vm = pl.BlockSpec(memory_space=pltpu.VMEM)
    return pl.pallas_call(
        body, name="small_allreduce", in_specs=[vm], out_specs=vm, out_shape=jax.ShapeDtypeStruct(sp.shape, sp.dtype),
        scratch_shapes=[pltpu.VMEM((8,) + sp.shape, sp.dtype), pltpu.SemaphoreType.DMA((7,)), pltpu.SemaphoreType.DMA((7,))],
    )(sp)


def _adamw_update(w, gv, m, v):
    mn = ADAM_B1 * m + (1.0 - ADAM_B1) * gv
    vn = ADAM_B2 * v + (1.0 - ADAM_B2) * (gv * gv)
    m_hat = mn / (1.0 - ADAM_B1 ** ADAM_STEP)
    v_hat = vn / (1.0 - ADAM_B2 ** ADAM_STEP)
    return -ADAM_LR * (m_hat / (jnp.sqrt(v_hat) + ADAM_EPS) + ADAM_WD * w), mn, vn


def _adamw(w, g, m, v, name):
    rows, cols = w.shape

    def body(w_ref, g_ref, m_ref, v_ref, d_ref, mo_ref, vo_ref):
        d_ref[...], mo_ref[...], vo_ref[...] = _adamw_update(w_ref[...], g_ref[...], m_ref[...], v_ref[...])

    if rows % 256 == 0 or cols % 256 != 0:
        tr = _pick(rows, (256, rows))
        grid, spec = (rows // tr,), pl.BlockSpec((tr, cols), lambda r: (r, 0))
    else:
        grid, spec = (cols // 256,), pl.BlockSpec((rows, 256), lambda r: (0, r))
    shp = jax.ShapeDtypeStruct(w.shape, F32)
    return pl.pallas_call(
        body, name=name, grid=grid, in_specs=[spec] * 4, out_specs=[spec] * 3, out_shape=[shp] * 3,
        compiler_params=_cparams(("parallel",)),
    )(w, g, m, v)


MAT_NAMES = ("l0_w_in", "l0_w_q_b", "l0_w_kv_b", "l0_w_out", "l1_w_in", "l1_w_out")
VEC_NAMES = ("l0_pre_g", "l0_post_g", "l0_q_a_g", "l0_kv_a_g", "l1_pre_g", "l1_post_g", "l1_b_f")
WEIGHT_NAMES = ("l0_pre_g", "l0_post_g", "l0_w_in", "l0_q_a_g", "l0_w_q_b", "l0_kv_a_g", "l0_w_kv_b", "l0_w_out",
                "l1_pre_g", "l1_post_g", "l1_w_in", "l1_b_f", "l1_w_out")
MAT_SHARD = {"l0_w_in": (1024, 808), "l0_w_q_b": (384, 192), "l0_w_kv_b": (256, 256), "l0_w_out": (256, 1024),
             "l1_w_in": (1024, 1028), "l1_w_out": (256, 1024)}
ROW_SHARDED = ("l0_w_out", "l1_w_out")
WHOLE_MATS = ("l0_w_in", "l1_w_in")
PACKED_MATS = ("l0_w_q_b", "l0_w_kv_b", "l0_w_out", "l1_w_out")
VEC_LEN = {"l0_pre_g": 1024, "l0_post_g": 1024, "l0_q_a_g": 384, "l0_kv_a_g": 256, "l1_pre_g": 1024,
           "l1_post_g": 1024, "l1_b_f": 16}


def _mat_rows(n):
    r, c = MAT_SHARD[n]
    return r * c // LANES


def _pack_shards(shards):
    return jnp.concatenate([shards[n].reshape(shards[n].shape[:-2] + (_mat_rows(n), LANES)) for n in PACKED_MATS],
                           axis=-2)


def _unpack_shards(pack):
    out, at = {}, 0
    for n in PACKED_MATS:
        out[n] = pack[..., at:at + _mat_rows(n), :].reshape(pack.shape[:-2] + MAT_SHARD[n])
        at += _mat_rows(n)
    return out


def _join_shards(n, s):
    if n in ROW_SHARDED:
        return s.reshape(4 * s.shape[1], s.shape[2])
    return s.transpose(1, 0, 2).reshape(s.shape[1], 4 * s.shape[2])


def _cut_shards(n, w):
    r, c = MAT_SHARD[n]
    if n in ROW_SHARDED:
        return w.reshape(4, r, c)
    return w.reshape(r, 4, c).transpose(1, 0, 2)


def _pack_vecs(vecs):
    parts = []
    for n in VEC_NAMES:
        v = vecs[n].reshape(-1)
        parts.append(jnp.pad(v, (0, VEC_ROWS * LANES - v.shape[0])).reshape(VEC_ROWS, LANES))
    return jnp.concatenate(parts, axis=0)


def _unpack_vecs(pack):
    return {n: pack[k * VEC_ROWS:(k + 1) * VEC_ROWS].reshape(-1)[:VEC_LEN[n]] for k, n in enumerate(VEC_NAMES)}


def kernel(x, positions, l0_pre_g, l0_post_g, l0_w_in, l0_q_a_g, l0_w_q_b, l0_kv_a_g, l0_w_kv_b, l0_w_out, l1_pre_g, l1_post_g, l1_w_in, l1_b_f, l1_w_out, loss_target, m_l0_pre_g, m_l0_post_g, m_l0_w_in, m_l0_q_a_g, m_l0_w_q_b, m_l0_kv_a_g, m_l0_w_kv_b, m_l0_w_out, m_l1_pre_g, m_l1_post_g, m_l1_w_in, m_l1_b_f, m_l1_w_out, v_l0_pre_g, v_l0_post_g, v_l0_w_in, v_l0_q_a_g, v_l0_w_q_b, v_l0_kv_a_g, v_l0_w_kv_b, v_l0_w_out, v_l1_pre_g, v_l1_post_g, v_l1_w_in, v_l1_b_f, v_l1_w_out):
    w = dict(l0_pre_g=l0_pre_g, l0_post_g=l0_post_g, l0_w_in=l0_w_in, l0_q_a_g=l0_q_a_g, l0_w_q_b=l0_w_q_b,
             l0_kv_a_g=l0_kv_a_g, l0_w_kv_b=l0_w_kv_b, l0_w_out=l0_w_out, l1_pre_g=l1_pre_g, l1_post_g=l1_post_g,
             l1_w_in=l1_w_in, l1_b_f=l1_b_f, l1_w_out=l1_w_out)
    m = dict(l0_pre_g=m_l0_pre_g, l0_post_g=m_l0_post_g, l0_w_in=m_l0_w_in, l0_q_a_g=m_l0_q_a_g, l0_w_q_b=m_l0_w_q_b,
             l0_kv_a_g=m_l0_kv_a_g, l0_w_kv_b=m_l0_w_kv_b, l0_w_out=m_l0_w_out, l1_pre_g=m_l1_pre_g,
             l1_post_g=m_l1_post_g, l1_w_in=m_l1_w_in, l1_b_f=m_l1_b_f, l1_w_out=m_l1_w_out)
    v = dict(l0_pre_g=v_l0_pre_g, l0_post_g=v_l0_post_g, l0_w_in=v_l0_w_in, l0_q_a_g=v_l0_q_a_g, l0_w_q_b=v_l0_w_q_b,
             l0_kv_a_g=v_l0_kv_a_g, l0_w_kv_b=v_l0_w_kv_b, l0_w_out=v_l0_w_out, l1_pre_g=v_l1_pre_g,
             l1_post_g=v_l1_post_g, l1_w_in=v_l1_w_in, l1_b_f=v_l1_b_f, l1_w_out=v_l1_w_out)

    cx, cy, cc = _place()
    me1 = jnp.reshape(2 * cx + cy, (1,)).astype(jnp.int32)
    c1 = jnp.reshape(cc, (1,)).astype(jnp.int32)
    w_bf = {n: w[n].astype(BF16) for n in MAT_NAMES}
    mine = [_pack_shards(w_bf)] + [w_bf[n] for n in WHOLE_MATS]
    got = [lax.dynamic_update_slice(g, a[None], (2 * cx + cy, 0, 0)) for g, a in zip(_weight_gather(mine), mine)]
    gathered = dict(_unpack_shards(got[0]), **dict(zip(WHOLE_MATS, got[1:])))
    full = {n: _join_shards(n, gathered[n]) for n in MAT_NAMES}
    gains = {n: w[n].reshape(1, -1) for n in VEC_NAMES}

    lsum, dx0, grads = _local_step(
        x[0], positions[0], loss_target[0], gains, _pad_w0(full["l0_w_in"]), _pad_wq(full["l0_w_q_b"]),
        _pad_wkv(full["l0_w_kv_b"]), full["l0_w_out"], _pad_w1(full["l1_w_in"]), full["l1_w_out"])

    gfull = {"l0_w_in": _unpad_w0(grads["l0_w_in"]), "l0_w_q_b": _unpad_wq(grads["l0_w_q_b"]),
             "l0_w_kv_b": _unpad_wkv(grads["l0_w_kv_b"]), "l0_w_out": grads["l0_w_out"],
             "l1_w_in": grads["l1_w_in"][:, :ODD_IN_WIDTH], "l1_w_out": grads["l1_w_out"]}
    cut = {n: _cut_shards(n, gfull[n]) for n in MAT_NAMES}
    tags = ("packed",) + WHOLE_MATS
    g_parts = [_pack_shards(cut)] + [cut[n] for n in WHOLE_MATS]
    q_cores = [_grad_add_cores(p, t, c1, "grad_add_cores_" + tag)
               for p, t, tag in zip(g_parts, _grad_core_exchange(g_parts), tags)]
    g_mine = [_grad_add_chips(q, s, me1, "grad_add_chips_" + tag)
              for q, s, tag in zip(q_cores, _grad_chip_exchange(q_cores), tags)]
    g_theirs = _grad_core_gather(g_mine)

    small = _small_allreduce(jnp.concatenate([_pack_vecs({n: grads[n] for n in VEC_NAMES}),
                                              lsum.reshape(D_MODEL // LANES, LANES)], axis=0))
    g_small = small[:SMALL_ROWS]
    loss = 0.5 * jnp.sum(small[SMALL_ROWS:]) / float(D_MODEL)

    whole = [jnp.concatenate([lax.select(cc == 0, a, b), lax.select(cc == 0, b, a)], axis=0)
             for a, b in zip(g_mine, g_theirs)]
    g_mats = dict(_unpack_shards(whole[0]), **dict(zip(WHOLE_MATS, whole[1:])))
    d_mats, m_mats, v_mats = {}, {}, {}
    for n in PACKED_MATS:
        d_mats[n], m_mats[n], v_mats[n] = _adamw(w[n], g_mats[n], m[n], v[n], "adamw_" + n)
    for n in WHOLE_MATS:
        gt = g_mats[n].T
        outs = _adamw(w[n].T, gt, m[n].T, v[n].T, "adamw_" + n)
        g_mats[n], d_mats[n], m_mats[n], v_mats[n] = gt.T, outs[0].T, outs[1].T, outs[2].T
    d_small, m_small, v_small = _adamw(_pack_vecs(w), g_small, _pack_vecs(m), _pack_vecs(v), "adamw_vecs")

    def leaves(mats, vec_pack):
        out = dict(mats)
        out.update(_unpack_vecs(vec_pack))
        return [out[n] for n in WEIGHT_NAMES]

    return (loss, dx0[None], *leaves(g_mats, g_small), *leaves(d_mats, d_small), *leaves(m_mats, m_small),
            *leaves(v_mats, v_small))
```

```python
import jax
import jax.numpy as jnp
from jax import lax
from jax.experimental import pallas as pl
from jax.experimental.pallas import tpu as pltpu

F32 = jnp.float32
BF16 = jnp.bfloat16
MESH = pl.DeviceIdType.MESH

D_MODEL = 1024
RMS_EPS = 1e-6
ROPE_THETA = 10000.0
SB_WIDTH = 512
MLA_Q_LORA = 384
MLA_KV_LORA = 256
MLA_ROPE_DIM = 32
MLA_WIDTH = 512
FOX_WIDTH = 1024
FOX_HEADS = 16
EVEN_IN_WIDTH = 3232
ODD_IN_WIDTH = 4112

ADAM_LR = 0.001
ADAM_B1 = 0.9
ADAM_B2 = 0.999
ADAM_EPS = 1e-08
ADAM_WD = 0.01
ADAM_STEP = 10

LANES = 128
VMEM_LIMIT = 56 * 1024 * 1024

L0_PREP = 0
L0_PREP_W = 768
L0_SBG = 768
L0_MLG = 1280
L0_SBQ = 1792
L0_SBK = 2304
L0_SBV = 2816
L0_WIDTH = 3328
L1_Q = 0
L1_K = 1024
L1_V = 2048
L1_G = 3072
L1_F = 4096
L1_WIDTH = 4224

ATT_T = 256
ATT_GROUP = 4
ATT_QSUB = 2
NEG = -1e30

VEC_ROWS = 8
SMALL_ROWS = 7 * VEC_ROWS


def _cparams(sem, **kw):
    return pltpu.CompilerParams(dimension_semantics=sem, vmem_limit_bytes=VMEM_LIMIT, **kw)


def _dot(a, b):
    return lax.dot_general(a, b, (((1,), (0,)), ((), ())), preferred_element_type=F32)


def _dot_nt(a, b):
    return lax.dot_general(a, b, (((1,), (1,)), ((), ())), preferred_element_type=F32)


def _sigmoid(x):
    return 1.0 / (1.0 + jnp.exp(-x))


def _rstd(x):
    return lax.rsqrt(jnp.mean(x * x, axis=-1, keepdims=True) + RMS_EPS)


def _norm_bwd(x, g, dy):
    r = _rstd(x)
    xn = x * r
    dxn = dy * g
    dx = r * (dxn - xn * jnp.mean(dxn * xn, axis=-1, keepdims=True))
    return dx, dy * xn


def _split3(x):
    hi = x.astype(BF16)
    r1 = x - hi.astype(F32)
    mid = r1.astype(BF16)
    lo = (r1 - mid.astype(F32)).astype(BF16)
    return hi, mid, lo


def _wide_tile(n, cap=1792):
    return max(t for t in range(LANES, min(n, cap) + 1, LANES) if n % t == 0)


def _pick(n, cands):
    for c in cands:
        if n % c == 0:
            return c
    raise ValueError(n)


def _norm_matmul(x, g, w, name, ride=None):
    S, K = x.shape
    N = w.shape[1]
    tm = _pick(S, (1024, 512, 256))
    tn = _wide_tile(N)
    ni, nj = S // tm, N // tn

    def body(x_ref, g_ref, w_ref, *rest):
        if ride is None:
            o_ref, ht_ref, h_ref = rest
        else:
            a_ref, o_ref, ht_ref, land_ref, h_ref, send_sems, recv_sems = rest
            begin, finish = _gather_phases([a_ref], [land_ref], send_sems, recv_sems)
            pl.when((pl.program_id(0) == 0) & (pl.program_id(1) == 0))(begin)

        @pl.when(pl.program_id(1) == 0)
        def _():
            xv = x_ref[...]
            h = (xv * _rstd(xv)) * g_ref[...]
            h_ref[...] = h.astype(BF16)
            ht_ref[...] = h.T.astype(BF16)
        o_ref[...] = _dot(h_ref[...], w_ref[...])
        if ride is not None:
            pl.when((pl.program_id(0) == ni - 1) & (pl.program_id(1) == nj - 1))(finish)

    in_specs = [pl.BlockSpec((tm, K), lambda i, j: (i, 0)),
                pl.BlockSpec((1, K), lambda i, j: (0, 0)),
                pl.BlockSpec((K, tn), lambda i, j: (0, j))]
    out_specs = [pl.BlockSpec((tm, tn), lambda i, j: (i, j)), pl.BlockSpec((K, tm), lambda i, j: (0, i))]
    out_shape = [jax.ShapeDtypeStruct((S, N), F32), jax.ShapeDtypeStruct((K, S), BF16)]
    scratch = [pltpu.VMEM((tm, K), BF16)]
    args = [x, g, w]
    if ride is not None:
        in_specs.append(_ANY)
        out_specs.append(_ANY)
        out_shape.append(jax.ShapeDtypeStruct((4,) + ride.shape, ride.dtype))
        scratch += [pltpu.SemaphoreType.DMA((6,)), pltpu.SemaphoreType.DMA((6,))]
        args.append(ride)
    return pl.pallas_call(
        body, name=name, grid=(ni, nj), in_specs=in_specs, out_specs=out_specs, out_shape=out_shape,
        scratch_shapes=scratch,
        compiler_params=_cparams(("parallel", "arbitrary") if ride is None else ("arbitrary", "arbitrary")),
    )(*args)


def _matmul_t(at, b, name):
    M, S = at.shape
    N = b.shape[1]
    tn = _wide_tile(N)
    ts = _pick(S, (512, 256))

    def body(a_ref, b_ref, o_ref):
        @pl.when(pl.program_id(1) == 0)
        def _():
            o_ref[...] = jnp.zeros_like(o_ref)
        o_ref[...] += _dot(a_ref[...], b_ref[...].astype(BF16))

    return pl.pallas_call(
        body, name=name, grid=(N // tn, S // ts),
        in_specs=[pl.BlockSpec((M, ts), lambda j, k: (0, k)),
                  pl.BlockSpec((ts, tn), lambda j, k: (k, j))],
        out_specs=pl.BlockSpec((M, tn), lambda j, k: (0, j)),
        out_shape=jax.ShapeDtypeStruct((M, N), F32),
        compiler_params=_cparams(("parallel", "arbitrary")),
    )(at, b)


def _matmul_t_many(at, bs, name):
    M, S = at.shape
    ts = _pick(S, (512, 256))
    n = len(bs)

    def body(*refs):
        a_ref, b_refs, o_refs = refs[0], refs[1:1 + n], refs[1 + n:]

        @pl.when(pl.program_id(0) == 0)
        def _():
            for o_ref in o_refs:
                o_ref[...] = jnp.zeros_like(o_ref)

        a = a_ref[...]
        for b_ref, o_ref in zip(b_refs, o_refs):
            o_ref[...] += _dot(a, b_ref[...].astype(BF16))

    return pl.pallas_call(
        body, name=name, grid=(S // ts,),
        in_specs=[pl.BlockSpec((M, ts), lambda k: (0, k))] + [pl.BlockSpec((ts, b.shape[1]), lambda k: (k, 0)) for b in bs],
        out_specs=[pl.BlockSpec((M, b.shape[1]), lambda k: (0, 0)) for b in bs],
        out_shape=[jax.ShapeDtypeStruct((M, b.shape[1]), F32) for b in bs],
        compiler_params=_cparams(("arbitrary",)),
    )(at, *bs)


def _in_proj_bwd(pieces, w, x, g, dx_up, name):
    S, K = x.shape
    N = w.shape[1]
    tm = _pick(S, (256,))
    offs = [off for off, _ in pieces]
    arrs = [a for _, a in pieces]

    def body(*refs):
        d_refs = refs[:len(arrs)]
        w_ref, x_ref, g_ref, u_ref, dx_ref, dg_ref = refs[len(arrs):]

        @pl.when(pl.program_id(0) == 0)
        def _():
            dg_ref[...] = jnp.zeros_like(dg_ref)

        acc = None
        for off, d_ref in zip(offs, d_refs):
            part = _dot_nt(d_ref[...].astype(BF16), w_ref[:, off:off + d_ref.shape[1]])
            acc = part if acc is None else acc + part
        dx, dgrow = _norm_bwd(x_ref[...], g_ref[...], acc)
        dx_ref[...] = u_ref[...] + dx
        dg_ref[...] += jnp.sum(dgrow, axis=0, keepdims=True)

    row = lambda i: (i, 0)
    fixed = lambda i: (0, 0)
    return pl.pallas_call(
        body, name=name, grid=(S // tm,),
        in_specs=[pl.BlockSpec((tm, a.shape[1]), row) for a in arrs] + [
            pl.BlockSpec((K, N), fixed), pl.BlockSpec((tm, K), row), pl.BlockSpec((1, K), fixed),
            pl.BlockSpec((tm, K), row)],
        out_specs=[pl.BlockSpec((tm, K), row), pl.BlockSpec((1, K), fixed)],
        out_shape=[jax.ShapeDtypeStruct((S, K), F32), jax.ShapeDtypeStruct((1, K), F32)],
        compiler_params=_cparams(("arbitrary",)),
    )(*arrs, w, x, g, dx_up)


def _out_proj(og_a, og_b, blk_a, blk_b, w, x, g, target, name):
    S = x.shape[0]
    D = x.shape[1]
    tm = _pick(S, (512, 256))
    with_loss = target is not None

    def body(*refs):
        if with_loss:
            a_ref, b_ref, wa_ref, wb_ref, x_ref, g_ref, t_ref, y_ref, o_ref, l_ref = refs
        else:
            a_ref, b_ref, wa_ref, wb_ref, x_ref, g_ref, y_ref, o_ref = refs
        y = _dot(a_ref[...], wa_ref[...]) + _dot(b_ref[...], wb_ref[...])
        y_ref[...] = y
        xn = x_ref[...] + (y * _rstd(y)) * g_ref[...]
        if with_loss:
            @pl.when(pl.program_id(0) == 0)
            def _():
                l_ref[...] = jnp.zeros_like(l_ref)
            d = xn - t_ref[...]
            o_ref[...] = d / float(D)
            l_ref[...] += jnp.sum(d * d, axis=0, keepdims=True)
        else:
            o_ref[...] = xn

    row = lambda i: (i, 0)
    in_specs = [pl.BlockSpec((tm, 512), lambda i: (i, blk_a)),
                pl.BlockSpec((tm, 512), lambda i: (i, blk_b)),
                pl.BlockSpec((512, D), lambda i: (0, 0)),
                pl.BlockSpec((512, D), lambda i: (1, 0)),
                pl.BlockSpec((tm, D), row),
                pl.BlockSpec((1, D), lambda i: (0, 0))]
    out_specs = [pl.BlockSpec((tm, D), row), pl.BlockSpec((tm, D), row)]
    out_shape = [jax.ShapeDtypeStruct((S, D), F32), jax.ShapeDtypeStruct((S, D), F32)]
    args = [og_a, og_b, w, w, x, g]
    if with_loss:
        in_specs.append(pl.BlockSpec((tm, D), row))
        out_specs.append(pl.BlockSpec((1, D), lambda i: (0, 0)))
        out_shape.append(jax.ShapeDtypeStruct((1, D), F32))
        args.append(target)
    return pl.pallas_call(
        body, name=name, grid=(S // tm,), in_specs=in_specs, out_specs=out_specs, out_shape=out_shape,
        compiler_params=_cparams(("arbitrary",)),
    )(*args)


def _out_proj_bwd(dx_up, y, g, w, proj, gate_offs, o_a, o_b, oblk_a, oblk_b, name):
    S, D = y.shape
    tm = _pick(S, (256,))
    gblk = [off // 256 + c for off in gate_offs for c in range(2)]

    def body(u_ref, y_ref, g_ref, w_ref, g0, g1, g2, g3, oa_ref, ob_ref, dy_ref, do_ref, dgate_ref, dg_ref):
        @pl.when(pl.program_id(0) == 0)
        def _():
            dg_ref[...] = jnp.zeros_like(dg_ref)
        dy, dgrow = _norm_bwd(y_ref[...], g_ref[...], u_ref[...])
        dg_ref[...] += jnp.sum(dgrow, axis=0, keepdims=True)
        dyb = dy.astype(BF16)
        dy_ref[...] = dyb
        dog = _dot_nt(dyb, w_ref[...])
        gates = (g0, g1, g2, g3)
        for c in range(4):
            gt = gates[c][...]
            sg = _sigmoid(gt)
            o_ref = oa_ref if c < 2 else ob_ref
            ov = o_ref[:, (c % 2) * 256:(c % 2 + 1) * 256]
            dc = dog[:, c * 256:(c + 1) * 256]
            do_ref[:, c * 256:(c + 1) * 256] = dc * (gt * sg)
            dgate_ref[:, c * 256:(c + 1) * 256] = dc * ov * (sg * (1.0 + gt * (1.0 - sg)))

    row = lambda i: (i, 0)
    gspec = lambda c: pl.BlockSpec((tm, 256), lambda i: (i, gblk[c]))
    return pl.pallas_call(
        body, name=name, grid=(S // tm,),
        in_specs=[pl.BlockSpec((tm, D), row), pl.BlockSpec((tm, D), row), pl.BlockSpec((1, D), lambda i: (0, 0)),
                  pl.BlockSpec((D, D), lambda i: (0, 0)),
                  gspec(0), gspec(1), gspec(2), gspec(3),
                  pl.BlockSpec((tm, 512), lambda i: (i, oblk_a)),
                  pl.BlockSpec((tm, 512), lambda i: (i, oblk_b))],
        out_specs=[pl.BlockSpec((tm, D), row), pl.BlockSpec((tm, D), row), pl.BlockSpec((tm, D), row),
                   pl.BlockSpec((1, D), lambda i: (0, 0))],
        out_shape=[jax.ShapeDtypeStruct((S, D), BF16), jax.ShapeDtypeStruct((S, D), F32),
                   jax.ShapeDtypeStruct((S, D), F32), jax.ShapeDtypeStruct((1, D), F32)],
        compiler_params=_cparams(("arbitrary",)),
    )(dx_up, y, g, w, proj, proj, proj, proj, o_a, o_b)


def _rope_tables(pos, invf, name):
    S = pos.shape[0]
    tm = _pick(S, (512, 256))

    def body(p_ref, f_ref, c_ref, s1_ref, s2_ref):
        lane = lax.broadcasted_iota(jnp.int32, (1, LANES), 1)
        ang = p_ref[...].astype(F32) * f_ref[...]
        c, s = jnp.cos(ang), jnp.sin(ang)
        c_ref[...] = jnp.where((lane >= 64) & (lane < 96), c, 1.0)
        s1_ref[...] = jnp.where((lane >= 64) & (lane < 80), -s, 0.0)
        s2_ref[...] = jnp.where((lane >= 80) & (lane < 96), s, 0.0)

    spec = pl.BlockSpec((tm, LANES), lambda i: (i, 0))
    return pl.pallas_call(
        body, name=name, grid=(S // tm,),
        in_specs=[pl.BlockSpec((tm, 1), lambda i: (i, 0)), pl.BlockSpec((1, LANES), lambda i: (0, 0))],
        out_specs=[spec, spec, spec],
        out_shape=[jax.ShapeDtypeStruct((S, LANES), F32)] * 3,
        compiler_params=_cparams(("parallel",)),
    )(pos, invf)


def _rope(x, c, s1, s2):
    return x * c + pltpu.roll(x, LANES - 16, 1) * s1 + pltpu.roll(x, 16, 1) * s2


def _rope_t(d, c, s1, s2):
    return d * c + pltpu.roll(d * s1, 16, 1) + pltpu.roll(d * s2, LANES - 16, 1)


def _mla_prep(proj, gq, gkv, wq, wkv, cosT, s1T, s2T, name):
    S = proj.shape[0]
    tm = _pick(S, (256,))

    def body(p_ref, gq_ref, gkv_ref, wq_ref, wkv_ref, c_ref, s1_ref, s2_ref, q_ref, k_ref, v_ref, qn_ref, cn_ref):
        qa = p_ref[:, 0:384]
        ckv = p_ref[:, 384:640]
        kr = p_ref[:, 640:768]
        qn32 = (qa * _rstd(qa)) * gq_ref[...]
        cn32 = (ckv * _rstd(ckv)) * gkv_ref[...]
        qn = qn32.astype(BF16)
        cn = cn32.astype(BF16)
        qn_ref[...] = qn32.T.astype(BF16)
        cn_ref[...] = cn32.T.astype(BF16)
        qb = _dot(qn, wq_ref[...])
        kvb = _dot(cn, wkv_ref[...])
        c, s1, s2 = c_ref[...], s1_ref[...], s2_ref[...]
        krr = _rope(kr, c, s1, s2)
        for h in range(8):
            sl = slice(h * LANES, (h + 1) * LANES)
            q_ref[:, sl] = _rope(qb[:, sl], c, s1, s2)
            k_ref[:, sl] = kvb[:, sl] + krr
        v_ref[...] = kvb[:, 1024:1536]

    row = lambda i: (i, 0)
    fixed = lambda i: (0, 0)
    tspec = pl.BlockSpec((tm, LANES), row)
    return pl.pallas_call(
        body, name=name, grid=(S // tm,),
        in_specs=[pl.BlockSpec((tm, L0_PREP_W), lambda i: (i, L0_PREP // L0_PREP_W)),
                  pl.BlockSpec((1, 384), fixed), pl.BlockSpec((1, 256), fixed),
                  pl.BlockSpec((384, 1024), fixed), pl.BlockSpec((256, 1536), fixed), tspec, tspec, tspec],
        out_specs=[pl.BlockSpec((tm, 1024), row), pl.BlockSpec((tm, 1024), row), pl.BlockSpec((tm, 512), row),
                   pl.BlockSpec((384, tm), lambda i: (0, i)), pl.BlockSpec((256, tm), lambda i: (0, i))],
        out_shape=[jax.ShapeDtypeStruct((S, 1024), F32), jax.ShapeDtypeStruct((S, 1024), F32),
                   jax.ShapeDtypeStruct((S, 512), F32), jax.ShapeDtypeStruct((384, S), BF16),
                   jax.ShapeDtypeStruct((256, S), BF16)],
        compiler_params=_cparams(("parallel",)),
    )(proj, gq, gkv, wq, wkv, cosT, s1T, s2T)


def _mla_prep_bwd(dq, dk, dv, proj, gq, gkv, wq, wkv, cosT, s1T, s2T, name):
    S = proj.shape[0]
    tm = _pick(S, (256,))

    def body(dq_ref, dk_ref, dv_ref, p_ref, gq_ref, gkv_ref, wq_ref, wkv_ref, c_ref, s1_ref, s2_ref,
             dp_ref, dqb_ref, dkvb_ref, dgq_ref, dgkv_ref):
        @pl.when(pl.program_id(0) == 0)
        def _():
            dgq_ref[...] = jnp.zeros_like(dgq_ref)
            dgkv_ref[...] = jnp.zeros_like(dgkv_ref)
        c, s1, s2 = c_ref[...], s1_ref[...], s2_ref[...]
        lane = lax.broadcasted_iota(jnp.int32, (1, LANES), 1)
        dkr = jnp.zeros((tm, LANES), F32)
        for h in range(8):
            sl = slice(h * LANES, (h + 1) * LANES)
            dqb_ref[:, sl] = _rope_t(dq_ref[:, sl], c, s1, s2).astype(BF16)
            dkh = dk_ref[:, sl]
            dkvb_ref[:, sl] = dkh.astype(BF16)
            dkr = dkr + dkh
        dkvb_ref[:, 1024:1536] = dv_ref[...].astype(BF16)
        dkr = jnp.where((lane >= 64) & (lane < 96), _rope_t(dkr, c, s1, s2), 0.0)
        dqn = _dot_nt(dqb_ref[...], wq_ref[...])
        dcn = _dot_nt(dkvb_ref[...], wkv_ref[...])
        dqa, gq_row = _norm_bwd(p_ref[:, 0:384], gq_ref[...], dqn)
        dckv, gkv_row = _norm_bwd(p_ref[:, 384:640], gkv_ref[...], dcn)
        dp_ref[:, 0:384] = dqa
        dp_ref[:, 384:640] = dckv
        dp_ref[:, 640:768] = dkr
        dgq_ref[...] += jnp.sum(gq_row, axis=0, keepdims=True)
        dgkv_ref[...] += jnp.sum(gkv_row, axis=0, keepdims=True)

    row = lambda i: (i, 0)
    fixed = lambda i: (0, 0)
    tspec = pl.BlockSpec((tm, LANES), row)
    return pl.pallas_call(
        body, name=name, grid=(S // tm,),
        in_specs=[pl.BlockSpec((tm, 1024), row), pl.BlockSpec((tm, 1024), row), pl.BlockSpec((tm, 512), row),
                  pl.BlockSpec((tm, L0_PREP_W), lambda i: (i, L0_PREP // L0_PREP_W)),
                  pl.BlockSpec((1, 384), fixed), pl.BlockSpec((1, 256), fixed),
                  pl.BlockSpec((384, 1024), fixed), pl.BlockSpec((256, 1536), fixed), tspec, tspec, tspec],
        out_specs=[pl.BlockSpec((tm, L0_PREP_W), row), pl.BlockSpec((tm, 1024), row), pl.BlockSpec((tm, 1536), row),
                   pl.BlockSpec((1, 384), fixed), pl.BlockSpec((1, 256), fixed)],
        out_shape=[jax.ShapeDtypeStruct((S, L0_PREP_W), F32), jax.ShapeDtypeStruct((S, 1024), BF16),
                   jax.ShapeDtypeStruct((S, 1536), BF16), jax.ShapeDtypeStruct((1, 384), F32),
                   jax.ShapeDtypeStruct((1, 256), F32)],
        compiler_params=_cparams(("arbitrary",)),
    )(dq, dk, dv, proj, gq, gkv, wq, wkv, cosT, s1T, s2T)


def _fox_prep(proj, bf, name):
    S = proj.shape[0]
    tm = _pick(S, (256,))

    def body(f_ref, b_ref, c_ref, carry_ref):
        @pl.when(pl.program_id(0) == 0)
        def _():
            carry_ref[...] = jnp.zeros_like(carry_ref)
        u = f_ref[...] + b_ref[...]
        lf = jnp.minimum(u, 0.0) - jnp.log(1.0 + jnp.exp(-jnp.abs(u)))
        r = lax.broadcasted_iota(jnp.int32, (tm, tm), 0)
        cidx = lax.broadcasted_iota(jnp.int32, (tm, tm), 1)
        tri = (cidx <= r).astype(BF16)
        hi, mid, lo = _split3(lf)
        c = carry_ref[...] + (_dot(tri, hi) + _dot(tri, mid) + _dot(tri, lo))
        c_ref[...] = c
        carry_ref[...] = c[tm - 1:tm, :]

    return pl.pallas_call(
        body, name=name, grid=(S // tm,),
        in_specs=[pl.BlockSpec((tm, LANES), lambda i: (i, L1_F // LANES)), pl.BlockSpec((1, LANES), lambda i: (0, 0))],
        out_specs=pl.BlockSpec((tm, LANES), lambda i: (i, 0)),
        out_shape=jax.ShapeDtypeStruct((S, LANES), F32),
        scratch_shapes=[pltpu.VMEM((1, LANES), F32)],
        compiler_params=_cparams(("arbitrary",)),
    )(proj, bf)


def _fox_prep_bwd(dc, proj, bf, name):
    S = proj.shape[0]
    tm = _pick(S, (256,))
    nb = S // tm

    def body(dc_ref, f_ref, b_ref, df_ref, db_ref, carry_ref):
        @pl.when(pl.program_id(0) == 0)
        def _():
            carry_ref[...] = jnp.zeros_like(carry_ref)
            db_ref[...] = jnp.zeros_like(db_ref)
        r = lax.broadcasted_iota(jnp.int32, (tm, tm), 0)
        cidx = lax.broadcasted_iota(jnp.int32, (tm, tm), 1)
        tri = (cidx >= r).astype(BF16)
        hi, mid, lo = _split3(dc_ref[...])
        dlf = carry_ref[...] + (_dot(tri, hi) + _dot(tri, mid) + _dot(tri, lo))
        carry_ref[...] = dlf[0:1, :]
        u = f_ref[...] + b_ref[...]
        e = jnp.exp(-jnp.abs(u))
        sneg = jnp.where(u >= 0.0, e, 1.0) / (1.0 + e)
        lane = lax.broadcasted_iota(jnp.int32, (1, LANES), 1)
        df = jnp.where(lane < FOX_HEADS, dlf * sneg, 0.0)
        df_ref[...] = df
        db_ref[...] += jnp.sum(df, axis=0, keepdims=True)

    return pl.pallas_call(
        body, name=name, grid=(nb,),
        in_specs=[pl.BlockSpec((tm, LANES), lambda i: (nb - 1 - i, 0)),
                  pl.BlockSpec((tm, LANES), lambda i: (nb - 1 - i, L1_F // LANES)),
                  pl.BlockSpec((1, LANES), lambda i: (0, 0))],
        out_specs=[pl.BlockSpec((tm, LANES), lambda i: (nb - 1 - i, 0)), pl.BlockSpec((1, LANES), lambda i: (0, 0))],
        out_shape=[jax.ShapeDtypeStruct((S, LANES), F32), jax.ShapeDtypeStruct((1, LANES), F32)],
        scratch_shapes=[pltpu.VMEM((1, LANES), F32)],
        compiler_params=_cparams(("arbitrary",)),
    )(dc, proj, bf)


def _att_specs(kind, S, T):
    if kind == "sb":
        qo, ko, vo, go = L0_SBQ // LANES, L0_SBK // LANES, L0_SBV // LANES, L0_SBG // LANES
    elif kind == "fox":
        qo, ko, vo, go = L1_Q // LANES, L1_K // LANES, L1_V // LANES, L1_G // LANES
    else:
        go = L0_MLG // LANES
        return (pl.BlockSpec((T, 256), lambda p, i: (i, p)), pl.BlockSpec((S, 256), lambda p, i: (0, p)),
                pl.BlockSpec((S, LANES), lambda p, i: (0, p)), pl.BlockSpec((T, LANES), lambda p, i: (i, go + p)))
    return (pl.BlockSpec((T, LANES), lambda p, i: (i, qo + p)), pl.BlockSpec((S, LANES), lambda p, i: (0, ko + p)),
            pl.BlockSpec((S, LANES), lambda p, i: (0, vo + p)), pl.BlockSpec((T, LANES), lambda p, i: (i, go + p)))


def _per_q_tile(tile_body, hows):
    T = ATT_T

    def view(ref, u, how):
        if how == "rows":
            return ref.at[pl.ds(u * T, T)]
        if how == "lanes":
            return ref.at[:, pl.ds(u * T, T)]
        if how == "stat":
            return ref.at[:, u]
        return ref

    def body(*refs):
        for u in range(ATT_QSUB):
            tile_body(pl.program_id(1) * ATT_QSUB + u, *[view(r, u, how) for r, how in zip(refs, hows)])

    return body


def _mask_flags(js, masked_at):
    return [t == masked_at for t in range(len(js))]


def _loop_tiles(i, tiles, right_to_left, G=ATT_GROUP):
    ng = i // G
    rest = i - ng * G

    def leftover():
        for r in range(G):
            @pl.when(rest == r)
            def _():
                if right_to_left:
                    tiles([i - u for u in range(r + 1)], 0)
                else:
                    tiles([ng * G + u for u in range(r + 1)], r)

    def group(g, carry):
        if right_to_left:
            tiles([ng * G - 1 - (g * G + u) for u in range(G)], None)
        else:
            tiles([g * G + u for u in range(G)], None)
        return carry

    if right_to_left:
        leftover()
    lax.fori_loop(0, ng, group, 0)
    if not right_to_left:
        leftover()


def _head_q(kind, q_ref, m0, scale):
    if kind == "mla":
        return [q_ref[:, 0:LANES].astype(BF16), q_ref[:, LANES:2 * LANES].astype(BF16)]
    qv = q_ref[...] * scale
    return [jnp.where(m0, qv, 0.0).astype(BF16), jnp.where(m0, 0.0, qv).astype(BF16)]


def _head_k(kind, k_ref, start, T):
    if kind == "mla":
        return [k_ref[pl.ds(start, T), 0:LANES].astype(BF16), k_ref[pl.ds(start, T), LANES:2 * LANES].astype(BF16)]
    kb = k_ref[pl.ds(start, T), :].astype(BF16)
    return [kb, kb]


def _softmax_fwd(kind, qkvg, c_col, S, npairs, name):
    T = ATT_T
    nq = S // T
    fox = kind == "fox"
    scale = (96 if kind == "mla" else 64) ** -0.5

    def body(i, *refs):
        if fox:
            q_ref, k_ref, v_ref, g_ref, cc_ref, o_ref, og_ref, ogt_ref, st_ref, m_ref, acc_ref = refs
        else:
            q_ref, k_ref, v_ref, g_ref, o_ref, og_ref, ogt_ref, st_ref, m_ref, acc_ref = refs
        m0 = lax.broadcasted_iota(jnp.int32, (1, LANES), 1) < 64
        top = lax.broadcasted_iota(jnp.int32, (LANES, 1), 0) < 64
        key = lax.broadcasted_iota(jnp.int32, (T, LANES), 0)
        qrow = lax.broadcasted_iota(jnp.int32, (T, LANES), 1)
        qh = _head_q(kind, q_ref, m0, scale)
        m_ref[...] = jnp.full(m_ref.shape, NEG, F32)
        acc_ref[...] = jnp.zeros(acc_ref.shape, F32)
        chains = [(h, b) for h in range(2) for b in range(T // LANES)]

        def tiles(js, masked_at):
            starts = [pl.multiple_of(j * T, T) for j in js]
            zss = []
            for start in starts:
                kh = _head_k(kind, k_ref, start, T)
                zss.append(_split_blocks([_dot_nt(kh[h], qh[h]) for h in range(2)]))
            pss, alss = [], []
            for start, zs, masked in zip(starts, zss, _mask_flags(js, masked_at)):
                ps, alphas = [], []
                for (h, b), z in zip(chains, zs):
                    lanes = slice(b * LANES, (b + 1) * LANES)
                    if kind == "mla":
                        z = z * scale
                    if fox:
                        z = z - cc_ref[h, pl.ds(start, T), :]
                    if masked:
                        z = jnp.where(key <= qrow + b * LANES, z, NEG)
                    m_prev = m_ref[h, :, lanes]
                    m_new = jnp.maximum(m_prev, jnp.max(z, axis=0, keepdims=True))
                    alphas.append(jnp.exp(m_prev - m_new))
                    ps.append(jnp.exp(z - m_new).astype(BF16))
                    m_ref[h, :, lanes] = m_new
                pss.append(_join_blocks(ps, T // LANES))
                alss.append(_join_blocks(alphas, T // LANES))
            for start, ps, alphas in zip(starts, pss, alss):
                vt = v_ref[pl.ds(start, T), :].T
                vth = [jnp.where(top, vt, 1.0).astype(BF16), jnp.where(top, 1.0, vt).astype(BF16)]
                for h in range(2):
                    acc_ref[h] = alphas[h] * acc_ref[h] + _dot(vth[h], ps[h])

        _loop_tiles(i, tiles, False, 2 * ATT_GROUP)
        acc = [acc_ref[0], acc_ref[1]]
        ot = jnp.concatenate([acc[0][0:64] / acc[0][64:128], acc[1][64:128] / acc[1][0:64]], axis=0)
        o = ot.T
        o_ref[...] = o
        gt = g_ref[...]
        og = o * (gt * _sigmoid(gt))
        og_ref[...] = og.astype(BF16)
        ogt_ref[...] = og.T.astype(BF16)
        st_ref[0] = m_ref[0] + jnp.log(acc[0][64:65])
        st_ref[1] = m_ref[1] + jnp.log(acc[1][0:1])

    QT = ATT_QSUB * T
    qs, ks, vs, gs = _att_specs(kind, S, QT)
    in_specs = [qs, ks, vs, gs]
    args = list(qkvg)
    hows = ["rows", None, None, "rows"]
    if fox:
        in_specs += [pl.BlockSpec((2, S, LANES), lambda p, i: (p, 0, 0))]
        args += [c_col]
        hows += [None]
    hows += ["rows", "rows", "lanes", "stat", None, None]
    W = npairs * LANES
    return pl.pallas_call(
        _per_q_tile(body, hows), name=name, grid=(npairs, nq // ATT_QSUB), in_specs=in_specs,
        out_specs=[pl.BlockSpec((QT, LANES), lambda p, i: (i, p)), pl.BlockSpec((QT, LANES), lambda p, i: (i, p)),
                   pl.BlockSpec((LANES, QT), lambda p, i: (p, i)),
                   pl.BlockSpec((2, ATT_QSUB, 1, T), lambda p, i: (p, i, 0, 0))],
        out_shape=[jax.ShapeDtypeStruct((S, W), F32), jax.ShapeDtypeStruct((S, W), BF16),
                   jax.ShapeDtypeStruct((W, S), BF16),
                   jax.ShapeDtypeStruct((2 * npairs, nq, 1, T), F32)],
        scratch_shapes=[pltpu.VMEM((2, 1, T), F32), pltpu.VMEM((2, LANES, T), F32)],
        compiler_params=_cparams(("parallel", "parallel")),
    )(*args)


def _softplus_parts(z):
    sp = jnp.maximum(z, 0.0) + jnp.log(1.0 + jnp.exp(-jnp.abs(z)))
    return sp, z - sp


def _cumsum_dot(tri2, his, los):
    return _split_blocks([_dot(tri2, jnp.concatenate([hi, lo], axis=0)) for hi, lo in zip(his, los)])


def _split2(x):
    hi = x.astype(BF16)
    return hi, (x - hi.astype(F32)).astype(BF16)


def _split_blocks(per_head):
    return [x[:, b * LANES:(b + 1) * LANES] for x in per_head for b in range(x.shape[1] // LANES)]


def _join_blocks(per_block, nb):
    return [jnp.concatenate(per_block[h * nb:(h + 1) * nb], axis=1) for h in range(len(per_block) // nb)]


def _row_of(col):
    return jnp.broadcast_to(col, (col.shape[0], LANES)).T[0:1]


def _softmax_bwd_t(kind, q, k, v, do, do_off, o, lse, c_col, S, npairs, name):
    T = ATT_T
    nq = S // T
    nb = T // LANES
    fox = kind == "fox"
    mla = kind == "mla"
    scale = (96 if mla else 64) ** -0.5
    kw = 256 if mla else LANES

    def body(i, *refs):
        if fox:
            (q_ref, k_ref, v_ref, do_ref, o_ref, st_ref, cc_ref,
             dq_ref, dk_ref, dv_ref, dck_ref, dcq_ref, dqt_ref, rs_ref, dkx_ref) = refs
        else:
            q_ref, k_ref, v_ref, do_ref, o_ref, st_ref, dq_ref, dk_ref, dv_ref, dqt_ref = refs

        @pl.when(i == 0)
        def _():
            dv_ref[...] = jnp.zeros_like(dv_ref)
            if fox:
                dkx_ref[...] = jnp.zeros_like(dkx_ref)
            else:
                dk_ref[...] = jnp.zeros_like(dk_ref)

        m0 = lax.broadcasted_iota(jnp.int32, (1, LANES), 1) < 64
        top = lax.broadcasted_iota(jnp.int32, (LANES, 1), 0) < 64
        key = lax.broadcasted_iota(jnp.int32, (T, LANES), 0)
        qrow = lax.broadcasted_iota(jnp.int32, (T, LANES), 1)
        qh = _head_q(kind, q_ref, m0, scale)
        if fox:
            qv = q_ref[...] * scale
            qk = [jnp.where(m0, qv, 1.0).astype(BF16), jnp.where(m0, 1.0, qv).astype(BF16)]
        else:
            qk = qh
        dov = do_ref[...]
        prod = dov * o_ref[...]
        dd = [_row_of(jnp.sum(jnp.where(m0, prod, 0.0), axis=1, keepdims=True)),
              _row_of(jnp.sum(jnp.where(m0, 0.0, prod), axis=1, keepdims=True))]
        doh = [jnp.where(m0, dov, 0.0).astype(BF16), jnp.where(m0, 0.0, dov).astype(BF16)]
        lse = [st_ref[0], st_ref[1]]
        dqt_ref[...] = jnp.zeros_like(dqt_ref)
        if fox:
            rs_ref[...] = jnp.zeros_like(rs_ref)
        chains = [(h, b) for h in range(2) for b in range(nb)]

        def tiles(js, masked_at):
            starts = [pl.multiple_of(j * T, T) for j in js]
            zss, dpss = [], []
            for start in starts:
                vb = v_ref[pl.ds(start, T), :].astype(BF16)
                kh = _head_k(kind, k_ref, start, T)
                zss.append(_split_blocks([_dot_nt(kh[h], qh[h]) for h in range(2)]))
                dpss.append(_split_blocks([_dot_nt(vb, doh[h]) for h in range(2)]))
            pss, dsss = [], []
            for start, zs, dps, masked in zip(starts, zss, dpss, _mask_flags(js, masked_at)):
                ps, dss = [], []
                for (h, b), z, dp in zip(chains, zs, dps):
                    lanes = slice(b * LANES, (b + 1) * LANES)
                    if mla:
                        z = z * scale
                    if fox:
                        z = z - cc_ref[h, pl.ds(start, T), :]
                    if masked:
                        z = jnp.where(key <= qrow + b * LANES, z, NEG)
                    p = jnp.exp(z - lse[h][:, lanes])
                    ds = p * (dp - dd[h][:, lanes])
                    dsb = ds.astype(BF16)
                    if fox:
                        rs_ref[h, :, lanes] += jnp.sum(dsb.astype(F32), axis=0, keepdims=True)
                    ps.append(p.astype(BF16))
                    dss.append(dsb)
                pss.append(_join_blocks(ps, nb))
                dsss.append(_join_blocks(dss, nb))
            for start, ps, dss in zip(starts, pss, dsss):
                kt = k_ref[pl.ds(start, T), :].T.astype(BF16)
                dvc = None
                for h in range(2):
                    dkh = _dot(dss[h], qk[h])
                    dvh = _dot(ps[h], doh[h])
                    dvc = dvh if dvc is None else dvc + dvh
                    kth = kt[h * LANES:(h + 1) * LANES] if mla else kt
                    dqt_ref[h] += _dot(kth, dss[h])
                    if fox:
                        dkx_ref[h, pl.ds(start, T), :] += dkh
                    elif mla:
                        dk_ref[pl.ds(start, T), h * LANES:(h + 1) * LANES] += dkh * scale
                    else:
                        dk_ref[pl.ds(start, T), :] += dkh
                dv_ref[pl.ds(start, T), :] += dvc

        _loop_tiles(i, tiles, False)
        if mla:
            dq_ref[:, 0:LANES] = dqt_ref[0].T * scale
            dq_ref[:, LANES:2 * LANES] = dqt_ref[1].T * scale
        else:
            dq_ref[...] = jnp.where(top, dqt_ref[0], dqt_ref[1]).T * scale
        if fox:
            dcq_ref[0] = rs_ref[0]
            dcq_ref[1] = rs_ref[1]

            @pl.when(i == nq - 1)
            def _():
                dk_ref[...] = jnp.where(m0, dkx_ref[0], dkx_ref[1])
                dck_ref[0] = dkx_ref[0].T[64:65]
                dck_ref[1] = dkx_ref[1].T[0:1]

    QT = ATT_QSUB * T
    qs, ks, vs, _ = _att_specs(kind, S, QT)
    stat = pl.BlockSpec((2, ATT_QSUB, 1, T), lambda p, i: (p, i, 0, 0))
    in_specs = [qs, ks, vs,
                pl.BlockSpec((QT, LANES), lambda p, i: (i, do_off + p)),
                pl.BlockSpec((QT, LANES), lambda p, i: (i, p)), stat]
    args = [q, k, v, do, o, lse]
    hows = ["rows", None, None, "rows", "rows", "stat"]
    W = npairs * LANES
    out_specs = [pl.BlockSpec((QT, kw), lambda p, i: (i, p)), pl.BlockSpec((S, kw), lambda p, i: (0, p)),
                 pl.BlockSpec((S, LANES), lambda p, i: (0, p))]
    out_shape = [jax.ShapeDtypeStruct((S, npairs * kw), F32), jax.ShapeDtypeStruct((S, npairs * kw), F32),
                 jax.ShapeDtypeStruct((S, W), F32)]
    scratch = [pltpu.VMEM((2, LANES, T), F32)]
    if fox:
        in_specs.append(pl.BlockSpec((2, S, LANES), lambda p, i: (p, 0, 0)))
        args.append(c_col)
        out_specs += [pl.BlockSpec((2, 1, S), lambda p, i: (p, 0, 0)), stat]
        out_shape += [jax.ShapeDtypeStruct((2 * npairs, 1, S), F32), jax.ShapeDtypeStruct((2 * npairs, nq, 1, T), F32)]
        scratch += [pltpu.VMEM((2, 1, T), F32), pltpu.VMEM((2, S, LANES), F32)]
        hows += [None, "rows", None, None, None, "stat", None, None, None]
    else:
        hows += ["rows", None, None, None]
    return pl.pallas_call(
        _per_q_tile(body, hows), name=name, grid=(npairs, nq // ATT_QSUB), in_specs=in_specs, out_specs=out_specs,
        out_shape=out_shape, scratch_shapes=scratch, compiler_params=_cparams(("parallel", "arbitrary")),
    )(*args)


def _sb_fwd_t(proj, S, npairs, name):
    T = ATT_T
    nq = S // T
    nb = T // LANES
    scale = 64 ** -0.5

    def body(i, q_ref, k_ref, v_ref, g_ref, o_ref, og_ref, ogt_ref, st_ref, rem_ref, acc_ref):
        m0 = lax.broadcasted_iota(jnp.int32, (1, LANES), 1) < 64
        top = lax.broadcasted_iota(jnp.int32, (LANES, 1), 0) < 64
        key = lax.broadcasted_iota(jnp.int32, (T, LANES), 0)
        qrow = lax.broadcasted_iota(jnp.int32, (T, LANES), 1)
        r = lax.broadcasted_iota(jnp.int32, (T, T), 0)
        c = lax.broadcasted_iota(jnp.int32, (T, T), 1)
        after = (c > r).astype(BF16)
        after2 = jnp.concatenate([after, after], axis=1)
        qh = _head_q("sb", q_ref, m0, scale)
        rem_ref[...] = jnp.zeros_like(rem_ref)
        acc_ref[...] = jnp.zeros_like(acc_ref)
        chains = [(h, b) for h in range(2) for b in range(nb)]

        def tiles(js, masked_at):
            zss = []
            for j in js:
                kb = k_ref[pl.ds(pl.multiple_of(j * T, T), T), :].astype(BF16)
                zss.append(_split_blocks([_dot_nt(kb, qh[h]) for h in range(2)]))
            lass, sums, hiss, loss = [], [], [], []
            for zs, masked in zip(zss, _mask_flags(js, masked_at)):
                las, sm, his, los = [], [], [], []
                for (h, b), z in zip(chains, zs):
                    sp, la = _softplus_parts(z)
                    if masked:
                        sp = jnp.where(key < qrow + b * LANES, sp, 0.0)
                    hi, lo = _split2(sp)
                    las.append(la)
                    sm.append(jnp.sum(sp, axis=0, keepdims=True))
                    his.append(hi)
                    los.append(lo)
                lass.append(las)
                sums.append(sm)
                hiss.append(_join_blocks(his, nb))
                loss.append(_join_blocks(los, nb))
            rcss = [_cumsum_dot(after2, his, los) for his, los in zip(hiss, loss)]
            wss = []
            for las, sm, rcs, masked in zip(lass, sums, rcss, _mask_flags(js, masked_at)):
                ws = []
                for (h, b), la, s, rc in zip(chains, las, sm, rcs):
                    lanes = slice(b * LANES, (b + 1) * LANES)
                    w = jnp.exp(la - (rem_ref[h, :, lanes] + rc))
                    if masked:
                        w = jnp.where(key < qrow + b * LANES, w, 0.0)
                    ws.append(w.astype(BF16))
                    rem_ref[h, :, lanes] += s
                wss.append(_join_blocks(ws, nb))
            for j, ws in zip(js, wss):
                vtb = v_ref[pl.ds(pl.multiple_of(j * T, T), T), :].T.astype(BF16)
                for h in range(2):
                    acc_ref[h] += _dot(vtb, ws[h])

        _loop_tiles(i, tiles, True)
        o = jnp.where(top, acc_ref[0], acc_ref[1]).T
        o_ref[...] = o
        gt = g_ref[...]
        og = o * (gt * _sigmoid(gt))
        og_ref[...] = og.astype(BF16)
        ogt_ref[...] = og.T.astype(BF16)
        st_ref[0] = rem_ref[0]
        st_ref[1] = rem_ref[1]

    QT = ATT_QSUB * T
    qs, ks, vs, gs = _att_specs("sb", S, QT)
    W = npairs * LANES
    hows = ["rows", None, None, "rows", "rows", "rows", "lanes", "stat", None, None]
    return pl.pallas_call(
        _per_q_tile(body, hows), name=name, grid=(npairs, nq // ATT_QSUB),
        in_specs=[qs, ks, vs, gs],
        out_specs=[pl.BlockSpec((QT, LANES), lambda p, i: (i, p)), pl.BlockSpec((QT, LANES), lambda p, i: (i, p)),
                   pl.BlockSpec((LANES, QT), lambda p, i: (p, i)),
                   pl.BlockSpec((2, ATT_QSUB, 1, T), lambda p, i: (p, i, 0, 0))],
        out_shape=[jax.ShapeDtypeStruct((S, W), F32), jax.ShapeDtypeStruct((S, W), BF16),
                   jax.ShapeDtypeStruct((W, S), BF16),
                   jax.ShapeDtypeStruct((2 * npairs, nq, 1, T), F32)],
        scratch_shapes=[pltpu.VMEM((2, 1, T), F32), pltpu.VMEM((2, LANES, T), F32)],
        compiler_params=_cparams(("parallel", "parallel")),
    )(proj, proj, proj, proj)


def _sb_bwd_t(proj, do, tot, S, npairs, name):
    T = ATT_T
    nq = S // T
    nb = T // LANES
    scale = 64 ** -0.5

    def body(i, q_ref, k_ref, v_ref, do_ref, st_ref, dq_ref, dk_ref, dv_ref, dqt_ref, pre_ref, gpre_ref):

        @pl.when(i == 0)
        def _():
            dk_ref[...] = jnp.zeros_like(dk_ref)
            dv_ref[...] = jnp.zeros_like(dv_ref)

        m0 = lax.broadcasted_iota(jnp.int32, (1, LANES), 1) < 64
        top = lax.broadcasted_iota(jnp.int32, (LANES, 1), 0) < 64
        key = lax.broadcasted_iota(jnp.int32, (T, LANES), 0)
        qrow = lax.broadcasted_iota(jnp.int32, (T, LANES), 1)
        r = lax.broadcasted_iota(jnp.int32, (T, T), 0)
        c = lax.broadcasted_iota(jnp.int32, (T, T), 1)
        upto = (c <= r).astype(BF16)
        upto2 = jnp.concatenate([upto, upto], axis=1)
        left = (c < r).astype(BF16)
        qh = _head_q("sb", q_ref, m0, scale)
        dov = do_ref[...]
        doh = [jnp.where(m0, dov, 0.0).astype(BF16), jnp.where(m0, 0.0, dov).astype(BF16)]
        tot_h = [st_ref[0], st_ref[1]]
        dqt_ref[...] = jnp.zeros_like(dqt_ref)
        pre_ref[...] = jnp.zeros_like(pre_ref)
        gpre_ref[...] = jnp.zeros_like(gpre_ref)
        chains = [(h, b) for h in range(2) for b in range(nb)]

        def tiles(js, masked_at):
            starts = [pl.multiple_of(j * T, T) for j in js]
            zss, dwss = [], []
            for start in starts:
                vb = v_ref[pl.ds(start, T), :].astype(BF16)
                kb = k_ref[pl.ds(start, T), :].astype(BF16)
                zss.append(_split_blocks([_dot_nt(kb, qh[h]) for h in range(2)]))
                dwss.append(_split_blocks([_dot_nt(vb, doh[h]) for h in range(2)]))
            lass, sums, hiss, loss = [], [], [], []
            for zs, masked in zip(zss, _mask_flags(js, masked_at)):
                las, sm, his, los = [], [], [], []
                for (h, b), z in zip(chains, zs):
                    sp, la = _softplus_parts(z)
                    if masked:
                        sp = jnp.where(key < qrow + b * LANES, sp, 0.0)
                    hi, lo = _split2(sp)
                    las.append(la)
                    sm.append(jnp.sum(sp, axis=0, keepdims=True))
                    his.append(hi)
                    los.append(lo)
                lass.append(las)
                sums.append(sm)
                hiss.append(_join_blocks(his, nb))
                loss.append(_join_blocks(los, nb))
            pcss = [_cumsum_dot(upto2, his, los) for his, los in zip(hiss, loss)]
            wss, gss = [], []
            for las, sm, pcs, dws, masked in zip(lass, sums, pcss, dwss, _mask_flags(js, masked_at)):
                ws, gs = [], []
                for (h, b), la, s, pc, dw in zip(chains, las, sm, pcs, dws):
                    lanes = slice(b * LANES, (b + 1) * LANES)
                    w = jnp.exp(la - ((tot_h[h][:, lanes] - pre_ref[h, :, lanes]) - pc))
                    if masked:
                        w = jnp.where(key < qrow + b * LANES, w, 0.0)
                    ws.append(w.astype(BF16))
                    gs.append(dw * w)
                    pre_ref[h, :, lanes] += s
                wss.append(_join_blocks(ws, nb))
                gss.append(gs)
            gcss = [_split_blocks([_dot(left, g) for g in _join_blocks([g.astype(BF16) for g in gs], nb)]) for gs in gss]
            dzss = []
            for las, gs, gcs, masked in zip(lass, gss, gcss, _mask_flags(js, masked_at)):
                dzs = []
                for (h, b), la, g, gc in zip(chains, las, gs, gcs):
                    lanes = slice(b * LANES, (b + 1) * LANES)
                    dz = g - (g + (gpre_ref[h, :, lanes] + gc)) * jnp.exp(la)
                    if masked:
                        dz = jnp.where(key < qrow + b * LANES, dz, 0.0)
                    dzs.append(dz.astype(BF16))
                    gpre_ref[h, :, lanes] += jnp.sum(g, axis=0, keepdims=True)
                dzss.append(_join_blocks(dzs, nb))
            for start, ws, dzs in zip(starts, wss, dzss):
                kt = k_ref[pl.ds(start, T), :].T.astype(BF16)
                dkc = dvc = None
                for h in range(2):
                    dkh = _dot(dzs[h], qh[h])
                    dvh = _dot(ws[h], doh[h])
                    dkc = dkh if dkc is None else dkc + dkh
                    dvc = dvh if dvc is None else dvc + dvh
                    dqt_ref[h] += _dot(kt, dzs[h])
                dk_ref[pl.ds(start, T), :] += dkc
                dv_ref[pl.ds(start, T), :] += dvc

        _loop_tiles(i, tiles, False)
        dq_ref[...] = jnp.where(top, dqt_ref[0], dqt_ref[1]).T * scale

    QT = ATT_QSUB * T
    qs, ks, vs, _ = _att_specs("sb", S, QT)
    W = npairs * LANES
    hows = ["rows", None, None, "rows", "stat", "rows", None, None, None, None, None]
    return pl.pallas_call(
        _per_q_tile(body, hows), name=name, grid=(npairs, nq // ATT_QSUB),
        in_specs=[qs, ks, vs,
                  pl.BlockSpec((QT, LANES), lambda p, i: (i, p)),
                  pl.BlockSpec((2, ATT_QSUB, 1, T), lambda p, i: (p, i, 0, 0))],
        out_specs=[pl.BlockSpec((QT, LANES), lambda p, i: (i, p)), pl.BlockSpec((S, LANES), lambda p, i: (0, p)),
                   pl.BlockSpec((S, LANES), lambda p, i: (0, p))],
        out_shape=[jax.ShapeDtypeStruct((S, W), F32)] * 3,
        scratch_shapes=[pltpu.VMEM((2, LANES, T), F32), pltpu.VMEM((2, 1, T), F32), pltpu.VMEM((2, 1, T), F32)],
        compiler_params=_cparams(("parallel", "arbitrary")),
    )(proj, proj, proj, do, tot)


def _pad_w0(w):
    z = lambda n: jnp.zeros((w.shape[0], n), w.dtype)
    return jnp.concatenate([w[:, 2048:2432], w[:, 2432:2688], z(64), w[:, 2688:2720], z(32),
                            w[:, 1536:2048], w[:, 2720:3232], w[:, 0:512], w[:, 512:1024], w[:, 1024:1536]], axis=1)


def _unpad_w0(wp):
    return jnp.concatenate([wp[:, L0_SBQ:L0_SBQ + 512], wp[:, L0_SBK:L0_SBK + 512], wp[:, L0_SBV:L0_SBV + 512],
                            wp[:, L0_SBG:L0_SBG + 512], wp[:, 0:384], wp[:, 384:640], wp[:, 704:736],
                            wp[:, L0_MLG:L0_MLG + 512]], axis=1)


def _pad_wq(w):
    return jnp.pad(w.reshape(384, 8, 96), ((0, 0), (0, 0), (0, 32))).reshape(384, 1024)


def _unpad_wq(wp):
    return wp.reshape(384, 8, 128)[:, :, :96].reshape(384, 768)


def _pad_wkv(w):
    w3 = w.reshape(256, 8, 128)
    k = jnp.pad(w3[:, :, :64], ((0, 0), (0, 0), (0, 64))).reshape(256, 1024)
    return jnp.concatenate([k, w3[:, :, 64:].reshape(256, 512)], axis=1)


def _unpad_wkv(wp):
    k = wp[:, :1024].reshape(256, 8, 128)[:, :, :64]
    v = wp[:, 1024:].reshape(256, 8, 64)
    return jnp.concatenate([k, v], axis=-1).reshape(256, 1024)


def _pad_w1(w):
    return jnp.concatenate([w, jnp.zeros((w.shape[0], L1_WIDTH - ODD_IN_WIDTH), w.dtype)], axis=1)


def _local_step(x, positions, target, g, w0p, wqp, wkvp, wo0, w1p, wo1):
    S = x.shape[0]
    nq = S // ATT_T
    invf = ROPE_THETA ** (-jnp.arange(0, MLA_ROPE_DIM, 2, dtype=F32) / MLA_ROPE_DIM)
    invf = jnp.concatenate([jnp.zeros((64,), F32), invf, invf, jnp.zeros((32,), F32)]).reshape(1, LANES)
    cosT, s1T, s2T = _rope_tables(positions.reshape(S, 1), invf, "rope_tables")
    bfp = jnp.pad(g["l1_b_f"], ((0, 0), (0, LANES - FOX_HEADS)))

    if isinstance(w1p, tuple):
        w1_shard, finish_w1 = w1p
        proj0, h0t, w1_all = _norm_matmul(x, g["l0_pre_g"], w0p, "l0_in_proj", ride=w1_shard)
        w1p = finish_w1(w1_all)
    else:
        proj0, h0t = _norm_matmul(x, g["l0_pre_g"], w0p, "l0_in_proj")
    qm, km, vm, qnt, cnt = _mla_prep(proj0, g["l0_q_a_g"], g["l0_kv_a_g"], wqp, wkvp, cosT, s1T, s2T, "mla_prep")
    o_sb, og_sb, ogt_sb, tot_sb = _sb_fwd_t(proj0, S, 4, "sb_fwd")
    o_ml, og_ml, ogt_ml, lse_ml = _softmax_fwd("mla", (qm, km, vm, proj0), None, S, 4, "mla_fwd")
    y0, x1 = _out_proj(og_sb, og_ml, 0, 0, wo0, x, g["l0_post_g"], None, "l0_out_proj")

    proj1, h1t = _norm_matmul(x1, g["l1_pre_g"], w1p, "l1_in_proj")
    cfx = _fox_prep(proj1, bfp, "fox_prep")
    c16 = cfx[:, :FOX_HEADS].T
    c_col = jnp.broadcast_to(c16[:, :, None], (FOX_HEADS, S, LANES))
    o_fx, og_fx, ogt_fx, lse_fx = _softmax_fwd("fox", (proj1, proj1, proj1, proj1), c_col, S, 8, "fox_fwd")
    y1, dx2, lsum = _out_proj(og_fx, og_fx, 0, 1, wo1, x1, g["l1_post_g"], target, "l1_out_proj")

    dy1, do1, dgate1, d_post1 = _out_proj_bwd(dx2, y1, g["l1_post_g"], wo1, proj1, (L1_G, L1_G + 512), o_fx, o_fx, 0, 1, "l1_out_bwd")
    dwo1 = _matmul_t(ogt_fx, dy1, "l1_dw_out")
    dq1, dk1, dv1, dck, dcq = _softmax_bwd_t("fox", proj1, proj1, proj1, do1, 0, o_fx, lse_fx, c_col, S, 8,
                                             "fox_bwd")
    dc = jnp.pad((dcq.reshape(FOX_HEADS, S) - dck.reshape(FOX_HEADS, S)).T, ((0, 0), (0, LANES - FOX_HEADS)))
    df, d_bf = _fox_prep_bwd(dc, proj1, bfp, "fox_prep_bwd")
    pieces1 = [(L1_Q, dq1), (L1_K, dk1), (L1_V, dv1), (L1_G, dgate1), (L1_F, df)]
    dx1, d_pre1 = _in_proj_bwd(pieces1, w1p, x1, g["l1_pre_g"], dx2, "l1_in_bwd")
    dw1p = jnp.concatenate(_matmul_t_many(h1t, [dq1, dk1], "l1_dw_in_a")
                           + _matmul_t_many(h1t, [dv1, dgate1, df], "l1_dw_in_b"), axis=1)

    dy0, do0, dgate0, d_post0 = _out_proj_bwd(dx1, y0, g["l0_post_g"], wo0, proj0, (L0_SBG, L0_MLG), o_sb, o_ml, 0, 0,
                                              "l0_out_bwd")
    dwo0 = jnp.concatenate([_matmul_t(ogt_sb, dy0, "l0_dw_out_sb"), _matmul_t(ogt_ml, dy0, "l0_dw_out_mla")], axis=0)
    dsq, dsk, dsv = _sb_bwd_t(proj0, do0, tot_sb, S, 4, "sb_bwd")
    dqm, dkm, dvm = _softmax_bwd_t("mla", qm, km, vm, do0, 4, o_ml, lse_ml, None, S, 4, "mla_bwd")
    dprep, dqb, dkvb, d_qag, d_kvag = _mla_prep_bwd(dqm, dkm, dvm, proj0, g["l0_q_a_g"], g["l0_kv_a_g"], wqp, wkvp,
                                                    cosT, s1T, s2T, "mla_prep_bwd")
    dwqp = _matmul_t(qnt, dqb, "l0_dw_qb")
    dwkvp = _matmul_t(cnt, dkvb, "l0_dw_kvb")
    pieces0 = [(L0_PREP, dprep), (L0_SBG, dgate0), (L0_SBQ, dsq), (L0_SBK, dsk), (L0_SBV, dsv)]
    dx0, d_pre0 = _in_proj_bwd(pieces0, w0p, x, g["l0_pre_g"], dx1, "l0_in_bwd")
    dw0p = jnp.concatenate(_matmul_t_many(h0t, [dprep, dgate0], "l0_dw_in_a")
                           + _matmul_t_many(h0t, [dsq, dsk, dsv], "l0_dw_in_b"), axis=1)

    grads = {
        "l0_pre_g": d_pre0, "l0_post_g": d_post0, "l0_w_in": dw0p, "l0_q_a_g": d_qag, "l0_w_q_b": dwqp,
        "l0_kv_a_g": d_kvag, "l0_w_kv_b": dwkvp, "l0_w_out": dwo0, "l1_pre_g": d_pre1, "l1_post_g": d_post1,
        "l1_w_in": dw1p, "l1_b_f": d_bf[:, :FOX_HEADS], "l1_w_out": dwo1,
    }
    return lsum, dx0, grads


_ANY = pl.BlockSpec(memory_space=pl.ANY)


def _place():
    return lax.axis_index("x"), lax.axis_index("y"), lax.axis_index("c")


def _other_chips(x, y):
    return [(1 - x, y), (x, 1 - y), (1 - x, 1 - y)]


def _half(rows, c):
    return pl.ds(c * (rows // 2), rows // 2)


def _gather_phases(p_refs, out_refs, send_sems, recv_sems):
    n = len(p_refs)
    x, y, c = _place()
    sibling = (x, y, 1 - c)
    chips = _other_chips(x, y)

    def blk(k, chip, cc):
        return out_refs[k].at[2 * chip[0] + chip[1], _half(p_refs[k].shape[0], cc)]

    def copy(s, src, dst, to):
        return pltpu.make_async_remote_copy(src_ref=src, dst_ref=dst, send_sem=send_sems.at[s],
                                            recv_sem=recv_sems.at[s], device_id=to, device_id_type=MESH)

    def first():
        return [copy(6 * k + j, p_refs[k].at[_half(p_refs[k].shape[0], c)], blk(k, (x, y), c), (*chip, c))
                for j, chip in enumerate(chips) for k in range(n)]

    def begin():
        for cp in first():
            cp.start()

    def finish():
        passed = []
        for j, chip in enumerate(chips):
            for k in range(n):
                copy(6 * k + j, blk(k, chip, c), blk(k, chip, c), (x, y, c)).wait_recv()
                passed.append(copy(6 * k + 3 + j, blk(k, chip, c), blk(k, chip, c), sibling))
                passed[-1].start()
        for j, chip in enumerate(chips):
            for k in range(n):
                copy(6 * k + 3 + j, blk(k, chip, 1 - c), blk(k, chip, 1 - c), (x, y, c)).wait_recv()
        for cp in first() + passed:
            cp.wait_send()

    return begin, finish


def _weight_gather(parts):
    n = len(parts)

    def body(*refs):
        begin, finish = _gather_phases(refs[:n], refs[n:2 * n], refs[2 * n], refs[2 * n + 1])
        begin()
        finish()

    return pl.pallas_call(
        body, name="weight_gather", in_specs=[_ANY] * n, out_specs=[_ANY] * n,
        out_shape=[jax.ShapeDtypeStruct((4,) + a.shape, a.dtype) for a in parts],
        scratch_shapes=[pltpu.SemaphoreType.DMA((6 * n,)), pltpu.SemaphoreType.DMA((6 * n,))],
    )(*parts)


def _grad_core_exchange(ps):
    n = len(ps)

    def body(*refs):
        p_refs, recv_refs, send_sems, recv_sems = refs[:n], refs[n:2 * n], refs[2 * n], refs[2 * n + 1]
        x, y, c = _place()
        give = [pltpu.make_async_remote_copy(src_ref=p_refs[k].at[j, _half(p_refs[k].shape[1], 1 - c)],
                                             dst_ref=recv_refs[k].at[j], send_sem=send_sems.at[4 * k + j],
                                             recv_sem=recv_sems.at[4 * k + j], device_id=(x, y, 1 - c),
                                             device_id_type=MESH) for k in range(n) for j in range(4)]
        for cp in give:
            cp.start()
        for cp in give:
            cp.wait()

    return pl.pallas_call(
        body, name="grad_core_exchange", in_specs=[_ANY] * n, out_specs=[_ANY] * n,
        out_shape=[jax.ShapeDtypeStruct((4, p.shape[1] // 2, p.shape[2]), p.dtype) for p in ps],
        scratch_shapes=[pltpu.SemaphoreType.DMA((4 * n,)), pltpu.SemaphoreType.DMA((4 * n,))],
    )(*ps)


def _grad_rows(rows):
    return _pick(rows, (1296, 512, rows))


def _grad_add_cores(p, theirs, c1, name):
    _, rh, cols = theirs.shape
    tr = _grad_rows(rh)

    def body(c_ref, a_ref, b_ref, o_ref):
        o_ref[...] = (a_ref[...] + b_ref[...]).astype(BF16)

    spec = pl.BlockSpec((None, tr, cols), lambda j, r, c: (j, r, 0))
    grid_spec = pltpu.PrefetchScalarGridSpec(
        num_scalar_prefetch=1, grid=(4, rh // tr),
        in_specs=[pl.BlockSpec((None, None, tr, cols), lambda j, r, c: (j, c[0], r, 0)), spec], out_specs=spec)
    return pl.pallas_call(
        body, name=name, grid_spec=grid_spec, out_shape=jax.ShapeDtypeStruct(theirs.shape, BF16),
        compiler_params=_cparams(("parallel", "parallel")),
    )(c1, p.reshape(4, 2, rh, cols), theirs)


def _grad_chip_exchange(qs):
    n = len(qs)

    def body(*refs):
        q_refs, out_refs, send_sems, recv_sems = refs[:n], refs[n:2 * n], refs[2 * n], refs[2 * n + 1]
        x, y, c = _place()
        me = 2 * x + y
        chips = _other_chips(x, y)
        sends = [pltpu.make_async_remote_copy(src_ref=q_refs[k].at[2 * chip[0] + chip[1]], dst_ref=out_refs[k].at[me],
                                              send_sem=send_sems.at[3 * k + j], recv_sem=recv_sems.at[3 * k + j],
                                              device_id=(*chip, c), device_id_type=MESH)
                 for j, chip in enumerate(chips) for k in range(n)]
        for cp in sends:
            cp.start()
        for j, chip in enumerate(chips):
            for k in range(n):
                slot = out_refs[k].at[2 * chip[0] + chip[1]]
                pltpu.make_async_remote_copy(src_ref=slot, dst_ref=slot, send_sem=send_sems.at[3 * k + j],
                                             recv_sem=recv_sems.at[3 * k + j], device_id=(x, y, c),
                                             device_id_type=MESH).wait_recv()
        for cp in sends:
            cp.wait_send()

    return pl.pallas_call(
        body, name="grad_chip_exchange", in_specs=[_ANY] * n, out_specs=[_ANY] * n,
        out_shape=[jax.ShapeDtypeStruct(q.shape, q.dtype) for q in qs],
        scratch_shapes=[pltpu.SemaphoreType.DMA((3 * n,)), pltpu.SemaphoreType.DMA((3 * n,))],
    )(*qs)


def _grad_add_chips(q, slots, me1, name):
    _, rh, cols = q.shape
    tr = _grad_rows(rh)

    def body(me_ref, own_ref, s0, s1, s2, s3, o_ref):
        me = me_ref[0]
        t = [jnp.where(me == j, own_ref[...], s[...]).astype(F32) for j, s in enumerate((s0, s1, s2, s3))]
        o_ref[...] = ((t[0] + t[1]) + t[2]) + t[3]

    def slot_spec(j):
        return pl.BlockSpec((None, tr, cols), lambda r, me: (jnp.where(me[0] == j, (j + 1) % 4, j), r, 0))

    grid_spec = pltpu.PrefetchScalarGridSpec(
        num_scalar_prefetch=1, grid=(rh // tr,),
        in_specs=[pl.BlockSpec((None, tr, cols), lambda r, me: (me[0], r, 0))] + [slot_spec(j) for j in range(4)],
        out_specs=pl.BlockSpec((tr, cols), lambda r, me: (r, 0)))
    return pl.pallas_call(
        body, name=name, grid_spec=grid_spec, out_shape=jax.ShapeDtypeStruct(q.shape[1:], F32),
        compiler_params=_cparams(("parallel",)),
    )(me1, q, slots, slots, slots, slots)


def _grad_core_gather(ts):
    n = len(ts)

    def body(*refs):
        t_refs, out_refs, send_sems, recv_sems = refs[:n], refs[n:2 * n], refs[2 * n], refs[2 * n + 1]
        x, y, c = _place()
        give = [pltpu.make_async_remote_copy(src_ref=t_refs[k], dst_ref=out_refs[k], send_sem=send_sems.at[k],
                                             recv_sem=recv_sems.at[k], device_id=(x, y, 1 - c), device_id_type=MESH)
                for k in range(n)]
        for cp in give:
            cp.start()
        for cp in give:
            cp.wait()

    return pl.pallas_call(
        body, name="grad_core_gather", in_specs=[_ANY] * n, out_specs=[_ANY] * n,
        out_shape=[jax.ShapeDtypeStruct(t.shape, t.dtype) for t in ts],
        scratch_shapes=[pltpu.SemaphoreType.DMA((n,)), pltpu.SemaphoreType.DMA((n,))],
    )(*ts)


def _small_allreduce(sp):
    def body(sp_ref, out_ref, gath_ref, send_sems, recv_sems):
        x, y, c = _place()
        me = 4 * x + 2 * y + c
        gath_ref[me] = sp_ref[...]
        peers = []
        for k in range(1, 8):
            px = 1 - x if k & 4 else x
            py = 1 - y if k & 2 else y
            pc = 1 - c if k & 1 else c
            peers.append((px, py, pc))
        sends = [pltpu.make_async_remote_copy(src_ref=sp_ref, dst_ref=gath_ref.at[me], send_sem=send_sems.at[k],
                                              recv_sem=recv_sems.at[k], device_id=peer, device_id_type=MESH)
                 for k, peer in enumerate(peers)]
        for cp in sends:
            cp.start()
        for k, (px, py, pc) in enumerate(peers):
            slot = gath_ref.at[4 * px + 2 * py + pc]
            pltpu.make_async_remote_copy(src_ref=slot, dst_ref=slot, send_sem=send_sems.at[k], recv_sem=recv_sems.at[k],
                                         device_id=(x, y, c), device_id_type=MESH).wait_recv()
        for cp in sends:
            cp.wait_send()
        tot = gath_ref[0]
        for d in range(1, 8):
            tot = tot + gath_ref[d]
        out_ref[...] = tot

    vm = pl.BlockSpec(memory_space=pltpu.VMEM)
    return pl.pallas_call(
        body, name="small_allreduce", in_specs=[vm], out_specs=vm, out_shape=jax.ShapeDtypeStruct(sp.shape, sp.dtype),
        scratch_shapes=[pltpu.VMEM((8,) + sp.shape, sp.dtype), pltpu.SemaphoreType.DMA((7,)), pltpu.SemaphoreType.DMA((7,))],
    )(sp)


def _adamw_update(w, gv, m, v):
    mn = ADAM_B1 * m + (1.0 - ADAM_B1) * gv
    vn = ADAM_B2 * v + (1.0 - ADAM_B2) * (gv * gv)
    m_hat = mn / (1.0 - ADAM_B1 ** ADAM_STEP)
    v_hat = vn / (1.0 - ADAM_B2 ** ADAM_STEP)
    return -ADAM_LR * (m_hat / (jnp.sqrt(v_hat) + ADAM_EPS) + ADAM_WD * w), mn, vn


def _adamw(w, g, m, v, name):
    rows, cols = w.shape

    def body(w_ref, g_ref, m_ref, v_ref, d_ref, mo_ref, vo_ref):
        d_ref[...], mo_ref[...], vo_ref[...] = _adamw_update(w_ref[...], g_ref[...], m_ref[...], v_ref[...])

    if rows % 256 == 0 or cols % 256 != 0:
        tr = _pick(rows, (256, rows))
        grid, spec = (rows // tr,), pl.BlockSpec((tr, cols), lambda r: (r, 0))
    else:
        grid, spec = (cols // 256,), pl.BlockSpec((rows, 256), lambda r: (0, r))
    shp = jax.ShapeDtypeStruct(w.shape, F32)
    return pl.pallas_call(
        body, name=name, grid=grid, in_specs=[spec] * 4, out_specs=[spec] * 3, out_shape=[shp] * 3,
        compiler_params=_cparams(("parallel",)),
    )(w, g, m, v)


MAT_NAMES = ("l0_w_in", "l0_w_q_b", "l0_w_kv_b", "l0_w_out", "l1_w_in", "l1_w_out")
VEC_NAMES = ("l0_pre_g", "l0_post_g", "l0_q_a_g", "l0_kv_a_g", "l1_pre_g", "l1_post_g", "l1_b_f")
WEIGHT_NAMES = ("l0_pre_g", "l0_post_g", "l0_w_in", "l0_q_a_g", "l0_w_q_b", "l0_kv_a_g", "l0_w_kv_b", "l0_w_out",
                "l1_pre_g", "l1_post_g", "l1_w_in", "l1_b_f", "l1_w_out")
MAT_SHARD = {"l0_w_in": (1024, 808), "l0_w_q_b": (384, 192), "l0_w_kv_b": (256, 256), "l0_w_out": (256, 1024),
             "l1_w_in": (1024, 1028), "l1_w_out": (256, 1024)}
ROW_SHARDED = ("l0_w_out", "l1_w_out")
WHOLE_MATS = ("l0_w_in", "l1_w_in")
PACKED_MATS = ("l0_w_q_b", "l0_w_kv_b", "l0_w_out", "l1_w_out")
VEC_LEN = {"l0_pre_g": 1024, "l0_post_g": 1024, "l0_q_a_g": 384, "l0_kv_a_g": 256, "l1_pre_g": 1024,
           "l1_post_g": 1024, "l1_b_f": 16}


def _mat_rows(n):
    r, c = MAT_SHARD[n]
    return r * c // LANES


def _pack_shards(shards):
    return jnp.concatenate([shards[n].reshape(shards[n].shape[:-2] + (_mat_rows(n), LANES)) for n in PACKED_MATS],
                           axis=-2)


def _unpack_shards(pack):
    out, at = {}, 0
    for n in PACKED_MATS:
        out[n] = pack[..., at:at + _mat_rows(n), :].reshape(pack.shape[:-2] + MAT_SHARD[n])
        at += _mat_rows(n)
    return out


def _join_shards(n, s):
    if n in ROW_SHARDED:
        return s.reshape(4 * s.shape[1], s.shape[2])
    return s.transpose(1, 0, 2).reshape(s.shape[1], 4 * s.shape[2])


def _cut_shards(n, w):
    r, c = MAT_SHARD[n]
    if n in ROW_SHARDED:
        return w.reshape(4, r, c)
    return w.reshape(r, 4, c).transpose(1, 0, 2)


def _pack_vecs(vecs):
    parts = []
    for n in VEC_NAMES:
        v = vecs[n].reshape(-1)
        parts.append(jnp.pad(v, (0, VEC_ROWS * LANES - v.shape[0])).reshape(VEC_ROWS, LANES))
    return jnp.concatenate(parts, axis=0)


def _unpack_vecs(pack):
    return {n: pack[k * VEC_ROWS:(k + 1) * VEC_ROWS].reshape(-1)[:VEC_LEN[n]] for k, n in enumerate(VEC_NAMES)}


def kernel(x, positions, l0_pre_g, l0_post_g, l0_w_in, l0_q_a_g, l0_w_q_b, l0_kv_a_g, l0_w_kv_b, l0_w_out, l1_pre_g, l1_post_g, l1_w_in, l1_b_f, l1_w_out, loss_target, m_l0_pre_g, m_l0_post_g, m_l0_w_in, m_l0_q_a_g, m_l0_w_q_b, m_l0_kv_a_g, m_l0_w_kv_b, m_l0_w_out, m_l1_pre_g, m_l1_post_g, m_l1_w_in, m_l1_b_f, m_l1_w_out, v_l0_pre_g, v_l0_post_g, v_l0_w_in, v_l0_q_a_g, v_l0_w_q_b, v_l0_kv_a_g, v_l0_w_kv_b, v_l0_w_out, v_l1_pre_g, v_l1_post_g, v_l1_w_in, v_l1_b_f, v_l1_w_out):
    w = dict(l0_pre_g=l0_pre_g, l0_post_g=l0_post_g, l0_w_in=l0_w_in, l0_q_a_g=l0_q_a_g, l0_w_q_b=l0_w_q_b,
             l0_kv_a_g=l0_kv_a_g, l0_w_kv_b=l0_w_kv_b, l0_w_out=l0_w_out, l1_pre_g=l1_pre_g, l1_post_g=l1_post_g,
             l1_w_in=l1_w_in, l1_b_f=l1_b_f, l1_w_out=l1_w_out)
    m = dict(l0_pre_g=m_l0_pre_g, l0_post_g=m_l0_post_g, l0_w_in=m_l0_w_in, l0_q_a_g=m_l0_q_a_g, l0_w_q_b=m_l0_w_q_b,
             l0_kv_a_g=m_l0_kv_a_g, l0_w_kv_b=m_l0_w_kv_b, l0_w_out=m_l0_w_out, l1_pre_g=m_l1_pre_g,
             l1_post_g=m_l1_post_g, l1_w_in=m_l1_w_in, l1_b_f=m_l1_b_f, l1_w_out=m_l1_w_out)
    v = dict(l0_pre_g=v_l0_pre_g, l0_post_g=v_l0_post_g, l0_w_in=v_l0_w_in, l0_q_a_g=v_l0_q_a_g, l0_w_q_b=v_l0_w_q_b,
             l0_kv_a_g=v_l0_kv_a_g, l0_w_kv_b=v_l0_w_kv_b, l0_w_out=v_l0_w_out, l1_pre_g=v_l1_pre_g,
             l1_post_g=v_l1_post_g, l1_w_in=v_l1_w_in, l1_b_f=v_l1_b_f, l1_w_out=v_l1_w_out)

    cx, cy, cc = _place()
    me1 = jnp.reshape(2 * cx + cy, (1,)).astype(jnp.int32)
    c1 = jnp.reshape(cc, (1,)).astype(jnp.int32)
    w_bf = {n: w[n].astype(BF16) for n in MAT_NAMES}
    def with_mine(got, own):
        return lax.dynamic_update_slice(got, own[None], (2 * cx + cy, 0, 0))

    mine = [_pack_shards(w_bf), w_bf["l0_w_in"]]
    got = [with_mine(g, a) for g, a in zip(_weight_gather(mine), mine)]
    gathered = dict(_unpack_shards(got[0]), l0_w_in=got[1])
    full = {n: _join_shards(n, gathered[n]) for n in MAT_NAMES if n != "l1_w_in"}
    gains = {n: w[n].reshape(1, -1) for n in VEC_NAMES}

    def finish_w1(w1_all):
        return _pad_w1(_join_shards("l1_w_in", with_mine(w1_all, w_bf["l1_w_in"])))

    lsum, dx0, grads = _local_step(
        x[0], positions[0], loss_target[0], gains, _pad_w0(full["l0_w_in"]), _pad_wq(full["l0_w_q_b"]),
        _pad_wkv(full["l0_w_kv_b"]), full["l0_w_out"], (w_bf["l1_w_in"], finish_w1), full["l1_w_out"])

    gfull = {"l0_w_in": _unpad_w0(grads["l0_w_in"]), "l0_w_q_b": _unpad_wq(grads["l0_w_q_b"]),
             "l0_w_kv_b": _unpad_wkv(grads["l0_w_kv_b"]), "l0_w_out": grads["l0_w_out"],
             "l1_w_in": grads["l1_w_in"][:, :ODD_IN_WIDTH], "l1_w_out": grads["l1_w_out"]}
    cut = {n: _cut_shards(n, gfull[n]) for n in MAT_NAMES}
    tags = ("packed",) + WHOLE_MATS
    g_parts = [_pack_shards(cut)] + [cut[n] for n in WHOLE_MATS]
    q_cores = [_grad_add_cores(p, t, c1, "grad_add_cores_" + tag)
               for p, t, tag in zip(g_parts, _grad_core_exchange(g_parts), tags)]
    g_mine = [_grad_add_chips(q, s, me1, "grad_add_chips_" + tag)
              for q, s, tag in zip(q_cores, _grad_chip_exchange(q_cores), tags)]
    g_theirs = _grad_core_gather(g_mine)

    small = _small_allreduce(jnp.concatenate([_pack_vecs({n: grads[n] for n in VEC_NAMES}),
                                              lsum.reshape(D_MODEL // LANES, LANES)], axis=0))
    g_small = small[:SMALL_ROWS]
    loss = 0.5 * jnp.sum(small[SMALL_ROWS:]) / float(D_MODEL)

    whole = [jnp.concatenate([lax.select(cc == 0, a, b), lax.select(cc == 0, b, a)], axis=0)
             for a, b in zip(g_mine, g_theirs)]
    g_mats = dict(_unpack_shards(whole[0]), **dict(zip(WHOLE_MATS, whole[1:])))
    d_mats, m_mats, v_mats = {}, {}, {}
    for n in PACKED_MATS:
        d_mats[n], m_mats[n], v_mats[n] = _adamw(w[n], g_mats[n], m[n], v[n], "adamw_" + n)
    for n in WHOLE_MATS:
        gt = g_mats[n].T
        outs = _adamw(w[n].T, gt, m[n].T, v[n].T, "adamw_" + n)
        g_mats[n], d_mats[n], m_mats[n], v_mats[n] = gt.T, outs[0].T, outs[1].T, outs[2].T
    d_small, m_small, v_small = _adamw(_pack_vecs(w), g_small, _pack_vecs(m), _pack_vecs(v), "adamw_vecs")

    def leaves(mats, vec_pack):
        out = dict(mats)
        out.update(_unpack_vecs(vec_pack))
        return [out[n] for n in WEIGHT_NAMES]

    return (loss, dx0[None], *leaves(g_mats, g_small), *leaves(d_mats, d_small), *leaves(m_mats, m_small),
            *leaves(v_mats, v_small))
```

```python
import jax
import jax.numpy as jnp
from jax import lax
from jax.experimental import pallas as pl
from jax.experimental.pallas import tpu as pltpu

F32 = jnp.float32
BF16 = jnp.bfloat16
MESH = pl.DeviceIdType.MESH

D_MODEL = 1024
RMS_EPS = 1e-6
ROPE_THETA = 10000.0
SB_WIDTH = 512
MLA_Q_LORA = 384
MLA_KV_LORA = 256
MLA_ROPE_DIM = 32
MLA_WIDTH = 512
FOX_WIDTH = 1024
FOX_HEADS = 16
EVEN_IN_WIDTH = 3232
ODD_IN_WIDTH = 4112

ADAM_LR = 0.001
ADAM_B1 = 0.9
ADAM_B2 = 0.999
ADAM_EPS = 1e-08
ADAM_WD = 0.01
ADAM_STEP = 10

LANES = 128
VMEM_LIMIT = 56 * 1024 * 1024

L0_PREP = 0
L0_PREP_W = 768
L0_SBG = 768
L0_MLG = 1280
L0_SBQ = 1792
L0_SBK = 2304
L0_SBV = 2816
L0_WIDTH = 3328
L1_Q = 0
L1_K = 1024
L1_V = 2048
L1_G = 3072
L1_F = 4096
L1_WIDTH = 4224

ATT_T = 256
ATT_GROUP = 4
ATT_QSUB = 2
NEG = -1e30

VEC_ROWS = 8
SMALL_ROWS = 7 * VEC_ROWS


def _cparams(sem, **kw):
    return pltpu.CompilerParams(dimension_semantics=sem, vmem_limit_bytes=VMEM_LIMIT, **kw)


def _dot(a, b):
    return lax.dot_general(a, b, (((1,), (0,)), ((), ())), preferred_element_type=F32)


def _dot_nt(a, b):
    return lax.dot_general(a, b, (((1,), (1,)), ((), ())), preferred_element_type=F32)


def _sigmoid(x):
    return 1.0 / (1.0 + jnp.exp(-x))


def _rstd(x):
    return lax.rsqrt(jnp.mean(x * x, axis=-1, keepdims=True) + RMS_EPS)


def _norm_bwd(x, g, dy):
    r = _rstd(x)
    xn = x * r
    dxn = dy * g
    dx = r * (dxn - xn * jnp.mean(dxn * xn, axis=-1, keepdims=True))
    return dx, dy * xn


def _split3(x):
    hi = x.astype(BF16)
    r1 = x - hi.astype(F32)
    mid = r1.astype(BF16)
    lo = (r1 - mid.astype(F32)).astype(BF16)
    return hi, mid, lo


def _wide_tile(n, cap=1792):
    return max(t for t in range(LANES, min(n, cap) + 1, LANES) if n % t == 0)


def _pick(n, cands):
    for c in cands:
        if n % c == 0:
            return c
    raise ValueError(n)


def _norm_matmul(x, g, w, name, ride=None):
    S, K = x.shape
    N = w.shape[1]
    tm = _pick(S, (1024, 512, 256))
    tn = _wide_tile(N)
    ni, nj = S // tm, N // tn

    def body(x_ref, g_ref, w_ref, *rest):
        if ride is None:
            o_ref, ht_ref, h_ref = rest
        else:
            a_ref, o_ref, ht_ref, land_ref, h_ref, send_sems, recv_sems = rest
            begin, finish = _gather_phases([a_ref], [land_ref], send_sems, recv_sems)
            pl.when((pl.program_id(0) == 0) & (pl.program_id(1) == 0))(begin)

        @pl.when(pl.program_id(1) == 0)
        def _():
            xv = x_ref[...]
            h = (xv * _rstd(xv)) * g_ref[...]
            h_ref[...] = h.astype(BF16)
            ht_ref[...] = h.T.astype(BF16)
        o_ref[...] = _dot(h_ref[...], w_ref[...])
        if ride is not None:
            pl.when((pl.program_id(0) == ni - 1) & (pl.program_id(1) == nj - 1))(finish)

    in_specs = [pl.BlockSpec((tm, K), lambda i, j: (i, 0)),
                pl.BlockSpec((1, K), lambda i, j: (0, 0)),
                pl.BlockSpec((K, tn), lambda i, j: (0, j))]
    out_specs = [pl.BlockSpec((tm, tn), lambda i, j: (i, j)), pl.BlockSpec((K, tm), lambda i, j: (0, i))]
    out_shape = [jax.ShapeDtypeStruct((S, N), F32), jax.ShapeDtypeStruct((K, S), BF16)]
    scratch = [pltpu.VMEM((tm, K), BF16)]
    args = [x, g, w]
    if ride is not None:
        in_specs.append(_ANY)
        out_specs.append(_ANY)
        out_shape.append(jax.ShapeDtypeStruct((4,) + ride.shape, ride.dtype))
        scratch += [pltpu.SemaphoreType.DMA((6,)), pltpu.SemaphoreType.DMA((6,))]
        args.append(ride)
    return pl.pallas_call(
        body, name=name, grid=(ni, nj), in_specs=in_specs, out_specs=out_specs, out_shape=out_shape,
        scratch_shapes=scratch,
        compiler_params=_cparams(("parallel", "arbitrary") if ride is None else ("arbitrary", "arbitrary")),
    )(*args)


def _matmul_t(at, b, name):
    M, S = at.shape
    N = b.shape[1]
    tn = _wide_tile(N)
    ts = _pick(S, (512, 256))

    def body(a_ref, b_ref, o_ref):
        @pl.when(pl.program_id(1) == 0)
        def _():
            o_ref[...] = jnp.zeros_like(o_ref)
        o_ref[...] += _dot(a_ref[...], b_ref[...].astype(BF16))

    return pl.pallas_call(
        body, name=name, grid=(N // tn, S // ts),
        in_specs=[pl.BlockSpec((M, ts), lambda j, k: (0, k)),
                  pl.BlockSpec((ts, tn), lambda j, k: (k, j))],
        out_specs=pl.BlockSpec((M, tn), lambda j, k: (0, j)),
        out_shape=jax.ShapeDtypeStruct((M, N), F32),
        compiler_params=_cparams(("parallel", "arbitrary")),
    )(at, b)


def _matmul_t_many(at, bs, name):
    M, S = at.shape
    ts = _pick(S, (512, 256))
    n = len(bs)

    def body(*refs):
        a_ref, b_refs, o_refs = refs[0], refs[1:1 + n], refs[1 + n:]

        @pl.when(pl.program_id(0) == 0)
        def _():
            for o_ref in o_refs:
                o_ref[...] = jnp.zeros_like(o_ref)

        a = a_ref[...]
        for b_ref, o_ref in zip(b_refs, o_refs):
            o_ref[...] += _dot(a, b_ref[...].astype(BF16))

    return pl.pallas_call(
        body, name=name, grid=(S // ts,),
        in_specs=[pl.BlockSpec((M, ts), lambda k: (0, k))] + [pl.BlockSpec((ts, b.shape[1]), lambda k: (k, 0)) for b in bs],
        out_specs=[pl.BlockSpec((M, b.shape[1]), lambda k: (0, 0)) for b in bs],
        out_shape=[jax.ShapeDtypeStruct((M, b.shape[1]), F32) for b in bs],
        compiler_params=_cparams(("arbitrary",)),
    )(at, *bs)


def _in_proj_bwd(pieces, w, x, g, dx_up, name, ride=None):
    S, K = x.shape
    N = w.shape[1]
    tm = _pick(S, (256,))
    nsteps = S // tm
    offs = [off for off, _ in pieces]
    arrs = [a for _, a in pieces]

    def body(*refs):
        d_refs = refs[:len(arrs)]
        if ride is None:
            w_ref, x_ref, g_ref, u_ref, dx_ref, dg_ref = refs[len(arrs):]
        else:
            w_ref, x_ref, g_ref, u_ref, q_ref, dx_ref, dg_ref, slots_ref, send_sems, recv_sems = refs[len(arrs):]
            begin, finish = _exchange_phases([q_ref], [slots_ref], send_sems, recv_sems)
            pl.when(pl.program_id(0) == 0)(begin)

        @pl.when(pl.program_id(0) == 0)
        def _():
            dg_ref[...] = jnp.zeros_like(dg_ref)

        acc = None
        for off, d_ref in zip(offs, d_refs):
            part = _dot_nt(d_ref[...].astype(BF16), w_ref[:, off:off + d_ref.shape[1]])
            acc = part if acc is None else acc + part
        dx, dgrow = _norm_bwd(x_ref[...], g_ref[...], acc)
        dx_ref[...] = u_ref[...] + dx
        dg_ref[...] += jnp.sum(dgrow, axis=0, keepdims=True)
        if ride is not None:
            pl.when(pl.program_id(0) == nsteps - 1)(finish)

    row = lambda i: (i, 0)
    fixed = lambda i: (0, 0)
    in_specs = [pl.BlockSpec((tm, a.shape[1]), row) for a in arrs] + [
        pl.BlockSpec((K, N), fixed), pl.BlockSpec((tm, K), row), pl.BlockSpec((1, K), fixed), pl.BlockSpec((tm, K), row)]
    out_specs = [pl.BlockSpec((tm, K), row), pl.BlockSpec((1, K), fixed)]
    out_shape = [jax.ShapeDtypeStruct((S, K), F32), jax.ShapeDtypeStruct((1, K), F32)]
    args = [*arrs, w, x, g, dx_up]
    scratch = []
    if ride is not None:
        in_specs.append(_ANY)
        out_specs.append(_ANY)
        out_shape.append(jax.ShapeDtypeStruct(ride.shape, ride.dtype))
        scratch = [pltpu.SemaphoreType.DMA((3,)), pltpu.SemaphoreType.DMA((3,))]
        args.append(ride)
    return pl.pallas_call(
        body, name=name, grid=(nsteps,), in_specs=in_specs, out_specs=out_specs, out_shape=out_shape,
        scratch_shapes=scratch, compiler_params=_cparams(("arbitrary",)),
    )(*args)


def _out_proj(og_a, og_b, blk_a, blk_b, w, x, g, target, name):
    S = x.shape[0]
    D = x.shape[1]
    tm = _pick(S, (512, 256))
    with_loss = target is not None

    def body(*refs):
        if with_loss:
            a_ref, b_ref, wa_ref, wb_ref, x_ref, g_ref, t_ref, y_ref, o_ref, l_ref = refs
        else:
            a_ref, b_ref, wa_ref, wb_ref, x_ref, g_ref, y_ref, o_ref = refs
        y = _dot(a_ref[...], wa_ref[...]) + _dot(b_ref[...], wb_ref[...])
        y_ref[...] = y
        xn = x_ref[...] + (y * _rstd(y)) * g_ref[...]
        if with_loss:
            @pl.when(pl.program_id(0) == 0)
            def _():
                l_ref[...] = jnp.zeros_like(l_ref)
            d = xn - t_ref[...]
            o_ref[...] = d / float(D)
            l_ref[...] += jnp.sum(d * d, axis=0, keepdims=True)
        else:
            o_ref[...] = xn

    row = lambda i: (i, 0)
    in_specs = [pl.BlockSpec((tm, 512), lambda i: (i, blk_a)),
                pl.BlockSpec((tm, 512), lambda i: (i, blk_b)),
                pl.BlockSpec((512, D), lambda i: (0, 0)),
                pl.BlockSpec((512, D), lambda i: (1, 0)),
                pl.BlockSpec((tm, D), row),
                pl.BlockSpec((1, D), lambda i: (0, 0))]
    out_specs = [pl.BlockSpec((tm, D), row), pl.BlockSpec((tm, D), row)]
    out_shape = [jax.ShapeDtypeStruct((S, D), F32), jax.ShapeDtypeStruct((S, D), F32)]
    args = [og_a, og_b, w, w, x, g]
    if with_loss:
        in_specs.append(pl.BlockSpec((tm, D), row))
        out_specs.append(pl.BlockSpec((1, D), lambda i: (0, 0)))
        out_shape.append(jax.ShapeDtypeStruct((1, D), F32))
        args.append(target)
    return pl.pallas_call(
        body, name=name, grid=(S // tm,), in_specs=in_specs, out_specs=out_specs, out_shape=out_shape,
        compiler_params=_cparams(("arbitrary",)),
    )(*args)


def _out_proj_bwd(dx_up, y, g, w, proj, gate_offs, o_a, o_b, oblk_a, oblk_b, name):
    S, D = y.shape
    tm = _pick(S, (256,))
    gblk = [off // 256 + c for off in gate_offs for c in range(2)]

    def body(u_ref, y_ref, g_ref, w_ref, g0, g1, g2, g3, oa_ref, ob_ref, dy_ref, do_ref, dgate_ref, dg_ref):
        @pl.when(pl.program_id(0) == 0)
        def _():
            dg_ref[...] = jnp.zeros_like(dg_ref)
        dy, dgrow = _norm_bwd(y_ref[...], g_ref[...], u_ref[...])
        dg_ref[...] += jnp.sum(dgrow, axis=0, keepdims=True)
        dyb = dy.astype(BF16)
        dy_ref[...] = dyb
        dog = _dot_nt(dyb, w_ref[...])
        gates = (g0, g1, g2, g3)
        for c in range(4):
            gt = gates[c][...]
            sg = _sigmoid(gt)
            o_ref = oa_ref if c < 2 else ob_ref
            ov = o_ref[:, (c % 2) * 256:(c % 2 + 1) * 256]
            dc = dog[:, c * 256:(c + 1) * 256]
            do_ref[:, c * 256:(c + 1) * 256] = dc * (gt * sg)
            dgate_ref[:, c * 256:(c + 1) * 256] = dc * ov * (sg * (1.0 + gt * (1.0 - sg)))

    row = lambda i: (i, 0)
    gspec = lambda c: pl.BlockSpec((tm, 256), lambda i: (i, gblk[c]))
    return pl.pallas_call(
        body, name=name, grid=(S // tm,),
        in_specs=[pl.BlockSpec((tm, D), row), pl.BlockSpec((tm, D), row), pl.BlockSpec((1, D), lambda i: (0, 0)),
                  pl.BlockSpec((D, D), lambda i: (0, 0)),
                  gspec(0), gspec(1), gspec(2), gspec(3),
                  pl.BlockSpec((tm, 512), lambda i: (i, oblk_a)),
                  pl.BlockSpec((tm, 512), lambda i: (i, oblk_b))],
        out_specs=[pl.BlockSpec((tm, D), row), pl.BlockSpec((tm, D), row), pl.BlockSpec((tm, D), row),
                   pl.BlockSpec((1, D), lambda i: (0, 0))],
        out_shape=[jax.ShapeDtypeStruct((S, D), BF16), jax.ShapeDtypeStruct((S, D), F32),
                   jax.ShapeDtypeStruct((S, D), F32), jax.ShapeDtypeStruct((1, D), F32)],
        compiler_params=_cparams(("arbitrary",)),
    )(dx_up, y, g, w, proj, proj, proj, proj, o_a, o_b)


def _rope_tables(pos, invf, name):
    S = pos.shape[0]
    tm = _pick(S, (512, 256))

    def body(p_ref, f_ref, c_ref, s1_ref, s2_ref):
        lane = lax.broadcasted_iota(jnp.int32, (1, LANES), 1)
        ang = p_ref[...].astype(F32) * f_ref[...]
        c, s = jnp.cos(ang), jnp.sin(ang)
        c_ref[...] = jnp.where((lane >= 64) & (lane < 96), c, 1.0)
        s1_ref[...] = jnp.where((lane >= 64) & (lane < 80), -s, 0.0)
        s2_ref[...] = jnp.where((lane >= 80) & (lane < 96), s, 0.0)

    spec = pl.BlockSpec((tm, LANES), lambda i: (i, 0))
    return pl.pallas_call(
        body, name=name, grid=(S // tm,),
        in_specs=[pl.BlockSpec((tm, 1), lambda i: (i, 0)), pl.BlockSpec((1, LANES), lambda i: (0, 0))],
        out_specs=[spec, spec, spec],
        out_shape=[jax.ShapeDtypeStruct((S, LANES), F32)] * 3,
        compiler_params=_cparams(("parallel",)),
    )(pos, invf)


def _rope(x, c, s1, s2):
    return x * c + pltpu.roll(x, LANES - 16, 1) * s1 + pltpu.roll(x, 16, 1) * s2


def _rope_t(d, c, s1, s2):
    return d * c + pltpu.roll(d * s1, 16, 1) + pltpu.roll(d * s2, LANES - 16, 1)


def _mla_prep(proj, gq, gkv, wq, wkv, cosT, s1T, s2T, name):
    S = proj.shape[0]
    tm = _pick(S, (256,))

    def body(p_ref, gq_ref, gkv_ref, wq_ref, wkv_ref, c_ref, s1_ref, s2_ref, q_ref, k_ref, v_ref, qn_ref, cn_ref):
        qa = p_ref[:, 0:384]
        ckv = p_ref[:, 384:640]
        kr = p_ref[:, 640:768]
        qn32 = (qa * _rstd(qa)) * gq_ref[...]
        cn32 = (ckv * _rstd(ckv)) * gkv_ref[...]
        qn = qn32.astype(BF16)
        cn = cn32.astype(BF16)
        qn_ref[...] = qn32.T.astype(BF16)
        cn_ref[...] = cn32.T.astype(BF16)
        qb = _dot(qn, wq_ref[...])
        kvb = _dot(cn, wkv_ref[...])
        c, s1, s2 = c_ref[...], s1_ref[...], s2_ref[...]
        krr = _rope(kr, c, s1, s2)
        for h in range(8):
            sl = slice(h * LANES, (h + 1) * LANES)
            q_ref[:, sl] = _rope(qb[:, sl], c, s1, s2)
            k_ref[:, sl] = kvb[:, sl] + krr
        v_ref[...] = kvb[:, 1024:1536]

    row = lambda i: (i, 0)
    fixed = lambda i: (0, 0)
    tspec = pl.BlockSpec((tm, LANES), row)
    return pl.pallas_call(
        body, name=name, grid=(S // tm,),
        in_specs=[pl.BlockSpec((tm, L0_PREP_W), lambda i: (i, L0_PREP // L0_PREP_W)),
                  pl.BlockSpec((1, 384), fixed), pl.BlockSpec((1, 256), fixed),
                  pl.BlockSpec((384, 1024), fixed), pl.BlockSpec((256, 1536), fixed), tspec, tspec, tspec],
        out_specs=[pl.BlockSpec((tm, 1024), row), pl.BlockSpec((tm, 1024), row), pl.BlockSpec((tm, 512), row),
                   pl.BlockSpec((384, tm), lambda i: (0, i)), pl.BlockSpec((256, tm), lambda i: (0, i))],
        out_shape=[jax.ShapeDtypeStruct((S, 1024), F32), jax.ShapeDtypeStruct((S, 1024), F32),
                   jax.ShapeDtypeStruct((S, 512), F32), jax.ShapeDtypeStruct((384, S), BF16),
                   jax.ShapeDtypeStruct((256, S), BF16)],
        compiler_params=_cparams(("parallel",)),
    )(proj, gq, gkv, wq, wkv, cosT, s1T, s2T)


def _mla_prep_bwd(dq, dk, dv, proj, gq, gkv, wq, wkv, cosT, s1T, s2T, name):
    S = proj.shape[0]
    tm = _pick(S, (256,))

    def body(dq_ref, dk_ref, dv_ref, p_ref, gq_ref, gkv_ref, wq_ref, wkv_ref, c_ref, s1_ref, s2_ref,
             dp_ref, dqb_ref, dkvb_ref, dgq_ref, dgkv_ref):
        @pl.when(pl.program_id(0) == 0)
        def _():
            dgq_ref[...] = jnp.zeros_like(dgq_ref)
            dgkv_ref[...] = jnp.zeros_like(dgkv_ref)
        c, s1, s2 = c_ref[...], s1_ref[...], s2_ref[...]
        lane = lax.broadcasted_iota(jnp.int32, (1, LANES), 1)
        dkr = jnp.zeros((tm, LANES), F32)
        for h in range(8):
            sl = slice(h * LANES, (h + 1) * LANES)
            dqb_ref[:, sl] = _rope_t(dq_ref[:, sl], c, s1, s2).astype(BF16)
            dkh = dk_ref[:, sl]
            dkvb_ref[:, sl] = dkh.astype(BF16)
            dkr = dkr + dkh
        dkvb_ref[:, 1024:1536] = dv_ref[...].astype(BF16)
        dkr = jnp.where((lane >= 64) & (lane < 96), _rope_t(dkr, c, s1, s2), 0.0)
        dqn = _dot_nt(dqb_ref[...], wq_ref[...])
        dcn = _dot_nt(dkvb_ref[...], wkv_ref[...])
        dqa, gq_row = _norm_bwd(p_ref[:, 0:384], gq_ref[...], dqn)
        dckv, gkv_row = _norm_bwd(p_ref[:, 384:640], gkv_ref[...], dcn)
        dp_ref[:, 0:384] = dqa
        dp_ref[:, 384:640] = dckv
        dp_ref[:, 640:768] = dkr
        dgq_ref[...] += jnp.sum(gq_row, axis=0, keepdims=True)
        dgkv_ref[...] += jnp.sum(gkv_row, axis=0, keepdims=True)

    row = lambda i: (i, 0)
    fixed = lambda i: (0, 0)
    tspec = pl.BlockSpec((tm, LANES), row)
    return pl.pallas_call(
        body, name=name, grid=(S // tm,),
        in_specs=[pl.BlockSpec((tm, 1024), row), pl.BlockSpec((tm, 1024), row), pl.BlockSpec((tm, 512), row),
                  pl.BlockSpec((tm, L0_PREP_W), lambda i: (i, L0_PREP // L0_PREP_W)),
                  pl.BlockSpec((1, 384), fixed), pl.BlockSpec((1, 256), fixed),
                  pl.BlockSpec((384, 1024), fixed), pl.BlockSpec((256, 1536), fixed), tspec, tspec, tspec],
        out_specs=[pl.BlockSpec((tm, L0_PREP_W), row), pl.BlockSpec((tm, 1024), row), pl.BlockSpec((tm, 1536), row),
                   pl.BlockSpec((1, 384), fixed), pl.BlockSpec((1, 256), fixed)],
        out_shape=[jax.ShapeDtypeStruct((S, L0_PREP_W), F32), jax.ShapeDtypeStruct((S, 1024), BF16),
                   jax.ShapeDtypeStruct((S, 1536), BF16), jax.ShapeDtypeStruct((1, 384), F32),
                   jax.ShapeDtypeStruct((1, 256), F32)],
        compiler_params=_cparams(("arbitrary",)),
    )(dq, dk, dv, proj, gq, gkv, wq, wkv, cosT, s1T, s2T)


def _fox_prep(proj, bf, name):
    S = proj.shape[0]
    tm = _pick(S, (256,))

    def body(f_ref, b_ref, c_ref, carry_ref):
        @pl.when(pl.program_id(0) == 0)
        def _():
            carry_ref[...] = jnp.zeros_like(carry_ref)
        u = f_ref[...] + b_ref[...]
        lf = jnp.minimum(u, 0.0) - jnp.log(1.0 + jnp.exp(-jnp.abs(u)))
        r = lax.broadcasted_iota(jnp.int32, (tm, tm), 0)
        cidx = lax.broadcasted_iota(jnp.int32, (tm, tm), 1)
        tri = (cidx <= r).astype(BF16)
        hi, mid, lo = _split3(lf)
        c = carry_ref[...] + (_dot(tri, hi) + _dot(tri, mid) + _dot(tri, lo))
        c_ref[...] = c
        carry_ref[...] = c[tm - 1:tm, :]

    return pl.pallas_call(
        body, name=name, grid=(S // tm,),
        in_specs=[pl.BlockSpec((tm, LANES), lambda i: (i, L1_F // LANES)), pl.BlockSpec((1, LANES), lambda i: (0, 0))],
        out_specs=pl.BlockSpec((tm, LANES), lambda i: (i, 0)),
        out_shape=jax.ShapeDtypeStruct((S, LANES), F32),
        scratch_shapes=[pltpu.VMEM((1, LANES), F32)],
        compiler_params=_cparams(("arbitrary",)),
    )(proj, bf)


def _fox_prep_bwd(dc, proj, bf, name):
    S = proj.shape[0]
    tm = _pick(S, (256,))
    nb = S // tm

    def body(dc_ref, f_ref, b_ref, df_ref, db_ref, carry_ref):
        @pl.when(pl.program_id(0) == 0)
        def _():
            carry_ref[...] = jnp.zeros_like(carry_ref)
            db_ref[...] = jnp.zeros_like(db_ref)
        r = lax.broadcasted_iota(jnp.int32, (tm, tm), 0)
        cidx = lax.broadcasted_iota(jnp.int32, (tm, tm), 1)
        tri = (cidx >= r).astype(BF16)
        hi, mid, lo = _split3(dc_ref[...])
        dlf = carry_ref[...] + (_dot(tri, hi) + _dot(tri, mid) + _dot(tri, lo))
        carry_ref[...] = dlf[0:1, :]
        u = f_ref[...] + b_ref[...]
        e = jnp.exp(-jnp.abs(u))
        sneg = jnp.where(u >= 0.0, e, 1.0) / (1.0 + e)
        lane = lax.broadcasted_iota(jnp.int32, (1, LANES), 1)
        df = jnp.where(lane < FOX_HEADS, dlf * sneg, 0.0)
        df_ref[...] = df
        db_ref[...] += jnp.sum(df, axis=0, keepdims=True)

    return pl.pallas_call(
        body, name=name, grid=(nb,),
        in_specs=[pl.BlockSpec((tm, LANES), lambda i: (nb - 1 - i, 0)),
                  pl.BlockSpec((tm, LANES), lambda i: (nb - 1 - i, L1_F // LANES)),
                  pl.BlockSpec((1, LANES), lambda i: (0, 0))],
        out_specs=[pl.BlockSpec((tm, LANES), lambda i: (nb - 1 - i, 0)), pl.BlockSpec((1, LANES), lambda i: (0, 0))],
        out_shape=[jax.ShapeDtypeStruct((S, LANES), F32), jax.ShapeDtypeStruct((1, LANES), F32)],
        scratch_shapes=[pltpu.VMEM((1, LANES), F32)],
        compiler_params=_cparams(("arbitrary",)),
    )(dc, proj, bf)


def _att_specs(kind, S, T):
    if kind == "sb":
        qo, ko, vo, go = L0_SBQ // LANES, L0_SBK // LANES, L0_SBV // LANES, L0_SBG // LANES
    elif kind == "fox":
        qo, ko, vo, go = L1_Q // LANES, L1_K // LANES, L1_V // LANES, L1_G // LANES
    else:
        go = L0_MLG // LANES
        return (pl.BlockSpec((T, 256), lambda p, i: (i, p)), pl.BlockSpec((S, 256), lambda p, i: (0, p)),
                pl.BlockSpec((S, LANES), lambda p, i: (0, p)), pl.BlockSpec((T, LANES), lambda p, i: (i, go + p)))
    return (pl.BlockSpec((T, LANES), lambda p, i: (i, qo + p)), pl.BlockSpec((S, LANES), lambda p, i: (0, ko + p)),
            pl.BlockSpec((S, LANES), lambda p, i: (0, vo + p)), pl.BlockSpec((T, LANES), lambda p, i: (i, go + p)))


def _per_q_tile(tile_body, hows):
    T = ATT_T

    def view(ref, u, how):
        if how == "rows":
            return ref.at[pl.ds(u * T, T)]
        if how == "lanes":
            return ref.at[:, pl.ds(u * T, T)]
        if how == "stat":
            return ref.at[:, u]
        return ref

    def body(*refs):
        for u in range(ATT_QSUB):
            tile_body(pl.program_id(1) * ATT_QSUB + u, *[view(r, u, how) for r, how in zip(refs, hows)])

    return body


def _mask_flags(js, masked_at):
    return [t == masked_at for t in range(len(js))]


def _loop_tiles(i, tiles, right_to_left, G=ATT_GROUP):
    ng = i // G
    rest = i - ng * G

    def leftover():
        for r in range(G):
            @pl.when(rest == r)
            def _():
                if right_to_left:
                    tiles([i - u for u in range(r + 1)], 0)
                else:
                    tiles([ng * G + u for u in range(r + 1)], r)

    def group(g, carry):
        if right_to_left:
            tiles([ng * G - 1 - (g * G + u) for u in range(G)], None)
        else:
            tiles([g * G + u for u in range(G)], None)
        return carry

    if right_to_left:
        leftover()
    lax.fori_loop(0, ng, group, 0)
    if not right_to_left:
        leftover()


def _head_q(kind, q_ref, m0, scale):
    if kind == "mla":
        return [q_ref[:, 0:LANES].astype(BF16), q_ref[:, LANES:2 * LANES].astype(BF16)]
    qv = q_ref[...] * scale
    return [jnp.where(m0, qv, 0.0).astype(BF16), jnp.where(m0, 0.0, qv).astype(BF16)]


def _head_k(kind, k_ref, start, T):
    if kind == "mla":
        return [k_ref[pl.ds(start, T), 0:LANES].astype(BF16), k_ref[pl.ds(start, T), LANES:2 * LANES].astype(BF16)]
    kb = k_ref[pl.ds(start, T), :].astype(BF16)
    return [kb, kb]


def _softmax_fwd(kind, qkvg, c_col, S, npairs, name):
    T = ATT_T
    nq = S // T
    fox = kind == "fox"
    scale = (96 if kind == "mla" else 64) ** -0.5

    def body(i, *refs):
        if fox:
            q_ref, k_ref, v_ref, g_ref, cc_ref, o_ref, og_ref, ogt_ref, st_ref, m_ref, acc_ref = refs
        else:
            q_ref, k_ref, v_ref, g_ref, o_ref, og_ref, ogt_ref, st_ref, m_ref, acc_ref = refs
        m0 = lax.broadcasted_iota(jnp.int32, (1, LANES), 1) < 64
        top = lax.broadcasted_iota(jnp.int32, (LANES, 1), 0) < 64
        key = lax.broadcasted_iota(jnp.int32, (T, LANES), 0)
        qrow = lax.broadcasted_iota(jnp.int32, (T, LANES), 1)
        qh = _head_q(kind, q_ref, m0, scale)
        m_ref[...] = jnp.full(m_ref.shape, NEG, F32)
        acc_ref[...] = jnp.zeros(acc_ref.shape, F32)
        chains = [(h, b) for h in range(2) for b in range(T // LANES)]

        def tiles(js, masked_at):
            starts = [pl.multiple_of(j * T, T) for j in js]
            zss = []
            for start in starts:
                kh = _head_k(kind, k_ref, start, T)
                zss.append(_split_blocks([_dot_nt(kh[h], qh[h]) for h in range(2)]))
            pss, alss = [], []
            for start, zs, masked in zip(starts, zss, _mask_flags(js, masked_at)):
                ps, alphas = [], []
                for (h, b), z in zip(chains, zs):
                    lanes = slice(b * LANES, (b + 1) * LANES)
                    if kind == "mla":
                        z = z * scale
                    if fox:
                        z = z - cc_ref[h, pl.ds(start, T), :]
                    if masked:
                        z = jnp.where(key <= qrow + b * LANES, z, NEG)
                    m_prev = m_ref[h, :, lanes]
                    m_new = jnp.maximum(m_prev, jnp.max(z, axis=0, keepdims=True))
                    alphas.append(jnp.exp(m_prev - m_new))
                    ps.append(jnp.exp(z - m_new).astype(BF16))
                    m_ref[h, :, lanes] = m_new
                pss.append(_join_blocks(ps, T // LANES))
                alss.append(_join_blocks(alphas, T // LANES))
            for start, ps, alphas in zip(starts, pss, alss):
                vt = v_ref[pl.ds(start, T), :].T
                vth = [jnp.where(top, vt, 1.0).astype(BF16), jnp.where(top, 1.0, vt).astype(BF16)]
                for h in range(2):
                    acc_ref[h] = alphas[h] * acc_ref[h] + _dot(vth[h], ps[h])

        _loop_tiles(i, tiles, False, 2 * ATT_GROUP)
        acc = [acc_ref[0], acc_ref[1]]
        ot = jnp.concatenate([acc[0][0:64] / acc[0][64:128], acc[1][64:128] / acc[1][0:64]], axis=0)
        o = ot.T
        o_ref[...] = o
        gt = g_ref[...]
        og = o * (gt * _sigmoid(gt))
        og_ref[...] = og.astype(BF16)
        ogt_ref[...] = og.T.astype(BF16)
        st_ref[0] = m_ref[0] + jnp.log(acc[0][64:65])
        st_ref[1] = m_ref[1] + jnp.log(acc[1][0:1])

    QT = ATT_QSUB * T
    qs, ks, vs, gs = _att_specs(kind, S, QT)
    in_specs = [qs, ks, vs, gs]
    args = list(qkvg)
    hows = ["rows", None, None, "rows"]
    if fox:
        in_specs += [pl.BlockSpec((2, S, LANES), lambda p, i: (p, 0, 0))]
        args += [c_col]
        hows += [None]
    hows += ["rows", "rows", "lanes", "stat", None, None]
    W = npairs * LANES
    return pl.pallas_call(
        _per_q_tile(body, hows), name=name, grid=(npairs, nq // ATT_QSUB), in_specs=in_specs,
        out_specs=[pl.BlockSpec((QT, LANES), lambda p, i: (i, p)), pl.BlockSpec((QT, LANES), lambda p, i: (i, p)),
                   pl.BlockSpec((LANES, QT), lambda p, i: (p, i)),
                   pl.BlockSpec((2, ATT_QSUB, 1, T), lambda p, i: (p, i, 0, 0))],
        out_shape=[jax.ShapeDtypeStruct((S, W), F32), jax.ShapeDtypeStruct((S, W), BF16),
                   jax.ShapeDtypeStruct((W, S), BF16),
                   jax.ShapeDtypeStruct((2 * npairs, nq, 1, T), F32)],
        scratch_shapes=[pltpu.VMEM((2, 1, T), F32), pltpu.VMEM((2, LANES, T), F32)],
        compiler_params=_cparams(("parallel", "parallel")),
    )(*args)


def _softplus_parts(z):
    sp = jnp.maximum(z, 0.0) + jnp.log(1.0 + jnp.exp(-jnp.abs(z)))
    return sp, z - sp


def _cumsum_dot(tri2, his, los):
    return _split_blocks([_dot(tri2, jnp.concatenate([hi, lo], axis=0)) for hi, lo in zip(his, los)])


def _split2(x):
    hi = x.astype(BF16)
    return hi, (x - hi.astype(F32)).astype(BF16)


def _split_blocks(per_head):
    return [x[:, b * LANES:(b + 1) * LANES] for x in per_head for b in range(x.shape[1] // LANES)]


def _join_blocks(per_block, nb):
    return [jnp.concatenate(per_block[h * nb:(h + 1) * nb], axis=1) for h in range(len(per_block) // nb)]


def _row_of(col):
    return jnp.broadcast_to(col, (col.shape[0], LANES)).T[0:1]


def _softmax_bwd_t(kind, q, k, v, do, do_off, o, lse, c_col, S, npairs, name):
    T = ATT_T
    nq = S // T
    nb = T // LANES
    fox = kind == "fox"
    mla = kind == "mla"
    scale = (96 if mla else 64) ** -0.5
    kw = 256 if mla else LANES

    def body(i, *refs):
        if fox:
            (q_ref, k_ref, v_ref, do_ref, o_ref, st_ref, cc_ref,
             dq_ref, dk_ref, dv_ref, dck_ref, dcq_ref, dqt_ref, rs_ref, dkx_ref) = refs
        else:
            q_ref, k_ref, v_ref, do_ref, o_ref, st_ref, dq_ref, dk_ref, dv_ref, dqt_ref = refs

        @pl.when(i == 0)
        def _():
            dv_ref[...] = jnp.zeros_like(dv_ref)
            if fox:
                dkx_ref[...] = jnp.zeros_like(dkx_ref)
            else:
                dk_ref[...] = jnp.zeros_like(dk_ref)

        m0 = lax.broadcasted_iota(jnp.int32, (1, LANES), 1) < 64
        top = lax.broadcasted_iota(jnp.int32, (LANES, 1), 0) < 64
        key = lax.broadcasted_iota(jnp.int32, (T, LANES), 0)
        qrow = lax.broadcasted_iota(jnp.int32, (T, LANES), 1)
        qh = _head_q(kind, q_ref, m0, scale)
        if fox:
            qv = q_ref[...] * scale
            qk = [jnp.where(m0, qv, 1.0).astype(BF16), jnp.where(m0, 1.0, qv).astype(BF16)]
        else:
            qk = qh
        dov = do_ref[...]
        prod = dov * o_ref[...]
        dd = [_row_of(jnp.sum(jnp.where(m0, prod, 0.0), axis=1, keepdims=True)),
              _row_of(jnp.sum(jnp.where(m0, 0.0, prod), axis=1, keepdims=True))]
        doh = [jnp.where(m0, dov, 0.0).astype(BF16), jnp.where(m0, 0.0, dov).astype(BF16)]
        lse = [st_ref[0], st_ref[1]]
        dqt_ref[...] = jnp.zeros_like(dqt_ref)
        if fox:
            rs_ref[...] = jnp.zeros_like(rs_ref)
        chains = [(h, b) for h in range(2) for b in range(nb)]

        def tiles(js, masked_at):
            starts = [pl.multiple_of(j * T, T) for j in js]
            zss, dpss = [], []
            for start in starts:
                vb = v_ref[pl.ds(start, T), :].astype(BF16)
                kh = _head_k(kind, k_ref, start, T)
                zss.append(_split_blocks([_dot_nt(kh[h], qh[h]) for h in range(2)]))
                dpss.append(_split_blocks([_dot_nt(vb, doh[h]) for h in range(2)]))
            pss, dsss = [], []
            for start, zs, dps, masked in zip(starts, zss, dpss, _mask_flags(js, masked_at)):
                ps, dss = [], []
                for (h, b), z, dp in zip(chains, zs, dps):
                    lanes = slice(b * LANES, (b + 1) * LANES)
                    if mla:
                        z = z * scale
                    if fox:
                        z = z - cc_ref[h, pl.ds(start, T), :]
                    if masked:
                        z = jnp.where(key <= qrow + b * LANES, z, NEG)
                    p = jnp.exp(z - lse[h][:, lanes])
                    ds = p * (dp - dd[h][:, lanes])
                    dsb = ds.astype(BF16)
                    if fox:
                        rs_ref[h, :, lanes] += jnp.sum(dsb.astype(F32), axis=0, keepdims=True)
                    ps.append(p.astype(BF16))
                    dss.append(dsb)
                pss.append(_join_blocks(ps, nb))
                dsss.append(_join_blocks(dss, nb))
            for start, ps, dss in zip(starts, pss, dsss):
                kt = k_ref[pl.ds(start, T), :].T.astype(BF16)
                dvc = None
                for h in range(2):
                    dkh = _dot(dss[h], qk[h])
                    dvh = _dot(ps[h], doh[h])
                    dvc = dvh if dvc is None else dvc + dvh
                    kth = kt[h * LANES:(h + 1) * LANES] if mla else kt
                    dqt_ref[h] += _dot(kth, dss[h])
                    if fox:
                        dkx_ref[h, pl.ds(start, T), :] += dkh
                    elif mla:
                        dk_ref[pl.ds(start, T), h * LANES:(h + 1) * LANES] += dkh * scale
                    else:
                        dk_ref[pl.ds(start, T), :] += dkh
                dv_ref[pl.ds(start, T), :] += dvc

        _loop_tiles(i, tiles, False)
        if mla:
            dq_ref[:, 0:LANES] = dqt_ref[0].T * scale
            dq_ref[:, LANES:2 * LANES] = dqt_ref[1].T * scale
        else:
            dq_ref[...] = jnp.where(top, dqt_ref[0], dqt_ref[1]).T * scale
        if fox:
            dcq_ref[0] = rs_ref[0]
            dcq_ref[1] = rs_ref[1]

            @pl.when(i == nq - 1)
            def _():
                dk_ref[...] = jnp.where(m0, dkx_ref[0], dkx_ref[1])
                dck_ref[0] = dkx_ref[0].T[64:65]
                dck_ref[1] = dkx_ref[1].T[0:1]

    QT = ATT_QSUB * T
    qs, ks, vs, _ = _att_specs(kind, S, QT)
    stat = pl.BlockSpec((2, ATT_QSUB, 1, T), lambda p, i: (p, i, 0, 0))
    in_specs = [qs, ks, vs,
                pl.BlockSpec((QT, LANES), lambda p, i: (i, do_off + p)),
                pl.BlockSpec((QT, LANES), lambda p, i: (i, p)), stat]
    args = [q, k, v, do, o, lse]
    hows = ["rows", None, None, "rows", "rows", "stat"]
    W = npairs * LANES
    out_specs = [pl.BlockSpec((QT, kw), lambda p, i: (i, p)), pl.BlockSpec((S, kw), lambda p, i: (0, p)),
                 pl.BlockSpec((S, LANES), lambda p, i: (0, p))]
    out_shape = [jax.ShapeDtypeStruct((S, npairs * kw), F32), jax.ShapeDtypeStruct((S, npairs * kw), F32),
                 jax.ShapeDtypeStruct((S, W), F32)]
    scratch = [pltpu.VMEM((2, LANES, T), F32)]
    if fox:
        in_specs.append(pl.BlockSpec((2, S, LANES), lambda p, i: (p, 0, 0)))
        args.append(c_col)
        out_specs += [pl.BlockSpec((2, 1, S), lambda p, i: (p, 0, 0)), stat]
        out_shape += [jax.ShapeDtypeStruct((2 * npairs, 1, S), F32), jax.ShapeDtypeStruct((2 * npairs, nq, 1, T), F32)]
        scratch += [pltpu.VMEM((2, 1, T), F32), pltpu.VMEM((2, S, LANES), F32)]
        hows += [None, "rows", None, None, None, "stat", None, None, None]
    else:
        hows += ["rows", None, None, None]
    return pl.pallas_call(
        _per_q_tile(body, hows), name=name, grid=(npairs, nq // ATT_QSUB), in_specs=in_specs, out_specs=out_specs,
        out_shape=out_shape, scratch_shapes=scratch, compiler_params=_cparams(("parallel", "arbitrary")),
    )(*args)


def _sb_fwd_t(proj, S, npairs, name):
    T = ATT_T
    nq = S // T
    nb = T // LANES
    scale = 64 ** -0.5

    def body(i, q_ref, k_ref, v_ref, g_ref, o_ref, og_ref, ogt_ref, st_ref, rem_ref, acc_ref):
        m0 = lax.broadcasted_iota(jnp.int32, (1, LANES), 1) < 64
        top = lax.broadcasted_iota(jnp.int32, (LANES, 1), 0) < 64
        key = lax.broadcasted_iota(jnp.int32, (T, LANES), 0)
        qrow = lax.broadcasted_iota(jnp.int32, (T, LANES), 1)
        r = lax.broadcasted_iota(jnp.int32, (T, T), 0)
        c = lax.broadcasted_iota(jnp.int32, (T, T), 1)
        after = (c > r).astype(BF16)
        after2 = jnp.concatenate([after, after], axis=1)
        qh = _head_q("sb", q_ref, m0, scale)
        rem_ref[...] = jnp.zeros_like(rem_ref)
        acc_ref[...] = jnp.zeros_like(acc_ref)
        chains = [(h, b) for h in range(2) for b in range(nb)]

        def tiles(js, masked_at):
            zss = []
            for j in js:
                kb = k_ref[pl.ds(pl.multiple_of(j * T, T), T), :].astype(BF16)
                zss.append(_split_blocks([_dot_nt(kb, qh[h]) for h in range(2)]))
            lass, sums, hiss, loss = [], [], [], []
            for zs, masked in zip(zss, _mask_flags(js, masked_at)):
                las, sm, his, los = [], [], [], []
                for (h, b), z in zip(chains, zs):
                    sp, la = _softplus_parts(z)
                    if masked:
                        sp = jnp.where(key < qrow + b * LANES, sp, 0.0)
                    hi, lo = _split2(sp)
                    las.append(la)
                    sm.append(jnp.sum(sp, axis=0, keepdims=True))
                    his.append(hi)
                    los.append(lo)
                lass.append(las)
                sums.append(sm)
                hiss.append(_join_blocks(his, nb))
                loss.append(_join_blocks(los, nb))
            rcss = [_cumsum_dot(after2, his, los) for his, los in zip(hiss, loss)]
            wss = []
            for las, sm, rcs, masked in zip(lass, sums, rcss, _mask_flags(js, masked_at)):
                ws = []
                for (h, b), la, s, rc in zip(chains, las, sm, rcs):
                    lanes = slice(b * LANES, (b + 1) * LANES)
                    w = jnp.exp(la - (rem_ref[h, :, lanes] + rc))
                    if masked:
                        w = jnp.where(key < qrow + b * LANES, w, 0.0)
                    ws.append(w.astype(BF16))
                    rem_ref[h, :, lanes] += s
                wss.append(_join_blocks(ws, nb))
            for j, ws in zip(js, wss):
                vtb = v_ref[pl.ds(pl.multiple_of(j * T, T), T), :].T.astype(BF16)
                for h in range(2):
                    acc_ref[h] += _dot(vtb, ws[h])

        _loop_tiles(i, tiles, True)
        o = jnp.where(top, acc_ref[0], acc_ref[1]).T
        o_ref[...] = o
        gt = g_ref[...]
        og = o * (gt * _sigmoid(gt))
        og_ref[...] = og.astype(BF16)
        ogt_ref[...] = og.T.astype(BF16)
        st_ref[0] = rem_ref[0]
        st_ref[1] = rem_ref[1]

    QT = ATT_QSUB * T
    qs, ks, vs, gs = _att_specs("sb", S, QT)
    W = npairs * LANES
    hows = ["rows", None, None, "rows", "rows", "rows", "lanes", "stat", None, None]
    return pl.pallas_call(
        _per_q_tile(body, hows), name=name, grid=(npairs, nq // ATT_QSUB),
        in_specs=[qs, ks, vs, gs],
        out_specs=[pl.BlockSpec((QT, LANES), lambda p, i: (i, p)), pl.BlockSpec((QT, LANES), lambda p, i: (i, p)),
                   pl.BlockSpec((LANES, QT), lambda p, i: (p, i)),
                   pl.BlockSpec((2, ATT_QSUB, 1, T), lambda p, i: (p, i, 0, 0))],
        out_shape=[jax.ShapeDtypeStruct((S, W), F32), jax.ShapeDtypeStruct((S, W), BF16),
                   jax.ShapeDtypeStruct((W, S), BF16),
                   jax.ShapeDtypeStruct((2 * npairs, nq, 1, T), F32)],
        scratch_shapes=[pltpu.VMEM((2, 1, T), F32), pltpu.VMEM((2, LANES, T), F32)],
        compiler_params=_cparams(("parallel", "parallel")),
    )(proj, proj, proj, proj)


def _sb_bwd_t(proj, do, tot, S, npairs, name):
    T = ATT_T
    nq = S // T
    nb = T // LANES
    scale = 64 ** -0.5

    def body(i, q_ref, k_ref, v_ref, do_ref, st_ref, dq_ref, dk_ref, dv_ref, dqt_ref, pre_ref, gpre_ref):

        @pl.when(i == 0)
        def _():
            dk_ref[...] = jnp.zeros_like(dk_ref)
            dv_ref[...] = jnp.zeros_like(dv_ref)

        m0 = lax.broadcasted_iota(jnp.int32, (1, LANES), 1) < 64
        top = lax.broadcasted_iota(jnp.int32, (LANES, 1), 0) < 64
        key = lax.broadcasted_iota(jnp.int32, (T, LANES), 0)
        qrow = lax.broadcasted_iota(jnp.int32, (T, LANES), 1)
        r = lax.broadcasted_iota(jnp.int32, (T, T), 0)
        c = lax.broadcasted_iota(jnp.int32, (T, T), 1)
        upto = (c <= r).astype(BF16)
        upto2 = jnp.concatenate([upto, upto], axis=1)
        left = (c < r).astype(BF16)
        qh = _head_q("sb", q_ref, m0, scale)
        dov = do_ref[...]
        doh = [jnp.where(m0, dov, 0.0).astype(BF16), jnp.where(m0, 0.0, dov).astype(BF16)]
        tot_h = [st_ref[0], st_ref[1]]
        dqt_ref[...] = jnp.zeros_like(dqt_ref)
        pre_ref[...] = jnp.zeros_like(pre_ref)
        gpre_ref[...] = jnp.zeros_like(gpre_ref)
        chains = [(h, b) for h in range(2) for b in range(nb)]

        def tiles(js, masked_at):
            starts = [pl.multiple_of(j * T, T) for j in js]
            zss, dwss = [], []
            for start in starts:
                vb = v_ref[pl.ds(start, T), :].astype(BF16)
                kb = k_ref[pl.ds(start, T), :].astype(BF16)
                zss.append(_split_blocks([_dot_nt(kb, qh[h]) for h in range(2)]))
                dwss.append(_split_blocks([_dot_nt(vb, doh[h]) for h in range(2)]))
            lass, sums, hiss, loss = [], [], [], []
            for zs, masked in zip(zss, _mask_flags(js, masked_at)):
                las, sm, his, los = [], [], [], []
                for (h, b), z in zip(chains, zs):
                    sp, la = _softplus_parts(z)
                    if masked:
                        sp = jnp.where(key < qrow + b * LANES, sp, 0.0)
                    hi, lo = _split2(sp)
                    las.append(la)
                    sm.append(jnp.sum(sp, axis=0, keepdims=True))
                    his.append(hi)
                    los.append(lo)
                lass.append(las)
                sums.append(sm)
                hiss.append(_join_blocks(his, nb))
                loss.append(_join_blocks(los, nb))
            pcss = [_cumsum_dot(upto2, his, los) for his, los in zip(hiss, loss)]
            wss, gss = [], []
            for las, sm, pcs, dws, masked in zip(lass, sums, pcss, dwss, _mask_flags(js, masked_at)):
                ws, gs = [], []
                for (h, b), la, s, pc, dw in zip(chains, las, sm, pcs, dws):
                    lanes = slice(b * LANES, (b + 1) * LANES)
                    w = jnp.exp(la - ((tot_h[h][:, lanes] - pre_ref[h, :, lanes]) - pc))
                    if masked:
                        w = jnp.where(key < qrow + b * LANES, w, 0.0)
                    ws.append(w.astype(BF16))
                    gs.append(dw * w)
                    pre_ref[h, :, lanes] += s
                wss.append(_join_blocks(ws, nb))
                gss.append(gs)
            gcss = [_split_blocks([_dot(left, g) for g in _join_blocks([g.astype(BF16) for g in gs], nb)]) for gs in gss]
            dzss = []
            for las, gs, gcs, masked in zip(lass, gss, gcss, _mask_flags(js, masked_at)):
                dzs = []
                for (h, b), la, g, gc in zip(chains, las, gs, gcs):
                    lanes = slice(b * LANES, (b + 1) * LANES)
                    dz = g - (g + (gpre_ref[h, :, lanes] + gc)) * jnp.exp(la)
                    if masked:
                        dz = jnp.where(key < qrow + b * LANES, dz, 0.0)
                    dzs.append(dz.astype(BF16))
                    gpre_ref[h, :, lanes] += jnp.sum(g, axis=0, keepdims=True)
                dzss.append(_join_blocks(dzs, nb))
            for start, ws, dzs in zip(starts, wss, dzss):
                kt = k_ref[pl.ds(start, T), :].T.astype(BF16)
                dkc = dvc = None
                for h in range(2):
                    dkh = _dot(dzs[h], qh[h])
                    dvh = _dot(ws[h], doh[h])
                    dkc = dkh if dkc is None else dkc + dkh
                    dvc = dvh if dvc is None else dvc + dvh
                    dqt_ref[h] += _dot(kt, dzs[h])
                dk_ref[pl.ds(start, T), :] += dkc
                dv_ref[pl.ds(start, T), :] += dvc

        _loop_tiles(i, tiles, False)
        dq_ref[...] = jnp.where(top, dqt_ref[0], dqt_ref[1]).T * scale

    QT = ATT_QSUB * T
    qs, ks, vs, _ = _att_specs("sb", S, QT)
    W = npairs * LANES
    hows = ["rows", None, None, "rows", "stat", "rows", None, None, None, None, None]
    return pl.pallas_call(
        _per_q_tile(body, hows), name=name, grid=(npairs, nq // ATT_QSUB),
        in_specs=[qs, ks, vs,
                  pl.BlockSpec((QT, LANES), lambda p, i: (i, p)),
                  pl.BlockSpec((2, ATT_QSUB, 1, T), lambda p, i: (p, i, 0, 0))],
        out_specs=[pl.BlockSpec((QT, LANES), lambda p, i: (i, p)), pl.BlockSpec((S, LANES), lambda p, i: (0, p)),
                   pl.BlockSpec((S, LANES), lambda p, i: (0, p))],
        out_shape=[jax.ShapeDtypeStruct((S, W), F32)] * 3,
        scratch_shapes=[pltpu.VMEM((2, LANES, T), F32), pltpu.VMEM((2, 1, T), F32), pltpu.VMEM((2, 1, T), F32)],
        compiler_params=_cparams(("parallel", "arbitrary")),
    )(proj, proj, proj, do, tot)


def _pad_w0(w):
    z = lambda n: jnp.zeros((w.shape[0], n), w.dtype)
    return jnp.concatenate([w[:, 2048:2432], w[:, 2432:2688], z(64), w[:, 2688:2720], z(32),
                            w[:, 1536:2048], w[:, 2720:3232], w[:, 0:512], w[:, 512:1024], w[:, 1024:1536]], axis=1)


def _unpad_w0(wp):
    return jnp.concatenate([wp[:, L0_SBQ:L0_SBQ + 512], wp[:, L0_SBK:L0_SBK + 512], wp[:, L0_SBV:L0_SBV + 512],
                            wp[:, L0_SBG:L0_SBG + 512], wp[:, 0:384], wp[:, 384:640], wp[:, 704:736],
                            wp[:, L0_MLG:L0_MLG + 512]], axis=1)


def _pad_wq(w):
    return jnp.pad(w.reshape(384, 8, 96), ((0, 0), (0, 0), (0, 32))).reshape(384, 1024)


def _unpad_wq(wp):
    return wp.reshape(384, 8, 128)[:, :, :96].reshape(384, 768)


def _pad_wkv(w):
    w3 = w.reshape(256, 8, 128)
    k = jnp.pad(w3[:, :, :64], ((0, 0), (0, 0), (0, 64))).reshape(256, 1024)
    return jnp.concatenate([k, w3[:, :, 64:].reshape(256, 512)], axis=1)


def _unpad_wkv(wp):
    k = wp[:, :1024].reshape(256, 8, 128)[:, :, :64]
    v = wp[:, 1024:].reshape(256, 8, 64)
    return jnp.concatenate([k, v], axis=-1).reshape(256, 1024)


def _pad_w1(w):
    return jnp.concatenate([w, jnp.zeros((w.shape[0], L1_WIDTH - ODD_IN_WIDTH), w.dtype)], axis=1)


def _local_step(x, positions, target, g, w0p, wqp, wkvp, wo0, w1p, wo1, send_early=None):
    S = x.shape[0]
    nq = S // ATT_T
    invf = ROPE_THETA ** (-jnp.arange(0, MLA_ROPE_DIM, 2, dtype=F32) / MLA_ROPE_DIM)
    invf = jnp.concatenate([jnp.zeros((64,), F32), invf, invf, jnp.zeros((32,), F32)]).reshape(1, LANES)
    cosT, s1T, s2T = _rope_tables(positions.reshape(S, 1), invf, "rope_tables")
    bfp = jnp.pad(g["l1_b_f"], ((0, 0), (0, LANES - FOX_HEADS)))

    if isinstance(w1p, tuple):
        w1_shard, finish_w1 = w1p
        proj0, h0t, w1_all = _norm_matmul(x, g["l0_pre_g"], w0p, "l0_in_proj", ride=w1_shard)
        w1p = finish_w1(w1_all)
    else:
        proj0, h0t = _norm_matmul(x, g["l0_pre_g"], w0p, "l0_in_proj")
    qm, km, vm, qnt, cnt = _mla_prep(proj0, g["l0_q_a_g"], g["l0_kv_a_g"], wqp, wkvp, cosT, s1T, s2T, "mla_prep")
    o_sb, og_sb, ogt_sb, tot_sb = _sb_fwd_t(proj0, S, 4, "sb_fwd")
    o_ml, og_ml, ogt_ml, lse_ml = _softmax_fwd("mla", (qm, km, vm, proj0), None, S, 4, "mla_fwd")
    y0, x1 = _out_proj(og_sb, og_ml, 0, 0, wo0, x, g["l0_post_g"], None, "l0_out_proj")

    proj1, h1t = _norm_matmul(x1, g["l1_pre_g"], w1p, "l1_in_proj")
    cfx = _fox_prep(proj1, bfp, "fox_prep")
    c16 = cfx[:, :FOX_HEADS].T
    c_col = jnp.broadcast_to(c16[:, :, None], (FOX_HEADS, S, LANES))
    o_fx, og_fx, ogt_fx, lse_fx = _softmax_fwd("fox", (proj1, proj1, proj1, proj1), c_col, S, 8, "fox_fwd")
    y1, dx2, lsum = _out_proj(og_fx, og_fx, 0, 1, wo1, x1, g["l1_post_g"], target, "l1_out_proj")

    dy1, do1, dgate1, d_post1 = _out_proj_bwd(dx2, y1, g["l1_post_g"], wo1, proj1, (L1_G, L1_G + 512), o_fx, o_fx, 0, 1, "l1_out_bwd")
    dwo1 = _matmul_t(ogt_fx, dy1, "l1_dw_out")
    dq1, dk1, dv1, dck, dcq = _softmax_bwd_t("fox", proj1, proj1, proj1, do1, 0, o_fx, lse_fx, c_col, S, 8,
                                             "fox_bwd")
    dc = jnp.pad((dcq.reshape(FOX_HEADS, S) - dck.reshape(FOX_HEADS, S)).T, ((0, 0), (0, LANES - FOX_HEADS)))
    df, d_bf = _fox_prep_bwd(dc, proj1, bfp, "fox_prep_bwd")
    pieces1 = [(L1_Q, dq1), (L1_K, dk1), (L1_V, dv1), (L1_G, dgate1), (L1_F, df)]
    dx1, d_pre1 = _in_proj_bwd(pieces1, w1p, x1, g["l1_pre_g"], dx2, "l1_in_bwd")
    dw1p = jnp.concatenate(_matmul_t_many(h1t, [dq1, dk1], "l1_dw_in_a")
                           + _matmul_t_many(h1t, [dv1, dgate1, df], "l1_dw_in_b"), axis=1)
    early = None if send_early is None else send_early(dw1p)

    dy0, do0, dgate0, d_post0 = _out_proj_bwd(dx1, y0, g["l0_post_g"], wo0, proj0, (L0_SBG, L0_MLG), o_sb, o_ml, 0, 0,
                                              "l0_out_bwd")
    dwo0 = jnp.concatenate([_matmul_t(ogt_sb, dy0, "l0_dw_out_sb"), _matmul_t(ogt_ml, dy0, "l0_dw_out_mla")], axis=0)
    dsq, dsk, dsv = _sb_bwd_t(proj0, do0, tot_sb, S, 4, "sb_bwd")
    dqm, dkm, dvm = _softmax_bwd_t("mla", qm, km, vm, do0, 4, o_ml, lse_ml, None, S, 4, "mla_bwd")
    dprep, dqb, dkvb, d_qag, d_kvag = _mla_prep_bwd(dqm, dkm, dvm, proj0, g["l0_q_a_g"], g["l0_kv_a_g"], wqp, wkvp,
                                                    cosT, s1T, s2T, "mla_prep_bwd")
    dwqp = _matmul_t(qnt, dqb, "l0_dw_qb")
    dwkvp = _matmul_t(cnt, dkvb, "l0_dw_kvb")
    pieces0 = [(L0_PREP, dprep), (L0_SBG, dgate0), (L0_SBQ, dsq), (L0_SBK, dsk), (L0_SBV, dsv)]
    if send_early is None:
        dx0, d_pre0 = _in_proj_bwd(pieces0, w0p, x, g["l0_pre_g"], dx1, "l0_in_bwd")
        early_slots = None
    else:
        dx0, d_pre0, early_slots = _in_proj_bwd(pieces0, w0p, x, g["l0_pre_g"], dx1, "l0_in_bwd", ride=early)
    dw0p = jnp.concatenate(_matmul_t_many(h0t, [dprep, dgate0], "l0_dw_in_a")
                           + _matmul_t_many(h0t, [dsq, dsk, dsv], "l0_dw_in_b"), axis=1)

    grads = {
        "l0_pre_g": d_pre0, "l0_post_g": d_post0, "l0_w_in": dw0p, "l0_q_a_g": d_qag, "l0_w_q_b": dwqp,
        "l0_kv_a_g": d_kvag, "l0_w_kv_b": dwkvp, "l0_w_out": dwo0, "l1_pre_g": d_pre1, "l1_post_g": d_post1,
        "l1_w_in": dw1p, "l1_b_f": d_bf[:, :FOX_HEADS], "l1_w_out": dwo1,
    }
    grads["early_q"], grads["early_slots"] = early, early_slots
    return lsum, dx0, grads


_ANY = pl.BlockSpec(memory_space=pl.ANY)


def _place():
    return lax.axis_index("x"), lax.axis_index("y"), lax.axis_index("c")


def _other_chips(x, y):
    return [(1 - x, y), (x, 1 - y), (1 - x, 1 - y)]


def _half(rows, c):
    return pl.ds(c * (rows // 2), rows // 2)


def _gather_phases(p_refs, out_refs, send_sems, recv_sems):
    n = len(p_refs)
    x, y, c = _place()
    sibling = (x, y, 1 - c)
    chips = _other_chips(x, y)

    def blk(k, chip, cc):
        return out_refs[k].at[2 * chip[0] + chip[1], _half(p_refs[k].shape[0], cc)]

    def copy(s, src, dst, to):
        return pltpu.make_async_remote_copy(src_ref=src, dst_ref=dst, send_sem=send_sems.at[s],
                                            recv_sem=recv_sems.at[s], device_id=to, device_id_type=MESH)

    def first():
        return [copy(6 * k + j, p_refs[k].at[_half(p_refs[k].shape[0], c)], blk(k, (x, y), c), (*chip, c))
                for j, chip in enumerate(chips) for k in range(n)]

    def begin():
        for cp in first():
            cp.start()

    def finish():
        passed = []
        for j, chip in enumerate(chips):
            for k in range(n):
                copy(6 * k + j, blk(k, chip, c), blk(k, chip, c), (x, y, c)).wait_recv()
                passed.append(copy(6 * k + 3 + j, blk(k, chip, c), blk(k, chip, c), sibling))
                passed[-1].start()
        for j, chip in enumerate(chips):
            for k in range(n):
                copy(6 * k + 3 + j, blk(k, chip, 1 - c), blk(k, chip, 1 - c), (x, y, c)).wait_recv()
        for cp in first() + passed:
            cp.wait_send()

    return begin, finish


def _weight_gather(parts):
    n = len(parts)

    def body(*refs):
        begin, finish = _gather_phases(refs[:n], refs[n:2 * n], refs[2 * n], refs[2 * n + 1])
        begin()
        finish()

    return pl.pallas_call(
        body, name="weight_gather", in_specs=[_ANY] * n, out_specs=[_ANY] * n,
        out_shape=[jax.ShapeDtypeStruct((4,) + a.shape, a.dtype) for a in parts],
        scratch_shapes=[pltpu.SemaphoreType.DMA((6 * n,)), pltpu.SemaphoreType.DMA((6 * n,))],
    )(*parts)


def _grad_core_exchange(ps, name="grad_core_exchange"):
    n = len(ps)

    def body(*refs):
        p_refs, recv_refs, send_sems, recv_sems = refs[:n], refs[n:2 * n], refs[2 * n], refs[2 * n + 1]
        x, y, c = _place()
        give = [pltpu.make_async_remote_copy(src_ref=p_refs[k].at[j, _half(p_refs[k].shape[1], 1 - c)],
                                             dst_ref=recv_refs[k].at[j], send_sem=send_sems.at[4 * k + j],
                                             recv_sem=recv_sems.at[4 * k + j], device_id=(x, y, 1 - c),
                                             device_id_type=MESH) for k in range(n) for j in range(4)]
        for cp in give:
            cp.start()
        for cp in give:
            cp.wait()

    return pl.pallas_call(
        body, name=name, in_specs=[_ANY] * n, out_specs=[_ANY] * n,
        out_shape=[jax.ShapeDtypeStruct((4, p.shape[1] // 2, p.shape[2]), p.dtype) for p in ps],
        scratch_shapes=[pltpu.SemaphoreType.DMA((4 * n,)), pltpu.SemaphoreType.DMA((4 * n,))],
    )(*ps)


def _grad_rows(rows):
    return _pick(rows, (1296, 512, rows))


def _grad_add_cores(p, theirs, c1, name):
    _, rh, cols = theirs.shape
    tr = _grad_rows(rh)

    def body(c_ref, a_ref, b_ref, o_ref):
        o_ref[...] = (a_ref[...] + b_ref[...]).astype(BF16)

    spec = pl.BlockSpec((None, tr, cols), lambda j, r, c: (j, r, 0))
    grid_spec = pltpu.PrefetchScalarGridSpec(
        num_scalar_prefetch=1, grid=(4, rh // tr),
        in_specs=[pl.BlockSpec((None, None, tr, cols), lambda j, r, c: (j, c[0], r, 0)), spec], out_specs=spec)
    return pl.pallas_call(
        body, name=name, grid_spec=grid_spec, out_shape=jax.ShapeDtypeStruct(theirs.shape, BF16),
        compiler_params=_cparams(("parallel", "parallel")),
    )(c1, p.reshape(4, 2, rh, cols), theirs)


def _exchange_phases(q_refs, out_refs, send_sems, recv_sems):
    n = len(q_refs)
    x, y, c = _place()
    me = 2 * x + y
    chips = _other_chips(x, y)

    def sends():
        return [pltpu.make_async_remote_copy(src_ref=q_refs[k].at[2 * chip[0] + chip[1]], dst_ref=out_refs[k].at[me],
                                             send_sem=send_sems.at[3 * k + j], recv_sem=recv_sems.at[3 * k + j],
                                             device_id=(*chip, c), device_id_type=MESH)
                for j, chip in enumerate(chips) for k in range(n)]

    def begin():
        for cp in sends():
            cp.start()

    def finish():
        for j, chip in enumerate(chips):
            for k in range(n):
                slot = out_refs[k].at[2 * chip[0] + chip[1]]
                pltpu.make_async_remote_copy(src_ref=slot, dst_ref=slot, send_sem=send_sems.at[3 * k + j],
                                             recv_sem=recv_sems.at[3 * k + j], device_id=(x, y, c),
                                             device_id_type=MESH).wait_recv()
        for cp in sends():
            cp.wait_send()

    return begin, finish


def _grad_chip_exchange(qs):
    n = len(qs)

    def body(*refs):
        begin, finish = _exchange_phases(refs[:n], refs[n:2 * n], refs[2 * n], refs[2 * n + 1])
        begin()
        finish()

    return pl.pallas_call(
        body, name="grad_chip_exchange", in_specs=[_ANY] * n, out_specs=[_ANY] * n,
        out_shape=[jax.ShapeDtypeStruct(q.shape, q.dtype) for q in qs],
        scratch_shapes=[pltpu.SemaphoreType.DMA((3 * n,)), pltpu.SemaphoreType.DMA((3 * n,))],
    )(*qs)


def _grad_add_chips(q, slots, me1, name):
    _, rh, cols = q.shape
    tr = _grad_rows(rh)

    def body(me_ref, own_ref, s0, s1, s2, s3, o_ref):
        me = me_ref[0]
        t = [jnp.where(me == j, own_ref[...], s[...]).astype(F32) for j, s in enumerate((s0, s1, s2, s3))]
        o_ref[...] = ((t[0] + t[1]) + t[2]) + t[3]

    def slot_spec(j):
        return pl.BlockSpec((None, tr, cols), lambda r, me: (jnp.where(me[0] == j, (j + 1) % 4, j), r, 0))

    grid_spec = pltpu.PrefetchScalarGridSpec(
        num_scalar_prefetch=1, grid=(rh // tr,),
        in_specs=[pl.BlockSpec((None, tr, cols), lambda r, me: (me[0], r, 0))] + [slot_spec(j) for j in range(4)],
        out_specs=pl.BlockSpec((tr, cols), lambda r, me: (r, 0)))
    return pl.pallas_call(
        body, name=name, grid_spec=grid_spec, out_shape=jax.ShapeDtypeStruct(q.shape[1:], F32),
        compiler_params=_cparams(("parallel",)),
    )(me1, q, slots, slots, slots, slots)


def _grad_core_gather(ts):
    n = len(ts)

    def body(*refs):
        t_refs, out_refs, send_sems, recv_sems = refs[:n], refs[n:2 * n], refs[2 * n], refs[2 * n + 1]
        x, y, c = _place()
        give = [pltpu.make_async_remote_copy(src_ref=t_refs[k], dst_ref=out_refs[k], send_sem=send_sems.at[k],
                                             recv_sem=recv_sems.at[k], device_id=(x, y, 1 - c), device_id_type=MESH)
                for k in range(n)]
        for cp in give:
            cp.start()
        for cp in give:
            cp.wait()

    return pl.pallas_call(
        body, name="grad_core_gather", in_specs=[_ANY] * n, out_specs=[_ANY] * n,
        out_shape=[jax.ShapeDtypeStruct(t.shape, t.dtype) for t in ts],
        scratch_shapes=[pltpu.SemaphoreType.DMA((n,)), pltpu.SemaphoreType.DMA((n,))],
    )(*ts)


def _small_allreduce(sp):
    def body(sp_ref, out_ref, gath_ref, send_sems, recv_sems):
        x, y, c = _place()
        me = 4 * x + 2 * y + c
        gath_ref[me] = sp_ref[...]
        peers = []
        for k in range(1, 8):
            px = 1 - x if k & 4 else x
            py = 1 - y if k & 2 else y
            pc = 1 - c if k & 1 else c
            peers.append((px, py, pc))
        sends = [pltpu.make_async_remote_copy(src_ref=sp_ref, dst_ref=gath_ref.at[me], send_sem=send_sems.at[k],
                                              recv_sem=recv_sems.at[k], device_id=peer, device_id_type=MESH)
                 for k, peer in enumerate(peers)]
        for cp in sends:
            cp.start()
        for k, (px, py, pc) in enumerate(peers):
            slot = gath_ref.at[4 * px + 2 * py + pc]
            pltpu.make_async_remote_copy(src_ref=slot, dst_ref=slot, send_sem=send_sems.at[k], recv_sem=recv_sems.at[k],
                                         device_id=(x, y, c), device_id_type=MESH).wait_recv()
        for cp in sends:
            cp.wait_send()
        tot = gath_ref[0]
        for d in range(1, 8):
            tot = tot + gath_ref[d]
        out_ref[...] = tot

    vm = pl.BlockSpec(memory_space=pltpu.VMEM)
    return pl.pallas_call(
        body, name="small_allreduce", in_specs=[vm], out_specs=vm, out_shape=jax.ShapeDtypeStruct(sp.shape, sp.dtype),
        scratch_shapes=[pltpu.VMEM((8,) + sp.shape, sp.dtype), pltpu.SemaphoreType.DMA((7,)), pltpu.SemaphoreType.DMA((7,))],
    )(sp)


def _adamw_update(w, gv, m, v):
    mn = ADAM_B1 * m + (1.0 - ADAM_B1) * gv
    vn = ADAM_B2 * v + (1.0 - ADAM_B2) * (gv * gv)
    m_hat = mn / (1.0 - ADAM_B1 ** ADAM_STEP)
    v_hat = vn / (1.0 - ADAM_B2 ** ADAM_STEP)
    return -ADAM_LR * (m_hat / (jnp.sqrt(v_hat) + ADAM_EPS) + ADAM_WD * w), mn, vn


def _adamw(w, g, m, v, name):
    rows, cols = w.shape

    def body(w_ref, g_ref, m_ref, v_ref, d_ref, mo_ref, vo_ref):
        d_ref[...], mo_ref[...], vo_ref[...] = _adamw_update(w_ref[...], g_ref[...], m_ref[...], v_ref[...])

    if rows % 256 == 0 or cols % 256 != 0:
        tr = _pick(rows, (256, rows))
        grid, spec = (rows // tr,), pl.BlockSpec((tr, cols), lambda r: (r, 0))
    else:
        grid, spec = (cols // 256,), pl.BlockSpec((rows, 256), lambda r: (0, r))
    shp = jax.ShapeDtypeStruct(w.shape, F32)
    return pl.pallas_call(
        body, name=name, grid=grid, in_specs=[spec] * 4, out_specs=[spec] * 3, out_shape=[shp] * 3,
        compiler_params=_cparams(("parallel",)),
    )(w, g, m, v)


MAT_NAMES = ("l0_w_in", "l0_w_q_b", "l0_w_kv_b", "l0_w_out", "l1_w_in", "l1_w_out")
VEC_NAMES = ("l0_pre_g", "l0_post_g", "l0_q_a_g", "l0_kv_a_g", "l1_pre_g", "l1_post_g", "l1_b_f")
WEIGHT_NAMES = ("l0_pre_g", "l0_post_g", "l0_w_in", "l0_q_a_g", "l0_w_q_b", "l0_kv_a_g", "l0_w_kv_b", "l0_w_out",
                "l1_pre_g", "l1_post_g", "l1_w_in", "l1_b_f", "l1_w_out")
MAT_SHARD = {"l0_w_in": (1024, 808), "l0_w_q_b": (384, 192), "l0_w_kv_b": (256, 256), "l0_w_out": (256, 1024),
             "l1_w_in": (1024, 1028), "l1_w_out": (256, 1024)}
ROW_SHARDED = ("l0_w_out", "l1_w_out")
WHOLE_MATS = ("l0_w_in", "l1_w_in")
PACKED_MATS = ("l0_w_q_b", "l0_w_kv_b", "l0_w_out", "l1_w_out")
VEC_LEN = {"l0_pre_g": 1024, "l0_post_g": 1024, "l0_q_a_g": 384, "l0_kv_a_g": 256, "l1_pre_g": 1024,
           "l1_post_g": 1024, "l1_b_f": 16}


def _mat_rows(n):
    r, c = MAT_SHARD[n]
    return r * c // LANES


def _pack_shards(shards):
    return jnp.concatenate([shards[n].reshape(shards[n].shape[:-2] + (_mat_rows(n), LANES)) for n in PACKED_MATS],
                           axis=-2)


def _unpack_shards(pack):
    out, at = {}, 0
    for n in PACKED_MATS:
        out[n] = pack[..., at:at + _mat_rows(n), :].reshape(pack.shape[:-2] + MAT_SHARD[n])
        at += _mat_rows(n)
    return out


def _join_shards(n, s):
    if n in ROW_SHARDED:
        return s.reshape(4 * s.shape[1], s.shape[2])
    return s.transpose(1, 0, 2).reshape(s.shape[1], 4 * s.shape[2])


def _cut_shards(n, w):
    r, c = MAT_SHARD[n]
    if n in ROW_SHARDED:
        return w.reshape(4, r, c)
    return w.reshape(r, 4, c).transpose(1, 0, 2)


def _pack_vecs(vecs):
    parts = []
    for n in VEC_NAMES:
        v = vecs[n].reshape(-1)
        parts.append(jnp.pad(v, (0, VEC_ROWS * LANES - v.shape[0])).reshape(VEC_ROWS, LANES))
    return jnp.concatenate(parts, axis=0)


def _unpack_vecs(pack):
    return {n: pack[k * VEC_ROWS:(k + 1) * VEC_ROWS].reshape(-1)[:VEC_LEN[n]] for k, n in enumerate(VEC_NAMES)}


def kernel(x, positions, l0_pre_g, l0_post_g, l0_w_in, l0_q_a_g, l0_w_q_b, l0_kv_a_g, l0_w_kv_b, l0_w_out, l1_pre_g, l1_post_g, l1_w_in, l1_b_f, l1_w_out, loss_target, m_l0_pre_g, m_l0_post_g, m_l0_w_in, m_l0_q_a_g, m_l0_w_q_b, m_l0_kv_a_g, m_l0_w_kv_b, m_l0_w_out, m_l1_pre_g, m_l1_post_g, m_l1_w_in, m_l1_b_f, m_l1_w_out, v_l0_pre_g, v_l0_post_g, v_l0_w_in, v_l0_q_a_g, v_l0_w_q_b, v_l0_kv_a_g, v_l0_w_kv_b, v_l0_w_out, v_l1_pre_g, v_l1_post_g, v_l1_w_in, v_l1_b_f, v_l1_w_out):
    w = dict(l0_pre_g=l0_pre_g, l0_post_g=l0_post_g, l0_w_in=l0_w_in, l0_q_a_g=l0_q_a_g, l0_w_q_b=l0_w_q_b,
             l0_kv_a_g=l0_kv_a_g, l0_w_kv_b=l0_w_kv_b, l0_w_out=l0_w_out, l1_pre_g=l1_pre_g, l1_post_g=l1_post_g,
             l1_w_in=l1_w_in, l1_b_f=l1_b_f, l1_w_out=l1_w_out)
    m = dict(l0_pre_g=m_l0_pre_g, l0_post_g=m_l0_post_g, l0_w_in=m_l0_w_in, l0_q_a_g=m_l0_q_a_g, l0_w_q_b=m_l0_w_q_b,
             l0_kv_a_g=m_l0_kv_a_g, l0_w_kv_b=m_l0_w_kv_b, l0_w_out=m_l0_w_out, l1_pre_g=m_l1_pre_g,
             l1_post_g=m_l1_post_g, l1_w_in=m_l1_w_in, l1_b_f=m_l1_b_f, l1_w_out=m_l1_w_out)
    v = dict(l0_pre_g=v_l0_pre_g, l0_post_g=v_l0_post_g, l0_w_in=v_l0_w_in, l0_q_a_g=v_l0_q_a_g, l0_w_q_b=v_l0_w_q_b,
             l0_kv_a_g=v_l0_kv_a_g, l0_w_kv_b=v_l0_w_kv_b, l0_w_out=v_l0_w_out, l1_pre_g=v_l1_pre_g,
             l1_post_g=v_l1_post_g, l1_w_in=v_l1_w_in, l1_b_f=v_l1_b_f, l1_w_out=v_l1_w_out)

    cx, cy, cc = _place()
    me1 = jnp.reshape(2 * cx + cy, (1,)).astype(jnp.int32)
    c1 = jnp.reshape(cc, (1,)).astype(jnp.int32)
    w_bf = {n: w[n].astype(BF16) for n in MAT_NAMES}
    def with_mine(got, own):
        return lax.dynamic_update_slice(got, own[None], (2 * cx + cy, 0, 0))

    mine = [_pack_shards(w_bf), w_bf["l0_w_in"]]
    got = [with_mine(g, a) for g, a in zip(_weight_gather(mine), mine)]
    gathered = dict(_unpack_shards(got[0]), l0_w_in=got[1])
    full = {n: _join_shards(n, gathered[n]) for n in MAT_NAMES if n != "l1_w_in"}
    gains = {n: w[n].reshape(1, -1) for n in VEC_NAMES}

    def finish_w1(w1_all):
        return _pad_w1(_join_shards("l1_w_in", with_mine(w1_all, w_bf["l1_w_in"])))

    def send_early(dw1p):
        g1 = _cut_shards("l1_w_in", dw1p[:, :ODD_IN_WIDTH])
        return _grad_add_cores(g1, _grad_core_exchange([g1], "grad_core_exchange_l1_w_in")[0], c1,
                               "grad_add_cores_l1_w_in")

    lsum, dx0, grads = _local_step(
        x[0], positions[0], loss_target[0], gains, _pad_w0(full["l0_w_in"]), _pad_wq(full["l0_w_q_b"]),
        _pad_wkv(full["l0_w_kv_b"]), full["l0_w_out"], (w_bf["l1_w_in"], finish_w1), full["l1_w_out"], send_early)

    gfull = {"l0_w_in": _unpad_w0(grads["l0_w_in"]), "l0_w_q_b": _unpad_wq(grads["l0_w_q_b"]),
             "l0_w_kv_b": _unpad_wkv(grads["l0_w_kv_b"]), "l0_w_out": grads["l0_w_out"],
             "l1_w_out": grads["l1_w_out"]}
    cut = {n: _cut_shards(n, gfull[n]) for n in gfull}
    tags = ("packed", "l0_w_in")
    g_parts = [_pack_shards(cut), cut["l0_w_in"]]
    q_cores = [_grad_add_cores(p, t, c1, "grad_add_cores_" + tag)
               for p, t, tag in zip(g_parts, _grad_core_exchange(g_parts), tags)]
    slots = list(_grad_chip_exchange(q_cores)) + [grads["early_slots"]]
    q_cores.append(grads["early_q"])
    g_mine = [_grad_add_chips(q, s, me1, "grad_add_chips_" + tag)
              for q, s, tag in zip(q_cores, slots, tags + ("l1_w_in",))]
    g_theirs = _grad_core_gather(g_mine)

    small = _small_allreduce(jnp.concatenate([_pack_vecs({n: grads[n] for n in VEC_NAMES}),
                                              lsum.reshape(D_MODEL // LANES, LANES)], axis=0))
    g_small = small[:SMALL_ROWS]
    loss = 0.5 * jnp.sum(small[SMALL_ROWS:]) / float(D_MODEL)

    whole = [jnp.concatenate([lax.select(cc == 0, a, b), lax.select(cc == 0, b, a)], axis=0)
             for a, b in zip(g_mine, g_theirs)]
    g_mats = dict(_unpack_shards(whole[0]), **dict(zip(WHOLE_MATS, whole[1:])))
    d_mats, m_mats, v_mats = {}, {}, {}
    for n in PACKED_MATS:
        d_mats[n], m_mats[n], v_mats[n] = _adamw(w[n], g_mats[n], m[n], v[n], "adamw_" + n)
    for n in WHOLE_MATS:
        gt = g_mats[n].T
        outs = _adamw(w[n].T, gt, m[n].T, v[n].T, "adamw_" + n)
        g_mats[n], d_mats[n], m_mats[n], v_mats[n] = gt.T, outs[0].T, outs[1].T, outs[2].T
    d_small, m_small, v_small = _adamw(_pack_vecs(w), g_small, _pack_vecs(m), _pack_vecs(v), "adamw_vecs")

    def leaves(mats, vec_pack):
        out = dict(mats)
        out.update(_unpack_vecs(vec_pack))
        return [out[n] for n in WEIGHT_NAMES]

    return (loss, dx0[None], *leaves(g_mats, g_small), *leaves(d_mats, d_small), *leaves(m_mats, m_small),
            *leaves(v_mats, v_small))
```

```python
import jax
import jax.numpy as jnp
from jax import lax
from jax.experimental import pallas as pl
from jax.experimental.pallas import tpu as pltpu

F32 = jnp.float32
BF16 = jnp.bfloat16
MESH = pl.DeviceIdType.MESH

D_MODEL = 1024
RMS_EPS = 1e-6
ROPE_THETA = 10000.0
SB_WIDTH = 512
MLA_Q_LORA = 384
MLA_KV_LORA = 256
MLA_ROPE_DIM = 32
MLA_WIDTH = 512
FOX_WIDTH = 1024
FOX_HEADS = 16
EVEN_IN_WIDTH = 3232
ODD_IN_WIDTH = 4112

ADAM_LR = 0.001
ADAM_B1 = 0.9
ADAM_B2 = 0.999
ADAM_EPS = 1e-08
ADAM_WD = 0.01
ADAM_STEP = 10

LANES = 128
VMEM_LIMIT = 56 * 1024 * 1024

L0_PREP = 0
L0_PREP_W = 768
L0_SBG = 768
L0_MLG = 1280
L0_SBQ = 1792
L0_SBK = 2304
L0_SBV = 2816
L0_WIDTH = 3328
L1_Q = 0
L1_K = 1024
L1_V = 2048
L1_G = 3072
L1_F = 4096
L1_WIDTH = 4224

ATT_T = 256
ATT_GROUP = 4
ATT_QSUB = 2
NEG = -1e30

VEC_ROWS = 8
SMALL_ROWS = 7 * VEC_ROWS


def _cparams(sem, **kw):
    return pltpu.CompilerParams(dimension_semantics=sem, vmem_limit_bytes=VMEM_LIMIT, **kw)


def _dot(a, b):
    return lax.dot_general(a, b, (((1,), (0,)), ((), ())), preferred_element_type=F32)


def _dot_nt(a, b):
    return lax.dot_general(a, b, (((1,), (1,)), ((), ())), preferred_element_type=F32)


def _sigmoid(x):
    return 1.0 / (1.0 + jnp.exp(-x))


def _rstd(x):
    return lax.rsqrt(jnp.mean(x * x, axis=-1, keepdims=True) + RMS_EPS)


def _norm_bwd(x, g, dy):
    r = _rstd(x)
    xn = x * r
    dxn = dy * g
    dx = r * (dxn - xn * jnp.mean(dxn * xn, axis=-1, keepdims=True))
    return dx, dy * xn


def _split3(x):
    hi = x.astype(BF16)
    r1 = x - hi.astype(F32)
    mid = r1.astype(BF16)
    lo = (r1 - mid.astype(F32)).astype(BF16)
    return hi, mid, lo


def _wide_tile(n, cap=1792):
    return max(t for t in range(LANES, min(n, cap) + 1, LANES) if n % t == 0)


def _pick(n, cands):
    for c in cands:
        if n % c == 0:
            return c
    raise ValueError(n)


def _norm_matmul(x, g, w, name, ride=None):
    S, K = x.shape
    N = w.shape[1]
    tm = _pick(S, (1024, 512, 256))
    tn = _wide_tile(N)
    ni, nj = S // tm, N // tn

    def body(x_ref, g_ref, w_ref, *rest):
        if ride is None:
            o_ref, ht_ref, h_ref = rest
        else:
            a_ref, o_ref, ht_ref, land_ref, h_ref, send_sems, recv_sems = rest
            begin, finish = _gather_phases([a_ref], [land_ref], send_sems, recv_sems)
            pl.when((pl.program_id(0) == 0) & (pl.program_id(1) == 0))(begin)

        @pl.when(pl.program_id(1) == 0)
        def _():
            xv = x_ref[...]
            h = (xv * _rstd(xv)) * g_ref[...]
            h_ref[...] = h.astype(BF16)
            ht_ref[...] = h.T.astype(BF16)
        o_ref[...] = _dot(h_ref[...], w_ref[...])
        if ride is not None:
            pl.when((pl.program_id(0) == ni - 1) & (pl.program_id(1) == nj - 1))(finish)

    in_specs = [pl.BlockSpec((tm, K), lambda i, j: (i, 0)),
                pl.BlockSpec((1, K), lambda i, j: (0, 0)),
                pl.BlockSpec((K, tn), lambda i, j: (0, j))]
    out_specs = [pl.BlockSpec((tm, tn), lambda i, j: (i, j)), pl.BlockSpec((K, tm), lambda i, j: (0, i))]
    out_shape = [jax.ShapeDtypeStruct((S, N), F32), jax.ShapeDtypeStruct((K, S), BF16)]
    scratch = [pltpu.VMEM((tm, K), BF16)]
    args = [x, g, w]
    if ride is not None:
        in_specs.append(_ANY)
        out_specs.append(_ANY)
        out_shape.append(jax.ShapeDtypeStruct((4,) + ride.shape, ride.dtype))
        scratch += [pltpu.SemaphoreType.DMA((6,)), pltpu.SemaphoreType.DMA((6,))]
        args.append(ride)
    return pl.pallas_call(
        body, name=name, grid=(ni, nj), in_specs=in_specs, out_specs=out_specs, out_shape=out_shape,
        scratch_shapes=scratch,
        compiler_params=_cparams(("parallel", "arbitrary") if ride is None else ("arbitrary", "arbitrary")),
    )(*args)


def _matmul_t(at, b, name):
    M, S = at.shape
    N = b.shape[1]
    tn = _wide_tile(N)
    ts = _pick(S, (512, 256))

    def body(a_ref, b_ref, o_ref):
        @pl.when(pl.program_id(1) == 0)
        def _():
            o_ref[...] = jnp.zeros_like(o_ref)
        o_ref[...] += _dot(a_ref[...], b_ref[...].astype(BF16))

    return pl.pallas_call(
        body, name=name, grid=(N // tn, S // ts),
        in_specs=[pl.BlockSpec((M, ts), lambda j, k: (0, k)),
                  pl.BlockSpec((ts, tn), lambda j, k: (k, j))],
        out_specs=pl.BlockSpec((M, tn), lambda j, k: (0, j)),
        out_shape=jax.ShapeDtypeStruct((M, N), F32),
        compiler_params=_cparams(("parallel", "arbitrary")),
    )(at, b)


def _matmul_t_many(at, bs, name):
    M, S = at.shape
    ts = _pick(S, (512, 256))
    n = len(bs)

    def body(*refs):
        a_ref, b_refs, o_refs = refs[0], refs[1:1 + n], refs[1 + n:]

        @pl.when(pl.program_id(0) == 0)
        def _():
            for o_ref in o_refs:
                o_ref[...] = jnp.zeros_like(o_ref)

        a = a_ref[...]
        for b_ref, o_ref in zip(b_refs, o_refs):
            o_ref[...] += _dot(a, b_ref[...].astype(BF16))

    return pl.pallas_call(
        body, name=name, grid=(S // ts,),
        in_specs=[pl.BlockSpec((M, ts), lambda k: (0, k))] + [pl.BlockSpec((ts, b.shape[1]), lambda k: (k, 0)) for b in bs],
        out_specs=[pl.BlockSpec((M, b.shape[1]), lambda k: (0, 0)) for b in bs],
        out_shape=[jax.ShapeDtypeStruct((M, b.shape[1]), F32) for b in bs],
        compiler_params=_cparams(("arbitrary",)),
    )(at, *bs)


def _in_proj_bwd(pieces, w, x, g, dx_up, name, ride=None):
    S, K = x.shape
    N = w.shape[1]
    tm = _pick(S, (256,))
    nsteps = S // tm
    offs = [off for off, _ in pieces]
    arrs = [a for _, a in pieces]

    def body(*refs):
        d_refs = refs[:len(arrs)]
        if ride is None:
            w_ref, x_ref, g_ref, u_ref, dx_ref, dg_ref = refs[len(arrs):]
        else:
            w_ref, x_ref, g_ref, u_ref, q_ref, dx_ref, dg_ref, slots_ref, send_sems, recv_sems = refs[len(arrs):]
            begin, finish = _exchange_phases([q_ref], [slots_ref], send_sems, recv_sems)
            pl.when(pl.program_id(0) == 0)(begin)

        @pl.when(pl.program_id(0) == 0)
        def _():
            dg_ref[...] = jnp.zeros_like(dg_ref)

        acc = None
        for off, d_ref in zip(offs, d_refs):
            part = _dot_nt(d_ref[...].astype(BF16), w_ref[:, off:off + d_ref.shape[1]])
            acc = part if acc is None else acc + part
        dx, dgrow = _norm_bwd(x_ref[...], g_ref[...], acc)
        dx_ref[...] = u_ref[...] + dx
        dg_ref[...] += jnp.sum(dgrow, axis=0, keepdims=True)
        if ride is not None:
            pl.when(pl.program_id(0) == nsteps - 1)(finish)

    row = lambda i: (i, 0)
    fixed = lambda i: (0, 0)
    in_specs = [pl.BlockSpec((tm, a.shape[1]), row) for a in arrs] + [
        pl.BlockSpec((K, N), fixed), pl.BlockSpec((tm, K), row), pl.BlockSpec((1, K), fixed), pl.BlockSpec((tm, K), row)]
    out_specs = [pl.BlockSpec((tm, K), row), pl.BlockSpec((1, K), fixed)]
    out_shape = [jax.ShapeDtypeStruct((S, K), F32), jax.ShapeDtypeStruct((1, K), F32)]
    args = [*arrs, w, x, g, dx_up]
    scratch = []
    if ride is not None:
        in_specs.append(_ANY)
        out_specs.append(_ANY)
        out_shape.append(jax.ShapeDtypeStruct(ride.shape, ride.dtype))
        scratch = [pltpu.SemaphoreType.DMA((3,)), pltpu.SemaphoreType.DMA((3,))]
        args.append(ride)
    return pl.pallas_call(
        body, name=name, grid=(nsteps,), in_specs=in_specs, out_specs=out_specs, out_shape=out_shape,
        scratch_shapes=scratch, compiler_params=_cparams(("arbitrary",)),
    )(*args)


def _out_proj(og_a, og_b, blk_a, blk_b, w, x, g, target, name):
    S = x.shape[0]
    D = x.shape[1]
    tm = _pick(S, (512, 256))
    with_loss = target is not None

    def body(*refs):
        if with_loss:
            a_ref, b_ref, wa_ref, wb_ref, x_ref, g_ref, t_ref, y_ref, o_ref, l_ref = refs
        else:
            a_ref, b_ref, wa_ref, wb_ref, x_ref, g_ref, y_ref, o_ref = refs
        y = _dot(a_ref[...], wa_ref[...]) + _dot(b_ref[...], wb_ref[...])
        y_ref[...] = y
        xn = x_ref[...] + (y * _rstd(y)) * g_ref[...]
        if with_loss:
            @pl.when(pl.program_id(0) == 0)
            def _():
                l_ref[...] = jnp.zeros_like(l_ref)
            d = xn - t_ref[...]
            o_ref[...] = d / float(D)
            l_ref[...] += jnp.sum(d * d, axis=0, keepdims=True)
        else:
            o_ref[...] = xn

    row = lambda i: (i, 0)
    in_specs = [pl.BlockSpec((tm, 512), lambda i: (i, blk_a)),
                pl.BlockSpec((tm, 512), lambda i: (i, blk_b)),
                pl.BlockSpec((512, D), lambda i: (0, 0)),
                pl.BlockSpec((512, D), lambda i: (1, 0)),
                pl.BlockSpec((tm, D), row),
                pl.BlockSpec((1, D), lambda i: (0, 0))]
    out_specs = [pl.BlockSpec((tm, D), row), pl.BlockSpec((tm, D), row)]
    out_shape = [jax.ShapeDtypeStruct((S, D), F32), jax.ShapeDtypeStruct((S, D), F32)]
    args = [og_a, og_b, w, w, x, g]
    if with_loss:
        in_specs.append(pl.BlockSpec((tm, D), row))
        out_specs.append(pl.BlockSpec((1, D), lambda i: (0, 0)))
        out_shape.append(jax.ShapeDtypeStruct((1, D), F32))
        args.append(target)
    return pl.pallas_call(
        body, name=name, grid=(S // tm,), in_specs=in_specs, out_specs=out_specs, out_shape=out_shape,
        compiler_params=_cparams(("arbitrary",)),
    )(*args)


def _out_proj_bwd(dx_up, y, g, w, proj, gate_offs, o_a, o_b, oblk_a, oblk_b, name):
    S, D = y.shape
    tm = _pick(S, (256,))
    gblk = [off // 256 + c for off in gate_offs for c in range(2)]

    def body(u_ref, y_ref, g_ref, w_ref, g0, g1, g2, g3, oa_ref, ob_ref, dy_ref, do_ref, dgate_ref, dg_ref):
        @pl.when(pl.program_id(0) == 0)
        def _():
            dg_ref[...] = jnp.zeros_like(dg_ref)
        dy, dgrow = _norm_bwd(y_ref[...], g_ref[...], u_ref[...])
        dg_ref[...] += jnp.sum(dgrow, axis=0, keepdims=True)
        dyb = dy.astype(BF16)
        dy_ref[...] = dyb
        dog = _dot_nt(dyb, w_ref[...])
        gates = (g0, g1, g2, g3)
        for c in range(4):
            gt = gates[c][...]
            sg = _sigmoid(gt)
            o_ref = oa_ref if c < 2 else ob_ref
            ov = o_ref[:, (c % 2) * 256:(c % 2 + 1) * 256]
            dc = dog[:, c * 256:(c + 1) * 256]
            do_ref[:, c * 256:(c + 1) * 256] = dc * (gt * sg)
            dgate_ref[:, c * 256:(c + 1) * 256] = dc * ov * (sg * (1.0 + gt * (1.0 - sg)))

    row = lambda i: (i, 0)
    gspec = lambda c: pl.BlockSpec((tm, 256), lambda i: (i, gblk[c]))
    return pl.pallas_call(
        body, name=name, grid=(S // tm,),
        in_specs=[pl.BlockSpec((tm, D), row), pl.BlockSpec((tm, D), row), pl.BlockSpec((1, D), lambda i: (0, 0)),
                  pl.BlockSpec((D, D), lambda i: (0, 0)),
                  gspec(0), gspec(1), gspec(2), gspec(3),
                  pl.BlockSpec((tm, 512), lambda i: (i, oblk_a)),
                  pl.BlockSpec((tm, 512), lambda i: (i, oblk_b))],
        out_specs=[pl.BlockSpec((tm, D), row), pl.BlockSpec((tm, D), row), pl.BlockSpec((tm, D), row),
                   pl.BlockSpec((1, D), lambda i: (0, 0))],
        out_shape=[jax.ShapeDtypeStruct((S, D), BF16), jax.ShapeDtypeStruct((S, D), F32),
                   jax.ShapeDtypeStruct((S, D), F32), jax.ShapeDtypeStruct((1, D), F32)],
        compiler_params=_cparams(("arbitrary",)),
    )(dx_up, y, g, w, proj, proj, proj, proj, o_a, o_b)


def _rope_tables(pos, invf, name):
    S = pos.shape[0]
    tm = _pick(S, (512, 256))

    def body(p_ref, f_ref, c_ref, s1_ref, s2_ref):
        lane = lax.broadcasted_iota(jnp.int32, (1, LANES), 1)
        ang = p_ref[...].astype(F32) * f_ref[...]
        c, s = jnp.cos(ang), jnp.sin(ang)
        c_ref[...] = jnp.where((lane >= 64) & (lane < 96), c, 1.0)
        s1_ref[...] = jnp.where((lane >= 64) & (lane < 80), -s, 0.0)
        s2_ref[...] = jnp.where((lane >= 80) & (lane < 96), s, 0.0)

    spec = pl.BlockSpec((tm, LANES), lambda i: (i, 0))
    return pl.pallas_call(
        body, name=name, grid=(S // tm,),
        in_specs=[pl.BlockSpec((tm, 1), lambda i: (i, 0)), pl.BlockSpec((1, LANES), lambda i: (0, 0))],
        out_specs=[spec, spec, spec],
        out_shape=[jax.ShapeDtypeStruct((S, LANES), F32)] * 3,
        compiler_params=_cparams(("parallel",)),
    )(pos, invf)


def _rope(x, c, s1, s2):
    return x * c + pltpu.roll(x, LANES - 16, 1) * s1 + pltpu.roll(x, 16, 1) * s2


def _rope_t(d, c, s1, s2):
    return d * c + pltpu.roll(d * s1, 16, 1) + pltpu.roll(d * s2, LANES - 16, 1)


def _mla_prep(proj, gq, gkv, wq, wkv, cosT, s1T, s2T, name):
    S = proj.shape[0]
    tm = _pick(S, (256,))

    def body(p_ref, gq_ref, gkv_ref, wq_ref, wkv_ref, c_ref, s1_ref, s2_ref, q_ref, k_ref, v_ref, qn_ref, cn_ref):
        qa = p_ref[:, 0:384]
        ckv = p_ref[:, 384:640]
        kr = p_ref[:, 640:768]
        qn32 = (qa * _rstd(qa)) * gq_ref[...]
        cn32 = (ckv * _rstd(ckv)) * gkv_ref[...]
        qn = qn32.astype(BF16)
        cn = cn32.astype(BF16)
        qn_ref[...] = qn32.T.astype(BF16)
        cn_ref[...] = cn32.T.astype(BF16)
        qb = _dot(qn, wq_ref[...])
        kvb = _dot(cn, wkv_ref[...])
        c, s1, s2 = c_ref[...], s1_ref[...], s2_ref[...]
        krr = _rope(kr, c, s1, s2)
        for h in range(8):
            sl = slice(h * LANES, (h + 1) * LANES)
            q_ref[:, sl] = _rope(qb[:, sl], c, s1, s2)
            k_ref[:, sl] = kvb[:, sl] + krr
        v_ref[...] = kvb[:, 1024:1536]

    row = lambda i: (i, 0)
    fixed = lambda i: (0, 0)
    tspec = pl.BlockSpec((tm, LANES), row)
    return pl.pallas_call(
        body, name=name, grid=(S // tm,),
        in_specs=[pl.BlockSpec((tm, L0_PREP_W), lambda i: (i, L0_PREP // L0_PREP_W)),
                  pl.BlockSpec((1, 384), fixed), pl.BlockSpec((1, 256), fixed),
                  pl.BlockSpec((384, 1024), fixed), pl.BlockSpec((256, 1536), fixed), tspec, tspec, tspec],
        out_specs=[pl.BlockSpec((tm, 1024), row), pl.BlockSpec((tm, 1024), row), pl.BlockSpec((tm, 512), row),
                   pl.BlockSpec((384, tm), lambda i: (0, i)), pl.BlockSpec((256, tm), lambda i: (0, i))],
        out_shape=[jax.ShapeDtypeStruct((S, 1024), F32), jax.ShapeDtypeStruct((S, 1024), F32),
                   jax.ShapeDtypeStruct((S, 512), F32), jax.ShapeDtypeStruct((384, S), BF16),
                   jax.ShapeDtypeStruct((256, S), BF16)],
        compiler_params=_cparams(("parallel",)),
    )(proj, gq, gkv, wq, wkv, cosT, s1T, s2T)


def _mla_prep_bwd(dq, dk, dv, proj, gq, gkv, wq, wkv, cosT, s1T, s2T, name):
    S = proj.shape[0]
    tm = _pick(S, (256,))

    def body(dq_ref, dk_ref, dv_ref, p_ref, gq_ref, gkv_ref, wq_ref, wkv_ref, c_ref, s1_ref, s2_ref,
             dp_ref, dqb_ref, dkvb_ref, dgq_ref, dgkv_ref):
        @pl.when(pl.program_id(0) == 0)
        def _():
            dgq_ref[...] = jnp.zeros_like(dgq_ref)
            dgkv_ref[...] = jnp.zeros_like(dgkv_ref)
        c, s1, s2 = c_ref[...], s1_ref[...], s2_ref[...]
        lane = lax.broadcasted_iota(jnp.int32, (1, LANES), 1)
        dkr = jnp.zeros((tm, LANES), F32)
        for h in range(8):
            sl = slice(h * LANES, (h + 1) * LANES)
            dqb_ref[:, sl] = _rope_t(dq_ref[:, sl], c, s1, s2).astype(BF16)
            dkh = dk_ref[:, sl]
            dkvb_ref[:, sl] = dkh.astype(BF16)
            dkr = dkr + dkh
        dkvb_ref[:, 1024:1536] = dv_ref[...].astype(BF16)
        dkr = jnp.where((lane >= 64) & (lane < 96), _rope_t(dkr, c, s1, s2), 0.0)
        dqn = _dot_nt(dqb_ref[...], wq_ref[...])
        dcn = _dot_nt(dkvb_ref[...], wkv_ref[...])
        dqa, gq_row = _norm_bwd(p_ref[:, 0:384], gq_ref[...], dqn)
        dckv, gkv_row = _norm_bwd(p_ref[:, 384:640], gkv_ref[...], dcn)
        dp_ref[:, 0:384] = dqa
        dp_ref[:, 384:640] = dckv
        dp_ref[:, 640:768] = dkr
        dgq_ref[...] += jnp.sum(gq_row, axis=0, keepdims=True)
        dgkv_ref[...] += jnp.sum(gkv_row, axis=0, keepdims=True)

    row = lambda i: (i, 0)
    fixed = lambda i: (0, 0)
    tspec = pl.BlockSpec((tm, LANES), row)
    return pl.pallas_call(
        body, name=name, grid=(S // tm,),
        in_specs=[pl.BlockSpec((tm, 1024), row), pl.BlockSpec((tm, 1024), row), pl.BlockSpec((tm, 512), row),
                  pl.BlockSpec((tm, L0_PREP_W), lambda i: (i, L0_PREP // L0_PREP_W)),
                  pl.BlockSpec((1, 384), fixed), pl.BlockSpec((1, 256), fixed),
                  pl.BlockSpec((384, 1024), fixed), pl.BlockSpec((256, 1536), fixed), tspec, tspec, tspec],
        out_specs=[pl.BlockSpec((tm, L0_PREP_W), row), pl.BlockSpec((tm, 1024), row), pl.BlockSpec((tm, 1536), row),
                   pl.BlockSpec((1, 384), fixed), pl.BlockSpec((1, 256), fixed)],
        out_shape=[jax.ShapeDtypeStruct((S, L0_PREP_W), F32), jax.ShapeDtypeStruct((S, 1024), BF16),
                   jax.ShapeDtypeStruct((S, 1536), BF16), jax.ShapeDtypeStruct((1, 384), F32),
                   jax.ShapeDtypeStruct((1, 256), F32)],
        compiler_params=_cparams(("arbitrary",)),
    )(dq, dk, dv, proj, gq, gkv, wq, wkv, cosT, s1T, s2T)


def _fox_prep(proj, bf, name):
    S = proj.shape[0]
    tm = _pick(S, (256,))

    def body(f_ref, b_ref, c_ref, carry_ref):
        @pl.when(pl.program_id(0) == 0)
        def _():
            carry_ref[...] = jnp.zeros_like(carry_ref)
        u = f_ref[...] + b_ref[...]
        lf = jnp.minimum(u, 0.0) - jnp.log(1.0 + jnp.exp(-jnp.abs(u)))
        r = lax.broadcasted_iota(jnp.int32, (tm, tm), 0)
        cidx = lax.broadcasted_iota(jnp.int32, (tm, tm), 1)
        tri = (cidx <= r).astype(BF16)
        hi, mid, lo = _split3(lf)
        c = carry_ref[...] + (_dot(tri, hi) + _dot(tri, mid) + _dot(tri, lo))
        c_ref[...] = c
        carry_ref[...] = c[tm - 1:tm, :]

    return pl.pallas_call(
        body, name=name, grid=(S // tm,),
        in_specs=[pl.BlockSpec((tm, LANES), lambda i: (i, L1_F // LANES)), pl.BlockSpec((1, LANES), lambda i: (0, 0))],
        out_specs=pl.BlockSpec((tm, LANES), lambda i: (i, 0)),
        out_shape=jax.ShapeDtypeStruct((S, LANES), F32),
        scratch_shapes=[pltpu.VMEM((1, LANES), F32)],
        compiler_params=_cparams(("arbitrary",)),
    )(proj, bf)


def _fox_prep_bwd(dc, proj, bf, name):
    S = proj.shape[0]
    tm = _pick(S, (256,))
    nb = S // tm

    def body(dc_ref, f_ref, b_ref, df_ref, db_ref, carry_ref):
        @pl.when(pl.program_id(0) == 0)
        def _():
            carry_ref[...] = jnp.zeros_like(carry_ref)
            db_ref[...] = jnp.zeros_like(db_ref)
        r = lax.broadcasted_iota(jnp.int32, (tm, tm), 0)
        cidx = lax.broadcasted_iota(jnp.int32, (tm, tm), 1)
        tri = (cidx >= r).astype(BF16)
        hi, mid, lo = _split3(dc_ref[...])
        dlf = carry_ref[...] + (_dot(tri, hi) + _dot(tri, mid) + _dot(tri, lo))
        carry_ref[...] = dlf[0:1, :]
        u = f_ref[...] + b_ref[...]
        e = jnp.exp(-jnp.abs(u))
        sneg = jnp.where(u >= 0.0, e, 1.0) / (1.0 + e)
        lane = lax.broadcasted_iota(jnp.int32, (1, LANES), 1)
        df = jnp.where(lane < FOX_HEADS, dlf * sneg, 0.0)
        df_ref[...] = df
        db_ref[...] += jnp.sum(df, axis=0, keepdims=True)

    return pl.pallas_call(
        body, name=name, grid=(nb,),
        in_specs=[pl.BlockSpec((tm, LANES), lambda i: (nb - 1 - i, 0)),
                  pl.BlockSpec((tm, LANES), lambda i: (nb - 1 - i, L1_F // LANES)),
                  pl.BlockSpec((1, LANES), lambda i: (0, 0))],
        out_specs=[pl.BlockSpec((tm, LANES), lambda i: (nb - 1 - i, 0)), pl.BlockSpec((1, LANES), lambda i: (0, 0))],
        out_shape=[jax.ShapeDtypeStruct((S, LANES), F32), jax.ShapeDtypeStruct((1, LANES), F32)],
        scratch_shapes=[pltpu.VMEM((1, LANES), F32)],
        compiler_params=_cparams(("arbitrary",)),
    )(dc, proj, bf)


def _att_specs(kind, S, T):
    if kind == "sb":
        qo, ko, vo, go = L0_SBQ // LANES, L0_SBK // LANES, L0_SBV // LANES, L0_SBG // LANES
    elif kind == "fox":
        qo, ko, vo, go = L1_Q // LANES, L1_K // LANES, L1_V // LANES, L1_G // LANES
    else:
        go = L0_MLG // LANES
        return (pl.BlockSpec((T, 256), lambda p, i: (i, p)), pl.BlockSpec((S, 256), lambda p, i: (0, p)),
                pl.BlockSpec((S, LANES), lambda p, i: (0, p)), pl.BlockSpec((T, LANES), lambda p, i: (i, go + p)))
    return (pl.BlockSpec((T, LANES), lambda p, i: (i, qo + p)), pl.BlockSpec((S, LANES), lambda p, i: (0, ko + p)),
            pl.BlockSpec((S, LANES), lambda p, i: (0, vo + p)), pl.BlockSpec((T, LANES), lambda p, i: (i, go + p)))


def _per_q_tile(tile_body, hows):
    T = ATT_T

    def view(ref, u, how):
        if how == "rows":
            return ref.at[pl.ds(u * T, T)]
        if how == "lanes":
            return ref.at[:, pl.ds(u * T, T)]
        if how == "stat":
            return ref.at[:, u]
        return ref

    def body(*refs):
        for u in range(ATT_QSUB):
            tile_body(pl.program_id(1) * ATT_QSUB + u, *[view(r, u, how) for r, how in zip(refs, hows)])

    return body


def _mask_flags(js, masked_at):
    return [t == masked_at for t in range(len(js))]


def _loop_tiles(i, tiles, right_to_left, G=ATT_GROUP):
    ng = i // G
    rest = i - ng * G

    def leftover():
        for r in range(G):
            @pl.when(rest == r)
            def _():
                if right_to_left:
                    tiles([i - u for u in range(r + 1)], 0)
                else:
                    tiles([ng * G + u for u in range(r + 1)], r)

    def group(g, carry):
        if right_to_left:
            tiles([ng * G - 1 - (g * G + u) for u in range(G)], None)
        else:
            tiles([g * G + u for u in range(G)], None)
        return carry

    if right_to_left:
        leftover()
    lax.fori_loop(0, ng, group, 0)
    if not right_to_left:
        leftover()


def _head_q(kind, q_ref, m0, scale):
    if kind == "mla":
        return [q_ref[:, 0:LANES].astype(BF16), q_ref[:, LANES:2 * LANES].astype(BF16)]
    qv = q_ref[...] * scale
    return [jnp.where(m0, qv, 0.0).astype(BF16), jnp.where(m0, 0.0, qv).astype(BF16)]


def _head_k(kind, k_ref, start, T):
    if kind == "mla":
        return [k_ref[pl.ds(start, T), 0:LANES].astype(BF16), k_ref[pl.ds(start, T), LANES:2 * LANES].astype(BF16)]
    kb = k_ref[pl.ds(start, T), :].astype(BF16)
    return [kb, kb]


def _softmax_fwd(kind, qkvg, c_col, S, npairs, name):
    T = ATT_T
    nq = S // T
    fox = kind == "fox"
    scale = (96 if kind == "mla" else 64) ** -0.5

    def body(i, *refs):
        if fox:
            q_ref, k_ref, v_ref, g_ref, cc_ref, o_ref, og_ref, ogt_ref, st_ref, m_ref, acc_ref = refs
        else:
            q_ref, k_ref, v_ref, g_ref, o_ref, og_ref, ogt_ref, st_ref, m_ref, acc_ref = refs
        m0 = lax.broadcasted_iota(jnp.int32, (1, LANES), 1) < 64
        top = lax.broadcasted_iota(jnp.int32, (LANES, 1), 0) < 64
        key = lax.broadcasted_iota(jnp.int32, (T, LANES), 0)
        qrow = lax.broadcasted_iota(jnp.int32, (T, LANES), 1)
        qh = _head_q(kind, q_ref, m0, scale)
        m_ref[...] = jnp.full(m_ref.shape, NEG, F32)
        acc_ref[...] = jnp.zeros(acc_ref.shape, F32)
        chains = [(h, b) for h in range(2) for b in range(T // LANES)]

        def tiles(js, masked_at):
            starts = [pl.multiple_of(j * T, T) for j in js]
            zss = []
            for start in starts:
                kh = _head_k(kind, k_ref, start, T)
                zss.append(_split_blocks([_dot_nt(kh[h], qh[h]) for h in range(2)]))
            pss, alss = [], []
            for start, zs, masked in zip(starts, zss, _mask_flags(js, masked_at)):
                ps, alphas = [], []
                for (h, b), z in zip(chains, zs):
                    lanes = slice(b * LANES, (b + 1) * LANES)
                    if kind == "mla":
                        z = z * scale
                    if fox:
                        z = z - cc_ref[h, pl.ds(start, T), :]
                    if masked:
                        z = jnp.where(key <= qrow + b * LANES, z, NEG)
                    m_prev = m_ref[h, :, lanes]
                    m_new = jnp.maximum(m_prev, jnp.max(z, axis=0, keepdims=True))
                    alphas.append(jnp.exp(m_prev - m_new))
                    ps.append(jnp.exp(z - m_new).astype(BF16))
                    m_ref[h, :, lanes] = m_new
                pss.append(_join_blocks(ps, T // LANES))
                alss.append(_join_blocks(alphas, T // LANES))
            for start, ps, alphas in zip(starts, pss, alss):
                vt = v_ref[pl.ds(start, T), :].T
                vth = [jnp.where(top, vt, 1.0).astype(BF16), jnp.where(top, 1.0, vt).astype(BF16)]
                for h in range(2):
                    acc_ref[h] = alphas[h] * acc_ref[h] + _dot(vth[h], ps[h])

        _loop_tiles(i, tiles, False, 2 * ATT_GROUP)
        acc = [acc_ref[0], acc_ref[1]]
        ot = jnp.concatenate([acc[0][0:64] / acc[0][64:128], acc[1][64:128] / acc[1][0:64]], axis=0)
        o = ot.T
        o_ref[...] = o
        gt = g_ref[...]
        og = o * (gt * _sigmoid(gt))
        og_ref[...] = og.astype(BF16)
        ogt_ref[...] = og.T.astype(BF16)
        st_ref[0] = m_ref[0] + jnp.log(acc[0][64:65])
        st_ref[1] = m_ref[1] + jnp.log(acc[1][0:1])

    QT = ATT_QSUB * T
    qs, ks, vs, gs = _att_specs(kind, S, QT)
    in_specs = [qs, ks, vs, gs]
    args = list(qkvg)
    hows = ["rows", None, None, "rows"]
    if fox:
        in_specs += [pl.BlockSpec((2, S, LANES), lambda p, i: (p, 0, 0))]
        args += [c_col]
        hows += [None]
    hows += ["rows", "rows", "lanes", "stat", None, None]
    W = npairs * LANES
    return pl.pallas_call(
        _per_q_tile(body, hows), name=name, grid=(npairs, nq // ATT_QSUB), in_specs=in_specs,
        out_specs=[pl.BlockSpec((QT, LANES), lambda p, i: (i, p)), pl.BlockSpec((QT, LANES), lambda p, i: (i, p)),
                   pl.BlockSpec((LANES, QT), lambda p, i: (p, i)),
                   pl.BlockSpec((2, ATT_QSUB, 1, T), lambda p, i: (p, i, 0, 0))],
        out_shape=[jax.ShapeDtypeStruct((S, W), F32), jax.ShapeDtypeStruct((S, W), BF16),
                   jax.ShapeDtypeStruct((W, S), BF16),
                   jax.ShapeDtypeStruct((2 * npairs, nq, 1, T), F32)],
        scratch_shapes=[pltpu.VMEM((2, 1, T), F32), pltpu.VMEM((2, LANES, T), F32)],
        compiler_params=_cparams(("parallel", "parallel")),
    )(*args)


def _softplus_parts(z):
    sp = jnp.maximum(z, 0.0) + jnp.log(1.0 + jnp.exp(-jnp.abs(z)))
    return sp, z - sp


def _cumsum_dot(tri2, his, los):
    return _split_blocks([_dot(tri2, jnp.concatenate([hi, lo], axis=0)) for hi, lo in zip(his, los)])


def _split2(x):
    hi = x.astype(BF16)
    return hi, (x - hi.astype(F32)).astype(BF16)


def _split_blocks(per_head):
    return [x[:, b * LANES:(b + 1) * LANES] for x in per_head for b in range(x.shape[1] // LANES)]


def _join_blocks(per_block, nb):
    return [jnp.concatenate(per_block[h * nb:(h + 1) * nb], axis=1) for h in range(len(per_block) // nb)]


def _row_of(col):
    return jnp.broadcast_to(col, (col.shape[0], LANES)).T[0:1]


def _softmax_bwd_t(kind, q, k, v, do, do_off, o, lse, c_col, S, npairs, name):
    T = ATT_T
    nq = S // T
    nb = T // LANES
    fox = kind == "fox"
    mla = kind == "mla"
    scale = (96 if mla else 64) ** -0.5
    kw = 256 if mla else LANES

    def body(i, *refs):
        if fox:
            (q_ref, k_ref, v_ref, do_ref, o_ref, st_ref, cc_ref,
             dq_ref, dk_ref, dv_ref, dck_ref, dcq_ref, dqt_ref, rs_ref, dkx_ref) = refs
        else:
            q_ref, k_ref, v_ref, do_ref, o_ref, st_ref, dq_ref, dk_ref, dv_ref, dqt_ref = refs

        @pl.when(i == 0)
        def _():
            dv_ref[...] = jnp.zeros_like(dv_ref)
            if fox:
                dkx_ref[...] = jnp.zeros_like(dkx_ref)
            else:
                dk_ref[...] = jnp.zeros_like(dk_ref)

        m0 = lax.broadcasted_iota(jnp.int32, (1, LANES), 1) < 64
        top = lax.broadcasted_iota(jnp.int32, (LANES, 1), 0) < 64
        key = lax.broadcasted_iota(jnp.int32, (T, LANES), 0)
        qrow = lax.broadcasted_iota(jnp.int32, (T, LANES), 1)
        qh = _head_q(kind, q_ref, m0, scale)
        if fox:
            qv = q_ref[...] * scale
            qk = [jnp.where(m0, qv, 1.0).astype(BF16), jnp.where(m0, 1.0, qv).astype(BF16)]
        else:
            qk = qh
        dov = do_ref[...]
        prod = dov * o_ref[...]
        dd = [_row_of(jnp.sum(jnp.where(m0, prod, 0.0), axis=1, keepdims=True)),
              _row_of(jnp.sum(jnp.where(m0, 0.0, prod), axis=1, keepdims=True))]
        doh = [jnp.where(m0, dov, 0.0).astype(BF16), jnp.where(m0, 0.0, dov).astype(BF16)]
        lse = [st_ref[0], st_ref[1]]
        dqt_ref[...] = jnp.zeros_like(dqt_ref)
        if fox:
            rs_ref[...] = jnp.zeros_like(rs_ref)
        chains = [(h, b) for h in range(2) for b in range(nb)]

        def tiles(js, masked_at):
            starts = [pl.multiple_of(j * T, T) for j in js]
            zss, dpss = [], []
            for start in starts:
                vb = v_ref[pl.ds(start, T), :].astype(BF16)
                kh = _head_k(kind, k_ref, start, T)
                zss.append(_split_blocks([_dot_nt(kh[h], qh[h]) for h in range(2)]))
                dpss.append(_split_blocks([_dot_nt(vb, doh[h]) for h in range(2)]))
            pss, dsss = [], []
            for start, zs, dps, masked in zip(starts, zss, dpss, _mask_flags(js, masked_at)):
                ps, dss = [], []
                for (h, b), z, dp in zip(chains, zs, dps):
                    lanes = slice(b * LANES, (b + 1) * LANES)
                    if mla:
                        z = z * scale
                    if fox:
                        z = z - cc_ref[h, pl.ds(start, T), :]
                    if masked:
                        z = jnp.where(key <= qrow + b * LANES, z, NEG)
                    p = jnp.exp(z - lse[h][:, lanes])
                    ds = p * (dp - dd[h][:, lanes])
                    dsb = ds.astype(BF16)
                    if fox:
                        rs_ref[h, :, lanes] += jnp.sum(dsb.astype(F32), axis=0, keepdims=True)
                    ps.append(p.astype(BF16))
                    dss.append(dsb)
                pss.append(_join_blocks(ps, nb))
                dsss.append(_join_blocks(dss, nb))
            for start, ps, dss in zip(starts, pss, dsss):
                kt = k_ref[pl.ds(start, T), :].T.astype(BF16)
                dvc = None
                for h in range(2):
                    dkh = _dot(dss[h], qk[h])
                    dvh = _dot(ps[h], doh[h])
                    dvc = dvh if dvc is None else dvc + dvh
                    kth = kt[h * LANES:(h + 1) * LANES] if mla else kt
                    dqt_ref[h] += _dot(kth, dss[h])
                    if fox:
                        dkx_ref[h, pl.ds(start, T), :] += dkh
                    elif mla:
                        dk_ref[pl.ds(start, T), h * LANES:(h + 1) * LANES] += dkh * scale
                    else:
                        dk_ref[pl.ds(start, T), :] += dkh
                dv_ref[pl.ds(start, T), :] += dvc

        _loop_tiles(i, tiles, False)
        if mla:
            dq_ref[:, 0:LANES] = dqt_ref[0].T * scale
            dq_ref[:, LANES:2 * LANES] = dqt_ref[1].T * scale
        else:
            dq_ref[...] = jnp.where(top, dqt_ref[0], dqt_ref[1]).T * scale
        if fox:
            dcq_ref[0] = rs_ref[0]
            dcq_ref[1] = rs_ref[1]

            @pl.when(i == nq - 1)
            def _():
                dk_ref[...] = jnp.where(m0, dkx_ref[0], dkx_ref[1])
                dck_ref[0] = dkx_ref[0].T[64:65]
                dck_ref[1] = dkx_ref[1].T[0:1]

    QT = ATT_QSUB * T
    qs, ks, vs, _ = _att_specs(kind, S, QT)
    stat = pl.BlockSpec((2, ATT_QSUB, 1, T), lambda p, i: (p, i, 0, 0))
    in_specs = [qs, ks, vs,
                pl.BlockSpec((QT, LANES), lambda p, i: (i, do_off + p)),
                pl.BlockSpec((QT, LANES), lambda p, i: (i, p)), stat]
    args = [q, k, v, do, o, lse]
    hows = ["rows", None, None, "rows", "rows", "stat"]
    W = npairs * LANES
    out_specs = [pl.BlockSpec((QT, kw), lambda p, i: (i, p)), pl.BlockSpec((S, kw), lambda p, i: (0, p)),
                 pl.BlockSpec((S, LANES), lambda p, i: (0, p))]
    out_shape = [jax.ShapeDtypeStruct((S, npairs * kw), F32), jax.ShapeDtypeStruct((S, npairs * kw), F32),
                 jax.ShapeDtypeStruct((S, W), F32)]
    scratch = [pltpu.VMEM((2, LANES, T), F32)]
    if fox:
        in_specs.append(pl.BlockSpec((2, S, LANES), lambda p, i: (p, 0, 0)))
        args.append(c_col)
        out_specs += [pl.BlockSpec((2, 1, S), lambda p, i: (p, 0, 0)), stat]
        out_shape += [jax.ShapeDtypeStruct((2 * npairs, 1, S), F32), jax.ShapeDtypeStruct((2 * npairs, nq, 1, T), F32)]
        scratch += [pltpu.VMEM((2, 1, T), F32), pltpu.VMEM((2, S, LANES), F32)]
        hows += [None, "rows", None, None, None, "stat", None, None, None]
    else:
        hows += ["rows", None, None, None]
    return pl.pallas_call(
        _per_q_tile(body, hows), name=name, grid=(npairs, nq // ATT_QSUB), in_specs=in_specs, out_specs=out_specs,
        out_shape=out_shape, scratch_shapes=scratch, compiler_params=_cparams(("parallel", "arbitrary")),
    )(*args)


def _sb_fwd_t(proj, S, npairs, name, ride=None):
    T = ATT_T
    nq = S // T
    nb = T // LANES
    scale = 64 ** -0.5

    def body(i, q_ref, k_ref, v_ref, g_ref, o_ref, og_ref, ogt_ref, st_ref, rem_ref, acc_ref):
        m0 = lax.broadcasted_iota(jnp.int32, (1, LANES), 1) < 64
        top = lax.broadcasted_iota(jnp.int32, (LANES, 1), 0) < 64
        key = lax.broadcasted_iota(jnp.int32, (T, LANES), 0)
        qrow = lax.broadcasted_iota(jnp.int32, (T, LANES), 1)
        r = lax.broadcasted_iota(jnp.int32, (T, T), 0)
        c = lax.broadcasted_iota(jnp.int32, (T, T), 1)
        after = (c > r).astype(BF16)
        after2 = jnp.concatenate([after, after], axis=1)
        qh = _head_q("sb", q_ref, m0, scale)
        rem_ref[...] = jnp.zeros_like(rem_ref)
        acc_ref[...] = jnp.zeros_like(acc_ref)
        chains = [(h, b) for h in range(2) for b in range(nb)]

        def tiles(js, masked_at):
            zss = []
            for j in js:
                kb = k_ref[pl.ds(pl.multiple_of(j * T, T), T), :].astype(BF16)
                zss.append(_split_blocks([_dot_nt(kb, qh[h]) for h in range(2)]))
            lass, sums, hiss, loss = [], [], [], []
            for zs, masked in zip(zss, _mask_flags(js, masked_at)):
                las, sm, his, los = [], [], [], []
                for (h, b), z in zip(chains, zs):
                    sp, la = _softplus_parts(z)
                    if masked:
                        sp = jnp.where(key < qrow + b * LANES, sp, 0.0)
                    hi, lo = _split2(sp)
                    las.append(la)
                    sm.append(jnp.sum(sp, axis=0, keepdims=True))
                    his.append(hi)
                    los.append(lo)
                lass.append(las)
                sums.append(sm)
                hiss.append(_join_blocks(his, nb))
                loss.append(_join_blocks(los, nb))
            rcss = [_cumsum_dot(after2, his, los) for his, los in zip(hiss, loss)]
            wss = []
            for las, sm, rcs, masked in zip(lass, sums, rcss, _mask_flags(js, masked_at)):
                ws = []
                for (h, b), la, s, rc in zip(chains, las, sm, rcs):
                    lanes = slice(b * LANES, (b + 1) * LANES)
                    w = jnp.exp(la - (rem_ref[h, :, lanes] + rc))
                    if masked:
                        w = jnp.where(key < qrow + b * LANES, w, 0.0)
                    ws.append(w.astype(BF16))
                    rem_ref[h, :, lanes] += s
                wss.append(_join_blocks(ws, nb))
            for j, ws in zip(js, wss):
                vtb = v_ref[pl.ds(pl.multiple_of(j * T, T), T), :].T.astype(BF16)
                for h in range(2):
                    acc_ref[h] += _dot(vtb, ws[h])

        _loop_tiles(i, tiles, True)
        o = jnp.where(top, acc_ref[0], acc_ref[1]).T
        o_ref[...] = o
        gt = g_ref[...]
        og = o * (gt * _sigmoid(gt))
        og_ref[...] = og.astype(BF16)
        ogt_ref[...] = og.T.astype(BF16)
        st_ref[0] = rem_ref[0]
        st_ref[1] = rem_ref[1]

    QT = ATT_QSUB * T
    qs, ks, vs, gs = _att_specs("sb", S, QT)
    W = npairs * LANES
    hows = ["rows", None, None, "rows", "rows", "rows", "lanes", "stat", None, None]
    attend = _per_q_tile(body, hows)
    steps = nq // ATT_QSUB
    in_specs = [qs, ks, vs, gs]
    out_specs = [pl.BlockSpec((QT, LANES), lambda p, i: (i, p)), pl.BlockSpec((QT, LANES), lambda p, i: (i, p)),
                 pl.BlockSpec((LANES, QT), lambda p, i: (p, i)),
                 pl.BlockSpec((2, ATT_QSUB, 1, T), lambda p, i: (p, i, 0, 0))]
    out_shape = [jax.ShapeDtypeStruct((S, W), F32), jax.ShapeDtypeStruct((S, W), BF16),
                 jax.ShapeDtypeStruct((W, S), BF16),
                 jax.ShapeDtypeStruct((2 * npairs, nq, 1, T), F32)]
    scratch = [pltpu.VMEM((2, 1, T), F32), pltpu.VMEM((2, LANES, T), F32)]
    args = [proj, proj, proj, proj]
    if ride is None:
        kernel_body, sem = attend, ("parallel", "parallel")
    else:
        def kernel_body(*refs):
            begin, finish = _gather_phases([refs[4]], [refs[9]], refs[12], refs[13])
            pl.when((pl.program_id(0) == 0) & (pl.program_id(1) == 0))(begin)
            attend(*refs[0:4], *refs[5:9], *refs[10:12])
            pl.when((pl.program_id(0) == npairs - 1) & (pl.program_id(1) == steps - 1))(finish)

        sem = ("arbitrary", "arbitrary")
        in_specs.append(_ANY)
        out_specs.append(_ANY)
        out_shape.append(jax.ShapeDtypeStruct((4,) + ride.shape, ride.dtype))
        scratch += [pltpu.SemaphoreType.DMA((6,)), pltpu.SemaphoreType.DMA((6,))]
        args.append(ride)
    return pl.pallas_call(
        kernel_body, name=name, grid=(npairs, steps), in_specs=in_specs, out_specs=out_specs, out_shape=out_shape,
        scratch_shapes=scratch, compiler_params=_cparams(sem),
    )(*args)


def _sb_bwd_t(proj, do, tot, S, npairs, name):
    T = ATT_T
    nq = S // T
    nb = T // LANES
    scale = 64 ** -0.5

    def body(i, q_ref, k_ref, v_ref, do_ref, st_ref, dq_ref, dk_ref, dv_ref, dqt_ref, pre_ref, gpre_ref):

        @pl.when(i == 0)
        def _():
            dk_ref[...] = jnp.zeros_like(dk_ref)
            dv_ref[...] = jnp.zeros_like(dv_ref)

        m0 = lax.broadcasted_iota(jnp.int32, (1, LANES), 1) < 64
        top = lax.broadcasted_iota(jnp.int32, (LANES, 1), 0) < 64
        key = lax.broadcasted_iota(jnp.int32, (T, LANES), 0)
        qrow = lax.broadcasted_iota(jnp.int32, (T, LANES), 1)
        r = lax.broadcasted_iota(jnp.int32, (T, T), 0)
        c = lax.broadcasted_iota(jnp.int32, (T, T), 1)
        upto = (c <= r).astype(BF16)
        upto2 = jnp.concatenate([upto, upto], axis=1)
        left = (c < r).astype(BF16)
        qh = _head_q("sb", q_ref, m0, scale)
        dov = do_ref[...]
        doh = [jnp.where(m0, dov, 0.0).astype(BF16), jnp.where(m0, 0.0, dov).astype(BF16)]
        tot_h = [st_ref[0], st_ref[1]]
        dqt_ref[...] = jnp.zeros_like(dqt_ref)
        pre_ref[...] = jnp.zeros_like(pre_ref)
        gpre_ref[...] = jnp.zeros_like(gpre_ref)
        chains = [(h, b) for h in range(2) for b in range(nb)]

        def tiles(js, masked_at):
            starts = [pl.multiple_of(j * T, T) for j in js]
            zss, dwss = [], []
            for start in starts:
                vb = v_ref[pl.ds(start, T), :].astype(BF16)
                kb = k_ref[pl.ds(start, T), :].astype(BF16)
                zss.append(_split_blocks([_dot_nt(kb, qh[h]) for h in range(2)]))
                dwss.append(_split_blocks([_dot_nt(vb, doh[h]) for h in range(2)]))
            lass, sums, hiss, loss = [], [], [], []
            for zs, masked in zip(zss, _mask_flags(js, masked_at)):
                las, sm, his, los = [], [], [], []
                for (h, b), z in zip(chains, zs):
                    sp, la = _softplus_parts(z)
                    if masked:
                        sp = jnp.where(key < qrow + b * LANES, sp, 0.0)
                    hi, lo = _split2(sp)
                    las.append(la)
                    sm.append(jnp.sum(sp, axis=0, keepdims=True))
                    his.append(hi)
                    los.append(lo)
                lass.append(las)
                sums.append(sm)
                hiss.append(_join_blocks(his, nb))
                loss.append(_join_blocks(los, nb))
            pcss = [_cumsum_dot(upto2, his, los) for his, los in zip(hiss, loss)]
            wss, gss = [], []
            for las, sm, pcs, dws, masked in zip(lass, sums, pcss, dwss, _mask_flags(js, masked_at)):
                ws, gs = [], []
                for (h, b), la, s, pc, dw in zip(chains, las, sm, pcs, dws):
                    lanes = slice(b * LANES, (b + 1) * LANES)
                    w = jnp.exp(la - ((tot_h[h][:, lanes] - pre_ref[h, :, lanes]) - pc))
                    if masked:
                        w = jnp.where(key < qrow + b * LANES, w, 0.0)
                    ws.append(w.astype(BF16))
                    gs.append(dw * w)
                    pre_ref[h, :, lanes] += s
                wss.append(_join_blocks(ws, nb))
                gss.append(gs)
            gcss = [_split_blocks([_dot(left, g) for g in _join_blocks([g.astype(BF16) for g in gs], nb)]) for gs in gss]
            dzss = []
            for las, gs, gcs, masked in zip(lass, gss, gcss, _mask_flags(js, masked_at)):
                dzs = []
                for (h, b), la, g, gc in zip(chains, las, gs, gcs):
                    lanes = slice(b * LANES, (b + 1) * LANES)
                    dz = g - (g + (gpre_ref[h, :, lanes] + gc)) * jnp.exp(la)
                    if masked:
                        dz = jnp.where(key < qrow + b * LANES, dz, 0.0)
                    dzs.append(dz.astype(BF16))
                    gpre_ref[h, :, lanes] += jnp.sum(g, axis=0, keepdims=True)
                dzss.append(_join_blocks(dzs, nb))
            for start, ws, dzs in zip(starts, wss, dzss):
                kt = k_ref[pl.ds(start, T), :].T.astype(BF16)
                dkc = dvc = None
                for h in range(2):
                    dkh = _dot(dzs[h], qh[h])
                    dvh = _dot(ws[h], doh[h])
                    dkc = dkh if dkc is None else dkc + dkh
                    dvc = dvh if dvc is None else dvc + dvh
                    dqt_ref[h] += _dot(kt, dzs[h])
                dk_ref[pl.ds(start, T), :] += dkc
                dv_ref[pl.ds(start, T), :] += dvc

        _loop_tiles(i, tiles, False)
        dq_ref[...] = jnp.where(top, dqt_ref[0], dqt_ref[1]).T * scale

    QT = ATT_QSUB * T
    qs, ks, vs, _ = _att_specs("sb", S, QT)
    W = npairs * LANES
    hows = ["rows", None, None, "rows", "stat", "rows", None, None, None, None, None]
    return pl.pallas_call(
        _per_q_tile(body, hows), name=name, grid=(npairs, nq // ATT_QSUB),
        in_specs=[qs, ks, vs,
                  pl.BlockSpec((QT, LANES), lambda p, i: (i, p)),
                  pl.BlockSpec((2, ATT_QSUB, 1, T), lambda p, i: (p, i, 0, 0))],
        out_specs=[pl.BlockSpec((QT, LANES), lambda p, i: (i, p)), pl.BlockSpec((S, LANES), lambda p, i: (0, p)),
                   pl.BlockSpec((S, LANES), lambda p, i: (0, p))],
        out_shape=[jax.ShapeDtypeStruct((S, W), F32)] * 3,
        scratch_shapes=[pltpu.VMEM((2, LANES, T), F32), pltpu.VMEM((2, 1, T), F32), pltpu.VMEM((2, 1, T), F32)],
        compiler_params=_cparams(("parallel", "arbitrary")),
    )(proj, proj, proj, do, tot)


def _pad_w0(w):
    z = lambda n: jnp.zeros((w.shape[0], n), w.dtype)
    return jnp.concatenate([w[:, 2048:2432], w[:, 2432:2688], z(64), w[:, 2688:2720], z(32),
                            w[:, 1536:2048], w[:, 2720:3232], w[:, 0:512], w[:, 512:1024], w[:, 1024:1536]], axis=1)


def _unpad_w0(wp):
    return jnp.concatenate([wp[:, L0_SBQ:L0_SBQ + 512], wp[:, L0_SBK:L0_SBK + 512], wp[:, L0_SBV:L0_SBV + 512],
                            wp[:, L0_SBG:L0_SBG + 512], wp[:, 0:384], wp[:, 384:640], wp[:, 704:736],
                            wp[:, L0_MLG:L0_MLG + 512]], axis=1)


def _pad_wq(w):
    return jnp.pad(w.reshape(384, 8, 96), ((0, 0), (0, 0), (0, 32))).reshape(384, 1024)


def _unpad_wq(wp):
    return wp.reshape(384, 8, 128)[:, :, :96].reshape(384, 768)


def _pad_wkv(w):
    w3 = w.reshape(256, 8, 128)
    k = jnp.pad(w3[:, :, :64], ((0, 0), (0, 0), (0, 64))).reshape(256, 1024)
    return jnp.concatenate([k, w3[:, :, 64:].reshape(256, 512)], axis=1)


def _unpad_wkv(wp):
    k = wp[:, :1024].reshape(256, 8, 128)[:, :, :64]
    v = wp[:, 1024:].reshape(256, 8, 64)
    return jnp.concatenate([k, v], axis=-1).reshape(256, 1024)


def _pad_w1(w):
    return jnp.concatenate([w, jnp.zeros((w.shape[0], L1_WIDTH - ODD_IN_WIDTH), w.dtype)], axis=1)


def _local_step(x, positions, target, g, w0p, wqp, wkvp, wo0, w1p, wo1, send_early=None, late=None):
    S = x.shape[0]
    nq = S // ATT_T
    invf = ROPE_THETA ** (-jnp.arange(0, MLA_ROPE_DIM, 2, dtype=F32) / MLA_ROPE_DIM)
    invf = jnp.concatenate([jnp.zeros((64,), F32), invf, invf, jnp.zeros((32,), F32)]).reshape(1, LANES)
    cosT, s1T, s2T = _rope_tables(positions.reshape(S, 1), invf, "rope_tables")
    bfp = jnp.pad(g["l1_b_f"], ((0, 0), (0, LANES - FOX_HEADS)))

    if late is None:
        proj0, h0t = _norm_matmul(x, g["l0_pre_g"], w0p, "l0_in_proj")
    else:
        pack_shard, w1_shard, open_pack, open_w1 = late
        proj0, h0t, pack_all = _norm_matmul(x, g["l0_pre_g"], w0p, "l0_in_proj", ride=pack_shard)
        wqp, wkvp, wo0, wo1 = open_pack(pack_all)
    qm, km, vm, qnt, cnt = _mla_prep(proj0, g["l0_q_a_g"], g["l0_kv_a_g"], wqp, wkvp, cosT, s1T, s2T, "mla_prep")
    if late is None:
        o_sb, og_sb, ogt_sb, tot_sb = _sb_fwd_t(proj0, S, 4, "sb_fwd")
    else:
        o_sb, og_sb, ogt_sb, tot_sb, w1_all = _sb_fwd_t(proj0, S, 4, "sb_fwd", ride=w1_shard)
        w1p = open_w1(w1_all)
    o_ml, og_ml, ogt_ml, lse_ml = _softmax_fwd("mla", (qm, km, vm, proj0), None, S, 4, "mla_fwd")
    y0, x1 = _out_proj(og_sb, og_ml, 0, 0, wo0, x, g["l0_post_g"], None, "l0_out_proj")

    proj1, h1t = _norm_matmul(x1, g["l1_pre_g"], w1p, "l1_in_proj")
    cfx = _fox_prep(proj1, bfp, "fox_prep")
    c16 = cfx[:, :FOX_HEADS].T
    c_col = jnp.broadcast_to(c16[:, :, None], (FOX_HEADS, S, LANES))
    o_fx, og_fx, ogt_fx, lse_fx = _softmax_fwd("fox", (proj1, proj1, proj1, proj1), c_col, S, 8, "fox_fwd")
    y1, dx2, lsum = _out_proj(og_fx, og_fx, 0, 1, wo1, x1, g["l1_post_g"], target, "l1_out_proj")

    dy1, do1, dgate1, d_post1 = _out_proj_bwd(dx2, y1, g["l1_post_g"], wo1, proj1, (L1_G, L1_G + 512), o_fx, o_fx, 0, 1, "l1_out_bwd")
    dwo1 = _matmul_t(ogt_fx, dy1, "l1_dw_out")
    dq1, dk1, dv1, dck, dcq = _softmax_bwd_t("fox", proj1, proj1, proj1, do1, 0, o_fx, lse_fx, c_col, S, 8,
                                             "fox_bwd")
    dc = jnp.pad((dcq.reshape(FOX_HEADS, S) - dck.reshape(FOX_HEADS, S)).T, ((0, 0), (0, LANES - FOX_HEADS)))
    df, d_bf = _fox_prep_bwd(dc, proj1, bfp, "fox_prep_bwd")
    pieces1 = [(L1_Q, dq1), (L1_K, dk1), (L1_V, dv1), (L1_G, dgate1), (L1_F, df)]
    dx1, d_pre1 = _in_proj_bwd(pieces1, w1p, x1, g["l1_pre_g"], dx2, "l1_in_bwd")
    dw1p = jnp.concatenate(_matmul_t_many(h1t, [dq1, dk1], "l1_dw_in_a")
                           + _matmul_t_many(h1t, [dv1, dgate1, df], "l1_dw_in_b"), axis=1)
    early = None if send_early is None else send_early(dw1p)

    dy0, do0, dgate0, d_post0 = _out_proj_bwd(dx1, y0, g["l0_post_g"], wo0, proj0, (L0_SBG, L0_MLG), o_sb, o_ml, 0, 0,
                                              "l0_out_bwd")
    dwo0 = jnp.concatenate([_matmul_t(ogt_sb, dy0, "l0_dw_out_sb"), _matmul_t(ogt_ml, dy0, "l0_dw_out_mla")], axis=0)
    dsq, dsk, dsv = _sb_bwd_t(proj0, do0, tot_sb, S, 4, "sb_bwd")
    dqm, dkm, dvm = _softmax_bwd_t("mla", qm, km, vm, do0, 4, o_ml, lse_ml, None, S, 4, "mla_bwd")
    dprep, dqb, dkvb, d_qag, d_kvag = _mla_prep_bwd(dqm, dkm, dvm, proj0, g["l0_q_a_g"], g["l0_kv_a_g"], wqp, wkvp,
                                                    cosT, s1T, s2T, "mla_prep_bwd")
    dwqp = _matmul_t(qnt, dqb, "l0_dw_qb")
    dwkvp = _matmul_t(cnt, dkvb, "l0_dw_kvb")
    pieces0 = [(L0_PREP, dprep), (L0_SBG, dgate0), (L0_SBQ, dsq), (L0_SBK, dsk), (L0_SBV, dsv)]
    if send_early is None:
        dx0, d_pre0 = _in_proj_bwd(pieces0, w0p, x, g["l0_pre_g"], dx1, "l0_in_bwd")
        early_slots = None
    else:
        dx0, d_pre0, early_slots = _in_proj_bwd(pieces0, w0p, x, g["l0_pre_g"], dx1, "l0_in_bwd", ride=early)
    dw0p = jnp.concatenate(_matmul_t_many(h0t, [dprep, dgate0], "l0_dw_in_a")
                           + _matmul_t_many(h0t, [dsq, dsk, dsv], "l0_dw_in_b"), axis=1)

    grads = {
        "l0_pre_g": d_pre0, "l0_post_g": d_post0, "l0_w_in": dw0p, "l0_q_a_g": d_qag, "l0_w_q_b": dwqp,
        "l0_kv_a_g": d_kvag, "l0_w_kv_b": dwkvp, "l0_w_out": dwo0, "l1_pre_g": d_pre1, "l1_post_g": d_post1,
        "l1_w_in": dw1p, "l1_b_f": d_bf[:, :FOX_HEADS], "l1_w_out": dwo1,
    }
    grads["early_q"], grads["early_slots"] = early, early_slots
    return lsum, dx0, grads


_ANY = pl.BlockSpec(memory_space=pl.ANY)


def _place():
    return lax.axis_index("x"), lax.axis_index("y"), lax.axis_index("c")


def _other_chips(x, y):
    return [(1 - x, y), (x, 1 - y), (1 - x, 1 - y)]


def _half(rows, c):
    return pl.ds(c * (rows // 2), rows // 2)


def _gather_phases(p_refs, out_refs, send_sems, recv_sems):
    n = len(p_refs)
    x, y, c = _place()
    sibling = (x, y, 1 - c)
    chips = _other_chips(x, y)

    def blk(k, chip, cc):
        return out_refs[k].at[2 * chip[0] + chip[1], _half(p_refs[k].shape[0], cc)]

    def copy(s, src, dst, to):
        return pltpu.make_async_remote_copy(src_ref=src, dst_ref=dst, send_sem=send_sems.at[s],
                                            recv_sem=recv_sems.at[s], device_id=to, device_id_type=MESH)

    def first():
        return [copy(6 * k + j, p_refs[k].at[_half(p_refs[k].shape[0], c)], blk(k, (x, y), c), (*chip, c))
                for j, chip in enumerate(chips) for k in range(n)]

    def begin():
        for cp in first():
            cp.start()

    def finish():
        passed = []
        for j, chip in enumerate(chips):
            for k in range(n):
                copy(6 * k + j, blk(k, chip, c), blk(k, chip, c), (x, y, c)).wait_recv()
                passed.append(copy(6 * k + 3 + j, blk(k, chip, c), blk(k, chip, c), sibling))
                passed[-1].start()
        for j, chip in enumerate(chips):
            for k in range(n):
                copy(6 * k + 3 + j, blk(k, chip, 1 - c), blk(k, chip, 1 - c), (x, y, c)).wait_recv()
        for cp in first() + passed:
            cp.wait_send()

    return begin, finish


def _weight_gather(parts):
    n = len(parts)

    def body(*refs):
        begin, finish = _gather_phases(refs[:n], refs[n:2 * n], refs[2 * n], refs[2 * n + 1])
        begin()
        finish()

    return pl.pallas_call(
        body, name="weight_gather", in_specs=[_ANY] * n, out_specs=[_ANY] * n,
        out_shape=[jax.ShapeDtypeStruct((4,) + a.shape, a.dtype) for a in parts],
        scratch_shapes=[pltpu.SemaphoreType.DMA((6 * n,)), pltpu.SemaphoreType.DMA((6 * n,))],
    )(*parts)


def _grad_core_exchange(ps, name="grad_core_exchange"):
    n = len(ps)

    def body(*refs):
        p_refs, recv_refs, send_sems, recv_sems = refs[:n], refs[n:2 * n], refs[2 * n], refs[2 * n + 1]
        x, y, c = _place()
        give = [pltpu.make_async_remote_copy(src_ref=p_refs[k].at[j, _half(p_refs[k].shape[1], 1 - c)],
                                             dst_ref=recv_refs[k].at[j], send_sem=send_sems.at[4 * k + j],
                                             recv_sem=recv_sems.at[4 * k + j], device_id=(x, y, 1 - c),
                                             device_id_type=MESH) for k in range(n) for j in range(4)]
        for cp in give:
            cp.start()
        for cp in give:
            cp.wait()

    return pl.pallas_call(
        body, name=name, in_specs=[_ANY] * n, out_specs=[_ANY] * n,
        out_shape=[jax.ShapeDtypeStruct((4, p.shape[1] // 2, p.shape[2]), p.dtype) for p in ps],
        scratch_shapes=[pltpu.SemaphoreType.DMA((4 * n,)), pltpu.SemaphoreType.DMA((4 * n,))],
    )(*ps)


def _grad_rows(rows):
    return _pick(rows, (1296, 512, rows))


def _grad_add_cores(p, theirs, c1, name):
    _, rh, cols = theirs.shape
    tr = _grad_rows(rh)

    def body(c_ref, a_ref, b_ref, o_ref):
        o_ref[...] = (a_ref[...] + b_ref[...]).astype(BF16)

    spec = pl.BlockSpec((None, tr, cols), lambda j, r, c: (j, r, 0))
    grid_spec = pltpu.PrefetchScalarGridSpec(
        num_scalar_prefetch=1, grid=(4, rh // tr),
        in_specs=[pl.BlockSpec((None, None, tr, cols), lambda j, r, c: (j, c[0], r, 0)), spec], out_specs=spec)
    return pl.pallas_call(
        body, name=name, grid_spec=grid_spec, out_shape=jax.ShapeDtypeStruct(theirs.shape, BF16),
        compiler_params=_cparams(("parallel", "parallel")),
    )(c1, p.reshape(4, 2, rh, cols), theirs)


def _exchange_phases(q_refs, out_refs, send_sems, recv_sems):
    n = len(q_refs)
    x, y, c = _place()
    me = 2 * x + y
    chips = _other_chips(x, y)

    def sends():
        return [pltpu.make_async_remote_copy(src_ref=q_refs[k].at[2 * chip[0] + chip[1]], dst_ref=out_refs[k].at[me],
                                             send_sem=send_sems.at[3 * k + j], recv_sem=recv_sems.at[3 * k + j],
                                             device_id=(*chip, c), device_id_type=MESH)
                for j, chip in enumerate(chips) for k in range(n)]

    def begin():
        for cp in sends():
            cp.start()

    def finish():
        for j, chip in enumerate(chips):
            for k in range(n):
                slot = out_refs[k].at[2 * chip[0] + chip[1]]
                pltpu.make_async_remote_copy(src_ref=slot, dst_ref=slot, send_sem=send_sems.at[3 * k + j],
                                             recv_sem=recv_sems.at[3 * k + j], device_id=(x, y, c),
                                             device_id_type=MESH).wait_recv()
        for cp in sends():
            cp.wait_send()

    return begin, finish


def _grad_chip_exchange(qs):
    n = len(qs)

    def body(*refs):
        begin, finish = _exchange_phases(refs[:n], refs[n:2 * n], refs[2 * n], refs[2 * n + 1])
        begin()
        finish()

    return pl.pallas_call(
        body, name="grad_chip_exchange", in_specs=[_ANY] * n, out_specs=[_ANY] * n,
        out_shape=[jax.ShapeDtypeStruct(q.shape, q.dtype) for q in qs],
        scratch_shapes=[pltpu.SemaphoreType.DMA((3 * n,)), pltpu.SemaphoreType.DMA((3 * n,))],
    )(*qs)


def _grad_add_chips(q, slots, me1, name):
    _, rh, cols = q.shape
    tr = _grad_rows(rh)

    def body(me_ref, own_ref, s0, s1, s2, s3, o_ref):
        me = me_ref[0]
        t = [jnp.where(me == j, own_ref[...], s[...]).astype(F32) for j, s in enumerate((s0, s1, s2, s3))]
        o_ref[...] = ((t[0] + t[1]) + t[2]) + t[3]

    def slot_spec(j):
        return pl.BlockSpec((None, tr, cols), lambda r, me: (jnp.where(me[0] == j, (j + 1) % 4, j), r, 0))

    grid_spec = pltpu.PrefetchScalarGridSpec(
        num_scalar_prefetch=1, grid=(rh // tr,),
        in_specs=[pl.BlockSpec((None, tr, cols), lambda r, me: (me[0], r, 0))] + [slot_spec(j) for j in range(4)],
        out_specs=pl.BlockSpec((tr, cols), lambda r, me: (r, 0)))
    return pl.pallas_call(
        body, name=name, grid_spec=grid_spec, out_shape=jax.ShapeDtypeStruct(q.shape[1:], F32),
        compiler_params=_cparams(("parallel",)),
    )(me1, q, slots, slots, slots, slots)


def _grad_core_gather(ts):
    n = len(ts)

    def body(*refs):
        t_refs, out_refs, send_sems, recv_sems = refs[:n], refs[n:2 * n], refs[2 * n], refs[2 * n + 1]
        x, y, c = _place()
        give = [pltpu.make_async_remote_copy(src_ref=t_refs[k], dst_ref=out_refs[k], send_sem=send_sems.at[k],
                                             recv_sem=recv_sems.at[k], device_id=(x, y, 1 - c), device_id_type=MESH)
                for k in range(n)]
        for cp in give:
            cp.start()
        for cp in give:
            cp.wait()

    return pl.pallas_call(
        body, name="grad_core_gather", in_specs=[_ANY] * n, out_specs=[_ANY] * n,
        out_shape=[jax.ShapeDtypeStruct(t.shape, t.dtype) for t in ts],
        scratch_shapes=[pltpu.SemaphoreType.DMA((n,)), pltpu.SemaphoreType.DMA((n,))],
    )(*ts)


def _small_allreduce(sp):
    def body(sp_ref, out_ref, gath_ref, send_sems, recv_sems):
        x, y, c = _place()
        me = 4 * x + 2 * y + c
        gath_ref[me] = sp_ref[...]
        peers = []
        for k in range(1, 8):
            px = 1 - x if k & 4 else x
            py = 1 - y if k & 2 else y
            pc = 1 - c if k & 1 else c
            peers.append((px, py, pc))
        sends = [pltpu.make_async_remote_copy(src_ref=sp_ref, dst_ref=gath_ref.at[me], send_sem=send_sems.at[k],
                                              recv_sem=recv_sems.at[k], device_id=peer, device_id_type=MESH)
                 for k, peer in enumerate(peers)]
        for cp in sends:
            cp.start()
        for k, (px, py, pc) in enumerate(peers):
            slot = gath_ref.at[4 * px + 2 * py + pc]
            pltpu.make_async_remote_copy(src_ref=slot, dst_ref=slot, send_sem=send_sems.at[k], recv_sem=recv_sems.at[k],
                                         device_id=(x, y, c), device_id_type=MESH).wait_recv()
        for cp in sends:
            cp.wait_send()
        tot = gath_ref[0]
        for d in range(1, 8):
            tot = tot + gath_ref[d]
        out_ref[...] = tot

    vm = pl.BlockSpec(memory_space=pltpu.VMEM)
    return pl.pallas_call(
        body, name="small_allreduce", in_specs=[vm], out_specs=vm, out_shape=jax.ShapeDtypeStruct(sp.shape, sp.dtype),
        scratch_shapes=[pltpu.VMEM((8,) + sp.shape, sp.dtype), pltpu.SemaphoreType.DMA((7,)), pltpu.SemaphoreType.DMA((7,))],
    )(sp)


def _adamw_update(w, gv, m, v):
    mn = ADAM_B1 * m + (1.0 - ADAM_B1) * gv
    vn = ADAM_B2 * v + (1.0 - ADAM_B2) * (gv * gv)
    m_hat = mn / (1.0 - ADAM_B1 ** ADAM_STEP)
    v_hat = vn / (1.0 - ADAM_B2 ** ADAM_STEP)
    return -ADAM_LR * (m_hat / (jnp.sqrt(v_hat) + ADAM_EPS) + ADAM_WD * w), mn, vn


def _adamw(w, g, m, v, name):
    rows, cols = w.shape

    def body(w_ref, g_ref, m_ref, v_ref, d_ref, mo_ref, vo_ref):
        d_ref[...], mo_ref[...], vo_ref[...] = _adamw_update(w_ref[...], g_ref[...], m_ref[...], v_ref[...])

    if rows % 256 == 0 or cols % 256 != 0:
        tr = _pick(rows, (256, rows))
        grid, spec = (rows // tr,), pl.BlockSpec((tr, cols), lambda r: (r, 0))
    else:
        grid, spec = (cols // 256,), pl.BlockSpec((rows, 256), lambda r: (0, r))
    shp = jax.ShapeDtypeStruct(w.shape, F32)
    return pl.pallas_call(
        body, name=name, grid=grid, in_specs=[spec] * 4, out_specs=[spec] * 3, out_shape=[shp] * 3,
        compiler_params=_cparams(("parallel",)),
    )(w, g, m, v)


MAT_NAMES = ("l0_w_in", "l0_w_q_b", "l0_w_kv_b", "l0_w_out", "l1_w_in", "l1_w_out")
VEC_NAMES = ("l0_pre_g", "l0_post_g", "l0_q_a_g", "l0_kv_a_g", "l1_pre_g", "l1_post_g", "l1_b_f")
WEIGHT_NAMES = ("l0_pre_g", "l0_post_g", "l0_w_in", "l0_q_a_g", "l0_w_q_b", "l0_kv_a_g", "l0_w_kv_b", "l0_w_out",
                "l1_pre_g", "l1_post_g", "l1_w_in", "l1_b_f", "l1_w_out")
MAT_SHARD = {"l0_w_in": (1024, 808), "l0_w_q_b": (384, 192), "l0_w_kv_b": (256, 256), "l0_w_out": (256, 1024),
             "l1_w_in": (1024, 1028), "l1_w_out": (256, 1024)}
ROW_SHARDED = ("l0_w_out", "l1_w_out")
WHOLE_MATS = ("l0_w_in", "l1_w_in")
PACKED_MATS = ("l0_w_q_b", "l0_w_kv_b", "l0_w_out", "l1_w_out")
VEC_LEN = {"l0_pre_g": 1024, "l0_post_g": 1024, "l0_q_a_g": 384, "l0_kv_a_g": 256, "l1_pre_g": 1024,
           "l1_post_g": 1024, "l1_b_f": 16}


def _mat_rows(n):
    r, c = MAT_SHARD[n]
    return r * c // LANES


def _pack_shards(shards):
    return jnp.concatenate([shards[n].reshape(shards[n].shape[:-2] + (_mat_rows(n), LANES)) for n in PACKED_MATS],
                           axis=-2)


def _unpack_shards(pack):
    out, at = {}, 0
    for n in PACKED_MATS:
        out[n] = pack[..., at:at + _mat_rows(n), :].reshape(pack.shape[:-2] + MAT_SHARD[n])
        at += _mat_rows(n)
    return out


def _join_shards(n, s):
    if n in ROW_SHARDED:
        return s.reshape(4 * s.shape[1], s.shape[2])
    return s.transpose(1, 0, 2).reshape(s.shape[1], 4 * s.shape[2])


def _cut_shards(n, w):
    r, c = MAT_SHARD[n]
    if n in ROW_SHARDED:
        return w.reshape(4, r, c)
    return w.reshape(r, 4, c).transpose(1, 0, 2)


def _pack_vecs(vecs):
    parts = []
    for n in VEC_NAMES:
        v = vecs[n].reshape(-1)
        parts.append(jnp.pad(v, (0, VEC_ROWS * LANES - v.shape[0])).reshape(VEC_ROWS, LANES))
    return jnp.concatenate(parts, axis=0)


def _unpack_vecs(pack):
    return {n: pack[k * VEC_ROWS:(k + 1) * VEC_ROWS].reshape(-1)[:VEC_LEN[n]] for k, n in enumerate(VEC_NAMES)}


def kernel(x, positions, l0_pre_g, l0_post_g, l0_w_in, l0_q_a_g, l0_w_q_b, l0_kv_a_g, l0_w_kv_b, l0_w_out, l1_pre_g, l1_post_g, l1_w_in, l1_b_f, l1_w_out, loss_target, m_l0_pre_g, m_l0_post_g, m_l0_w_in, m_l0_q_a_g, m_l0_w_q_b, m_l0_kv_a_g, m_l0_w_kv_b, m_l0_w_out, m_l1_pre_g, m_l1_post_g, m_l1_w_in, m_l1_b_f, m_l1_w_out, v_l0_pre_g, v_l0_post_g, v_l0_w_in, v_l0_q_a_g, v_l0_w_q_b, v_l0_kv_a_g, v_l0_w_kv_b, v_l0_w_out, v_l1_pre_g, v_l1_post_g, v_l1_w_in, v_l1_b_f, v_l1_w_out):
    w = dict(l0_pre_g=l0_pre_g, l0_post_g=l0_post_g, l0_w_in=l0_w_in, l0_q_a_g=l0_q_a_g, l0_w_q_b=l0_w_q_b,
             l0_kv_a_g=l0_kv_a_g, l0_w_kv_b=l0_w_kv_b, l0_w_out=l0_w_out, l1_pre_g=l1_pre_g, l1_post_g=l1_post_g,
             l1_w_in=l1_w_in, l1_b_f=l1_b_f, l1_w_out=l1_w_out)
    m = dict(l0_pre_g=m_l0_pre_g, l0_post_g=m_l0_post_g, l0_w_in=m_l0_w_in, l0_q_a_g=m_l0_q_a_g, l0_w_q_b=m_l0_w_q_b,
             l0_kv_a_g=m_l0_kv_a_g, l0_w_kv_b=m_l0_w_kv_b, l0_w_out=m_l0_w_out, l1_pre_g=m_l1_pre_g,
             l1_post_g=m_l1_post_g, l1_w_in=m_l1_w_in, l1_b_f=m_l1_b_f, l1_w_out=m_l1_w_out)
    v = dict(l0_pre_g=v_l0_pre_g, l0_post_g=v_l0_post_g, l0_w_in=v_l0_w_in, l0_q_a_g=v_l0_q_a_g, l0_w_q_b=v_l0_w_q_b,
             l0_kv_a_g=v_l0_kv_a_g, l0_w_kv_b=v_l0_w_kv_b, l0_w_out=v_l0_w_out, l1_pre_g=v_l1_pre_g,
             l1_post_g=v_l1_post_g, l1_w_in=v_l1_w_in, l1_b_f=v_l1_b_f, l1_w_out=v_l1_w_out)

    cx, cy, cc = _place()
    me1 = jnp.reshape(2 * cx + cy, (1,)).astype(jnp.int32)
    c1 = jnp.reshape(cc, (1,)).astype(jnp.int32)
    w_bf = {n: w[n].astype(BF16) for n in MAT_NAMES}
    def with_mine(got, own):
        return lax.dynamic_update_slice(got, own[None], (2 * cx + cy, 0, 0))

    w0_all = with_mine(_weight_gather([w_bf["l0_w_in"]])[0], w_bf["l0_w_in"])
    pack_bf = _pack_shards(w_bf)
    gains = {n: w[n].reshape(1, -1) for n in VEC_NAMES}

    def open_pack(pack_all):
        small = _unpack_shards(with_mine(pack_all, pack_bf))
        full = {n: _join_shards(n, small[n]) for n in PACKED_MATS}
        return _pad_wq(full["l0_w_q_b"]), _pad_wkv(full["l0_w_kv_b"]), full["l0_w_out"], full["l1_w_out"]

    def open_w1(w1_all):
        return _pad_w1(_join_shards("l1_w_in", with_mine(w1_all, w_bf["l1_w_in"])))

    def send_early(dw1p):
        g1 = _cut_shards("l1_w_in", dw1p[:, :ODD_IN_WIDTH])
        return _grad_add_cores(g1, _grad_core_exchange([g1], "grad_core_exchange_l1_w_in")[0], c1,
                               "grad_add_cores_l1_w_in")

    lsum, dx0, grads = _local_step(
        x[0], positions[0], loss_target[0], gains, _pad_w0(_join_shards("l0_w_in", w0_all)), None, None, None, None, None,
        send_early, (pack_bf, w_bf["l1_w_in"], open_pack, open_w1))

    gfull = {"l0_w_in": _unpad_w0(grads["l0_w_in"]), "l0_w_q_b": _unpad_wq(grads["l0_w_q_b"]),
             "l0_w_kv_b": _unpad_wkv(grads["l0_w_kv_b"]), "l0_w_out": grads["l0_w_out"],
             "l1_w_out": grads["l1_w_out"]}
    cut = {n: _cut_shards(n, gfull[n]) for n in gfull}
    tags = ("packed", "l0_w_in")
    g_parts = [_pack_shards(cut), cut["l0_w_in"]]
    q_cores = [_grad_add_cores(p, t, c1, "grad_add_cores_" + tag)
               for p, t, tag in zip(g_parts, _grad_core_exchange(g_parts), tags)]
    slots = list(_grad_chip_exchange(q_cores)) + [grads["early_slots"]]
    q_cores.append(grads["early_q"])
    g_mine = [_grad_add_chips(q, s, me1, "grad_add_chips_" + tag)
              for q, s, tag in zip(q_cores, slots, tags + ("l1_w_in",))]
    g_theirs = _grad_core_gather(g_mine)

    small = _small_allreduce(jnp.concatenate([_pack_vecs({n: grads[n] for n in VEC_NAMES}),
                                              lsum.reshape(D_MODEL // LANES, LANES)], axis=0))
    g_small = small[:SMALL_ROWS]
    loss = 0.5 * jnp.sum(small[SMALL_ROWS:]) / float(D_MODEL)

    whole = [jnp.concatenate([lax.select(cc == 0, a, b), lax.select(cc == 0, b, a)], axis=0)
             for a, b in zip(g_mine, g_theirs)]
    g_mats = dict(_unpack_shards(whole[0]), **dict(zip(WHOLE_MATS, whole[1:])))
    d_mats, m_mats, v_mats = {}, {}, {}
    for n in PACKED_MATS:
        d_mats[n], m_mats[n], v_mats[n] = _adamw(w[n], g_mats[n], m[n], v[n], "adamw_" + n)
    for n in WHOLE_MATS:
        gt = g_mats[n].T
        outs = _adamw(w[n].T, gt, m[n].T, v[n].T, "adamw_" + n)
        g_mats[n], d_mats[n], m_mats[n], v_mats[n] = gt.T, outs[0].T, outs[1].T, outs[2].T
    d_small, m_small, v_small = _adamw(_pack_vecs(w), g_small, _pack_vecs(m), _pack_vecs(v), "adamw_vecs")

    def leaves(mats, vec_pack):
        out = dict(mats)
        out.update(_unpack_vecs(vec_pack))
        return [out[n] for n in WEIGHT_NAMES]

    return (loss, dx0[None], *leaves(g_mats, g_small), *leaves(d_mats, d_small), *leaves(m_mats, m_small),
            *leaves(v_mats, v_small))
```

```python
import jax
import jax.numpy as jnp
from jax import lax
from jax.experimental import pallas as pl
from jax.experimental.pallas import tpu as pltpu

F32 = jnp.float32
BF16 = jnp.bfloat16
MESH = pl.DeviceIdType.MESH

D_MODEL = 1024
RMS_EPS = 1e-6
ROPE_THETA = 10000.0
SB_WIDTH = 512
MLA_Q_LORA = 384
MLA_KV_LORA = 256
MLA_ROPE_DIM = 32
MLA_WIDTH = 512
FOX_WIDTH = 1024
FOX_HEADS = 16
EVEN_IN_WIDTH = 3232
ODD_IN_WIDTH = 4112

ADAM_LR = 0.001
ADAM_B1 = 0.9
ADAM_B2 = 0.999
ADAM_EPS = 1e-08
ADAM_WD = 0.01
ADAM_STEP = 10

LANES = 128
VMEM_LIMIT = 56 * 1024 * 1024

L0_PREP = 0
L0_PREP_W = 768
L0_SBG = 768
L0_MLG = 1280
L0_SBQ = 1792
L0_SBK = 2304
L0_SBV = 2816
L0_WIDTH = 3328
L1_Q = 0
L1_K = 1024
L1_V = 2048
L1_G = 3072
L1_F = 4096
L1_WIDTH = 4224

ATT_T = 256
ATT_GROUP = 4
ATT_QSUB = 2
NEG = -1e30

VEC_ROWS = 8
SMALL_ROWS = 7 * VEC_ROWS


def _cparams(sem, **kw):
    return pltpu.CompilerParams(dimension_semantics=sem, vmem_limit_bytes=VMEM_LIMIT, **kw)


def _dot(a, b):
    return lax.dot_general(a, b, (((1,), (0,)), ((), ())), preferred_element_type=F32)


def _dot_nt(a, b):
    return lax.dot_general(a, b, (((1,), (1,)), ((), ())), preferred_element_type=F32)


def _sigmoid(x):
    return 0.5 * jnp.tanh(0.5 * x) + 0.5


def _rstd(x):
    return lax.rsqrt(jnp.mean(x * x, axis=-1, keepdims=True) + RMS_EPS)


def _norm_bwd(x, g, dy):
    r = _rstd(x)
    xn = x * r
    dxn = dy * g
    dx = r * (dxn - xn * jnp.mean(dxn * xn, axis=-1, keepdims=True))
    return dx, dy * xn


def _split3(x):
    hi = x.astype(BF16)
    r1 = x - hi.astype(F32)
    mid = r1.astype(BF16)
    lo = (r1 - mid.astype(F32)).astype(BF16)
    return hi, mid, lo


def _wide_tile(n, cap=1792):
    return max(t for t in range(LANES, min(n, cap) + 1, LANES) if n % t == 0)


def _pick(n, cands):
    for c in cands:
        if n % c == 0:
            return c
    raise ValueError(n)


def _norm_matmul(x, g, w, name, ride=None):
    S, K = x.shape
    N = w.shape[1]
    tm = _pick(S, (1024, 512, 256))
    tn = _wide_tile(N)
    ni, nj = S // tm, N // tn

    def body(x_ref, g_ref, w_ref, *rest):
        if ride is None:
            o_ref, ht_ref, h_ref = rest
        else:
            a_ref, o_ref, ht_ref, land_ref, h_ref, send_sems, recv_sems = rest
            begin, finish = _gather_phases([a_ref], [land_ref], send_sems, recv_sems)
            pl.when((pl.program_id(0) == 0) & (pl.program_id(1) == 0))(begin)

        @pl.when(pl.program_id(1) == 0)
        def _():
            xv = x_ref[...]
            h = (xv * _rstd(xv)) * g_ref[...]
            h_ref[...] = h.astype(BF16)
            ht_ref[...] = h.T.astype(BF16)
        o_ref[...] = _dot(h_ref[...], w_ref[...])
        if ride is not None:
            pl.when((pl.program_id(0) == ni - 1) & (pl.program_id(1) == nj - 1))(finish)

    in_specs = [pl.BlockSpec((tm, K), lambda i, j: (i, 0)),
                pl.BlockSpec((1, K), lambda i, j: (0, 0)),
                pl.BlockSpec((K, tn), lambda i, j: (0, j))]
    out_specs = [pl.BlockSpec((tm, tn), lambda i, j: (i, j)), pl.BlockSpec((K, tm), lambda i, j: (0, i))]
    out_shape = [jax.ShapeDtypeStruct((S, N), F32), jax.ShapeDtypeStruct((K, S), BF16)]
    scratch = [pltpu.VMEM((tm, K), BF16)]
    args = [x, g, w]
    if ride is not None:
        in_specs.append(_ANY)
        out_specs.append(_ANY)
        out_shape.append(jax.ShapeDtypeStruct((4,) + ride.shape, ride.dtype))
        scratch += [pltpu.SemaphoreType.DMA((6,)), pltpu.SemaphoreType.DMA((6,))]
        args.append(ride)
    return pl.pallas_call(
        body, name=name, grid=(ni, nj), in_specs=in_specs, out_specs=out_specs, out_shape=out_shape,
        scratch_shapes=scratch,
        compiler_params=_cparams(("parallel", "arbitrary") if ride is None else ("arbitrary", "arbitrary")),
    )(*args)


def _matmul_t(at, b, name):
    M, S = at.shape
    N = b.shape[1]
    tn = _wide_tile(N)
    ts = _pick(S, (512, 256))

    def body(a_ref, b_ref, o_ref):
        @pl.when(pl.program_id(1) == 0)
        def _():
            o_ref[...] = jnp.zeros_like(o_ref)
        o_ref[...] += _dot(a_ref[...], b_ref[...].astype(BF16))

    return pl.pallas_call(
        body, name=name, grid=(N // tn, S // ts),
        in_specs=[pl.BlockSpec((M, ts), lambda j, k: (0, k)),
                  pl.BlockSpec((ts, tn), lambda j, k: (k, j))],
        out_specs=pl.BlockSpec((M, tn), lambda j, k: (0, j)),
        out_shape=jax.ShapeDtypeStruct((M, N), F32),
        compiler_params=_cparams(("parallel", "arbitrary")),
    )(at, b)


def _matmul_t_many(at, bs, name):
    M, S = at.shape
    ts = _pick(S, (512, 256))
    n = len(bs)

    def body(*refs):
        a_ref, b_refs, o_refs = refs[0], refs[1:1 + n], refs[1 + n:]

        @pl.when(pl.program_id(0) == 0)
        def _():
            for o_ref in o_refs:
                o_ref[...] = jnp.zeros_like(o_ref)

        a = a_ref[...]
        for b_ref, o_ref in zip(b_refs, o_refs):
            o_ref[...] += _dot(a, b_ref[...].astype(BF16))

    return pl.pallas_call(
        body, name=name, grid=(S // ts,),
        in_specs=[pl.BlockSpec((M, ts), lambda k: (0, k))] + [pl.BlockSpec((ts, b.shape[1]), lambda k: (k, 0)) for b in bs],
        out_specs=[pl.BlockSpec((M, b.shape[1]), lambda k: (0, 0)) for b in bs],
        out_shape=[jax.ShapeDtypeStruct((M, b.shape[1]), F32) for b in bs],
        compiler_params=_cparams(("arbitrary",)),
    )(at, *bs)


def _in_proj_bwd(pieces, w, x, g, dx_up, name, ride=None):
    S, K = x.shape
    N = w.shape[1]
    tm = _pick(S, (256,))
    nsteps = S // tm
    offs = [off for off, _ in pieces]
    arrs = [a for _, a in pieces]

    def body(*refs):
        d_refs = refs[:len(arrs)]
        if ride is None:
            w_ref, x_ref, g_ref, u_ref, dx_ref, dg_ref = refs[len(arrs):]
        else:
            w_ref, x_ref, g_ref, u_ref, q_ref, dx_ref, dg_ref, slots_ref, send_sems, recv_sems = refs[len(arrs):]
            begin, finish = _exchange_phases([q_ref], [slots_ref], send_sems, recv_sems)
            pl.when(pl.program_id(0) == 0)(begin)

        @pl.when(pl.program_id(0) == 0)
        def _():
            dg_ref[...] = jnp.zeros_like(dg_ref)

        acc = None
        for off, d_ref in zip(offs, d_refs):
            part = _dot_nt(d_ref[...].astype(BF16), w_ref[:, off:off + d_ref.shape[1]])
            acc = part if acc is None else acc + part
        dx, dgrow = _norm_bwd(x_ref[...], g_ref[...], acc)
        dx_ref[...] = u_ref[...] + dx
        dg_ref[...] += jnp.sum(dgrow, axis=0, keepdims=True)
        if ride is not None:
            pl.when(pl.program_id(0) == nsteps - 1)(finish)

    row = lambda i: (i, 0)
    fixed = lambda i: (0, 0)
    in_specs = [pl.BlockSpec((tm, a.shape[1]), row) for a in arrs] + [
        pl.BlockSpec((K, N), fixed), pl.BlockSpec((tm, K), row), pl.BlockSpec((1, K), fixed), pl.BlockSpec((tm, K), row)]
    out_specs = [pl.BlockSpec((tm, K), row), pl.BlockSpec((1, K), fixed)]
    out_shape = [jax.ShapeDtypeStruct((S, K), F32), jax.ShapeDtypeStruct((1, K), F32)]
    args = [*arrs, w, x, g, dx_up]
    scratch = []
    if ride is not None:
        in_specs.append(_ANY)
        out_specs.append(_ANY)
        out_shape.append(jax.ShapeDtypeStruct(ride.shape, ride.dtype))
        scratch = [pltpu.SemaphoreType.DMA((3,)), pltpu.SemaphoreType.DMA((3,))]
        args.append(ride)
    return pl.pallas_call(
        body, name=name, grid=(nsteps,), in_specs=in_specs, out_specs=out_specs, out_shape=out_shape,
        scratch_shapes=scratch, compiler_params=_cparams(("arbitrary",)),
    )(*args)


def _out_proj(og_a, og_b, blk_a, blk_b, w, x, g, target, name):
    S = x.shape[0]
    D = x.shape[1]
    tm = _pick(S, (512, 256))
    with_loss = target is not None

    def body(*refs):
        if with_loss:
            a_ref, b_ref, wa_ref, wb_ref, x_ref, g_ref, t_ref, y_ref, o_ref, l_ref = refs
        else:
            a_ref, b_ref, wa_ref, wb_ref, x_ref, g_ref, y_ref, o_ref = refs
        y = _dot(a_ref[...], wa_ref[...]) + _dot(b_ref[...], wb_ref[...])
        y_ref[...] = y
        xn = x_ref[...] + (y * _rstd(y)) * g_ref[...]
        if with_loss:
            @pl.when(pl.program_id(0) == 0)
            def _():
                l_ref[...] = jnp.zeros_like(l_ref)
            d = xn - t_ref[...]
            o_ref[...] = d / float(D)
            l_ref[...] += jnp.sum(d * d, axis=0, keepdims=True)
        else:
            o_ref[...] = xn

    row = lambda i: (i, 0)
    in_specs = [pl.BlockSpec((tm, 512), lambda i: (i, blk_a)),
                pl.BlockSpec((tm, 512), lambda i: (i, blk_b)),
                pl.BlockSpec((512, D), lambda i: (0, 0)),
                pl.BlockSpec((512, D), lambda i: (1, 0)),
                pl.BlockSpec((tm, D), row),
                pl.BlockSpec((1, D), lambda i: (0, 0))]
    out_specs = [pl.BlockSpec((tm, D), row), pl.BlockSpec((tm, D), row)]
    out_shape = [jax.ShapeDtypeStruct((S, D), F32), jax.ShapeDtypeStruct((S, D), F32)]
    args = [og_a, og_b, w, w, x, g]
    if with_loss:
        in_specs.append(pl.BlockSpec((tm, D), row))
        out_specs.append(pl.BlockSpec((1, D), lambda i: (0, 0)))
        out_shape.append(jax.ShapeDtypeStruct((1, D), F32))
        args.append(target)
    return pl.pallas_call(
        body, name=name, grid=(S // tm,), in_specs=in_specs, out_specs=out_specs, out_shape=out_shape,
        compiler_params=_cparams(("arbitrary",)),
    )(*args)


def _out_proj_bwd(dx_up, y, g, w, proj, gate_offs, o_a, o_b, oblk_a, oblk_b, name):
    S, D = y.shape
    tm = _pick(S, (256,))
    gblk = [off // 256 + c for off in gate_offs for c in range(2)]

    def body(u_ref, y_ref, g_ref, w_ref, g0, g1, g2, g3, oa_ref, ob_ref, dy_ref, do_ref, dgate_ref, dg_ref):
        @pl.when(pl.program_id(0) == 0)
        def _():
            dg_ref[...] = jnp.zeros_like(dg_ref)
        dy, dgrow = _norm_bwd(y_ref[...], g_ref[...], u_ref[...])
        dg_ref[...] += jnp.sum(dgrow, axis=0, keepdims=True)
        dyb = dy.astype(BF16)
        dy_ref[...] = dyb
        dog = _dot_nt(dyb, w_ref[...])
        gates = (g0, g1, g2, g3)
        for c in range(4):
            gt = gates[c][...]
            sg = _sigmoid(gt)
            o_ref = oa_ref if c < 2 else ob_ref
            ov = o_ref[:, (c % 2) * 256:(c % 2 + 1) * 256]
            dc = dog[:, c * 256:(c + 1) * 256]
            do_ref[:, c * 256:(c + 1) * 256] = dc * (gt * sg)
            dgate_ref[:, c * 256:(c + 1) * 256] = dc * ov * (sg * (1.0 + gt * (1.0 - sg)))

    row = lambda i: (i, 0)
    gspec = lambda c: pl.BlockSpec((tm, 256), lambda i: (i, gblk[c]))
    return pl.pallas_call(
        body, name=name, grid=(S // tm,),
        in_specs=[pl.BlockSpec((tm, D), row), pl.BlockSpec((tm, D), row), pl.BlockSpec((1, D), lambda i: (0, 0)),
                  pl.BlockSpec((D, D), lambda i: (0, 0)),
                  gspec(0), gspec(1), gspec(2), gspec(3),
                  pl.BlockSpec((tm, 512), lambda i: (i, oblk_a)),
                  pl.BlockSpec((tm, 512), lambda i: (i, oblk_b))],
        out_specs=[pl.BlockSpec((tm, D), row), pl.BlockSpec((tm, D), row), pl.BlockSpec((tm, D), row),
                   pl.BlockSpec((1, D), lambda i: (0, 0))],
        out_shape=[jax.ShapeDtypeStruct((S, D), BF16), jax.ShapeDtypeStruct((S, D), F32),
                   jax.ShapeDtypeStruct((S, D), F32), jax.ShapeDtypeStruct((1, D), F32)],
        compiler_params=_cparams(("arbitrary",)),
    )(dx_up, y, g, w, proj, proj, proj, proj, o_a, o_b)


def _rope_tables(pos, invf, name):
    S = pos.shape[0]
    tm = _pick(S, (512, 256))

    def body(p_ref, f_ref, c_ref, s1_ref, s2_ref):
        lane = lax.broadcasted_iota(jnp.int32, (1, LANES), 1)
        ang = p_ref[...].astype(F32) * f_ref[...]
        c, s = jnp.cos(ang), jnp.sin(ang)
        c_ref[...] = jnp.where((lane >= 64) & (lane < 96), c, 1.0)
        s1_ref[...] = jnp.where((lane >= 64) & (lane < 80), -s, 0.0)
        s2_ref[...] = jnp.where((lane >= 80) & (lane < 96), s, 0.0)

    spec = pl.BlockSpec((tm, LANES), lambda i: (i, 0))
    return pl.pallas_call(
        body, name=name, grid=(S // tm,),
        in_specs=[pl.BlockSpec((tm, 1), lambda i: (i, 0)), pl.BlockSpec((1, LANES), lambda i: (0, 0))],
        out_specs=[spec, spec, spec],
        out_shape=[jax.ShapeDtypeStruct((S, LANES), F32)] * 3,
        compiler_params=_cparams(("parallel",)),
    )(pos, invf)


def _rope(x, c, s1, s2):
    return x * c + pltpu.roll(x, LANES - 16, 1) * s1 + pltpu.roll(x, 16, 1) * s2


def _rope_t(d, c, s1, s2):
    return d * c + pltpu.roll(d * s1, 16, 1) + pltpu.roll(d * s2, LANES - 16, 1)


def _mla_prep(proj, gq, gkv, wq, wkv, cosT, s1T, s2T, name):
    S = proj.shape[0]
    tm = _pick(S, (256,))

    def body(p_ref, gq_ref, gkv_ref, wq_ref, wkv_ref, c_ref, s1_ref, s2_ref, q_ref, k_ref, v_ref, qn_ref, cn_ref):
        qa = p_ref[:, 0:384]
        ckv = p_ref[:, 384:640]
        kr = p_ref[:, 640:768]
        qn32 = (qa * _rstd(qa)) * gq_ref[...]
        cn32 = (ckv * _rstd(ckv)) * gkv_ref[...]
        qn = qn32.astype(BF16)
        cn = cn32.astype(BF16)
        qn_ref[...] = qn32.T.astype(BF16)
        cn_ref[...] = cn32.T.astype(BF16)
        qb = _dot(qn, wq_ref[...])
        kvb = _dot(cn, wkv_ref[...])
        c, s1, s2 = c_ref[...], s1_ref[...], s2_ref[...]
        krr = _rope(kr, c, s1, s2)
        for h in range(8):
            sl = slice(h * LANES, (h + 1) * LANES)
            q_ref[:, sl] = _rope(qb[:, sl], c, s1, s2)
            k_ref[:, sl] = kvb[:, sl] + krr
        v_ref[...] = kvb[:, 1024:1536]

    row = lambda i: (i, 0)
    fixed = lambda i: (0, 0)
    tspec = pl.BlockSpec((tm, LANES), row)
    return pl.pallas_call(
        body, name=name, grid=(S // tm,),
        in_specs=[pl.BlockSpec((tm, L0_PREP_W), lambda i: (i, L0_PREP // L0_PREP_W)),
                  pl.BlockSpec((1, 384), fixed), pl.BlockSpec((1, 256), fixed),
                  pl.BlockSpec((384, 1024), fixed), pl.BlockSpec((256, 1536), fixed), tspec, tspec, tspec],
        out_specs=[pl.BlockSpec((tm, 1024), row), pl.BlockSpec((tm, 1024), row), pl.BlockSpec((tm, 512), row),
                   pl.BlockSpec((384, tm), lambda i: (0, i)), pl.BlockSpec((256, tm), lambda i: (0, i))],
        out_shape=[jax.ShapeDtypeStruct((S, 1024), F32), jax.ShapeDtypeStruct((S, 1024), F32),
                   jax.ShapeDtypeStruct((S, 512), F32), jax.ShapeDtypeStruct((384, S), BF16),
                   jax.ShapeDtypeStruct((256, S), BF16)],
        compiler_params=_cparams(("parallel",)),
    )(proj, gq, gkv, wq, wkv, cosT, s1T, s2T)


def _mla_prep_bwd(dq, dk, dv, proj, gq, gkv, wq, wkv, cosT, s1T, s2T, name):
    S = proj.shape[0]
    tm = _pick(S, (256,))

    def body(dq_ref, dk_ref, dv_ref, p_ref, gq_ref, gkv_ref, wq_ref, wkv_ref, c_ref, s1_ref, s2_ref,
             dp_ref, dqb_ref, dkvb_ref, dgq_ref, dgkv_ref):
        @pl.when(pl.program_id(0) == 0)
        def _():
            dgq_ref[...] = jnp.zeros_like(dgq_ref)
            dgkv_ref[...] = jnp.zeros_like(dgkv_ref)
        c, s1, s2 = c_ref[...], s1_ref[...], s2_ref[...]
        lane = lax.broadcasted_iota(jnp.int32, (1, LANES), 1)
        dkr = jnp.zeros((tm, LANES), F32)
        for h in range(8):
            sl = slice(h * LANES, (h + 1) * LANES)
            dqb_ref[:, sl] = _rope_t(dq_ref[:, sl], c, s1, s2).astype(BF16)
            dkh = dk_ref[:, sl]
            dkvb_ref[:, sl] = dkh.astype(BF16)
            dkr = dkr + dkh
        dkvb_ref[:, 1024:1536] = dv_ref[...].astype(BF16)
        dkr = jnp.where((lane >= 64) & (lane < 96), _rope_t(dkr, c, s1, s2), 0.0)
        dqn = _dot_nt(dqb_ref[...], wq_ref[...])
        dcn = _dot_nt(dkvb_ref[...], wkv_ref[...])
        dqa, gq_row = _norm_bwd(p_ref[:, 0:384], gq_ref[...], dqn)
        dckv, gkv_row = _norm_bwd(p_ref[:, 384:640], gkv_ref[...], dcn)
        dp_ref[:, 0:384] = dqa
        dp_ref[:, 384:640] = dckv
        dp_ref[:, 640:768] = dkr
        dgq_ref[...] += jnp.sum(gq_row, axis=0, keepdims=True)
        dgkv_ref[...] += jnp.sum(gkv_row, axis=0, keepdims=True)

    row = lambda i: (i, 0)
    fixed = lambda i: (0, 0)
    tspec = pl.BlockSpec((tm, LANES), row)
    return pl.pallas_call(
        body, name=name, grid=(S // tm,),
        in_specs=[pl.BlockSpec((tm, 1024), row), pl.BlockSpec((tm, 1024), row), pl.BlockSpec((tm, 512), row),
                  pl.BlockSpec((tm, L0_PREP_W), lambda i: (i, L0_PREP // L0_PREP_W)),
                  pl.BlockSpec((1, 384), fixed), pl.BlockSpec((1, 256), fixed),
                  pl.BlockSpec((384, 1024), fixed), pl.BlockSpec((256, 1536), fixed), tspec, tspec, tspec],
        out_specs=[pl.BlockSpec((tm, L0_PREP_W), row), pl.BlockSpec((tm, 1024), row), pl.BlockSpec((tm, 1536), row),
                   pl.BlockSpec((1, 384), fixed), pl.BlockSpec((1, 256), fixed)],
        out_shape=[jax.ShapeDtypeStruct((S, L0_PREP_W), F32), jax.ShapeDtypeStruct((S, 1024), BF16),
                   jax.ShapeDtypeStruct((S, 1536), BF16), jax.ShapeDtypeStruct((1, 384), F32),
                   jax.ShapeDtypeStruct((1, 256), F32)],
        compiler_params=_cparams(("arbitrary",)),
    )(dq, dk, dv, proj, gq, gkv, wq, wkv, cosT, s1T, s2T)


def _fox_prep(proj, bf, name):
    S = proj.shape[0]
    tm = _pick(S, (256,))

    def body(f_ref, b_ref, c_ref, carry_ref):
        @pl.when(pl.program_id(0) == 0)
        def _():
            carry_ref[...] = jnp.zeros_like(carry_ref)
        u = f_ref[...] + b_ref[...]
        lf = jnp.minimum(u, 0.0) - jnp.log(1.0 + jnp.exp(-jnp.abs(u)))
        r = lax.broadcasted_iota(jnp.int32, (tm, tm), 0)
        cidx = lax.broadcasted_iota(jnp.int32, (tm, tm), 1)
        tri = (cidx <= r).astype(BF16)
        hi, mid, lo = _split3(lf)
        c = carry_ref[...] + (_dot(tri, hi) + _dot(tri, mid) + _dot(tri, lo))
        c_ref[...] = c
        carry_ref[...] = c[tm - 1:tm, :]

    return pl.pallas_call(
        body, name=name, grid=(S // tm,),
        in_specs=[pl.BlockSpec((tm, LANES), lambda i: (i, L1_F // LANES)), pl.BlockSpec((1, LANES), lambda i: (0, 0))],
        out_specs=pl.BlockSpec((tm, LANES), lambda i: (i, 0)),
        out_shape=jax.ShapeDtypeStruct((S, LANES), F32),
        scratch_shapes=[pltpu.VMEM((1, LANES), F32)],
        compiler_params=_cparams(("arbitrary",)),
    )(proj, bf)


def _fox_prep_bwd(dc, proj, bf, name):
    S = proj.shape[0]
    tm = _pick(S, (256,))
    nb = S // tm

    def body(dc_ref, f_ref, b_ref, df_ref, db_ref, carry_ref):
        @pl.when(pl.program_id(0) == 0)
        def _():
            carry_ref[...] = jnp.zeros_like(carry_ref)
            db_ref[...] = jnp.zeros_like(db_ref)
        r = lax.broadcasted_iota(jnp.int32, (tm, tm), 0)
        cidx = lax.broadcasted_iota(jnp.int32, (tm, tm), 1)
        tri = (cidx >= r).astype(BF16)
        hi, mid, lo = _split3(dc_ref[...])
        dlf = carry_ref[...] + (_dot(tri, hi) + _dot(tri, mid) + _dot(tri, lo))
        carry_ref[...] = dlf[0:1, :]
        u = f_ref[...] + b_ref[...]
        e = jnp.exp(-jnp.abs(u))
        sneg = jnp.where(u >= 0.0, e, 1.0) / (1.0 + e)
        lane = lax.broadcasted_iota(jnp.int32, (1, LANES), 1)
        df = jnp.where(lane < FOX_HEADS, dlf * sneg, 0.0)
        df_ref[...] = df
        db_ref[...] += jnp.sum(df, axis=0, keepdims=True)

    return pl.pallas_call(
        body, name=name, grid=(nb,),
        in_specs=[pl.BlockSpec((tm, LANES), lambda i: (nb - 1 - i, 0)),
                  pl.BlockSpec((tm, LANES), lambda i: (nb - 1 - i, L1_F // LANES)),
                  pl.BlockSpec((1, LANES), lambda i: (0, 0))],
        out_specs=[pl.BlockSpec((tm, LANES), lambda i: (nb - 1 - i, 0)), pl.BlockSpec((1, LANES), lambda i: (0, 0))],
        out_shape=[jax.ShapeDtypeStruct((S, LANES), F32), jax.ShapeDtypeStruct((1, LANES), F32)],
        scratch_shapes=[pltpu.VMEM((1, LANES), F32)],
        compiler_params=_cparams(("arbitrary",)),
    )(dc, proj, bf)


def _att_specs(kind, S, T):
    if kind == "sb":
        qo, ko, vo, go = L0_SBQ // LANES, L0_SBK // LANES, L0_SBV // LANES, L0_SBG // LANES
    elif kind == "fox":
        qo, ko, vo, go = L1_Q // LANES, L1_K // LANES, L1_V // LANES, L1_G // LANES
    else:
        go = L0_MLG // LANES
        return (pl.BlockSpec((T, 256), lambda p, i: (i, p)), pl.BlockSpec((S, 256), lambda p, i: (0, p)),
                pl.BlockSpec((S, LANES), lambda p, i: (0, p)), pl.BlockSpec((T, LANES), lambda p, i: (i, go + p)))
    return (pl.BlockSpec((T, LANES), lambda p, i: (i, qo + p)), pl.BlockSpec((S, LANES), lambda p, i: (0, ko + p)),
            pl.BlockSpec((S, LANES), lambda p, i: (0, vo + p)), pl.BlockSpec((T, LANES), lambda p, i: (i, go + p)))


def _per_q_tile(tile_body, hows):
    T = ATT_T

    def view(ref, u, how):
        if how == "rows":
            return ref.at[pl.ds(u * T, T)]
        if how == "lanes":
            return ref.at[:, pl.ds(u * T, T)]
        if how == "stat":
            return ref.at[:, u]
        return ref

    def body(*refs):
        for u in range(ATT_QSUB):
            tile_body(pl.program_id(1) * ATT_QSUB + u, *[view(r, u, how) for r, how in zip(refs, hows)])

    return body


def _mask_flags(js, masked_at):
    return [t == masked_at for t in range(len(js))]


def _loop_tiles(i, tiles, right_to_left, G=ATT_GROUP):
    ng = i // G
    rest = i - ng * G

    def leftover():
        for r in range(G):
            @pl.when(rest == r)
            def _():
                if right_to_left:
                    tiles([i - u for u in range(r + 1)], 0)
                else:
                    tiles([ng * G + u for u in range(r + 1)], r)

    def group(g, carry):
        if right_to_left:
            tiles([ng * G - 1 - (g * G + u) for u in range(G)], None)
        else:
            tiles([g * G + u for u in range(G)], None)
        return carry

    if right_to_left:
        leftover()
    lax.fori_loop(0, ng, group, 0)
    if not right_to_left:
        leftover()


def _head_q(kind, q_ref, m0, scale):
    if kind == "mla":
        return [q_ref[:, 0:LANES].astype(BF16), q_ref[:, LANES:2 * LANES].astype(BF16)]
    qv = q_ref[...] * scale
    return [jnp.where(m0, qv, 0.0).astype(BF16), jnp.where(m0, 0.0, qv).astype(BF16)]


def _head_k(kind, k_ref, start, T):
    if kind == "mla":
        return [k_ref[pl.ds(start, T), 0:LANES].astype(BF16), k_ref[pl.ds(start, T), LANES:2 * LANES].astype(BF16)]
    kb = k_ref[pl.ds(start, T), :].astype(BF16)
    return [kb, kb]


def _softmax_fwd(kind, qkvg, c_col, S, npairs, name):
    T = ATT_T
    nq = S // T
    fox = kind == "fox"
    scale = (96 if kind == "mla" else 64) ** -0.5

    def body(i, *refs):
        if fox:
            q_ref, k_ref, v_ref, g_ref, cc_ref, o_ref, og_ref, ogt_ref, st_ref, m_ref, acc_ref = refs
        else:
            q_ref, k_ref, v_ref, g_ref, o_ref, og_ref, ogt_ref, st_ref, m_ref, acc_ref = refs
        m0 = lax.broadcasted_iota(jnp.int32, (1, LANES), 1) < 64
        top = lax.broadcasted_iota(jnp.int32, (LANES, 1), 0) < 64
        key = lax.broadcasted_iota(jnp.int32, (T, LANES), 0)
        qrow = lax.broadcasted_iota(jnp.int32, (T, LANES), 1)
        qh = _head_q(kind, q_ref, m0, scale)
        m_ref[...] = jnp.full(m_ref.shape, NEG, F32)
        acc_ref[...] = jnp.zeros(acc_ref.shape, F32)
        chains = [(h, b) for h in range(2) for b in range(T // LANES)]

        def tiles(js, masked_at):
            starts = [pl.multiple_of(j * T, T) for j in js]
            zss = []
            for start in starts:
                kh = _head_k(kind, k_ref, start, T)
                zss.append(_split_blocks([_dot_nt(kh[h], qh[h]) for h in range(2)]))
            pss, alss = [], []
            for start, zs, masked in zip(starts, zss, _mask_flags(js, masked_at)):
                ps, alphas = [], []
                for (h, b), z in zip(chains, zs):
                    lanes = slice(b * LANES, (b + 1) * LANES)
                    if kind == "mla":
                        z = z * scale
                    if fox:
                        z = z - cc_ref[h, pl.ds(start, T), :]
                    if masked:
                        z = jnp.where(key <= qrow + b * LANES, z, NEG)
                    m_prev = m_ref[h, :, lanes]
                    m_new = jnp.maximum(m_prev, jnp.max(z, axis=0, keepdims=True))
                    alphas.append(jnp.exp(m_prev - m_new))
                    ps.append(jnp.exp(z - m_new).astype(BF16))
                    m_ref[h, :, lanes] = m_new
                pss.append(_join_blocks(ps, T // LANES))
                alss.append(_join_blocks(alphas, T // LANES))
            for start, ps, alphas in zip(starts, pss, alss):
                vt = v_ref[pl.ds(start, T), :].T
                vth = [jnp.where(top, vt, 1.0).astype(BF16), jnp.where(top, 1.0, vt).astype(BF16)]
                for h in range(2):
                    acc_ref[h] = alphas[h] * acc_ref[h] + _dot(vth[h], ps[h])

        _loop_tiles(i, tiles, False, 2 * ATT_GROUP)
        acc = [acc_ref[0], acc_ref[1]]
        ot = jnp.concatenate([acc[0][0:64] / acc[0][64:128], acc[1][64:128] / acc[1][0:64]], axis=0)
        o = ot.T
        o_ref[...] = o
        gt = g_ref[...]
        og = o * (gt * _sigmoid(gt))
        og_ref[...] = og.astype(BF16)
        ogt_ref[...] = og.T.astype(BF16)
        st_ref[0] = m_ref[0] + jnp.log(acc[0][64:65])
        st_ref[1] = m_ref[1] + jnp.log(acc[1][0:1])

    QT = ATT_QSUB * T
    qs, ks, vs, gs = _att_specs(kind, S, QT)
    in_specs = [qs, ks, vs, gs]
    args = list(qkvg)
    hows = ["rows", None, None, "rows"]
    if fox:
        in_specs += [pl.BlockSpec((2, S, LANES), lambda p, i: (p, 0, 0))]
        args += [c_col]
        hows += [None]
    hows += ["rows", "rows", "lanes", "stat", None, None]
    W = npairs * LANES
    return pl.pallas_call(
        _per_q_tile(body, hows), name=name, grid=(npairs, nq // ATT_QSUB), in_specs=in_specs,
        out_specs=[pl.BlockSpec((QT, LANES), lambda p, i: (i, p)), pl.BlockSpec((QT, LANES), lambda p, i: (i, p)),
                   pl.BlockSpec((LANES, QT), lambda p, i: (p, i)),
                   pl.BlockSpec((2, ATT_QSUB, 1, T), lambda p, i: (p, i, 0, 0))],
        out_shape=[jax.ShapeDtypeStruct((S, W), F32), jax.ShapeDtypeStruct((S, W), BF16),
                   jax.ShapeDtypeStruct((W, S), BF16),
                   jax.ShapeDtypeStruct((2 * npairs, nq, 1, T), F32)],
        scratch_shapes=[pltpu.VMEM((2, 1, T), F32), pltpu.VMEM((2, LANES, T), F32)],
        compiler_params=_cparams(("parallel", "parallel")),
    )(*args)


def _softplus_parts(z):
    sp = jnp.maximum(z, 0.0) + jnp.log(1.0 + jnp.exp(-jnp.abs(z)))
    return sp, z - sp


def _cumsum_dot(tri2, his, los):
    return _split_blocks([_dot(tri2, jnp.concatenate([hi, lo], axis=0)) for hi, lo in zip(his, los)])


def _split2(x):
    hi = x.astype(BF16)
    return hi, (x - hi.astype(F32)).astype(BF16)


def _split_blocks(per_head):
    return [x[:, b * LANES:(b + 1) * LANES] for x in per_head for b in range(x.shape[1] // LANES)]


def _join_blocks(per_block, nb):
    return [jnp.concatenate(per_block[h * nb:(h + 1) * nb], axis=1) for h in range(len(per_block) // nb)]


def _row_of(col):
    return jnp.broadcast_to(col, (col.shape[0], LANES)).T[0:1]


def _softmax_bwd_t(kind, q, k, v, do, do_off, o, lse, c_col, S, npairs, name):
    T = ATT_T
    nq = S // T
    nb = T // LANES
    fox = kind == "fox"
    mla = kind == "mla"
    scale = (96 if mla else 64) ** -0.5
    kw = 256 if mla else LANES

    def body(i, *refs):
        if fox:
            (q_ref, k_ref, v_ref, do_ref, o_ref, st_ref, cc_ref,
             dq_ref, dk_ref, dv_ref, dck_ref, dcq_ref, dqt_ref, rs_ref, dkx_ref) = refs
        else:
            q_ref, k_ref, v_ref, do_ref, o_ref, st_ref, dq_ref, dk_ref, dv_ref, dqt_ref = refs

        @pl.when(i == 0)
        def _():
            dv_ref[...] = jnp.zeros_like(dv_ref)
            if fox:
                dkx_ref[...] = jnp.zeros_like(dkx_ref)
            else:
                dk_ref[...] = jnp.zeros_like(dk_ref)

        m0 = lax.broadcasted_iota(jnp.int32, (1, LANES), 1) < 64
        top = lax.broadcasted_iota(jnp.int32, (LANES, 1), 0) < 64
        key = lax.broadcasted_iota(jnp.int32, (T, LANES), 0)
        qrow = lax.broadcasted_iota(jnp.int32, (T, LANES), 1)
        qh = _head_q(kind, q_ref, m0, scale)
        if fox:
            qv = q_ref[...] * scale
            qk = [jnp.where(m0, qv, 1.0).astype(BF16), jnp.where(m0, 1.0, qv).astype(BF16)]
        else:
            qk = qh
        dov = do_ref[...]
        prod = dov * o_ref[...]
        dd = [_row_of(jnp.sum(jnp.where(m0, prod, 0.0), axis=1, keepdims=True)),
              _row_of(jnp.sum(jnp.where(m0, 0.0, prod), axis=1, keepdims=True))]
        doh = [jnp.where(m0, dov, 0.0).astype(BF16), jnp.where(m0, 0.0, dov).astype(BF16)]
        lse = [st_ref[0], st_ref[1]]
        dqt_ref[...] = jnp.zeros_like(dqt_ref)
        if fox:
            rs_ref[...] = jnp.zeros_like(rs_ref)
        chains = [(h, b) for h in range(2) for b in range(nb)]

        def tiles(js, masked_at):
            starts = [pl.multiple_of(j * T, T) for j in js]
            zss, dpss = [], []
            for start in starts:
                vb = v_ref[pl.ds(start, T), :].astype(BF16)
                kh = _head_k(kind, k_ref, start, T)
                zss.append(_split_blocks([_dot_nt(kh[h], qh[h]) for h in range(2)]))
                dpss.append(_split_blocks([_dot_nt(vb, doh[h]) for h in range(2)]))
            pss, dsss = [], []
            for start, zs, dps, masked in zip(starts, zss, dpss, _mask_flags(js, masked_at)):
                ps, dss = [], []
                for (h, b), z, dp in zip(chains, zs, dps):
                    lanes = slice(b * LANES, (b + 1) * LANES)
                    if mla:
                        z = z * scale
                    if fox:
                        z = z - cc_ref[h, pl.ds(start, T), :]
                    if masked:
                        z = jnp.where(key <= qrow + b * LANES, z, NEG)
                    p = jnp.exp(z - lse[h][:, lanes])
                    ds = p * (dp - dd[h][:, lanes])
                    dsb = ds.astype(BF16)
                    if fox:
                        rs_ref[h, :, lanes] += jnp.sum(dsb.astype(F32), axis=0, keepdims=True)
                    ps.append(p.astype(BF16))
                    dss.append(dsb)
                pss.append(_join_blocks(ps, nb))
                dsss.append(_join_blocks(dss, nb))
            for start, ps, dss in zip(starts, pss, dsss):
                kt = k_ref[pl.ds(start, T), :].T.astype(BF16)
                dvc = None
                for h in range(2):
                    dkh = _dot(dss[h], qk[h])
                    dvh = _dot(ps[h], doh[h])
                    dvc = dvh if dvc is None else dvc + dvh
                    kth = kt[h * LANES:(h + 1) * LANES] if mla else kt
                    dqt_ref[h] += _dot(kth, dss[h])
                    if fox:
                        dkx_ref[h, pl.ds(start, T), :] += dkh
                    elif mla:
                        dk_ref[pl.ds(start, T), h * LANES:(h + 1) * LANES] += dkh * scale
                    else:
                        dk_ref[pl.ds(start, T), :] += dkh
                dv_ref[pl.ds(start, T), :] += dvc

        _loop_tiles(i, tiles, False)
        if mla:
            dq_ref[:, 0:LANES] = dqt_ref[0].T * scale
            dq_ref[:, LANES:2 * LANES] = dqt_ref[1].T * scale
        else:
            dq_ref[...] = jnp.where(top, dqt_ref[0], dqt_ref[1]).T * scale
        if fox:
            dcq_ref[0] = rs_ref[0]
            dcq_ref[1] = rs_ref[1]

            @pl.when(i == nq - 1)
            def _():
                dk_ref[...] = jnp.where(m0, dkx_ref[0], dkx_ref[1])
                dck_ref[0] = dkx_ref[0].T[64:65]
                dck_ref[1] = dkx_ref[1].T[0:1]

    QT = ATT_QSUB * T
    qs, ks, vs, _ = _att_specs(kind, S, QT)
    stat = pl.BlockSpec((2, ATT_QSUB, 1, T), lambda p, i: (p, i, 0, 0))
    in_specs = [qs, ks, vs,
                pl.BlockSpec((QT, LANES), lambda p, i: (i, do_off + p)),
                pl.BlockSpec((QT, LANES), lambda p, i: (i, p)), stat]
    args = [q, k, v, do, o, lse]
    hows = ["rows", None, None, "rows", "rows", "stat"]
    W = npairs * LANES
    out_specs = [pl.BlockSpec((QT, kw), lambda p, i: (i, p)), pl.BlockSpec((S, kw), lambda p, i: (0, p)),
                 pl.BlockSpec((S, LANES), lambda p, i: (0, p))]
    out_shape = [jax.ShapeDtypeStruct((S, npairs * kw), F32), jax.ShapeDtypeStruct((S, npairs * kw), F32),
                 jax.ShapeDtypeStruct((S, W), F32)]
    scratch = [pltpu.VMEM((2, LANES, T), F32)]
    if fox:
        in_specs.append(pl.BlockSpec((2, S, LANES), lambda p, i: (p, 0, 0)))
        args.append(c_col)
        out_specs += [pl.BlockSpec((2, 1, S), lambda p, i: (p, 0, 0)), stat]
        out_shape += [jax.ShapeDtypeStruct((2 * npairs, 1, S), F32), jax.ShapeDtypeStruct((2 * npairs, nq, 1, T), F32)]
        scratch += [pltpu.VMEM((2, 1, T), F32), pltpu.VMEM((2, S, LANES), F32)]
        hows += [None, "rows", None, None, None, "stat", None, None, None]
    else:
        hows += ["rows", None, None, None]
    return pl.pallas_call(
        _per_q_tile(body, hows), name=name, grid=(npairs, nq // ATT_QSUB), in_specs=in_specs, out_specs=out_specs,
        out_shape=out_shape, scratch_shapes=scratch, compiler_params=_cparams(("parallel", "arbitrary")),
    )(*args)


def _sb_fwd_t(proj, S, npairs, name):
    T = ATT_T
    nq = S // T
    nb = T // LANES
    scale = 64 ** -0.5

    def body(i, q_ref, k_ref, v_ref, g_ref, o_ref, og_ref, ogt_ref, st_ref, rem_ref, acc_ref):
        m0 = lax.broadcasted_iota(jnp.int32, (1, LANES), 1) < 64
        top = lax.broadcasted_iota(jnp.int32, (LANES, 1), 0) < 64
        key = lax.broadcasted_iota(jnp.int32, (T, LANES), 0)
        qrow = lax.broadcasted_iota(jnp.int32, (T, LANES), 1)
        r = lax.broadcasted_iota(jnp.int32, (T, T), 0)
        c = lax.broadcasted_iota(jnp.int32, (T, T), 1)
        after = (c > r).astype(BF16)
        after2 = jnp.concatenate([after, after], axis=1)
        qh = _head_q("sb", q_ref, m0, scale)
        rem_ref[...] = jnp.zeros_like(rem_ref)
        acc_ref[...] = jnp.zeros_like(acc_ref)
        chains = [(h, b) for h in range(2) for b in range(nb)]

        def tiles(js, masked_at):
            zss = []
            for j in js:
                kb = k_ref[pl.ds(pl.multiple_of(j * T, T), T), :].astype(BF16)
                zss.append(_split_blocks([_dot_nt(kb, qh[h]) for h in range(2)]))
            lass, sums, hiss, loss = [], [], [], []
            for zs, masked in zip(zss, _mask_flags(js, masked_at)):
                las, sm, his, los = [], [], [], []
                for (h, b), z in zip(chains, zs):
                    sp, la = _softplus_parts(z)
                    if masked:
                        sp = jnp.where(key < qrow + b * LANES, sp, 0.0)
                    hi, lo = _split2(sp)
                    las.append(la)
                    sm.append(jnp.sum(sp, axis=0, keepdims=True))
                    his.append(hi)
                    los.append(lo)
                lass.append(las)
                sums.append(sm)
                hiss.append(_join_blocks(his, nb))
                loss.append(_join_blocks(los, nb))
            rcss = [_cumsum_dot(after2, his, los) for his, los in zip(hiss, loss)]
            wss = []
            for las, sm, rcs, masked in zip(lass, sums, rcss, _mask_flags(js, masked_at)):
                ws = []
                for (h, b), la, s, rc in zip(chains, las, sm, rcs):
                    lanes = slice(b * LANES, (b + 1) * LANES)
                    w = jnp.exp(la - (rem_ref[h, :, lanes] + rc))
                    if masked:
                        w = jnp.where(key < qrow + b * LANES, w, 0.0)
                    ws.append(w.astype(BF16))
                    rem_ref[h, :, lanes] += s
                wss.append(_join_blocks(ws, nb))
            for j, ws in zip(js, wss):
                vtb = v_ref[pl.ds(pl.multiple_of(j * T, T), T), :].T.astype(BF16)
                for h in range(2):
                    acc_ref[h] += _dot(vtb, ws[h])

        _loop_tiles(i, tiles, True)
        o = jnp.where(top, acc_ref[0], acc_ref[1]).T
        o_ref[...] = o
        gt = g_ref[...]
        og = o * (gt * _sigmoid(gt))
        og_ref[...] = og.astype(BF16)
        ogt_ref[...] = og.T.astype(BF16)
        st_ref[0] = rem_ref[0]
        st_ref[1] = rem_ref[1]

    QT = ATT_QSUB * T
    qs, ks, vs, gs = _att_specs("sb", S, QT)
    W = npairs * LANES
    hows = ["rows", None, None, "rows", "rows", "rows", "lanes", "stat", None, None]
    return pl.pallas_call(
        _per_q_tile(body, hows), name=name, grid=(npairs, nq // ATT_QSUB),
        in_specs=[qs, ks, vs, gs],
        out_specs=[pl.BlockSpec((QT, LANES), lambda p, i: (i, p)), pl.BlockSpec((QT, LANES), lambda p, i: (i, p)),
                   pl.BlockSpec((LANES, QT), lambda p, i: (p, i)),
                   pl.BlockSpec((2, ATT_QSUB, 1, T), lambda p, i: (p, i, 0, 0))],
        out_shape=[jax.ShapeDtypeStruct((S, W), F32), jax.ShapeDtypeStruct((S, W), BF16),
                   jax.ShapeDtypeStruct((W, S), BF16),
                   jax.ShapeDtypeStruct((2 * npairs, nq, 1, T), F32)],
        scratch_shapes=[pltpu.VMEM((2, 1, T), F32), pltpu.VMEM((2, LANES, T), F32)],
        compiler_params=_cparams(("parallel", "parallel")),
    )(proj, proj, proj, proj)


def _sb_bwd_t(proj, do, tot, S, npairs, name):
    T = ATT_T
    nq = S // T
    nb = T // LANES
    scale = 64 ** -0.5

    def body(i, q_ref, k_ref, v_ref, do_ref, st_ref, dq_ref, dk_ref, dv_ref, dqt_ref, pre_ref, gpre_ref):

        @pl.when(i == 0)
        def _():
            dk_ref[...] = jnp.zeros_like(dk_ref)
            dv_ref[...] = jnp.zeros_like(dv_ref)

        m0 = lax.broadcasted_iota(jnp.int32, (1, LANES), 1) < 64
        top = lax.broadcasted_iota(jnp.int32, (LANES, 1), 0) < 64
        key = lax.broadcasted_iota(jnp.int32, (T, LANES), 0)
        qrow = lax.broadcasted_iota(jnp.int32, (T, LANES), 1)
        r = lax.broadcasted_iota(jnp.int32, (T, T), 0)
        c = lax.broadcasted_iota(jnp.int32, (T, T), 1)
        upto = (c <= r).astype(BF16)
        upto2 = jnp.concatenate([upto, upto], axis=1)
        left = (c < r).astype(BF16)
        qh = _head_q("sb", q_ref, m0, scale)
        dov = do_ref[...]
        doh = [jnp.where(m0, dov, 0.0).astype(BF16), jnp.where(m0, 0.0, dov).astype(BF16)]
        tot_h = [st_ref[0], st_ref[1]]
        dqt_ref[...] = jnp.zeros_like(dqt_ref)
        pre_ref[...] = jnp.zeros_like(pre_ref)
        gpre_ref[...] = jnp.zeros_like(gpre_ref)
        chains = [(h, b) for h in range(2) for b in range(nb)]

        def tiles(js, masked_at):
            starts = [pl.multiple_of(j * T, T) for j in js]
            zss, dwss = [], []
            for start in starts:
                vb = v_ref[pl.ds(start, T), :].astype(BF16)
                kb = k_ref[pl.ds(start, T), :].astype(BF16)
                zss.append(_split_blocks([_dot_nt(kb, qh[h]) for h in range(2)]))
                dwss.append(_split_blocks([_dot_nt(vb, doh[h]) for h in range(2)]))
            lass, sums, hiss, loss = [], [], [], []
            for zs, masked in zip(zss, _mask_flags(js, masked_at)):
                las, sm, his, los = [], [], [], []
                for (h, b), z in zip(chains, zs):
                    sp, la = _softplus_parts(z)
                    if masked:
                        sp = jnp.where(key < qrow + b * LANES, sp, 0.0)
                    hi, lo = _split2(sp)
                    las.append(la)
                    sm.append(jnp.sum(sp, axis=0, keepdims=True))
                    his.append(hi)
                    los.append(lo)
                lass.append(las)
                sums.append(sm)
                hiss.append(_join_blocks(his, nb))
                loss.append(_join_blocks(los, nb))
            pcss = [_cumsum_dot(upto2, his, los) for his, los in zip(hiss, loss)]
            wss, gss = [], []
            for las, sm, pcs, dws, masked in zip(lass, sums, pcss, dwss, _mask_flags(js, masked_at)):
                ws, gs = [], []
                for (h, b), la, s, pc, dw in zip(chains, las, sm, pcs, dws):
                    lanes = slice(b * LANES, (b + 1) * LANES)
                    w = jnp.exp(la - ((tot_h[h][:, lanes] - pre_ref[h, :, lanes]) - pc))
                    if masked:
                        w = jnp.where(key < qrow + b * LANES, w, 0.0)
                    ws.append(w.astype(BF16))
                    gs.append(dw * w)
                    pre_ref[h, :, lanes] += s
                wss.append(_join_blocks(ws, nb))
                gss.append(gs)
            gcss = [_split_blocks([_dot(left, g) for g in _join_blocks([g.astype(BF16) for g in gs], nb)]) for gs in gss]
            dzss = []
            for las, gs, gcs, masked in zip(lass, gss, gcss, _mask_flags(js, masked_at)):
                dzs = []
                for (h, b), la, g, gc in zip(chains, las, gs, gcs):
                    lanes = slice(b * LANES, (b + 1) * LANES)
                    dz = g - (g + (gpre_ref[h, :, lanes] + gc)) * jnp.exp(la)
                    if masked:
                        dz = jnp.where(key < qrow + b * LANES, dz, 0.0)
                    dzs.append(dz.astype(BF16))
                    gpre_ref[h, :, lanes] += jnp.sum(g, axis=0, keepdims=True)
                dzss.append(_join_blocks(dzs, nb))
            for start, ws, dzs in zip(starts, wss, dzss):
                kt = k_ref[pl.ds(start, T), :].T.astype(BF16)
                dkc = dvc = None
                for h in range(2):
                    dkh = _dot(dzs[h], qh[h])
                    dvh = _dot(ws[h], doh[h])
                    dkc = dkh if dkc is None else dkc + dkh
                    dvc = dvh if dvc is None else dvc + dvh
                    dqt_ref[h] += _dot(kt, dzs[h])
                dk_ref[pl.ds(start, T), :] += dkc
                dv_ref[pl.ds(start, T), :] += dvc

        _loop_tiles(i, tiles, False)
        dq_ref[...] = jnp.where(top, dqt_ref[0], dqt_ref[1]).T * scale

    QT = ATT_QSUB * T
    qs, ks, vs, _ = _att_specs("sb", S, QT)
    W = npairs * LANES
    hows = ["rows", None, None, "rows", "stat", "rows", None, None, None, None, None]
    return pl.pallas_call(
        _per_q_tile(body, hows), name=name, grid=(npairs, nq // ATT_QSUB),
        in_specs=[qs, ks, vs,
                  pl.BlockSpec((QT, LANES), lambda p, i: (i, p)),
                  pl.BlockSpec((2, ATT_QSUB, 1, T), lambda p, i: (p, i, 0, 0))],
        out_specs=[pl.BlockSpec((QT, LANES), lambda p, i: (i, p)), pl.BlockSpec((S, LANES), lambda p, i: (0, p)),
                   pl.BlockSpec((S, LANES), lambda p, i: (0, p))],
        out_shape=[jax.ShapeDtypeStruct((S, W), F32)] * 3,
        scratch_shapes=[pltpu.VMEM((2, LANES, T), F32), pltpu.VMEM((2, 1, T), F32), pltpu.VMEM((2, 1, T), F32)],
        compiler_params=_cparams(("parallel", "arbitrary")),
    )(proj, proj, proj, do, tot)


def _pad_w0(w):
    z = lambda n: jnp.zeros((w.shape[0], n), w.dtype)
    return jnp.concatenate([w[:, 2048:2432], w[:, 2432:2688], z(64), w[:, 2688:2720], z(32),
                            w[:, 1536:2048], w[:, 2720:3232], w[:, 0:512], w[:, 512:1024], w[:, 1024:1536]], axis=1)


def _unpad_w0(wp):
    return jnp.concatenate([wp[:, L0_SBQ:L0_SBQ + 512], wp[:, L0_SBK:L0_SBK + 512], wp[:, L0_SBV:L0_SBV + 512],
                            wp[:, L0_SBG:L0_SBG + 512], wp[:, 0:384], wp[:, 384:640], wp[:, 704:736],
                            wp[:, L0_MLG:L0_MLG + 512]], axis=1)


def _pad_wq(w):
    return jnp.pad(w.reshape(384, 8, 96), ((0, 0), (0, 0), (0, 32))).reshape(384, 1024)


def _unpad_wq(wp):
    return wp.reshape(384, 8, 128)[:, :, :96].reshape(384, 768)


def _pad_wkv(w):
    w3 = w.reshape(256, 8, 128)
    k = jnp.pad(w3[:, :, :64], ((0, 0), (0, 0), (0, 64))).reshape(256, 1024)
    return jnp.concatenate([k, w3[:, :, 64:].reshape(256, 512)], axis=1)


def _unpad_wkv(wp):
    k = wp[:, :1024].reshape(256, 8, 128)[:, :, :64]
    v = wp[:, 1024:].reshape(256, 8, 64)
    return jnp.concatenate([k, v], axis=-1).reshape(256, 1024)


def _pad_w1(w):
    return jnp.concatenate([w, jnp.zeros((w.shape[0], L1_WIDTH - ODD_IN_WIDTH), w.dtype)], axis=1)


def _local_step(x, positions, target, g, w0p, wqp, wkvp, wo0, w1p, wo1, send_early=None):
    S = x.shape[0]
    nq = S // ATT_T
    invf = ROPE_THETA ** (-jnp.arange(0, MLA_ROPE_DIM, 2, dtype=F32) / MLA_ROPE_DIM)
    invf = jnp.concatenate([jnp.zeros((64,), F32), invf, invf, jnp.zeros((32,), F32)]).reshape(1, LANES)
    cosT, s1T, s2T = _rope_tables(positions.reshape(S, 1), invf, "rope_tables")
    bfp = jnp.pad(g["l1_b_f"], ((0, 0), (0, LANES - FOX_HEADS)))

    if isinstance(w1p, tuple):
        w1_shard, finish_w1 = w1p
        proj0, h0t, w1_all = _norm_matmul(x, g["l0_pre_g"], w0p, "l0_in_proj", ride=w1_shard)
        w1p = finish_w1(w1_all)
    else:
        proj0, h0t = _norm_matmul(x, g["l0_pre_g"], w0p, "l0_in_proj")
    qm, km, vm, qnt, cnt = _mla_prep(proj0, g["l0_q_a_g"], g["l0_kv_a_g"], wqp, wkvp, cosT, s1T, s2T, "mla_prep")
    o_sb, og_sb, ogt_sb, tot_sb = _sb_fwd_t(proj0, S, 4, "sb_fwd")
    o_ml, og_ml, ogt_ml, lse_ml = _softmax_fwd("mla", (qm, km, vm, proj0), None, S, 4, "mla_fwd")
    y0, x1 = _out_proj(og_sb, og_ml, 0, 0, wo0, x, g["l0_post_g"], None, "l0_out_proj")

    proj1, h1t = _norm_matmul(x1, g["l1_pre_g"], w1p, "l1_in_proj")
    cfx = _fox_prep(proj1, bfp, "fox_prep")
    c16 = cfx[:, :FOX_HEADS].T
    c_col = jnp.broadcast_to(c16[:, :, None], (FOX_HEADS, S, LANES))
    o_fx, og_fx, ogt_fx, lse_fx = _softmax_fwd("fox", (proj1, proj1, proj1, proj1), c_col, S, 8, "fox_fwd")
    y1, dx2, lsum = _out_proj(og_fx, og_fx, 0, 1, wo1, x1, g["l1_post_g"], target, "l1_out_proj")

    dy1, do1, dgate1, d_post1 = _out_proj_bwd(dx2, y1, g["l1_post_g"], wo1, proj1, (L1_G, L1_G + 512), o_fx, o_fx, 0, 1, "l1_out_bwd")
    dwo1 = _matmul_t(ogt_fx, dy1, "l1_dw_out")
    dq1, dk1, dv1, dck, dcq = _softmax_bwd_t("fox", proj1, proj1, proj1, do1, 0, o_fx, lse_fx, c_col, S, 8,
                                             "fox_bwd")
    dc = jnp.pad((dcq.reshape(FOX_HEADS, S) - dck.reshape(FOX_HEADS, S)).T, ((0, 0), (0, LANES - FOX_HEADS)))
    df, d_bf = _fox_prep_bwd(dc, proj1, bfp, "fox_prep_bwd")
    pieces1 = [(L1_Q, dq1), (L1_K, dk1), (L1_V, dv1), (L1_G, dgate1), (L1_F, df)]
    dx1, d_pre1 = _in_proj_bwd(pieces1, w1p, x1, g["l1_pre_g"], dx2, "l1_in_bwd")
    dw1p = jnp.concatenate(_matmul_t_many(h1t, [dq1, dk1], "l1_dw_in_a")
                           + _matmul_t_many(h1t, [dv1, dgate1, df], "l1_dw_in_b"), axis=1)
    early = None if send_early is None else send_early(dw1p)

    dy0, do0, dgate0, d_post0 = _out_proj_bwd(dx1, y0, g["l0_post_g"], wo0, proj0, (L0_SBG, L0_MLG), o_sb, o_ml, 0, 0,
                                              "l0_out_bwd")
    dwo0 = jnp.concatenate([_matmul_t(ogt_sb, dy0, "l0_dw_out_sb"), _matmul_t(ogt_ml, dy0, "l0_dw_out_mla")], axis=0)
    dsq, dsk, dsv = _sb_bwd_t(proj0, do0, tot_sb, S, 4, "sb_bwd")
    dqm, dkm, dvm = _softmax_bwd_t("mla", qm, km, vm, do0, 4, o_ml, lse_ml, None, S, 4, "mla_bwd")
    dprep, dqb, dkvb, d_qag, d_kvag = _mla_prep_bwd(dqm, dkm, dvm, proj0, g["l0_q_a_g"], g["l0_kv_a_g"], wqp, wkvp,
                                                    cosT, s1T, s2T, "mla_prep_bwd")
    dwqp = _matmul_t(qnt, dqb, "l0_dw_qb")
    dwkvp = _matmul_t(cnt, dkvb, "l0_dw_kvb")
    pieces0 = [(L0_PREP, dprep), (L0_SBG, dgate0), (L0_SBQ, dsq), (L0_SBK, dsk), (L0_SBV, dsv)]
    if send_early is None:
        dx0, d_pre0 = _in_proj_bwd(pieces0, w0p, x, g["l0_pre_g"], dx1, "l0_in_bwd")
        early_slots = None
    else:
        dx0, d_pre0, early_slots = _in_proj_bwd(pieces0, w0p, x, g["l0_pre_g"], dx1, "l0_in_bwd", ride=early)
    dw0p = jnp.concatenate(_matmul_t_many(h0t, [dprep, dgate0], "l0_dw_in_a")
                           + _matmul_t_many(h0t, [dsq, dsk, dsv], "l0_dw_in_b"), axis=1)

    grads = {
        "l0_pre_g": d_pre0, "l0_post_g": d_post0, "l0_w_in": dw0p, "l0_q_a_g": d_qag, "l0_w_q_b": dwqp,
        "l0_kv_a_g": d_kvag, "l0_w_kv_b": dwkvp, "l0_w_out": dwo0, "l1_pre_g": d_pre1, "l1_post_g": d_post1,
        "l1_w_in": dw1p, "l1_b_f": d_bf[:, :FOX_HEADS], "l1_w_out": dwo1,
    }
    grads["early_q"], grads["early_slots"] = early, early_slots
    return lsum, dx0, grads


_ANY = pl.BlockSpec(memory_space=pl.ANY)


def _place():
    return lax.axis_index("x"), lax.axis_index("y"), lax.axis_index("c")


def _other_chips(x, y):
    return [(1 - x, y), (x, 1 - y), (1 - x, 1 - y)]


def _half(rows, c):
    return pl.ds(c * (rows // 2), rows // 2)


def _gather_phases(p_refs, out_refs, send_sems, recv_sems):
    n = len(p_refs)
    x, y, c = _place()
    sibling = (x, y, 1 - c)
    chips = _other_chips(x, y)

    def blk(k, chip, cc):
        return out_refs[k].at[2 * chip[0] + chip[1], _half(p_refs[k].shape[0], cc)]

    def copy(s, src, dst, to):
        return pltpu.make_async_remote_copy(src_ref=src, dst_ref=dst, send_sem=send_sems.at[s],
                                            recv_sem=recv_sems.at[s], device_id=to, device_id_type=MESH)

    def first():
        return [copy(6 * k + j, p_refs[k].at[_half(p_refs[k].shape[0], c)], blk(k, (x, y), c), (*chip, c))
                for j, chip in enumerate(chips) for k in range(n)]

    def begin():
        for cp in first():
            cp.start()

    def finish():
        passed = []
        for j, chip in enumerate(chips):
            for k in range(n):
                copy(6 * k + j, blk(k, chip, c), blk(k, chip, c), (x, y, c)).wait_recv()
                passed.append(copy(6 * k + 3 + j, blk(k, chip, c), blk(k, chip, c), sibling))
                passed[-1].start()
        for j, chip in enumerate(chips):
            for k in range(n):
                copy(6 * k + 3 + j, blk(k, chip, 1 - c), blk(k, chip, 1 - c), (x, y, c)).wait_recv()
        for cp in first() + passed:
            cp.wait_send()

    return begin, finish


def _weight_gather(parts):
    n = len(parts)

    def body(*refs):
        begin, finish = _gather_phases(refs[:n], refs[n:2 * n], refs[2 * n], refs[2 * n + 1])
        begin()
        finish()

    return pl.pallas_call(
        body, name="weight_gather", in_specs=[_ANY] * n, out_specs=[_ANY] * n,
        out_shape=[jax.ShapeDtypeStruct((4,) + a.shape, a.dtype) for a in parts],
        scratch_shapes=[pltpu.SemaphoreType.DMA((6 * n,)), pltpu.SemaphoreType.DMA((6 * n,))],
    )(*parts)


def _grad_core_exchange(ps, name="grad_core_exchange"):
    n = len(ps)

    def body(*refs):
        p_refs, recv_refs, send_sems, recv_sems = refs[:n], refs[n:2 * n], refs[2 * n], refs[2 * n + 1]
        x, y, c = _place()
        give = [pltpu.make_async_remote_copy(src_ref=p_refs[k].at[j, _half(p_refs[k].shape[1], 1 - c)],
                                             dst_ref=recv_refs[k].at[j], send_sem=send_sems.at[4 * k + j],
                                             recv_sem=recv_sems.at[4 * k + j], device_id=(x, y, 1 - c),
                                             device_id_type=MESH) for k in range(n) for j in range(4)]
        for cp in give:
            cp.start()
        for cp in give:
            cp.wait()

    return pl.pallas_call(
        body, name=name, in_specs=[_ANY] * n, out_specs=[_ANY] * n,
        out_shape=[jax.ShapeDtypeStruct((4, p.shape[1] // 2, p.shape[2]), p.dtype) for p in ps],
        scratch_shapes=[pltpu.SemaphoreType.DMA((4 * n,)), pltpu.SemaphoreType.DMA((4 * n,))],
    )(*ps)


def _grad_rows(rows):
    return _pick(rows, (1296, 512, rows))


def _grad_add_cores(p, theirs, c1, name):
    _, rh, cols = theirs.shape
    tr = _grad_rows(rh)

    def body(c_ref, a_ref, b_ref, o_ref):
        o_ref[...] = (a_ref[...] + b_ref[...]).astype(BF16)

    spec = pl.BlockSpec((None, tr, cols), lambda j, r, c: (j, r, 0))
    grid_spec = pltpu.PrefetchScalarGridSpec(
        num_scalar_prefetch=1, grid=(4, rh // tr),
        in_specs=[pl.BlockSpec((None, None, tr, cols), lambda j, r, c: (j, c[0], r, 0)), spec], out_specs=spec)
    return pl.pallas_call(
        body, name=name, grid_spec=grid_spec, out_shape=jax.ShapeDtypeStruct(theirs.shape, BF16),
        compiler_params=_cparams(("parallel", "parallel")),
    )(c1, p.reshape(4, 2, rh, cols), theirs)


def _exchange_phases(q_refs, out_refs, send_sems, recv_sems):
    n = len(q_refs)
    x, y, c = _place()
    me = 2 * x + y
    chips = _other_chips(x, y)

    def sends():
        return [pltpu.make_async_remote_copy(src_ref=q_refs[k].at[2 * chip[0] + chip[1]], dst_ref=out_refs[k].at[me],
                                             send_sem=send_sems.at[3 * k + j], recv_sem=recv_sems.at[3 * k + j],
                                             device_id=(*chip, c), device_id_type=MESH)
                for j, chip in enumerate(chips) for k in range(n)]

    def begin():
        for cp in sends():
            cp.start()

    def finish():
        for j, chip in enumerate(chips):
            for k in range(n):
                slot = out_refs[k].at[2 * chip[0] + chip[1]]
                pltpu.make_async_remote_copy(src_ref=slot, dst_ref=slot, send_sem=send_sems.at[3 * k + j],
                                             recv_sem=recv_sems.at[3 * k + j], device_id=(x, y, c),
                                             device_id_type=MESH).wait_recv()
        for cp in sends():
            cp.wait_send()

    return begin, finish


def _grad_chip_exchange(qs):
    n = len(qs)

    def body(*refs):
        begin, finish = _exchange_phases(refs[:n], refs[n:2 * n], refs[2 * n], refs[2 * n + 1])
        begin()
        finish()

    return pl.pallas_call(
        body, name="grad_chip_exchange", in_specs=[_ANY] * n, out_specs=[_ANY] * n,
        out_shape=[jax.ShapeDtypeStruct(q.shape, q.dtype) for q in qs],
        scratch_shapes=[pltpu.SemaphoreType.DMA((3 * n,)), pltpu.SemaphoreType.DMA((3 * n,))],
    )(*qs)


def _grad_add_chips(q, slots, me1, name):
    _, rh, cols = q.shape
    tr = _grad_rows(rh)

    def body(me_ref, own_ref, s0, s1, s2, s3, o_ref):
        me = me_ref[0]
        t = [jnp.where(me == j, own_ref[...], s[...]).astype(F32) for j, s in enumerate((s0, s1, s2, s3))]
        o_ref[...] = ((t[0] + t[1]) + t[2]) + t[3]

    def slot_spec(j):
        return pl.BlockSpec((None, tr, cols), lambda r, me: (jnp.where(me[0] == j, (j + 1) % 4, j), r, 0))

    grid_spec = pltpu.PrefetchScalarGridSpec(
        num_scalar_prefetch=1, grid=(rh // tr,),
        in_specs=[pl.BlockSpec((None, tr, cols), lambda r, me: (me[0], r, 0))] + [slot_spec(j) for j in range(4)],
        out_specs=pl.BlockSpec((tr, cols), lambda r, me: (r, 0)))
    return pl.pallas_call(
        body, name=name, grid_spec=grid_spec, out_shape=jax.ShapeDtypeStruct(q.shape[1:], F32),
        compiler_params=_cparams(("parallel",)),
    )(me1, q, slots, slots, slots, slots)


def _final_exchange(ts, sp):
    n = len(ts)

    def body(*refs):
        t_refs, sp_ref, out_refs, tot_ref = refs[:n], refs[n], refs[n + 1:2 * n + 1], refs[2 * n + 1]
        gath_ref, send_sems, recv_sems = refs[2 * n + 2:]
        x, y, c = _place()
        me = 4 * x + 2 * y + c
        gath_ref[me] = sp_ref[...]
        peers = []
        for k in range(1, 8):
            px = 1 - x if k & 4 else x
            py = 1 - y if k & 2 else y
            pc = 1 - c if k & 1 else c
            peers.append((px, py, pc))
        give = [pltpu.make_async_remote_copy(src_ref=t_refs[k], dst_ref=out_refs[k], send_sem=send_sems.at[k],
                                             recv_sem=recv_sems.at[k], device_id=(x, y, 1 - c), device_id_type=MESH)
                for k in range(n)]
        sends = [pltpu.make_async_remote_copy(src_ref=sp_ref, dst_ref=gath_ref.at[me], send_sem=send_sems.at[n + k],
                                              recv_sem=recv_sems.at[n + k], device_id=peer, device_id_type=MESH)
                 for k, peer in enumerate(peers)]
        for cp in give + sends:
            cp.start()
        for k, (px, py, pc) in enumerate(peers):
            slot = gath_ref.at[4 * px + 2 * py + pc]
            pltpu.make_async_remote_copy(src_ref=slot, dst_ref=slot, send_sem=send_sems.at[n + k],
                                         recv_sem=recv_sems.at[n + k], device_id=(x, y, c),
                                         device_id_type=MESH).wait_recv()
        for cp in sends:
            cp.wait_send()
        tot = gath_ref[0]
        for d in range(1, 8):
            tot = tot + gath_ref[d]
        tot_ref[...] = tot
        for cp in give:
            cp.wait()

    vm = pl.BlockSpec(memory_space=pltpu.VMEM)
    return pl.pallas_call(
        body, name="final_exchange", in_specs=[_ANY] * n + [vm], out_specs=[_ANY] * n + [vm],
        out_shape=[jax.ShapeDtypeStruct(t.shape, t.dtype) for t in ts] + [jax.ShapeDtypeStruct(sp.shape, sp.dtype)],
        scratch_shapes=[pltpu.VMEM((8,) + sp.shape, sp.dtype), pltpu.SemaphoreType.DMA((n + 7,)),
                        pltpu.SemaphoreType.DMA((n + 7,))],
    )(*ts, sp)


def _adamw_update(w, gv, m, v):
    mn = ADAM_B1 * m + (1.0 - ADAM_B1) * gv
    vn = ADAM_B2 * v + (1.0 - ADAM_B2) * (gv * gv)
    m_hat = mn / (1.0 - ADAM_B1 ** ADAM_STEP)
    v_hat = vn / (1.0 - ADAM_B2 ** ADAM_STEP)
    return -ADAM_LR * (m_hat / (jnp.sqrt(v_hat) + ADAM_EPS) + ADAM_WD * w), mn, vn


def _adamw(w, g, m, v, name):
    rows, cols = w.shape

    def body(w_ref, g_ref, m_ref, v_ref, d_ref, mo_ref, vo_ref):
        d_ref[...], mo_ref[...], vo_ref[...] = _adamw_update(w_ref[...], g_ref[...], m_ref[...], v_ref[...])

    if rows % 256 == 0 or cols % 256 != 0:
        tr = _pick(rows, (256, rows))
        grid, spec = (rows // tr,), pl.BlockSpec((tr, cols), lambda r: (r, 0))
    else:
        grid, spec = (cols // 256,), pl.BlockSpec((rows, 256), lambda r: (0, r))
    shp = jax.ShapeDtypeStruct(w.shape, F32)
    return pl.pallas_call(
        body, name=name, grid=grid, in_specs=[spec] * 4, out_specs=[spec] * 3, out_shape=[shp] * 3,
        compiler_params=_cparams(("parallel",)),
    )(w, g, m, v)


MAT_NAMES = ("l0_w_in", "l0_w_q_b", "l0_w_kv_b", "l0_w_out", "l1_w_in", "l1_w_out")
VEC_NAMES = ("l0_pre_g", "l0_post_g", "l0_q_a_g", "l0_kv_a_g", "l1_pre_g", "l1_post_g", "l1_b_f")
WEIGHT_NAMES = ("l0_pre_g", "l0_post_g", "l0_w_in", "l0_q_a_g", "l0_w_q_b", "l0_kv_a_g", "l0_w_kv_b", "l0_w_out",
                "l1_pre_g", "l1_post_g", "l1_w_in", "l1_b_f", "l1_w_out")
MAT_SHARD = {"l0_w_in": (1024, 808), "l0_w_q_b": (384, 192), "l0_w_kv_b": (256, 256), "l0_w_out": (256, 1024),
             "l1_w_in": (1024, 1028), "l1_w_out": (256, 1024)}
ROW_SHARDED = ("l0_w_out", "l1_w_out")
WHOLE_MATS = ("l0_w_in", "l1_w_in")
PACKED_MATS = ("l0_w_q_b", "l0_w_kv_b", "l0_w_out", "l1_w_out")
VEC_LEN = {"l0_pre_g": 1024, "l0_post_g": 1024, "l0_q_a_g": 384, "l0_kv_a_g": 256, "l1_pre_g": 1024,
           "l1_post_g": 1024, "l1_b_f": 16}


def _mat_rows(n):
    r, c = MAT_SHARD[n]
    return r * c // LANES


def _pack_shards(shards):
    return jnp.concatenate([shards[n].reshape(shards[n].shape[:-2] + (_mat_rows(n), LANES)) for n in PACKED_MATS],
                           axis=-2)


def _unpack_shards(pack):
    out, at = {}, 0
    for n in PACKED_MATS:
        out[n] = pack[..., at:at + _mat_rows(n), :].reshape(pack.shape[:-2] + MAT_SHARD[n])
        at += _mat_rows(n)
    return out


def _join_shards(n, s):
    if n in ROW_SHARDED:
        return s.reshape(4 * s.shape[1], s.shape[2])
    return s.transpose(1, 0, 2).reshape(s.shape[1], 4 * s.shape[2])


def _cut_shards(n, w):
    r, c = MAT_SHARD[n]
    if n in ROW_SHARDED:
        return w.reshape(4, r, c)
    return w.reshape(r, 4, c).transpose(1, 0, 2)


def _pack_vecs(vecs):
    parts = []
    for n in VEC_NAMES:
        v = vecs[n].reshape(-1)
        parts.append(jnp.pad(v, (0, VEC_ROWS * LANES - v.shape[0])).reshape(VEC_ROWS, LANES))
    return jnp.concatenate(parts, axis=0)


def _unpack_vecs(pack):
    return {n: pack[k * VEC_ROWS:(k + 1) * VEC_ROWS].reshape(-1)[:VEC_LEN[n]] for k, n in enumerate(VEC_NAMES)}


def kernel(x, positions, l0_pre_g, l0_post_g, l0_w_in, l0_q_a_g, l0_w_q_b, l0_kv_a_g, l0_w_kv_b, l0_w_out, l1_pre_g, l1_post_g, l1_w_in, l1_b_f, l1_w_out, loss_target, m_l0_pre_g, m_l0_post_g, m_l0_w_in, m_l0_q_a_g, m_l0_w_q_b, m_l0_kv_a_g, m_l0_w_kv_b, m_l0_w_out, m_l1_pre_g, m_l1_post_g, m_l1_w_in, m_l1_b_f, m_l1_w_out, v_l0_pre_g, v_l0_post_g, v_l0_w_in, v_l0_q_a_g, v_l0_w_q_b, v_l0_kv_a_g, v_l0_w_kv_b, v_l0_w_out, v_l1_pre_g, v_l1_post_g, v_l1_w_in, v_l1_b_f, v_l1_w_out):
    w = dict(l0_pre_g=l0_pre_g, l0_post_g=l0_post_g, l0_w_in=l0_w_in, l0_q_a_g=l0_q_a_g, l0_w_q_b=l0_w_q_b,
             l0_kv_a_g=l0_kv_a_g, l0_w_kv_b=l0_w_kv_b, l0_w_out=l0_w_out, l1_pre_g=l1_pre_g, l1_post_g=l1_post_g,
             l1_w_in=l1_w_in, l1_b_f=l1_b_f, l1_w_out=l1_w_out)
    m = dict(l0_pre_g=m_l0_pre_g, l0_post_g=m_l0_post_g, l0_w_in=m_l0_w_in, l0_q_a_g=m_l0_q_a_g, l0_w_q_b=m_l0_w_q_b,
             l0_kv_a_g=m_l0_kv_a_g, l0_w_kv_b=m_l0_w_kv_b, l0_w_out=m_l0_w_out, l1_pre_g=m_l1_pre_g,
             l1_post_g=m_l1_post_g, l1_w_in=m_l1_w_in, l1_b_f=m_l1_b_f, l1_w_out=m_l1_w_out)
    v = dict(l0_pre_g=v_l0_pre_g, l0_post_g=v_l0_post_g, l0_w_in=v_l0_w_in, l0_q_a_g=v_l0_q_a_g, l0_w_q_b=v_l0_w_q_b,
             l0_kv_a_g=v_l0_kv_a_g, l0_w_kv_b=v_l0_w_kv_b, l0_w_out=v_l0_w_out, l1_pre_g=v_l1_pre_g,
             l1_post_g=v_l1_post_g, l1_w_in=v_l1_w_in, l1_b_f=v_l1_b_f, l1_w_out=v_l1_w_out)

    cx, cy, cc = _place()
    me1 = jnp.reshape(2 * cx + cy, (1,)).astype(jnp.int32)
    c1 = jnp.reshape(cc, (1,)).astype(jnp.int32)
    w_bf = {n: w[n].astype(BF16) for n in MAT_NAMES}
    def with_mine(got, own):
        return lax.dynamic_update_slice(got, own[None], (2 * cx + cy, 0, 0))

    mine = [_pack_shards(w_bf), w_bf["l0_w_in"]]
    got = [with_mine(g, a) for g, a in zip(_weight_gather(mine), mine)]
    gathered = dict(_unpack_shards(got[0]), l0_w_in=got[1])
    full = {n: _join_shards(n, gathered[n]) for n in MAT_NAMES if n != "l1_w_in"}
    gains = {n: w[n].reshape(1, -1) for n in VEC_NAMES}

    def finish_w1(w1_all):
        return _pad_w1(_join_shards("l1_w_in", with_mine(w1_all, w_bf["l1_w_in"])))

    def send_early(dw1p):
        g1 = _cut_shards("l1_w_in", dw1p[:, :ODD_IN_WIDTH])
        return _grad_add_cores(g1, _grad_core_exchange([g1], "grad_core_exchange_l1_w_in")[0], c1,
                               "grad_add_cores_l1_w_in")

    lsum, dx0, grads = _local_step(
        x[0], positions[0], loss_target[0], gains, _pad_w0(full["l0_w_in"]), _pad_wq(full["l0_w_q_b"]),
        _pad_wkv(full["l0_w_kv_b"]), full["l0_w_out"], (w_bf["l1_w_in"], finish_w1), full["l1_w_out"], send_early)

    gfull = {"l0_w_in": _unpad_w0(grads["l0_w_in"]), "l0_w_q_b": _unpad_wq(grads["l0_w_q_b"]),
             "l0_w_kv_b": _unpad_wkv(grads["l0_w_kv_b"]), "l0_w_out": grads["l0_w_out"],
             "l1_w_out": grads["l1_w_out"]}
    cut = {n: _cut_shards(n, gfull[n]) for n in gfull}
    tags = ("packed", "l0_w_in")
    g_parts = [_pack_shards(cut), cut["l0_w_in"]]
    q_cores = [_grad_add_cores(p, t, c1, "grad_add_cores_" + tag)
               for p, t, tag in zip(g_parts, _grad_core_exchange(g_parts), tags)]
    slots = list(_grad_chip_exchange(q_cores)) + [grads["early_slots"]]
    q_cores.append(grads["early_q"])
    g_mine = [_grad_add_chips(q, s, me1, "grad_add_chips_" + tag)
              for q, s, tag in zip(q_cores, slots, tags + ("l1_w_in",))]
    *g_theirs, small = _final_exchange(g_mine, jnp.concatenate([_pack_vecs({n: grads[n] for n in VEC_NAMES}),
                                                                lsum.reshape(D_MODEL // LANES, LANES)], axis=0))
    g_small = small[:SMALL_ROWS]
    loss = 0.5 * jnp.sum(small[SMALL_ROWS:]) / float(D_MODEL)

    whole = [jnp.concatenate([lax.select(cc == 0, a, b), lax.select(cc == 0, b, a)], axis=0)
             for a, b in zip(g_mine, g_theirs)]
    g_mats = dict(_unpack_shards(whole[0]), **dict(zip(WHOLE_MATS, whole[1:])))
    d_mats, m_mats, v_mats = {}, {}, {}
    for n in PACKED_MATS:
        d_mats[n], m_mats[n], v_mats[n] = _adamw(w[n], g_mats[n], m[n], v[n], "adamw_" + n)
    for n in WHOLE_MATS:
        gt = g_mats[n].T
        outs = _adamw(w[n].T, gt, m[n].T, v[n].T, "adamw_" + n)
        g_mats[n], d_mats[n], m_mats[n], v_mats[n] = gt.T, outs[0].T, outs[1].T, outs[2].T
    d_small, m_small, v_small = _adamw(_pack_vecs(w), g_small, _pack_vecs(m), _pack_vecs(v), "adamw_vecs")

    def leaves(mats, vec_pack):
        out = dict(mats)
        out.update(_unpack_vecs(vec_pack))
        return [out[n] for n in WEIGHT_NAMES]

    return (loss, dx0[None], *leaves(g_mats, g_small), *leaves(d_mats, d_small), *leaves(m_mats, m_small),
            *leaves(v_mats, v_small))
```

```python
import jax
import jax.numpy as jnp
from jax import lax
from jax.experimental import pallas as pl
from jax.experimental.pallas import tpu as pltpu

F32 = jnp.float32
BF16 = jnp.bfloat16
MESH = pl.DeviceIdType.MESH

D_MODEL = 1024
RMS_EPS = 1e-6
ROPE_THETA = 10000.0
SB_WIDTH = 512
MLA_Q_LORA = 384
MLA_KV_LORA = 256
MLA_ROPE_DIM = 32
MLA_WIDTH = 512
FOX_WIDTH = 1024
FOX_HEADS = 16
EVEN_IN_WIDTH = 3232
ODD_IN_WIDTH = 4112

ADAM_LR = 0.001
ADAM_B1 = 0.9
ADAM_B2 = 0.999
ADAM_EPS = 1e-08
ADAM_WD = 0.01
ADAM_STEP = 10

LANES = 128
VMEM_LIMIT = 56 * 1024 * 1024

L0_PREP = 0
L0_PREP_W = 768
L0_SBG = 768
L0_MLG = 1280
L0_SBQ = 1792
L0_SBK = 2304
L0_SBV = 2816
L0_WIDTH = 3328
L1_Q = 0
L1_K = 1024
L1_V = 2048
L1_G = 3072
L1_F = 4096
L1_WIDTH = 4224

ATT_T = 256
ATT_GROUP = 4
ATT_QSUB = 4
NEG = -1e30

VEC_ROWS = 8
SMALL_ROWS = 7 * VEC_ROWS


def _cparams(sem, **kw):
    return pltpu.CompilerParams(dimension_semantics=sem, vmem_limit_bytes=VMEM_LIMIT, **kw)


def _dot(a, b):
    return lax.dot_general(a, b, (((1,), (0,)), ((), ())), preferred_element_type=F32)


def _dot_nt(a, b):
    return lax.dot_general(a, b, (((1,), (1,)), ((), ())), preferred_element_type=F32)


def _sigmoid(x):
    return 0.5 * jnp.tanh(0.5 * x) + 0.5


def _rstd(x):
    return lax.rsqrt(jnp.mean(x * x, axis=-1, keepdims=True) + RMS_EPS)


def _norm_bwd(x, g, dy):
    r = _rstd(x)
    xn = x * r
    dxn = dy * g
    dx = r * (dxn - xn * jnp.mean(dxn * xn, axis=-1, keepdims=True))
    return dx, dy * xn


def _split3(x):
    hi = x.astype(BF16)
    r1 = x - hi.astype(F32)
    mid = r1.astype(BF16)
    lo = (r1 - mid.astype(F32)).astype(BF16)
    return hi, mid, lo


def _wide_tile(n, cap=1792):
    return max(t for t in range(LANES, min(n, cap) + 1, LANES) if n % t == 0)


def _pick(n, cands):
    for c in cands:
        if n % c == 0:
            return c
    raise ValueError(n)


def _norm_matmul(x, g, w, name, ride=None):
    S, K = x.shape
    N = w.shape[1]
    tm = _pick(S, (1024, 512, 256))
    tn = _wide_tile(N)
    ni, nj = S // tm, N // tn

    def body(x_ref, g_ref, w_ref, *rest):
        if ride is None:
            o_ref, ht_ref, h_ref = rest
        else:
            a_ref, o_ref, ht_ref, land_ref, h_ref, send_sems, recv_sems = rest
            begin, finish = _gather_phases([a_ref], [land_ref], send_sems, recv_sems)
            pl.when((pl.program_id(0) == 0) & (pl.program_id(1) == 0))(begin)

        @pl.when(pl.program_id(1) == 0)
        def _():
            xv = x_ref[...]
            h = (xv * _rstd(xv)) * g_ref[...]
            h_ref[...] = h.astype(BF16)
            ht_ref[...] = h.T.astype(BF16)
        o_ref[...] = _dot(h_ref[...], w_ref[...])
        if ride is not None:
            pl.when((pl.program_id(0) == ni - 1) & (pl.program_id(1) == nj - 1))(finish)

    in_specs = [pl.BlockSpec((tm, K), lambda i, j: (i, 0)),
                pl.BlockSpec((1, K), lambda i, j: (0, 0)),
                pl.BlockSpec((K, tn), lambda i, j: (0, j))]
    out_specs = [pl.BlockSpec((tm, tn), lambda i, j: (i, j)), pl.BlockSpec((K, tm), lambda i, j: (0, i))]
    out_shape = [jax.ShapeDtypeStruct((S, N), F32), jax.ShapeDtypeStruct((K, S), BF16)]
    scratch = [pltpu.VMEM((tm, K), BF16)]
    args = [x, g, w]
    if ride is not None:
        in_specs.append(_ANY)
        out_specs.append(_ANY)
        out_shape.append(jax.ShapeDtypeStruct((4,) + ride.shape, ride.dtype))
        scratch += [pltpu.SemaphoreType.DMA((6,)), pltpu.SemaphoreType.DMA((6,))]
        args.append(ride)
    return pl.pallas_call(
        body, name=name, grid=(ni, nj), in_specs=in_specs, out_specs=out_specs, out_shape=out_shape,
        scratch_shapes=scratch,
        compiler_params=_cparams(("parallel", "arbitrary") if ride is None else ("arbitrary", "arbitrary")),
    )(*args)


def _matmul_t(at, b, name):
    M, S = at.shape
    N = b.shape[1]
    tn = _wide_tile(N)
    ts = _pick(S, (512, 256))

    def body(a_ref, b_ref, o_ref):
        @pl.when(pl.program_id(1) == 0)
        def _():
            o_ref[...] = jnp.zeros_like(o_ref)
        o_ref[...] += _dot(a_ref[...], b_ref[...].astype(BF16))

    return pl.pallas_call(
        body, name=name, grid=(N // tn, S // ts),
        in_specs=[pl.BlockSpec((M, ts), lambda j, k: (0, k)),
                  pl.BlockSpec((ts, tn), lambda j, k: (k, j))],
        out_specs=pl.BlockSpec((M, tn), lambda j, k: (0, j)),
        out_shape=jax.ShapeDtypeStruct((M, N), F32),
        compiler_params=_cparams(("parallel", "arbitrary")),
    )(at, b)


def _matmul_t_many(at, bs, name):
    M, S = at.shape
    ts = _pick(S, (512, 256))
    n = len(bs)

    def body(*refs):
        a_ref, b_refs, o_refs = refs[0], refs[1:1 + n], refs[1 + n:]

        @pl.when(pl.program_id(0) == 0)
        def _():
            for o_ref in o_refs:
                o_ref[...] = jnp.zeros_like(o_ref)

        a = a_ref[...]
        for b_ref, o_ref in zip(b_refs, o_refs):
            o_ref[...] += _dot(a, b_ref[...].astype(BF16))

    return pl.pallas_call(
        body, name=name, grid=(S // ts,),
        in_specs=[pl.BlockSpec((M, ts), lambda k: (0, k))] + [pl.BlockSpec((ts, b.shape[1]), lambda k: (k, 0)) for b in bs],
        out_specs=[pl.BlockSpec((M, b.shape[1]), lambda k: (0, 0)) for b in bs],
        out_shape=[jax.ShapeDtypeStruct((M, b.shape[1]), F32) for b in bs],
        compiler_params=_cparams(("arbitrary",)),
    )(at, *bs)


def _in_proj_bwd(pieces, w, x, g, dx_up, name, ride=None):
    S, K = x.shape
    N = w.shape[1]
    tm = _pick(S, (256,))
    nsteps = S // tm
    offs = [off for off, _ in pieces]
    arrs = [a for _, a in pieces]

    def body(*refs):
        d_refs = refs[:len(arrs)]
        if ride is None:
            w_ref, x_ref, g_ref, u_ref, dx_ref, dg_ref = refs[len(arrs):]
        else:
            w_ref, x_ref, g_ref, u_ref, q_ref, dx_ref, dg_ref, slots_ref, send_sems, recv_sems = refs[len(arrs):]
            begin, finish = _exchange_phases([q_ref], [slots_ref], send_sems, recv_sems)
            pl.when(pl.program_id(0) == 0)(begin)

        @pl.when(pl.program_id(0) == 0)
        def _():
            dg_ref[...] = jnp.zeros_like(dg_ref)

        acc = None
        for off, d_ref in zip(offs, d_refs):
            part = _dot_nt(d_ref[...].astype(BF16), w_ref[:, off:off + d_ref.shape[1]])
            acc = part if acc is None else acc + part
        dx, dgrow = _norm_bwd(x_ref[...], g_ref[...], acc)
        dx_ref[...] = u_ref[...] + dx
        dg_ref[...] += jnp.sum(dgrow, axis=0, keepdims=True)
        if ride is not None:
            pl.when(pl.program_id(0) == nsteps - 1)(finish)

    row = lambda i: (i, 0)
    fixed = lambda i: (0, 0)
    in_specs = [pl.BlockSpec((tm, a.shape[1]), row) for a in arrs] + [
        pl.BlockSpec((K, N), fixed), pl.BlockSpec((tm, K), row), pl.BlockSpec((1, K), fixed), pl.BlockSpec((tm, K), row)]
    out_specs = [pl.BlockSpec((tm, K), row), pl.BlockSpec((1, K), fixed)]
    out_shape = [jax.ShapeDtypeStruct((S, K), F32), jax.ShapeDtypeStruct((1, K), F32)]
    args = [*arrs, w, x, g, dx_up]
    scratch = []
    if ride is not None:
        in_specs.append(_ANY)
        out_specs.append(_ANY)
        out_shape.append(jax.ShapeDtypeStruct(ride.shape, ride.dtype))
        scratch = [pltpu.SemaphoreType.DMA((3,)), pltpu.SemaphoreType.DMA((3,))]
        args.append(ride)
    return pl.pallas_call(
        body, name=name, grid=(nsteps,), in_specs=in_specs, out_specs=out_specs, out_shape=out_shape,
        scratch_shapes=scratch, compiler_params=_cparams(("arbitrary",)),
    )(*args)


def _out_proj(og_a, og_b, blk_a, blk_b, w, x, g, target, name):
    S = x.shape[0]
    D = x.shape[1]
    tm = _pick(S, (512, 256))
    with_loss = target is not None

    def body(*refs):
        if with_loss:
            a_ref, b_ref, wa_ref, wb_ref, x_ref, g_ref, t_ref, y_ref, o_ref, l_ref = refs
        else:
            a_ref, b_ref, wa_ref, wb_ref, x_ref, g_ref, y_ref, o_ref = refs
        y = _dot(a_ref[...], wa_ref[...]) + _dot(b_ref[...], wb_ref[...])
        y_ref[...] = y
        xn = x_ref[...] + (y * _rstd(y)) * g_ref[...]
        if with_loss:
            @pl.when(pl.program_id(0) == 0)
            def _():
                l_ref[...] = jnp.zeros_like(l_ref)
            d = xn - t_ref[...]
            o_ref[...] = d / float(D)
            l_ref[...] += jnp.sum(d * d, axis=0, keepdims=True)
        else:
            o_ref[...] = xn

    row = lambda i: (i, 0)
    in_specs = [pl.BlockSpec((tm, 512), lambda i: (i, blk_a)),
                pl.BlockSpec((tm, 512), lambda i: (i, blk_b)),
                pl.BlockSpec((512, D), lambda i: (0, 0)),
                pl.BlockSpec((512, D), lambda i: (1, 0)),
                pl.BlockSpec((tm, D), row),
                pl.BlockSpec((1, D), lambda i: (0, 0))]
    out_specs = [pl.BlockSpec((tm, D), row), pl.BlockSpec((tm, D), row)]
    out_shape = [jax.ShapeDtypeStruct((S, D), F32), jax.ShapeDtypeStruct((S, D), F32)]
    args = [og_a, og_b, w, w, x, g]
    if with_loss:
        in_specs.append(pl.BlockSpec((tm, D), row))
        out_specs.append(pl.BlockSpec((1, D), lambda i: (0, 0)))
        out_shape.append(jax.ShapeDtypeStruct((1, D), F32))
        args.append(target)
    return pl.pallas_call(
        body, name=name, grid=(S // tm,), in_specs=in_specs, out_specs=out_specs, out_shape=out_shape,
        compiler_params=_cparams(("arbitrary",)),
    )(*args)


def _out_proj_bwd(dx_up, y, g, w, proj, gate_offs, o_a, o_b, oblk_a, oblk_b, name):
    S, D = y.shape
    tm = _pick(S, (256,))
    gblk = [off // 256 + c for off in gate_offs for c in range(2)]

    def body(u_ref, y_ref, g_ref, w_ref, g0, g1, g2, g3, oa_ref, ob_ref, dy_ref, do_ref, dgate_ref, dg_ref):
        @pl.when(pl.program_id(0) == 0)
        def _():
            dg_ref[...] = jnp.zeros_like(dg_ref)
        dy, dgrow = _norm_bwd(y_ref[...], g_ref[...], u_ref[...])
        dg_ref[...] += jnp.sum(dgrow, axis=0, keepdims=True)
        dyb = dy.astype(BF16)
        dy_ref[...] = dyb
        dog = _dot_nt(dyb, w_ref[...])
        gates = (g0, g1, g2, g3)
        for c in range(4):
            gt = gates[c][...]
            sg = _sigmoid(gt)
            o_ref = oa_ref if c < 2 else ob_ref
            ov = o_ref[:, (c % 2) * 256:(c % 2 + 1) * 256]
            dc = dog[:, c * 256:(c + 1) * 256]
            do_ref[:, c * 256:(c + 1) * 256] = dc * (gt * sg)
            dgate_ref[:, c * 256:(c + 1) * 256] = dc * ov * (sg * (1.0 + gt * (1.0 - sg)))

    row = lambda i: (i, 0)
    gspec = lambda c: pl.BlockSpec((tm, 256), lambda i: (i, gblk[c]))
    return pl.pallas_call(
        body, name=name, grid=(S // tm,),
        in_specs=[pl.BlockSpec((tm, D), row), pl.BlockSpec((tm, D), row), pl.BlockSpec((1, D), lambda i: (0, 0)),
                  pl.BlockSpec((D, D), lambda i: (0, 0)),
                  gspec(0), gspec(1), gspec(2), gspec(3),
                  pl.BlockSpec((tm, 512), lambda i: (i, oblk_a)),
                  pl.BlockSpec((tm, 512), lambda i: (i, oblk_b))],
        out_specs=[pl.BlockSpec((tm, D), row), pl.BlockSpec((tm, D), row), pl.BlockSpec((tm, D), row),
                   pl.BlockSpec((1, D), lambda i: (0, 0))],
        out_shape=[jax.ShapeDtypeStruct((S, D), BF16), jax.ShapeDtypeStruct((S, D), F32),
                   jax.ShapeDtypeStruct((S, D), F32), jax.ShapeDtypeStruct((1, D), F32)],
        compiler_params=_cparams(("arbitrary",)),
    )(dx_up, y, g, w, proj, proj, proj, proj, o_a, o_b)


def _rope_tables(pos, invf, name):
    S = pos.shape[0]
    tm = _pick(S, (512, 256))

    def body(p_ref, f_ref, c_ref, s1_ref, s2_ref):
        lane = lax.broadcasted_iota(jnp.int32, (1, LANES), 1)
        ang = p_ref[...].astype(F32) * f_ref[...]
        c, s = jnp.cos(ang), jnp.sin(ang)
        c_ref[...] = jnp.where((lane >= 64) & (lane < 96), c, 1.0)
        s1_ref[...] = jnp.where((lane >= 64) & (lane < 80), -s, 0.0)
        s2_ref[...] = jnp.where((lane >= 80) & (lane < 96), s, 0.0)

    spec = pl.BlockSpec((tm, LANES), lambda i: (i, 0))
    return pl.pallas_call(
        body, name=name, grid=(S // tm,),
        in_specs=[pl.BlockSpec((tm, 1), lambda i: (i, 0)), pl.BlockSpec((1, LANES), lambda i: (0, 0))],
        out_specs=[spec, spec, spec],
        out_shape=[jax.ShapeDtypeStruct((S, LANES), F32)] * 3,
        compiler_params=_cparams(("parallel",)),
    )(pos, invf)


def _rope(x, c, s1, s2):
    return x * c + pltpu.roll(x, LANES - 16, 1) * s1 + pltpu.roll(x, 16, 1) * s2


def _rope_t(d, c, s1, s2):
    return d * c + pltpu.roll(d * s1, 16, 1) + pltpu.roll(d * s2, LANES - 16, 1)


def _mla_prep(proj, gq, gkv, wq, wkv, cosT, s1T, s2T, name):
    S = proj.shape[0]
    tm = _pick(S, (256,))

    def body(p_ref, gq_ref, gkv_ref, wq_ref, wkv_ref, c_ref, s1_ref, s2_ref, q_ref, k_ref, v_ref, qn_ref, cn_ref):
        qa = p_ref[:, 0:384]
        ckv = p_ref[:, 384:640]
        kr = p_ref[:, 640:768]
        qn32 = (qa * _rstd(qa)) * gq_ref[...]
        cn32 = (ckv * _rstd(ckv)) * gkv_ref[...]
        qn = qn32.astype(BF16)
        cn = cn32.astype(BF16)
        qn_ref[...] = qn32.T.astype(BF16)
        cn_ref[...] = cn32.T.astype(BF16)
        qb = _dot(qn, wq_ref[...])
        kvb = _dot(cn, wkv_ref[...])
        c, s1, s2 = c_ref[...], s1_ref[...], s2_ref[...]
        krr = _rope(kr, c, s1, s2)
        for h in range(8):
            sl = slice(h * LANES, (h + 1) * LANES)
            q_ref[:, sl] = _rope(qb[:, sl], c, s1, s2)
            k_ref[:, sl] = kvb[:, sl] + krr
        v_ref[...] = kvb[:, 1024:1536]

    row = lambda i: (i, 0)
    fixed = lambda i: (0, 0)
    tspec = pl.BlockSpec((tm, LANES), row)
    return pl.pallas_call(
        body, name=name, grid=(S // tm,),
        in_specs=[pl.BlockSpec((tm, L0_PREP_W), lambda i: (i, L0_PREP // L0_PREP_W)),
                  pl.BlockSpec((1, 384), fixed), pl.BlockSpec((1, 256), fixed),
                  pl.BlockSpec((384, 1024), fixed), pl.BlockSpec((256, 1536), fixed), tspec, tspec, tspec],
        out_specs=[pl.BlockSpec((tm, 1024), row), pl.BlockSpec((tm, 1024), row), pl.BlockSpec((tm, 512), row),
                   pl.BlockSpec((384, tm), lambda i: (0, i)), pl.BlockSpec((256, tm), lambda i: (0, i))],
        out_shape=[jax.ShapeDtypeStruct((S, 1024), F32), jax.ShapeDtypeStruct((S, 1024), F32),
                   jax.ShapeDtypeStruct((S, 512), F32), jax.ShapeDtypeStruct((384, S), BF16),
                   jax.ShapeDtypeStruct((256, S), BF16)],
        compiler_params=_cparams(("parallel",)),
    )(proj, gq, gkv, wq, wkv, cosT, s1T, s2T)


def _mla_prep_bwd(dq, dk, dv, proj, gq, gkv, wq, wkv, cosT, s1T, s2T, name):
    S = proj.shape[0]
    tm = _pick(S, (256,))

    def body(dq_ref, dk_ref, dv_ref, p_ref, gq_ref, gkv_ref, wq_ref, wkv_ref, c_ref, s1_ref, s2_ref,
             dp_ref, dqb_ref, dkvb_ref, dgq_ref, dgkv_ref):
        @pl.when(pl.program_id(0) == 0)
        def _():
            dgq_ref[...] = jnp.zeros_like(dgq_ref)
            dgkv_ref[...] = jnp.zeros_like(dgkv_ref)
        c, s1, s2 = c_ref[...], s1_ref[...], s2_ref[...]
        lane = lax.broadcasted_iota(jnp.int32, (1, LANES), 1)
        dkr = jnp.zeros((tm, LANES), F32)
        for h in range(8):
            sl = slice(h * LANES, (h + 1) * LANES)
            dqb_ref[:, sl] = _rope_t(dq_ref[:, sl], c, s1, s2).astype(BF16)
            dkh = dk_ref[:, sl]
            dkvb_ref[:, sl] = dkh.astype(BF16)
            dkr = dkr + dkh
        dkvb_ref[:, 1024:1536] = dv_ref[...].astype(BF16)
        dkr = jnp.where((lane >= 64) & (lane < 96), _rope_t(dkr, c, s1, s2), 0.0)
        dqn = _dot_nt(dqb_ref[...], wq_ref[...])
        dcn = _dot_nt(dkvb_ref[...], wkv_ref[...])
        dqa, gq_row = _norm_bwd(p_ref[:, 0:384], gq_ref[...], dqn)
        dckv, gkv_row = _norm_bwd(p_ref[:, 384:640], gkv_ref[...], dcn)
        dp_ref[:, 0:384] = dqa
        dp_ref[:, 384:640] = dckv
        dp_ref[:, 640:768] = dkr
        dgq_ref[...] += jnp.sum(gq_row, axis=0, keepdims=True)
        dgkv_ref[...] += jnp.sum(gkv_row, axis=0, keepdims=True)

    row = lambda i: (i, 0)
    fixed = lambda i: (0, 0)
    tspec = pl.BlockSpec((tm, LANES), row)
    return pl.pallas_call(
        body, name=name, grid=(S // tm,),
        in_specs=[pl.BlockSpec((tm, 1024), row), pl.BlockSpec((tm, 1024), row), pl.BlockSpec((tm, 512), row),
                  pl.BlockSpec((tm, L0_PREP_W), lambda i: (i, L0_PREP // L0_PREP_W)),
                  pl.BlockSpec((1, 384), fixed), pl.BlockSpec((1, 256), fixed),
                  pl.BlockSpec((384, 1024), fixed), pl.BlockSpec((256, 1536), fixed), tspec, tspec, tspec],
        out_specs=[pl.BlockSpec((tm, L0_PREP_W), row), pl.BlockSpec((tm, 1024), row), pl.BlockSpec((tm, 1536), row),
                   pl.BlockSpec((1, 384), fixed), pl.BlockSpec((1, 256), fixed)],
        out_shape=[jax.ShapeDtypeStruct((S, L0_PREP_W), F32), jax.ShapeDtypeStruct((S, 1024), BF16),
                   jax.ShapeDtypeStruct((S, 1536), BF16), jax.ShapeDtypeStruct((1, 384), F32),
                   jax.ShapeDtypeStruct((1, 256), F32)],
        compiler_params=_cparams(("arbitrary",)),
    )(dq, dk, dv, proj, gq, gkv, wq, wkv, cosT, s1T, s2T)


def _fox_prep(proj, bf, name):
    S = proj.shape[0]
    tm = _pick(S, (256,))

    def body(f_ref, b_ref, c_ref, carry_ref):
        @pl.when(pl.program_id(0) == 0)
        def _():
            carry_ref[...] = jnp.zeros_like(carry_ref)
        u = f_ref[...] + b_ref[...]
        lf = jnp.minimum(u, 0.0) - jnp.log(1.0 + jnp.exp(-jnp.abs(u)))
        r = lax.broadcasted_iota(jnp.int32, (tm, tm), 0)
        cidx = lax.broadcasted_iota(jnp.int32, (tm, tm), 1)
        tri = (cidx <= r).astype(BF16)
        hi, mid, lo = _split3(lf)
        c = carry_ref[...] + (_dot(tri, hi) + _dot(tri, mid) + _dot(tri, lo))
        c_ref[...] = c
        carry_ref[...] = c[tm - 1:tm, :]

    return pl.pallas_call(
        body, name=name, grid=(S // tm,),
        in_specs=[pl.BlockSpec((tm, LANES), lambda i: (i, L1_F // LANES)), pl.BlockSpec((1, LANES), lambda i: (0, 0))],
        out_specs=pl.BlockSpec((tm, LANES), lambda i: (i, 0)),
        out_shape=jax.ShapeDtypeStruct((S, LANES), F32),
        scratch_shapes=[pltpu.VMEM((1, LANES), F32)],
        compiler_params=_cparams(("arbitrary",)),
    )(proj, bf)


def _fox_prep_bwd(dc, proj, bf, name):
    S = proj.shape[0]
    tm = _pick(S, (256,))
    nb = S // tm

    def body(dc_ref, f_ref, b_ref, df_ref, db_ref, carry_ref):
        @pl.when(pl.program_id(0) == 0)
        def _():
            carry_ref[...] = jnp.zeros_like(carry_ref)
            db_ref[...] = jnp.zeros_like(db_ref)
        r = lax.broadcasted_iota(jnp.int32, (tm, tm), 0)
        cidx = lax.broadcasted_iota(jnp.int32, (tm, tm), 1)
        tri = (cidx >= r).astype(BF16)
        hi, mid, lo = _split3(dc_ref[...])
        dlf = carry_ref[...] + (_dot(tri, hi) + _dot(tri, mid) + _dot(tri, lo))
        carry_ref[...] = dlf[0:1, :]
        u = f_ref[...] + b_ref[...]
        e = jnp.exp(-jnp.abs(u))
        sneg = jnp.where(u >= 0.0, e, 1.0) / (1.0 + e)
        lane = lax.broadcasted_iota(jnp.int32, (1, LANES), 1)
        df = jnp.where(lane < FOX_HEADS, dlf * sneg, 0.0)
        df_ref[...] = df
        db_ref[...] += jnp.sum(df, axis=0, keepdims=True)

    return pl.pallas_call(
        body, name=name, grid=(nb,),
        in_specs=[pl.BlockSpec((tm, LANES), lambda i: (nb - 1 - i, 0)),
                  pl.BlockSpec((tm, LANES), lambda i: (nb - 1 - i, L1_F // LANES)),
                  pl.BlockSpec((1, LANES), lambda i: (0, 0))],
        out_specs=[pl.BlockSpec((tm, LANES), lambda i: (nb - 1 - i, 0)), pl.BlockSpec((1, LANES), lambda i: (0, 0))],
        out_shape=[jax.ShapeDtypeStruct((S, LANES), F32), jax.ShapeDtypeStruct((1, LANES), F32)],
        scratch_shapes=[pltpu.VMEM((1, LANES), F32)],
        compiler_params=_cparams(("arbitrary",)),
    )(dc, proj, bf)


def _att_specs(kind, S, T):
    if kind == "sb":
        qo, ko, vo, go = L0_SBQ // LANES, L0_SBK // LANES, L0_SBV // LANES, L0_SBG // LANES
    elif kind == "fox":
        qo, ko, vo, go = L1_Q // LANES, L1_K // LANES, L1_V // LANES, L1_G // LANES
    else:
        go = L0_MLG // LANES
        return (pl.BlockSpec((T, 256), lambda p, i: (i, p)), pl.BlockSpec((S, 256), lambda p, i: (0, p)),
                pl.BlockSpec((S, LANES), lambda p, i: (0, p)), pl.BlockSpec((T, LANES), lambda p, i: (i, go + p)))
    return (pl.BlockSpec((T, LANES), lambda p, i: (i, qo + p)), pl.BlockSpec((S, LANES), lambda p, i: (0, ko + p)),
            pl.BlockSpec((S, LANES), lambda p, i: (0, vo + p)), pl.BlockSpec((T, LANES), lambda p, i: (i, go + p)))


def _per_q_tile(tile_body, hows):
    T = ATT_T

    def view(ref, u, how):
        if how == "rows":
            return ref.at[pl.ds(u * T, T)]
        if how == "lanes":
            return ref.at[:, pl.ds(u * T, T)]
        if how == "stat":
            return ref.at[:, u]
        return ref

    def body(*refs):
        for u in range(ATT_QSUB):
            tile_body(pl.program_id(1) * ATT_QSUB + u, *[view(r, u, how) for r, how in zip(refs, hows)])

    return body


def _mask_flags(js, masked_at):
    return [t == masked_at for t in range(len(js))]


def _loop_tiles(i, tiles, right_to_left, G=ATT_GROUP):
    ng = i // G
    rest = i - ng * G

    def leftover():
        for r in range(G):
            @pl.when(rest == r)
            def _():
                if right_to_left:
                    tiles([i - u for u in range(r + 1)], 0)
                else:
                    tiles([ng * G + u for u in range(r + 1)], r)

    def group(g, carry):
        if right_to_left:
            tiles([ng * G - 1 - (g * G + u) for u in range(G)], None)
        else:
            tiles([g * G + u for u in range(G)], None)
        return carry

    if right_to_left:
        leftover()
    lax.fori_loop(0, ng, group, 0)
    if not right_to_left:
        leftover()


def _head_q(kind, q_ref, m0, scale):
    if kind == "mla":
        return [q_ref[:, 0:LANES].astype(BF16), q_ref[:, LANES:2 * LANES].astype(BF16)]
    qv = q_ref[...] * scale
    return [jnp.where(m0, qv, 0.0).astype(BF16), jnp.where(m0, 0.0, qv).astype(BF16)]


def _head_k(kind, k_ref, start, T):
    if kind == "mla":
        return [k_ref[pl.ds(start, T), 0:LANES].astype(BF16), k_ref[pl.ds(start, T), LANES:2 * LANES].astype(BF16)]
    kb = k_ref[pl.ds(start, T), :].astype(BF16)
    return [kb, kb]


def _softmax_fwd(kind, qkvg, c_col, S, npairs, name):
    T = ATT_T
    nq = S // T
    fox = kind == "fox"
    scale = (96 if kind == "mla" else 64) ** -0.5

    def body(i, *refs):
        if fox:
            q_ref, k_ref, v_ref, g_ref, cc_ref, o_ref, og_ref, ogt_ref, st_ref, m_ref, acc_ref = refs
        else:
            q_ref, k_ref, v_ref, g_ref, o_ref, og_ref, ogt_ref, st_ref, m_ref, acc_ref = refs
        m0 = lax.broadcasted_iota(jnp.int32, (1, LANES), 1) < 64
        top = lax.broadcasted_iota(jnp.int32, (LANES, 1), 0) < 64
        key = lax.broadcasted_iota(jnp.int32, (T, LANES), 0)
        qrow = lax.broadcasted_iota(jnp.int32, (T, LANES), 1)
        qh = _head_q(kind, q_ref, m0, scale)
        m_ref[...] = jnp.full(m_ref.shape, NEG, F32)
        acc_ref[...] = jnp.zeros(acc_ref.shape, F32)
        chains = [(h, b) for h in range(2) for b in range(T // LANES)]

        def tiles(js, masked_at):
            starts = [pl.multiple_of(j * T, T) for j in js]
            zss = []
            for start in starts:
                kh = _head_k(kind, k_ref, start, T)
                zss.append(_split_blocks([_dot_nt(kh[h], qh[h]) for h in range(2)]))
            pss, alss = [], []
            for start, zs, masked in zip(starts, zss, _mask_flags(js, masked_at)):
                ps, alphas = [], []
                for (h, b), z in zip(chains, zs):
                    lanes = slice(b * LANES, (b + 1) * LANES)
                    if kind == "mla":
                        z = z * scale
                    if fox:
                        z = z - cc_ref[h, pl.ds(start, T), :]
                    if masked:
                        z = jnp.where(key <= qrow + b * LANES, z, NEG)
                    m_prev = m_ref[h, :, lanes]
                    m_new = jnp.maximum(m_prev, jnp.max(z, axis=0, keepdims=True))
                    alphas.append(jnp.exp(m_prev - m_new))
                    ps.append(jnp.exp(z - m_new).astype(BF16))
                    m_ref[h, :, lanes] = m_new
                pss.append(_join_blocks(ps, T // LANES))
                alss.append(_join_blocks(alphas, T // LANES))
            for start, ps, alphas in zip(starts, pss, alss):
                vt = v_ref[pl.ds(start, T), :].T
                vth = [jnp.where(top, vt, 1.0).astype(BF16), jnp.where(top, 1.0, vt).astype(BF16)]
                for h in range(2):
                    acc_ref[h] = alphas[h] * acc_ref[h] + _dot(vth[h], ps[h])

        _loop_tiles(i, tiles, False, 2 * ATT_GROUP)
        acc = [acc_ref[0], acc_ref[1]]
        ot = jnp.concatenate([acc[0][0:64] / acc[0][64:128], acc[1][64:128] / acc[1][0:64]], axis=0)
        o = ot.T
        o_ref[...] = o
        gt = g_ref[...]
        og = o * (gt * _sigmoid(gt))
        og_ref[...] = og.astype(BF16)
        ogt_ref[...] = og.T.astype(BF16)
        st_ref[0] = m_ref[0] + jnp.log(acc[0][64:65])
        st_ref[1] = m_ref[1] + jnp.log(acc[1][0:1])

    QT = ATT_QSUB * T
    qs, ks, vs, gs = _att_specs(kind, S, QT)
    in_specs = [qs, ks, vs, gs]
    args = list(qkvg)
    hows = ["rows", None, None, "rows"]
    if fox:
        in_specs += [pl.BlockSpec((2, S, LANES), lambda p, i: (p, 0, 0))]
        args += [c_col]
        hows += [None]
    hows += ["rows", "rows", "lanes", "stat", None, None]
    W = npairs * LANES
    return pl.pallas_call(
        _per_q_tile(body, hows), name=name, grid=(npairs, nq // ATT_QSUB), in_specs=in_specs,
        out_specs=[pl.BlockSpec((QT, LANES), lambda p, i: (i, p)), pl.BlockSpec((QT, LANES), lambda p, i: (i, p)),
                   pl.BlockSpec((LANES, QT), lambda p, i: (p, i)),
                   pl.BlockSpec((2, ATT_QSUB, 1, T), lambda p, i: (p, i, 0, 0))],
        out_shape=[jax.ShapeDtypeStruct((S, W), F32), jax.ShapeDtypeStruct((S, W), BF16),
                   jax.ShapeDtypeStruct((W, S), BF16),
                   jax.ShapeDtypeStruct((2 * npairs, nq, 1, T), F32)],
        scratch_shapes=[pltpu.VMEM((2, 1, T), F32), pltpu.VMEM((2, LANES, T), F32)],
        compiler_params=_cparams(("parallel", "parallel")),
    )(*args)


def _softplus_parts(z):
    sp = jnp.maximum(z, 0.0) + jnp.log(1.0 + jnp.exp(-jnp.abs(z)))
    return sp, z - sp


def _cumsum_dot(tri2, his, los):
    return _split_blocks([_dot(tri2, jnp.concatenate([hi, lo], axis=0)) for hi, lo in zip(his, los)])


def _split2(x):
    hi = x.astype(BF16)
    return hi, (x - hi.astype(F32)).astype(BF16)


def _split_blocks(per_head):
    return [x[:, b * LANES:(b + 1) * LANES] for x in per_head for b in range(x.shape[1] // LANES)]


def _join_blocks(per_block, nb):
    return [jnp.concatenate(per_block[h * nb:(h + 1) * nb], axis=1) for h in range(len(per_block) // nb)]


def _row_of(col):
    return jnp.broadcast_to(col, (col.shape[0], LANES)).T[0:1]


def _softmax_bwd_t(kind, q, k, v, do, do_off, o, lse, c_col, S, npairs, name):
    T = ATT_T
    nq = S // T
    nb = T // LANES
    fox = kind == "fox"
    mla = kind == "mla"
    scale = (96 if mla else 64) ** -0.5
    kw = 256 if mla else LANES

    def body(i, *refs):
        if fox:
            (q_ref, k_ref, v_ref, do_ref, o_ref, st_ref, cc_ref,
             dq_ref, dk_ref, dv_ref, dck_ref, dcq_ref, dqt_ref, rs_ref, dkx_ref) = refs
        else:
            q_ref, k_ref, v_ref, do_ref, o_ref, st_ref, dq_ref, dk_ref, dv_ref, dqt_ref = refs

        @pl.when(i == 0)
        def _():
            dv_ref[...] = jnp.zeros_like(dv_ref)
            if fox:
                dkx_ref[...] = jnp.zeros_like(dkx_ref)
            else:
                dk_ref[...] = jnp.zeros_like(dk_ref)

        m0 = lax.broadcasted_iota(jnp.int32, (1, LANES), 1) < 64
        top = lax.broadcasted_iota(jnp.int32, (LANES, 1), 0) < 64
        key = lax.broadcasted_iota(jnp.int32, (T, LANES), 0)
        qrow = lax.broadcasted_iota(jnp.int32, (T, LANES), 1)
        qh = _head_q(kind, q_ref, m0, scale)
        if fox:
            qv = q_ref[...] * scale
            qk = [jnp.where(m0, qv, 1.0).astype(BF16), jnp.where(m0, 1.0, qv).astype(BF16)]
        else:
            qk = qh
        dov = do_ref[...]
        prod = dov * o_ref[...]
        dd = [_row_of(jnp.sum(jnp.where(m0, prod, 0.0), axis=1, keepdims=True)),
              _row_of(jnp.sum(jnp.where(m0, 0.0, prod), axis=1, keepdims=True))]
        doh = [jnp.where(m0, dov, 0.0).astype(BF16), jnp.where(m0, 0.0, dov).astype(BF16)]
        lse = [st_ref[0], st_ref[1]]
        dqt_ref[...] = jnp.zeros_like(dqt_ref)
        if fox:
            rs_ref[...] = jnp.zeros_like(rs_ref)
        chains = [(h, b) for h in range(2) for b in range(nb)]

        def tiles(js, masked_at):
            starts = [pl.multiple_of(j * T, T) for j in js]
            zss, dpss = [], []
            for start in starts:
                vb = v_ref[pl.ds(start, T), :].astype(BF16)
                kh = _head_k(kind, k_ref, start, T)
                zss.append(_split_blocks([_dot_nt(kh[h], qh[h]) for h in range(2)]))
                dpss.append(_split_blocks([_dot_nt(vb, doh[h]) for h in range(2)]))
            pss, dsss = [], []
            for start, zs, dps, masked in zip(starts, zss, dpss, _mask_flags(js, masked_at)):
                ps, dss = [], []
                for (h, b), z, dp in zip(chains, zs, dps):
                    lanes = slice(b * LANES, (b + 1) * LANES)
                    if mla:
                        z = z * scale
                    if fox:
                        z = z - cc_ref[h, pl.ds(start, T), :]
                    if masked:
                        z = jnp.where(key <= qrow + b * LANES, z, NEG)
                    p = jnp.exp(z - lse[h][:, lanes])
                    ds = p * (dp - dd[h][:, lanes])
                    dsb = ds.astype(BF16)
                    if fox:
                        rs_ref[h, :, lanes] += jnp.sum(dsb.astype(F32), axis=0, keepdims=True)
                    ps.append(p.astype(BF16))
                    dss.append(dsb)
                pss.append(_join_blocks(ps, nb))
                dsss.append(_join_blocks(dss, nb))
            for start, ps, dss in zip(starts, pss, dsss):
                kt = k_ref[pl.ds(start, T), :].T.astype(BF16)
                dvc = None
                for h in range(2):
                    dkh = _dot(dss[h], qk[h])
                    dvh = _dot(ps[h], doh[h])
                    dvc = dvh if dvc is None else dvc + dvh
                    kth = kt[h * LANES:(h + 1) * LANES] if mla else kt
                    dqt_ref[h] += _dot(kth, dss[h])
                    if fox:
                        dkx_ref[h, pl.ds(start, T), :] += dkh
                    elif mla:
                        dk_ref[pl.ds(start, T), h * LANES:(h + 1) * LANES] += dkh * scale
                    else:
                        dk_ref[pl.ds(start, T), :] += dkh
                dv_ref[pl.ds(start, T), :] += dvc

        _loop_tiles(i, tiles, False)
        if mla:
            dq_ref[:, 0:LANES] = dqt_ref[0].T * scale
            dq_ref[:, LANES:2 * LANES] = dqt_ref[1].T * scale
        else:
            dq_ref[...] = jnp.where(top, dqt_ref[0], dqt_ref[1]).T * scale
        if fox:
            dcq_ref[0] = rs_ref[0]
            dcq_ref[1] = rs_ref[1]

            @pl.when(i == nq - 1)
            def _():
                dk_ref[...] = jnp.where(m0, dkx_ref[0], dkx_ref[1])
                dck_ref[0] = dkx_ref[0].T[64:65]
                dck_ref[1] = dkx_ref[1].T[0:1]

    QT = ATT_QSUB * T
    qs, ks, vs, _ = _att_specs(kind, S, QT)
    stat = pl.BlockSpec((2, ATT_QSUB, 1, T), lambda p, i: (p, i, 0, 0))
    in_specs = [qs, ks, vs,
                pl.BlockSpec((QT, LANES), lambda p, i: (i, do_off + p)),
                pl.BlockSpec((QT, LANES), lambda p, i: (i, p)), stat]
    args = [q, k, v, do, o, lse]
    hows = ["rows", None, None, "rows", "rows", "stat"]
    W = npairs * LANES
    out_specs = [pl.BlockSpec((QT, kw), lambda p, i: (i, p)), pl.BlockSpec((S, kw), lambda p, i: (0, p)),
                 pl.BlockSpec((S, LANES), lambda p, i: (0, p))]
    out_shape = [jax.ShapeDtypeStruct((S, npairs * kw), F32), jax.ShapeDtypeStruct((S, npairs * kw), F32),
                 jax.ShapeDtypeStruct((S, W), F32)]
    scratch = [pltpu.VMEM((2, LANES, T), F32)]
    if fox:
        in_specs.append(pl.BlockSpec((2, S, LANES), lambda p, i: (p, 0, 0)))
        args.append(c_col)
        out_specs += [pl.BlockSpec((2, 1, S), lambda p, i: (p, 0, 0)), stat]
        out_shape += [jax.ShapeDtypeStruct((2 * npairs, 1, S), F32), jax.ShapeDtypeStruct((2 * npairs, nq, 1, T), F32)]
        scratch += [pltpu.VMEM((2, 1, T), F32), pltpu.VMEM((2, S, LANES), F32)]
        hows += [None, "rows", None, None, None, "stat", None, None, None]
    else:
        hows += ["rows", None, None, None]
    return pl.pallas_call(
        _per_q_tile(body, hows), name=name, grid=(npairs, nq // ATT_QSUB), in_specs=in_specs, out_specs=out_specs,
        out_shape=out_shape, scratch_shapes=scratch, compiler_params=_cparams(("parallel", "arbitrary")),
    )(*args)


def _sb_fwd_t(proj, S, npairs, name):
    T = ATT_T
    nq = S // T
    nb = T // LANES
    scale = 64 ** -0.5

    def body(i, q_ref, k_ref, v_ref, g_ref, o_ref, og_ref, ogt_ref, st_ref, rem_ref, acc_ref):
        m0 = lax.broadcasted_iota(jnp.int32, (1, LANES), 1) < 64
        top = lax.broadcasted_iota(jnp.int32, (LANES, 1), 0) < 64
        key = lax.broadcasted_iota(jnp.int32, (T, LANES), 0)
        qrow = lax.broadcasted_iota(jnp.int32, (T, LANES), 1)
        r = lax.broadcasted_iota(jnp.int32, (T, T), 0)
        c = lax.broadcasted_iota(jnp.int32, (T, T), 1)
        after = (c > r).astype(BF16)
        after2 = jnp.concatenate([after, after], axis=1)
        qh = _head_q("sb", q_ref, m0, scale)
        rem_ref[...] = jnp.zeros_like(rem_ref)
        acc_ref[...] = jnp.zeros_like(acc_ref)
        chains = [(h, b) for h in range(2) for b in range(nb)]

        def tiles(js, masked_at):
            zss = []
            for j in js:
                kb = k_ref[pl.ds(pl.multiple_of(j * T, T), T), :].astype(BF16)
                zss.append(_split_blocks([_dot_nt(kb, qh[h]) for h in range(2)]))
            lass, sums, hiss, loss = [], [], [], []
            for zs, masked in zip(zss, _mask_flags(js, masked_at)):
                las, sm, his, los = [], [], [], []
                for (h, b), z in zip(chains, zs):
                    sp, la = _softplus_parts(z)
                    if masked:
                        sp = jnp.where(key < qrow + b * LANES, sp, 0.0)
                    hi, lo = _split2(sp)
                    las.append(la)
                    sm.append(jnp.sum(sp, axis=0, keepdims=True))
                    his.append(hi)
                    los.append(lo)
                lass.append(las)
                sums.append(sm)
                hiss.append(_join_blocks(his, nb))
                loss.append(_join_blocks(los, nb))
            rcss = [_cumsum_dot(after2, his, los) for his, los in zip(hiss, loss)]
            wss = []
            for las, sm, rcs, masked in zip(lass, sums, rcss, _mask_flags(js, masked_at)):
                ws = []
                for (h, b), la, s, rc in zip(chains, las, sm, rcs):
                    lanes = slice(b * LANES, (b + 1) * LANES)
                    w = jnp.exp(la - (rem_ref[h, :, lanes] + rc))
                    if masked:
                        w = jnp.where(key < qrow + b * LANES, w, 0.0)
                    ws.append(w.astype(BF16))
                    rem_ref[h, :, lanes] += s
                wss.append(_join_blocks(ws, nb))
            for j, ws in zip(js, wss):
                vtb = v_ref[pl.ds(pl.multiple_of(j * T, T), T), :].T.astype(BF16)
                for h in range(2):
                    acc_ref[h] += _dot(vtb, ws[h])

        _loop_tiles(i, tiles, True)
        o = jnp.where(top, acc_ref[0], acc_ref[1]).T
        o_ref[...] = o
        gt = g_ref[...]
        og = o * (gt * _sigmoid(gt))
        og_ref[...] = og.astype(BF16)
        ogt_ref[...] = og.T.astype(BF16)
        st_ref[0] = rem_ref[0]
        st_ref[1] = rem_ref[1]

    QT = ATT_QSUB * T
    qs, ks, vs, gs = _att_specs("sb", S, QT)
    W = npairs * LANES
    hows = ["rows", None, None, "rows", "rows", "rows", "lanes", "stat", None, None]
    return pl.pallas_call(
        _per_q_tile(body, hows), name=name, grid=(npairs, nq // ATT_QSUB),
        in_specs=[qs, ks, vs, gs],
        out_specs=[pl.BlockSpec((QT, LANES), lambda p, i: (i, p)), pl.BlockSpec((QT, LANES), lambda p, i: (i, p)),
                   pl.BlockSpec((LANES, QT), lambda p, i: (p, i)),
                   pl.BlockSpec((2, ATT_QSUB, 1, T), lambda p, i: (p, i, 0, 0))],
        out_shape=[jax.ShapeDtypeStruct((S, W), F32), jax.ShapeDtypeStruct((S, W), BF16),
                   jax.ShapeDtypeStruct((W, S), BF16),
                   jax.ShapeDtypeStruct((2 * npairs, nq, 1, T), F32)],
        scratch_shapes=[pltpu.VMEM((2, 1, T), F32), pltpu.VMEM((2, LANES, T), F32)],
        compiler_params=_cparams(("parallel", "parallel")),
    )(proj, proj, proj, proj)


def _sb_bwd_t(proj, do, tot, S, npairs, name):
    T = ATT_T
    nq = S // T
    nb = T // LANES
    scale = 64 ** -0.5

    def body(i, q_ref, k_ref, v_ref, do_ref, st_ref, dq_ref, dk_ref, dv_ref, dqt_ref, pre_ref, gpre_ref):

        @pl.when(i == 0)
        def _():
            dk_ref[...] = jnp.zeros_like(dk_ref)
            dv_ref[...] = jnp.zeros_like(dv_ref)

        m0 = lax.broadcasted_iota(jnp.int32, (1, LANES), 1) < 64
        top = lax.broadcasted_iota(jnp.int32, (LANES, 1), 0) < 64
        key = lax.broadcasted_iota(jnp.int32, (T, LANES), 0)
        qrow = lax.broadcasted_iota(jnp.int32, (T, LANES), 1)
        r = lax.broadcasted_iota(jnp.int32, (T, T), 0)
        c = lax.broadcasted_iota(jnp.int32, (T, T), 1)
        upto = (c <= r).astype(BF16)
        upto2 = jnp.concatenate([upto, upto], axis=1)
        left = (c < r).astype(BF16)
        qh = _head_q("sb", q_ref, m0, scale)
        dov = do_ref[...]
        doh = [jnp.where(m0, dov, 0.0).astype(BF16), jnp.where(m0, 0.0, dov).astype(BF16)]
        tot_h = [st_ref[0], st_ref[1]]
        dqt_ref[...] = jnp.zeros_like(dqt_ref)
        pre_ref[...] = jnp.zeros_like(pre_ref)
        gpre_ref[...] = jnp.zeros_like(gpre_ref)
        chains = [(h, b) for h in range(2) for b in range(nb)]

        def tiles(js, masked_at):
            starts = [pl.multiple_of(j * T, T) for j in js]
            zss, dwss = [], []
            for start in starts:
                vb = v_ref[pl.ds(start, T), :].astype(BF16)
                kb = k_ref[pl.ds(start, T), :].astype(BF16)
                zss.append(_split_blocks([_dot_nt(kb, qh[h]) for h in range(2)]))
                dwss.append(_split_blocks([_dot_nt(vb, doh[h]) for h in range(2)]))
            lass, sums, hiss, loss = [], [], [], []
            for zs, masked in zip(zss, _mask_flags(js, masked_at)):
                las, sm, his, los = [], [], [], []
                for (h, b), z in zip(chains, zs):
                    sp, la = _softplus_parts(z)
                    if masked:
                        sp = jnp.where(key < qrow + b * LANES, sp, 0.0)
                    hi, lo = _split2(sp)
                    las.append(la)
                    sm.append(jnp.sum(sp, axis=0, keepdims=True))
                    his.append(hi)
                    los.append(lo)
                lass.append(las)
                sums.append(sm)
                hiss.append(_join_blocks(his, nb))
                loss.append(_join_blocks(los, nb))
            pcss = [_cumsum_dot(upto2, his, los) for his, los in zip(hiss, loss)]
            wss, gss = [], []
            for las, sm, pcs, dws, masked in zip(lass, sums, pcss, dwss, _mask_flags(js, masked_at)):
                ws, gs = [], []
                for (h, b), la, s, pc, dw in zip(chains, las, sm, pcs, dws):
                    lanes = slice(b * LANES, (b + 1) * LANES)
                    w = jnp.exp(la - ((tot_h[h][:, lanes] - pre_ref[h, :, lanes]) - pc))
                    if masked:
                        w = jnp.where(key < qrow + b * LANES, w, 0.0)
                    ws.append(w.astype(BF16))
                    gs.append(dw * w)
                    pre_ref[h, :, lanes] += s
                wss.append(_join_blocks(ws, nb))
                gss.append(gs)
            gcss = [_split_blocks([_dot(left, g) for g in _join_blocks([g.astype(BF16) for g in gs], nb)]) for gs in gss]
            dzss = []
            for las, gs, gcs, masked in zip(lass, gss, gcss, _mask_flags(js, masked_at)):
                dzs = []
                for (h, b), la, g, gc in zip(chains, las, gs, gcs):
                    lanes = slice(b * LANES, (b + 1) * LANES)
                    dz = g - (g + (gpre_ref[h, :, lanes] + gc)) * jnp.exp(la)
                    if masked:
                        dz = jnp.where(key < qrow + b * LANES, dz, 0.0)
                    dzs.append(dz.astype(BF16))
                    gpre_ref[h, :, lanes] += jnp.sum(g, axis=0, keepdims=True)
                dzss.append(_join_blocks(dzs, nb))
            for start, ws, dzs in zip(starts, wss, dzss):
                kt = k_ref[pl.ds(start, T), :].T.astype(BF16)
                dkc = dvc = None
                for h in range(2):
                    dkh = _dot(dzs[h], qh[h])
                    dvh = _dot(ws[h], doh[h])
                    dkc = dkh if dkc is None else dkc + dkh
                    dvc = dvh if dvc is None else dvc + dvh
                    dqt_ref[h] += _dot(kt, dzs[h])
                dk_ref[pl.ds(start, T), :] += dkc
                dv_ref[pl.ds(start, T), :] += dvc

        _loop_tiles(i, tiles, False)
        dq_ref[...] = jnp.where(top, dqt_ref[0], dqt_ref[1]).T * scale

    QT = ATT_QSUB * T
    qs, ks, vs, _ = _att_specs("sb", S, QT)
    W = npairs * LANES
    hows = ["rows", None, None, "rows", "stat", "rows", None, None, None, None, None]
    return pl.pallas_call(
        _per_q_tile(body, hows), name=name, grid=(npairs, nq // ATT_QSUB),
        in_specs=[qs, ks, vs,
                  pl.BlockSpec((QT, LANES), lambda p, i: (i, p)),
                  pl.BlockSpec((2, ATT_QSUB, 1, T), lambda p, i: (p, i, 0, 0))],
        out_specs=[pl.BlockSpec((QT, LANES), lambda p, i: (i, p)), pl.BlockSpec((S, LANES), lambda p, i: (0, p)),
                   pl.BlockSpec((S, LANES), lambda p, i: (0, p))],
        out_shape=[jax.ShapeDtypeStruct((S, W), F32)] * 3,
        scratch_shapes=[pltpu.VMEM((2, LANES, T), F32), pltpu.VMEM((2, 1, T), F32), pltpu.VMEM((2, 1, T), F32)],
        compiler_params=_cparams(("parallel", "arbitrary")),
    )(proj, proj, proj, do, tot)


def _pad_w0(w):
    z = lambda n: jnp.zeros((w.shape[0], n), w.dtype)
    return jnp.concatenate([w[:, 2048:2432], w[:, 2432:2688], z(64), w[:, 2688:2720], z(32),
                            w[:, 1536:2048], w[:, 2720:3232], w[:, 0:512], w[:, 512:1024], w[:, 1024:1536]], axis=1)


def _unpad_w0(wp):
    return jnp.concatenate([wp[:, L0_SBQ:L0_SBQ + 512], wp[:, L0_SBK:L0_SBK + 512], wp[:, L0_SBV:L0_SBV + 512],
                            wp[:, L0_SBG:L0_SBG + 512], wp[:, 0:384], wp[:, 384:640], wp[:, 704:736],
                            wp[:, L0_MLG:L0_MLG + 512]], axis=1)


def _pad_wq(w):
    return jnp.pad(w.reshape(384, 8, 96), ((0, 0), (0, 0), (0, 32))).reshape(384, 1024)


def _unpad_wq(wp):
    return wp.reshape(384, 8, 128)[:, :, :96].reshape(384, 768)


def _pad_wkv(w):
    w3 = w.reshape(256, 8, 128)
    k = jnp.pad(w3[:, :, :64], ((0, 0), (0, 0), (0, 64))).reshape(256, 1024)
    return jnp.concatenate([k, w3[:, :, 64:].reshape(256, 512)], axis=1)


def _unpad_wkv(wp):
    k = wp[:, :1024].reshape(256, 8, 128)[:, :, :64]
    v = wp[:, 1024:].reshape(256, 8, 64)
    return jnp.concatenate([k, v], axis=-1).reshape(256, 1024)


def _pad_w1(w):
    return jnp.concatenate([w, jnp.zeros((w.shape[0], L1_WIDTH - ODD_IN_WIDTH), w.dtype)], axis=1)


def _local_step(x, positions, target, g, w0p, wqp, wkvp, wo0, w1p, wo1, send_early=None):
    S = x.shape[0]
    nq = S // ATT_T
    invf = ROPE_THETA ** (-jnp.arange(0, MLA_ROPE_DIM, 2, dtype=F32) / MLA_ROPE_DIM)
    invf = jnp.concatenate([jnp.zeros((64,), F32), invf, invf, jnp.zeros((32,), F32)]).reshape(1, LANES)
    cosT, s1T, s2T = _rope_tables(positions.reshape(S, 1), invf, "rope_tables")
    bfp = jnp.pad(g["l1_b_f"], ((0, 0), (0, LANES - FOX_HEADS)))

    if isinstance(w1p, tuple):
        w1_shard, finish_w1 = w1p
        proj0, h0t, w1_all = _norm_matmul(x, g["l0_pre_g"], w0p, "l0_in_proj", ride=w1_shard)
        w1p = finish_w1(w1_all)
    else:
        proj0, h0t = _norm_matmul(x, g["l0_pre_g"], w0p, "l0_in_proj")
    qm, km, vm, qnt, cnt = _mla_prep(proj0, g["l0_q_a_g"], g["l0_kv_a_g"], wqp, wkvp, cosT, s1T, s2T, "mla_prep")
    o_sb, og_sb, ogt_sb, tot_sb = _sb_fwd_t(proj0, S, 4, "sb_fwd")
    o_ml, og_ml, ogt_ml, lse_ml = _softmax_fwd("mla", (qm, km, vm, proj0), None, S, 4, "mla_fwd")
    y0, x1 = _out_proj(og_sb, og_ml, 0, 0, wo0, x, g["l0_post_g"], None, "l0_out_proj")

    proj1, h1t = _norm_matmul(x1, g["l1_pre_g"], w1p, "l1_in_proj")
    cfx = _fox_prep(proj1, bfp, "fox_prep")
    c16 = cfx[:, :FOX_HEADS].T
    c_col = jnp.broadcast_to(c16[:, :, None], (FOX_HEADS, S, LANES))
    o_fx, og_fx, ogt_fx, lse_fx = _softmax_fwd("fox", (proj1, proj1, proj1, proj1), c_col, S, 8, "fox_fwd")
    y1, dx2, lsum = _out_proj(og_fx, og_fx, 0, 1, wo1, x1, g["l1_post_g"], target, "l1_out_proj")

    dy1, do1, dgate1, d_post1 = _out_proj_bwd(dx2, y1, g["l1_post_g"], wo1, proj1, (L1_G, L1_G + 512), o_fx, o_fx, 0, 1, "l1_out_bwd")
    dwo1 = _matmul_t(ogt_fx, dy1, "l1_dw_out")
    dq1, dk1, dv1, dck, dcq = _softmax_bwd_t("fox", proj1, proj1, proj1, do1, 0, o_fx, lse_fx, c_col, S, 8,
                                             "fox_bwd")
    dc = jnp.pad((dcq.reshape(FOX_HEADS, S) - dck.reshape(FOX_HEADS, S)).T, ((0, 0), (0, LANES - FOX_HEADS)))
    df, d_bf = _fox_prep_bwd(dc, proj1, bfp, "fox_prep_bwd")
    pieces1 = [(L1_Q, dq1), (L1_K, dk1), (L1_V, dv1), (L1_G, dgate1), (L1_F, df)]
    dx1, d_pre1 = _in_proj_bwd(pieces1, w1p, x1, g["l1_pre_g"], dx2, "l1_in_bwd")
    dw1p = jnp.concatenate(_matmul_t_many(h1t, [dq1, dk1], "l1_dw_in_a")
                           + _matmul_t_many(h1t, [dv1, dgate1, df], "l1_dw_in_b"), axis=1)
    early = None if send_early is None else send_early(dw1p)

    dy0, do0, dgate0, d_post0 = _out_proj_bwd(dx1, y0, g["l0_post_g"], wo0, proj0, (L0_SBG, L0_MLG), o_sb, o_ml, 0, 0,
                                              "l0_out_bwd")
    dwo0 = jnp.concatenate([_matmul_t(ogt_sb, dy0, "l0_dw_out_sb"), _matmul_t(ogt_ml, dy0, "l0_dw_out_mla")], axis=0)
    dsq, dsk, dsv = _sb_bwd_t(proj0, do0, tot_sb, S, 4, "sb_bwd")
    dqm, dkm, dvm = _softmax_bwd_t("mla", qm, km, vm, do0, 4, o_ml, lse_ml, None, S, 4, "mla_bwd")
    dprep, dqb, dkvb, d_qag, d_kvag = _mla_prep_bwd(dqm, dkm, dvm, proj0, g["l0_q_a_g"], g["l0_kv_a_g"], wqp, wkvp,
                                                    cosT, s1T, s2T, "mla_prep_bwd")
    dwqp = _matmul_t(qnt, dqb, "l0_dw_qb")
    dwkvp = _matmul_t(cnt, dkvb, "l0_dw_kvb")
    pieces0 = [(L0_PREP, dprep), (L0_SBG, dgate0), (L0_SBQ, dsq), (L0_SBK, dsk), (L0_SBV, dsv)]
    if send_early is None:
        dx0, d_pre0 = _in_proj_bwd(pieces0, w0p, x, g["l0_pre_g"], dx1, "l0_in_bwd")
        early_slots = None
    else:
        dx0, d_pre0, early_slots = _in_proj_bwd(pieces0, w0p, x, g["l0_pre_g"], dx1, "l0_in_bwd", ride=early)
    dw0p = jnp.concatenate(_matmul_t_many(h0t, [dprep, dgate0], "l0_dw_in_a")
                           + _matmul_t_many(h0t, [dsq, dsk, dsv], "l0_dw_in_b"), axis=1)

    grads = {
        "l0_pre_g": d_pre0, "l0_post_g": d_post0, "l0_w_in": dw0p, "l0_q_a_g": d_qag, "l0_w_q_b": dwqp,
        "l0_kv_a_g": d_kvag, "l0_w_kv_b": dwkvp, "l0_w_out": dwo0, "l1_pre_g": d_pre1, "l1_post_g": d_post1,
        "l1_w_in": dw1p, "l1_b_f": d_bf[:, :FOX_HEADS], "l1_w_out": dwo1,
    }
    grads["early_q"], grads["early_slots"] = early, early_slots
    return lsum, dx0, grads


_ANY = pl.BlockSpec(memory_space=pl.ANY)


def _place():
    return lax.axis_index("x"), lax.axis_index("y"), lax.axis_index("c")


def _other_chips(x, y):
    return [(1 - x, y), (x, 1 - y), (1 - x, 1 - y)]


def _half(rows, c):
    return pl.ds(c * (rows // 2), rows // 2)


def _gather_phases(p_refs, out_refs, send_sems, recv_sems):
    n = len(p_refs)
    x, y, c = _place()
    sibling = (x, y, 1 - c)
    chips = _other_chips(x, y)

    def blk(k, chip, cc):
        return out_refs[k].at[2 * chip[0] + chip[1], _half(p_refs[k].shape[0], cc)]

    def copy(s, src, dst, to):
        return pltpu.make_async_remote_copy(src_ref=src, dst_ref=dst, send_sem=send_sems.at[s],
                                            recv_sem=recv_sems.at[s], device_id=to, device_id_type=MESH)

    def first():
        return [copy(6 * k + j, p_refs[k].at[_half(p_refs[k].shape[0], c)], blk(k, (x, y), c), (*chip, c))
                for j, chip in enumerate(chips) for k in range(n)]

    def begin():
        for cp in first():
            cp.start()

    def finish():
        passed = []
        for j, chip in enumerate(chips):
            for k in range(n):
                copy(6 * k + j, blk(k, chip, c), blk(k, chip, c), (x, y, c)).wait_recv()
                passed.append(copy(6 * k + 3 + j, blk(k, chip, c), blk(k, chip, c), sibling))
                passed[-1].start()
        for j, chip in enumerate(chips):
            for k in range(n):
                copy(6 * k + 3 + j, blk(k, chip, 1 - c), blk(k, chip, 1 - c), (x, y, c)).wait_recv()
        for cp in first() + passed:
            cp.wait_send()

    return begin, finish


def _weight_gather(parts):
    n = len(parts)

    def body(*refs):
        begin, finish = _gather_phases(refs[:n], refs[n:2 * n], refs[2 * n], refs[2 * n + 1])
        begin()
        finish()

    return pl.pallas_call(
        body, name="weight_gather", in_specs=[_ANY] * n, out_specs=[_ANY] * n,
        out_shape=[jax.ShapeDtypeStruct((4,) + a.shape, a.dtype) for a in parts],
        scratch_shapes=[pltpu.SemaphoreType.DMA((6 * n,)), pltpu.SemaphoreType.DMA((6 * n,))],
    )(*parts)


def _grad_core_exchange(ps, name="grad_core_exchange"):
    n = len(ps)

    def body(*refs):
        p_refs, recv_refs, send_sems, recv_sems = refs[:n], refs[n:2 * n], refs[2 * n], refs[2 * n + 1]
        x, y, c = _place()
        give = [pltpu.make_async_remote_copy(src_ref=p_refs[k].at[j, _half(p_refs[k].shape[1], 1 - c)],
                                             dst_ref=recv_refs[k].at[j], send_sem=send_sems.at[4 * k + j],
                                             recv_sem=recv_sems.at[4 * k + j], device_id=(x, y, 1 - c),
                                             device_id_type=MESH) for k in range(n) for j in range(4)]
        for cp in give:
            cp.start()
        for cp in give:
            cp.wait()

    return pl.pallas_call(
        body, name=name, in_specs=[_ANY] * n, out_specs=[_ANY] * n,
        out_shape=[jax.ShapeDtypeStruct((4, p.shape[1] // 2, p.shape[2]), p.dtype) for p in ps],
        scratch_shapes=[pltpu.SemaphoreType.DMA((4 * n,)), pltpu.SemaphoreType.DMA((4 * n,))],
    )(*ps)


def _grad_rows(rows):
    return _pick(rows, (1296, 512, rows))


def _grad_add_cores(p, theirs, c1, name):
    _, rh, cols = theirs.shape
    tr = _grad_rows(rh)

    def body(c_ref, a_ref, b_ref, o_ref):
        o_ref[...] = (a_ref[...] + b_ref[...]).astype(BF16)

    spec = pl.BlockSpec((None, tr, cols), lambda j, r, c: (j, r, 0))
    grid_spec = pltpu.PrefetchScalarGridSpec(
        num_scalar_prefetch=1, grid=(4, rh // tr),
        in_specs=[pl.BlockSpec((None, None, tr, cols), lambda j, r, c: (j, c[0], r, 0)), spec], out_specs=spec)
    return pl.pallas_call(
        body, name=name, grid_spec=grid_spec, out_shape=jax.ShapeDtypeStruct(theirs.shape, BF16),
        compiler_params=_cparams(("parallel", "parallel")),
    )(c1, p.reshape(4, 2, rh, cols), theirs)


def _exchange_phases(q_refs, out_refs, send_sems, recv_sems):
    n = len(q_refs)
    x, y, c = _place()
    me = 2 * x + y
    chips = _other_chips(x, y)

    def sends():
        return [pltpu.make_async_remote_copy(src_ref=q_refs[k].at[2 * chip[0] + chip[1]], dst_ref=out_refs[k].at[me],
                                             send_sem=send_sems.at[3 * k + j], recv_sem=recv_sems.at[3 * k + j],
                                             device_id=(*chip, c), device_id_type=MESH)
                for j, chip in enumerate(chips) for k in range(n)]

    def begin():
        for cp in sends():
            cp.start()

    def finish():
        for j, chip in enumerate(chips):
            for k in range(n):
                slot = out_refs[k].at[2 * chip[0] + chip[1]]
                pltpu.make_async_remote_copy(src_ref=slot, dst_ref=slot, send_sem=send_sems.at[3 * k + j],
                                             recv_sem=recv_sems.at[3 * k + j], device_id=(x, y, c),
                                             device_id_type=MESH).wait_recv()
        for cp in sends():
            cp.wait_send()

    return begin, finish


def _grad_chip_exchange(qs):
    n = len(qs)

    def body(*refs):
        begin, finish = _exchange_phases(refs[:n], refs[n:2 * n], refs[2 * n], refs[2 * n + 1])
        begin()
        finish()

    return pl.pallas_call(
        body, name="grad_chip_exchange", in_specs=[_ANY] * n, out_specs=[_ANY] * n,
        out_shape=[jax.ShapeDtypeStruct(q.shape, q.dtype) for q in qs],
        scratch_shapes=[pltpu.SemaphoreType.DMA((3 * n,)), pltpu.SemaphoreType.DMA((3 * n,))],
    )(*qs)


def _grad_add_chips(q, slots, me1, name):
    _, rh, cols = q.shape
    tr = _grad_rows(rh)

    def body(me_ref, own_ref, s0, s1, s2, s3, o_ref):
        me = me_ref[0]
        t = [jnp.where(me == j, own_ref[...], s[...]).astype(F32) for j, s in enumerate((s0, s1, s2, s3))]
        o_ref[...] = ((t[0] + t[1]) + t[2]) + t[3]

    def slot_spec(j):
        return pl.BlockSpec((None, tr, cols), lambda r, me: (jnp.where(me[0] == j, (j + 1) % 4, j), r, 0))

    grid_spec = pltpu.PrefetchScalarGridSpec(
        num_scalar_prefetch=1, grid=(rh // tr,),
        in_specs=[pl.BlockSpec((None, tr, cols), lambda r, me: (me[0], r, 0))] + [slot_spec(j) for j in range(4)],
        out_specs=pl.BlockSpec((tr, cols), lambda r, me: (r, 0)))
    return pl.pallas_call(
        body, name=name, grid_spec=grid_spec, out_shape=jax.ShapeDtypeStruct(q.shape[1:], F32),
        compiler_params=_cparams(("parallel",)),
    )(me1, q, slots, slots, slots, slots)


def _final_exchange(ts, sp):
    n = len(ts)

    def body(*refs):
        t_refs, sp_ref, out_refs, tot_ref = refs[:n], refs[n], refs[n + 1:2 * n + 1], refs[2 * n + 1]
        gath_ref, send_sems, recv_sems = refs[2 * n + 2:]
        x, y, c = _place()
        me = 4 * x + 2 * y + c
        gath_ref[me] = sp_ref[...]
        peers = []
        for k in range(1, 8):
            px = 1 - x if k & 4 else x
            py = 1 - y if k & 2 else y
            pc = 1 - c if k & 1 else c
            peers.append((px, py, pc))
        give = [pltpu.make_async_remote_copy(src_ref=t_refs[k], dst_ref=out_refs[k], send_sem=send_sems.at[k],
                                             recv_sem=recv_sems.at[k], device_id=(x, y, 1 - c), device_id_type=MESH)
                for k in range(n)]
        sends = [pltpu.make_async_remote_copy(src_ref=sp_ref, dst_ref=gath_ref.at[me], send_sem=send_sems.at[n + k],
                                              recv_sem=recv_sems.at[n + k], device_id=peer, device_id_type=MESH)
                 for k, peer in enumerate(peers)]
        for cp in give + sends:
            cp.start()
        for k, (px, py, pc) in enumerate(peers):
            slot = gath_ref.at[4 * px + 2 * py + pc]
            pltpu.make_async_remote_copy(src_ref=slot, dst_ref=slot, send_sem=send_sems.at[n + k],
                                         recv_sem=recv_sems.at[n + k], device_id=(x, y, c),
                                         device_id_type=MESH).wait_recv()
        for cp in sends:
            cp.wait_send()
        tot = gath_ref[0]
        for d in range(1, 8):
            tot = tot + gath_ref[d]
        tot_ref[...] = tot
        for cp in give:
            cp.wait()

    vm = pl.BlockSpec(memory_space=pltpu.VMEM)
    return pl.pallas_call(
        body, name="final_exchange", in_specs=[_ANY] * n + [vm], out_specs=[_ANY] * n + [vm],
        out_shape=[jax.ShapeDtypeStruct(t.shape, t.dtype) for t in ts] + [jax.ShapeDtypeStruct(sp.shape, sp.dtype)],
        scratch_shapes=[pltpu.VMEM((8,) + sp.shape, sp.dtype), pltpu.SemaphoreType.DMA((n + 7,)),
                        pltpu.SemaphoreType.DMA((n + 7,))],
    )(*ts, sp)


def _adamw_update(w, gv, m, v):
    mn = ADAM_B1 * m + (1.0 - ADAM_B1) * gv
    vn = ADAM_B2 * v + (1.0 - ADAM_B2) * (gv * gv)
    m_hat = mn / (1.0 - ADAM_B1 ** ADAM_STEP)
    v_hat = vn / (1.0 - ADAM_B2 ** ADAM_STEP)
    return -ADAM_LR * (m_hat / (jnp.sqrt(v_hat) + ADAM_EPS) + ADAM_WD * w), mn, vn


def _adamw(w, g, m, v, name):
    rows, cols = w.shape

    def body(w_ref, g_ref, m_ref, v_ref, d_ref, mo_ref, vo_ref):
        d_ref[...], mo_ref[...], vo_ref[...] = _adamw_update(w_ref[...], g_ref[...], m_ref[...], v_ref[...])

    if rows % 256 == 0 or cols % 256 != 0:
        tr = _pick(rows, (256, rows))
        grid, spec = (rows // tr,), pl.BlockSpec((tr, cols), lambda r: (r, 0))
    else:
        grid, spec = (cols // 256,), pl.BlockSpec((rows, 256), lambda r: (0, r))
    shp = jax.ShapeDtypeStruct(w.shape, F32)
    return pl.pallas_call(
        body, name=name, grid=grid, in_specs=[spec] * 4, out_specs=[spec] * 3, out_shape=[shp] * 3,
        compiler_params=_cparams(("parallel",)),
    )(w, g, m, v)


MAT_NAMES = ("l0_w_in", "l0_w_q_b", "l0_w_kv_b", "l0_w_out", "l1_w_in", "l1_w_out")
VEC_NAMES = ("l0_pre_g", "l0_post_g", "l0_q_a_g", "l0_kv_a_g", "l1_pre_g", "l1_post_g", "l1_b_f")
WEIGHT_NAMES = ("l0_pre_g", "l0_post_g", "l0_w_in", "l0_q_a_g", "l0_w_q_b", "l0_kv_a_g", "l0_w_kv_b", "l0_w_out",
                "l1_pre_g", "l1_post_g", "l1_w_in", "l1_b_f", "l1_w_out")
MAT_SHARD = {"l0_w_in": (1024, 808), "l0_w_q_b": (384, 192), "l0_w_kv_b": (256, 256), "l0_w_out": (256, 1024),
             "l1_w_in": (1024, 1028), "l1_w_out": (256, 1024)}
ROW_SHARDED = ("l0_w_out", "l1_w_out")
WHOLE_MATS = ("l0_w_in", "l1_w_in")
PACKED_MATS = ("l0_w_q_b", "l0_w_kv_b", "l0_w_out", "l1_w_out")
VEC_LEN = {"l0_pre_g": 1024, "l0_post_g": 1024, "l0_q_a_g": 384, "l0_kv_a_g": 256, "l1_pre_g": 1024,
           "l1_post_g": 1024, "l1_b_f": 16}


def _mat_rows(n):
    r, c = MAT_SHARD[n]
    return r * c // LANES


def _pack_shards(shards):
    return jnp.concatenate([shards[n].reshape(shards[n].shape[:-2] + (_mat_rows(n), LANES)) for n in PACKED_MATS],
                           axis=-2)


def _unpack_shards(pack):
    out, at = {}, 0
    for n in PACKED_MATS:
        out[n] = pack[..., at:at + _mat_rows(n), :].reshape(pack.shape[:-2] + MAT_SHARD[n])
        at += _mat_rows(n)
    return out


def _join_shards(n, s):
    if n in ROW_SHARDED:
        return s.reshape(4 * s.shape[1], s.shape[2])
    return s.transpose(1, 0, 2).reshape(s.shape[1], 4 * s.shape[2])


def _cut_shards(n, w):
    r, c = MAT_SHARD[n]
    if n in ROW_SHARDED:
        return w.reshape(4, r, c)
    return w.reshape(r, 4, c).transpose(1, 0, 2)


def _pack_vecs(vecs):
    parts = []
    for n in VEC_NAMES:
        v = vecs[n].reshape(-1)
        parts.append(jnp.pad(v, (0, VEC_ROWS * LANES - v.shape[0])).reshape(VEC_ROWS, LANES))
    return jnp.concatenate(parts, axis=0)


def _unpack_vecs(pack):
    return {n: pack[k * VEC_ROWS:(k + 1) * VEC_ROWS].reshape(-1)[:VEC_LEN[n]] for k, n in enumerate(VEC_NAMES)}


def kernel(x, positions, l0_pre_g, l0_post_g, l0_w_in, l0_q_a_g, l0_w_q_b, l0_kv_a_g, l0_w_kv_b, l0_w_out, l1_pre_g, l1_post_g, l1_w_in, l1_b_f, l1_w_out, loss_target, m_l0_pre_g, m_l0_post_g, m_l0_w_in, m_l0_q_a_g, m_l0_w_q_b, m_l0_kv_a_g, m_l0_w_kv_b, m_l0_w_out, m_l1_pre_g, m_l1_post_g, m_l1_w_in, m_l1_b_f, m_l1_w_out, v_l0_pre_g, v_l0_post_g, v_l0_w_in, v_l0_q_a_g, v_l0_w_q_b, v_l0_kv_a_g, v_l0_w_kv_b, v_l0_w_out, v_l1_pre_g, v_l1_post_g, v_l1_w_in, v_l1_b_f, v_l1_w_out):
    w = dict(l0_pre_g=l0_pre_g, l0_post_g=l0_post_g, l0_w_in=l0_w_in, l0_q_a_g=l0_q_a_g, l0_w_q_b=l0_w_q_b,
             l0_kv_a_g=l0_kv_a_g, l0_w_kv_b=l0_w_kv_b, l0_w_out=l0_w_out, l1_pre_g=l1_pre_g, l1_post_g=l1_post_g,
             l1_w_in=l1_w_in, l1_b_f=l1_b_f, l1_w_out=l1_w_out)
    m = dict(l0_pre_g=m_l0_pre_g, l0_post_g=m_l0_post_g, l0_w_in=m_l0_w_in, l0_q_a_g=m_l0_q_a_g, l0_w_q_b=m_l0_w_q_b,
             l0_kv_a_g=m_l0_kv_a_g, l0_w_kv_b=m_l0_w_kv_b, l0_w_out=m_l0_w_out, l1_pre_g=m_l1_pre_g,
             l1_post_g=m_l1_post_g, l1_w_in=m_l1_w_in, l1_b_f=m_l1_b_f, l1_w_out=m_l1_w_out)
    v = dict(l0_pre_g=v_l0_pre_g, l0_post_g=v_l0_post_g, l0_w_in=v_l0_w_in, l0_q_a_g=v_l0_q_a_g, l0_w_q_b=v_l0_w_q_b,
             l0_kv_a_g=v_l0_kv_a_g, l0_w_kv_b=v_l0_w_kv_b, l0_w_out=v_l0_w_out, l1_pre_g=v_l1_pre_g,
             l1_post_g=v_l1_post_g, l1_w_in=v_l1_w_in, l1_b_f=v_l1_b_f, l1_w_out=v_l1_w_out)

    cx, cy, cc = _place()
    me1 = jnp.reshape(2 * cx + cy, (1,)).astype(jnp.int32)
    c1 = jnp.reshape(cc, (1,)).astype(jnp.int32)
    w_bf = {n: w[n].astype(BF16) for n in MAT_NAMES}
    def with_mine(got, own):
        return lax.dynamic_update_slice(got, own[None], (2 * cx + cy, 0, 0))

    mine = [_pack_shards(w_bf), w_bf["l0_w_in"]]
    got = [with_mine(g, a) for g, a in zip(_weight_gather(mine), mine)]
    gathered = dict(_unpack_shards(got[0]), l0_w_in=got[1])
    full = {n: _join_shards(n, gathered[n]) for n in MAT_NAMES if n != "l1_w_in"}
    gains = {n: w[n].reshape(1, -1) for n in VEC_NAMES}

    def finish_w1(w1_all):
        return _pad_w1(_join_shards("l1_w_in", with_mine(w1_all, w_bf["l1_w_in"])))

    def send_early(dw1p):
        g1 = _cut_shards("l1_w_in", dw1p[:, :ODD_IN_WIDTH])
        return _grad_add_cores(g1, _grad_core_exchange([g1], "grad_core_exchange_l1_w_in")[0], c1,
                               "grad_add_cores_l1_w_in")

    lsum, dx0, grads = _local_step(
        x[0], positions[0], loss_target[0], gains, _pad_w0(full["l0_w_in"]), _pad_wq(full["l0_w_q_b"]),
        _pad_wkv(full["l0_w_kv_b"]), full["l0_w_out"], (w_bf["l1_w_in"], finish_w1), full["l1_w_out"], send_early)

    gfull = {"l0_w_in": _unpad_w0(grads["l0_w_in"]), "l0_w_q_b": _unpad_wq(grads["l0_w_q_b"]),
             "l0_w_kv_b": _unpad_wkv(grads["l0_w_kv_b"]), "l0_w_out": grads["l0_w_out"],
             "l1_w_out": grads["l1_w_out"]}
    cut = {n: _cut_shards(n, gfull[n]) for n in gfull}
    tags = ("packed", "l0_w_in")
    g_parts = [_pack_shards(cut), cut["l0_w_in"]]
    q_cores = [_grad_add_cores(p, t, c1, "grad_add_cores_" + tag)
               for p, t, tag in zip(g_parts, _grad_core_exchange(g_parts), tags)]
    slots = list(_grad_chip_exchange(q_cores)) + [grads["early_slots"]]
    q_cores.append(grads["early_q"])
    g_mine = [_grad_add_chips(q, s, me1, "grad_add_chips_" + tag)
              for q, s, tag in zip(q_cores, slots, tags + ("l1_w_in",))]
    *g_theirs, small = _final_exchange(g_mine, jnp.concatenate([_pack_vecs({n: grads[n] for n in VEC_NAMES}),
                                                                lsum.reshape(D_MODEL // LANES, LANES)], axis=0))
    g_small = small[:SMALL_ROWS]
    loss = 0.5 * jnp.sum(small[SMALL_ROWS:]) / float(D_MODEL)

    whole = [jnp.concatenate([lax.select(cc == 0, a, b), lax.select(cc == 0, b, a)], axis=0)
             for a, b in zip(g_mine, g_theirs)]
    g_mats = dict(_unpack_shards(whole[0]), **dict(zip(WHOLE_MATS, whole[1:])))
    d_mats, m_mats, v_mats = {}, {}, {}
    for n in PACKED_MATS:
        d_mats[n], m_mats[n], v_mats[n] = _adamw(w[n], g_mats[n], m[n], v[n], "adamw_" + n)
    for n in WHOLE_MATS:
        gt = g_mats[n].T
        outs = _adamw(w[n].T, gt, m[n].T, v[n].T, "adamw_" + n)
        g_mats[n], d_mats[n], m_mats[n], v_mats[n] = gt.T, outs[0].T, outs[1].T, outs[2].T
    d_small, m_small, v_small = _adamw(_pack_vecs(w), g_small, _pack_vecs(m), _pack_vecs(v), "adamw_vecs")

    def leaves(mats, vec_pack):
        out = dict(mats)
        out.update(_unpack_vecs(vec_pack))
        return [out[n] for n in WEIGHT_NAMES]

    return (loss, dx0[None], *leaves(g_mats, g_small), *leaves(d_mats, d_small), *leaves(m_mats, m_small),
            *leaves(v_mats, v_small))
```

```python
import jax
import jax.numpy as jnp
from jax import lax
from jax.experimental import pallas as pl
from jax.experimental.pallas import tpu as pltpu

F32 = jnp.float32
BF16 = jnp.bfloat16
MESH = pl.DeviceIdType.MESH

D_MODEL = 1024
RMS_EPS = 1e-6
ROPE_THETA = 10000.0
SB_WIDTH = 512
MLA_Q_LORA = 384
MLA_KV_LORA = 256
MLA_ROPE_DIM = 32
MLA_WIDTH = 512
FOX_WIDTH = 1024
FOX_HEADS = 16
EVEN_IN_WIDTH = 3232
ODD_IN_WIDTH = 4112

ADAM_LR = 0.001
ADAM_B1 = 0.9
ADAM_B2 = 0.999
ADAM_EPS = 1e-08
ADAM_WD = 0.01
ADAM_STEP = 10

LANES = 128
VMEM_LIMIT = 56 * 1024 * 1024

L0_PREP = 0
L0_PREP_W = 768
L0_SBG = 768
L0_MLG = 1280
L0_SBQ = 1792
L0_SBK = 2304
L0_SBV = 2816
L0_WIDTH = 3328
L1_Q = 0
L1_K = 1024
L1_V = 2048
L1_G = 3072
L1_F = 4096
L1_WIDTH = 4224

ATT_T = 256
ATT_GROUP = 4
ATT_QSUB = 2
NEG = -1e30
MLA_SCALE = 96 ** -0.5

VEC_ROWS = 8
SMALL_ROWS = 7 * VEC_ROWS


def _cparams(sem, **kw):
    return pltpu.CompilerParams(dimension_semantics=sem, vmem_limit_bytes=VMEM_LIMIT, **kw)


def _dot(a, b):
    return lax.dot_general(a, b, (((1,), (0,)), ((), ())), preferred_element_type=F32)


def _dot_nt(a, b):
    return lax.dot_general(a, b, (((1,), (1,)), ((), ())), preferred_element_type=F32)


def _sigmoid(x):
    return 0.5 * jnp.tanh(0.5 * x) + 0.5


def _rstd(x):
    return lax.rsqrt(jnp.mean(x * x, axis=-1, keepdims=True) + RMS_EPS)


def _norm_bwd(x, g, dy):
    r = _rstd(x)
    xn = x * r
    dxn = dy * g
    dx = r * (dxn - xn * jnp.mean(dxn * xn, axis=-1, keepdims=True))
    return dx, dy * xn


def _split3(x):
    hi = x.astype(BF16)
    r1 = x - hi.astype(F32)
    mid = r1.astype(BF16)
    lo = (r1 - mid.astype(F32)).astype(BF16)
    return hi, mid, lo


def _wide_tile(n, cap=1792):
    return max(t for t in range(LANES, min(n, cap) + 1, LANES) if n % t == 0)


def _pick(n, cands):
    for c in cands:
        if n % c == 0:
            return c
    raise ValueError(n)


def _norm_matmul(x, g, w, name, ride=None):
    S, K = x.shape
    N = w.shape[1]
    tm = _pick(S, (1024, 512, 256))
    tn = _wide_tile(N)
    ni, nj = S // tm, N // tn

    def body(x_ref, g_ref, w_ref, *rest):
        if ride is None:
            o_ref, ht_ref, h_ref = rest
        else:
            a_ref, o_ref, ht_ref, land_ref, h_ref, send_sems, recv_sems = rest
            begin, finish = _gather_phases([a_ref], [land_ref], send_sems, recv_sems)
            pl.when((pl.program_id(0) == 0) & (pl.program_id(1) == 0))(begin)

        @pl.when(pl.program_id(1) == 0)
        def _():
            xv = x_ref[...]
            h = (xv * _rstd(xv)) * g_ref[...]
            h_ref[...] = h.astype(BF16)
            ht_ref[...] = h.T.astype(BF16)
        o_ref[...] = _dot(h_ref[...], w_ref[...])
        if ride is not None:
            pl.when((pl.program_id(0) == ni - 1) & (pl.program_id(1) == nj - 1))(finish)

    in_specs = [pl.BlockSpec((tm, K), lambda i, j: (i, 0)),
                pl.BlockSpec((1, K), lambda i, j: (0, 0)),
                pl.BlockSpec((K, tn), lambda i, j: (0, j))]
    out_specs = [pl.BlockSpec((tm, tn), lambda i, j: (i, j)), pl.BlockSpec((K, tm), lambda i, j: (0, i))]
    out_shape = [jax.ShapeDtypeStruct((S, N), F32), jax.ShapeDtypeStruct((K, S), BF16)]
    scratch = [pltpu.VMEM((tm, K), BF16)]
    args = [x, g, w]
    if ride is not None:
        in_specs.append(_ANY)
        out_specs.append(_ANY)
        out_shape.append(jax.ShapeDtypeStruct((4,) + ride.shape, ride.dtype))
        scratch += [pltpu.SemaphoreType.DMA((6,)), pltpu.SemaphoreType.DMA((6,))]
        args.append(ride)
    return pl.pallas_call(
        body, name=name, grid=(ni, nj), in_specs=in_specs, out_specs=out_specs, out_shape=out_shape,
        scratch_shapes=scratch,
        compiler_params=_cparams(("parallel", "arbitrary") if ride is None else ("arbitrary", "arbitrary")),
    )(*args)


def _matmul_t(at, b, name):
    M, S = at.shape
    N = b.shape[1]
    tn = _wide_tile(N)
    ts = _pick(S, (512, 256))

    def body(a_ref, b_ref, o_ref):
        @pl.when(pl.program_id(1) == 0)
        def _():
            o_ref[...] = jnp.zeros_like(o_ref)
        o_ref[...] += _dot(a_ref[...], b_ref[...].astype(BF16))

    return pl.pallas_call(
        body, name=name, grid=(N // tn, S // ts),
        in_specs=[pl.BlockSpec((M, ts), lambda j, k: (0, k)),
                  pl.BlockSpec((ts, tn), lambda j, k: (k, j))],
        out_specs=pl.BlockSpec((M, tn), lambda j, k: (0, j)),
        out_shape=jax.ShapeDtypeStruct((M, N), F32),
        compiler_params=_cparams(("parallel", "arbitrary")),
    )(at, b)


def _matmul_t_many(at, bs, name):
    M, S = at.shape
    ts = _pick(S, (512, 256))
    n = len(bs)

    def body(*refs):
        a_ref, b_refs, o_refs = refs[0], refs[1:1 + n], refs[1 + n:]

        @pl.when(pl.program_id(0) == 0)
        def _():
            for o_ref in o_refs:
                o_ref[...] = jnp.zeros_like(o_ref)

        a = a_ref[...]
        for b_ref, o_ref in zip(b_refs, o_refs):
            o_ref[...] += _dot(a, b_ref[...].astype(BF16))

    return pl.pallas_call(
        body, name=name, grid=(S // ts,),
        in_specs=[pl.BlockSpec((M, ts), lambda k: (0, k))] + [pl.BlockSpec((ts, b.shape[1]), lambda k: (k, 0)) for b in bs],
        out_specs=[pl.BlockSpec((M, b.shape[1]), lambda k: (0, 0)) for b in bs],
        out_shape=[jax.ShapeDtypeStruct((M, b.shape[1]), F32) for b in bs],
        compiler_params=_cparams(("arbitrary",)),
    )(at, *bs)


def _in_proj_bwd(pieces, w, x, g, dx_up, name, ride=None):
    S, K = x.shape
    N = w.shape[1]
    tm = _pick(S, (256,))
    nsteps = S // tm
    offs = [off for off, _ in pieces]
    arrs = [a for _, a in pieces]

    def body(*refs):
        d_refs = refs[:len(arrs)]
        if ride is None:
            w_ref, x_ref, g_ref, u_ref, dx_ref, dg_ref = refs[len(arrs):]
        else:
            w_ref, x_ref, g_ref, u_ref, q_ref, dx_ref, dg_ref, slots_ref, send_sems, recv_sems = refs[len(arrs):]
            begin, finish = _exchange_phases([q_ref], [slots_ref], send_sems, recv_sems)
            pl.when(pl.program_id(0) == 0)(begin)

        @pl.when(pl.program_id(0) == 0)
        def _():
            dg_ref[...] = jnp.zeros_like(dg_ref)

        acc = None
        for off, d_ref in zip(offs, d_refs):
            part = _dot_nt(d_ref[...].astype(BF16), w_ref[:, off:off + d_ref.shape[1]])
            acc = part if acc is None else acc + part
        dx, dgrow = _norm_bwd(x_ref[...], g_ref[...], acc)
        dx_ref[...] = u_ref[...] + dx
        dg_ref[...] += jnp.sum(dgrow, axis=0, keepdims=True)
        if ride is not None:
            pl.when(pl.program_id(0) == nsteps - 1)(finish)

    row = lambda i: (i, 0)
    fixed = lambda i: (0, 0)
    in_specs = [pl.BlockSpec((tm, a.shape[1]), row) for a in arrs] + [
        pl.BlockSpec((K, N), fixed), pl.BlockSpec((tm, K), row), pl.BlockSpec((1, K), fixed), pl.BlockSpec((tm, K), row)]
    out_specs = [pl.BlockSpec((tm, K), row), pl.BlockSpec((1, K), fixed)]
    out_shape = [jax.ShapeDtypeStruct((S, K), F32), jax.ShapeDtypeStruct((1, K), F32)]
    args = [*arrs, w, x, g, dx_up]
    scratch = []
    if ride is not None:
        in_specs.append(_ANY)
        out_specs.append(_ANY)
        out_shape.append(jax.ShapeDtypeStruct(ride.shape, ride.dtype))
        scratch = [pltpu.SemaphoreType.DMA((3,)), pltpu.SemaphoreType.DMA((3,))]
        args.append(ride)
    return pl.pallas_call(
        body, name=name, grid=(nsteps,), in_specs=in_specs, out_specs=out_specs, out_shape=out_shape,
        scratch_shapes=scratch, compiler_params=_cparams(("arbitrary",)),
    )(*args)


def _out_proj(og_a, og_b, blk_a, blk_b, w, x, g, target, name):
    S = x.shape[0]
    D = x.shape[1]
    tm = _pick(S, (512, 256))
    with_loss = target is not None

    def body(*refs):
        if with_loss:
            a_ref, b_ref, wa_ref, wb_ref, x_ref, g_ref, t_ref, y_ref, o_ref, l_ref = refs
        else:
            a_ref, b_ref, wa_ref, wb_ref, x_ref, g_ref, y_ref, o_ref = refs
        y = _dot(a_ref[...], wa_ref[...]) + _dot(b_ref[...], wb_ref[...])
        y_ref[...] = y
        xn = x_ref[...] + (y * _rstd(y)) * g_ref[...]
        if with_loss:
            @pl.when(pl.program_id(0) == 0)
            def _():
                l_ref[...] = jnp.zeros_like(l_ref)
            d = xn - t_ref[...]
            o_ref[...] = d / float(D)
            l_ref[...] += jnp.sum(d * d, axis=0, keepdims=True)
        else:
            o_ref[...] = xn

    row = lambda i: (i, 0)
    in_specs = [pl.BlockSpec((tm, 512), lambda i: (i, blk_a)),
                pl.BlockSpec((tm, 512), lambda i: (i, blk_b)),
                pl.BlockSpec((512, D), lambda i: (0, 0)),
                pl.BlockSpec((512, D), lambda i: (1, 0)),
                pl.BlockSpec((tm, D), row),
                pl.BlockSpec((1, D), lambda i: (0, 0))]
    out_specs = [pl.BlockSpec((tm, D), row), pl.BlockSpec((tm, D), row)]
    out_shape = [jax.ShapeDtypeStruct((S, D), F32), jax.ShapeDtypeStruct((S, D), F32)]
    args = [og_a, og_b, w, w, x, g]
    if with_loss:
        in_specs.append(pl.BlockSpec((tm, D), row))
        out_specs.append(pl.BlockSpec((1, D), lambda i: (0, 0)))
        out_shape.append(jax.ShapeDtypeStruct((1, D), F32))
        args.append(target)
    return pl.pallas_call(
        body, name=name, grid=(S // tm,), in_specs=in_specs, out_specs=out_specs, out_shape=out_shape,
        compiler_params=_cparams(("arbitrary",)),
    )(*args)


def _out_proj_bwd(dx_up, y, g, w, proj, gate_offs, o_a, o_b, oblk_a, oblk_b, name):
    S, D = y.shape
    tm = _pick(S, (256,))
    gblk = [off // 256 + c for off in gate_offs for c in range(2)]

    def body(u_ref, y_ref, g_ref, w_ref, g0, g1, g2, g3, oa_ref, ob_ref, dy_ref, do_ref, dgate_ref, dg_ref):
        @pl.when(pl.program_id(0) == 0)
        def _():
            dg_ref[...] = jnp.zeros_like(dg_ref)
        dy, dgrow = _norm_bwd(y_ref[...], g_ref[...], u_ref[...])
        dg_ref[...] += jnp.sum(dgrow, axis=0, keepdims=True)
        dyb = dy.astype(BF16)
        dy_ref[...] = dyb
        dog = _dot_nt(dyb, w_ref[...])
        gates = (g0, g1, g2, g3)
        for c in range(4):
            gt = gates[c][...]
            sg = _sigmoid(gt)
            o_ref = oa_ref if c < 2 else ob_ref
            ov = o_ref[:, (c % 2) * 256:(c % 2 + 1) * 256]
            dc = dog[:, c * 256:(c + 1) * 256]
            do_ref[:, c * 256:(c + 1) * 256] = dc * (gt * sg)
            dgate_ref[:, c * 256:(c + 1) * 256] = dc * ov * (sg * (1.0 + gt * (1.0 - sg)))

    row = lambda i: (i, 0)
    gspec = lambda c: pl.BlockSpec((tm, 256), lambda i: (i, gblk[c]))
    return pl.pallas_call(
        body, name=name, grid=(S // tm,),
        in_specs=[pl.BlockSpec((tm, D), row), pl.BlockSpec((tm, D), row), pl.BlockSpec((1, D), lambda i: (0, 0)),
                  pl.BlockSpec((D, D), lambda i: (0, 0)),
                  gspec(0), gspec(1), gspec(2), gspec(3),
                  pl.BlockSpec((tm, 512), lambda i: (i, oblk_a)),
                  pl.BlockSpec((tm, 512), lambda i: (i, oblk_b))],
        out_specs=[pl.BlockSpec((tm, D), row), pl.BlockSpec((tm, D), row), pl.BlockSpec((tm, D), row),
                   pl.BlockSpec((1, D), lambda i: (0, 0))],
        out_shape=[jax.ShapeDtypeStruct((S, D), BF16), jax.ShapeDtypeStruct((S, D), F32),
                   jax.ShapeDtypeStruct((S, D), F32), jax.ShapeDtypeStruct((1, D), F32)],
        compiler_params=_cparams(("arbitrary",)),
    )(dx_up, y, g, w, proj, proj, proj, proj, o_a, o_b)


def _rope_tables(pos, invf, name):
    S = pos.shape[0]
    tm = _pick(S, (512, 256))

    def body(p_ref, f_ref, c_ref, s1_ref, s2_ref):
        lane = lax.broadcasted_iota(jnp.int32, (1, LANES), 1)
        ang = p_ref[...].astype(F32) * f_ref[...]
        c, s = jnp.cos(ang), jnp.sin(ang)
        c_ref[...] = jnp.where((lane >= 64) & (lane < 96), c, 1.0)
        s1_ref[...] = jnp.where((lane >= 64) & (lane < 80), -s, 0.0)
        s2_ref[...] = jnp.where((lane >= 80) & (lane < 96), s, 0.0)

    spec = pl.BlockSpec((tm, LANES), lambda i: (i, 0))
    return pl.pallas_call(
        body, name=name, grid=(S // tm,),
        in_specs=[pl.BlockSpec((tm, 1), lambda i: (i, 0)), pl.BlockSpec((1, LANES), lambda i: (0, 0))],
        out_specs=[spec, spec, spec],
        out_shape=[jax.ShapeDtypeStruct((S, LANES), F32)] * 3,
        compiler_params=_cparams(("parallel",)),
    )(pos, invf)


def _rope(x, c, s1, s2):
    return x * c + pltpu.roll(x, LANES - 16, 1) * s1 + pltpu.roll(x, 16, 1) * s2


def _rope_t(d, c, s1, s2):
    return d * c + pltpu.roll(d * s1, 16, 1) + pltpu.roll(d * s2, LANES - 16, 1)


def _mla_prep(proj, gq, gkv, wq, wkv, cosT, s1T, s2T, name):
    S = proj.shape[0]
    tm = _pick(S, (256,))

    def body(p_ref, gq_ref, gkv_ref, wq_ref, wkv_ref, c_ref, s1_ref, s2_ref, q_ref, k_ref, v_ref, qn_ref, cn_ref):
        qa = p_ref[:, 0:384]
        ckv = p_ref[:, 384:640]
        kr = p_ref[:, 640:768]
        qn32 = (qa * _rstd(qa)) * gq_ref[...]
        cn32 = (ckv * _rstd(ckv)) * gkv_ref[...]
        qn = qn32.astype(BF16)
        cn = cn32.astype(BF16)
        qn_ref[...] = qn32.T.astype(BF16)
        cn_ref[...] = cn32.T.astype(BF16)
        qb = _dot(qn, wq_ref[...])
        kvb = _dot(cn, wkv_ref[...])
        c, s1, s2 = c_ref[...], s1_ref[...], s2_ref[...]
        krr = _rope(kr, c, s1, s2)
        for h in range(8):
            sl = slice(h * LANES, (h + 1) * LANES)
            q_ref[:, sl] = _rope(qb[:, sl], c, s1, s2) * MLA_SCALE
            k_ref[:, sl] = kvb[:, sl] + krr
        v_ref[...] = kvb[:, 1024:1536]

    row = lambda i: (i, 0)
    fixed = lambda i: (0, 0)
    tspec = pl.BlockSpec((tm, LANES), row)
    return pl.pallas_call(
        body, name=name, grid=(S // tm,),
        in_specs=[pl.BlockSpec((tm, L0_PREP_W), lambda i: (i, L0_PREP // L0_PREP_W)),
                  pl.BlockSpec((1, 384), fixed), pl.BlockSpec((1, 256), fixed),
                  pl.BlockSpec((384, 1024), fixed), pl.BlockSpec((256, 1536), fixed), tspec, tspec, tspec],
        out_specs=[pl.BlockSpec((tm, 1024), row), pl.BlockSpec((tm, 1024), row), pl.BlockSpec((tm, 512), row),
                   pl.BlockSpec((384, tm), lambda i: (0, i)), pl.BlockSpec((256, tm), lambda i: (0, i))],
        out_shape=[jax.ShapeDtypeStruct((S, 1024), F32), jax.ShapeDtypeStruct((S, 1024), F32),
                   jax.ShapeDtypeStruct((S, 512), F32), jax.ShapeDtypeStruct((384, S), BF16),
                   jax.ShapeDtypeStruct((256, S), BF16)],
        compiler_params=_cparams(("parallel",)),
    )(proj, gq, gkv, wq, wkv, cosT, s1T, s2T)


def _mla_prep_bwd(dq, dk, dv, proj, gq, gkv, wq, wkv, cosT, s1T, s2T, name):
    S = proj.shape[0]
    tm = _pick(S, (256,))

    def body(dq_ref, dk_ref, dv_ref, p_ref, gq_ref, gkv_ref, wq_ref, wkv_ref, c_ref, s1_ref, s2_ref,
             dp_ref, dqb_ref, dkvb_ref, dgq_ref, dgkv_ref):
        @pl.when(pl.program_id(0) == 0)
        def _():
            dgq_ref[...] = jnp.zeros_like(dgq_ref)
            dgkv_ref[...] = jnp.zeros_like(dgkv_ref)
        c, s1, s2 = c_ref[...], s1_ref[...], s2_ref[...]
        lane = lax.broadcasted_iota(jnp.int32, (1, LANES), 1)
        dkr = jnp.zeros((tm, LANES), F32)
        for h in range(8):
            sl = slice(h * LANES, (h + 1) * LANES)
            dqb_ref[:, sl] = _rope_t(dq_ref[:, sl], c, s1, s2).astype(BF16)
            dkh = dk_ref[:, sl]
            dkvb_ref[:, sl] = dkh.astype(BF16)
            dkr = dkr + dkh
        dkvb_ref[:, 1024:1536] = dv_ref[...].astype(BF16)
        dkr = jnp.where((lane >= 64) & (lane < 96), _rope_t(dkr, c, s1, s2), 0.0)
        dqn = _dot_nt(dqb_ref[...], wq_ref[...])
        dcn = _dot_nt(dkvb_ref[...], wkv_ref[...])
        dqa, gq_row = _norm_bwd(p_ref[:, 0:384], gq_ref[...], dqn)
        dckv, gkv_row = _norm_bwd(p_ref[:, 384:640], gkv_ref[...], dcn)
        dp_ref[:, 0:384] = dqa
        dp_ref[:, 384:640] = dckv
        dp_ref[:, 640:768] = dkr
        dgq_ref[...] += jnp.sum(gq_row, axis=0, keepdims=True)
        dgkv_ref[...] += jnp.sum(gkv_row, axis=0, keepdims=True)

    row = lambda i: (i, 0)
    fixed = lambda i: (0, 0)
    tspec = pl.BlockSpec((tm, LANES), row)
    return pl.pallas_call(
        body, name=name, grid=(S // tm,),
        in_specs=[pl.BlockSpec((tm, 1024), row), pl.BlockSpec((tm, 1024), row), pl.BlockSpec((tm, 512), row),
                  pl.BlockSpec((tm, L0_PREP_W), lambda i: (i, L0_PREP // L0_PREP_W)),
                  pl.BlockSpec((1, 384), fixed), pl.BlockSpec((1, 256), fixed),
                  pl.BlockSpec((384, 1024), fixed), pl.BlockSpec((256, 1536), fixed), tspec, tspec, tspec],
        out_specs=[pl.BlockSpec((tm, L0_PREP_W), row), pl.BlockSpec((tm, 1024), row), pl.BlockSpec((tm, 1536), row),
                   pl.BlockSpec((1, 384), fixed), pl.BlockSpec((1, 256), fixed)],
        out_shape=[jax.ShapeDtypeStruct((S, L0_PREP_W), F32), jax.ShapeDtypeStruct((S, 1024), BF16),
                   jax.ShapeDtypeStruct((S, 1536), BF16), jax.ShapeDtypeStruct((1, 384), F32),
                   jax.ShapeDtypeStruct((1, 256), F32)],
        compiler_params=_cparams(("arbitrary",)),
    )(dq, dk, dv, proj, gq, gkv, wq, wkv, cosT, s1T, s2T)


def _fox_prep(proj, bf, name):
    S = proj.shape[0]
    tm = _pick(S, (256,))

    def body(f_ref, b_ref, c_ref, carry_ref):
        @pl.when(pl.program_id(0) == 0)
        def _():
            carry_ref[...] = jnp.zeros_like(carry_ref)
        u = f_ref[...] + b_ref[...]
        lf = jnp.minimum(u, 0.0) - jnp.log(1.0 + jnp.exp(-jnp.abs(u)))
        r = lax.broadcasted_iota(jnp.int32, (tm, tm), 0)
        cidx = lax.broadcasted_iota(jnp.int32, (tm, tm), 1)
        tri = (cidx <= r).astype(BF16)
        hi, mid, lo = _split3(lf)
        c = carry_ref[...] + (_dot(tri, hi) + _dot(tri, mid) + _dot(tri, lo))
        c_ref[...] = c
        carry_ref[...] = c[tm - 1:tm, :]

    return pl.pallas_call(
        body, name=name, grid=(S // tm,),
        in_specs=[pl.BlockSpec((tm, LANES), lambda i: (i, L1_F // LANES)), pl.BlockSpec((1, LANES), lambda i: (0, 0))],
        out_specs=pl.BlockSpec((tm, LANES), lambda i: (i, 0)),
        out_shape=jax.ShapeDtypeStruct((S, LANES), F32),
        scratch_shapes=[pltpu.VMEM((1, LANES), F32)],
        compiler_params=_cparams(("arbitrary",)),
    )(proj, bf)


def _fox_prep_bwd(dc, proj, bf, name):
    S = proj.shape[0]
    tm = _pick(S, (256,))
    nb = S // tm

    def body(dc_ref, f_ref, b_ref, df_ref, db_ref, carry_ref):
        @pl.when(pl.program_id(0) == 0)
        def _():
            carry_ref[...] = jnp.zeros_like(carry_ref)
            db_ref[...] = jnp.zeros_like(db_ref)
        r = lax.broadcasted_iota(jnp.int32, (tm, tm), 0)
        cidx = lax.broadcasted_iota(jnp.int32, (tm, tm), 1)
        tri = (cidx >= r).astype(BF16)
        hi, mid, lo = _split3(dc_ref[...])
        dlf = carry_ref[...] + (_dot(tri, hi) + _dot(tri, mid) + _dot(tri, lo))
        carry_ref[...] = dlf[0:1, :]
        u = f_ref[...] + b_ref[...]
        e = jnp.exp(-jnp.abs(u))
        sneg = jnp.where(u >= 0.0, e, 1.0) / (1.0 + e)
        lane = lax.broadcasted_iota(jnp.int32, (1, LANES), 1)
        df = jnp.where(lane < FOX_HEADS, dlf * sneg, 0.0)
        df_ref[...] = df
        db_ref[...] += jnp.sum(df, axis=0, keepdims=True)

    return pl.pallas_call(
        body, name=name, grid=(nb,),
        in_specs=[pl.BlockSpec((tm, LANES), lambda i: (nb - 1 - i, 0)),
                  pl.BlockSpec((tm, LANES), lambda i: (nb - 1 - i, L1_F // LANES)),
                  pl.BlockSpec((1, LANES), lambda i: (0, 0))],
        out_specs=[pl.BlockSpec((tm, LANES), lambda i: (nb - 1 - i, 0)), pl.BlockSpec((1, LANES), lambda i: (0, 0))],
        out_shape=[jax.ShapeDtypeStruct((S, LANES), F32), jax.ShapeDtypeStruct((1, LANES), F32)],
        scratch_shapes=[pltpu.VMEM((1, LANES), F32)],
        compiler_params=_cparams(("arbitrary",)),
    )(dc, proj, bf)


def _att_specs(kind, S, T):
    if kind == "sb":
        qo, ko, vo, go = L0_SBQ // LANES, L0_SBK // LANES, L0_SBV // LANES, L0_SBG // LANES
    elif kind == "fox":
        qo, ko, vo, go = L1_Q // LANES, L1_K // LANES, L1_V // LANES, L1_G // LANES
    else:
        go = L0_MLG // LANES
        return (pl.BlockSpec((T, 256), lambda p, i: (i, p)), pl.BlockSpec((S, 256), lambda p, i: (0, p)),
                pl.BlockSpec((S, LANES), lambda p, i: (0, p)), pl.BlockSpec((T, LANES), lambda p, i: (i, go + p)))
    return (pl.BlockSpec((T, LANES), lambda p, i: (i, qo + p)), pl.BlockSpec((S, LANES), lambda p, i: (0, ko + p)),
            pl.BlockSpec((S, LANES), lambda p, i: (0, vo + p)), pl.BlockSpec((T, LANES), lambda p, i: (i, go + p)))


def _per_q_tile(tile_body, hows):
    T = ATT_T

    def view(ref, u, how):
        if how == "rows":
            return ref.at[pl.ds(u * T, T)]
        if how == "lanes":
            return ref.at[:, pl.ds(u * T, T)]
        if how == "stat":
            return ref.at[:, u]
        return ref

    def body(*refs):
        for u in range(ATT_QSUB):
            tile_body(pl.program_id(1) * ATT_QSUB + u, *[view(r, u, how) for r, how in zip(refs, hows)])

    return body


def _mask_flags(js, masked_at):
    return [t == masked_at for t in range(len(js))]


def _loop_tiles(i, tiles, right_to_left, G=ATT_GROUP):
    ng = i // G
    rest = i - ng * G

    def leftover():
        for r in range(G):
            @pl.when(rest == r)
            def _():
                if right_to_left:
                    tiles([i - u for u in range(r + 1)], 0)
                else:
                    tiles([ng * G + u for u in range(r + 1)], r)

    def group(g, carry):
        if right_to_left:
            tiles([ng * G - 1 - (g * G + u) for u in range(G)], None)
        else:
            tiles([g * G + u for u in range(G)], None)
        return carry

    if right_to_left:
        leftover()
    lax.fori_loop(0, ng, group, 0)
    if not right_to_left:
        leftover()


def _head_q(kind, q_ref, m0, scale):
    if kind == "mla":
        return [q_ref[:, 0:LANES].astype(BF16), q_ref[:, LANES:2 * LANES].astype(BF16)]
    qv = q_ref[...] * scale
    return [jnp.where(m0, qv, 0.0).astype(BF16), jnp.where(m0, 0.0, qv).astype(BF16)]


def _head_k(kind, k_ref, start, T):
    if kind == "mla":
        return [k_ref[pl.ds(start, T), 0:LANES].astype(BF16), k_ref[pl.ds(start, T), LANES:2 * LANES].astype(BF16)]
    kb = k_ref[pl.ds(start, T), :].astype(BF16)
    return [kb, kb]


def _softmax_fwd(kind, qkvg, c_col, S, npairs, name):
    T = ATT_T
    nq = S // T
    fox = kind == "fox"
    scale = (96 if kind == "mla" else 64) ** -0.5

    def body(i, *refs):
        if fox:
            q_ref, k_ref, v_ref, g_ref, cc_ref, o_ref, og_ref, ogt_ref, st_ref, m_ref, acc_ref = refs
        else:
            q_ref, k_ref, v_ref, g_ref, o_ref, og_ref, ogt_ref, st_ref, m_ref, acc_ref = refs
        m0 = lax.broadcasted_iota(jnp.int32, (1, LANES), 1) < 64
        top = lax.broadcasted_iota(jnp.int32, (LANES, 1), 0) < 64
        key = lax.broadcasted_iota(jnp.int32, (T, LANES), 0)
        qrow = lax.broadcasted_iota(jnp.int32, (T, LANES), 1)
        qh = _head_q(kind, q_ref, m0, scale)
        m_ref[...] = jnp.full(m_ref.shape, NEG, F32)
        acc_ref[...] = jnp.zeros(acc_ref.shape, F32)
        chains = [(h, b) for h in range(2) for b in range(T // LANES)]

        def tiles(js, masked_at):
            starts = [pl.multiple_of(j * T, T) for j in js]
            zss = []
            for start in starts:
                kh = _head_k(kind, k_ref, start, T)
                zss.append(_split_blocks([_dot_nt(kh[h], qh[h]) for h in range(2)]))
            pss, alss = [], []
            for start, zs, masked in zip(starts, zss, _mask_flags(js, masked_at)):
                ps, alphas = [], []
                for (h, b), z in zip(chains, zs):
                    lanes = slice(b * LANES, (b + 1) * LANES)
                    if fox:
                        z = z - cc_ref[h, pl.ds(start, T), :]
                    if masked:
                        z = jnp.where(key <= qrow + b * LANES, z, NEG)
                    m_prev = m_ref[h, :, lanes]
                    m_new = jnp.maximum(m_prev, jnp.max(z, axis=0, keepdims=True))
                    alphas.append(jnp.exp(m_prev - m_new))
                    ps.append(jnp.exp(z - m_new).astype(BF16))
                    m_ref[h, :, lanes] = m_new
                pss.append(_join_blocks(ps, T // LANES))
                alss.append(_join_blocks(alphas, T // LANES))
            for start, ps, alphas in zip(starts, pss, alss):
                vt = v_ref[pl.ds(start, T), :].T
                vth = [jnp.where(top, vt, 1.0).astype(BF16), jnp.where(top, 1.0, vt).astype(BF16)]
                for h in range(2):
                    acc_ref[h] = alphas[h] * acc_ref[h] + _dot(vth[h], ps[h])

        _loop_tiles(i, tiles, False, 2 * ATT_GROUP)
        acc = [acc_ref[0], acc_ref[1]]
        ot = jnp.concatenate([acc[0][0:64] / acc[0][64:128], acc[1][64:128] / acc[1][0:64]], axis=0)
        o = ot.T
        o_ref[...] = o
        gt = g_ref[...]
        og = o * (gt * _sigmoid(gt))
        og_ref[...] = og.astype(BF16)
        ogt_ref[...] = og.T.astype(BF16)
        st_ref[0] = m_ref[0] + jnp.log(acc[0][64:65])
        st_ref[1] = m_ref[1] + jnp.log(acc[1][0:1])

    QT = ATT_QSUB * T
    qs, ks, vs, gs = _att_specs(kind, S, QT)
    in_specs = [qs, ks, vs, gs]
    args = list(qkvg)
    hows = ["rows", None, None, "rows"]
    if fox:
        in_specs += [pl.BlockSpec((2, S, LANES), lambda p, i: (p, 0, 0))]
        args += [c_col]
        hows += [None]
    hows += ["rows", "rows", "lanes", "stat", None, None]
    W = npairs * LANES
    return pl.pallas_call(
        _per_q_tile(body, hows), name=name, grid=(npairs, nq // ATT_QSUB), in_specs=in_specs,
        out_specs=[pl.BlockSpec((QT, LANES), lambda p, i: (i, p)), pl.BlockSpec((QT, LANES), lambda p, i: (i, p)),
                   pl.BlockSpec((LANES, QT), lambda p, i: (p, i)),
                   pl.BlockSpec((2, ATT_QSUB, 1, T), lambda p, i: (p, i, 0, 0))],
        out_shape=[jax.ShapeDtypeStruct((S, W), F32), jax.ShapeDtypeStruct((S, W), BF16),
                   jax.ShapeDtypeStruct((W, S), BF16),
                   jax.ShapeDtypeStruct((2 * npairs, nq, 1, T), F32)],
        scratch_shapes=[pltpu.VMEM((2, 1, T), F32), pltpu.VMEM((2, LANES, T), F32)],
        compiler_params=_cparams(("parallel", "parallel")),
    )(*args)


def _softplus_parts(z):
    sp = jnp.maximum(z, 0.0) + jnp.log(1.0 + jnp.exp(-jnp.abs(z)))
    return sp, z - sp


def _cumsum_dot(tri2, his, los):
    return _split_blocks([_dot(tri2, jnp.concatenate([hi, lo], axis=0)) for hi, lo in zip(his, los)])


def _split2(x):
    hi = x.astype(BF16)
    return hi, (x - hi.astype(F32)).astype(BF16)


def _split_blocks(per_head):
    return [x[:, b * LANES:(b + 1) * LANES] for x in per_head for b in range(x.shape[1] // LANES)]


def _join_blocks(per_block, nb):
    return [jnp.concatenate(per_block[h * nb:(h + 1) * nb], axis=1) for h in range(len(per_block) // nb)]


def _row_of(col):
    return jnp.broadcast_to(col, (col.shape[0], LANES)).T[0:1]


def _softmax_bwd_t(kind, q, k, v, do, do_off, o, lse, c_col, S, npairs, name):
    T = ATT_T
    nq = S // T
    nb = T // LANES
    fox = kind == "fox"
    mla = kind == "mla"
    scale = (96 if mla else 64) ** -0.5
    kw = 256 if mla else LANES

    def body(i, *refs):
        if fox:
            (q_ref, k_ref, v_ref, do_ref, o_ref, st_ref, cc_ref,
             dq_ref, dk_ref, dv_ref, dck_ref, dcq_ref, dqt_ref, rs_ref, dkx_ref) = refs
        else:
            q_ref, k_ref, v_ref, do_ref, o_ref, st_ref, dq_ref, dk_ref, dv_ref, dqt_ref = refs

        @pl.when(i == 0)
        def _():
            dv_ref[...] = jnp.zeros_like(dv_ref)
            if fox:
                dkx_ref[...] = jnp.zeros_like(dkx_ref)
            else:
                dk_ref[...] = jnp.zeros_like(dk_ref)

        m0 = lax.broadcasted_iota(jnp.int32, (1, LANES), 1) < 64
        top = lax.broadcasted_iota(jnp.int32, (LANES, 1), 0) < 64
        key = lax.broadcasted_iota(jnp.int32, (T, LANES), 0)
        qrow = lax.broadcasted_iota(jnp.int32, (T, LANES), 1)
        qh = _head_q(kind, q_ref, m0, scale)
        if fox:
            qv = q_ref[...] * scale
            qk = [jnp.where(m0, qv, 1.0).astype(BF16), jnp.where(m0, 1.0, qv).astype(BF16)]
        else:
            qk = qh
        dov = do_ref[...]
        prod = dov * o_ref[...]
        dd = [_row_of(jnp.sum(jnp.where(m0, prod, 0.0), axis=1, keepdims=True)),
              _row_of(jnp.sum(jnp.where(m0, 0.0, prod), axis=1, keepdims=True))]
        doh = [jnp.where(m0, dov, 0.0).astype(BF16), jnp.where(m0, 0.0, dov).astype(BF16)]
        lse = [st_ref[0], st_ref[1]]
        dqt_ref[...] = jnp.zeros_like(dqt_ref)
        if fox:
            rs_ref[...] = jnp.zeros_like(rs_ref)
        chains = [(h, b) for h in range(2) for b in range(nb)]

        def tiles(js, masked_at):
            starts = [pl.multiple_of(j * T, T) for j in js]
            zss, dpss = [], []
            for start in starts:
                vb = v_ref[pl.ds(start, T), :].astype(BF16)
                kh = _head_k(kind, k_ref, start, T)
                zss.append(_split_blocks([_dot_nt(kh[h], qh[h]) for h in range(2)]))
                dpss.append(_split_blocks([_dot_nt(vb, doh[h]) for h in range(2)]))
            pss, dsss = [], []
            for start, zs, dps, masked in zip(starts, zss, dpss, _mask_flags(js, masked_at)):
                ps, dss = [], []
                for (h, b), z, dp in zip(chains, zs, dps):
                    lanes = slice(b * LANES, (b + 1) * LANES)
                    if fox:
                        z = z - cc_ref[h, pl.ds(start, T), :]
                    if masked:
                        z = jnp.where(key <= qrow + b * LANES, z, NEG)
                    p = jnp.exp(z - lse[h][:, lanes])
                    ds = p * (dp - dd[h][:, lanes])
                    dsb = ds.astype(BF16)
                    if fox:
                        rs_ref[h, :, lanes] += jnp.sum(dsb.astype(F32), axis=0, keepdims=True)
                    ps.append(p.astype(BF16))
                    dss.append(dsb)
                pss.append(_join_blocks(ps, nb))
                dsss.append(_join_blocks(dss, nb))
            for start, ps, dss in zip(starts, pss, dsss):
                kt = k_ref[pl.ds(start, T), :].T.astype(BF16)
                dvc = None
                for h in range(2):
                    dkh = _dot(dss[h], qk[h])
                    dvh = _dot(ps[h], doh[h])
                    dvc = dvh if dvc is None else dvc + dvh
                    kth = kt[h * LANES:(h + 1) * LANES] if mla else kt
                    dqt_ref[h] += _dot(kth, dss[h])
                    if fox:
                        dkx_ref[h, pl.ds(start, T), :] += dkh
                    elif mla:
                        dk_ref[pl.ds(start, T), h * LANES:(h + 1) * LANES] += dkh
                    else:
                        dk_ref[pl.ds(start, T), :] += dkh
                dv_ref[pl.ds(start, T), :] += dvc

        _loop_tiles(i, tiles, False)
        if mla:
            dq_ref[:, 0:LANES] = dqt_ref[0].T * scale
            dq_ref[:, LANES:2 * LANES] = dqt_ref[1].T * scale
        else:
            dq_ref[...] = jnp.where(top, dqt_ref[0], dqt_ref[1]).T * scale
        if fox:
            dcq_ref[0] = rs_ref[0]
            dcq_ref[1] = rs_ref[1]

            @pl.when(i == nq - 1)
            def _():
                dk_ref[...] = jnp.where(m0, dkx_ref[0], dkx_ref[1])
                dck_ref[0] = dkx_ref[0].T[64:65]
                dck_ref[1] = dkx_ref[1].T[0:1]

    QT = ATT_QSUB * T
    qs, ks, vs, _ = _att_specs(kind, S, QT)
    stat = pl.BlockSpec((2, ATT_QSUB, 1, T), lambda p, i: (p, i, 0, 0))
    in_specs = [qs, ks, vs,
                pl.BlockSpec((QT, LANES), lambda p, i: (i, do_off + p)),
                pl.BlockSpec((QT, LANES), lambda p, i: (i, p)), stat]
    args = [q, k, v, do, o, lse]
    hows = ["rows", None, None, "rows", "rows", "stat"]
    W = npairs * LANES
    out_specs = [pl.BlockSpec((QT, kw), lambda p, i: (i, p)), pl.BlockSpec((S, kw), lambda p, i: (0, p)),
                 pl.BlockSpec((S, LANES), lambda p, i: (0, p))]
    out_shape = [jax.ShapeDtypeStruct((S, npairs * kw), F32), jax.ShapeDtypeStruct((S, npairs * kw), F32),
                 jax.ShapeDtypeStruct((S, W), F32)]
    scratch = [pltpu.VMEM((2, LANES, T), F32)]
    if fox:
        in_specs.append(pl.BlockSpec((2, S, LANES), lambda p, i: (p, 0, 0)))
        args.append(c_col)
        out_specs += [pl.BlockSpec((2, 1, S), lambda p, i: (p, 0, 0)), stat]
        out_shape += [jax.ShapeDtypeStruct((2 * npairs, 1, S), F32), jax.ShapeDtypeStruct((2 * npairs, nq, 1, T), F32)]
        scratch += [pltpu.VMEM((2, 1, T), F32), pltpu.VMEM((2, S, LANES), F32)]
        hows += [None, "rows", None, None, None, "stat", None, None, None]
    else:
        hows += ["rows", None, None, None]
    return pl.pallas_call(
        _per_q_tile(body, hows), name=name, grid=(npairs, nq // ATT_QSUB), in_specs=in_specs, out_specs=out_specs,
        out_shape=out_shape, scratch_shapes=scratch, compiler_params=_cparams(("parallel", "arbitrary")),
    )(*args)


def _sb_fwd_t(proj, S, npairs, name):
    T = ATT_T
    nq = S // T
    nb = T // LANES
    scale = 64 ** -0.5

    def body(i, q_ref, k_ref, v_ref, g_ref, o_ref, og_ref, ogt_ref, st_ref, rem_ref, acc_ref):
        m0 = lax.broadcasted_iota(jnp.int32, (1, LANES), 1) < 64
        top = lax.broadcasted_iota(jnp.int32, (LANES, 1), 0) < 64
        key = lax.broadcasted_iota(jnp.int32, (T, LANES), 0)
        qrow = lax.broadcasted_iota(jnp.int32, (T, LANES), 1)
        r = lax.broadcasted_iota(jnp.int32, (T, T), 0)
        c = lax.broadcasted_iota(jnp.int32, (T, T), 1)
        after = (c > r).astype(BF16)
        after2 = jnp.concatenate([after, after], axis=1)
        qh = _head_q("sb", q_ref, m0, scale)
        rem_ref[...] = jnp.zeros_like(rem_ref)
        acc_ref[...] = jnp.zeros_like(acc_ref)
        chains = [(h, b) for h in range(2) for b in range(nb)]

        def tiles(js, masked_at):
            zss = []
            for j in js:
                kb = k_ref[pl.ds(pl.multiple_of(j * T, T), T), :].astype(BF16)
                zss.append(_split_blocks([_dot_nt(kb, qh[h]) for h in range(2)]))
            lass, sums, hiss, loss = [], [], [], []
            for zs, masked in zip(zss, _mask_flags(js, masked_at)):
                las, sm, his, los = [], [], [], []
                for (h, b), z in zip(chains, zs):
                    sp, la = _softplus_parts(z)
                    if masked:
                        sp = jnp.where(key < qrow + b * LANES, sp, 0.0)
                    hi, lo = _split2(sp)
                    las.append(la)
                    sm.append(jnp.sum(sp, axis=0, keepdims=True))
                    his.append(hi)
                    los.append(lo)
                lass.append(las)
                sums.append(sm)
                hiss.append(_join_blocks(his, nb))
                loss.append(_join_blocks(los, nb))
            rcss = [_cumsum_dot(after2, his, los) for his, los in zip(hiss, loss)]
            wss = []
            for las, sm, rcs, masked in zip(lass, sums, rcss, _mask_flags(js, masked_at)):
                ws = []
                for (h, b), la, s, rc in zip(chains, las, sm, rcs):
                    lanes = slice(b * LANES, (b + 1) * LANES)
                    w = jnp.exp(la - (rem_ref[h, :, lanes] + rc))
                    if masked:
                        w = jnp.where(key < qrow + b * LANES, w, 0.0)
                    ws.append(w.astype(BF16))
                    rem_ref[h, :, lanes] += s
                wss.append(_join_blocks(ws, nb))
            for j, ws in zip(js, wss):
                vtb = v_ref[pl.ds(pl.multiple_of(j * T, T), T), :].T.astype(BF16)
                for h in range(2):
                    acc_ref[h] += _dot(vtb, ws[h])

        _loop_tiles(i, tiles, True)
        o = jnp.where(top, acc_ref[0], acc_ref[1]).T
        o_ref[...] = o
        gt = g_ref[...]
        og = o * (gt * _sigmoid(gt))
        og_ref[...] = og.astype(BF16)
        ogt_ref[...] = og.T.astype(BF16)
        st_ref[0] = rem_ref[0]
        st_ref[1] = rem_ref[1]

    QT = ATT_QSUB * T
    qs, ks, vs, gs = _att_specs("sb", S, QT)
    W = npairs * LANES
    hows = ["rows", None, None, "rows", "rows", "rows", "lanes", "stat", None, None]
    return pl.pallas_call(
        _per_q_tile(body, hows), name=name, grid=(npairs, nq // ATT_QSUB),
        in_specs=[qs, ks, vs, gs],
        out_specs=[pl.BlockSpec((QT, LANES), lambda p, i: (i, p)), pl.BlockSpec((QT, LANES), lambda p, i: (i, p)),
                   pl.BlockSpec((LANES, QT), lambda p, i: (p, i)),
                   pl.BlockSpec((2, ATT_QSUB, 1, T), lambda p, i: (p, i, 0, 0))],
        out_shape=[jax.ShapeDtypeStruct((S, W), F32), jax.ShapeDtypeStruct((S, W), BF16),
                   jax.ShapeDtypeStruct((W, S), BF16),
                   jax.ShapeDtypeStruct((2 * npairs, nq, 1, T), F32)],
        scratch_shapes=[pltpu.VMEM((2, 1, T), F32), pltpu.VMEM((2, LANES, T), F32)],
        compiler_params=_cparams(("parallel", "parallel")),
    )(proj, proj, proj, proj)


def _sb_bwd_t(proj, do, tot, S, npairs, name):
    T = ATT_T
    nq = S // T
    nb = T // LANES
    scale = 64 ** -0.5

    def body(i, q_ref, k_ref, v_ref, do_ref, st_ref, dq_ref, dk_ref, dv_ref, dqt_ref, pre_ref, gpre_ref):

        @pl.when(i == 0)
        def _():
            dk_ref[...] = jnp.zeros_like(dk_ref)
            dv_ref[...] = jnp.zeros_like(dv_ref)

        m0 = lax.broadcasted_iota(jnp.int32, (1, LANES), 1) < 64
        top = lax.broadcasted_iota(jnp.int32, (LANES, 1), 0) < 64
        key = lax.broadcasted_iota(jnp.int32, (T, LANES), 0)
        qrow = lax.broadcasted_iota(jnp.int32, (T, LANES), 1)
        r = lax.broadcasted_iota(jnp.int32, (T, T), 0)
        c = lax.broadcasted_iota(jnp.int32, (T, T), 1)
        upto = (c <= r).astype(BF16)
        upto2 = jnp.concatenate([upto, upto], axis=1)
        left = (c < r).astype(BF16)
        qh = _head_q("sb", q_ref, m0, scale)
        dov = do_ref[...]
        doh = [jnp.where(m0, dov, 0.0).astype(BF16), jnp.where(m0, 0.0, dov).astype(BF16)]
        tot_h = [st_ref[0], st_ref[1]]
        dqt_ref[...] = jnp.zeros_like(dqt_ref)
        pre_ref[...] = jnp.zeros_like(pre_ref)
        gpre_ref[...] = jnp.zeros_like(gpre_ref)
        chains = [(h, b) for h in range(2) for b in range(nb)]

        def tiles(js, masked_at):
            starts = [pl.multiple_of(j * T, T) for j in js]
            zss, dwss = [], []
            for start in starts:
                vb = v_ref[pl.ds(start, T), :].astype(BF16)
                kb = k_ref[pl.ds(start, T), :].astype(BF16)
                zss.append(_split_blocks([_dot_nt(kb, qh[h]) for h in range(2)]))
                dwss.append(_split_blocks([_dot_nt(vb, doh[h]) for h in range(2)]))
            lass, sums, hiss, loss = [], [], [], []
            for zs, masked in zip(zss, _mask_flags(js, masked_at)):
                las, sm, his, los = [], [], [], []
                for (h, b), z in zip(chains, zs):
                    sp, la = _softplus_parts(z)
                    if masked:
                        sp = jnp.where(key < qrow + b * LANES, sp, 0.0)
                    hi, lo = _split2(sp)
                    las.append(la)
                    sm.append(jnp.sum(sp, axis=0, keepdims=True))
                    his.append(hi)
                    los.append(lo)
                lass.append(las)
                sums.append(sm)
                hiss.append(_join_blocks(his, nb))
                loss.append(_join_blocks(los, nb))
            pcss = [_cumsum_dot(upto2, his, los) for his, los in zip(hiss, loss)]
            wss, gss = [], []
            for las, sm, pcs, dws, masked in zip(lass, sums, pcss, dwss, _mask_flags(js, masked_at)):
                ws, gs = [], []
                for (h, b), la, s, pc, dw in zip(chains, las, sm, pcs, dws):
                    lanes = slice(b * LANES, (b + 1) * LANES)
                    w = jnp.exp(la - ((tot_h[h][:, lanes] - pre_ref[h, :, lanes]) - pc))
                    if masked:
                        w = jnp.where(key < qrow + b * LANES, w, 0.0)
                    ws.append(w.astype(BF16))
                    gs.append(dw * w)
                    pre_ref[h, :, lanes] += s
                wss.append(_join_blocks(ws, nb))
                gss.append(gs)
            gcss = [_split_blocks([_dot(left, g) for g in _join_blocks([g.astype(BF16) for g in gs], nb)]) for gs in gss]
            dzss = []
            for las, gs, gcs, masked in zip(lass, gss, gcss, _mask_flags(js, masked_at)):
                dzs = []
                for (h, b), la, g, gc in zip(chains, las, gs, gcs):
                    lanes = slice(b * LANES, (b + 1) * LANES)
                    dz = g - (g + (gpre_ref[h, :, lanes] + gc)) * jnp.exp(la)
                    if masked:
                        dz = jnp.where(key < qrow + b * LANES, dz, 0.0)
                    dzs.append(dz.astype(BF16))
                    gpre_ref[h, :, lanes] += jnp.sum(g, axis=0, keepdims=True)
                dzss.append(_join_blocks(dzs, nb))
            for start, ws, dzs in zip(starts, wss, dzss):
                kt = k_ref[pl.ds(start, T), :].T.astype(BF16)
                dkc = dvc = None
                for h in range(2):
                    dkh = _dot(dzs[h], qh[h])
                    dvh = _dot(ws[h], doh[h])
                    dkc = dkh if dkc is None else dkc + dkh
                    dvc = dvh if dvc is None else dvc + dvh
                    dqt_ref[h] += _dot(kt, dzs[h])
                dk_ref[pl.ds(start, T), :] += dkc
                dv_ref[pl.ds(start, T), :] += dvc

        _loop_tiles(i, tiles, False)
        dq_ref[...] = jnp.where(top, dqt_ref[0], dqt_ref[1]).T * scale

    QT = ATT_QSUB * T
    qs, ks, vs, _ = _att_specs("sb", S, QT)
    W = npairs * LANES
    hows = ["rows", None, None, "rows", "stat", "rows", None, None, None, None, None]
    return pl.pallas_call(
        _per_q_tile(body, hows), name=name, grid=(npairs, nq // ATT_QSUB),
        in_specs=[qs, ks, vs,
                  pl.BlockSpec((QT, LANES), lambda p, i: (i, p)),
                  pl.BlockSpec((2, ATT_QSUB, 1, T), lambda p, i: (p, i, 0, 0))],
        out_specs=[pl.BlockSpec((QT, LANES), lambda p, i: (i, p)), pl.BlockSpec((S, LANES), lambda p, i: (0, p)),
                   pl.BlockSpec((S, LANES), lambda p, i: (0, p))],
        out_shape=[jax.ShapeDtypeStruct((S, W), F32)] * 3,
        scratch_shapes=[pltpu.VMEM((2, LANES, T), F32), pltpu.VMEM((2, 1, T), F32), pltpu.VMEM((2, 1, T), F32)],
        compiler_params=_cparams(("parallel", "arbitrary")),
    )(proj, proj, proj, do, tot)


def _pad_w0(w):
    z = lambda n: jnp.zeros((w.shape[0], n), w.dtype)
    return jnp.concatenate([w[:, 2048:2432], w[:, 2432:2688], z(64), w[:, 2688:2720], z(32),
                            w[:, 1536:2048], w[:, 2720:3232], w[:, 0:512], w[:, 512:1024], w[:, 1024:1536]], axis=1)


def _unpad_w0(wp):
    return jnp.concatenate([wp[:, L0_SBQ:L0_SBQ + 512], wp[:, L0_SBK:L0_SBK + 512], wp[:, L0_SBV:L0_SBV + 512],
                            wp[:, L0_SBG:L0_SBG + 512], wp[:, 0:384], wp[:, 384:640], wp[:, 704:736],
                            wp[:, L0_MLG:L0_MLG + 512]], axis=1)


def _pad_wq(w):
    return jnp.pad(w.reshape(384, 8, 96), ((0, 0), (0, 0), (0, 32))).reshape(384, 1024)


def _unpad_wq(wp):
    return wp.reshape(384, 8, 128)[:, :, :96].reshape(384, 768)


def _pad_wkv(w):
    w3 = w.reshape(256, 8, 128)
    k = jnp.pad(w3[:, :, :64], ((0, 0), (0, 0), (0, 64))).reshape(256, 1024)
    return jnp.concatenate([k, w3[:, :, 64:].reshape(256, 512)], axis=1)


def _unpad_wkv(wp):
    k = wp[:, :1024].reshape(256, 8, 128)[:, :, :64]
    v = wp[:, 1024:].reshape(256, 8, 64)
    return jnp.concatenate([k, v], axis=-1).reshape(256, 1024)


def _pad_w1(w):
    return jnp.concatenate([w, jnp.zeros((w.shape[0], L1_WIDTH - ODD_IN_WIDTH), w.dtype)], axis=1)


def _local_step(x, positions, target, g, w0p, wqp, wkvp, wo0, w1p, wo1, send_early=None):
    S = x.shape[0]
    nq = S // ATT_T
    invf = ROPE_THETA ** (-jnp.arange(0, MLA_ROPE_DIM, 2, dtype=F32) / MLA_ROPE_DIM)
    invf = jnp.concatenate([jnp.zeros((64,), F32), invf, invf, jnp.zeros((32,), F32)]).reshape(1, LANES)
    cosT, s1T, s2T = _rope_tables(positions.reshape(S, 1), invf, "rope_tables")
    bfp = jnp.pad(g["l1_b_f"], ((0, 0), (0, LANES - FOX_HEADS)))

    if isinstance(w1p, tuple):
        w1_shard, finish_w1 = w1p
        proj0, h0t, w1_all = _norm_matmul(x, g["l0_pre_g"], w0p, "l0_in_proj", ride=w1_shard)
        w1p = finish_w1(w1_all)
    else:
        proj0, h0t = _norm_matmul(x, g["l0_pre_g"], w0p, "l0_in_proj")
    qm, km, vm, qnt, cnt = _mla_prep(proj0, g["l0_q_a_g"], g["l0_kv_a_g"], wqp, wkvp, cosT, s1T, s2T, "mla_prep")
    o_sb, og_sb, ogt_sb, tot_sb = _sb_fwd_t(proj0, S, 4, "sb_fwd")
    o_ml, og_ml, ogt_ml, lse_ml = _softmax_fwd("mla", (qm, km, vm, proj0), None, S, 4, "mla_fwd")
    y0, x1 = _out_proj(og_sb, og_ml, 0, 0, wo0, x, g["l0_post_g"], None, "l0_out_proj")

    proj1, h1t = _norm_matmul(x1, g["l1_pre_g"], w1p, "l1_in_proj")
    cfx = _fox_prep(proj1, bfp, "fox_prep")
    c16 = cfx[:, :FOX_HEADS].T
    c_col = jnp.broadcast_to(c16[:, :, None], (FOX_HEADS, S, LANES))
    o_fx, og_fx, ogt_fx, lse_fx = _softmax_fwd("fox", (proj1, proj1, proj1, proj1), c_col, S, 8, "fox_fwd")
    y1, dx2, lsum = _out_proj(og_fx, og_fx, 0, 1, wo1, x1, g["l1_post_g"], target, "l1_out_proj")

    dy1, do1, dgate1, d_post1 = _out_proj_bwd(dx2, y1, g["l1_post_g"], wo1, proj1, (L1_G, L1_G + 512), o_fx, o_fx, 0, 1, "l1_out_bwd")
    dwo1 = _matmul_t(ogt_fx, dy1, "l1_dw_out")
    dq1, dk1, dv1, dck, dcq = _softmax_bwd_t("fox", proj1, proj1, proj1, do1, 0, o_fx, lse_fx, c_col, S, 8,
                                             "fox_bwd")
    dc = jnp.pad((dcq.reshape(FOX_HEADS, S) - dck.reshape(FOX_HEADS, S)).T, ((0, 0), (0, LANES - FOX_HEADS)))
    df, d_bf = _fox_prep_bwd(dc, proj1, bfp, "fox_prep_bwd")
    pieces1 = [(L1_Q, dq1), (L1_K, dk1), (L1_V, dv1), (L1_G, dgate1), (L1_F, df)]
    dx1, d_pre1 = _in_proj_bwd(pieces1, w1p, x1, g["l1_pre_g"], dx2, "l1_in_bwd")
    dw1p = jnp.concatenate(_matmul_t_many(h1t, [dq1, dk1], "l1_dw_in_a")
                           + _matmul_t_many(h1t, [dv1, dgate1, df], "l1_dw_in_b"), axis=1)
    early = None if send_early is None else send_early(dw1p)

    dy0, do0, dgate0, d_post0 = _out_proj_bwd(dx1, y0, g["l0_post_g"], wo0, proj0, (L0_SBG, L0_MLG), o_sb, o_ml, 0, 0,
                                              "l0_out_bwd")
    dwo0 = jnp.concatenate([_matmul_t(ogt_sb, dy0, "l0_dw_out_sb"), _matmul_t(ogt_ml, dy0, "l0_dw_out_mla")], axis=0)
    dsq, dsk, dsv = _sb_bwd_t(proj0, do0, tot_sb, S, 4, "sb_bwd")
    dqm, dkm, dvm = _softmax_bwd_t("mla", qm, km, vm, do0, 4, o_ml, lse_ml, None, S, 4, "mla_bwd")
    dprep, dqb, dkvb, d_qag, d_kvag = _mla_prep_bwd(dqm, dkm, dvm, proj0, g["l0_q_a_g"], g["l0_kv_a_g"], wqp, wkvp,
                                                    cosT, s1T, s2T, "mla_prep_bwd")
    dwqp = _matmul_t(qnt, dqb, "l0_dw_qb")
    dwkvp = _matmul_t(cnt, dkvb, "l0_dw_kvb")
    pieces0 = [(L0_PREP, dprep), (L0_SBG, dgate0), (L0_SBQ, dsq), (L0_SBK, dsk), (L0_SBV, dsv)]
    if send_early is None:
        dx0, d_pre0 = _in_proj_bwd(pieces0, w0p, x, g["l0_pre_g"], dx1, "l0_in_bwd")
        early_slots = None
    else:
        dx0, d_pre0, early_slots = _in_proj_bwd(pieces0, w0p, x, g["l0_pre_g"], dx1, "l0_in_bwd", ride=early)
    dw0p = jnp.concatenate(_matmul_t_many(h0t, [dprep, dgate0], "l0_dw_in_a")
                           + _matmul_t_many(h0t, [dsq, dsk, dsv], "l0_dw_in_b"), axis=1)

    grads = {
        "l0_pre_g": d_pre0, "l0_post_g": d_post0, "l0_w_in": dw0p, "l0_q_a_g": d_qag, "l0_w_q_b": dwqp,
        "l0_kv_a_g": d_kvag, "l0_w_kv_b": dwkvp, "l0_w_out": dwo0, "l1_pre_g": d_pre1, "l1_post_g": d_post1,
        "l1_w_in": dw1p, "l1_b_f": d_bf[:, :FOX_HEADS], "l1_w_out": dwo1,
    }
    grads["early_q"], grads["early_slots"] = early, early_slots
    return lsum, dx0, grads


_ANY = pl.BlockSpec(memory_space=pl.ANY)


def _place():
    return lax.axis_index("x"), lax.axis_index("y"), lax.axis_index("c")


def _other_chips(x, y):
    return [(1 - x, y), (x, 1 - y), (1 - x, 1 - y)]


def _half(rows, c):
    return pl.ds(c * (rows // 2), rows // 2)


def _gather_phases(p_refs, out_refs, send_sems, recv_sems):
    n = len(p_refs)
    x, y, c = _place()
    sibling = (x, y, 1 - c)
    chips = _other_chips(x, y)

    def blk(k, chip, cc):
        return out_refs[k].at[2 * chip[0] + chip[1], _half(p_refs[k].shape[0], cc)]

    def copy(s, src, dst, to):
        return pltpu.make_async_remote_copy(src_ref=src, dst_ref=dst, send_sem=send_sems.at[s],
                                            recv_sem=recv_sems.at[s], device_id=to, device_id_type=MESH)

    def first():
        return [copy(6 * k + j, p_refs[k].at[_half(p_refs[k].shape[0], c)], blk(k, (x, y), c), (*chip, c))
                for j, chip in enumerate(chips) for k in range(n)]

    def begin():
        for cp in first():
            cp.start()

    def finish():
        passed = []
        for j, chip in enumerate(chips):
            for k in range(n):
                copy(6 * k + j, blk(k, chip, c), blk(k, chip, c), (x, y, c)).wait_recv()
                passed.append(copy(6 * k + 3 + j, blk(k, chip, c), blk(k, chip, c), sibling))
                passed[-1].start()
        for j, chip in enumerate(chips):
            for k in range(n):
                copy(6 * k + 3 + j, blk(k, chip, 1 - c), blk(k, chip, 1 - c), (x, y, c)).wait_recv()
        for cp in first() + passed:
            cp.wait_send()

    return begin, finish


def _weight_gather(parts):
    n = len(parts)

    def body(*refs):
        begin, finish = _gather_phases(refs[:n], refs[n:2 * n], refs[2 * n], refs[2 * n + 1])
        begin()
        finish()

    return pl.pallas_call(
        body, name="weight_gather", in_specs=[_ANY] * n, out_specs=[_ANY] * n,
        out_shape=[jax.ShapeDtypeStruct((4,) + a.shape, a.dtype) for a in parts],
        scratch_shapes=[pltpu.SemaphoreType.DMA((6 * n,)), pltpu.SemaphoreType.DMA((6 * n,))],
    )(*parts)


def _grad_core_exchange(ps, name="grad_core_exchange"):
    n = len(ps)

    def body(*refs):
        p_refs, recv_refs, send_sems, recv_sems = refs[:n], refs[n:2 * n], refs[2 * n], refs[2 * n + 1]
        x, y, c = _place()
        give = [pltpu.make_async_remote_copy(src_ref=p_refs[k].at[j, _half(p_refs[k].shape[1], 1 - c)],
                                             dst_ref=recv_refs[k].at[j], send_sem=send_sems.at[4 * k + j],
                                             recv_sem=recv_sems.at[4 * k + j], device_id=(x, y, 1 - c),
                                             device_id_type=MESH) for k in range(n) for j in range(4)]
        for cp in give:
            cp.start()
        for cp in give:
            cp.wait()

    return pl.pallas_call(
        body, name=name, in_specs=[_ANY] * n, out_specs=[_ANY] * n,
        out_shape=[jax.ShapeDtypeStruct((4, p.shape[1] // 2, p.shape[2]), p.dtype) for p in ps],
        scratch_shapes=[pltpu.SemaphoreType.DMA((4 * n,)), pltpu.SemaphoreType.DMA((4 * n,))],
    )(*ps)


def _grad_rows(rows):
    return _pick(rows, (1296, 512, rows))


def _grad_add_cores(p, theirs, c1, name):
    _, rh, cols = theirs.shape
    tr = _grad_rows(rh)

    def body(c_ref, a_ref, b_ref, o_ref):
        o_ref[...] = (a_ref[...] + b_ref[...]).astype(BF16)

    spec = pl.BlockSpec((None, tr, cols), lambda j, r, c: (j, r, 0))
    grid_spec = pltpu.PrefetchScalarGridSpec(
        num_scalar_prefetch=1, grid=(4, rh // tr),
        in_specs=[pl.BlockSpec((None, None, tr, cols), lambda j, r, c: (j, c[0], r, 0)), spec], out_specs=spec)
    return pl.pallas_call(
        body, name=name, grid_spec=grid_spec, out_shape=jax.ShapeDtypeStruct(theirs.shape, BF16),
        compiler_params=_cparams(("parallel", "parallel")),
    )(c1, p.reshape(4, 2, rh, cols), theirs)


def _exchange_phases(q_refs, out_refs, send_sems, recv_sems):
    n = len(q_refs)
    x, y, c = _place()
    me = 2 * x + y
    chips = _other_chips(x, y)

    def sends():
        return [pltpu.make_async_remote_copy(src_ref=q_refs[k].at[2 * chip[0] + chip[1]], dst_ref=out_refs[k].at[me],
                                             send_sem=send_sems.at[3 * k + j], recv_sem=recv_sems.at[3 * k + j],
                                             device_id=(*chip, c), device_id_type=MESH)
                for j, chip in enumerate(chips) for k in range(n)]

    def begin():
        for cp in sends():
            cp.start()

    def finish():
        for j, chip in enumerate(chips):
            for k in range(n):
                slot = out_refs[k].at[2 * chip[0] + chip[1]]
                pltpu.make_async_remote_copy(src_ref=slot, dst_ref=slot, send_sem=send_sems.at[3 * k + j],
                                             recv_sem=recv_sems.at[3 * k + j], device_id=(x, y, c),
                                             device_id_type=MESH).wait_recv()
        for cp in sends():
            cp.wait_send()

    return begin, finish


def _grad_chip_exchange(qs):
    n = len(qs)

    def body(*refs):
        begin, finish = _exchange_phases(refs[:n], refs[n:2 * n], refs[2 * n], refs[2 * n + 1])
        begin()
        finish()

    return pl.pallas_call(
        body, name="grad_chip_exchange", in_specs=[_ANY] * n, out_specs=[_ANY] * n,
        out_shape=[jax.ShapeDtypeStruct(q.shape, q.dtype) for q in qs],
        scratch_shapes=[pltpu.SemaphoreType.DMA((3 * n,)), pltpu.SemaphoreType.DMA((3 * n,))],
    )(*qs)


def _grad_add_chips(q, slots, me1, name):
    _, rh, cols = q.shape
    tr = _grad_rows(rh)

    def body(me_ref, own_ref, s0, s1, s2, s3, o_ref):
        me = me_ref[0]
        t = [jnp.where(me == j, own_ref[...], s[...]).astype(F32) for j, s in enumerate((s0, s1, s2, s3))]
        o_ref[...] = ((t[0] + t[1]) + t[2]) + t[3]

    def slot_spec(j):
        return pl.BlockSpec((None, tr, cols), lambda r, me: (jnp.where(me[0] == j, (j + 1) % 4, j), r, 0))

    grid_spec = pltpu.PrefetchScalarGridSpec(
        num_scalar_prefetch=1, grid=(rh // tr,),
        in_specs=[pl.BlockSpec((None, tr, cols), lambda r, me: (me[0], r, 0))] + [slot_spec(j) for j in range(4)],
        out_specs=pl.BlockSpec((tr, cols), lambda r, me: (r, 0)))
    return pl.pallas_call(
        body, name=name, grid_spec=grid_spec, out_shape=jax.ShapeDtypeStruct(q.shape[1:], F32),
        compiler_params=_cparams(("parallel",)),
    )(me1, q, slots, slots, slots, slots)


def _final_exchange(ts, sp):
    n = len(ts)

    def body(*refs):
        t_refs, sp_ref, out_refs, tot_ref = refs[:n], refs[n], refs[n + 1:2 * n + 1], refs[2 * n + 1]
        gath_ref, send_sems, recv_sems = refs[2 * n + 2:]
        x, y, c = _place()
        me = 4 * x + 2 * y + c
        gath_ref[me] = sp_ref[...]
        peers = []
        for k in range(1, 8):
            px = 1 - x if k & 4 else x
            py = 1 - y if k & 2 else y
            pc = 1 - c if k & 1 else c
            peers.append((px, py, pc))
        give = [pltpu.make_async_remote_copy(src_ref=t_refs[k], dst_ref=out_refs[k], send_sem=send_sems.at[k],
                                             recv_sem=recv_sems.at[k], device_id=(x, y, 1 - c), device_id_type=MESH)
                for k in range(n)]
        sends = [pltpu.make_async_remote_copy(src_ref=sp_ref, dst_ref=gath_ref.at[me], send_sem=send_sems.at[n + k],
                                              recv_sem=recv_sems.at[n + k], device_id=peer, device_id_type=MESH)
                 for k, peer in enumerate(peers)]
        for cp in give + sends:
            cp.start()
        for k, (px, py, pc) in enumerate(peers):
            slot = gath_ref.at[4 * px + 2 * py + pc]
            pltpu.make_async_remote_copy(src_ref=slot, dst_ref=slot, send_sem=send_sems.at[n + k],
                                         recv_sem=recv_sems.at[n + k], device_id=(x, y, c),
                                         device_id_type=MESH).wait_recv()
        for cp in sends:
            cp.wait_send()
        tot = gath_ref[0]
        for d in range(1, 8):
            tot = tot + gath_ref[d]
        tot_ref[...] = tot
        for cp in give:
            cp.wait()

    vm = pl.BlockSpec(memory_space=pltpu.VMEM)
    return pl.pallas_call(
        body, name="final_exchange", in_specs=[_ANY] * n + [vm], out_specs=[_ANY] * n + [vm],
        out_shape=[jax.ShapeDtypeStruct(t.shape, t.dtype) for t in ts] + [jax.ShapeDtypeStruct(sp.shape, sp.dtype)],
        scratch_shapes=[pltpu.VMEM((8,) + sp.shape, sp.dtype), pltpu.SemaphoreType.DMA((n + 7,)),
                        pltpu.SemaphoreType.DMA((n + 7,))],
    )(*ts, sp)


def _adamw_update(w, gv, m, v):
    mn = ADAM_B1 * m + (1.0 - ADAM_B1) * gv
    vn = ADAM_B2 * v + (1.0 - ADAM_B2) * (gv * gv)
    m_hat = mn / (1.0 - ADAM_B1 ** ADAM_STEP)
    v_hat = vn / (1.0 - ADAM_B2 ** ADAM_STEP)
    return -ADAM_LR * (m_hat / (jnp.sqrt(v_hat) + ADAM_EPS) + ADAM_WD * w), mn, vn


def _adamw(w, g, m, v, name):
    rows, cols = w.shape

    def body(w_ref, g_ref, m_ref, v_ref, d_ref, mo_ref, vo_ref):
        d_ref[...], mo_ref[...], vo_ref[...] = _adamw_update(w_ref[...], g_ref[...], m_ref[...], v_ref[...])

    if rows % 256 == 0 or cols % 256 != 0:
        tr = _pick(rows, (256, rows))
        grid, spec = (rows // tr,), pl.BlockSpec((tr, cols), lambda r: (r, 0))
    else:
        grid, spec = (cols // 256,), pl.BlockSpec((rows, 256), lambda r: (0, r))
    shp = jax.ShapeDtypeStruct(w.shape, F32)
    return pl.pallas_call(
        body, name=name, grid=grid, in_specs=[spec] * 4, out_specs=[spec] * 3, out_shape=[shp] * 3,
        compiler_params=_cparams(("parallel",)),
    )(w, g, m, v)


MAT_NAMES = ("l0_w_in", "l0_w_q_b", "l0_w_kv_b", "l0_w_out", "l1_w_in", "l1_w_out")
VEC_NAMES = ("l0_pre_g", "l0_post_g", "l0_q_a_g", "l0_kv_a_g", "l1_pre_g", "l1_post_g", "l1_b_f")
WEIGHT_NAMES = ("l0_pre_g", "l0_post_g", "l0_w_in", "l0_q_a_g", "l0_w_q_b", "l0_kv_a_g", "l0_w_kv_b", "l0_w_out",
                "l1_pre_g", "l1_post_g", "l1_w_in", "l1_b_f", "l1_w_out")
MAT_SHARD = {"l0_w_in": (1024, 808), "l0_w_q_b": (384, 192), "l0_w_kv_b": (256, 256), "l0_w_out": (256, 1024),
             "l1_w_in": (1024, 1028), "l1_w_out": (256, 1024)}
ROW_SHARDED = ("l0_w_out", "l1_w_out")
WHOLE_MATS = ("l0_w_in", "l1_w_in")
PACKED_MATS = ("l0_w_q_b", "l0_w_kv_b", "l0_w_out", "l1_w_out")
VEC_LEN = {"l0_pre_g": 1024, "l0_post_g": 1024, "l0_q_a_g": 384, "l0_kv_a_g": 256, "l1_pre_g": 1024,
           "l1_post_g": 1024, "l1_b_f": 16}


def _mat_rows(n):
    r, c = MAT_SHARD[n]
    return r * c // LANES


def _pack_shards(shards):
    return jnp.concatenate([shards[n].reshape(shards[n].shape[:-2] + (_mat_rows(n), LANES)) for n in PACKED_MATS],
                           axis=-2)


def _unpack_shards(pack):
    out, at = {}, 0
    for n in PACKED_MATS:
        out[n] = pack[..., at:at + _mat_rows(n), :].reshape(pack.shape[:-2] + MAT_SHARD[n])
        at += _mat_rows(n)
    return out


def _join_shards(n, s):
    if n in ROW_SHARDED:
        return s.reshape(4 * s.shape[1], s.shape[2])
    return s.transpose(1, 0, 2).reshape(s.shape[1], 4 * s.shape[2])


def _cut_shards(n, w):
    r, c = MAT_SHARD[n]
    if n in ROW_SHARDED:
        return w.reshape(4, r, c)
    return w.reshape(r, 4, c).transpose(1, 0, 2)


def _pack_vecs(vecs):
    parts = []
    for n in VEC_NAMES:
        v = vecs[n].reshape(-1)
        parts.append(jnp.pad(v, (0, VEC_ROWS * LANES - v.shape[0])).reshape(VEC_ROWS, LANES))
    return jnp.concatenate(parts, axis=0)


def _unpack_vecs(pack):
    return {n: pack[k * VEC_ROWS:(k + 1) * VEC_ROWS].reshape(-1)[:VEC_LEN[n]] for k, n in enumerate(VEC_NAMES)}


def kernel(x, positions, l0_pre_g, l0_post_g, l0_w_in, l0_q_a_g, l0_w_q_b, l0_kv_a_g, l0_w_kv_b, l0_w_out, l1_pre_g, l1_post_g, l1_w_in, l1_b_f, l1_w_out, loss_target, m_l0_pre_g, m_l0_post_g, m_l0_w_in, m_l0_q_a_g, m_l0_w_q_b, m_l0_kv_a_g, m_l0_w_kv_b, m_l0_w_out, m_l1_pre_g, m_l1_post_g, m_l1_w_in, m_l1_b_f, m_l1_w_out, v_l0_pre_g, v_l0_post_g, v_l0_w_in, v_l0_q_a_g, v_l0_w_q_b, v_l0_kv_a_g, v_l0_w_kv_b, v_l0_w_out, v_l1_pre_g, v_l1_post_g, v_l1_w_in, v_l1_b_f, v_l1_w_out):
    w = dict(l0_pre_g=l0_pre_g, l0_post_g=l0_post_g, l0_w_in=l0_w_in, l0_q_a_g=l0_q_a_g, l0_w_q_b=l0_w_q_b,
             l0_kv_a_g=l0_kv_a_g, l0_w_kv_b=l0_w_kv_b, l0_w_out=l0_w_out, l1_pre_g=l1_pre_g, l1_post_g=l1_post_g,
             l1_w_in=l1_w_in, l1_b_f=l1_b_f, l1_w_out=l1_w_out)
    m = dict(l0_pre_g=m_l0_pre_g, l0_post_g=m_l0_post_g, l0_w_in=m_l0_w_in, l0_q_a_g=m_l0_q_a_g, l0_w_q_b=m_l0_w_q_b,
             l0_kv_a_g=m_l0_kv_a_g, l0_w_kv_b=m_l0_w_kv_b, l0_w_out=m_l0_w_out, l1_pre_g=m_l1_pre_g,
             l1_post_g=m_l1_post_g, l1_w_in=m_l1_w_in, l1_b_f=m_l1_b_f, l1_w_out=m_l1_w_out)
    v = dict(l0_pre_g=v_l0_pre_g, l0_post_g=v_l0_post_g, l0_w_in=v_l0_w_in, l0_q_a_g=v_l0_q_a_g, l0_w_q_b=v_l0_w_q_b,
             l0_kv_a_g=v_l0_kv_a_g, l0_w_kv_b=v_l0_w_kv_b, l0_w_out=v_l0_w_out, l1_pre_g=v_l1_pre_g,
             l1_post_g=v_l1_post_g, l1_w_in=v_l1_w_in, l1_b_f=v_l1_b_f, l1_w_out=v_l1_w_out)

    cx, cy, cc = _place()
    me1 = jnp.reshape(2 * cx + cy, (1,)).astype(jnp.int32)
    c1 = jnp.reshape(cc, (1,)).astype(jnp.int32)
    w_bf = {n: w[n].astype(BF16) for n in MAT_NAMES}
    def with_mine(got, own):
        return lax.dynamic_update_slice(got, own[None], (2 * cx + cy, 0, 0))

    mine = [_pack_shards(w_bf), w_bf["l0_w_in"]]
    got = [with_mine(g, a) for g, a in zip(_weight_gather(mine), mine)]
    gathered = dict(_unpack_shards(got[0]), l0_w_in=got[1])
    full = {n: _join_shards(n, gathered[n]) for n in MAT_NAMES if n != "l1_w_in"}
    gains = {n: w[n].reshape(1, -1) for n in VEC_NAMES}

    def finish_w1(w1_all):
        return _pad_w1(_join_shards("l1_w_in", with_mine(w1_all, w_bf["l1_w_in"])))

    def send_early(dw1p):
        g1 = _cut_shards("l1_w_in", dw1p[:, :ODD_IN_WIDTH])
        return _grad_add_cores(g1, _grad_core_exchange([g1], "grad_core_exchange_l1_w_in")[0], c1,
                               "grad_add_cores_l1_w_in")

    lsum, dx0, grads = _local_step(
        x[0], positions[0], loss_target[0], gains, _pad_w0(full["l0_w_in"]), _pad_wq(full["l0_w_q_b"]),
        _pad_wkv(full["l0_w_kv_b"]), full["l0_w_out"], (w_bf["l1_w_in"], finish_w1), full["l1_w_out"], send_early)

    gfull = {"l0_w_in": _unpad_w0(grads["l0_w_in"]), "l0_w_q_b": _unpad_wq(grads["l0_w_q_b"]),
             "l0_w_kv_b": _unpad_wkv(grads["l0_w_kv_b"]), "l0_w_out": grads["l0_w_out"],
             "l1_w_out": grads["l1_w_out"]}
    cut = {n: _cut_shards(n, gfull[n]) for n in gfull}
    tags = ("packed", "l0_w_in")
    g_parts = [_pack_shards(cut), cut["l0_w_in"]]
    q_cores = [_grad_add_cores(p, t, c1, "grad_add_cores_" + tag)
               for p, t, tag in zip(g_parts, _grad_core_exchange(g_parts), tags)]
    slots = list(_grad_chip_exchange(q_cores)) + [grads["early_slots"]]
    q_cores.append(grads["early_q"])
    g_mine = [_grad_add_chips(q, s, me1, "grad_add_chips_" + tag)
              for q, s, tag in zip(q_cores, slots, tags + ("l1_w_in",))]
    *g_theirs, small = _final_exchange(g_mine, jnp.concatenate([_pack_vecs({n: grads[n] for n in VEC_NAMES}),
                                                                lsum.reshape(D_MODEL // LANES, LANES)], axis=0))
    g_small = small[:SMALL_ROWS]
    loss = 0.5 * jnp.sum(small[SMALL_ROWS:]) / float(D_MODEL)

    whole = [jnp.concatenate([lax.select(cc == 0, a, b), lax.select(cc == 0, b, a)], axis=0)
             for a, b in zip(g_mine, g_theirs)]
    g_mats = dict(_unpack_shards(whole[0]), **dict(zip(WHOLE_MATS, whole[1:])))
    d_mats, m_mats, v_mats = {}, {}, {}
    for n in PACKED_MATS:
        d_mats[n], m_mats[n], v_mats[n] = _adamw(w[n], g_mats[n], m[n], v[n], "adamw_" + n)
    for n in WHOLE_MATS:
        gt = g_mats[n].T
        outs = _adamw(w[n].T, gt, m[n].T, v[n].T, "adamw_" + n)
        g_mats[n], d_mats[n], m_mats[n], v_mats[n] = gt.T, outs[0].T, outs[1].T, outs[2].T
    d_small, m_small, v_small = _adamw(_pack_vecs(w), g_small, _pack_vecs(m), _pack_vecs(v), "adamw_vecs")

    def leaves(mats, vec_pack):
        out = dict(mats)
        out.update(_unpack_vecs(vec_pack))
        return [out[n] for n in WEIGHT_NAMES]

    return (loss, dx0[None], *leaves(g_mats, g_small), *leaves(d_mats, d_small), *leaves(m_mats, m_small),
            *leaves(v_mats, v_small))
```
